```python
import jax
import jax.numpy as jnp
from jax import lax
import numpy as np

D_MODEL = 1024
BATCH = 8
SEQ = 4096
DEPTH = 2

N_META = 16
EPS = 1e-6
GATE_CLAMP = 1.0 - 1e-6
CONV_DIM = D_MODEL // 2
CONV_K = 31
MLA_HEADS = D_MODEL // 128
Q_RANK = D_MODEL // 4
KV_RANK = D_MODEL // 8
NOPE_DIM = 64
ROPE_DIM = 32
V_DIM = 64
QK_DIM = NOPE_DIM + ROPE_DIM
ROPE_BASE = 10000.0
Q_BLOCK = 128
HGRN_HEADS = D_MODEL // 256
HGRN_DK = 128
HGRN_DV = (D_MODEL // 2) // HGRN_HEADS
HGRN_CHUNK = 64
D_FF = 4 * D_MODEL
N_BRANCH = 3

SPLIT_SIZES = (
    2 * CONV_DIM,
    Q_RANK,
    KV_RANK,
    ROPE_DIM,
    HGRN_HEADS * HGRN_DK,
    HGRN_HEADS * HGRN_DK,
    HGRN_HEADS * HGRN_DV,
    HGRN_HEADS * HGRN_DV,
    N_BRANCH * D_MODEL,
)
SPLIT_POINTS = tuple(int(s) for s in np.cumsum(SPLIT_SIZES)[:-1])
N_IN = int(sum(SPLIT_SIZES))

kernel_name = 'hybrid_conv_mla_hgrn2_block'


def rms_norm(x, g):
    xf = x.astype(jnp.float32)
    y = xf * lax.rsqrt(jnp.mean(xf * xf, axis=-1, keepdims=True) + EPS)
    return (y * g.astype(jnp.float32)).astype(x.dtype)


def layer_norm(x, g, b):
    xf = x.astype(jnp.float32)
    mu = jnp.mean(xf, axis=-1, keepdims=True)
    xc = xf - mu
    y = xc * lax.rsqrt(jnp.mean(xc * xc, axis=-1, keepdims=True) + EPS)
    return (y * g.astype(jnp.float32) + b.astype(jnp.float32)).astype(x.dtype)


def apply_rope(x, cos, sin):
    half = ROPE_DIM // 2
    xf = x.astype(jnp.float32)
    x1, x2 = xf[..., :half], xf[..., half:]
    out = jnp.concatenate([x1 * cos - x2 * sin, x1 * sin + x2 * cos], axis=-1)
    return out.astype(x.dtype)


def conv_module(u, conv_w, conv_b, ln_g, ln_b, w_proj):
    a, gt = jnp.split(u, 2, axis=-1)
    h = a * jax.nn.sigmoid(gt)
    h = lax.conv_general_dilated(
        h, conv_w[:, None, :].astype(h.dtype), window_strides=(1,), padding=[(CONV_K - 1, 0)],
        dimension_numbers=('NWC', 'WIO', 'NWC'), feature_group_count=CONV_DIM) + conv_b
    h = jax.nn.silu(layer_norm(h, ln_g, ln_b))
    return h @ w_proj


def causal_block_attention(q, k, v):
    B, L, H, Dq = q.shape
    n_blk = -(-L // Q_BLOCK)
    Lp = n_blk * Q_BLOCK
    qp = jnp.pad(q, ((0, 0), (0, Lp - L), (0, 0), (0, 0)))
    qb = qp.reshape(B, n_blk, Q_BLOCK, H, Dq).transpose(1, 0, 2, 3, 4)
    starts = jnp.arange(n_blk, dtype=jnp.int32) * Q_BLOCK
    k_pos = jnp.arange(L, dtype=jnp.int32)
    scale = Dq ** -0.5

    def one_block(args):
        q_blk, start = args
        s = jnp.einsum('bqhd,bkhd->bhqk', q_blk, k).astype(jnp.float32) * scale
        q_pos = start + jnp.arange(Q_BLOCK, dtype=jnp.int32)
        mask = k_pos[None, :] <= q_pos[:, None]
        s = jnp.where(mask, s, -1e30)
        p = jax.nn.softmax(s, axis=-1).astype(v.dtype)
        return jnp.einsum('bhqk,bkhd->bqhd', p, v)

    ob = lax.map(one_block, (qb, starts))
    return ob.transpose(1, 0, 2, 3, 4).reshape(B, Lp, H, V_DIM)[:, :L]


def mla(c_q, c_kv, k_rope, cos, sin, q_a_g, w_uq, kv_a_g, w_ukv, q_norm_g, k_norm_g, w_proj):
    B, L = c_q.shape[:2]
    q = (rms_norm(c_q, q_a_g) @ w_uq).reshape(B, L, MLA_HEADS, QK_DIM)
    kv = (rms_norm(c_kv, kv_a_g) @ w_ukv).reshape(B, L, MLA_HEADS, NOPE_DIM + V_DIM)
    k_nope, v = kv[..., :NOPE_DIM], kv[..., NOPE_DIM:]
    k_r = jnp.broadcast_to(k_rope[:, :, None, :], (B, L, MLA_HEADS, ROPE_DIM))
    k = jnp.concatenate([k_nope, k_r], axis=-1)
    q = rms_norm(q, q_norm_g)
    k = rms_norm(k, k_norm_g)
    q = jnp.concatenate([q[..., :NOPE_DIM], apply_rope(q[..., NOPE_DIM:], cos, sin)], axis=-1)
    k = jnp.concatenate([k[..., :NOPE_DIM], apply_rope(k[..., NOPE_DIM:], cos, sin)], axis=-1)
    o = causal_block_attention(q, k, v)
    return o.reshape(B, L, MLA_HEADS * V_DIM) @ w_proj


def hgrn2(q, f_raw, i, g, lb, norm_g, w_proj):
    B, L = q.shape[:2]
    f32 = jnp.float32
    lbf = lb.astype(f32)
    fr = f_raw.astype(f32)
    k = (1.0 - lbf) * jax.nn.sigmoid(-fr)
    log_f = jnp.log1p(-jnp.minimum(k, GATE_CLAMP))
    v = jax.nn.silu(i.astype(f32))
    qf = q.astype(f32)
    pad_front = (-N_META) % HGRN_CHUNK
    pad_back = (-(pad_front + L)) % HGRN_CHUNK
    pads = ((0, 0), (pad_front, pad_back), (0, 0))
    qf, k, v, log_f = [jnp.pad(a, pads) for a in (qf, k, v, log_f)]
    Lp = L + pad_front + pad_back
    nc = Lp // HGRN_CHUNK

    def to_chunks(a, d):
        return a.reshape(B, nc, HGRN_CHUNK, HGRN_HEADS, d).transpose(1, 0, 3, 2, 4)

    qc, kc, lfc = to_chunks(qf, HGRN_DK), to_chunks(k, HGRN_DK), to_chunks(log_f, HGRN_DK)
    vc = to_chunks(v, HGRN_DV)
    causal = jnp.tril(jnp.ones((HGRN_CHUNK, HGRN_CHUNK), dtype=bool))[:, :, None]

    def chunk_step(S, inp):
        q_c, k_c, v_c, lf_c = inp
        b = jnp.cumsum(lf_c, axis=2)
        o_inter = jnp.einsum('bhtk,bhkv->bhtv', q_c * jnp.exp(b), S)
        diff = b[:, :, :, None, :] - b[:, :, None, :, :]
        decay = jnp.where(causal, jnp.exp(jnp.where(causal, diff, 0.0)), 0.0)
        A = jnp.einsum('bhtk,bhtsk,bhsk->bhts', q_c, decay, k_c)
        o_intra = jnp.einsum('bhts,bhsv->bhtv', A, v_c)
        b_last = b[:, :, -1:, :]
        S_new = jnp.exp(b_last[:, :, 0, :])[..., None] * S + jnp.einsum(
            'bhsk,bhsv->bhkv', k_c * jnp.exp(b_last - b), v_c)
        return S_new, o_inter + o_intra

    S0 = jnp.zeros((B, HGRN_HEADS, HGRN_DK, HGRN_DV), f32)
    _, oc = lax.scan(chunk_step, S0, (qc, kc, vc, lfc))
    o = oc.transpose(1, 0, 3, 2, 4).reshape(B, Lp, HGRN_HEADS, HGRN_DV)[:, pad_front:pad_front + L]
    o = o * lax.rsqrt(jnp.mean(o * o, axis=-1, keepdims=True) + EPS)
    o = o.reshape(B, L, HGRN_HEADS * HGRN_DV) * norm_g.astype(f32)
    o = (o * jax.nn.silu(g.astype(f32))).astype(q.dtype)
    return o @ w_proj


def _fwd_setup_inputs(seed: int = 0) -> dict:
    key = jax.random.key(seed)
    ks = iter(jax.random.split(key, 32))

    def nrm(shape, scale):
        return jax.random.normal(next(ks), shape, jnp.float32) * scale

    def gain(shape):
        return 1.0 + 0.1 * nrm(shape, 1.0)

    D = D_MODEL
    return {
        'x': nrm((BATCH, SEQ, D), 1.0),
        'meta': nrm((N_META, D), 1.0),
        'norm1_g': gain((DEPTH, D)),
        'w_in': nrm((DEPTH, D, N_IN), D ** -0.5),
        'conv_w': nrm((DEPTH, CONV_K, CONV_DIM), CONV_K ** -0.5),
        'conv_b': nrm((DEPTH, CONV_DIM), 0.02),
        'conv_ln_g': gain((DEPTH, CONV_DIM)),
        'conv_ln_b': nrm((DEPTH, CONV_DIM), 0.02),
        'w_conv_out': nrm((DEPTH, CONV_DIM, D), CONV_DIM ** -0.5),
        'q_a_norm_g': gain((DEPTH, Q_RANK)),
        'w_uq': nrm((DEPTH, Q_RANK, MLA_HEADS * QK_DIM), Q_RANK ** -0.5),
        'kv_a_norm_g': gain((DEPTH, KV_RANK)),
        'w_ukv': nrm((DEPTH, KV_RANK, MLA_HEADS * (NOPE_DIM + V_DIM)), KV_RANK ** -0.5),
        'q_norm_g': gain((DEPTH, QK_DIM)),
        'k_norm_g': gain((DEPTH, QK_DIM)),
        'w_attn_out': nrm((DEPTH, MLA_HEADS * V_DIM, D), (MLA_HEADS * V_DIM) ** -0.5),
        'hgrn_lb_logits': nrm((DEPTH, HGRN_HEADS * HGRN_DK), 1.0),
        'hgrn_norm_g': gain((DEPTH, HGRN_HEADS * HGRN_DV)),
        'w_hgrn_out': nrm((DEPTH, HGRN_HEADS * HGRN_DV, D), (HGRN_HEADS * HGRN_DV) ** -0.5),
        'w_out': nrm((DEPTH, D, D), D ** -0.5),
        'norm2_g': gain((DEPTH, D)),
        'w_ff1': nrm((DEPTH, D, D_FF), D ** -0.5),
        'w_ff2': nrm((DEPTH, D_FF, D), D_FF ** -0.5),
    }


def _fwd_reference(x, meta, norm1_g, w_in, conv_w, conv_b, conv_ln_g, conv_ln_b, w_conv_out,
              q_a_norm_g, w_uq, kv_a_norm_g, w_ukv, q_norm_g, k_norm_g, w_attn_out,
              hgrn_lb_logits, hgrn_norm_g, w_hgrn_out, w_out, norm2_g, w_ff1, w_ff2):
    B = x.shape[0]
    D = D_MODEL
    x = jnp.concatenate([jnp.broadcast_to(meta[None].astype(x.dtype), (B, N_META, D)), x], axis=1)
    L = x.shape[1]
    half = ROPE_DIM // 2
    pos = jnp.arange(L, dtype=jnp.float32)
    inv_freq = ROPE_BASE ** (-jnp.arange(half, dtype=jnp.float32) / half)
    ang = pos[:, None] * inv_freq[None, :]
    cos = jnp.cos(ang)[None, :, None, :]
    sin = jnp.sin(ang)[None, :, None, :]
    p_lb = jax.nn.softmax(hgrn_lb_logits.astype(jnp.float32), axis=0)
    lower_bounds = jnp.cumsum(p_lb, axis=0) - p_lb[0:1]

    for l in range(DEPTH):
        h = rms_norm(x, norm1_g[l])
        u = h @ w_in[l]
        (u_conv, c_q, c_kv, k_rope, hq, hf, hi, hg, u_gate) = jnp.split(u, SPLIT_POINTS, axis=-1)
        y_a = conv_module(u_conv, conv_w[l], conv_b[l], conv_ln_g[l], conv_ln_b[l], w_conv_out[l])
        y_b = mla(c_q, c_kv, k_rope, cos, sin, q_a_norm_g[l], w_uq[l], kv_a_norm_g[l], w_ukv[l],
                  q_norm_g[l], k_norm_g[l], w_attn_out[l])
        y_c = hgrn2(hq, hf, hi, hg, lower_bounds[l], hgrn_norm_g[l], w_hgrn_out[l])
        gates = jax.nn.sigmoid(u_gate).reshape(B, L, N_BRANCH, D)
        mix = gates[:, :, 0] * y_a + gates[:, :, 1] * y_b + gates[:, :, 2] * y_c
        x = x + mix @ w_out[l]
        h2 = rms_norm(x, norm2_g[l])
        x = x + jnp.square(jax.nn.relu(h2 @ w_ff1[l])) @ w_ff2[l]

    return x[:, N_META:]


import jax as _jax
import jax.numpy as _jnp

TWIN_FORMAT = 'train_step'
FWD_PARAMS = ['x', 'meta', 'norm1_g', 'w_in', 'conv_w', 'conv_b', 'conv_ln_g', 'conv_ln_b', 'w_conv_out', 'q_a_norm_g', 'w_uq', 'kv_a_norm_g', 'w_ukv', 'q_norm_g', 'k_norm_g', 'w_attn_out', 'hgrn_lb_logits', 'hgrn_norm_g', 'w_hgrn_out', 'w_out', 'norm2_g', 'w_ff1', 'w_ff2']
TWIN_WEIGHTS = ['meta', 'norm1_g', 'w_in', 'conv_w', 'conv_b', 'conv_ln_g', 'conv_ln_b', 'w_conv_out', 'q_a_norm_g', 'w_uq', 'kv_a_norm_g', 'w_ukv', 'q_norm_g', 'k_norm_g', 'w_attn_out', 'hgrn_lb_logits', 'hgrn_norm_g', 'w_hgrn_out', 'w_out', 'norm2_g', 'w_ff1', 'w_ff2']
TWIN_DIFF_INPUT = 'x'
TWIN_INPUTS = ['x', 'meta', 'norm1_g', 'w_in', 'conv_w', 'conv_b', 'conv_ln_g', 'conv_ln_b', 'w_conv_out', 'q_a_norm_g', 'w_uq', 'kv_a_norm_g', 'w_ukv', 'q_norm_g', 'k_norm_g', 'w_attn_out', 'hgrn_lb_logits', 'hgrn_norm_g', 'w_hgrn_out', 'w_out', 'norm2_g', 'w_ff1', 'w_ff2', 'loss_target', 'm_meta', 'm_norm1_g', 'm_w_in', 'm_conv_w', 'm_conv_b', 'm_conv_ln_g', 'm_conv_ln_b', 'm_w_conv_out', 'm_q_a_norm_g', 'm_w_uq', 'm_kv_a_norm_g', 'm_w_ukv', 'm_q_norm_g', 'm_k_norm_g', 'm_w_attn_out', 'm_hgrn_lb_logits', 'm_hgrn_norm_g', 'm_w_hgrn_out', 'm_w_out', 'm_norm2_g', 'm_w_ff1', 'm_w_ff2', 'v_meta', 'v_norm1_g', 'v_w_in', 'v_conv_w', 'v_conv_b', 'v_conv_ln_g', 'v_conv_ln_b', 'v_w_conv_out', 'v_q_a_norm_g', 'v_w_uq', 'v_kv_a_norm_g', 'v_w_ukv', 'v_q_norm_g', 'v_k_norm_g', 'v_w_attn_out', 'v_hgrn_lb_logits', 'v_hgrn_norm_g', 'v_w_hgrn_out', 'v_w_out', 'v_norm2_g', 'v_w_ff1', 'v_w_ff2']
TWIN_OUTPUTS = ['loss', 'grad_x', 'grad_meta', 'grad_norm1_g', 'grad_w_in', 'grad_conv_w', 'grad_conv_b', 'grad_conv_ln_g', 'grad_conv_ln_b', 'grad_w_conv_out', 'grad_q_a_norm_g', 'grad_w_uq', 'grad_kv_a_norm_g', 'grad_w_ukv', 'grad_q_norm_g', 'grad_k_norm_g', 'grad_w_attn_out', 'grad_hgrn_lb_logits', 'grad_hgrn_norm_g', 'grad_w_hgrn_out', 'grad_w_out', 'grad_norm2_g', 'grad_w_ff1', 'grad_w_ff2', 'delta_meta', 'delta_norm1_g', 'delta_w_in', 'delta_conv_w', 'delta_conv_b', 'delta_conv_ln_g', 'delta_conv_ln_b', 'delta_w_conv_out', 'delta_q_a_norm_g', 'delta_w_uq', 'delta_kv_a_norm_g', 'delta_w_ukv', 'delta_q_norm_g', 'delta_k_norm_g', 'delta_w_attn_out', 'delta_hgrn_lb_logits', 'delta_hgrn_norm_g', 'delta_w_hgrn_out', 'delta_w_out', 'delta_norm2_g', 'delta_w_ff1', 'delta_w_ff2', 'new_m_meta', 'new_m_norm1_g', 'new_m_w_in', 'new_m_conv_w', 'new_m_conv_b', 'new_m_conv_ln_g', 'new_m_conv_ln_b', 'new_m_w_conv_out', 'new_m_q_a_norm_g', 'new_m_w_uq', 'new_m_kv_a_norm_g', 'new_m_w_ukv', 'new_m_q_norm_g', 'new_m_k_norm_g', 'new_m_w_attn_out', 'new_m_hgrn_lb_logits', 'new_m_hgrn_norm_g', 'new_m_w_hgrn_out', 'new_m_w_out', 'new_m_norm2_g', 'new_m_w_ff1', 'new_m_w_ff2', 'new_v_meta', 'new_v_norm1_g', 'new_v_w_in', 'new_v_conv_w', 'new_v_conv_b', 'new_v_conv_ln_g', 'new_v_conv_ln_b', 'new_v_w_conv_out', 'new_v_q_a_norm_g', 'new_v_w_uq', 'new_v_kv_a_norm_g', 'new_v_w_ukv', 'new_v_q_norm_g', 'new_v_k_norm_g', 'new_v_w_attn_out', 'new_v_hgrn_lb_logits', 'new_v_hgrn_norm_g', 'new_v_w_hgrn_out', 'new_v_w_out', 'new_v_norm2_g', 'new_v_w_ff1', 'new_v_w_ff2']
TWIN_LEAF_KINDS = {'loss': 'loss', 'grad_x': 'grad_x', 'grad_meta': 'grad_w', 'grad_norm1_g': 'grad_w', 'grad_w_in': 'grad_w', 'grad_conv_w': 'grad_w', 'grad_conv_b': 'grad_w', 'grad_conv_ln_g': 'grad_w', 'grad_conv_ln_b': 'grad_w', 'grad_w_conv_out': 'grad_w', 'grad_q_a_norm_g': 'grad_w', 'grad_w_uq': 'grad_w', 'grad_kv_a_norm_g': 'grad_w', 'grad_w_ukv': 'grad_w', 'grad_q_norm_g': 'grad_w', 'grad_k_norm_g': 'grad_w', 'grad_w_attn_out': 'grad_w', 'grad_hgrn_lb_logits': 'grad_w', 'grad_hgrn_norm_g': 'grad_w', 'grad_w_hgrn_out': 'grad_w', 'grad_w_out': 'grad_w', 'grad_norm2_g': 'grad_w', 'grad_w_ff1': 'grad_w', 'grad_w_ff2': 'grad_w', 'delta_meta': 'delta_w', 'delta_norm1_g': 'delta_w', 'delta_w_in': 'delta_w', 'delta_conv_w': 'delta_w', 'delta_conv_b': 'delta_w', 'delta_conv_ln_g': 'delta_w', 'delta_conv_ln_b': 'delta_w', 'delta_w_conv_out': 'delta_w', 'delta_q_a_norm_g': 'delta_w', 'delta_w_uq': 'delta_w', 'delta_kv_a_norm_g': 'delta_w', 'delta_w_ukv': 'delta_w', 'delta_q_norm_g': 'delta_w', 'delta_k_norm_g': 'delta_w', 'delta_w_attn_out': 'delta_w', 'delta_hgrn_lb_logits': 'delta_w', 'delta_hgrn_norm_g': 'delta_w', 'delta_w_hgrn_out': 'delta_w', 'delta_w_out': 'delta_w', 'delta_norm2_g': 'delta_w', 'delta_w_ff1': 'delta_w', 'delta_w_ff2': 'delta_w', 'new_m_meta': 'new_m', 'new_m_norm1_g': 'new_m', 'new_m_w_in': 'new_m', 'new_m_conv_w': 'new_m', 'new_m_conv_b': 'new_m', 'new_m_conv_ln_g': 'new_m', 'new_m_conv_ln_b': 'new_m', 'new_m_w_conv_out': 'new_m', 'new_m_q_a_norm_g': 'new_m', 'new_m_w_uq': 'new_m', 'new_m_kv_a_norm_g': 'new_m', 'new_m_w_ukv': 'new_m', 'new_m_q_norm_g': 'new_m', 'new_m_k_norm_g': 'new_m', 'new_m_w_attn_out': 'new_m', 'new_m_hgrn_lb_logits': 'new_m', 'new_m_hgrn_norm_g': 'new_m', 'new_m_w_hgrn_out': 'new_m', 'new_m_w_out': 'new_m', 'new_m_norm2_g': 'new_m', 'new_m_w_ff1': 'new_m', 'new_m_w_ff2': 'new_m', 'new_v_meta': 'new_v', 'new_v_norm1_g': 'new_v', 'new_v_w_in': 'new_v', 'new_v_conv_w': 'new_v', 'new_v_conv_b': 'new_v', 'new_v_conv_ln_g': 'new_v', 'new_v_conv_ln_b': 'new_v', 'new_v_w_conv_out': 'new_v', 'new_v_q_a_norm_g': 'new_v', 'new_v_w_uq': 'new_v', 'new_v_kv_a_norm_g': 'new_v', 'new_v_w_ukv': 'new_v', 'new_v_q_norm_g': 'new_v', 'new_v_k_norm_g': 'new_v', 'new_v_w_attn_out': 'new_v', 'new_v_hgrn_lb_logits': 'new_v', 'new_v_hgrn_norm_g': 'new_v', 'new_v_w_hgrn_out': 'new_v', 'new_v_w_out': 'new_v', 'new_v_norm2_g': 'new_v', 'new_v_w_ff1': 'new_v', 'new_v_w_ff2': 'new_v'}


def _forward(args):
    return _fwd_reference(*[args[k] for k in FWD_PARAMS])


def _output_shape():
    out = _jax.eval_shape(lambda: _forward(_fwd_setup_inputs(0)))
    return out.shape, out.dtype

N_MICROBATCH = 1
ADAM_LR = 0.001
ADAM_B1 = 0.9
ADAM_B2 = 0.999
ADAM_EPS = 1e-08
ADAM_WD = 0.01
ADAM_STEP = 10
PER_EXAMPLE_BATCH_AXIS = {'x': 0, 'loss_target': 0}
SHARED_INPUTS = []
_WEIGHT_DTYPES = {'meta': _jnp.float32, 'norm1_g': _jnp.float32, 'w_in': _jnp.float32, 'conv_w': _jnp.float32, 'conv_b': _jnp.float32, 'conv_ln_g': _jnp.float32, 'conv_ln_b': _jnp.float32, 'w_conv_out': _jnp.float32, 'q_a_norm_g': _jnp.float32, 'w_uq': _jnp.float32, 'kv_a_norm_g': _jnp.float32, 'w_ukv': _jnp.float32, 'q_norm_g': _jnp.float32, 'k_norm_g': _jnp.float32, 'w_attn_out': _jnp.float32, 'hgrn_lb_logits': _jnp.float32, 'hgrn_norm_g': _jnp.float32, 'w_hgrn_out': _jnp.float32, 'w_out': _jnp.float32, 'norm2_g': _jnp.float32, 'w_ff1': _jnp.float32, 'w_ff2': _jnp.float32}
MOMENT_SCALE = {'meta': 8.897228e-02, 'norm1_g': 5.984249e+00, 'w_in': 1.474821e+00, 'conv_w': 2.503272e+00, 'conv_b': 2.637568e+01, 'conv_ln_g': 1.327922e+01, 'conv_ln_b': 1.552374e+01, 'w_conv_out': 4.617453e+00, 'q_a_norm_g': 2.728206e-01, 'w_uq': 1.840652e-01, 'kv_a_norm_g': 1.104831e+01, 'w_ukv': 3.925495e+00, 'q_norm_g': 8.341818e-01, 'k_norm_g': 8.763603e-01, 'w_attn_out': 3.564525e+00, 'hgrn_lb_logits': 1.299549e-01, 'hgrn_norm_g': 8.171237e+00, 'w_hgrn_out': 4.565319e-01, 'w_out': 5.734626e+00, 'norm2_g': 9.994856e+01, 'w_ff1': 4.271599e+00, 'w_ff2': 1.696134e+01}


def _to_microbatches(a, axis):
    t = _jnp.moveaxis(a, axis, 0)
    t = t.reshape((N_MICROBATCH, t.shape[0] // N_MICROBATCH) + t.shape[1:])
    return _jnp.moveaxis(t, 1, axis + 1)


def setup_inputs(seed: int = 0) -> dict:
    inp = _fwd_setup_inputs(seed)
    key = _jax.random.fold_in(_jax.random.key(seed), 7919)
    shape, _ = _output_shape()
    out = dict(inp)
    out["loss_target"] = _jax.random.normal(_jax.random.fold_in(key, 0), shape, _jnp.float32)
    for i, name in enumerate(TWIN_WEIGHTS):
        w = inp[name].astype(_jnp.float32)
        if MOMENT_SCALE is None:
            s = _jnp.sqrt(_jnp.mean(_jnp.square(w)) + 1e-30)
        else:
            s = MOMENT_SCALE[name]
        km, kv = _jax.random.split(_jax.random.fold_in(key, i + 1))
        out[name] = w
        out["m_" + name] = s * _jax.random.normal(km, w.shape, _jnp.float32)
        out["v_" + name] = (s * s) * _jax.random.uniform(kv, w.shape, _jnp.float32, 0.5, 1.5)
    if N_MICROBATCH > 1:
        for name, axis in PER_EXAMPLE_BATCH_AXIS.items():
            out[name] = _to_microbatches(out[name], axis)
    return {'x': out['x'], 'meta': out['meta'], 'norm1_g': out['norm1_g'], 'w_in': out['w_in'], 'conv_w': out['conv_w'], 'conv_b': out['conv_b'], 'conv_ln_g': out['conv_ln_g'], 'conv_ln_b': out['conv_ln_b'], 'w_conv_out': out['w_conv_out'], 'q_a_norm_g': out['q_a_norm_g'], 'w_uq': out['w_uq'], 'kv_a_norm_g': out['kv_a_norm_g'], 'w_ukv': out['w_ukv'], 'q_norm_g': out['q_norm_g'], 'k_norm_g': out['k_norm_g'], 'w_attn_out': out['w_attn_out'], 'hgrn_lb_logits': out['hgrn_lb_logits'], 'hgrn_norm_g': out['hgrn_norm_g'], 'w_hgrn_out': out['w_hgrn_out'], 'w_out': out['w_out'], 'norm2_g': out['norm2_g'], 'w_ff1': out['w_ff1'], 'w_ff2': out['w_ff2'], 'loss_target': out['loss_target'], 'm_meta': out['m_meta'], 'm_norm1_g': out['m_norm1_g'], 'm_w_in': out['m_w_in'], 'm_conv_w': out['m_conv_w'], 'm_conv_b': out['m_conv_b'], 'm_conv_ln_g': out['m_conv_ln_g'], 'm_conv_ln_b': out['m_conv_ln_b'], 'm_w_conv_out': out['m_w_conv_out'], 'm_q_a_norm_g': out['m_q_a_norm_g'], 'm_w_uq': out['m_w_uq'], 'm_kv_a_norm_g': out['m_kv_a_norm_g'], 'm_w_ukv': out['m_w_ukv'], 'm_q_norm_g': out['m_q_norm_g'], 'm_k_norm_g': out['m_k_norm_g'], 'm_w_attn_out': out['m_w_attn_out'], 'm_hgrn_lb_logits': out['m_hgrn_lb_logits'], 'm_hgrn_norm_g': out['m_hgrn_norm_g'], 'm_w_hgrn_out': out['m_w_hgrn_out'], 'm_w_out': out['m_w_out'], 'm_norm2_g': out['m_norm2_g'], 'm_w_ff1': out['m_w_ff1'], 'm_w_ff2': out['m_w_ff2'], 'v_meta': out['v_meta'], 'v_norm1_g': out['v_norm1_g'], 'v_w_in': out['v_w_in'], 'v_conv_w': out['v_conv_w'], 'v_conv_b': out['v_conv_b'], 'v_conv_ln_g': out['v_conv_ln_g'], 'v_conv_ln_b': out['v_conv_ln_b'], 'v_w_conv_out': out['v_w_conv_out'], 'v_q_a_norm_g': out['v_q_a_norm_g'], 'v_w_uq': out['v_w_uq'], 'v_kv_a_norm_g': out['v_kv_a_norm_g'], 'v_w_ukv': out['v_w_ukv'], 'v_q_norm_g': out['v_q_norm_g'], 'v_k_norm_g': out['v_k_norm_g'], 'v_w_attn_out': out['v_w_attn_out'], 'v_hgrn_lb_logits': out['v_hgrn_lb_logits'], 'v_hgrn_norm_g': out['v_hgrn_norm_g'], 'v_w_hgrn_out': out['v_w_hgrn_out'], 'v_w_out': out['v_w_out'], 'v_norm2_g': out['v_norm2_g'], 'v_w_ff1': out['v_w_ff1'], 'v_w_ff2': out['v_w_ff2']}


def _loss(weights, diff, rest, loss_target):
    with _jax.named_scope("forward"):
        args = {**rest, TWIN_DIFF_INPUT: diff, **{k: w.astype(_WEIGHT_DTYPES[k]) for k, w in weights.items()}}
        y = _forward(args)
    with _jax.named_scope("loss_head"):
        err = _jnp.square(y.astype(_jnp.float32) - loss_target)
        return 0.5 * _jnp.sum(_jnp.mean(err, axis=-1)) if err.ndim else 0.5 * err


def _adamw(w, g, m, v):
    m = ADAM_B1 * m + (1.0 - ADAM_B1) * g
    v = ADAM_B2 * v + (1.0 - ADAM_B2) * _jnp.square(g)
    m_hat = m / (1.0 - ADAM_B1 ** ADAM_STEP)
    v_hat = v / (1.0 - ADAM_B2 ** ADAM_STEP)
    delta = -ADAM_LR * (m_hat / (_jnp.sqrt(v_hat) + ADAM_EPS) + ADAM_WD * w)
    return delta, m, v


def reference(x, meta, norm1_g, w_in, conv_w, conv_b, conv_ln_g, conv_ln_b, w_conv_out, q_a_norm_g, w_uq, kv_a_norm_g, w_ukv, q_norm_g, k_norm_g, w_attn_out, hgrn_lb_logits, hgrn_norm_g, w_hgrn_out, w_out, norm2_g, w_ff1, w_ff2, loss_target, m_meta, m_norm1_g, m_w_in, m_conv_w, m_conv_b, m_conv_ln_g, m_conv_ln_b, m_w_conv_out, m_q_a_norm_g, m_w_uq, m_kv_a_norm_g, m_w_ukv, m_q_norm_g, m_k_norm_g, m_w_attn_out, m_hgrn_lb_logits, m_hgrn_norm_g, m_w_hgrn_out, m_w_out, m_norm2_g, m_w_ff1, m_w_ff2, v_meta, v_norm1_g, v_w_in, v_conv_w, v_conv_b, v_conv_ln_g, v_conv_ln_b, v_w_conv_out, v_q_a_norm_g, v_w_uq, v_kv_a_norm_g, v_w_ukv, v_q_norm_g, v_k_norm_g, v_w_attn_out, v_hgrn_lb_logits, v_hgrn_norm_g, v_w_hgrn_out, v_w_out, v_norm2_g, v_w_ff1, v_w_ff2):
    given = dict(x=x, meta=meta, norm1_g=norm1_g, w_in=w_in, conv_w=conv_w, conv_b=conv_b, conv_ln_g=conv_ln_g, conv_ln_b=conv_ln_b, w_conv_out=w_conv_out, q_a_norm_g=q_a_norm_g, w_uq=w_uq, kv_a_norm_g=kv_a_norm_g, w_ukv=w_ukv, q_norm_g=q_norm_g, k_norm_g=k_norm_g, w_attn_out=w_attn_out, hgrn_lb_logits=hgrn_lb_logits, hgrn_norm_g=hgrn_norm_g, w_hgrn_out=w_hgrn_out, w_out=w_out, norm2_g=norm2_g, w_ff1=w_ff1, w_ff2=w_ff2, loss_target=loss_target, m_meta=m_meta, m_norm1_g=m_norm1_g, m_w_in=m_w_in, m_conv_w=m_conv_w, m_conv_b=m_conv_b, m_conv_ln_g=m_conv_ln_g, m_conv_ln_b=m_conv_ln_b, m_w_conv_out=m_w_conv_out, m_q_a_norm_g=m_q_a_norm_g, m_w_uq=m_w_uq, m_kv_a_norm_g=m_kv_a_norm_g, m_w_ukv=m_w_ukv, m_q_norm_g=m_q_norm_g, m_k_norm_g=m_k_norm_g, m_w_attn_out=m_w_attn_out, m_hgrn_lb_logits=m_hgrn_lb_logits, m_hgrn_norm_g=m_hgrn_norm_g, m_w_hgrn_out=m_w_hgrn_out, m_w_out=m_w_out, m_norm2_g=m_norm2_g, m_w_ff1=m_w_ff1, m_w_ff2=m_w_ff2, v_meta=v_meta, v_norm1_g=v_norm1_g, v_w_in=v_w_in, v_conv_w=v_conv_w, v_conv_b=v_conv_b, v_conv_ln_g=v_conv_ln_g, v_conv_ln_b=v_conv_ln_b, v_w_conv_out=v_w_conv_out, v_q_a_norm_g=v_q_a_norm_g, v_w_uq=v_w_uq, v_kv_a_norm_g=v_kv_a_norm_g, v_w_ukv=v_w_ukv, v_q_norm_g=v_q_norm_g, v_k_norm_g=v_k_norm_g, v_w_attn_out=v_w_attn_out, v_hgrn_lb_logits=v_hgrn_lb_logits, v_hgrn_norm_g=v_hgrn_norm_g, v_w_hgrn_out=v_w_hgrn_out, v_w_out=v_w_out, v_norm2_g=v_norm2_g, v_w_ff1=v_w_ff1, v_w_ff2=v_w_ff2)
    weights = {n: given[n] for n in TWIN_WEIGHTS}
    shared = {n: given[n] for n in SHARED_INPUTS}
    per_example = {n: given[n] for n in ['x']}
    grad_fn = _jax.value_and_grad(_loss, argnums=(0, 1))

    def one_microbatch(ex, loss_target):
        ex = dict(ex)
        diff = ex.pop(TWIN_DIFF_INPUT)
        return grad_fn(weights, diff, {**shared, **ex}, loss_target)

    if N_MICROBATCH == 1:
        loss, (grad_w, grad_x) = one_microbatch(per_example, given["loss_target"])
    else:
        def body(carry, xs):
            loss_sum, grad_sum = carry
            l_k, (gw_k, gx_k) = one_microbatch(xs[0], xs[1])
            with _jax.named_scope("update"):
                return (loss_sum + l_k, _jax.tree.map(_jnp.add, grad_sum, gw_k)), gx_k

        init = (_jnp.zeros((), _jnp.float32), _jax.tree.map(_jnp.zeros_like, weights))
        (loss, grad_w), grad_x = _jax.lax.scan(body, init, (per_example, given["loss_target"]))
    with _jax.named_scope("update"):
        delta_w, new_m, new_v = {}, {}, {}
        for n in TWIN_WEIGHTS:
            delta_w[n], new_m[n], new_v[n] = _adamw(weights[n], grad_w[n], given["m_" + n], given["v_" + n])
    return (loss, grad_x, *[grad_w[n] for n in TWIN_WEIGHTS], *[delta_w[n] for n in TWIN_WEIGHTS],
            *[new_m[n] for n in TWIN_WEIGHTS], *[new_v[n] for n in TWIN_WEIGHTS])
```

```python
import functools

import jax
import jax.numpy as jnp
from jax import lax
from jax.experimental import pallas as pl
from jax.experimental.pallas import tpu as pltpu

F32 = jnp.float32
_MM = jnp.bfloat16

D = 1024
N_META = 16
FRONT = 48
ROW0 = FRONT + N_META
EPS = 1e-6
GATE_CLAMP = 1.0 - 1e-6
CONV_K = 31
CONV_DIM = 512
NH = 8
QK_DIM = 96
ATT_SCALE = QK_DIM ** -0.5
HH = 4
CHUNK = 64
SUB = 16
EXP_CLIP = 60.0
NEG = -1e30
LANE = 128

SEG_GATES = (0, 3072)
SEG_AG = (3072, 4096)
SEG_H4 = (4096, 6144)
SEG_CQ = (6144, 6400)
SEG_CKV = (6400, 6528)
SEG_KR = (6528, 6656)
N_IN_P = 6656

ADAM_LR = 0.001
ADAM_B1 = 0.9
ADAM_B2 = 0.999
ADAM_EPS = 1e-08
ADAM_WD = 0.01
ADAM_STEP = 10

VMEM_LIMIT = 56 * 1024 * 1024


def _tile(n, pref):
    best = 64
    for t in range(64, pref + 1, 64):
        if n % t == 0:
            best = t
    return best


def _cp(*sem):
    return pltpu.CompilerParams(dimension_semantics=tuple(sem), vmem_limit_bytes=VMEM_LIMIT)


def _row(tm, n, col=0):
    return pl.BlockSpec((tm, n), lambda i: (i, col))


def _full(shape):
    return pl.BlockSpec(shape, lambda i: (0,) * len(shape))


def _mm(a, b):
    return jnp.dot(a.astype(_MM), b.astype(_MM), preferred_element_type=F32)


def _mm_nt(a, b):
    return lax.dot_general(a.astype(_MM), b.astype(_MM), (((1,), (1,)), ((), ())), preferred_element_type=F32)


def _mm_tn(a, b):
    return lax.dot_general(a.astype(_MM), b.astype(_MM), (((0,), (0,)), ((), ())), preferred_element_type=F32)


def _split3(x):
    hi = x.astype(jnp.bfloat16)
    return hi, (x - hi.astype(F32)).astype(jnp.bfloat16)


def _dot3(a, b, dims):
    ah, al = _split3(a)
    bh, bl = _split3(b)
    dg = lambda u, v: lax.dot_general(u, v, (dims, ((), ())), preferred_element_type=F32)
    return dg(ah, bh) + (dg(ah, bl) + dg(al, bh))


def _hmm(a, b):
    return _dot3(a, b, ((1,), (0,)))


def _hmm_nt(a, b):
    return _dot3(a, b, ((1,), (1,)))


def _hmm_tn(a, b):
    return _dot3(a, b, ((0,), (0,)))


def _sigmoid(x):
    return 1.0 / (1.0 + jnp.exp(-x))


def _rstd(x, n=None):
    n = x.shape[-1] if n is None else n
    return lax.rsqrt(jnp.sum(x * x, axis=-1, keepdims=True) * (1.0 / n) + EPS)


def _rms_bwd(dy, x, rstd, g, n=None):
    n = x.shape[-1] if n is None else n
    xh = x * rstd
    dxh = dy * g
    dx = rstd * (dxh - xh * (jnp.sum(dxh * xh, axis=-1, keepdims=True) * (1.0 / n)))
    return dx, dy * xh


def _valid_rows(i, tm, t_valid_end):
    r = i * tm + lax.broadcasted_iota(jnp.int32, (tm, 1), 0)
    return ((r >= FRONT) & (r < t_valid_end)).astype(F32)


def _colsum8(x):
    n, c = x.shape
    return jnp.sum(x.reshape(n // 8, 8, c), axis=0)


def _in_proj_fwd(x, g1, w):
    t = x.shape[0]
    tm = _tile(t, 192)
    segs = (SEG_GATES, SEG_AG, SEG_H4, SEG_CQ, SEG_CKV, SEG_KR)

    def body(x_ref, g_ref, w_ref, gates_ref, ag_ref, h4_ref, cq_ref, ckv_ref, kr_ref, hb_ref):
        xv = x_ref[...]
        hb = (xv * _rstd(xv) * g_ref[...]).astype(_MM)
        hb_ref[...] = hb
        for ref, (a, b) in zip((gates_ref, ag_ref, h4_ref, cq_ref, ckv_ref, kr_ref), segs):
            ref[...] = jnp.dot(hb, w_ref[:, a:b], preferred_element_type=F32)

    outs = [jax.ShapeDtypeStruct((t, b - a), F32) for a, b in segs] + [jax.ShapeDtypeStruct((t, D), _MM)]
    return pl.pallas_call(
        body, name="in_proj_fwd", grid=(t // tm,),
        in_specs=[_row(tm, D), _full((1, D)), _full((D, N_IN_P))],
        out_specs=[_row(tm, b - a) for a, b in segs] + [_row(tm, D)],
        out_shape=outs, compiler_params=_cp("parallel"),
    )(x, g1, w)


def _in_proj_bwd(du, x, dx1, g1, wt, t_end):
    t = x.shape[0]
    tm = _tile(t, 192)

    def body(du_ref, x_ref, dx1_ref, g_ref, wt_ref, dx_ref, dg_ref):
        i = pl.program_id(0)
        dh = jnp.dot(du_ref[...], wt_ref[...], preferred_element_type=F32)
        xv = x_ref[...]
        dxn, dgrow = _rms_bwd(dh, xv, _rstd(xv), g_ref[...])
        dx_ref[...] = _valid_rows(i, tm, t_end) * (dx1_ref[...] + dxn)

        @pl.when(i == 0)
        def _():
            dg_ref[...] = jnp.zeros_like(dg_ref)
        dg_ref[...] += _colsum8(dgrow)

    return pl.pallas_call(
        body, name="in_proj_bwd", grid=(t // tm,),
        in_specs=[_row(tm, N_IN_P), _row(tm, D), _row(tm, D), _full((1, D)), _full((N_IN_P, D))],
        out_specs=[_row(tm, D), _full((8, D))],
        out_shape=[jax.ShapeDtypeStruct((t, D), F32), jax.ShapeDtypeStruct((8, D), F32)],
        compiler_params=_cp("arbitrary"),
    )(du, x, dx1, g1, wt)


CONV_CH = 128


def _conv_fwd(ag, cw, cb):
    t = ag.shape[0]
    n = t // CONV_CH

    def body(a_ref, g_ref, w_ref, b_ref, z_ref, hp):
        hp[0:32, :] = jnp.zeros((32, LANE), F32)

        def fill(i, c):
            r = pl.multiple_of(i * CONV_CH, CONV_CH)
            hp[pl.ds(32 + r, CONV_CH), :] = a_ref[pl.ds(r, CONV_CH), :] * _sigmoid(g_ref[pl.ds(r, CONV_CH), :])
            return c
        lax.fori_loop(0, n, fill, 0)

        def conv(i, c):
            r = pl.multiple_of(i * CONV_CH, CONV_CH)
            acc = jnp.broadcast_to(b_ref[...], (CONV_CH, LANE))
            for k in range(CONV_K):
                acc = acc + w_ref[k:k + 1, :] * hp[pl.ds(r + (k + 2), CONV_CH), :]
            z_ref[pl.ds(r, CONV_CH), :] = acc
            return c
        lax.fori_loop(0, n, conv, 0)

    nb = CONV_DIM // LANE
    return pl.pallas_call(
        body, name="conv_fwd", grid=(nb,),
        in_specs=[pl.BlockSpec((t, LANE), lambda j: (0, j)), pl.BlockSpec((t, LANE), lambda j: (0, nb + j)),
                  pl.BlockSpec((32, LANE), lambda j: (0, j)), pl.BlockSpec((1, LANE), lambda j: (0, j))],
        out_specs=pl.BlockSpec((t, LANE), lambda j: (0, j)),
        out_shape=jax.ShapeDtypeStruct((t, CONV_DIM), F32),
        scratch_shapes=[pltpu.VMEM((t + 32, LANE), F32)],
        compiler_params=_cp("parallel"),
    )(ag, ag, cw, cb)


def _conv_bwd(ag, cw, dz):
    t = ag.shape[0]
    n = t // CONV_CH

    def body(a_ref, g_ref, w_ref, dz_ref, da_ref, dg_ref, dcw_ref, hp, dzp, accw):
        hp[0:32, :] = jnp.zeros((32, LANE), F32)
        dzp[pl.ds(t, 32), :] = jnp.zeros((32, LANE), F32)
        accw[...] = jnp.zeros_like(accw)

        def fill(i, c):
            r = pl.multiple_of(i * CONV_CH, CONV_CH)
            hp[pl.ds(32 + r, CONV_CH), :] = a_ref[pl.ds(r, CONV_CH), :] * _sigmoid(g_ref[pl.ds(r, CONV_CH), :])
            dzp[pl.ds(r, CONV_CH), :] = dz_ref[pl.ds(r, CONV_CH), :]
            return c
        lax.fori_loop(0, n, fill, 0)

        def step(i, c):
            r = pl.multiple_of(i * CONV_CH, CONV_CH)
            dzc = dz_ref[pl.ds(r, CONV_CH), :]
            dh = jnp.zeros((CONV_CH, LANE), F32)
            for k in range(CONV_K):
                dh = dh + w_ref[k:k + 1, :] * dzp[pl.ds(r + (CONV_K - 1 - k), CONV_CH), :]
                accw[8 * k:8 * k + 8, :] += _colsum8(dzc * hp[pl.ds(r + (k + 2), CONV_CH), :])
            a = a_ref[pl.ds(r, CONV_CH), :]
            sg = _sigmoid(g_ref[pl.ds(r, CONV_CH), :])
            da_ref[pl.ds(r, CONV_CH), :] = dh * sg
            dg_ref[pl.ds(r, CONV_CH), :] = dh * a * sg * (1.0 - sg)
            return c
        lax.fori_loop(0, n, step, 0)

        for k in range(CONV_K):
            dcw_ref[k:k + 1, :] = jnp.sum(accw[8 * k:8 * k + 8, :], axis=0, keepdims=True)
        dcw_ref[CONV_K:32, :] = jnp.zeros((32 - CONV_K, LANE), F32)

    nb = CONV_DIM // LANE
    colspec = pl.BlockSpec((t, LANE), lambda j: (0, j))
    return pl.pallas_call(
        body, name="conv_bwd", grid=(nb,),
        in_specs=[colspec, pl.BlockSpec((t, LANE), lambda j: (0, nb + j)),
                  pl.BlockSpec((32, LANE), lambda j: (0, j)), colspec],
        out_specs=[colspec, colspec, pl.BlockSpec((32, LANE), lambda j: (0, j))],
        out_shape=[jax.ShapeDtypeStruct((t, CONV_DIM), F32), jax.ShapeDtypeStruct((t, CONV_DIM), F32),
                   jax.ShapeDtypeStruct((32, CONV_DIM), F32)],
        scratch_shapes=[pltpu.VMEM((t + 32, LANE), F32), pltpu.VMEM((t + 32, LANE), F32),
                        pltpu.VMEM((8 * 32, LANE), F32)],
        compiler_params=_cp("parallel"),
    )(ag, ag, cw, dz)


def _rope(x, c, s1, s2):
    return x * c + pltpu.roll(x, LANE - 16, 1) * s1 + pltpu.roll(x, 16, 1) * s2


def _rope_t(dy, c, s1, s2):
    return dy * c + pltpu.roll(dy * s1, 16, 1) + pltpu.roll(dy * s2, LANE - 16, 1)


def _mla_pre_fwd(cq, ckv, kr, qag, wuq, kvag, wk, wv, qng, kng, rc, rs1, rs2):
    t = cq.shape[0]
    tm = _tile(t, 384)

    def body(cq_ref, ckv_ref, kr_ref, qag_ref, wuq_ref, kvag_ref, wk_ref, wv_ref, qng_ref, kng_ref,
             c_ref, s1_ref, s2_ref, q_ref, k_ref, v_ref, cqn_ref, ckvn_ref):
        cqv = cq_ref[...]
        cqn = (cqv * _rstd(cqv) * qag_ref[...]).astype(_MM)
        cqn_ref[...] = cqn
        ckvv = ckv_ref[...]
        ckvn = (ckvv * _rstd(ckvv) * kvag_ref[...]).astype(_MM)
        ckvn_ref[...] = ckvn
        qraw = jnp.dot(cqn, wuq_ref[...], preferred_element_type=F32)
        kraw = jnp.dot(ckvn, wk_ref[...], preferred_element_type=F32)
        v_ref[...] = jnp.dot(ckvn, wv_ref[...], preferred_element_type=F32).astype(_MM)
        krv = kr_ref[...]
        c, s1, s2 = c_ref[...], s1_ref[...], s2_ref[...]
        for h in range(NH):
            sl = slice(LANE * h, LANE * (h + 1))
            qh = qraw[:, sl]
            qn = qh * _rstd(qh, QK_DIM) * qng_ref[...]
            q_ref[:, sl] = (_rope(qn, c, s1, s2) * ATT_SCALE).astype(_MM)
            kh = kraw[:, sl] + krv
            kn = kh * _rstd(kh, QK_DIM) * kng_ref[...]
            k_ref[:, sl] = _rope(kn, c, s1, s2).astype(_MM)

    hd = NH * LANE
    return pl.pallas_call(
        body, name="mla_pre_fwd", grid=(t // tm,),
        in_specs=[_row(tm, 256), _row(tm, 128), _row(tm, 128), _full((1, 256)), _full((256, hd)),
                  _full((1, 128)), _full((128, hd)), _full((128, hd)), _full((1, LANE)), _full((1, LANE)),
                  _row(tm, LANE), _row(tm, LANE), _row(tm, LANE)],
        out_specs=[_row(tm, hd), _row(tm, hd), _row(tm, hd), _row(tm, 256), _row(tm, 128)],
        out_shape=[jax.ShapeDtypeStruct((t, hd), _MM)] * 3 + [jax.ShapeDtypeStruct((t, 256), _MM),
                                                              jax.ShapeDtypeStruct((t, 128), _MM)],
        compiler_params=_cp("parallel"),
    )(cq, ckv, kr, qag, wuq, kvag, wk, wv, qng, kng, rc, rs1, rs2)


def _mla_pre_bwd(dq, dk, dv, cq, ckv, kr, qag, wuq, wuqt, kvag, wk, wkt, wvt, qng, kng, rc, rs1, rs2):
    t = cq.shape[0]
    tm = _tile(t, 192)
    hd = NH * LANE

    def body(dq_ref, dk_ref, dv_ref, cq_ref, ckv_ref, kr_ref, qag_ref, wuq_ref, wuqt_ref, kvag_ref, wk_ref,
             wkt_ref, wvt_ref, qng_ref, kng_ref, c_ref, s1_ref, s2_ref,
             dcq_ref, dckv_ref, dkr_ref, dqraw_ref, dkraw_ref, dqag_ref, dkvag_ref, dqng_ref, dkng_ref):
        i = pl.program_id(0)
        cqv = cq_ref[...]
        rq_in = _rstd(cqv)
        cqn = (cqv * rq_in * qag_ref[...]).astype(_MM)
        ckvv = ckv_ref[...]
        rkv_in = _rstd(ckvv)
        ckvn = (ckvv * rkv_in * kvag_ref[...]).astype(_MM)
        qraw = jnp.dot(cqn, wuq_ref[...], preferred_element_type=F32)
        kraw = jnp.dot(ckvn, wk_ref[...], preferred_element_type=F32)
        krv = kr_ref[...]
        c, s1, s2 = c_ref[...], s1_ref[...], s2_ref[...]
        dkr = jnp.zeros((tm, LANE), F32)
        dqng = jnp.zeros((8, LANE), F32)
        dkng = jnp.zeros((8, LANE), F32)
        for h in range(NH):
            sl = slice(LANE * h, LANE * (h + 1))
            qh = qraw[:, sl]
            dqn = _rope_t(dq_ref[:, sl] * ATT_SCALE, c, s1, s2)
            dqh, gq = _rms_bwd(dqn, qh, _rstd(qh, QK_DIM), qng_ref[...], QK_DIM)
            dqraw_ref[:, sl] = dqh.astype(_MM)
            dqng = dqng + _colsum8(gq)
            kh = kraw[:, sl] + krv
            dkn = _rope_t(dk_ref[:, sl], c, s1, s2)
            dkh, gk = _rms_bwd(dkn, kh, _rstd(kh, QK_DIM), kng_ref[...], QK_DIM)
            dkraw_ref[:, sl] = dkh.astype(_MM)
            dkr = dkr + dkh
            dkng = dkng + _colsum8(gk)
        dkr_ref[...] = dkr.astype(_MM)
        dcqn = jnp.dot(dqraw_ref[...], wuqt_ref[...], preferred_element_type=F32)
        dcq, gqa = _rms_bwd(dcqn, cqv, rq_in, qag_ref[...])
        dcq_ref[...] = dcq.astype(_MM)
        dckvn = (jnp.dot(dkraw_ref[...], wkt_ref[...], preferred_element_type=F32)
                 + jnp.dot(dv_ref[...].astype(_MM), wvt_ref[...], preferred_element_type=F32))
        dckv, gkva = _rms_bwd(dckvn, ckvv, rkv_in, kvag_ref[...])
        dckv_ref[...] = dckv.astype(_MM)

        @pl.when(i == 0)
        def _():
            dqag_ref[...] = jnp.zeros_like(dqag_ref)
            dkvag_ref[...] = jnp.zeros_like(dkvag_ref)
            dqng_ref[...] = jnp.zeros_like(dqng_ref)
            dkng_ref[...] = jnp.zeros_like(dkng_ref)
        dqag_ref[...] += _colsum8(gqa)
        dkvag_ref[...] += _colsum8(gkva)
        dqng_ref[...] += dqng
        dkng_ref[...] += dkng

    return pl.pallas_call(
        body, name="mla_pre_bwd", grid=(t // tm,),
        in_specs=[_row(tm, hd), _row(tm, hd), _row(tm, hd), _row(tm, 256), _row(tm, 128), _row(tm, 128),
                  _full((1, 256)), _full((256, hd)), _full((hd, 256)), _full((1, 128)), _full((128, hd)),
                  _full((hd, 128)), _full((hd, 128)), _full((1, LANE)), _full((1, LANE)),
                  _row(tm, LANE), _row(tm, LANE), _row(tm, LANE)],
        out_specs=[_row(tm, 256), _row(tm, 128), _row(tm, 128), _row(tm, hd), _row(tm, hd),
                   _full((8, 256)), _full((8, 128)), _full((8, LANE)), _full((8, LANE))],
        out_shape=[jax.ShapeDtypeStruct((t, 256), _MM), jax.ShapeDtypeStruct((t, 128), _MM),
                   jax.ShapeDtypeStruct((t, 128), _MM), jax.ShapeDtypeStruct((t, hd), _MM),
                   jax.ShapeDtypeStruct((t, hd), _MM), jax.ShapeDtypeStruct((8, 256), F32),
                   jax.ShapeDtypeStruct((8, 128), F32), jax.ShapeDtypeStruct((8, LANE), F32),
                   jax.ShapeDtypeStruct((8, LANE), F32)],
        compiler_params=_cp("arbitrary"),
    )(dq, dk, dv, cq, ckv, kr, qag, wuq, wuqt, kvag, wk, wkt, wvt, qng, kng, rc, rs1, rs2)


def _attn_mask(r0, c0, tq):
    rows = r0 + lax.broadcasted_iota(jnp.int32, (tq, 1), 0)
    cols = c0 + lax.broadcasted_iota(jnp.int32, (1, tq), 1)
    return (cols <= rows) & (cols >= FRONT)


def _attn_fwd(q, k, v):
    t = q.shape[0]
    tq = _tile(t, 384)
    nq = t // tq

    def body(q_ref, k_ref, v_ref, o_ref, lse_ref):
        def qloop(qi, carry):
            r0 = pl.multiple_of(qi * tq, tq)
            qb = q_ref[pl.ds(r0, tq), :]

            def kloop(kj, st):
                m, l, acc = st
                c0 = pl.multiple_of(kj * tq, tq)
                s = _mm_nt(qb, k_ref[pl.ds(c0, tq), :])
                s = jnp.where(_attn_mask(r0, c0, tq), s, NEG)
                m2 = jnp.maximum(m, jnp.max(s, axis=-1, keepdims=True))
                p = jnp.exp(s - m2)
                a = jnp.exp(m - m2)
                l = a * l + jnp.sum(p, axis=-1, keepdims=True)
                acc = a * acc + _mm(p, v_ref[pl.ds(c0, tq), :])
                return m2, l, acc

            m, l, acc = lax.fori_loop(
                0, qi + 1, kloop,
                (jnp.full((tq, 1), NEG, F32), jnp.zeros((tq, 1), F32), jnp.zeros((tq, LANE), F32)))
            o_ref[pl.ds(r0, tq), :] = acc / l
            lse_ref[pl.ds(r0, tq), :] = m + jnp.log(l)
            return carry
        lax.fori_loop(0, nq, qloop, 0)

    hs = pl.BlockSpec((t, LANE), lambda h: (0, h))
    return pl.pallas_call(
        body, name="attn_fwd", grid=(NH,),
        in_specs=[hs, hs, hs],
        out_specs=[hs, pl.BlockSpec((None, t, 1), lambda h: (h, 0, 0))],
        out_shape=[jax.ShapeDtypeStruct((t, NH * LANE), F32), jax.ShapeDtypeStruct((NH, t, 1), F32)],
        compiler_params=_cp("parallel"),
    )(q, k, v)


def _attn_bwd(q, k, v, o, lse, do):
    t = q.shape[0]
    tq = _tile(t, 384)
    nq = t // tq

    def body(q_ref, k_ref, v_ref, o_ref, lse_ref, do_ref, dq_ref, dk_ref, dv_ref, delta):
        def prep(i, c):
            r0 = pl.multiple_of(i * tq, tq)
            delta[pl.ds(r0, tq), :] = jnp.sum(do_ref[pl.ds(r0, tq), :] * o_ref[pl.ds(r0, tq), :], axis=-1,
                                              keepdims=True)
            dq_ref[pl.ds(r0, tq), :] = jnp.zeros((tq, LANE), F32)
            return c
        lax.fori_loop(0, nq, prep, 0)

        def kloop(kj, carry):
            c0 = pl.multiple_of(kj * tq, tq)
            kb = k_ref[pl.ds(c0, tq), :]
            vb = v_ref[pl.ds(c0, tq), :]

            def qloop(qi, st):
                dkb, dvb = st
                r0 = pl.multiple_of(qi * tq, tq)
                qb = q_ref[pl.ds(r0, tq), :]
                dob = do_ref[pl.ds(r0, tq), :].astype(_MM)
                s = _mm_nt(qb, kb)
                s = jnp.where(_attn_mask(r0, c0, tq), s, NEG)
                p = jnp.exp(s - lse_ref[pl.ds(r0, tq), :])
                dvb = dvb + _mm_tn(p, dob)
                dp = _mm_nt(dob, vb)
                ds = (p * (dp - delta[pl.ds(r0, tq), :])).astype(_MM)
                dkb = dkb + _mm_tn(ds, qb)
                dq_ref[pl.ds(r0, tq), :] += _mm(ds, kb)
                return dkb, dvb

            dkb, dvb = lax.fori_loop(kj, nq, qloop, (jnp.zeros((tq, LANE), F32), jnp.zeros((tq, LANE), F32)))
            dk_ref[pl.ds(c0, tq), :] = dkb
            dv_ref[pl.ds(c0, tq), :] = dvb
            return carry
        lax.fori_loop(0, nq, kloop, 0)

    hs = pl.BlockSpec((t, LANE), lambda h: (0, h))
    return pl.pallas_call(
        body, name="attn_bwd", grid=(NH,),
        in_specs=[hs, hs, hs, hs, pl.BlockSpec((None, t, 1), lambda h: (h, 0, 0)), hs],
        out_specs=[hs, hs, hs],
        out_shape=[jax.ShapeDtypeStruct((t, NH * LANE), F32)] * 3,
        scratch_shapes=[pltpu.VMEM((t, 1), F32)],
        compiler_params=_cp("parallel"),
    )(q, k, v, o, lse, do)


def _cumsum_rows(x):
    n = x.shape[0]
    rows = lax.broadcasted_iota(jnp.int32, (n, 1), 0)
    d = 1
    while d < n:
        x = x + jnp.where(rows >= d, pltpu.roll(x, d, 0), 0.0)
        d *= 2
    return x


def _revcumsum_rows(x):
    n = x.shape[0]
    rows = lax.broadcasted_iota(jnp.int32, (n, 1), 0)
    d = 1
    while d < n:
        x = x + jnp.where(rows < n - d, pltpu.roll(x, n - d, 0), 0.0)
        d *= 2
    return x


def _hgrn_gates(f, lb):
    sneg = _sigmoid(-f)
    kk = (1.0 - lb) * sneg
    lf = jnp.log1p(-jnp.minimum(kk, GATE_CLAMP))
    return kk, lf, sneg


def _silu(x):
    return x * _sigmoid(x)


def _dsilu(x):
    s = _sigmoid(x)
    return s * (1.0 + x * (1.0 - s))


def _hgrn_intra(q, kk, b):
    parts = []
    for blk in range(CHUNK // SUB):
        lo = blk * SUB
        ref = jnp.zeros((1, LANE), F32) if blk == 0 else b[lo - 1:lo, :]
        eq = jnp.exp(b[lo:lo + SUB, :] - ref)
        ek = jnp.exp(jnp.minimum(ref - b, EXP_CLIP))
        parts.append((q[lo:lo + SUB, :] * eq, kk * ek, eq, ek))
    return parts


def _chunk_causal():
    return lax.broadcasted_iota(jnp.int32, (CHUNK, CHUNK), 1) <= lax.broadcasted_iota(jnp.int32, (CHUNK, CHUNK), 0)


def _hgrn_fwd(h4, lb):
    t = h4.shape[0]
    nc = t // CHUNK

    def body(q_ref, f_ref, i_ref, lb_ref, o_ref, s_ref, st):
        st[...] = jnp.zeros_like(st)
        causal = _chunk_causal()

        def chunk(c, carry):
            r0 = pl.multiple_of(c * CHUNK, CHUNK)
            q = q_ref[pl.ds(r0, CHUNK), :]
            kk, lf, _ = _hgrn_gates(f_ref[pl.ds(r0, CHUNK), :], lb_ref[...])
            v = _silu(i_ref[pl.ds(r0, CHUNK), :])
            b = _cumsum_rows(lf)
            s_prev = st[...]
            s_ref[c] = s_prev
            o = _hmm_nt(q * jnp.exp(b), s_prev)
            a = jnp.concatenate([_hmm_nt(qs, ks) for qs, ks, _, _ in _hgrn_intra(q, kk, b)], axis=0)
            a = jnp.where(causal, a, 0.0)
            o_ref[pl.ds(r0, CHUNK), :] = o + _hmm(a, v)
            bl = b[CHUNK - 1:CHUNK, :]
            st[...] = s_prev * jnp.exp(bl) + _hmm_tn(v, kk * jnp.exp(bl - b))
            return carry
        lax.fori_loop(0, nc, chunk, 0)

    def col(j):
        return pl.BlockSpec((t, LANE), lambda h: (0, HH * j + h))
    return pl.pallas_call(
        body, name="hgrn_fwd", grid=(HH,),
        in_specs=[col(0), col(1), col(2), pl.BlockSpec((1, LANE), lambda h: (0, h))],
        out_specs=[pl.BlockSpec((t, LANE), lambda h: (0, h)),
                   pl.BlockSpec((None, nc, LANE, LANE), lambda h: (h, 0, 0, 0))],
        out_shape=[jax.ShapeDtypeStruct((t, HH * LANE), F32), jax.ShapeDtypeStruct((HH, nc, LANE, LANE), F32)],
        scratch_shapes=[pltpu.VMEM((LANE, LANE), F32)],
        compiler_params=_cp("parallel"),
    )(h4, h4, h4, lb)


def _hgrn_bwd(h4, lb, do, states):
    t = h4.shape[0]
    nc = t // CHUNK

    def body(q_ref, f_ref, i_ref, lb_ref, do_ref, s_ref, dq_ref, df_ref, di_ref, dlb_ref, dst, carry):
        dst[...] = jnp.zeros_like(dst)
        carry[...] = jnp.zeros_like(carry)
        dlb_ref[...] = jnp.zeros_like(dlb_ref)
        causal = _chunk_causal()

        def chunk(cc, cr):
            c = nc - 1 - cc
            r0 = pl.multiple_of(c * CHUNK, CHUNK)
            q = q_ref[pl.ds(r0, CHUNK), :]
            lbv = lb_ref[...]
            kk, lf, sneg = _hgrn_gates(f_ref[pl.ds(r0, CHUNK), :], lbv)
            iv = i_ref[pl.ds(r0, CHUNK), :]
            v = _silu(iv)
            b = _cumsum_rows(lf)
            s_prev = s_ref[c]
            ds_new = dst[...]
            dob = do_ref[pl.ds(r0, CHUNK), :]
            e = jnp.exp(b)
            qe = q * e
            bl = b[CHUNK - 1:CHUNK, :]
            etail = jnp.exp(bl - b)
            kd = kk * etail
            dq_inter = _hmm(dob, s_prev) * e
            dv = _hmm_nt(kd, ds_new)
            dkk = _hmm(v, ds_new) * etail
            parts = _hgrn_intra(q, kk, b)
            a = jnp.where(causal, jnp.concatenate([_hmm_nt(qs, ks) for qs, ks, _, _ in parts], axis=0), 0.0)
            da = jnp.where(causal, _hmm_nt(dob, v), 0.0)
            dv = dv + _hmm_tn(a, dob)
            dq_rows = []
            for blk, (qs, ks, eq, ek) in enumerate(parts):
                da_blk = da[blk * SUB:(blk + 1) * SUB, :]
                dq_rows.append(_hmm(da_blk, ks) * eq)
                dkk = dkk + _hmm_tn(da_blk, qs) * ek
            dq = dq_inter + jnp.concatenate(dq_rows, axis=0)
            dst[...] = ds_new * jnp.exp(bl) + _hmm_tn(dob, qe)
            g = q * dq - kk * dkk
            dlf = _revcumsum_rows(g) + carry[0:1, :]
            carry[0:1, :] += jnp.sum(g, axis=0, keepdims=True)
            dkk_tot = dkk + dlf * jnp.where(kk < GATE_CLAMP, -1.0 / (1.0 - kk), 0.0)
            dq_ref[pl.ds(r0, CHUNK), :] = dq
            df_ref[pl.ds(r0, CHUNK), :] = dkk_tot * (1.0 - lbv) * (-sneg * (1.0 - sneg))
            di_ref[pl.ds(r0, CHUNK), :] = dv * _dsilu(iv)
            dlb_ref[...] += _colsum8(dkk_tot * (-sneg))
            return cr
        lax.fori_loop(0, nc, chunk, 0)

    def col(j):
        return pl.BlockSpec((t, LANE), lambda h: (0, HH * j + h))
    hs = pl.BlockSpec((t, LANE), lambda h: (0, h))
    return pl.pallas_call(
        body, name="hgrn_bwd", grid=(HH,),
        in_specs=[col(0), col(1), col(2), pl.BlockSpec((1, LANE), lambda h: (0, h)), hs,
                  pl.BlockSpec((None, nc, LANE, LANE), lambda h: (h, 0, 0, 0))],
        out_specs=[hs, hs, hs, pl.BlockSpec((8, LANE), lambda h: (0, h))],
        out_shape=[jax.ShapeDtypeStruct((t, HH * LANE), F32)] * 3 + [jax.ShapeDtypeStruct((8, HH * LANE), F32)],
        scratch_shapes=[pltpu.VMEM((LANE, LANE), F32), pltpu.VMEM((8, LANE), F32)],
        compiler_params=_cp("parallel"),
    )(h4, h4, h4, lb, do, states)


def _ln_fwd(z, g, b):
    mu = jnp.mean(z, axis=-1, keepdims=True)
    zc = z - mu
    rstd = lax.rsqrt(jnp.mean(zc * zc, axis=-1, keepdims=True) + EPS)
    zh = zc * rstd
    return zh * g + b, zh, rstd


def _mix_fwd(x, z, o_att, o_h, h4, gates, lng, lnb, wco, wao, ng, who, wout, t_end):
    t = x.shape[0]
    tm = _tile(t, 192)

    def body(x_ref, z_ref, oa_ref, oh_ref, hg_ref, gt_ref, lng_ref, lnb_ref, wco_ref, wao_ref, ng_ref, who_ref,
             wout_ref, x1_ref, mix_ref, ca_ref, oc_ref, ya_ref, yb_ref, yc_ref):
        i = pl.program_id(0)
        ln, _, _ = _ln_fwd(z_ref[...], lng_ref[...], lnb_ref[...])
        ca = _silu(ln).astype(_MM)
        ca_ref[...] = ca
        ya = jnp.dot(ca, wco_ref[...], preferred_element_type=F32)
        yb = _mm(oa_ref[...], wao_ref[...])
        hg = hg_ref[...]
        for h in range(HH):
            sl = slice(LANE * h, LANE * (h + 1))
            oh = oh_ref[:, sl]
            oc_ref[:, sl] = (oh * _rstd(oh) * ng_ref[:, sl] * _silu(hg[:, sl])).astype(_MM)
        yc = jnp.dot(oc_ref[...], who_ref[...], preferred_element_type=F32)
        ya_ref[...] = ya
        yb_ref[...] = yb
        yc_ref[...] = yc
        mix = (_sigmoid(gt_ref[:, 0:D]) * ya + _sigmoid(gt_ref[:, D:2 * D]) * yb
               + _sigmoid(gt_ref[:, 2 * D:3 * D]) * yc).astype(_MM)
        mix_ref[...] = mix
        x1_ref[...] = x_ref[...] + _valid_rows(i, tm, t_end) * jnp.dot(mix, wout_ref[...],
                                                                       preferred_element_type=F32)

    hd = NH * LANE
    return pl.pallas_call(
        body, name="mix_fwd", grid=(t // tm,),
        in_specs=[_row(tm, D), _row(tm, CONV_DIM), _row(tm, hd), _row(tm, 512), _row(tm, 512, 3), _row(tm, 3 * D),
                  _full((1, 512)), _full((1, 512)), _full((512, D)), _full((hd, D)), _full((1, 512)),
                  _full((512, D)), _full((D, D))],
        out_specs=[_row(tm, D), _row(tm, D), _row(tm, 512), _row(tm, 512), _row(tm, D), _row(tm, D), _row(tm, D)],
        out_shape=[jax.ShapeDtypeStruct((t, D), F32), jax.ShapeDtypeStruct((t, D), _MM),
                   jax.ShapeDtypeStruct((t, 512), _MM), jax.ShapeDtypeStruct((t, 512), _MM),
                   jax.ShapeDtypeStruct((t, D), F32), jax.ShapeDtypeStruct((t, D), F32),
                   jax.ShapeDtypeStruct((t, D), F32)],
        compiler_params=_cp("parallel"),
    )(x, z, o_att, o_h, h4, gates, lng, lnb, wco, wao, ng, who, wout)


def _mix_bwd(dx1, ya, yb, yc, gates, z, o_h, h4, lng, lnb, ng, woutt, wcot, waot, whot):
    t = dx1.shape[0]
    tm = _tile(t, 192)
    hd = NH * LANE

    def body(dx1_ref, ya_ref, yb_ref, yc_ref, gt_ref, z_ref, oh_ref, hg_ref, lng_ref, lnb_ref, ng_ref,
             woutt_ref, wcot_ref, waot_ref, whot_ref,
             dgt_ref, dya_ref, dyb_ref, dyc_ref, dz_ref, doa_ref, doh_ref, dhg_ref,
             dlng_ref, dlnb_ref, dcb_ref, dng_ref):
        i = pl.program_id(0)
        dmix = _mm(dx1_ref[...], woutt_ref[...])
        dys = []
        for j, y_ref in enumerate((ya_ref, yb_ref, yc_ref)):
            sg = _sigmoid(gt_ref[:, j * D:(j + 1) * D])
            dgt_ref[:, j * D:(j + 1) * D] = (dmix * y_ref[...] * sg * (1.0 - sg)).astype(_MM)
            dys.append((dmix * sg).astype(_MM))
        dya_ref[...], dyb_ref[...], dyc_ref[...] = dys
        dca = jnp.dot(dys[0], wcot_ref[...], preferred_element_type=F32)
        ln, zh, rstd = _ln_fwd(z_ref[...], lng_ref[...], lnb_ref[...])
        dln = dca * _dsilu(ln)
        dzh = dln * lng_ref[...]
        dz = rstd * (dzh - jnp.mean(dzh, axis=-1, keepdims=True)
                     - zh * jnp.mean(dzh * zh, axis=-1, keepdims=True))
        dz_ref[...] = dz
        doa_ref[...] = jnp.dot(dys[1], waot_ref[...], preferred_element_type=F32)
        doc = jnp.dot(dys[2], whot_ref[...], preferred_element_type=F32)
        hg = hg_ref[...]
        dng_rows = []
        for h in range(HH):
            sl = slice(LANE * h, LANE * (h + 1))
            oh = oh_ref[:, sl]
            r = _rstd(oh)
            don = doc[:, sl] * _silu(hg[:, sl])
            dhg_ref[:, sl] = (doc[:, sl] * oh * r * ng_ref[:, sl] * _dsilu(hg[:, sl])).astype(_MM)
            doh, gn = _rms_bwd(don, oh, r, ng_ref[:, sl])
            doh_ref[:, sl] = doh
            dng_rows.append(_colsum8(gn))

        @pl.when(i == 0)
        def _():
            dlng_ref[...] = jnp.zeros_like(dlng_ref)
            dlnb_ref[...] = jnp.zeros_like(dlnb_ref)
            dcb_ref[...] = jnp.zeros_like(dcb_ref)
            dng_ref[...] = jnp.zeros_like(dng_ref)
        dlng_ref[...] += _colsum8(dln * zh)
        dlnb_ref[...] += _colsum8(dln)
        dcb_ref[...] += _colsum8(dz)
        dng_ref[...] += jnp.concatenate(dng_rows, axis=1)

    return pl.pallas_call(
        body, name="mix_bwd", grid=(t // tm,),
        in_specs=[_row(tm, D), _row(tm, D), _row(tm, D), _row(tm, D), _row(tm, 3 * D), _row(tm, 512), _row(tm, 512),
                  _row(tm, 512, 3), _full((1, 512)), _full((1, 512)), _full((1, 512)),
                  _full((D, D)), _full((D, 512)), _full((D, hd)), _full((D, 512))],
        out_specs=[_row(tm, 3 * D), _row(tm, D), _row(tm, D), _row(tm, D), _row(tm, 512), _row(tm, hd),
                   _row(tm, 512), _row(tm, 512), _full((8, 512)), _full((8, 512)), _full((8, 512)), _full((8, 512))],
        out_shape=[jax.ShapeDtypeStruct((t, 3 * D), _MM), jax.ShapeDtypeStruct((t, D), _MM),
                   jax.ShapeDtypeStruct((t, D), _MM), jax.ShapeDtypeStruct((t, D), _MM),
                   jax.ShapeDtypeStruct((t, 512), F32), jax.ShapeDtypeStruct((t, hd), F32),
                   jax.ShapeDtypeStruct((t, 512), F32), jax.ShapeDtypeStruct((t, 512), _MM)]
        + [jax.ShapeDtypeStruct((8, 512), F32)] * 4,
        compiler_params=_cp("arbitrary"),
    )(dx1, ya, yb, yc, gates, z, o_h, h4, lng, lnb, ng, woutt, wcot, waot, whot)


D_FF = 4096


def _ffn_fwd(x1, g2, w1, w2):
    t = x1.shape[0]
    tm = _tile(t, 192)

    def body(x1_ref, g_ref, w1_ref, w2_ref, x2_ref, p_ref):
        xv = x1_ref[...]
        h2 = (xv * _rstd(xv) * g_ref[...]).astype(_MM)
        p = jnp.dot(h2, w1_ref[...], preferred_element_type=F32)
        p_ref[...] = p
        r = jnp.maximum(p, 0.0)
        x2_ref[...] = xv + jnp.dot((r * r).astype(_MM), w2_ref[...], preferred_element_type=F32)

    return pl.pallas_call(
        body, name="ffn_fwd", grid=(t // tm,),
        in_specs=[_row(tm, D), _full((1, D)), _full((D, D_FF)), _full((D_FF, D))],
        out_specs=[_row(tm, D), _row(tm, D_FF)],
        out_shape=[jax.ShapeDtypeStruct((t, D), F32), jax.ShapeDtypeStruct((t, D_FF), F32)],
        compiler_params=_cp("parallel"),
    )(x1, g2, w1, w2)


def _ffn_bwd(dx2, x1, p, g2, w1t, w2t):
    t = x1.shape[0]
    tm = _tile(t, 192)

    def body(dx2_ref, x1_ref, p_ref, g_ref, w1t_ref, w2t_ref, dx1_ref, h2_ref, act_ref, dp_ref, dg_ref):
        i = pl.program_id(0)
        xv = x1_ref[...]
        rstd = _rstd(xv)
        h2_ref[...] = (xv * rstd * g_ref[...]).astype(_MM)
        r = jnp.maximum(p_ref[...], 0.0)
        act_ref[...] = (r * r).astype(_MM)
        dx2 = dx2_ref[...]
        da = _mm(dx2, w2t_ref[...])
        dp = (2.0 * r * da).astype(_MM)
        dp_ref[...] = dp
        dh2 = jnp.dot(dp, w1t_ref[...], preferred_element_type=F32)
        dxn, dgrow = _rms_bwd(dh2, xv, rstd, g_ref[...])
        dx1_ref[...] = dx2 + dxn

        @pl.when(i == 0)
        def _():
            dg_ref[...] = jnp.zeros_like(dg_ref)
        dg_ref[...] += _colsum8(dgrow)

    return pl.pallas_call(
        body, name="ffn_bwd", grid=(t // tm,),
        in_specs=[_row(tm, D), _row(tm, D), _row(tm, D_FF), _full((1, D)), _full((D_FF, D)), _full((D, D_FF))],
        out_specs=[_row(tm, D), _row(tm, D), _row(tm, D_FF), _row(tm, D_FF), _full((8, D))],
        out_shape=[jax.ShapeDtypeStruct((t, D), F32), jax.ShapeDtypeStruct((t, D), _MM),
                   jax.ShapeDtypeStruct((t, D_FF), _MM), jax.ShapeDtypeStruct((t, D_FF), _MM),
                   jax.ShapeDtypeStruct((8, D), F32)],
        compiler_params=_cp("arbitrary"),
    )(dx2, x1, p, g2, w1t, w2t)


def _wgrad(a, b, name):
    t, ka = a.shape
    nb = b.shape[1]
    tm = _tile(t, 384)
    tn = 512 if nb % 512 == 0 else nb

    def body(a_ref, b_ref, o_ref):
        @pl.when(pl.program_id(1) == 0)
        def _():
            o_ref[...] = jnp.zeros_like(o_ref)
        o_ref[...] += _mm_tn(a_ref[...], b_ref[...])

    return pl.pallas_call(
        body, name="wgrad_" + name, grid=(nb // tn, t // tm),
        in_specs=[pl.BlockSpec((tm, ka), lambda n, i: (i, 0)), pl.BlockSpec((tm, tn), lambda n, i: (i, n))],
        out_specs=pl.BlockSpec((ka, tn), lambda n, i: (0, n)),
        out_shape=jax.ShapeDtypeStruct((ka, nb), F32),
        compiler_params=_cp("parallel", "arbitrary"),
    )(a, b)


def _loss_head(y, target, t_end):
    t = y.shape[0]
    tm = _tile(t, 384)

    def body(y_ref, tg_ref, dy_ref, l_ref):
        i = pl.program_id(0)
        r = i * tm + lax.broadcasted_iota(jnp.int32, (tm, 1), 0)
        real = ((r >= ROW0) & (r < t_end)).astype(F32)
        diff = (y_ref[...] - tg_ref[...]) * real
        dy_ref[...] = diff * (1.0 / D)

        @pl.when(i == 0)
        def _():
            l_ref[...] = jnp.zeros_like(l_ref)
        sq = _colsum8(diff * diff)
        part = sq[:, 0:LANE]
        for j in range(1, D // LANE):
            part = part + sq[:, j * LANE:(j + 1) * LANE]
        l_ref[...] += part * (0.5 / D)

    return pl.pallas_call(
        body, name="loss_head", grid=(t // tm,),
        in_specs=[_row(tm, D), _row(tm, D)],
        out_specs=[_row(tm, D), _full((8, LANE))],
        out_shape=[jax.ShapeDtypeStruct((t, D), F32), jax.ShapeDtypeStruct((8, LANE), F32)],
        compiler_params=_cp("arbitrary"),
    )(y, target)


def _lower_bounds_fwd(logits):
    depth, n = logits.shape

    def body(l_ref, lb_ref):
        lg = l_ref[...]
        m = jnp.max(lg, axis=0, keepdims=True)
        e = jnp.exp(lg - m)
        p = e / jnp.sum(e, axis=0, keepdims=True)
        acc = jnp.zeros((1, n), F32)
        for l in range(depth):
            if l > 0:
                acc = acc + p[l:l + 1, :]
            lb_ref[l:l + 1, :] = acc

    return pl.pallas_call(body, name="lower_bounds_fwd", out_shape=jax.ShapeDtypeStruct((depth, n), F32))(logits)


def _lower_bounds_bwd(logits, dlb):
    depth, n = logits.shape

    def body(l_ref, dlb_ref, dl_ref):
        lg = l_ref[...]
        m = jnp.max(lg, axis=0, keepdims=True)
        e = jnp.exp(lg - m)
        p = e / jnp.sum(e, axis=0, keepdims=True)
        dps = [jnp.zeros((1, n), F32)]
        for j in range(1, depth):
            acc = jnp.zeros((1, n), F32)
            for l in range(j, depth):
                acc = acc + dlb_ref[l:l + 1, :]
            dps.append(acc)
        dot = jnp.zeros((1, n), F32)
        for j in range(depth):
            dot = dot + p[j:j + 1, :] * dps[j]
        for j in range(depth):
            dl_ref[j:j + 1, :] = p[j:j + 1, :] * (dps[j] - dot)

    return pl.pallas_call(body, name="lower_bounds_bwd", out_shape=jax.ShapeDtypeStruct((depth, n), F32))(logits, dlb)


def _adamw(w, g, m, v):
    rows, cols = w.shape
    tr = rows
    for cand in (552, 512, 256, 64, 32, 16, 8):
        if rows % cand == 0:
            tr = cand
            break

    def body(w_ref, g_ref, m_ref, v_ref, d_ref, mo_ref, vo_ref):
        gv = g_ref[...]
        mn = ADAM_B1 * m_ref[...] + (1.0 - ADAM_B1) * gv
        vn = ADAM_B2 * v_ref[...] + (1.0 - ADAM_B2) * (gv * gv)
        m_hat = mn / (1.0 - ADAM_B1 ** ADAM_STEP)
        v_hat = vn / (1.0 - ADAM_B2 ** ADAM_STEP)
        d_ref[...] = -ADAM_LR * (m_hat / (jnp.sqrt(v_hat) + ADAM_EPS) + ADAM_WD * w_ref[...])
        mo_ref[...] = mn
        vo_ref[...] = vn

    spec = pl.BlockSpec((tr, cols), lambda i: (i, 0))
    return pl.pallas_call(
        body, name="adamw", grid=(rows // tr,),
        in_specs=[spec] * 4, out_specs=[spec] * 3,
        out_shape=[jax.ShapeDtypeStruct((rows, cols), F32)] * 3,
        compiler_params=_cp("parallel"),
    )(w, g, m, v)


DEPTH = 2
BIG = ("w_in", "w_conv_out", "w_uq", "w_ukv", "w_attn_out", "w_hgrn_out", "w_out", "w_ff1", "w_ff2")
SMALL = ("norm1_g", "conv_b", "conv_ln_g", "conv_ln_b", "q_a_norm_g", "kv_a_norm_g", "q_norm_g", "k_norm_g",
         "hgrn_lb_logits", "hgrn_norm_g", "norm2_g")


def _pad_heads(w, nh, used, axis):
    shp = w.shape
    w = w.reshape(shp[:axis] + (nh, used) + shp[axis + 1:])
    pad = [(0, 0)] * w.ndim
    pad[axis + 1] = (0, LANE - used)
    w = jnp.pad(w, pad)
    return w.reshape(shp[:axis] + (nh * LANE,) + shp[axis + 1:])


def _unpad_heads(w, nh, used, axis):
    shp = w.shape
    w = w.reshape(shp[:axis] + (nh, LANE) + shp[axis + 1:])
    w = lax.slice_in_dim(w, 0, used, axis=axis + 1)
    return w.reshape(shp[:axis] + (nh * used,) + shp[axis + 1:])


def _prep_layer(wd, l):
    mm = lambda a: a.astype(_MM)
    w_in = mm(wd["w_in"][l])
    zc = lambda n: jnp.zeros((D, n), _MM)
    w_in_p = jnp.concatenate([w_in[:, 3488:6560], w_in[:, 0:1024], w_in[:, 1440:3488], w_in[:, 1024:1280],
                              w_in[:, 1280:1408], zc(64), w_in[:, 1408:1440], zc(32)], axis=1)
    wuq = _pad_heads(mm(wd["w_uq"][l]), NH, QK_DIM, 1)
    wukv = mm(wd["w_ukv"][l]).reshape(128, NH, 128)
    wk = _pad_heads(wukv[:, :, :64].reshape(128, NH * 64), NH, 64, 1)
    wv = _pad_heads(wukv[:, :, 64:].reshape(128, NH * 64), NH, 64, 1)
    wao = _pad_heads(mm(wd["w_attn_out"][l]), NH, 64, 0)
    row = lambda a: a.astype(F32).reshape(1, -1)
    p = dict(
        w_in=w_in_p, w_in_t=w_in_p.T, wuq=wuq, wuq_t=wuq.T, wk=wk, wk_t=wk.T, wv=wv, wv_t=wv.T,
        wao=wao, wao_t=wao.T, wco=mm(wd["w_conv_out"][l]), who=mm(wd["w_hgrn_out"][l]), wout=mm(wd["w_out"][l]),
        w1=mm(wd["w_ff1"][l]), w2=mm(wd["w_ff2"][l]),
        g1=row(wd["norm1_g"][l]), g2=row(wd["norm2_g"][l]),
        cw=jnp.pad(wd["conv_w"][l].astype(F32), ((0, 1), (0, 0))), cb=row(wd["conv_b"][l]),
        lng=row(wd["conv_ln_g"][l]), lnb=row(wd["conv_ln_b"][l]),
        qag=row(wd["q_a_norm_g"][l]), kvag=row(wd["kv_a_norm_g"][l]),
        qng=jnp.pad(row(wd["q_norm_g"][l]), ((0, 0), (0, LANE - QK_DIM))),
        kng=jnp.pad(row(wd["k_norm_g"][l]), ((0, 0), (0, LANE - QK_DIM))),
        ng=row(wd["hgrn_norm_g"][l]),
    )
    for k in ("wco", "who", "wout", "w1", "w2"):
        p[k + "_t"] = p[k].T
    return p


def _rope_tables(t):
    pos = (jnp.arange(t, dtype=jnp.int32) - FRONT).astype(F32)
    inv_freq = 10000.0 ** (-jnp.arange(16, dtype=F32) / 16)
    ang = pos[:, None] * inv_freq[None, :]
    cos, sin = jnp.cos(ang), jnp.sin(ang)
    one = jnp.ones((t, 64), F32)
    z16, z32, z64 = jnp.zeros((t, 16), F32), jnp.zeros((t, 32), F32), jnp.zeros((t, 64), F32)
    c = jnp.concatenate([one, cos, cos, z32], axis=1)
    s1 = jnp.concatenate([z64, -sin, z16, z32], axis=1)
    s2 = jnp.concatenate([z64, z16, sin, z32], axis=1)
    return c, s1, s2


def _layer_fwd(x, p, lb, rope, t_end):
    gates, ag, h4, cq, ckv, kr, hb = _in_proj_fwd(x, p["g1"], p["w_in"])
    z = _conv_fwd(ag, p["cw"], p["cb"])
    q, k, v, cqn, ckvn = _mla_pre_fwd(cq, ckv, kr, p["qag"], p["wuq"], p["kvag"], p["wk"], p["wv"], p["qng"],
                                      p["kng"], *rope)
    o_att, lse = _attn_fwd(q, k, v)
    o_h, states = _hgrn_fwd(h4, lb)
    x1, mix, ca, oc, ya, yb, yc = _mix_fwd(x, z, o_att, o_h, h4, gates, p["lng"], p["lnb"], p["wco"], p["wao"],
                                           p["ng"], p["who"], p["wout"], t_end)
    x2, pre = _ffn_fwd(x1, p["g2"], p["w1"], p["w2"])
    saved = dict(x=x, gates=gates, ag=ag, h4=h4, cq=cq, ckv=ckv, kr=kr, hb=hb, z=z, q=q, k=k, v=v, cqn=cqn,
                 ckvn=ckvn, o_att=o_att, lse=lse, o_h=o_h, states=states, x1=x1, mix=mix, ca=ca, oc=oc,
                 ya=ya, yb=yb, yc=yc, pre=pre)
    return x2, saved


def _layer_bwd(dx2, s, p, lb, rope, t_end):
    dx1, h2, act, dp, dg2 = _ffn_bwd(dx2, s["x1"], s["pre"], p["g2"], p["w1_t"], p["w2_t"])
    g = {"w_ff1": _wgrad(h2, dp, "ff1"), "w_ff2": _wgrad(act, dx2, "ff2"), "norm2_g": dg2.sum(0)}
    (dgt, dya, dyb, dyc, dz, doa, doh, dhg, dlng, dlnb, dcb, dng) = _mix_bwd(
        dx1, s["ya"], s["yb"], s["yc"], s["gates"], s["z"], s["o_h"], s["h4"], p["lng"], p["lnb"], p["ng"],
        p["wout_t"], p["wco_t"], p["wao_t"], p["who_t"])
    g["w_out"] = _wgrad(s["mix"], dx1, "out")
    g["w_conv_out"] = _wgrad(s["ca"], dya, "conv_out")
    g["w_attn_out"] = _unpad_heads(_wgrad(s["o_att"], dyb, "attn_out"), NH, 64, 0)
    g["w_hgrn_out"] = _wgrad(s["oc"], dyc, "hgrn_out")
    g["conv_ln_g"], g["conv_ln_b"], g["conv_b"], g["hgrn_norm_g"] = dlng.sum(0), dlnb.sum(0), dcb.sum(0), dng.sum(0)
    da, dg, dcw = _conv_bwd(s["ag"], p["cw"], dz)
    g["conv_w"] = dcw[:CONV_K]
    dq, dk, dv = _attn_bwd(s["q"], s["k"], s["v"], s["o_att"], s["lse"], doa)
    dcq, dckv, dkr, dqraw, dkraw, dqag, dkvag, dqng, dkng = _mla_pre_bwd(
        dq, dk, dv, s["cq"], s["ckv"], s["kr"], p["qag"], p["wuq"], p["wuq_t"], p["kvag"], p["wk"], p["wk_t"],
        p["wv_t"], p["qng"], p["kng"], *rope)
    g["w_uq"] = _unpad_heads(_wgrad(s["cqn"], dqraw, "uq"), NH, QK_DIM, 1)
    dwk = _unpad_heads(_wgrad(s["ckvn"], dkraw, "uk"), NH, 64, 1).reshape(128, NH, 64)
    dwv = _unpad_heads(_wgrad(s["ckvn"], dv, "uv"), NH, 64, 1).reshape(128, NH, 64)
    g["w_ukv"] = jnp.concatenate([dwk, dwv], axis=2).reshape(128, NH * 128)
    g["q_a_norm_g"], g["kv_a_norm_g"] = dqag.sum(0), dkvag.sum(0)
    g["q_norm_g"], g["k_norm_g"] = dqng.sum(0)[:QK_DIM], dkng.sum(0)[:QK_DIM]
    dhq, dhf, dhi, dlb = _hgrn_bwd(s["h4"], lb, doh, s["states"])
    mm = lambda a: a.astype(_MM)
    du = jnp.concatenate([dgt, mm(da), mm(dg), mm(dhq), mm(dhf), mm(dhi), dhg, dcq, dckv, dkr], axis=1)
    dx, dg1 = _in_proj_bwd(du, s["x"], dx1, p["g1"], p["w_in_t"], t_end)
    g["norm1_g"] = dg1.sum(0)
    dwin = _wgrad(s["hb"], du, "in")
    g["w_in"] = jnp.concatenate([dwin[:, 3072:4096], dwin[:, 6144:6400], dwin[:, 6400:6528],
                                 dwin[:, 6528 + 64:6528 + 96], dwin[:, 4096:6144], dwin[:, 0:3072]], axis=1)
    return dx, g, dlb.sum(0)


def _device_step(x, target, wd):
    s_real = x.shape[0]
    t_end = ROW0 + s_real
    t = -(-t_end // LANE) * LANE
    zrow = lambda n: jnp.zeros((n, D), F32)
    xp = jnp.concatenate([zrow(FRONT), wd["meta"].astype(F32), x, zrow(t - t_end)], axis=0)
    tp = jnp.concatenate([zrow(ROW0), target, zrow(t - t_end)], axis=0)
    rope = _rope_tables(t)
    logits = wd["hgrn_lb_logits"].astype(F32)
    lbs = _lower_bounds_fwd(logits)
    prm = [_prep_layer(wd, l) for l in range(DEPTH)]
    saved = []
    h = xp
    for l in range(DEPTH):
        h, sv = _layer_fwd(h, prm[l], lbs[l:l + 1], rope, t_end)
        saved.append(sv)
    dh, lsum = _loss_head(h, tp, t_end)
    loss = jnp.sum(lsum)
    grads = [None] * DEPTH
    dlbs = [None] * DEPTH
    for l in reversed(range(DEPTH)):
        dh, grads[l], dlbs[l] = _layer_bwd(dh, saved[l], prm[l], lbs[l:l + 1], rope, t_end)
    out = {k: jnp.stack([grads[l][k] for l in range(DEPTH)]) for k in grads[0]}
    out["hgrn_lb_logits"] = _lower_bounds_bwd(logits, jnp.stack(dlbs))
    out["meta"] = dh[FRONT:ROW0]
    return loss, dh[ROW0:t_end], out


MESH = pl.DeviceIdType.MESH
_ANY = pl.BlockSpec(memory_space=pl.ANY)
PACK_COLS = 1024
PACK_ROWS = 8832
HALF_ROWS = PACK_ROWS // 2
SMALL_ROWS = 64


def _mesh_pos():
    return lax.axis_index("x"), lax.axis_index("y"), lax.axis_index("c")


def _other_chips(x, y):
    return [(1 - x, y), (x, 1 - y), (1 - x, 1 - y)]


def _all_gather_big(mine):
    n, c_ = mine.shape

    def body(src, out, send_sems, recv_sems, lsem):
        x, y, c = _mesh_pos()
        me, sib = (x, y, c), (x, y, 1 - c)
        chips = _other_chips(x, y)

        def slot(px, py, pc):
            return out.at[2 * px + py, pc]

        def copy(k, blk, to, from_src=False):
            return pltpu.make_async_remote_copy(
                src_ref=src if from_src else slot(*blk), dst_ref=slot(*blk),
                send_sem=send_sems.at[k], recv_sem=recv_sems.at[k], device_id=to, device_id_type=MESH)

        local = pltpu.make_async_copy(src, slot(*me), lsem)
        local.start()
        first = [copy(0, me, sib, True)] + [copy(1 + j, me, (*ch, c), True) for j, ch in enumerate(chips)]
        for cp in first:
            cp.start()
        passed = [copy(4 + j, (*ch, c), sib) for j, ch in enumerate(chips)]
        for j, ch in enumerate(chips):
            copy(1 + j, (*ch, c), me).wait_recv()
            passed[j].start()
        copy(0, sib, me).wait_recv()
        for j, ch in enumerate(chips):
            copy(4 + j, (*ch, 1 - c), me).wait_recv()
        for cp in first + passed:
            cp.wait_send()
        local.wait()

    return pl.pallas_call(
        body, name="all_gather_big", in_specs=[_ANY], out_specs=_ANY,
        out_shape=jax.ShapeDtypeStruct((4, 2, n, c_), mine.dtype),
        scratch_shapes=[pltpu.SemaphoreType.DMA((7,)), pltpu.SemaphoreType.DMA((7,)), pltpu.SemaphoreType.DMA],
    )(mine)


def _sibling_swap(g):
    _, _, n, c_ = g.shape

    def body(src, out, send_sems, recv_sems):
        x, y, c = _mesh_pos()
        cps = [pltpu.make_async_remote_copy(src_ref=src.at[s, 1 - c], dst_ref=out.at[s], send_sem=send_sems.at[s],
                                            recv_sem=recv_sems.at[s], device_id=(x, y, 1 - c), device_id_type=MESH)
               for s in range(4)]
        for cp in cps:
            cp.start()
        for cp in cps:
            cp.wait()

    return pl.pallas_call(
        body, name="sibling_swap", in_specs=[_ANY], out_specs=_ANY,
        out_shape=jax.ShapeDtypeStruct((4, n, c_), g.dtype),
        scratch_shapes=[pltpu.SemaphoreType.DMA((4,)), pltpu.SemaphoreType.DMA((4,))],
    )(g)


def _chip_scatter(g):
    _, n, c_ = g.shape

    def body(src, out, send_sems, recv_sems, lsem):
        x, y, c = _mesh_pos()
        me = 2 * x + y
        chips = _other_chips(x, y)
        local = pltpu.make_async_copy(src.at[me], out.at[me], lsem)
        local.start()
        cps = [pltpu.make_async_remote_copy(src_ref=src.at[2 * px + py], dst_ref=out.at[me], send_sem=send_sems.at[j],
                                            recv_sem=recv_sems.at[j], device_id=(px, py, c), device_id_type=MESH)
               for j, (px, py) in enumerate(chips)]
        for cp in cps:
            cp.start()
        for j, (px, py) in enumerate(chips):
            pltpu.make_async_remote_copy(src_ref=src.at[me], dst_ref=out.at[2 * px + py], send_sem=send_sems.at[j],
                                         recv_sem=recv_sems.at[j], device_id=(x, y, c),
                                         device_id_type=MESH).wait_recv()
        for cp in cps:
            cp.wait_send()
        local.wait()

    return pl.pallas_call(
        body, name="chip_scatter", in_specs=[_ANY], out_specs=_ANY,
        out_shape=jax.ShapeDtypeStruct((4, n, c_), g.dtype),
        scratch_shapes=[pltpu.SemaphoreType.DMA((3,)), pltpu.SemaphoreType.DMA((3,)), pltpu.SemaphoreType.DMA],
    )(g)


def _sibling_gather(red):
    n, c_ = red.shape

    def body(src, out, send_sem, recv_sem, lsem):
        x, y, c = _mesh_pos()
        local = pltpu.make_async_copy(src, out.at[c], lsem)
        local.start()
        cp = pltpu.make_async_remote_copy(src_ref=src, dst_ref=out.at[c], send_sem=send_sem, recv_sem=recv_sem,
                                          device_id=(x, y, 1 - c), device_id_type=MESH)
        cp.start()
        pltpu.make_async_remote_copy(src_ref=src, dst_ref=out.at[1 - c], send_sem=send_sem, recv_sem=recv_sem,
                                     device_id=(x, y, c), device_id_type=MESH).wait_recv()
        cp.wait_send()
        local.wait()

    return pl.pallas_call(
        body, name="sibling_gather", in_specs=[_ANY], out_specs=_ANY,
        out_shape=jax.ShapeDtypeStruct((2, n, c_), red.dtype),
        scratch_shapes=[pltpu.SemaphoreType.DMA, pltpu.SemaphoreType.DMA, pltpu.SemaphoreType.DMA],
    )(red)


def _all_reduce_small(v, name):
    rows, cols = v.shape

    def body(v_ref, o_ref, slots, send_sems, recv_sems):
        x, y, c = _mesh_pos()
        me = 4 * x + 2 * y + c
        slots[me] = v_ref[...]
        peers = []
        for rel in range(1, 8):
            fx, fy, fc = (rel >> 2) & 1, (rel >> 1) & 1, rel & 1
            px = 1 - x if fx else x
            py = 1 - y if fy else y
            pc = 1 - c if fc else c
            peers.append((px, py, pc))
        cps = [pltpu.make_async_remote_copy(src_ref=v_ref, dst_ref=slots.at[me], send_sem=send_sems.at[k],
                                            recv_sem=recv_sems.at[k], device_id=peer, device_id_type=MESH)
               for k, peer in enumerate(peers)]
        for cp in cps:
            cp.start()
        for k, (px, py, pc) in enumerate(peers):
            pltpu.make_async_remote_copy(src_ref=v_ref, dst_ref=slots.at[4 * px + 2 * py + pc],
                                         send_sem=send_sems.at[k], recv_sem=recv_sems.at[k], device_id=(x, y, c),
                                         device_id_type=MESH).wait_recv()
        for cp in cps:
            cp.wait_send()
        acc = slots[0]
        for d in range(1, 8):
            acc = acc + slots[d]
        o_ref[...] = acc

    vm = pl.BlockSpec(memory_space=pltpu.VMEM)
    return pl.pallas_call(
        body, name=name, in_specs=[vm], out_specs=vm,
        out_shape=jax.ShapeDtypeStruct((rows, cols), F32),
        scratch_shapes=[pltpu.VMEM((8, rows, cols), F32), pltpu.SemaphoreType.DMA((7,)),
                        pltpu.SemaphoreType.DMA((7,))],
    )(v)


PACK_TILE = 1104


def _add_half(g, recv, c_arr):
    _, _, n, c_ = g.shape
    tr = PACK_TILE

    def body(c_ref, g_ref, r_ref, o_ref):
        o_ref[...] = (g_ref[...] + r_ref[...]).astype(o_ref.dtype)

    return pl.pallas_call(
        body, name="add_half",
        grid_spec=pltpu.PrefetchScalarGridSpec(
            num_scalar_prefetch=1, grid=(4, n // tr),
            in_specs=[pl.BlockSpec((None, None, tr, c_), lambda s, i, c: (s, c[0], i, 0)),
                      pl.BlockSpec((None, tr, c_), lambda s, i, c: (s, i, 0))],
            out_specs=pl.BlockSpec((None, tr, c_), lambda s, i, c: (s, i, 0))),
        out_shape=jax.ShapeDtypeStruct((4, n, c_), jnp.bfloat16),
        compiler_params=_cp("parallel", "parallel"),
    )(c_arr, g, recv)


def _sum_chips(parts):
    _, n, c_ = parts.shape
    tr = PACK_TILE

    def body(p_ref, o_ref):
        acc = p_ref[0].astype(F32)
        for s in range(1, 4):
            acc = acc + p_ref[s].astype(F32)
        o_ref[...] = acc

    return pl.pallas_call(
        body, name="sum_chips", grid=(n // tr,),
        in_specs=[pl.BlockSpec((4, tr, c_), lambda i: (0, i, 0))],
        out_specs=pl.BlockSpec((tr, c_), lambda i: (i, 0)),
        out_shape=jax.ShapeDtypeStruct((n, c_), F32),
        compiler_params=_cp("parallel"),
    )(parts)


BIG_SHAPES = {"w_in": ((1024, 6560), 1), "w_conv_out": ((512, 1024), 1), "w_uq": ((256, 768), 1),
              "w_ukv": ((128, 1024), 1), "w_attn_out": ((512, 1024), 1), "w_hgrn_out": ((512, 1024), 1),
              "w_out": ((1024, 1024), 0), "w_ff1": ((1024, 4096), 1), "w_ff2": ((4096, 1024), 0)}
SMALL_SIZES = {"norm1_g": 1024, "conv_b": 512, "conv_ln_g": 512, "conv_ln_b": 512, "q_a_norm_g": 256,
               "kv_a_norm_g": 128, "q_norm_g": 96, "k_norm_g": 96, "hgrn_lb_logits": 512, "hgrn_norm_g": 512,
               "norm2_g": 1024}
PACK_LEN = PACK_ROWS * PACK_COLS
SMALL_LEN = SMALL_ROWS * PACK_COLS


def _pack_local(arrs):
    flat = jnp.concatenate([arrs[k].reshape(-1) for k in BIG])
    return jnp.pad(flat, (0, PACK_LEN - flat.shape[0])).reshape(PACK_ROWS, PACK_COLS)


def _unpack_local(buf):
    flat = buf.reshape(-1)
    out, off = {}, 0
    for k in BIG:
        (r, c_), ax = BIG_SHAPES[k]
        shp = (DEPTH, r, c_ // 4) if ax == 1 else (DEPTH, r // 4, c_)
        n = DEPTH * r * c_ // 4
        out[k] = flat[off:off + n].reshape(shp)
        off += n
    return out


def _unpack_gathered(buf):
    flat = buf.reshape(4, -1)
    out, off = {}, 0
    for k in BIG:
        (r, c_), ax = BIG_SHAPES[k]
        n = DEPTH * r * c_ // 4
        seg = flat[:, off:off + n]
        if ax == 1:
            out[k] = seg.reshape(4, DEPTH, r, c_ // 4).transpose(1, 2, 0, 3).reshape(DEPTH, r, c_)
        else:
            out[k] = seg.reshape(4, DEPTH, r // 4, c_).transpose(1, 0, 2, 3).reshape(DEPTH, r, c_)
        off += n
    return out


def _pack_full_grads(g):
    segs = []
    for k in BIG:
        (r, c_), ax = BIG_SHAPES[k]
        if ax == 1:
            segs.append(g[k].reshape(DEPTH, r, 4, c_ // 4).transpose(2, 0, 1, 3).reshape(4, -1))
        else:
            segs.append(g[k].reshape(DEPTH, 4, r // 4, c_).transpose(1, 0, 2, 3).reshape(4, -1))
    flat = jnp.concatenate(segs, axis=1)
    flat = jnp.pad(flat, ((0, 0), (0, PACK_LEN - flat.shape[1])))
    return flat.reshape(4, 2, HALF_ROWS, PACK_COLS)


def _pack_small(vals, meta_full, conv_w_full):
    flat = jnp.concatenate([vals[k].reshape(-1) for k in SMALL] + [meta_full.reshape(-1), conv_w_full.reshape(-1)])
    return jnp.pad(flat, (0, SMALL_LEN - flat.shape[0])).reshape(SMALL_ROWS, PACK_COLS)


def _unpack_small(buf):
    flat = buf.reshape(-1)
    out, off = {}, 0
    for k in SMALL:
        n = DEPTH * SMALL_SIZES[k]
        out[k] = flat[off:off + n].reshape(DEPTH, SMALL_SIZES[k])
        off += n
    meta = flat[off:off + N_META * D].reshape(N_META, D)
    off += N_META * D
    conv_w = flat[off:off + DEPTH * CONV_K * CONV_DIM].reshape(DEPTH, CONV_K, CONV_DIM)
    return out, meta, conv_w


def kernel(x, meta, norm1_g, w_in, conv_w, conv_b, conv_ln_g, conv_ln_b, w_conv_out, q_a_norm_g, w_uq, kv_a_norm_g, w_ukv, q_norm_g, k_norm_g, w_attn_out, hgrn_lb_logits, hgrn_norm_g, w_hgrn_out, w_out, norm2_g, w_ff1, w_ff2, loss_target, m_meta, m_norm1_g, m_w_in, m_conv_w, m_conv_b, m_conv_ln_g, m_conv_ln_b, m_w_conv_out, m_q_a_norm_g, m_w_uq, m_kv_a_norm_g, m_w_ukv, m_q_norm_g, m_k_norm_g, m_w_attn_out, m_hgrn_lb_logits, m_hgrn_norm_g, m_w_hgrn_out, m_w_out, m_norm2_g, m_w_ff1, m_w_ff2, v_meta, v_norm1_g, v_w_in, v_conv_w, v_conv_b, v_conv_ln_g, v_conv_ln_b, v_w_conv_out, v_q_a_norm_g, v_w_uq, v_kv_a_norm_g, v_w_ukv, v_q_norm_g, v_k_norm_g, v_w_attn_out, v_hgrn_lb_logits, v_hgrn_norm_g, v_w_hgrn_out, v_w_out, v_norm2_g, v_w_ff1, v_w_ff2):
    names = ("meta", "norm1_g", "w_in", "conv_w", "conv_b", "conv_ln_g", "conv_ln_b", "w_conv_out", "q_a_norm_g",
             "w_uq", "kv_a_norm_g", "w_ukv", "q_norm_g", "k_norm_g", "w_attn_out", "hgrn_lb_logits", "hgrn_norm_g",
             "w_hgrn_out", "w_out", "norm2_g", "w_ff1", "w_ff2")
    w = dict(zip(names, (meta, norm1_g, w_in, conv_w, conv_b, conv_ln_g, conv_ln_b, w_conv_out, q_a_norm_g, w_uq,
                         kv_a_norm_g, w_ukv, q_norm_g, k_norm_g, w_attn_out, hgrn_lb_logits, hgrn_norm_g, w_hgrn_out,
                         w_out, norm2_g, w_ff1, w_ff2)))
    m = dict(zip(names, (m_meta, m_norm1_g, m_w_in, m_conv_w, m_conv_b, m_conv_ln_g, m_conv_ln_b, m_w_conv_out,
                         m_q_a_norm_g, m_w_uq, m_kv_a_norm_g, m_w_ukv, m_q_norm_g, m_k_norm_g, m_w_attn_out,
                         m_hgrn_lb_logits, m_hgrn_norm_g, m_w_hgrn_out, m_w_out, m_norm2_g, m_w_ff1, m_w_ff2)))
    v = dict(zip(names, (v_meta, v_norm1_g, v_w_in, v_conv_w, v_conv_b, v_conv_ln_g, v_conv_ln_b, v_w_conv_out,
                         v_q_a_norm_g, v_w_uq, v_kv_a_norm_g, v_w_ukv, v_q_norm_g, v_k_norm_g, v_w_attn_out,
                         v_hgrn_lb_logits, v_hgrn_norm_g, v_w_hgrn_out, v_w_out, v_norm2_g, v_w_ff1, v_w_ff2)))
    cx, cy, cc = _mesh_pos()
    chip = 2 * cx + cy
    zero = jnp.zeros((), jnp.int32)

    w_pack = _pack_local(w)
    mine = lax.dynamic_slice(w_pack.astype(_MM), (cc * HALF_ROWS, zero), (HALF_ROWS, PACK_COLS))
    gathered = _all_gather_big(mine).reshape(4, PACK_ROWS, PACK_COLS)
    full = _unpack_gathered(gathered)
    meta_slab = lax.dynamic_update_slice(jnp.zeros((N_META, D), F32), meta, (zero, chip * (D // 4)))
    convw_slab = lax.dynamic_update_slice(jnp.zeros((DEPTH, CONV_K, CONV_DIM), F32), conv_w,
                                          (zero, zero, chip * (CONV_DIM // 4)))
    zsmall = {k: jnp.zeros((DEPTH, SMALL_SIZES[k]), F32) for k in SMALL}
    south = (cc == 0).astype(F32)
    _, meta_full, convw_full = _unpack_small(
        _all_reduce_small(_pack_small(zsmall, meta_slab, convw_slab) * south, "gather_small"))
    for k in SMALL:
        full[k] = w[k]
    full["meta"] = meta_full
    full["conv_w"] = convw_full

    loss_share, grad_x, g = _device_step(x[0], loss_target[0], full)
    loss = lax.psum(loss_share, ("x", "y", "c"))

    gpack = _pack_full_grads(g)
    from_sibling = _sibling_swap(gpack)
    chip_sum = _add_half(gpack, from_sibling, jnp.reshape(cc, (1,)).astype(jnp.int32))
    red = _sum_chips(_chip_scatter(chip_sum))
    g_big = _unpack_local(_sibling_gather(red).reshape(PACK_ROWS, PACK_COLS))
    g_small, g_meta_full, g_convw_full = _unpack_small(
        _all_reduce_small(_pack_small(g, g["meta"], g["conv_w"]), "reduce_small"))
    grads = dict(g_big)
    grads.update(g_small)
    grads["meta"] = lax.dynamic_slice(g_meta_full, (zero, chip * (D // 4)), (N_META, D // 4))
    grads["conv_w"] = lax.dynamic_slice(g_convw_full, (zero, zero, chip * (CONV_DIM // 4)),
                                        (DEPTH, CONV_K, CONV_DIM // 4))

    d_big, m_big, v_big = [_unpack_local(a) for a in _adamw(w_pack, _pack_local(grads), _pack_local(m),
                                                             _pack_local(v))]

    def small_pack(src):
        return _pack_small(src, jnp.pad(src["meta"], ((0, 0), (0, D - D // 4))),
                           jnp.pad(src["conv_w"], ((0, 0), (0, 0), (0, CONV_DIM - CONV_DIM // 4))))

    def small_unpack(buf):
        out, meta_p, convw_p = _unpack_small(buf)
        out["meta"] = meta_p[:, :D // 4]
        out["conv_w"] = convw_p[:, :, :CONV_DIM // 4]
        return out

    d_small, m_small, v_small = [small_unpack(a) for a in _adamw(small_pack(w), small_pack(grads), small_pack(m),
                                                                  small_pack(v))]
    delta, new_m, new_v = dict(d_big), dict(m_big), dict(v_big)
    delta.update(d_small)
    new_m.update(m_small)
    new_v.update(v_small)
    return (loss, grad_x[None], *[grads[k] for k in names], *[delta[k] for k in names],
            *[new_m[k] for k in names], *[new_v[k] for k in names])
```

```python
import functools

import jax
import jax.numpy as jnp
from jax import lax
from jax.experimental import pallas as pl
from jax.experimental.pallas import tpu as pltpu

F32 = jnp.float32
_MM = jnp.bfloat16

D = 1024
N_META = 16
FRONT = 48
ROW0 = FRONT + N_META
EPS = 1e-6
GATE_CLAMP = 1.0 - 1e-6
CONV_K = 31
CONV_DIM = 512
NH = 8
QK_DIM = 96
ATT_SCALE = QK_DIM ** -0.5
HH = 4
CHUNK = 64
SUB = 16
EXP_CLIP = 60.0
NEG = -1e30
LANE = 128

SEG_GATES = (0, 3072)
SEG_AG = (3072, 4096)
SEG_H4 = (4096, 6144)
SEG_CQ = (6144, 6400)
SEG_CKV = (6400, 6528)
SEG_KR = (6528, 6656)
N_IN_P = 6656

ADAM_LR = 0.001
ADAM_B1 = 0.9
ADAM_B2 = 0.999
ADAM_EPS = 1e-08
ADAM_WD = 0.01
ADAM_STEP = 10

VMEM_LIMIT = 56 * 1024 * 1024


def _tile(n, pref):
    best = 64
    for t in range(64, pref + 1, 64):
        if n % t == 0:
            best = t
    return best


def _cp(*sem):
    return pltpu.CompilerParams(dimension_semantics=tuple(sem), vmem_limit_bytes=VMEM_LIMIT)


def _row(tm, n, col=0):
    return pl.BlockSpec((tm, n), lambda i: (i, col))


def _full(shape):
    return pl.BlockSpec(shape, lambda i: (0,) * len(shape))


def _mm(a, b):
    return jnp.dot(a.astype(_MM), b.astype(_MM), preferred_element_type=F32)


def _mm_nt(a, b):
    return lax.dot_general(a.astype(_MM), b.astype(_MM), (((1,), (1,)), ((), ())), preferred_element_type=F32)


def _mm_tn(a, b):
    return lax.dot_general(a.astype(_MM), b.astype(_MM), (((0,), (0,)), ((), ())), preferred_element_type=F32)


def _split3(x):
    hi = x.astype(jnp.bfloat16)
    return hi, (x - hi.astype(F32)).astype(jnp.bfloat16)


def _dot3(a, b, dims):
    ah, al = _split3(a)
    bh, bl = _split3(b)
    dg = lambda u, v: lax.dot_general(u, v, (dims, ((), ())), preferred_element_type=F32)
    return dg(ah, bh) + (dg(ah, bl) + dg(al, bh))


def _hmm(a, b):
    return _dot3(a, b, ((1,), (0,)))


def _hmm_nt(a, b):
    return _dot3(a, b, ((1,), (1,)))


def _hmm_tn(a, b):
    return _dot3(a, b, ((0,), (0,)))


def _sigmoid(x):
    return 1.0 / (1.0 + jnp.exp(-x))


def _rstd(x, n=None):
    n = x.shape[-1] if n is None else n
    return lax.rsqrt(jnp.sum(x * x, axis=-1, keepdims=True) * (1.0 / n) + EPS)


def _rms_bwd(dy, x, rstd, g, n=None):
    n = x.shape[-1] if n is None else n
    xh = x * rstd
    dxh = dy * g
    dx = rstd * (dxh - xh * (jnp.sum(dxh * xh, axis=-1, keepdims=True) * (1.0 / n)))
    return dx, dy * xh


def _valid_rows(i, tm, t_valid_end):
    r = i * tm + lax.broadcasted_iota(jnp.int32, (tm, 1), 0)
    return ((r >= FRONT) & (r < t_valid_end)).astype(F32)


def _colsum8(x):
    n, c = x.shape
    return jnp.sum(x.reshape(n // 8, 8, c), axis=0)


def _in_proj_fwd(x, g1, w):
    t = x.shape[0]
    tm = _tile(t, 192)
    segs = (SEG_GATES, SEG_AG, SEG_H4, SEG_CQ, SEG_CKV, SEG_KR)

    def body(x_ref, g_ref, w_ref, gates_ref, ag_ref, h4_ref, cq_ref, ckv_ref, kr_ref, hb_ref):
        xv = x_ref[...]
        hb = (xv * _rstd(xv) * g_ref[...]).astype(_MM)
        hb_ref[...] = hb
        for ref, (a, b) in zip((gates_ref, ag_ref, h4_ref, cq_ref, ckv_ref, kr_ref), segs):
            ref[...] = jnp.dot(hb, w_ref[:, a:b], preferred_element_type=F32)

    outs = [jax.ShapeDtypeStruct((t, b - a), F32) for a, b in segs] + [jax.ShapeDtypeStruct((t, D), _MM)]
    return pl.pallas_call(
        body, name="in_proj_fwd", grid=(t // tm,),
        in_specs=[_row(tm, D), _full((1, D)), _full((D, N_IN_P))],
        out_specs=[_row(tm, b - a) for a, b in segs] + [_row(tm, D)],
        out_shape=outs, compiler_params=_cp("parallel"),
    )(x, g1, w)


def _in_proj_bwd(du, x, dx1, g1, wt, t_end):
    t = x.shape[0]
    tm = _tile(t, 192)

    def body(du_ref, x_ref, dx1_ref, g_ref, wt_ref, dx_ref, dg_ref):
        i = pl.program_id(0)
        dh = jnp.dot(du_ref[...], wt_ref[...], preferred_element_type=F32)
        xv = x_ref[...]
        dxn, dgrow = _rms_bwd(dh, xv, _rstd(xv), g_ref[...])
        dx_ref[...] = _valid_rows(i, tm, t_end) * (dx1_ref[...] + dxn)

        @pl.when(i == 0)
        def _():
            dg_ref[...] = jnp.zeros_like(dg_ref)
        dg_ref[...] += _colsum8(dgrow)

    return pl.pallas_call(
        body, name="in_proj_bwd", grid=(t // tm,),
        in_specs=[_row(tm, N_IN_P), _row(tm, D), _row(tm, D), _full((1, D)), _full((N_IN_P, D))],
        out_specs=[_row(tm, D), _full((8, D))],
        out_shape=[jax.ShapeDtypeStruct((t, D), F32), jax.ShapeDtypeStruct((8, D), F32)],
        compiler_params=_cp("arbitrary"),
    )(du, x, dx1, g1, wt)


CONV_CH = 128


def _conv_fwd(ag, cw, cb):
    t = ag.shape[0]
    n = t // CONV_CH

    def body(a_ref, g_ref, w_ref, b_ref, z_ref, hp):
        hp[0:32, :] = jnp.zeros((32, LANE), F32)

        def fill(i, c):
            r = pl.multiple_of(i * CONV_CH, CONV_CH)
            hp[pl.ds(32 + r, CONV_CH), :] = a_ref[pl.ds(r, CONV_CH), :] * _sigmoid(g_ref[pl.ds(r, CONV_CH), :])
            return c
        lax.fori_loop(0, n, fill, 0)

        def conv(i, c):
            r = pl.multiple_of(i * CONV_CH, CONV_CH)
            acc = jnp.broadcast_to(b_ref[...], (CONV_CH, LANE))
            for k in range(CONV_K):
                acc = acc + w_ref[k:k + 1, :] * hp[pl.ds(r + (k + 2), CONV_CH), :]
            z_ref[pl.ds(r, CONV_CH), :] = acc
            return c
        lax.fori_loop(0, n, conv, 0)

    nb = CONV_DIM // LANE
    return pl.pallas_call(
        body, name="conv_fwd", grid=(nb,),
        in_specs=[pl.BlockSpec((t, LANE), lambda j: (0, j)), pl.BlockSpec((t, LANE), lambda j: (0, nb + j)),
                  pl.BlockSpec((32, LANE), lambda j: (0, j)), pl.BlockSpec((1, LANE), lambda j: (0, j))],
        out_specs=pl.BlockSpec((t, LANE), lambda j: (0, j)),
        out_shape=jax.ShapeDtypeStruct((t, CONV_DIM), F32),
        scratch_shapes=[pltpu.VMEM((t + 32, LANE), F32)],
        compiler_params=_cp("parallel"),
    )(ag, ag, cw, cb)


def _conv_bwd(ag, cw, dz):
    t = ag.shape[0]
    n = t // CONV_CH

    def body(a_ref, g_ref, w_ref, dz_ref, da_ref, dg_ref, dcw_ref, hp, dzp, accw):
        hp[0:32, :] = jnp.zeros((32, LANE), F32)
        dzp[pl.ds(t, 32), :] = jnp.zeros((32, LANE), F32)
        accw[...] = jnp.zeros_like(accw)

        def fill(i, c):
            r = pl.multiple_of(i * CONV_CH, CONV_CH)
            hp[pl.ds(32 + r, CONV_CH), :] = a_ref[pl.ds(r, CONV_CH), :] * _sigmoid(g_ref[pl.ds(r, CONV_CH), :])
            dzp[pl.ds(r, CONV_CH), :] = dz_ref[pl.ds(r, CONV_CH), :]
            return c
        lax.fori_loop(0, n, fill, 0)

        def step(i, c):
            r = pl.multiple_of(i * CONV_CH, CONV_CH)
            dzc = dz_ref[pl.ds(r, CONV_CH), :]
            dh = jnp.zeros((CONV_CH, LANE), F32)
            for k in range(CONV_K):
                dh = dh + w_ref[k:k + 1, :] * dzp[pl.ds(r + (CONV_K - 1 - k), CONV_CH), :]
                accw[8 * k:8 * k + 8, :] += _colsum8(dzc * hp[pl.ds(r + (k + 2), CONV_CH), :])
            a = a_ref[pl.ds(r, CONV_CH), :]
            sg = _sigmoid(g_ref[pl.ds(r, CONV_CH), :])
            da_ref[pl.ds(r, CONV_CH), :] = dh * sg
            dg_ref[pl.ds(r, CONV_CH), :] = dh * a * sg * (1.0 - sg)
            return c
        lax.fori_loop(0, n, step, 0)

        for k in range(CONV_K):
            dcw_ref[k:k + 1, :] = jnp.sum(accw[8 * k:8 * k + 8, :], axis=0, keepdims=True)
        dcw_ref[CONV_K:32, :] = jnp.zeros((32 - CONV_K, LANE), F32)

    nb = CONV_DIM // LANE
    colspec = pl.BlockSpec((t, LANE), lambda j: (0, j))
    return pl.pallas_call(
        body, name="conv_bwd", grid=(nb,),
        in_specs=[colspec, pl.BlockSpec((t, LANE), lambda j: (0, nb + j)),
                  pl.BlockSpec((32, LANE), lambda j: (0, j)), colspec],
        out_specs=[colspec, colspec, pl.BlockSpec((32, LANE), lambda j: (0, j))],
        out_shape=[jax.ShapeDtypeStruct((t, CONV_DIM), F32), jax.ShapeDtypeStruct((t, CONV_DIM), F32),
                   jax.ShapeDtypeStruct((32, CONV_DIM), F32)],
        scratch_shapes=[pltpu.VMEM((t + 32, LANE), F32), pltpu.VMEM((t + 32, LANE), F32),
                        pltpu.VMEM((8 * 32, LANE), F32)],
        compiler_params=_cp("parallel"),
    )(ag, ag, cw, dz)


def _rope(x, c, s1, s2):
    return x * c + pltpu.roll(x, LANE - 16, 1) * s1 + pltpu.roll(x, 16, 1) * s2


def _rope_t(dy, c, s1, s2):
    return dy * c + pltpu.roll(dy * s1, 16, 1) + pltpu.roll(dy * s2, LANE - 16, 1)


def _mla_pre_fwd(cq, ckv, kr, qag, wuq, kvag, wk, wv, qng, kng, rc, rs1, rs2):
    t = cq.shape[0]
    tm = _tile(t, 384)

    def body(cq_ref, ckv_ref, kr_ref, qag_ref, wuq_ref, kvag_ref, wk_ref, wv_ref, qng_ref, kng_ref,
             c_ref, s1_ref, s2_ref, q_ref, k_ref, v_ref, cqn_ref, ckvn_ref):
        cqv = cq_ref[...]
        cqn = (cqv * _rstd(cqv) * qag_ref[...]).astype(_MM)
        cqn_ref[...] = cqn
        ckvv = ckv_ref[...]
        ckvn = (ckvv * _rstd(ckvv) * kvag_ref[...]).astype(_MM)
        ckvn_ref[...] = ckvn
        qraw = jnp.dot(cqn, wuq_ref[...], preferred_element_type=F32)
        kraw = jnp.dot(ckvn, wk_ref[...], preferred_element_type=F32)
        v_ref[...] = jnp.dot(ckvn, wv_ref[...], preferred_element_type=F32).astype(_MM)
        krv = kr_ref[...]
        c, s1, s2 = c_ref[...], s1_ref[...], s2_ref[...]
        for h in range(NH):
            sl = slice(LANE * h, LANE * (h + 1))
            qh = qraw[:, sl]
            qn = qh * _rstd(qh, QK_DIM) * qng_ref[...]
            q_ref[:, sl] = (_rope(qn, c, s1, s2) * ATT_SCALE).astype(_MM)
            kh = kraw[:, sl] + krv
            kn = kh * _rstd(kh, QK_DIM) * kng_ref[...]
            k_ref[:, sl] = _rope(kn, c, s1, s2).astype(_MM)

    hd = NH * LANE
    return pl.pallas_call(
        body, name="mla_pre_fwd", grid=(t // tm,),
        in_specs=[_row(tm, 256), _row(tm, 128), _row(tm, 128), _full((1, 256)), _full((256, hd)),
                  _full((1, 128)), _full((128, hd)), _full((128, hd)), _full((1, LANE)), _full((1, LANE)),
                  _row(tm, LANE), _row(tm, LANE), _row(tm, LANE)],
        out_specs=[_row(tm, hd), _row(tm, hd), _row(tm, hd), _row(tm, 256), _row(tm, 128)],
        out_shape=[jax.ShapeDtypeStruct((t, hd), _MM)] * 3 + [jax.ShapeDtypeStruct((t, 256), _MM),
                                                              jax.ShapeDtypeStruct((t, 128), _MM)],
        compiler_params=_cp("parallel"),
    )(cq, ckv, kr, qag, wuq, kvag, wk, wv, qng, kng, rc, rs1, rs2)


def _mla_pre_bwd(dq, dk, dv, cq, ckv, kr, qag, wuq, wuqt, kvag, wk, wkt, wvt, qng, kng, rc, rs1, rs2):
    t = cq.shape[0]
    tm = _tile(t, 192)
    hd = NH * LANE

    def body(dq_ref, dk_ref, dv_ref, cq_ref, ckv_ref, kr_ref, qag_ref, wuq_ref, wuqt_ref, kvag_ref, wk_ref,
             wkt_ref, wvt_ref, qng_ref, kng_ref, c_ref, s1_ref, s2_ref,
             dcq_ref, dckv_ref, dkr_ref, dqraw_ref, dkraw_ref, dqag_ref, dkvag_ref, dqng_ref, dkng_ref):
        i = pl.program_id(0)
        cqv = cq_ref[...]
        rq_in = _rstd(cqv)
        cqn = (cqv * rq_in * qag_ref[...]).astype(_MM)
        ckvv = ckv_ref[...]
        rkv_in = _rstd(ckvv)
        ckvn = (ckvv * rkv_in * kvag_ref[...]).astype(_MM)
        qraw = jnp.dot(cqn, wuq_ref[...], preferred_element_type=F32)
        kraw = jnp.dot(ckvn, wk_ref[...], preferred_element_type=F32)
        krv = kr_ref[...]
        c, s1, s2 = c_ref[...], s1_ref[...], s2_ref[...]
        dkr = jnp.zeros((tm, LANE), F32)
        dqng = jnp.zeros((8, LANE), F32)
        dkng = jnp.zeros((8, LANE), F32)
        for h in range(NH):
            sl = slice(LANE * h, LANE * (h + 1))
            qh = qraw[:, sl]
            dqn = _rope_t(dq_ref[:, sl] * ATT_SCALE, c, s1, s2)
            dqh, gq = _rms_bwd(dqn, qh, _rstd(qh, QK_DIM), qng_ref[...], QK_DIM)
            dqraw_ref[:, sl] = dqh.astype(_MM)
            dqng = dqng + _colsum8(gq)
            kh = kraw[:, sl] + krv
            dkn = _rope_t(dk_ref[:, sl], c, s1, s2)
            dkh, gk = _rms_bwd(dkn, kh, _rstd(kh, QK_DIM), kng_ref[...], QK_DIM)
            dkraw_ref[:, sl] = dkh.astype(_MM)
            dkr = dkr + dkh
            dkng = dkng + _colsum8(gk)
        dkr_ref[...] = dkr.astype(_MM)
        dcqn = jnp.dot(dqraw_ref[...], wuqt_ref[...], preferred_element_type=F32)
        dcq, gqa = _rms_bwd(dcqn, cqv, rq_in, qag_ref[...])
        dcq_ref[...] = dcq.astype(_MM)
        dckvn = (jnp.dot(dkraw_ref[...], wkt_ref[...], preferred_element_type=F32)
                 + jnp.dot(dv_ref[...].astype(_MM), wvt_ref[...], preferred_element_type=F32))
        dckv, gkva = _rms_bwd(dckvn, ckvv, rkv_in, kvag_ref[...])
        dckv_ref[...] = dckv.astype(_MM)

        @pl.when(i == 0)
        def _():
            dqag_ref[...] = jnp.zeros_like(dqag_ref)
            dkvag_ref[...] = jnp.zeros_like(dkvag_ref)
            dqng_ref[...] = jnp.zeros_like(dqng_ref)
            dkng_ref[...] = jnp.zeros_like(dkng_ref)
        dqag_ref[...] += _colsum8(gqa)
        dkvag_ref[...] += _colsum8(gkva)
        dqng_ref[...] += dqng
        dkng_ref[...] += dkng

    return pl.pallas_call(
        body, name="mla_pre_bwd", grid=(t // tm,),
        in_specs=[_row(tm, hd), _row(tm, hd), _row(tm, hd), _row(tm, 256), _row(tm, 128), _row(tm, 128),
                  _full((1, 256)), _full((256, hd)), _full((hd, 256)), _full((1, 128)), _full((128, hd)),
                  _full((hd, 128)), _full((hd, 128)), _full((1, LANE)), _full((1, LANE)),
                  _row(tm, LANE), _row(tm, LANE), _row(tm, LANE)],
        out_specs=[_row(tm, 256), _row(tm, 128), _row(tm, 128), _row(tm, hd), _row(tm, hd),
                   _full((8, 256)), _full((8, 128)), _full((8, LANE)), _full((8, LANE))],
        out_shape=[jax.ShapeDtypeStruct((t, 256), _MM), jax.ShapeDtypeStruct((t, 128), _MM),
                   jax.ShapeDtypeStruct((t, 128), _MM), jax.ShapeDtypeStruct((t, hd), _MM),
                   jax.ShapeDtypeStruct((t, hd), _MM), jax.ShapeDtypeStruct((8, 256), F32),
                   jax.ShapeDtypeStruct((8, 128), F32), jax.ShapeDtypeStruct((8, LANE), F32),
                   jax.ShapeDtypeStruct((8, LANE), F32)],
        compiler_params=_cp("arbitrary"),
    )(dq, dk, dv, cq, ckv, kr, qag, wuq, wuqt, kvag, wk, wkt, wvt, qng, kng, rc, rs1, rs2)


def _attn_mask(r0, c0, tq):
    rows = r0 + lax.broadcasted_iota(jnp.int32, (tq, 1), 0)
    cols = c0 + lax.broadcasted_iota(jnp.int32, (1, tq), 1)
    return (cols <= rows) & (cols >= FRONT)


def _attn_fwd(q, k, v):
    t = q.shape[0]
    tq = _tile(t, 384)
    nq = t // tq

    def body(q_ref, k_ref, v_ref, o_ref, lse_ref):
        def qloop(qi, carry):
            r0 = pl.multiple_of(qi * tq, tq)
            qb = q_ref[pl.ds(r0, tq), :]

            def kloop(kj, st):
                m, l, acc = st
                c0 = pl.multiple_of(kj * tq, tq)
                s = _mm_nt(qb, k_ref[pl.ds(c0, tq), :])
                s = jnp.where(_attn_mask(r0, c0, tq), s, NEG)
                m2 = jnp.maximum(m, jnp.max(s, axis=-1, keepdims=True))
                p = jnp.exp(s - m2)
                a = jnp.exp(m - m2)
                l = a * l + jnp.sum(p, axis=-1, keepdims=True)
                acc = a * acc + _mm(p, v_ref[pl.ds(c0, tq), :])
                return m2, l, acc

            m, l, acc = lax.fori_loop(
                0, qi + 1, kloop,
                (jnp.full((tq, 1), NEG, F32), jnp.zeros((tq, 1), F32), jnp.zeros((tq, LANE), F32)))
            o_ref[pl.ds(r0, tq), :] = acc / l
            lse_ref[pl.ds(r0, tq), :] = m + jnp.log(l)
            return carry
        lax.fori_loop(0, nq, qloop, 0)

    hs = pl.BlockSpec((t, LANE), lambda h: (0, h))
    return pl.pallas_call(
        body, name="attn_fwd", grid=(NH,),
        in_specs=[hs, hs, hs],
        out_specs=[hs, pl.BlockSpec((None, t, 1), lambda h: (h, 0, 0))],
        out_shape=[jax.ShapeDtypeStruct((t, NH * LANE), F32), jax.ShapeDtypeStruct((NH, t, 1), F32)],
        compiler_params=_cp("parallel"),
    )(q, k, v)


def _attn_bwd(q, k, v, o, lse, do):
    t = q.shape[0]
    tq = _tile(t, 384)
    nq = t // tq

    def body(q_ref, k_ref, v_ref, o_ref, lse_ref, do_ref, dq_ref, dk_ref, dv_ref, delta):
        def prep(i, c):
            r0 = pl.multiple_of(i * tq, tq)
            delta[pl.ds(r0, tq), :] = jnp.sum(do_ref[pl.ds(r0, tq), :] * o_ref[pl.ds(r0, tq), :], axis=-1,
                                              keepdims=True)
            dq_ref[pl.ds(r0, tq), :] = jnp.zeros((tq, LANE), F32)
            return c
        lax.fori_loop(0, nq, prep, 0)

        def kloop(kj, carry):
            c0 = pl.multiple_of(kj * tq, tq)
            kb = k_ref[pl.ds(c0, tq), :]
            vb = v_ref[pl.ds(c0, tq), :]

            def qloop(qi, st):
                dkb, dvb = st
                r0 = pl.multiple_of(qi * tq, tq)
                qb = q_ref[pl.ds(r0, tq), :]
                dob = do_ref[pl.ds(r0, tq), :].astype(_MM)
                s = _mm_nt(qb, kb)
                s = jnp.where(_attn_mask(r0, c0, tq), s, NEG)
                p = jnp.exp(s - lse_ref[pl.ds(r0, tq), :])
                dvb = dvb + _mm_tn(p, dob)
                dp = _mm_nt(dob, vb)
                ds = (p * (dp - delta[pl.ds(r0, tq), :])).astype(_MM)
                dkb = dkb + _mm_tn(ds, qb)
                dq_ref[pl.ds(r0, tq), :] += _mm(ds, kb)
                return dkb, dvb

            dkb, dvb = lax.fori_loop(kj, nq, qloop, (jnp.zeros((tq, LANE), F32), jnp.zeros((tq, LANE), F32)))
            dk_ref[pl.ds(c0, tq), :] = dkb
            dv_ref[pl.ds(c0, tq), :] = dvb
            return carry
        lax.fori_loop(0, nq, kloop, 0)

    hs = pl.BlockSpec((t, LANE), lambda h: (0, h))
    return pl.pallas_call(
        body, name="attn_bwd", grid=(NH,),
        in_specs=[hs, hs, hs, hs, pl.BlockSpec((None, t, 1), lambda h: (h, 0, 0)), hs],
        out_specs=[hs, hs, hs],
        out_shape=[jax.ShapeDtypeStruct((t, NH * LANE), F32)] * 3,
        scratch_shapes=[pltpu.VMEM((t, 1), F32)],
        compiler_params=_cp("parallel"),
    )(q, k, v, o, lse, do)


def _cumsum_rows(x):
    n = x.shape[0]
    rows = lax.broadcasted_iota(jnp.int32, (n, 1), 0)
    d = 1
    while d < n:
        x = x + jnp.where(rows >= d, pltpu.roll(x, d, 0), 0.0)
        d *= 2
    return x


def _revcumsum_rows(x):
    n = x.shape[0]
    rows = lax.broadcasted_iota(jnp.int32, (n, 1), 0)
    d = 1
    while d < n:
        x = x + jnp.where(rows < n - d, pltpu.roll(x, n - d, 0), 0.0)
        d *= 2
    return x


def _hgrn_gates(f, lb):
    sneg = _sigmoid(-f)
    kk = (1.0 - lb) * sneg
    lf = jnp.log1p(-jnp.minimum(kk, GATE_CLAMP))
    return kk, lf, sneg


def _silu(x):
    return x * _sigmoid(x)


def _dsilu(x):
    s = _sigmoid(x)
    return s * (1.0 + x * (1.0 - s))


def _hgrn_intra(q, kk, b):
    parts = []
    for blk in range(CHUNK // SUB):
        lo = blk * SUB
        ref = jnp.zeros((1, LANE), F32) if blk == 0 else b[lo - 1:lo, :]
        eq = jnp.exp(b[lo:lo + SUB, :] - ref)
        ek = jnp.exp(jnp.minimum(ref - b, EXP_CLIP))
        parts.append((q[lo:lo + SUB, :] * eq, kk * ek, eq, ek))
    return parts


def _chunk_causal():
    return lax.broadcasted_iota(jnp.int32, (CHUNK, CHUNK), 1) <= lax.broadcasted_iota(jnp.int32, (CHUNK, CHUNK), 0)


def _hgrn_fwd(h4, lb):
    t = h4.shape[0]
    nc = t // CHUNK

    def body(q_ref, f_ref, i_ref, lb_ref, o_ref, s_ref, st):
        st[...] = jnp.zeros_like(st)
        causal = _chunk_causal()

        def chunk(c, carry):
            r0 = pl.multiple_of(c * CHUNK, CHUNK)
            q = q_ref[pl.ds(r0, CHUNK), :]
            kk, lf, _ = _hgrn_gates(f_ref[pl.ds(r0, CHUNK), :], lb_ref[...])
            v = _silu(i_ref[pl.ds(r0, CHUNK), :])
            b = _cumsum_rows(lf)
            s_prev = st[...]
            s_ref[c] = s_prev
            o = _hmm_nt(q * jnp.exp(b), s_prev)
            a = jnp.concatenate([_hmm_nt(qs, ks) for qs, ks, _, _ in _hgrn_intra(q, kk, b)], axis=0)
            a = jnp.where(causal, a, 0.0)
            o_ref[pl.ds(r0, CHUNK), :] = o + _hmm(a, v)
            bl = b[CHUNK - 1:CHUNK, :]
            st[...] = s_prev * jnp.exp(bl) + _hmm_tn(v, kk * jnp.exp(bl - b))
            return carry
        lax.fori_loop(0, nc, chunk, 0)

    def col(j):
        return pl.BlockSpec((t, LANE), lambda h: (0, HH * j + h))
    return pl.pallas_call(
        body, name="hgrn_fwd", grid=(HH,),
        in_specs=[col(0), col(1), col(2), pl.BlockSpec((1, LANE), lambda h: (0, h))],
        out_specs=[pl.BlockSpec((t, LANE), lambda h: (0, h)),
                   pl.BlockSpec((None, nc, LANE, LANE), lambda h: (h, 0, 0, 0))],
        out_shape=[jax.ShapeDtypeStruct((t, HH * LANE), F32), jax.ShapeDtypeStruct((HH, nc, LANE, LANE), F32)],
        scratch_shapes=[pltpu.VMEM((LANE, LANE), F32)],
        compiler_params=_cp("parallel"),
    )(h4, h4, h4, lb)


def _hgrn_bwd(h4, lb, do, states):
    t = h4.shape[0]
    nc = t // CHUNK

    def body(q_ref, f_ref, i_ref, lb_ref, do_ref, s_ref, dq_ref, df_ref, di_ref, dlb_ref, dst, carry):
        dst[...] = jnp.zeros_like(dst)
        carry[...] = jnp.zeros_like(carry)
        dlb_ref[...] = jnp.zeros_like(dlb_ref)
        causal = _chunk_causal()

        def chunk(cc, cr):
            c = nc - 1 - cc
            r0 = pl.multiple_of(c * CHUNK, CHUNK)
            q = q_ref[pl.ds(r0, CHUNK), :]
            lbv = lb_ref[...]
            kk, lf, sneg = _hgrn_gates(f_ref[pl.ds(r0, CHUNK), :], lbv)
            iv = i_ref[pl.ds(r0, CHUNK), :]
            v = _silu(iv)
            b = _cumsum_rows(lf)
            s_prev = s_ref[c]
            ds_new = dst[...]
            dob = do_ref[pl.ds(r0, CHUNK), :]
            e = jnp.exp(b)
            qe = q * e
            bl = b[CHUNK - 1:CHUNK, :]
            etail = jnp.exp(bl - b)
            kd = kk * etail
            dq_inter = _hmm(dob, s_prev) * e
            dv = _hmm_nt(kd, ds_new)
            dkk = _hmm(v, ds_new) * etail
            parts = _hgrn_intra(q, kk, b)
            a = jnp.where(causal, jnp.concatenate([_hmm_nt(qs, ks) for qs, ks, _, _ in parts], axis=0), 0.0)
            da = jnp.where(causal, _hmm_nt(dob, v), 0.0)
            dv = dv + _hmm_tn(a, dob)
            dq_rows = []
            for blk, (qs, ks, eq, ek) in enumerate(parts):
                da_blk = da[blk * SUB:(blk + 1) * SUB, :]
                dq_rows.append(_hmm(da_blk, ks) * eq)
                dkk = dkk + _hmm_tn(da_blk, qs) * ek
            dq = dq_inter + jnp.concatenate(dq_rows, axis=0)
            dst[...] = ds_new * jnp.exp(bl) + _hmm_tn(dob, qe)
            g = q * dq - kk * dkk
            dlf = _revcumsum_rows(g) + carry[0:1, :]
            carry[0:1, :] += jnp.sum(g, axis=0, keepdims=True)
            dkk_tot = dkk + dlf * jnp.where(kk < GATE_CLAMP, -1.0 / (1.0 - kk), 0.0)
            dq_ref[pl.ds(r0, CHUNK), :] = dq
            df_ref[pl.ds(r0, CHUNK), :] = dkk_tot * (1.0 - lbv) * (-sneg * (1.0 - sneg))
            di_ref[pl.ds(r0, CHUNK), :] = dv * _dsilu(iv)
            dlb_ref[...] += _colsum8(dkk_tot * (-sneg))
            return cr
        lax.fori_loop(0, nc, chunk, 0)

    def col(j):
        return pl.BlockSpec((t, LANE), lambda h: (0, HH * j + h))
    hs = pl.BlockSpec((t, LANE), lambda h: (0, h))
    return pl.pallas_call(
        body, name="hgrn_bwd", grid=(HH,),
        in_specs=[col(0), col(1), col(2), pl.BlockSpec((1, LANE), lambda h: (0, h)), hs,
                  pl.BlockSpec((None, nc, LANE, LANE), lambda h: (h, 0, 0, 0))],
        out_specs=[hs, hs, hs, pl.BlockSpec((8, LANE), lambda h: (0, h))],
        out_shape=[jax.ShapeDtypeStruct((t, HH * LANE), F32)] * 3 + [jax.ShapeDtypeStruct((8, HH * LANE), F32)],
        scratch_shapes=[pltpu.VMEM((LANE, LANE), F32), pltpu.VMEM((8, LANE), F32)],
        compiler_params=_cp("parallel"),
    )(h4, h4, h4, lb, do, states)


def _ln_fwd(z, g, b):
    mu = jnp.mean(z, axis=-1, keepdims=True)
    zc = z - mu
    rstd = lax.rsqrt(jnp.mean(zc * zc, axis=-1, keepdims=True) + EPS)
    zh = zc * rstd
    return zh * g + b, zh, rstd


def _mix_fwd(x, z, o_att, o_h, h4, gates, lng, lnb, wco, wao, ng, who, wout, t_end):
    t = x.shape[0]
    tm = _tile(t, 192)

    def body(x_ref, z_ref, oa_ref, oh_ref, hg_ref, gt_ref, lng_ref, lnb_ref, wco_ref, wao_ref, ng_ref, who_ref,
             wout_ref, x1_ref, mix_ref, ca_ref, oc_ref, ya_ref, yb_ref, yc_ref):
        i = pl.program_id(0)
        ln, _, _ = _ln_fwd(z_ref[...], lng_ref[...], lnb_ref[...])
        ca = _silu(ln).astype(_MM)
        ca_ref[...] = ca
        ya = jnp.dot(ca, wco_ref[...], preferred_element_type=F32)
        yb = _mm(oa_ref[...], wao_ref[...])
        hg = hg_ref[...]
        for h in range(HH):
            sl = slice(LANE * h, LANE * (h + 1))
            oh = oh_ref[:, sl]
            oc_ref[:, sl] = (oh * _rstd(oh) * ng_ref[:, sl] * _silu(hg[:, sl])).astype(_MM)
        yc = jnp.dot(oc_ref[...], who_ref[...], preferred_element_type=F32)
        ya_ref[...] = ya
        yb_ref[...] = yb
        yc_ref[...] = yc
        mix = (_sigmoid(gt_ref[:, 0:D]) * ya + _sigmoid(gt_ref[:, D:2 * D]) * yb
               + _sigmoid(gt_ref[:, 2 * D:3 * D]) * yc).astype(_MM)
        mix_ref[...] = mix
        x1_ref[...] = x_ref[...] + _valid_rows(i, tm, t_end) * jnp.dot(mix, wout_ref[...],
                                                                       preferred_element_type=F32)

    hd = NH * LANE
    return pl.pallas_call(
        body, name="mix_fwd", grid=(t // tm,),
        in_specs=[_row(tm, D), _row(tm, CONV_DIM), _row(tm, hd), _row(tm, 512), _row(tm, 512, 3), _row(tm, 3 * D),
                  _full((1, 512)), _full((1, 512)), _full((512, D)), _full((hd, D)), _full((1, 512)),
                  _full((512, D)), _full((D, D))],
        out_specs=[_row(tm, D), _row(tm, D), _row(tm, 512), _row(tm, 512), _row(tm, D), _row(tm, D), _row(tm, D)],
        out_shape=[jax.ShapeDtypeStruct((t, D), F32), jax.ShapeDtypeStruct((t, D), _MM),
                   jax.ShapeDtypeStruct((t, 512), _MM), jax.ShapeDtypeStruct((t, 512), _MM),
                   jax.ShapeDtypeStruct((t, D), F32), jax.ShapeDtypeStruct((t, D), F32),
                   jax.ShapeDtypeStruct((t, D), F32)],
        compiler_params=_cp("parallel"),
    )(x, z, o_att, o_h, h4, gates, lng, lnb, wco, wao, ng, who, wout)


def _mix_bwd(dx1, ya, yb, yc, gates, z, o_h, h4, lng, lnb, ng, woutt, wcot, waot, whot):
    t = dx1.shape[0]
    tm = _tile(t, 192)
    hd = NH * LANE

    def body(dx1_ref, ya_ref, yb_ref, yc_ref, gt_ref, z_ref, oh_ref, hg_ref, lng_ref, lnb_ref, ng_ref,
             woutt_ref, wcot_ref, waot_ref, whot_ref,
             dgt_ref, dya_ref, dyb_ref, dyc_ref, dz_ref, doa_ref, doh_ref, dhg_ref,
             dlng_ref, dlnb_ref, dcb_ref, dng_ref):
        i = pl.program_id(0)
        dmix = _mm(dx1_ref[...], woutt_ref[...])
        dys = []
        for j, y_ref in enumerate((ya_ref, yb_ref, yc_ref)):
            sg = _sigmoid(gt_ref[:, j * D:(j + 1) * D])
            dgt_ref[:, j * D:(j + 1) * D] = (dmix * y_ref[...] * sg * (1.0 - sg)).astype(_MM)
            dys.append((dmix * sg).astype(_MM))
        dya_ref[...], dyb_ref[...], dyc_ref[...] = dys
        dca = jnp.dot(dys[0], wcot_ref[...], preferred_element_type=F32)
        ln, zh, rstd = _ln_fwd(z_ref[...], lng_ref[...], lnb_ref[...])
        dln = dca * _dsilu(ln)
        dzh = dln * lng_ref[...]
        dz = rstd * (dzh - jnp.mean(dzh, axis=-1, keepdims=True)
                     - zh * jnp.mean(dzh * zh, axis=-1, keepdims=True))
        dz_ref[...] = dz
        doa_ref[...] = jnp.dot(dys[1], waot_ref[...], preferred_element_type=F32)
        doc = jnp.dot(dys[2], whot_ref[...], preferred_element_type=F32)
        hg = hg_ref[...]
        dng_rows = []
        for h in range(HH):
            sl = slice(LANE * h, LANE * (h + 1))
            oh = oh_ref[:, sl]
            r = _rstd(oh)
            don = doc[:, sl] * _silu(hg[:, sl])
            dhg_ref[:, sl] = (doc[:, sl] * oh * r * ng_ref[:, sl] * _dsilu(hg[:, sl])).astype(_MM)
            doh, gn = _rms_bwd(don, oh, r, ng_ref[:, sl])
            doh_ref[:, sl] = doh
            dng_rows.append(_colsum8(gn))

        @pl.when(i == 0)
        def _():
            dlng_ref[...] = jnp.zeros_like(dlng_ref)
            dlnb_ref[...] = jnp.zeros_like(dlnb_ref)
            dcb_ref[...] = jnp.zeros_like(dcb_ref)
            dng_ref[...] = jnp.zeros_like(dng_ref)
        dlng_ref[...] += _colsum8(dln * zh)
        dlnb_ref[...] += _colsum8(dln)
        dcb_ref[...] += _colsum8(dz)
        dng_ref[...] += jnp.concatenate(dng_rows, axis=1)

    return pl.pallas_call(
        body, name="mix_bwd", grid=(t // tm,),
        in_specs=[_row(tm, D), _row(tm, D), _row(tm, D), _row(tm, D), _row(tm, 3 * D), _row(tm, 512), _row(tm, 512),
                  _row(tm, 512, 3), _full((1, 512)), _full((1, 512)), _full((1, 512)),
                  _full((D, D)), _full((D, 512)), _full((D, hd)), _full((D, 512))],
        out_specs=[_row(tm, 3 * D), _row(tm, D), _row(tm, D), _row(tm, D), _row(tm, 512), _row(tm, hd),
                   _row(tm, 512), _row(tm, 512), _full((8, 512)), _full((8, 512)), _full((8, 512)), _full((8, 512))],
        out_shape=[jax.ShapeDtypeStruct((t, 3 * D), _MM), jax.ShapeDtypeStruct((t, D), _MM),
                   jax.ShapeDtypeStruct((t, D), _MM), jax.ShapeDtypeStruct((t, D), _MM),
                   jax.ShapeDtypeStruct((t, 512), F32), jax.ShapeDtypeStruct((t, hd), F32),
                   jax.ShapeDtypeStruct((t, 512), F32), jax.ShapeDtypeStruct((t, 512), _MM)]
        + [jax.ShapeDtypeStruct((8, 512), F32)] * 4,
        compiler_params=_cp("arbitrary"),
    )(dx1, ya, yb, yc, gates, z, o_h, h4, lng, lnb, ng, woutt, wcot, waot, whot)


D_FF = 4096


def _ffn_fwd(x1, g2, w1, w2):
    t = x1.shape[0]
    tm = _tile(t, 192)

    def body(x1_ref, g_ref, w1_ref, w2_ref, x2_ref, p_ref):
        xv = x1_ref[...]
        h2 = (xv * _rstd(xv) * g_ref[...]).astype(_MM)
        p = jnp.dot(h2, w1_ref[...], preferred_element_type=F32)
        p_ref[...] = p
        r = jnp.maximum(p, 0.0)
        x2_ref[...] = xv + jnp.dot((r * r).astype(_MM), w2_ref[...], preferred_element_type=F32)

    return pl.pallas_call(
        body, name="ffn_fwd", grid=(t // tm,),
        in_specs=[_row(tm, D), _full((1, D)), _full((D, D_FF)), _full((D_FF, D))],
        out_specs=[_row(tm, D), _row(tm, D_FF)],
        out_shape=[jax.ShapeDtypeStruct((t, D), F32), jax.ShapeDtypeStruct((t, D_FF), F32)],
        compiler_params=_cp("parallel"),
    )(x1, g2, w1, w2)


def _ffn_bwd(dx2, x1, p, g2, w1t, w2t):
    t = x1.shape[0]
    tm = _tile(t, 192)

    def body(dx2_ref, x1_ref, p_ref, g_ref, w1t_ref, w2t_ref, dx1_ref, h2_ref, act_ref, dp_ref, dg_ref):
        i = pl.program_id(0)
        xv = x1_ref[...]
        rstd = _rstd(xv)
        h2_ref[...] = (xv * rstd * g_ref[...]).astype(_MM)
        r = jnp.maximum(p_ref[...], 0.0)
        act_ref[...] = (r * r).astype(_MM)
        dx2 = dx2_ref[...]
        da = _mm(dx2, w2t_ref[...])
        dp = (2.0 * r * da).astype(_MM)
        dp_ref[...] = dp
        dh2 = jnp.dot(dp, w1t_ref[...], preferred_element_type=F32)
        dxn, dgrow = _rms_bwd(dh2, xv, rstd, g_ref[...])
        dx1_ref[...] = dx2 + dxn

        @pl.when(i == 0)
        def _():
            dg_ref[...] = jnp.zeros_like(dg_ref)
        dg_ref[...] += _colsum8(dgrow)

    return pl.pallas_call(
        body, name="ffn_bwd", grid=(t // tm,),
        in_specs=[_row(tm, D), _row(tm, D), _row(tm, D_FF), _full((1, D)), _full((D_FF, D)), _full((D, D_FF))],
        out_specs=[_row(tm, D), _row(tm, D), _row(tm, D_FF), _row(tm, D_FF), _full((8, D))],
        out_shape=[jax.ShapeDtypeStruct((t, D), F32), jax.ShapeDtypeStruct((t, D), _MM),
                   jax.ShapeDtypeStruct((t, D_FF), _MM), jax.ShapeDtypeStruct((t, D_FF), _MM),
                   jax.ShapeDtypeStruct((8, D), F32)],
        compiler_params=_cp("arbitrary"),
    )(dx2, x1, p, g2, w1t, w2t)


def _wgrad(a, b, name, chips=1):
    t, ka = a.shape
    nb = b.shape[1]
    tm = _tile(t, 384)
    cs = nb // chips
    tn = 512 if cs % 512 == 0 else (256 if cs % 256 == 0 else cs)
    per = cs // tn

    def body(a_ref, b_ref, o_ref):
        @pl.when(pl.program_id(1) == 0)
        def _():
            o_ref[...] = jnp.zeros_like(o_ref)
        o_ref[...] += _mm_tn(a_ref[...], b_ref[...])

    if chips == 1:
        out_spec = pl.BlockSpec((ka, tn), lambda n, i: (0, n))
        out_shape = jax.ShapeDtypeStruct((ka, nb), F32)
    else:
        out_spec = pl.BlockSpec((None, ka, tn), lambda n, i: (n // per, 0, n % per))
        out_shape = jax.ShapeDtypeStruct((chips, ka, cs), F32)
    return pl.pallas_call(
        body, name="wgrad_" + name, grid=(nb // tn, t // tm),
        in_specs=[pl.BlockSpec((tm, ka), lambda n, i: (i, 0)), pl.BlockSpec((tm, tn), lambda n, i: (i, n))],
        out_specs=out_spec, out_shape=out_shape,
        compiler_params=_cp("parallel", "arbitrary"),
    )(a, b)


def _loss_head(y, target, t_end):
    t = y.shape[0]
    tm = _tile(t, 384)

    def body(y_ref, tg_ref, dy_ref, l_ref):
        i = pl.program_id(0)
        r = i * tm + lax.broadcasted_iota(jnp.int32, (tm, 1), 0)
        real = ((r >= ROW0) & (r < t_end)).astype(F32)
        diff = (y_ref[...] - tg_ref[...]) * real
        dy_ref[...] = diff * (1.0 / D)

        @pl.when(i == 0)
        def _():
            l_ref[...] = jnp.zeros_like(l_ref)
        sq = _colsum8(diff * diff)
        part = sq[:, 0:LANE]
        for j in range(1, D // LANE):
            part = part + sq[:, j * LANE:(j + 1) * LANE]
        l_ref[...] += part * (0.5 / D)

    return pl.pallas_call(
        body, name="loss_head", grid=(t // tm,),
        in_specs=[_row(tm, D), _row(tm, D)],
        out_specs=[_row(tm, D), _full((8, LANE))],
        out_shape=[jax.ShapeDtypeStruct((t, D), F32), jax.ShapeDtypeStruct((8, LANE), F32)],
        compiler_params=_cp("arbitrary"),
    )(y, target)


def _lower_bounds_fwd(logits):
    depth, n = logits.shape

    def body(l_ref, lb_ref):
        lg = l_ref[...]
        m = jnp.max(lg, axis=0, keepdims=True)
        e = jnp.exp(lg - m)
        p = e / jnp.sum(e, axis=0, keepdims=True)
        acc = jnp.zeros((1, n), F32)
        for l in range(depth):
            if l > 0:
                acc = acc + p[l:l + 1, :]
            lb_ref[l:l + 1, :] = acc

    return pl.pallas_call(body, name="lower_bounds_fwd", out_shape=jax.ShapeDtypeStruct((depth, n), F32))(logits)


def _lower_bounds_bwd(logits, dlb):
    depth, n = logits.shape

    def body(l_ref, dlb_ref, dl_ref):
        lg = l_ref[...]
        m = jnp.max(lg, axis=0, keepdims=True)
        e = jnp.exp(lg - m)
        p = e / jnp.sum(e, axis=0, keepdims=True)
        dps = [jnp.zeros((1, n), F32)]
        for j in range(1, depth):
            acc = jnp.zeros((1, n), F32)
            for l in range(j, depth):
                acc = acc + dlb_ref[l:l + 1, :]
            dps.append(acc)
        dot = jnp.zeros((1, n), F32)
        for j in range(depth):
            dot = dot + p[j:j + 1, :] * dps[j]
        for j in range(depth):
            dl_ref[j:j + 1, :] = p[j:j + 1, :] * (dps[j] - dot)

    return pl.pallas_call(body, name="lower_bounds_bwd", out_shape=jax.ShapeDtypeStruct((depth, n), F32))(logits, dlb)


def _ew_tile(rows, cols, n_arrays):
    cap = max(16, (24 * 1024 * 1024) // (8 * n_arrays * cols))
    best = None
    for t in range(16, rows + 1, 16):
        if rows % t == 0 and t <= cap:
            best = t
    return rows if best is None else best


def _adamw(w, g, m, v, name):
    rows, cols = w.shape
    tr = _ew_tile(rows, cols, 7)

    def body(w_ref, g_ref, m_ref, v_ref, d_ref, mo_ref, vo_ref):
        gv = g_ref[...]
        mn = ADAM_B1 * m_ref[...] + (1.0 - ADAM_B1) * gv
        vn = ADAM_B2 * v_ref[...] + (1.0 - ADAM_B2) * (gv * gv)
        m_hat = mn / (1.0 - ADAM_B1 ** ADAM_STEP)
        v_hat = vn / (1.0 - ADAM_B2 ** ADAM_STEP)
        d_ref[...] = -ADAM_LR * (m_hat / (jnp.sqrt(v_hat) + ADAM_EPS) + ADAM_WD * w_ref[...])
        mo_ref[...] = mn
        vo_ref[...] = vn

    spec = pl.BlockSpec((tr, cols), lambda i: (i, 0))
    return pl.pallas_call(
        body, name="adamw_" + name, grid=(rows // tr,),
        in_specs=[spec] * 4, out_specs=[spec] * 3,
        out_shape=[jax.ShapeDtypeStruct((rows, cols), F32)] * 3,
        compiler_params=_cp("parallel"),
    )(w, g, m, v)


DEPTH = 2
BIG_SHAPES = {"w_in": ((1024, 6560), 1), "w_conv_out": ((512, 1024), 1), "w_uq": ((256, 768), 1),
              "w_ukv": ((128, 1024), 1), "w_attn_out": ((512, 1024), 1), "w_hgrn_out": ((512, 1024), 1),
              "w_out": ((1024, 1024), 0), "w_ff1": ((1024, 4096), 1), "w_ff2": ((4096, 1024), 0)}
BIG = tuple(BIG_SHAPES)
SMALL_SIZES = {"norm1_g": 1024, "conv_b": 512, "conv_ln_g": 512, "conv_ln_b": 512, "q_a_norm_g": 256,
               "kv_a_norm_g": 128, "q_norm_g": 96, "k_norm_g": 96, "hgrn_lb_logits": 512, "hgrn_norm_g": 512,
               "norm2_g": 1024}
SMALL = tuple(SMALL_SIZES)
W_IN_COLS = 6560
W_IN_SHARD = W_IN_COLS // 4
W_IN_SEGS = ((0, 1024, SEG_AG[0]), (1024, 1280, SEG_CQ[0]), (1280, 1408, SEG_CKV[0]), (1408, 1440, SEG_KR[0] + 64),
             (1440, 3488, SEG_H4[0]), (3488, 6560, SEG_GATES[0]))


def _pad_heads(w, nh, used, axis):
    shp = w.shape
    w = w.reshape(shp[:axis] + (nh, used) + shp[axis + 1:])
    pad = [(0, 0)] * w.ndim
    pad[axis + 1] = (0, LANE - used)
    w = jnp.pad(w, pad)
    return w.reshape(shp[:axis] + (nh * LANE,) + shp[axis + 1:])


def _unpad_heads(w, nh, used, axis):
    shp = w.shape
    w = w.reshape(shp[:axis] + (nh, LANE) + shp[axis + 1:])
    w = lax.slice_in_dim(w, 0, used, axis=axis + 1)
    return w.reshape(shp[:axis] + (nh * used,) + shp[axis + 1:])


def _w_in_from_chips(p4):
    def orig(a, b):
        out = []
        while a < b:
            s = a // W_IN_SHARD
            e = min(b, (s + 1) * W_IN_SHARD)
            out.append(p4[s][:, a - W_IN_SHARD * s:e - W_IN_SHARD * s])
            a = e
        return out
    zc = lambda n: jnp.zeros((D, n), p4.dtype)
    parts = (orig(3488, 6560) + orig(0, 1024) + orig(1440, 3488) + orig(1024, 1280) + orig(1280, 1408)
             + [zc(64)] + orig(1408, 1440) + [zc(32)])
    return jnp.concatenate(parts, axis=1)


def _w_in_grad_to_chips(dw):
    chips = []
    for s in range(4):
        a, b = W_IN_SHARD * s, W_IN_SHARD * (s + 1)
        parts = []
        for o0, o1, p0 in W_IN_SEGS:
            lo, hi = max(a, o0), min(b, o1)
            if lo < hi:
                parts.append(dw[:, p0 + lo - o0:p0 + hi - o0])
        chips.append(jnp.concatenate(parts, axis=1))
    return jnp.stack(chips)


def _cat_chips(p4, axis):
    return jnp.concatenate([p4[s] for s in range(4)], axis=axis)


def _prep_layer(pieces, small, l):
    mm = lambda a: a.astype(_MM)
    pc = lambda k: mm(pieces[k][:, l])
    w_in_p = _w_in_from_chips(pc("w_in"))
    wuq = jnp.concatenate([_pad_heads(pc("w_uq")[s], 2, QK_DIM, 1) for s in range(4)], axis=1)
    wukv = _cat_chips(pc("w_ukv"), 1).reshape(128, NH, 128)
    wk = _pad_heads(wukv[:, :, :64].reshape(128, NH * 64), NH, 64, 1)
    wv = _pad_heads(wukv[:, :, 64:].reshape(128, NH * 64), NH, 64, 1)
    wao = _pad_heads(_cat_chips(pc("w_attn_out"), 1), NH, 64, 0)
    row = lambda a: a.astype(F32).reshape(1, -1)
    p = dict(
        w_in=w_in_p, w_in_t=w_in_p.T, wuq=wuq, wuq_t=wuq.T, wk=wk, wk_t=wk.T, wv=wv, wv_t=wv.T,
        wao=wao, wao_t=wao.T, wco=_cat_chips(pc("w_conv_out"), 1), who=_cat_chips(pc("w_hgrn_out"), 1),
        wout=_cat_chips(pc("w_out"), 0), w1=_cat_chips(pc("w_ff1"), 1), w2=_cat_chips(pc("w_ff2"), 0),
        g1=row(small["norm1_g"][l]), g2=row(small["norm2_g"][l]),
        cw=jnp.pad(small["conv_w"][l].astype(F32), ((0, 1), (0, 0))), cb=row(small["conv_b"][l]),
        lng=row(small["conv_ln_g"][l]), lnb=row(small["conv_ln_b"][l]),
        qag=row(small["q_a_norm_g"][l]), kvag=row(small["kv_a_norm_g"][l]),
        qng=jnp.pad(row(small["q_norm_g"][l]), ((0, 0), (0, LANE - QK_DIM))),
        kng=jnp.pad(row(small["k_norm_g"][l]), ((0, 0), (0, LANE - QK_DIM))),
        ng=row(small["hgrn_norm_g"][l]),
    )
    for k in ("wco", "who", "wout", "w1", "w2"):
        p[k + "_t"] = p[k].T
    return p


def _rope_tables(t):
    pos = (jnp.arange(t, dtype=jnp.int32) - FRONT).astype(F32)
    inv_freq = 10000.0 ** (-jnp.arange(16, dtype=F32) / 16)
    ang = pos[:, None] * inv_freq[None, :]
    cos, sin = jnp.cos(ang), jnp.sin(ang)
    one = jnp.ones((t, 64), F32)
    z16, z32, z64 = jnp.zeros((t, 16), F32), jnp.zeros((t, 32), F32), jnp.zeros((t, 64), F32)
    c = jnp.concatenate([one, cos, cos, z32], axis=1)
    s1 = jnp.concatenate([z64, -sin, z16, z32], axis=1)
    s2 = jnp.concatenate([z64, z16, sin, z32], axis=1)
    return c, s1, s2


def _layer_fwd(x, p, lb, rope, t_end):
    gates, ag, h4, cq, ckv, kr, hb = _in_proj_fwd(x, p["g1"], p["w_in"])
    z = _conv_fwd(ag, p["cw"], p["cb"])
    q, k, v, cqn, ckvn = _mla_pre_fwd(cq, ckv, kr, p["qag"], p["wuq"], p["kvag"], p["wk"], p["wv"], p["qng"],
                                      p["kng"], *rope)
    o_att, lse = _attn_fwd(q, k, v)
    o_h, states = _hgrn_fwd(h4, lb)
    x1, mix, ca, oc, ya, yb, yc = _mix_fwd(x, z, o_att, o_h, h4, gates, p["lng"], p["lnb"], p["wco"], p["wao"],
                                           p["ng"], p["who"], p["wout"], t_end)
    x2, pre = _ffn_fwd(x1, p["g2"], p["w1"], p["w2"])
    saved = dict(x=x, gates=gates, ag=ag, h4=h4, cq=cq, ckv=ckv, kr=kr, hb=hb, z=z, q=q, k=k, v=v, cqn=cqn,
                 ckvn=ckvn, o_att=o_att, lse=lse, o_h=o_h, states=states, x1=x1, mix=mix, ca=ca, oc=oc,
                 ya=ya, yb=yb, yc=yc, pre=pre)
    return x2, saved


def _layer_bwd(dx2, s, p, lb, rope, t_end):
    dx1, h2, act, dp, dg2 = _ffn_bwd(dx2, s["x1"], s["pre"], p["g2"], p["w1_t"], p["w2_t"])
    g = {"w_ff1": _wgrad(h2, dp, "ff1", 4), "w_ff2": _wgrad(act, dx2, "ff2").reshape(4, D_FF // 4, D),
         "norm2_g": dg2.sum(0)}
    (dgt, dya, dyb, dyc, dz, doa, doh, dhg, dlng, dlnb, dcb, dng) = _mix_bwd(
        dx1, s["ya"], s["yb"], s["yc"], s["gates"], s["z"], s["o_h"], s["h4"], p["lng"], p["lnb"], p["ng"],
        p["wout_t"], p["wco_t"], p["wao_t"], p["who_t"])
    g["w_out"] = _wgrad(s["mix"], dx1, "out").reshape(4, D // 4, D)
    g["w_conv_out"] = _wgrad(s["ca"], dya, "conv_out", 4)
    g["w_attn_out"] = _unpad_heads(_wgrad(s["o_att"], dyb, "attn_out", 4), NH, 64, 1)
    g["w_hgrn_out"] = _wgrad(s["oc"], dyc, "hgrn_out", 4)
    g["conv_ln_g"], g["conv_ln_b"], g["conv_b"], g["hgrn_norm_g"] = dlng.sum(0), dlnb.sum(0), dcb.sum(0), dng.sum(0)
    da, dg, dcw = _conv_bwd(s["ag"], p["cw"], dz)
    g["conv_w"] = dcw[:CONV_K]
    dq, dk, dv = _attn_bwd(s["q"], s["k"], s["v"], s["o_att"], s["lse"], doa)
    dcq, dckv, dkr, dqraw, dkraw, dqag, dkvag, dqng, dkng = _mla_pre_bwd(
        dq, dk, dv, s["cq"], s["ckv"], s["kr"], p["qag"], p["wuq"], p["wuq_t"], p["kvag"], p["wk"], p["wk_t"],
        p["wv_t"], p["qng"], p["kng"], *rope)
    g["w_uq"] = _unpad_heads(_wgrad(s["cqn"], dqraw, "uq", 4), 2, QK_DIM, 2)
    dwk = _unpad_heads(_wgrad(s["ckvn"], dkraw, "uk"), NH, 64, 1).reshape(128, NH, 64)
    dwv = _unpad_heads(_wgrad(s["ckvn"], dv, "uv"), NH, 64, 1).reshape(128, NH, 64)
    g["w_ukv"] = jnp.concatenate([dwk, dwv], axis=2).reshape(128, 4, 256).transpose(1, 0, 2)
    g["q_a_norm_g"], g["kv_a_norm_g"] = dqag.sum(0), dkvag.sum(0)
    g["q_norm_g"], g["k_norm_g"] = dqng.sum(0)[:QK_DIM], dkng.sum(0)[:QK_DIM]
    dhq, dhf, dhi, dlb = _hgrn_bwd(s["h4"], lb, doh, s["states"])
    mm = lambda a: a.astype(_MM)
    du = jnp.concatenate([dgt, mm(da), mm(dg), mm(dhq), mm(dhf), mm(dhi), dhg, dcq, dckv, dkr], axis=1)
    dx, dg1 = _in_proj_bwd(du, s["x"], dx1, p["g1"], p["w_in_t"], t_end)
    g["norm1_g"] = dg1.sum(0)
    g["w_in"] = _w_in_grad_to_chips(_wgrad(s["hb"], du, "in"))
    return dx, g, dlb.sum(0)


def _device_step(x, target, pieces, small):
    s_real = x.shape[0]
    t_end = ROW0 + s_real
    t = -(-t_end // LANE) * LANE
    zrow = lambda n: jnp.zeros((n, D), F32)
    xp = jnp.concatenate([zrow(FRONT), small["meta"].astype(F32), x, zrow(t - t_end)], axis=0)
    tp = jnp.concatenate([zrow(ROW0), target, zrow(t - t_end)], axis=0)
    rope = _rope_tables(t)
    logits = small["hgrn_lb_logits"].astype(F32)
    lbs = _lower_bounds_fwd(logits)
    prm = [_prep_layer(pieces, small, l) for l in range(DEPTH)]
    saved = []
    h = xp
    for l in range(DEPTH):
        h, sv = _layer_fwd(h, prm[l], lbs[l:l + 1], rope, t_end)
        saved.append(sv)
    dh, lsum = _loss_head(h, tp, t_end)
    loss = jnp.sum(lsum)
    grads = [None] * DEPTH
    dlbs = [None] * DEPTH
    for l in reversed(range(DEPTH)):
        dh, grads[l], dlbs[l] = _layer_bwd(dh, saved[l], prm[l], lbs[l:l + 1], rope, t_end)
    dlogits = _lower_bounds_bwd(logits, jnp.stack(dlbs))
    for l in range(DEPTH):
        grads[l]["hgrn_lb_logits"] = dlogits[l]
    return loss, dh[ROW0:t_end], grads, dh[FRONT:ROW0]


MESH = pl.DeviceIdType.MESH
_ANY = pl.BlockSpec(memory_space=pl.ANY)
SMALL_ROWS = 64
SMALL_LEN = SMALL_ROWS * 1024


def _mesh_pos():
    return lax.axis_index("x"), lax.axis_index("y"), lax.axis_index("c")


def _other_chips(x, y):
    return [(1 - x, y), (x, 1 - y), (1 - x, 1 - y)]


def _all_gather_weights(shards):
    nw = len(shards)

    def body(*refs):
        srcs, outs = refs[:nw], refs[nw:2 * nw]
        send_sems, recv_sems, lsems = refs[2 * nw:]
        x, y, c = _mesh_pos()
        me, sib = (x, y, c), (x, y, 1 - c)
        chips = _other_chips(x, y)

        def copy(w, k, blk, to, from_src=False):
            px, py, pc = blk
            slot = outs[w].at[2 * px + py, pc]
            return pltpu.make_async_remote_copy(
                src_ref=srcs[w].at[c] if from_src else slot, dst_ref=slot,
                send_sem=send_sems.at[w, k], recv_sem=recv_sems.at[w, k], device_id=to, device_id_type=MESH)

        started = []
        for w in range(nw):
            local = pltpu.make_async_copy(srcs[w].at[c], outs[w].at[2 * x + y, c], lsems.at[w])
            local.start()
            started.append(local)
        first = []
        for j, ch in enumerate(chips):
            for w in range(nw):
                first.append(copy(w, 1 + j, me, (*ch, c), True))
        for w in range(nw):
            first.append(copy(w, 0, me, sib, True))
        for cp in first:
            cp.start()
        passed = []
        for j, ch in enumerate(chips):
            for w in range(nw):
                copy(w, 1 + j, (*ch, c), me).wait_recv()
                fwd = copy(w, 4 + j, (*ch, c), sib)
                fwd.start()
                passed.append(fwd)
        for w in range(nw):
            copy(w, 0, sib, me).wait_recv()
        for j, ch in enumerate(chips):
            for w in range(nw):
                copy(w, 4 + j, (*ch, 1 - c), me).wait_recv()
        for cp in first + passed:
            cp.wait_send()
        for local in started:
            local.wait()

    return pl.pallas_call(
        body, name="all_gather_weights", in_specs=[_ANY] * nw, out_specs=[_ANY] * nw,
        out_shape=[jax.ShapeDtypeStruct((4,) + a.shape, a.dtype) for a in shards],
        scratch_shapes=[pltpu.SemaphoreType.DMA((nw, 7)), pltpu.SemaphoreType.DMA((nw, 7)),
                        pltpu.SemaphoreType.DMA((nw,))],
    )(*shards)


def _sibling_swap(g0, g1):
    nw = len(g0)

    def body(*refs):
        a0, a1, outs = refs[:nw], refs[nw:2 * nw], refs[2 * nw:3 * nw]
        send_sems, recv_sems = refs[3 * nw:]
        x, y, c = _mesh_pos()

        def copy(w, src):
            return pltpu.make_async_remote_copy(src_ref=src, dst_ref=outs[w], send_sem=send_sems.at[w],
                                                recv_sem=recv_sems.at[w], device_id=(x, y, 1 - c),
                                                device_id_type=MESH)

        @pl.when(c == 0)
        def _():
            for w in range(nw):
                copy(w, a1[w]).start()

        @pl.when(c == 1)
        def _():
            for w in range(nw):
                copy(w, a0[w]).start()

        for w in range(nw):
            copy(w, a0[w]).wait()

    return pl.pallas_call(
        body, name="sibling_swap", in_specs=[_ANY] * (2 * nw), out_specs=[_ANY] * nw,
        out_shape=[jax.ShapeDtypeStruct(a.shape, a.dtype) for a in g0],
        scratch_shapes=[pltpu.SemaphoreType.DMA((nw,)), pltpu.SemaphoreType.DMA((nw,))],
    )(*g0, *g1)


def _chip_scatter(parts):
    nw = len(parts)

    def body(*refs):
        srcs, outs = refs[:nw], refs[nw:2 * nw]
        send_sems, recv_sems, lsems = refs[2 * nw:]
        x, y, c = _mesh_pos()
        me = 2 * x + y
        chips = _other_chips(x, y)
        locals_ = []
        for w in range(nw):
            lc = pltpu.make_async_copy(srcs[w].at[me], outs[w].at[me], lsems.at[w])
            lc.start()
            locals_.append(lc)
        cps = []
        for j, (px, py) in enumerate(chips):
            for w in range(nw):
                cps.append(pltpu.make_async_remote_copy(
                    src_ref=srcs[w].at[2 * px + py], dst_ref=outs[w].at[me], send_sem=send_sems.at[w, j],
                    recv_sem=recv_sems.at[w, j], device_id=(px, py, c), device_id_type=MESH))
        for cp in cps:
            cp.start()
        for j, (px, py) in enumerate(chips):
            for w in range(nw):
                pltpu.make_async_remote_copy(
                    src_ref=srcs[w].at[me], dst_ref=outs[w].at[2 * px + py], send_sem=send_sems.at[w, j],
                    recv_sem=recv_sems.at[w, j], device_id=(x, y, c), device_id_type=MESH).wait_recv()
        for cp in cps:
            cp.wait_send()
        for lc in locals_:
            lc.wait()

    return pl.pallas_call(
        body, name="chip_scatter", in_specs=[_ANY] * nw, out_specs=[_ANY] * nw,
        out_shape=[jax.ShapeDtypeStruct(a.shape, a.dtype) for a in parts],
        scratch_shapes=[pltpu.SemaphoreType.DMA((nw, 3)), pltpu.SemaphoreType.DMA((nw, 3)),
                        pltpu.SemaphoreType.DMA((nw,))],
    )(*parts)


def _sibling_gather(reds):
    nw = len(reds)

    def body(*refs):
        srcs, outs = refs[:nw], refs[nw:2 * nw]
        send_sems, recv_sems, lsems = refs[2 * nw:]
        x, y, c = _mesh_pos()
        started = []
        for w in range(nw):
            lc = pltpu.make_async_copy(srcs[w], outs[w].at[c], lsems.at[w])
            lc.start()
            cp = pltpu.make_async_remote_copy(src_ref=srcs[w], dst_ref=outs[w].at[c], send_sem=send_sems.at[w],
                                              recv_sem=recv_sems.at[w], device_id=(x, y, 1 - c),
                                              device_id_type=MESH)
            cp.start()
            started.append((lc, cp))
        for w in range(nw):
            pltpu.make_async_remote_copy(src_ref=srcs[w], dst_ref=outs[w].at[1 - c], send_sem=send_sems.at[w],
                                         recv_sem=recv_sems.at[w], device_id=(x, y, c),
                                         device_id_type=MESH).wait_recv()
        for lc, cp in started:
            cp.wait_send()
            lc.wait()

    return pl.pallas_call(
        body, name="sibling_gather", in_specs=[_ANY] * nw, out_specs=[_ANY] * nw,
        out_shape=[jax.ShapeDtypeStruct((2,) + a.shape, a.dtype) for a in reds],
        scratch_shapes=[pltpu.SemaphoreType.DMA((nw,)), pltpu.SemaphoreType.DMA((nw,)),
                        pltpu.SemaphoreType.DMA((nw,))],
    )(*reds)


def _all_reduce_small(v, name):
    rows, cols = v.shape

    def body(v_ref, o_ref, slots, send_sems, recv_sems):
        x, y, c = _mesh_pos()
        me = 4 * x + 2 * y + c
        slots[me] = v_ref[...]
        peers = []
        for rel in range(1, 8):
            fx, fy, fc = (rel >> 2) & 1, (rel >> 1) & 1, rel & 1
            px = 1 - x if fx else x
            py = 1 - y if fy else y
            pc = 1 - c if fc else c
            peers.append((px, py, pc))
        cps = [pltpu.make_async_remote_copy(src_ref=v_ref, dst_ref=slots.at[me], send_sem=send_sems.at[k],
                                            recv_sem=recv_sems.at[k], device_id=peer, device_id_type=MESH)
               for k, peer in enumerate(peers)]
        for cp in cps:
            cp.start()
        for k, (px, py, pc) in enumerate(peers):
            pltpu.make_async_remote_copy(src_ref=v_ref, dst_ref=slots.at[4 * px + 2 * py + pc],
                                         send_sem=send_sems.at[k], recv_sem=recv_sems.at[k], device_id=(x, y, c),
                                         device_id_type=MESH).wait_recv()
        for cp in cps:
            cp.wait_send()
        acc = slots[0]
        for d in range(1, 8):
            acc = acc + slots[d]
        o_ref[...] = acc

    vm = pl.BlockSpec(memory_space=pltpu.VMEM)
    return pl.pallas_call(
        body, name=name, in_specs=[vm], out_specs=vm,
        out_shape=jax.ShapeDtypeStruct((rows, cols), F32),
        scratch_shapes=[pltpu.VMEM((8, rows, cols), F32), pltpu.SemaphoreType.DMA((7,)),
                        pltpu.SemaphoreType.DMA((7,))],
    )(v)


def _add_to_wire(a, b, name):
    n4, r, c_ = a.shape
    rows = n4 * r
    tr = _ew_tile(rows, c_, 3)

    def body(a_ref, b_ref, o_ref):
        o_ref[...] = (a_ref[...] + b_ref[...]).astype(o_ref.dtype)

    spec = pl.BlockSpec((tr, c_), lambda i: (i, 0))
    out = pl.pallas_call(
        body, name="add_to_wire_" + name, grid=(rows // tr,), in_specs=[spec, spec], out_specs=spec,
        out_shape=jax.ShapeDtypeStruct((rows, c_), jnp.bfloat16), compiler_params=_cp("parallel"),
    )(a.reshape(rows, c_), b.reshape(rows, c_))
    return out.reshape(n4, r, c_)


def _sum_chips(parts, name):
    _, r, c_ = parts.shape
    tr = _ew_tile(r, c_, 3)

    def body(p_ref, o_ref):
        acc = p_ref[0].astype(F32)
        for s in range(1, 4):
            acc = acc + p_ref[s].astype(F32)
        o_ref[...] = acc

    return pl.pallas_call(
        body, name="sum_chips_" + name, grid=(r // tr,),
        in_specs=[pl.BlockSpec((4, tr, c_), lambda i: (0, i, 0))],
        out_specs=pl.BlockSpec((tr, c_), lambda i: (i, 0)),
        out_shape=jax.ShapeDtypeStruct((r, c_), F32),
        compiler_params=_cp("parallel"),
    )(parts)


def _pack_small(vals, meta_full, conv_w_full):
    flat = jnp.concatenate([vals[k].reshape(-1) for k in SMALL] + [meta_full.reshape(-1), conv_w_full.reshape(-1)])
    return jnp.pad(flat, (0, SMALL_LEN - flat.shape[0])).reshape(SMALL_ROWS, 1024)


def _unpack_small(buf):
    flat = buf.reshape(-1)
    out, off = {}, 0
    for k in SMALL:
        n = DEPTH * SMALL_SIZES[k]
        out[k] = flat[off:off + n].reshape(DEPTH, SMALL_SIZES[k])
        off += n
    meta = flat[off:off + N_META * D].reshape(N_META, D)
    off += N_META * D
    conv_w = flat[off:off + DEPTH * CONV_K * CONV_DIM].reshape(DEPTH, CONV_K, CONV_DIM)
    return out, meta, conv_w


def kernel(x, meta, norm1_g, w_in, conv_w, conv_b, conv_ln_g, conv_ln_b, w_conv_out, q_a_norm_g, w_uq, kv_a_norm_g, w_ukv, q_norm_g, k_norm_g, w_attn_out, hgrn_lb_logits, hgrn_norm_g, w_hgrn_out, w_out, norm2_g, w_ff1, w_ff2, loss_target, m_meta, m_norm1_g, m_w_in, m_conv_w, m_conv_b, m_conv_ln_g, m_conv_ln_b, m_w_conv_out, m_q_a_norm_g, m_w_uq, m_kv_a_norm_g, m_w_ukv, m_q_norm_g, m_k_norm_g, m_w_attn_out, m_hgrn_lb_logits, m_hgrn_norm_g, m_w_hgrn_out, m_w_out, m_norm2_g, m_w_ff1, m_w_ff2, v_meta, v_norm1_g, v_w_in, v_conv_w, v_conv_b, v_conv_ln_g, v_conv_ln_b, v_w_conv_out, v_q_a_norm_g, v_w_uq, v_kv_a_norm_g, v_w_ukv, v_q_norm_g, v_k_norm_g, v_w_attn_out, v_hgrn_lb_logits, v_hgrn_norm_g, v_w_hgrn_out, v_w_out, v_norm2_g, v_w_ff1, v_w_ff2):
    names = ("meta", "norm1_g", "w_in", "conv_w", "conv_b", "conv_ln_g", "conv_ln_b", "w_conv_out", "q_a_norm_g",
             "w_uq", "kv_a_norm_g", "w_ukv", "q_norm_g", "k_norm_g", "w_attn_out", "hgrn_lb_logits", "hgrn_norm_g",
             "w_hgrn_out", "w_out", "norm2_g", "w_ff1", "w_ff2")
    w = dict(zip(names, (meta, norm1_g, w_in, conv_w, conv_b, conv_ln_g, conv_ln_b, w_conv_out, q_a_norm_g, w_uq,
                         kv_a_norm_g, w_ukv, q_norm_g, k_norm_g, w_attn_out, hgrn_lb_logits, hgrn_norm_g, w_hgrn_out,
                         w_out, norm2_g, w_ff1, w_ff2)))
    m = dict(zip(names, (m_meta, m_norm1_g, m_w_in, m_conv_w, m_conv_b, m_conv_ln_g, m_conv_ln_b, m_w_conv_out,
                         m_q_a_norm_g, m_w_uq, m_kv_a_norm_g, m_w_ukv, m_q_norm_g, m_k_norm_g, m_w_attn_out,
                         m_hgrn_lb_logits, m_hgrn_norm_g, m_w_hgrn_out, m_w_out, m_norm2_g, m_w_ff1, m_w_ff2)))
    v = dict(zip(names, (v_meta, v_norm1_g, v_w_in, v_conv_w, v_conv_b, v_conv_ln_g, v_conv_ln_b, v_w_conv_out,
                         v_q_a_norm_g, v_w_uq, v_kv_a_norm_g, v_w_ukv, v_q_norm_g, v_k_norm_g, v_w_attn_out,
                         v_hgrn_lb_logits, v_hgrn_norm_g, v_w_hgrn_out, v_w_out, v_norm2_g, v_w_ff1, v_w_ff2)))
    cx, cy, cc = _mesh_pos()
    chip = 2 * cx + cy
    zero = jnp.zeros((), jnp.int32)

    gathered = _all_gather_weights([w[k].astype(_MM) for k in BIG])
    pieces = dict(zip(BIG, gathered))
    meta_slab = lax.dynamic_update_slice(jnp.zeros((N_META, D), F32), meta, (zero, chip * (D // 4)))
    convw_slab = lax.dynamic_update_slice(jnp.zeros((DEPTH, CONV_K, CONV_DIM), F32), conv_w,
                                          (zero, zero, chip * (CONV_DIM // 4)))
    zsmall = {k: jnp.zeros((DEPTH, SMALL_SIZES[k]), F32) for k in SMALL}
    south = (cc == 0).astype(F32)
    _, meta_full, convw_full = _unpack_small(
        _all_reduce_small(_pack_small(zsmall, meta_slab, convw_slab) * south, "gather_small"))
    small = {k: w[k] for k in SMALL}
    small["meta"] = meta_full
    small["conv_w"] = convw_full

    loss_share, grad_x, gl, g_meta = _device_step(x[0], loss_target[0], pieces, small)
    loss = lax.psum(loss_share, ("x", "y", "c"))

    g0 = [gl[0][k] for k in BIG]
    g1 = [gl[1][k] for k in BIG]
    from_sibling = _sibling_swap(g0, g1)

    def chip_sums(mine):
        return [_add_to_wire(a, b, k) for k, a, b in zip(BIG, mine, from_sibling)]

    wire = lax.cond(cc == 0, lambda: chip_sums(g0), lambda: chip_sums(g1))
    reds = [_sum_chips(p, k) for k, p in zip(BIG, _chip_scatter(wire))]
    grads = dict(zip(BIG, _sibling_gather(reds)))

    g_small_local = {k: jnp.stack([gl[l][k] for l in range(DEPTH)]) for k in SMALL}
    g_convw_local = jnp.stack([gl[l]["conv_w"] for l in range(DEPTH)])
    g_small, g_meta_full, g_convw_full = _unpack_small(
        _all_reduce_small(_pack_small(g_small_local, g_meta, g_convw_local), "reduce_small"))
    grads.update(g_small)
    grads["meta"] = lax.dynamic_slice(g_meta_full, (zero, chip * (D // 4)), (N_META, D // 4))
    grads["conv_w"] = lax.dynamic_slice(g_convw_full, (zero, zero, chip * (CONV_DIM // 4)),
                                        (DEPTH, CONV_K, CONV_DIM // 4))

    delta, new_m, new_v = {}, {}, {}
    for k in BIG:
        shp = w[k].shape
        two_d = lambda a: a.reshape(shp[0] * shp[1], shp[2])
        d_, m_, v_ = _adamw(two_d(w[k]), two_d(grads[k]), two_d(m[k]), two_d(v[k]), k)
        delta[k], new_m[k], new_v[k] = d_.reshape(shp), m_.reshape(shp), v_.reshape(shp)

    def small_pack(src):
        return _pack_small(src, jnp.pad(src["meta"], ((0, 0), (0, D - D // 4))),
                           jnp.pad(src["conv_w"], ((0, 0), (0, 0), (0, CONV_DIM - CONV_DIM // 4))))

    def small_unpack(buf):
        out, meta_p, convw_p = _unpack_small(buf)
        out["meta"] = meta_p[:, :D // 4]
        out["conv_w"] = convw_p[:, :, :CONV_DIM // 4]
        return out

    d_s, m_s, v_s = [small_unpack(a) for a in _adamw(small_pack(w), small_pack(grads), small_pack(m),
                                                     small_pack(v), "small")]
    delta.update(d_s)
    new_m.update(m_s)
    new_v.update(v_s)
    return (loss, grad_x[None], *[grads[k] for k in names], *[delta[k] for k in names],
            *[new_m[k] for k in names], *[new_v[k] for k in names])
```

```python
import functools

import jax
import jax.numpy as jnp
from jax import lax
from jax.experimental import pallas as pl
from jax.experimental.pallas import tpu as pltpu

F32 = jnp.float32
_MM = jnp.bfloat16

D = 1024
N_META = 16
FRONT = 48
ROW0 = FRONT + N_META
EPS = 1e-6
GATE_CLAMP = 1.0 - 1e-6
CONV_K = 31
CONV_DIM = 512
NH = 8
QK_DIM = 96
ATT_SCALE = QK_DIM ** -0.5
HH = 4
CHUNK = 64
SUB = 16
EXP_CLIP = 60.0
NEG = -1e30
LANE = 128

SEG_GATES = (0, 3072)
SEG_AG = (3072, 4096)
SEG_H4 = (4096, 6144)
SEG_CQ = (6144, 6400)
SEG_CKV = (6400, 6528)
SEG_KR = (6528, 6656)
N_IN_P = 6656

ADAM_LR = 0.001
ADAM_B1 = 0.9
ADAM_B2 = 0.999
ADAM_EPS = 1e-08
ADAM_WD = 0.01
ADAM_STEP = 10

VMEM_LIMIT = 56 * 1024 * 1024


def _tile(n, pref):
    best = 64
    for t in range(64, pref + 1, 64):
        if n % t == 0:
            best = t
    return best


def _cp(*sem):
    return pltpu.CompilerParams(dimension_semantics=tuple(sem), vmem_limit_bytes=VMEM_LIMIT)


def _row(tm, n, col=0):
    return pl.BlockSpec((tm, n), lambda i: (i, col))


def _full(shape):
    return pl.BlockSpec(shape, lambda i: (0,) * len(shape))


def _mm(a, b):
    return jnp.dot(a.astype(_MM), b.astype(_MM), preferred_element_type=F32)


def _mm_nt(a, b):
    return lax.dot_general(a.astype(_MM), b.astype(_MM), (((1,), (1,)), ((), ())), preferred_element_type=F32)


def _mm_tn(a, b):
    return lax.dot_general(a.astype(_MM), b.astype(_MM), (((0,), (0,)), ((), ())), preferred_element_type=F32)


def _split3(x):
    hi = x.astype(jnp.bfloat16)
    return hi, (x - hi.astype(F32)).astype(jnp.bfloat16)


def _dot3(a, b, dims):
    ah, al = _split3(a)
    bh, bl = _split3(b)
    dg = lambda u, v: lax.dot_general(u, v, (dims, ((), ())), preferred_element_type=F32)
    return dg(ah, bh) + (dg(ah, bl) + dg(al, bh))


def _hmm(a, b):
    return _dot3(a, b, ((1,), (0,)))


def _hmm_nt(a, b):
    return _dot3(a, b, ((1,), (1,)))


def _hmm_tn(a, b):
    return _dot3(a, b, ((0,), (0,)))


def _sigmoid(x):
    return 1.0 / (1.0 + jnp.exp(-x))


def _rstd(x, n=None):
    n = x.shape[-1] if n is None else n
    return lax.rsqrt(jnp.sum(x * x, axis=-1, keepdims=True) * (1.0 / n) + EPS)


def _rms_bwd(dy, x, rstd, g, n=None):
    n = x.shape[-1] if n is None else n
    xh = x * rstd
    dxh = dy * g
    dx = rstd * (dxh - xh * (jnp.sum(dxh * xh, axis=-1, keepdims=True) * (1.0 / n)))
    return dx, dy * xh


def _valid_rows(i, tm, t_valid_end):
    r = i * tm + lax.broadcasted_iota(jnp.int32, (tm, 1), 0)
    return ((r >= FRONT) & (r < t_valid_end)).astype(F32)


def _colsum8(x):
    n, c = x.shape
    return jnp.sum(x.reshape(n // 8, 8, c), axis=0)


def _in_proj_fwd(x, g1, w):
    t = x.shape[0]
    tm = _tile(t, 192)
    segs = (SEG_GATES, SEG_AG, SEG_H4, SEG_CQ, SEG_CKV, SEG_KR)

    def body(x_ref, g_ref, w_ref, gates_ref, ag_ref, h4_ref, cq_ref, ckv_ref, kr_ref, hb_ref):
        xv = x_ref[...]
        hb = (xv * _rstd(xv) * g_ref[...]).astype(_MM)
        hb_ref[...] = hb
        for ref, (a, b) in zip((gates_ref, ag_ref, h4_ref, cq_ref, ckv_ref, kr_ref), segs):
            ref[...] = jnp.dot(hb, w_ref[:, a:b], preferred_element_type=F32)

    outs = [jax.ShapeDtypeStruct((t, b - a), F32) for a, b in segs] + [jax.ShapeDtypeStruct((t, D), _MM)]
    return pl.pallas_call(
        body, name="in_proj_fwd", grid=(t // tm,),
        in_specs=[_row(tm, D), _full((1, D)), _full((D, N_IN_P))],
        out_specs=[_row(tm, b - a) for a, b in segs] + [_row(tm, D)],
        out_shape=outs, compiler_params=_cp("parallel"),
    )(x, g1, w)


def _in_proj_bwd(du, x, dx1, g1, wt, t_end):
    t = x.shape[0]
    tm = _tile(t, 192)

    def body(du_ref, x_ref, dx1_ref, g_ref, wt_ref, dx_ref, dg_ref):
        i = pl.program_id(0)
        dh = jnp.dot(du_ref[...], wt_ref[...], preferred_element_type=F32)
        xv = x_ref[...]
        dxn, dgrow = _rms_bwd(dh, xv, _rstd(xv), g_ref[...])
        dx_ref[...] = _valid_rows(i, tm, t_end) * (dx1_ref[...] + dxn)

        @pl.when(i == 0)
        def _():
            dg_ref[...] = jnp.zeros_like(dg_ref)
        dg_ref[...] += _colsum8(dgrow)

    return pl.pallas_call(
        body, name="in_proj_bwd", grid=(t // tm,),
        in_specs=[_row(tm, N_IN_P), _row(tm, D), _row(tm, D), _full((1, D)), _full((N_IN_P, D))],
        out_specs=[_row(tm, D), _full((8, D))],
        out_shape=[jax.ShapeDtypeStruct((t, D), F32), jax.ShapeDtypeStruct((8, D), F32)],
        compiler_params=_cp("arbitrary"),
    )(du, x, dx1, g1, wt)


CONV_CH = 128


def _conv_fwd(ag, cw, cb):
    t = ag.shape[0]
    n = t // CONV_CH

    def body(a_ref, g_ref, w_ref, b_ref, z_ref, hp):
        hp[0:32, :] = jnp.zeros((32, LANE), F32)

        def fill(i, c):
            r = pl.multiple_of(i * CONV_CH, CONV_CH)
            hp[pl.ds(32 + r, CONV_CH), :] = a_ref[pl.ds(r, CONV_CH), :] * _sigmoid(g_ref[pl.ds(r, CONV_CH), :])
            return c
        lax.fori_loop(0, n, fill, 0)

        def conv(i, c):
            r = pl.multiple_of(i * CONV_CH, CONV_CH)
            acc = jnp.broadcast_to(b_ref[...], (CONV_CH, LANE))
            for k in range(CONV_K):
                acc = acc + w_ref[k:k + 1, :] * hp[pl.ds(r + (k + 2), CONV_CH), :]
            z_ref[pl.ds(r, CONV_CH), :] = acc
            return c
        lax.fori_loop(0, n, conv, 0)

    nb = CONV_DIM // LANE
    return pl.pallas_call(
        body, name="conv_fwd", grid=(nb,),
        in_specs=[pl.BlockSpec((t, LANE), lambda j: (0, j)), pl.BlockSpec((t, LANE), lambda j: (0, nb + j)),
                  pl.BlockSpec((32, LANE), lambda j: (0, j)), pl.BlockSpec((1, LANE), lambda j: (0, j))],
        out_specs=pl.BlockSpec((t, LANE), lambda j: (0, j)),
        out_shape=jax.ShapeDtypeStruct((t, CONV_DIM), F32),
        scratch_shapes=[pltpu.VMEM((t + 32, LANE), F32)],
        compiler_params=_cp("parallel"),
    )(ag, ag, cw, cb)


def _conv_bwd(ag, cw, dz):
    t = ag.shape[0]
    n = t // CONV_CH

    def body(a_ref, g_ref, w_ref, dz_ref, da_ref, dg_ref, dcw_ref, hp, dzp, accw):
        hp[0:32, :] = jnp.zeros((32, LANE), F32)
        dzp[pl.ds(t, 32), :] = jnp.zeros((32, LANE), F32)
        accw[...] = jnp.zeros_like(accw)

        def fill(i, c):
            r = pl.multiple_of(i * CONV_CH, CONV_CH)
            hp[pl.ds(32 + r, CONV_CH), :] = a_ref[pl.ds(r, CONV_CH), :] * _sigmoid(g_ref[pl.ds(r, CONV_CH), :])
            dzp[pl.ds(r, CONV_CH), :] = dz_ref[pl.ds(r, CONV_CH), :]
            return c
        lax.fori_loop(0, n, fill, 0)

        def step(i, c):
            r = pl.multiple_of(i * CONV_CH, CONV_CH)
            dzc = dz_ref[pl.ds(r, CONV_CH), :]
            dh = jnp.zeros((CONV_CH, LANE), F32)
            for k in range(CONV_K):
                dh = dh + w_ref[k:k + 1, :] * dzp[pl.ds(r + (CONV_K - 1 - k), CONV_CH), :]
                accw[8 * k:8 * k + 8, :] += _colsum8(dzc * hp[pl.ds(r + (k + 2), CONV_CH), :])
            a = a_ref[pl.ds(r, CONV_CH), :]
            sg = _sigmoid(g_ref[pl.ds(r, CONV_CH), :])
            da_ref[pl.ds(r, CONV_CH), :] = dh * sg
            dg_ref[pl.ds(r, CONV_CH), :] = dh * a * sg * (1.0 - sg)
            return c
        lax.fori_loop(0, n, step, 0)

        for k in range(CONV_K):
            dcw_ref[k:k + 1, :] = jnp.sum(accw[8 * k:8 * k + 8, :], axis=0, keepdims=True)
        dcw_ref[CONV_K:32, :] = jnp.zeros((32 - CONV_K, LANE), F32)

    nb = CONV_DIM // LANE
    colspec = pl.BlockSpec((t, LANE), lambda j: (0, j))
    return pl.pallas_call(
        body, name="conv_bwd", grid=(nb,),
        in_specs=[colspec, pl.BlockSpec((t, LANE), lambda j: (0, nb + j)),
                  pl.BlockSpec((32, LANE), lambda j: (0, j)), colspec],
        out_specs=[colspec, colspec, pl.BlockSpec((32, LANE), lambda j: (0, j))],
        out_shape=[jax.ShapeDtypeStruct((t, CONV_DIM), F32), jax.ShapeDtypeStruct((t, CONV_DIM), F32),
                   jax.ShapeDtypeStruct((32, CONV_DIM), F32)],
        scratch_shapes=[pltpu.VMEM((t + 32, LANE), F32), pltpu.VMEM((t + 32, LANE), F32),
                        pltpu.VMEM((8 * 32, LANE), F32)],
        compiler_params=_cp("parallel"),
    )(ag, ag, cw, dz)


def _rope(x, c, s1, s2):
    return x * c + pltpu.roll(x, LANE - 16, 1) * s1 + pltpu.roll(x, 16, 1) * s2


def _rope_t(dy, c, s1, s2):
    return dy * c + pltpu.roll(dy * s1, 16, 1) + pltpu.roll(dy * s2, LANE - 16, 1)


def _mla_pre_fwd(cq, ckv, kr, qag, wuq, kvag, wk, wv, qng, kng, rc, rs1, rs2):
    t = cq.shape[0]
    tm = _tile(t, 384)

    def body(cq_ref, ckv_ref, kr_ref, qag_ref, wuq_ref, kvag_ref, wk_ref, wv_ref, qng_ref, kng_ref,
             c_ref, s1_ref, s2_ref, q_ref, k_ref, v_ref, cqn_ref, ckvn_ref):
        cqv = cq_ref[...]
        cqn = (cqv * _rstd(cqv) * qag_ref[...]).astype(_MM)
        cqn_ref[...] = cqn
        ckvv = ckv_ref[...]
        ckvn = (ckvv * _rstd(ckvv) * kvag_ref[...]).astype(_MM)
        ckvn_ref[...] = ckvn
        qraw = jnp.dot(cqn, wuq_ref[...], preferred_element_type=F32)
        kraw = jnp.dot(ckvn, wk_ref[...], preferred_element_type=F32)
        v_ref[...] = jnp.dot(ckvn, wv_ref[...], preferred_element_type=F32).astype(_MM)
        krv = kr_ref[...]
        c, s1, s2 = c_ref[...], s1_ref[...], s2_ref[...]
        for h in range(NH):
            sl = slice(LANE * h, LANE * (h + 1))
            qh = qraw[:, sl]
            qn = qh * _rstd(qh, QK_DIM) * qng_ref[...]
            q_ref[:, sl] = (_rope(qn, c, s1, s2) * ATT_SCALE).astype(_MM)
            kh = kraw[:, sl] + krv
            kn = kh * _rstd(kh, QK_DIM) * kng_ref[...]
            k_ref[:, sl] = _rope(kn, c, s1, s2).astype(_MM)

    hd = NH * LANE
    return pl.pallas_call(
        body, name="mla_pre_fwd", grid=(t // tm,),
        in_specs=[_row(tm, 256), _row(tm, 128), _row(tm, 128), _full((1, 256)), _full((256, hd)),
                  _full((1, 128)), _full((128, hd)), _full((128, hd)), _full((1, LANE)), _full((1, LANE)),
                  _row(tm, LANE), _row(tm, LANE), _row(tm, LANE)],
        out_specs=[_row(tm, hd), _row(tm, hd), _row(tm, hd), _row(tm, 256), _row(tm, 128)],
        out_shape=[jax.ShapeDtypeStruct((t, hd), _MM)] * 3 + [jax.ShapeDtypeStruct((t, 256), _MM),
                                                              jax.ShapeDtypeStruct((t, 128), _MM)],
        compiler_params=_cp("parallel"),
    )(cq, ckv, kr, qag, wuq, kvag, wk, wv, qng, kng, rc, rs1, rs2)


def _mla_pre_bwd(dq, dk, dv, cq, ckv, kr, qag, wuq, wuqt, kvag, wk, wkt, wvt, qng, kng, rc, rs1, rs2):
    t = cq.shape[0]
    tm = _tile(t, 192)
    hd = NH * LANE

    def body(dq_ref, dk_ref, dv_ref, cq_ref, ckv_ref, kr_ref, qag_ref, wuq_ref, wuqt_ref, kvag_ref, wk_ref,
             wkt_ref, wvt_ref, qng_ref, kng_ref, c_ref, s1_ref, s2_ref,
             dcq_ref, dckv_ref, dkr_ref, dqraw_ref, dkraw_ref, dqag_ref, dkvag_ref, dqng_ref, dkng_ref):
        i = pl.program_id(0)
        cqv = cq_ref[...]
        rq_in = _rstd(cqv)
        cqn = (cqv * rq_in * qag_ref[...]).astype(_MM)
        ckvv = ckv_ref[...]
        rkv_in = _rstd(ckvv)
        ckvn = (ckvv * rkv_in * kvag_ref[...]).astype(_MM)
        qraw = jnp.dot(cqn, wuq_ref[...], preferred_element_type=F32)
        kraw = jnp.dot(ckvn, wk_ref[...], preferred_element_type=F32)
        krv = kr_ref[...]
        c, s1, s2 = c_ref[...], s1_ref[...], s2_ref[...]
        dkr = jnp.zeros((tm, LANE), F32)
        dqng = jnp.zeros((8, LANE), F32)
        dkng = jnp.zeros((8, LANE), F32)
        for h in range(NH):
            sl = slice(LANE * h, LANE * (h + 1))
            qh = qraw[:, sl]
            dqn = _rope_t(dq_ref[:, sl] * ATT_SCALE, c, s1, s2)
            dqh, gq = _rms_bwd(dqn, qh, _rstd(qh, QK_DIM), qng_ref[...], QK_DIM)
            dqraw_ref[:, sl] = dqh.astype(_MM)
            dqng = dqng + _colsum8(gq)
            kh = kraw[:, sl] + krv
            dkn = _rope_t(dk_ref[:, sl], c, s1, s2)
            dkh, gk = _rms_bwd(dkn, kh, _rstd(kh, QK_DIM), kng_ref[...], QK_DIM)
            dkraw_ref[:, sl] = dkh.astype(_MM)
            dkr = dkr + dkh
            dkng = dkng + _colsum8(gk)
        dkr_ref[...] = dkr.astype(_MM)
        dcqn = jnp.dot(dqraw_ref[...], wuqt_ref[...], preferred_element_type=F32)
        dcq, gqa = _rms_bwd(dcqn, cqv, rq_in, qag_ref[...])
        dcq_ref[...] = dcq.astype(_MM)
        dckvn = (jnp.dot(dkraw_ref[...], wkt_ref[...], preferred_element_type=F32)
                 + jnp.dot(dv_ref[...].astype(_MM), wvt_ref[...], preferred_element_type=F32))
        dckv, gkva = _rms_bwd(dckvn, ckvv, rkv_in, kvag_ref[...])
        dckv_ref[...] = dckv.astype(_MM)

        @pl.when(i == 0)
        def _():
            dqag_ref[...] = jnp.zeros_like(dqag_ref)
            dkvag_ref[...] = jnp.zeros_like(dkvag_ref)
            dqng_ref[...] = jnp.zeros_like(dqng_ref)
            dkng_ref[...] = jnp.zeros_like(dkng_ref)
        dqag_ref[...] += _colsum8(gqa)
        dkvag_ref[...] += _colsum8(gkva)
        dqng_ref[...] += dqng
        dkng_ref[...] += dkng

    return pl.pallas_call(
        body, name="mla_pre_bwd", grid=(t // tm,),
        in_specs=[_row(tm, hd), _row(tm, hd), _row(tm, hd), _row(tm, 256), _row(tm, 128), _row(tm, 128),
                  _full((1, 256)), _full((256, hd)), _full((hd, 256)), _full((1, 128)), _full((128, hd)),
                  _full((hd, 128)), _full((hd, 128)), _full((1, LANE)), _full((1, LANE)),
                  _row(tm, LANE), _row(tm, LANE), _row(tm, LANE)],
        out_specs=[_row(tm, 256), _row(tm, 128), _row(tm, 128), _row(tm, hd), _row(tm, hd),
                   _full((8, 256)), _full((8, 128)), _full((8, LANE)), _full((8, LANE))],
        out_shape=[jax.ShapeDtypeStruct((t, 256), _MM), jax.ShapeDtypeStruct((t, 128), _MM),
                   jax.ShapeDtypeStruct((t, 128), _MM), jax.ShapeDtypeStruct((t, hd), _MM),
                   jax.ShapeDtypeStruct((t, hd), _MM), jax.ShapeDtypeStruct((8, 256), F32),
                   jax.ShapeDtypeStruct((8, 128), F32), jax.ShapeDtypeStruct((8, LANE), F32),
                   jax.ShapeDtypeStruct((8, LANE), F32)],
        compiler_params=_cp("arbitrary"),
    )(dq, dk, dv, cq, ckv, kr, qag, wuq, wuqt, kvag, wk, wkt, wvt, qng, kng, rc, rs1, rs2)


def _attn_mask(r0, c0, tq):
    rows = r0 + lax.broadcasted_iota(jnp.int32, (tq, 1), 0)
    cols = c0 + lax.broadcasted_iota(jnp.int32, (1, tq), 1)
    return (cols <= rows) & (cols >= FRONT)


def _attn_fwd(q, k, v):
    t = q.shape[0]
    tq = _tile(t, 384)
    nq = t // tq

    def body(q_ref, k_ref, v_ref, o_ref, lse_ref):
        def qloop(qi, carry):
            r0 = pl.multiple_of(qi * tq, tq)
            qb = q_ref[pl.ds(r0, tq), :]

            def kstep(kj, st, masked):
                m, l, acc = st
                c0 = pl.multiple_of(kj * tq, tq)
                s = _mm_nt(qb, k_ref[pl.ds(c0, tq), :])
                if masked:
                    s = jnp.where(_attn_mask(r0, c0, tq), s, NEG)
                m2 = jnp.maximum(m, jnp.max(s, axis=-1, keepdims=True))
                p = jnp.exp(s - m2)
                a = jnp.exp(m - m2)
                l = a * l + jnp.sum(p, axis=-1, keepdims=True)
                acc = a * acc + _mm(p, v_ref[pl.ds(c0, tq), :])
                return m2, l, acc

            st = kstep(0, (jnp.full((tq, 1), NEG, F32), jnp.zeros((tq, 1), F32), jnp.zeros((tq, LANE), F32)), True)
            st = lax.fori_loop(1, qi, lambda kj, s_: kstep(kj, s_, False), st)
            m, l, acc = lax.cond(qi > 0, lambda s_: kstep(qi, s_, True), lambda s_: s_, st)
            o_ref[pl.ds(r0, tq), :] = acc / l
            lse_ref[pl.ds(r0, tq), :] = m + jnp.log(l)
            return carry
        lax.fori_loop(0, nq, qloop, 0)

    hs = pl.BlockSpec((t, LANE), lambda h: (0, h))
    return pl.pallas_call(
        body, name="attn_fwd", grid=(NH,),
        in_specs=[hs, hs, hs],
        out_specs=[hs, pl.BlockSpec((None, t, 1), lambda h: (h, 0, 0))],
        out_shape=[jax.ShapeDtypeStruct((t, NH * LANE), F32), jax.ShapeDtypeStruct((NH, t, 1), F32)],
        compiler_params=_cp("parallel"),
    )(q, k, v)


def _attn_bwd(q, k, v, o, lse, do):
    t = q.shape[0]
    tq = _tile(t, 384)
    nq = t // tq

    def body(q_ref, k_ref, v_ref, o_ref, lse_ref, do_ref, dq_ref, dk_ref, dv_ref, delta):
        def prep(i, c):
            r0 = pl.multiple_of(i * tq, tq)
            delta[pl.ds(r0, tq), :] = jnp.sum(do_ref[pl.ds(r0, tq), :] * o_ref[pl.ds(r0, tq), :], axis=-1,
                                              keepdims=True)
            dq_ref[pl.ds(r0, tq), :] = jnp.zeros((tq, LANE), F32)
            return c
        lax.fori_loop(0, nq, prep, 0)

        def kloop(kj, carry):
            c0 = pl.multiple_of(kj * tq, tq)
            kb = k_ref[pl.ds(c0, tq), :]
            vb = v_ref[pl.ds(c0, tq), :]

            def qstep(qi, st, masked):
                dkb, dvb = st
                r0 = pl.multiple_of(qi * tq, tq)
                qb = q_ref[pl.ds(r0, tq), :]
                dob = do_ref[pl.ds(r0, tq), :].astype(_MM)
                s = _mm_nt(qb, kb)
                if masked:
                    s = jnp.where(_attn_mask(r0, c0, tq), s, NEG)
                p = jnp.exp(s - lse_ref[pl.ds(r0, tq), :])
                dvb = dvb + _mm_tn(p, dob)
                dp = _mm_nt(dob, vb)
                ds = (p * (dp - delta[pl.ds(r0, tq), :])).astype(_MM)
                dkb = dkb + _mm_tn(ds, qb)
                dq_ref[pl.ds(r0, tq), :] += _mm(ds, kb)
                return dkb, dvb

            st = qstep(kj, (jnp.zeros((tq, LANE), F32), jnp.zeros((tq, LANE), F32)), True)
            dkb, dvb = lax.cond(
                kj == 0,
                lambda s_: lax.fori_loop(kj + 1, nq, lambda qi, t_: qstep(qi, t_, True), s_),
                lambda s_: lax.fori_loop(kj + 1, nq, lambda qi, t_: qstep(qi, t_, False), s_), st)
            dk_ref[pl.ds(c0, tq), :] = dkb
            dv_ref[pl.ds(c0, tq), :] = dvb
            return carry
        lax.fori_loop(0, nq, kloop, 0)

    hs = pl.BlockSpec((t, LANE), lambda h: (0, h))
    return pl.pallas_call(
        body, name="attn_bwd", grid=(NH,),
        in_specs=[hs, hs, hs, hs, pl.BlockSpec((None, t, 1), lambda h: (h, 0, 0)), hs],
        out_specs=[hs, hs, hs],
        out_shape=[jax.ShapeDtypeStruct((t, NH * LANE), F32)] * 3,
        scratch_shapes=[pltpu.VMEM((t, 1), F32)],
        compiler_params=_cp("parallel"),
    )(q, k, v, o, lse, do)


def _cumsum_rows(x):
    n = x.shape[0]
    rows = lax.broadcasted_iota(jnp.int32, (n, 1), 0)
    d = 1
    while d < n:
        x = x + jnp.where(rows >= d, pltpu.roll(x, d, 0), 0.0)
        d *= 2
    return x


def _revcumsum_rows(x):
    n = x.shape[0]
    rows = lax.broadcasted_iota(jnp.int32, (n, 1), 0)
    d = 1
    while d < n:
        x = x + jnp.where(rows < n - d, pltpu.roll(x, n - d, 0), 0.0)
        d *= 2
    return x


def _hgrn_gates(f, lb):
    sneg = _sigmoid(-f)
    kk = (1.0 - lb) * sneg
    lf = jnp.log1p(-jnp.minimum(kk, GATE_CLAMP))
    return kk, lf, sneg


def _silu(x):
    return x * _sigmoid(x)


def _dsilu(x):
    s = _sigmoid(x)
    return s * (1.0 + x * (1.0 - s))


def _hgrn_intra(q, kk, b):
    parts = []
    for blk in range(CHUNK // SUB):
        lo = blk * SUB
        ref = jnp.zeros((1, LANE), F32) if blk == 0 else b[lo - 1:lo, :]
        eq = jnp.exp(b[lo:lo + SUB, :] - ref)
        ek = jnp.exp(jnp.minimum(ref - b, EXP_CLIP))
        parts.append((q[lo:lo + SUB, :] * eq, kk * ek, eq, ek))
    return parts


def _chunk_causal():
    return lax.broadcasted_iota(jnp.int32, (CHUNK, CHUNK), 1) <= lax.broadcasted_iota(jnp.int32, (CHUNK, CHUNK), 0)


def _hgrn_fwd(h4, lb):
    t = h4.shape[0]
    nc = t // CHUNK

    def body(q_ref, f_ref, i_ref, lb_ref, o_ref, s_ref, st):
        st[...] = jnp.zeros_like(st)
        causal = _chunk_causal()

        def chunk(c, carry):
            r0 = pl.multiple_of(c * CHUNK, CHUNK)
            q = q_ref[pl.ds(r0, CHUNK), :]
            kk, lf, _ = _hgrn_gates(f_ref[pl.ds(r0, CHUNK), :], lb_ref[...])
            v = _silu(i_ref[pl.ds(r0, CHUNK), :])
            b = _cumsum_rows(lf)
            s_prev = st[...]
            s_ref[c] = s_prev
            o = _hmm_nt(q * jnp.exp(b), s_prev)
            a = jnp.concatenate([_hmm_nt(qs, ks) for qs, ks, _, _ in _hgrn_intra(q, kk, b)], axis=0)
            a = jnp.where(causal, a, 0.0)
            o_ref[pl.ds(r0, CHUNK), :] = o + _hmm(a, v)
            bl = b[CHUNK - 1:CHUNK, :]
            st[...] = s_prev * jnp.exp(bl) + _hmm_tn(v, kk * jnp.exp(bl - b))
            return carry
        lax.fori_loop(0, nc, chunk, 0)

    def col(j):
        return pl.BlockSpec((t, LANE), lambda h: (0, HH * j + h))
    return pl.pallas_call(
        body, name="hgrn_fwd", grid=(HH,),
        in_specs=[col(0), col(1), col(2), pl.BlockSpec((1, LANE), lambda h: (0, h))],
        out_specs=[pl.BlockSpec((t, LANE), lambda h: (0, h)),
                   pl.BlockSpec((None, nc, LANE, LANE), lambda h: (h, 0, 0, 0))],
        out_shape=[jax.ShapeDtypeStruct((t, HH * LANE), F32), jax.ShapeDtypeStruct((HH, nc, LANE, LANE), F32)],
        scratch_shapes=[pltpu.VMEM((LANE, LANE), F32)],
        compiler_params=_cp("parallel"),
    )(h4, h4, h4, lb)


def _hgrn_bwd(h4, lb, do, states):
    t = h4.shape[0]
    nc = t // CHUNK

    def body(q_ref, f_ref, i_ref, lb_ref, do_ref, s_ref, dq_ref, df_ref, di_ref, dlb_ref, dst, carry):
        dst[...] = jnp.zeros_like(dst)
        carry[...] = jnp.zeros_like(carry)
        dlb_ref[...] = jnp.zeros_like(dlb_ref)
        causal = _chunk_causal()

        def chunk(cc, cr):
            c = nc - 1 - cc
            r0 = pl.multiple_of(c * CHUNK, CHUNK)
            q = q_ref[pl.ds(r0, CHUNK), :]
            lbv = lb_ref[...]
            kk, lf, sneg = _hgrn_gates(f_ref[pl.ds(r0, CHUNK), :], lbv)
            iv = i_ref[pl.ds(r0, CHUNK), :]
            v = _silu(iv)
            b = _cumsum_rows(lf)
            s_prev = s_ref[c]
            ds_new = dst[...]
            dob = do_ref[pl.ds(r0, CHUNK), :]
            e = jnp.exp(b)
            qe = q * e
            bl = b[CHUNK - 1:CHUNK, :]
            etail = jnp.exp(bl - b)
            kd = kk * etail
            dq_inter = _hmm(dob, s_prev) * e
            dv = _hmm_nt(kd, ds_new)
            dkk = _hmm(v, ds_new) * etail
            parts = _hgrn_intra(q, kk, b)
            a = jnp.where(causal, jnp.concatenate([_hmm_nt(qs, ks) for qs, ks, _, _ in parts], axis=0), 0.0)
            da = jnp.where(causal, _hmm_nt(dob, v), 0.0)
            dv = dv + _hmm_tn(a, dob)
            dq_rows = []
            for blk, (qs, ks, eq, ek) in enumerate(parts):
                da_blk = da[blk * SUB:(blk + 1) * SUB, :]
                dq_rows.append(_hmm(da_blk, ks) * eq)
                dkk = dkk + _hmm_tn(da_blk, qs) * ek
            dq = dq_inter + jnp.concatenate(dq_rows, axis=0)
            dst[...] = ds_new * jnp.exp(bl) + _hmm_tn(dob, qe)
            g = q * dq - kk * dkk
            dlf = _revcumsum_rows(g) + carry[0:1, :]
            carry[0:1, :] += jnp.sum(g, axis=0, keepdims=True)
            dkk_tot = dkk + dlf * jnp.where(kk < GATE_CLAMP, -1.0 / (1.0 - kk), 0.0)
            dq_ref[pl.ds(r0, CHUNK), :] = dq
            df_ref[pl.ds(r0, CHUNK), :] = dkk_tot * (1.0 - lbv) * (-sneg * (1.0 - sneg))
            di_ref[pl.ds(r0, CHUNK), :] = dv * _dsilu(iv)
            dlb_ref[...] += _colsum8(dkk_tot * (-sneg))
            return cr
        lax.fori_loop(0, nc, chunk, 0)

    def col(j):
        return pl.BlockSpec((t, LANE), lambda h: (0, HH * j + h))
    hs = pl.BlockSpec((t, LANE), lambda h: (0, h))
    return pl.pallas_call(
        body, name="hgrn_bwd", grid=(HH,),
        in_specs=[col(0), col(1), col(2), pl.BlockSpec((1, LANE), lambda h: (0, h)), hs,
                  pl.BlockSpec((None, nc, LANE, LANE), lambda h: (h, 0, 0, 0))],
        out_specs=[hs, hs, hs, pl.BlockSpec((8, LANE), lambda h: (0, h))],
        out_shape=[jax.ShapeDtypeStruct((t, HH * LANE), F32)] * 3 + [jax.ShapeDtypeStruct((8, HH * LANE), F32)],
        scratch_shapes=[pltpu.VMEM((LANE, LANE), F32), pltpu.VMEM((8, LANE), F32)],
        compiler_params=_cp("parallel"),
    )(h4, h4, h4, lb, do, states)


def _ln_fwd(z, g, b):
    mu = jnp.mean(z, axis=-1, keepdims=True)
    zc = z - mu
    rstd = lax.rsqrt(jnp.mean(zc * zc, axis=-1, keepdims=True) + EPS)
    zh = zc * rstd
    return zh * g + b, zh, rstd


def _mix_fwd(x, z, o_att, o_h, h4, gates, lng, lnb, wco, wao, ng, who, wout, t_end):
    t = x.shape[0]
    tm = _tile(t, 192)

    def body(x_ref, z_ref, oa_ref, oh_ref, hg_ref, gt_ref, lng_ref, lnb_ref, wco_ref, wao_ref, ng_ref, who_ref,
             wout_ref, x1_ref, mix_ref, ca_ref, oc_ref, ya_ref, yb_ref, yc_ref):
        i = pl.program_id(0)
        ln, _, _ = _ln_fwd(z_ref[...], lng_ref[...], lnb_ref[...])
        ca = _silu(ln).astype(_MM)
        ca_ref[...] = ca
        ya = jnp.dot(ca, wco_ref[...], preferred_element_type=F32)
        yb = _mm(oa_ref[...], wao_ref[...])
        hg = hg_ref[...]
        for h in range(HH):
            sl = slice(LANE * h, LANE * (h + 1))
            oh = oh_ref[:, sl]
            oc_ref[:, sl] = (oh * _rstd(oh) * ng_ref[:, sl] * _silu(hg[:, sl])).astype(_MM)
        yc = jnp.dot(oc_ref[...], who_ref[...], preferred_element_type=F32)
        ya_ref[...] = ya
        yb_ref[...] = yb
        yc_ref[...] = yc
        mix = (_sigmoid(gt_ref[:, 0:D]) * ya + _sigmoid(gt_ref[:, D:2 * D]) * yb
               + _sigmoid(gt_ref[:, 2 * D:3 * D]) * yc).astype(_MM)
        mix_ref[...] = mix
        x1_ref[...] = x_ref[...] + _valid_rows(i, tm, t_end) * jnp.dot(mix, wout_ref[...],
                                                                       preferred_element_type=F32)

    hd = NH * LANE
    return pl.pallas_call(
        body, name="mix_fwd", grid=(t // tm,),
        in_specs=[_row(tm, D), _row(tm, CONV_DIM), _row(tm, hd), _row(tm, 512), _row(tm, 512, 3), _row(tm, 3 * D),
                  _full((1, 512)), _full((1, 512)), _full((512, D)), _full((hd, D)), _full((1, 512)),
                  _full((512, D)), _full((D, D))],
        out_specs=[_row(tm, D), _row(tm, D), _row(tm, 512), _row(tm, 512), _row(tm, D), _row(tm, D), _row(tm, D)],
        out_shape=[jax.ShapeDtypeStruct((t, D), F32), jax.ShapeDtypeStruct((t, D), _MM),
                   jax.ShapeDtypeStruct((t, 512), _MM), jax.ShapeDtypeStruct((t, 512), _MM),
                   jax.ShapeDtypeStruct((t, D), F32), jax.ShapeDtypeStruct((t, D), F32),
                   jax.ShapeDtypeStruct((t, D), F32)],
        compiler_params=_cp("parallel"),
    )(x, z, o_att, o_h, h4, gates, lng, lnb, wco, wao, ng, who, wout)


def _mix_bwd(dx1, ya, yb, yc, gates, z, o_h, h4, lng, lnb, ng, woutt, wcot, waot, whot):
    t = dx1.shape[0]
    tm = _tile(t, 192)
    hd = NH * LANE

    def body(dx1_ref, ya_ref, yb_ref, yc_ref, gt_ref, z_ref, oh_ref, hg_ref, lng_ref, lnb_ref, ng_ref,
             woutt_ref, wcot_ref, waot_ref, whot_ref,
             dgt_ref, dya_ref, dyb_ref, dyc_ref, dz_ref, doa_ref, doh_ref, dhg_ref,
             dlng_ref, dlnb_ref, dcb_ref, dng_ref):
        i = pl.program_id(0)
        dmix = _mm(dx1_ref[...], woutt_ref[...])
        dys = []
        for j, y_ref in enumerate((ya_ref, yb_ref, yc_ref)):
            sg = _sigmoid(gt_ref[:, j * D:(j + 1) * D])
            dgt_ref[:, j * D:(j + 1) * D] = (dmix * y_ref[...] * sg * (1.0 - sg)).astype(_MM)
            dys.append((dmix * sg).astype(_MM))
        dya_ref[...], dyb_ref[...], dyc_ref[...] = dys
        dca = jnp.dot(dys[0], wcot_ref[...], preferred_element_type=F32)
        ln, zh, rstd = _ln_fwd(z_ref[...], lng_ref[...], lnb_ref[...])
        dln = dca * _dsilu(ln)
        dzh = dln * lng_ref[...]
        dz = rstd * (dzh - jnp.mean(dzh, axis=-1, keepdims=True)
                     - zh * jnp.mean(dzh * zh, axis=-1, keepdims=True))
        dz_ref[...] = dz
        doa_ref[...] = jnp.dot(dys[1], waot_ref[...], preferred_element_type=F32)
        doc = jnp.dot(dys[2], whot_ref[...], preferred_element_type=F32)
        hg = hg_ref[...]
        dng_rows = []
        for h in range(HH):
            sl = slice(LANE * h, LANE * (h + 1))
            oh = oh_ref[:, sl]
            r = _rstd(oh)
            don = doc[:, sl] * _silu(hg[:, sl])
            dhg_ref[:, sl] = (doc[:, sl] * oh * r * ng_ref[:, sl] * _dsilu(hg[:, sl])).astype(_MM)
            doh, gn = _rms_bwd(don, oh, r, ng_ref[:, sl])
            doh_ref[:, sl] = doh
            dng_rows.append(_colsum8(gn))

        @pl.when(i == 0)
        def _():
            dlng_ref[...] = jnp.zeros_like(dlng_ref)
            dlnb_ref[...] = jnp.zeros_like(dlnb_ref)
            dcb_ref[...] = jnp.zeros_like(dcb_ref)
            dng_ref[...] = jnp.zeros_like(dng_ref)
        dlng_ref[...] += _colsum8(dln * zh)
        dlnb_ref[...] += _colsum8(dln)
        dcb_ref[...] += _colsum8(dz)
        dng_ref[...] += jnp.concatenate(dng_rows, axis=1)

    return pl.pallas_call(
        body, name="mix_bwd", grid=(t // tm,),
        in_specs=[_row(tm, D), _row(tm, D), _row(tm, D), _row(tm, D), _row(tm, 3 * D), _row(tm, 512), _row(tm, 512),
                  _row(tm, 512, 3), _full((1, 512)), _full((1, 512)), _full((1, 512)),
                  _full((D, D)), _full((D, 512)), _full((D, hd)), _full((D, 512))],
        out_specs=[_row(tm, 3 * D), _row(tm, D), _row(tm, D), _row(tm, D), _row(tm, 512), _row(tm, hd),
                   _row(tm, 512), _row(tm, 512), _full((8, 512)), _full((8, 512)), _full((8, 512)), _full((8, 512))],
        out_shape=[jax.ShapeDtypeStruct((t, 3 * D), _MM), jax.ShapeDtypeStruct((t, D), _MM),
                   jax.ShapeDtypeStruct((t, D), _MM), jax.ShapeDtypeStruct((t, D), _MM),
                   jax.ShapeDtypeStruct((t, 512), F32), jax.ShapeDtypeStruct((t, hd), F32),
                   jax.ShapeDtypeStruct((t, 512), F32), jax.ShapeDtypeStruct((t, 512), _MM)]
        + [jax.ShapeDtypeStruct((8, 512), F32)] * 4,
        compiler_params=_cp("arbitrary"),
    )(dx1, ya, yb, yc, gates, z, o_h, h4, lng, lnb, ng, woutt, wcot, waot, whot)


D_FF = 4096


def _ffn_fwd(x1, g2, w1, w2):
    t = x1.shape[0]
    tm = _tile(t, 192)

    def body(x1_ref, g_ref, w1_ref, w2_ref, x2_ref, p_ref):
        xv = x1_ref[...]
        h2 = (xv * _rstd(xv) * g_ref[...]).astype(_MM)
        p = jnp.dot(h2, w1_ref[...], preferred_element_type=F32)
        p_ref[...] = p
        r = jnp.maximum(p, 0.0)
        x2_ref[...] = xv + jnp.dot((r * r).astype(_MM), w2_ref[...], preferred_element_type=F32)

    return pl.pallas_call(
        body, name="ffn_fwd", grid=(t // tm,),
        in_specs=[_row(tm, D), _full((1, D)), _full((D, D_FF)), _full((D_FF, D))],
        out_specs=[_row(tm, D), _row(tm, D_FF)],
        out_shape=[jax.ShapeDtypeStruct((t, D), F32), jax.ShapeDtypeStruct((t, D_FF), F32)],
        compiler_params=_cp("parallel"),
    )(x1, g2, w1, w2)


def _ffn_bwd(dx2, x1, p, g2, w1t, w2t):
    t = x1.shape[0]
    tm = _tile(t, 192)

    def body(dx2_ref, x1_ref, p_ref, g_ref, w1t_ref, w2t_ref, dx1_ref, h2_ref, act_ref, dp_ref, dg_ref):
        i = pl.program_id(0)
        xv = x1_ref[...]
        rstd = _rstd(xv)
        h2_ref[...] = (xv * rstd * g_ref[...]).astype(_MM)
        r = jnp.maximum(p_ref[...], 0.0)
        act_ref[...] = (r * r).astype(_MM)
        dx2 = dx2_ref[...]
        da = _mm(dx2, w2t_ref[...])
        dp = (2.0 * r * da).astype(_MM)
        dp_ref[...] = dp
        dh2 = jnp.dot(dp, w1t_ref[...], preferred_element_type=F32)
        dxn, dgrow = _rms_bwd(dh2, xv, rstd, g_ref[...])
        dx1_ref[...] = dx2 + dxn

        @pl.when(i == 0)
        def _():
            dg_ref[...] = jnp.zeros_like(dg_ref)
        dg_ref[...] += _colsum8(dgrow)

    return pl.pallas_call(
        body, name="ffn_bwd", grid=(t // tm,),
        in_specs=[_row(tm, D), _row(tm, D), _row(tm, D_FF), _full((1, D)), _full((D_FF, D)), _full((D, D_FF))],
        out_specs=[_row(tm, D), _row(tm, D), _row(tm, D_FF), _row(tm, D_FF), _full((8, D))],
        out_shape=[jax.ShapeDtypeStruct((t, D), F32), jax.ShapeDtypeStruct((t, D), _MM),
                   jax.ShapeDtypeStruct((t, D_FF), _MM), jax.ShapeDtypeStruct((t, D_FF), _MM),
                   jax.ShapeDtypeStruct((8, D), F32)],
        compiler_params=_cp("arbitrary"),
    )(dx2, x1, p, g2, w1t, w2t)


def _wgrad(a, b, name, chips=1):
    t, ka = a.shape
    nb = b.shape[1]
    tm = _tile(t, 384)
    cs = nb // chips
    tn = max(d for d in range(LANE, cs + 1, LANE) if cs % d == 0 and (ka * d * 4 <= 16 * 1024 * 1024 or d == LANE))
    per = cs // tn

    def body(a_ref, b_ref, o_ref):
        @pl.when(pl.program_id(1) == 0)
        def _():
            o_ref[...] = jnp.zeros_like(o_ref)
        o_ref[...] += _mm_tn(a_ref[...], b_ref[...])

    if chips == 1:
        out_spec = pl.BlockSpec((ka, tn), lambda n, i: (0, n))
        out_shape = jax.ShapeDtypeStruct((ka, nb), F32)
    else:
        out_spec = pl.BlockSpec((None, ka, tn), lambda n, i: (n // per, 0, n % per))
        out_shape = jax.ShapeDtypeStruct((chips, ka, cs), F32)
    return pl.pallas_call(
        body, name="wgrad_" + name, grid=(nb // tn, t // tm),
        in_specs=[pl.BlockSpec((tm, ka), lambda n, i: (i, 0)), pl.BlockSpec((tm, tn), lambda n, i: (i, n))],
        out_specs=out_spec, out_shape=out_shape,
        compiler_params=_cp("parallel", "arbitrary"),
    )(a, b)


def _loss_head(y, target, t_end):
    t = y.shape[0]
    tm = _tile(t, 384)

    def body(y_ref, tg_ref, dy_ref, l_ref):
        i = pl.program_id(0)
        r = i * tm + lax.broadcasted_iota(jnp.int32, (tm, 1), 0)
        real = ((r >= ROW0) & (r < t_end)).astype(F32)
        diff = (y_ref[...] - tg_ref[...]) * real
        dy_ref[...] = diff * (1.0 / D)

        @pl.when(i == 0)
        def _():
            l_ref[...] = jnp.zeros_like(l_ref)
        sq = _colsum8(diff * diff)
        part = sq[:, 0:LANE]
        for j in range(1, D // LANE):
            part = part + sq[:, j * LANE:(j + 1) * LANE]
        l_ref[...] += part * (0.5 / D)

    return pl.pallas_call(
        body, name="loss_head", grid=(t // tm,),
        in_specs=[_row(tm, D), _row(tm, D)],
        out_specs=[_row(tm, D), _full((8, LANE))],
        out_shape=[jax.ShapeDtypeStruct((t, D), F32), jax.ShapeDtypeStruct((8, LANE), F32)],
        compiler_params=_cp("arbitrary"),
    )(y, target)


def _lower_bounds_fwd(logits):
    depth, n = logits.shape

    def body(l_ref, lb_ref):
        lg = l_ref[...]
        m = jnp.max(lg, axis=0, keepdims=True)
        e = jnp.exp(lg - m)
        p = e / jnp.sum(e, axis=0, keepdims=True)
        acc = jnp.zeros((1, n), F32)
        for l in range(depth):
            if l > 0:
                acc = acc + p[l:l + 1, :]
            lb_ref[l:l + 1, :] = acc

    return pl.pallas_call(body, name="lower_bounds_fwd", out_shape=jax.ShapeDtypeStruct((depth, n), F32))(logits)


def _lower_bounds_bwd(logits, dlb):
    depth, n = logits.shape

    def body(l_ref, dlb_ref, dl_ref):
        lg = l_ref[...]
        m = jnp.max(lg, axis=0, keepdims=True)
        e = jnp.exp(lg - m)
        p = e / jnp.sum(e, axis=0, keepdims=True)
        dps = [jnp.zeros((1, n), F32)]
        for j in range(1, depth):
            acc = jnp.zeros((1, n), F32)
            for l in range(j, depth):
                acc = acc + dlb_ref[l:l + 1, :]
            dps.append(acc)
        dot = jnp.zeros((1, n), F32)
        for j in range(depth):
            dot = dot + p[j:j + 1, :] * dps[j]
        for j in range(depth):
            dl_ref[j:j + 1, :] = p[j:j + 1, :] * (dps[j] - dot)

    return pl.pallas_call(body, name="lower_bounds_bwd", out_shape=jax.ShapeDtypeStruct((depth, n), F32))(logits, dlb)


def _ew_tile(rows, cols, n_arrays):
    cap = max(16, (24 * 1024 * 1024) // (8 * n_arrays * cols))
    best = None
    for t in range(16, rows + 1, 16):
        if rows % t == 0 and t <= cap:
            best = t
    return rows if best is None else best


def _adamw_math(w, g, m, v):
    mn = ADAM_B1 * m + (1.0 - ADAM_B1) * g
    vn = ADAM_B2 * v + (1.0 - ADAM_B2) * (g * g)
    m_hat = mn / (1.0 - ADAM_B1 ** ADAM_STEP)
    v_hat = vn / (1.0 - ADAM_B2 ** ADAM_STEP)
    return -ADAM_LR * (m_hat / (jnp.sqrt(v_hat) + ADAM_EPS) + ADAM_WD * w), mn, vn


def _adamw_layers(w, m, v, g_mine, g_sibling, name):
    _, r, c_ = w.shape
    tr = _ew_tile(r, c_, 9)

    def body(w_ref, m_ref, v_ref, gm_ref, gs_ref, g_ref, d_ref, mo_ref, vo_ref):
        mine = pl.program_id(0) == lax.axis_index("c")
        g = jnp.where(mine, gm_ref[...], gs_ref[...])
        g_ref[...] = g
        d_ref[...], mo_ref[...], vo_ref[...] = _adamw_math(w_ref[...], g, m_ref[...], v_ref[...])

    lay = pl.BlockSpec((None, tr, c_), lambda l, i: (l, i, 0))
    flat = pl.BlockSpec((tr, c_), lambda l, i: (i, 0))
    return pl.pallas_call(
        body, name="adamw_" + name, grid=(2, r // tr),
        in_specs=[lay, lay, lay, flat, flat], out_specs=[lay] * 4,
        out_shape=[jax.ShapeDtypeStruct(w.shape, F32)] * 4,
        compiler_params=_cp("parallel", "parallel"),
    )(w, m, v, g_mine, g_sibling)


def _adamw(w, g, m, v, name):
    rows, cols = w.shape
    tr = _ew_tile(rows, cols, 7)

    def body(w_ref, g_ref, m_ref, v_ref, d_ref, mo_ref, vo_ref):
        d_ref[...], mo_ref[...], vo_ref[...] = _adamw_math(w_ref[...], g_ref[...], m_ref[...], v_ref[...])

    spec = pl.BlockSpec((tr, cols), lambda i: (i, 0))
    return pl.pallas_call(
        body, name="adamw_" + name, grid=(rows // tr,),
        in_specs=[spec] * 4, out_specs=[spec] * 3,
        out_shape=[jax.ShapeDtypeStruct((rows, cols), F32)] * 3,
        compiler_params=_cp("parallel"),
    )(w, g, m, v)


DEPTH = 2
BIG_SHAPES = {"w_in": ((1024, 6560), 1), "w_conv_out": ((512, 1024), 1), "w_uq": ((256, 768), 1),
              "w_ukv": ((128, 1024), 1), "w_attn_out": ((512, 1024), 1), "w_hgrn_out": ((512, 1024), 1),
              "w_out": ((1024, 1024), 0), "w_ff1": ((1024, 4096), 1), "w_ff2": ((4096, 1024), 0)}
BIG = tuple(BIG_SHAPES)
SMALL_SIZES = {"norm1_g": 1024, "conv_b": 512, "conv_ln_g": 512, "conv_ln_b": 512, "q_a_norm_g": 256,
               "kv_a_norm_g": 128, "q_norm_g": 96, "k_norm_g": 96, "hgrn_lb_logits": 512, "hgrn_norm_g": 512,
               "norm2_g": 1024}
SMALL = tuple(SMALL_SIZES)
W_IN_COLS = 6560
W_IN_SHARD = W_IN_COLS // 4
W_IN_SEGS = ((0, 1024, SEG_AG[0]), (1024, 1280, SEG_CQ[0]), (1280, 1408, SEG_CKV[0]), (1408, 1440, SEG_KR[0] + 64),
             (1440, 3488, SEG_H4[0]), (3488, 6560, SEG_GATES[0]))


def _pad_heads(w, nh, used, axis):
    shp = w.shape
    w = w.reshape(shp[:axis] + (nh, used) + shp[axis + 1:])
    pad = [(0, 0)] * w.ndim
    pad[axis + 1] = (0, LANE - used)
    w = jnp.pad(w, pad)
    return w.reshape(shp[:axis] + (nh * LANE,) + shp[axis + 1:])


def _unpad_heads(w, nh, used, axis):
    shp = w.shape
    w = w.reshape(shp[:axis] + (nh, LANE) + shp[axis + 1:])
    w = lax.slice_in_dim(w, 0, used, axis=axis + 1)
    return w.reshape(shp[:axis] + (nh * used,) + shp[axis + 1:])


def _w_in_from_chips(p4):
    def orig(a, b):
        out = []
        while a < b:
            s = a // W_IN_SHARD
            e = min(b, (s + 1) * W_IN_SHARD)
            out.append(p4[s][:, a - W_IN_SHARD * s:e - W_IN_SHARD * s])
            a = e
        return out
    zc = lambda n: jnp.zeros((D, n), p4[0].dtype)
    parts = (orig(3488, 6560) + orig(0, 1024) + orig(1440, 3488) + orig(1024, 1280) + orig(1280, 1408)
             + [zc(64)] + orig(1408, 1440) + [zc(32)])
    return jnp.concatenate(parts, axis=1)


def _w_in_grad_to_chips(dw):
    chips = []
    for s in range(4):
        a, b = W_IN_SHARD * s, W_IN_SHARD * (s + 1)
        parts = []
        for o0, o1, p0 in W_IN_SEGS:
            lo, hi = max(a, o0), min(b, o1)
            if lo < hi:
                parts.append(dw[:, p0 + lo - o0:p0 + hi - o0])
        chips.append(jnp.concatenate(parts, axis=1))
    return jnp.stack(chips)


def _cat_chips(p4, axis):
    return jnp.concatenate([p4[s] for s in range(4)], axis=axis)


def _prep_layer(pieces, small, l):
    mm = lambda a: a.astype(_MM)
    pc = lambda k: [mm(pieces[k][s][l]) for s in range(4)]
    w_in_p = _w_in_from_chips(pc("w_in"))
    wuq = jnp.concatenate([_pad_heads(pc("w_uq")[s], 2, QK_DIM, 1) for s in range(4)], axis=1)
    wukv = _cat_chips(pc("w_ukv"), 1).reshape(128, NH, 128)
    wk = _pad_heads(wukv[:, :, :64].reshape(128, NH * 64), NH, 64, 1)
    wv = _pad_heads(wukv[:, :, 64:].reshape(128, NH * 64), NH, 64, 1)
    wao = _pad_heads(_cat_chips(pc("w_attn_out"), 1), NH, 64, 0)
    row = lambda a: a.astype(F32).reshape(1, -1)
    p = dict(
        w_in=w_in_p, w_in_t=w_in_p.T, wuq=wuq, wuq_t=wuq.T, wk=wk, wk_t=wk.T, wv=wv, wv_t=wv.T,
        wao=wao, wao_t=wao.T, wco=_cat_chips(pc("w_conv_out"), 1), who=_cat_chips(pc("w_hgrn_out"), 1),
        wout=_cat_chips(pc("w_out"), 0), w1=_cat_chips(pc("w_ff1"), 1), w2=_cat_chips(pc("w_ff2"), 0),
        g1=row(small["norm1_g"][l]), g2=row(small["norm2_g"][l]),
        cw=jnp.pad(small["conv_w"][l].astype(F32), ((0, 1), (0, 0))), cb=row(small["conv_b"][l]),
        lng=row(small["conv_ln_g"][l]), lnb=row(small["conv_ln_b"][l]),
        qag=row(small["q_a_norm_g"][l]), kvag=row(small["kv_a_norm_g"][l]),
        qng=jnp.pad(row(small["q_norm_g"][l]), ((0, 0), (0, LANE - QK_DIM))),
        kng=jnp.pad(row(small["k_norm_g"][l]), ((0, 0), (0, LANE - QK_DIM))),
        ng=row(small["hgrn_norm_g"][l]),
    )
    for k in ("wco", "who", "wout", "w1", "w2"):
        p[k + "_t"] = p[k].T
    return p


def _rope_tables(t):
    pos = (jnp.arange(t, dtype=jnp.int32) - FRONT).astype(F32)
    inv_freq = 10000.0 ** (-jnp.arange(16, dtype=F32) / 16)
    ang = pos[:, None] * inv_freq[None, :]
    cos, sin = jnp.cos(ang), jnp.sin(ang)
    one = jnp.ones((t, 64), F32)
    z16, z32, z64 = jnp.zeros((t, 16), F32), jnp.zeros((t, 32), F32), jnp.zeros((t, 64), F32)
    c = jnp.concatenate([one, cos, cos, z32], axis=1)
    s1 = jnp.concatenate([z64, -sin, z16, z32], axis=1)
    s2 = jnp.concatenate([z64, z16, sin, z32], axis=1)
    return c, s1, s2


def _layer_fwd(x, p, lb, rope, t_end):
    gates, ag, h4, cq, ckv, kr, hb = _in_proj_fwd(x, p["g1"], p["w_in"])
    z = _conv_fwd(ag, p["cw"], p["cb"])
    q, k, v, cqn, ckvn = _mla_pre_fwd(cq, ckv, kr, p["qag"], p["wuq"], p["kvag"], p["wk"], p["wv"], p["qng"],
                                      p["kng"], *rope)
    o_att, lse = _attn_fwd(q, k, v)
    o_h, states = _hgrn_fwd(h4, lb)
    x1, mix, ca, oc, ya, yb, yc = _mix_fwd(x, z, o_att, o_h, h4, gates, p["lng"], p["lnb"], p["wco"], p["wao"],
                                           p["ng"], p["who"], p["wout"], t_end)
    x2, pre = _ffn_fwd(x1, p["g2"], p["w1"], p["w2"])
    saved = dict(x=x, gates=gates, ag=ag, h4=h4, cq=cq, ckv=ckv, kr=kr, hb=hb, z=z, q=q, k=k, v=v, cqn=cqn,
                 ckvn=ckvn, o_att=o_att, lse=lse, o_h=o_h, states=states, x1=x1, mix=mix, ca=ca, oc=oc,
                 ya=ya, yb=yb, yc=yc, pre=pre)
    return x2, saved


def _layer_bwd(dx2, s, p, lb, rope, t_end):
    dx1, h2, act, dp, dg2 = _ffn_bwd(dx2, s["x1"], s["pre"], p["g2"], p["w1_t"], p["w2_t"])
    g = {"w_ff1": _wgrad(h2, dp, "ff1", 4), "w_ff2": _wgrad(act, dx2, "ff2").reshape(4, D_FF // 4, D),
         "norm2_g": dg2.sum(0)}
    (dgt, dya, dyb, dyc, dz, doa, doh, dhg, dlng, dlnb, dcb, dng) = _mix_bwd(
        dx1, s["ya"], s["yb"], s["yc"], s["gates"], s["z"], s["o_h"], s["h4"], p["lng"], p["lnb"], p["ng"],
        p["wout_t"], p["wco_t"], p["wao_t"], p["who_t"])
    g["w_out"] = _wgrad(s["mix"], dx1, "out").reshape(4, D // 4, D)
    g["w_conv_out"] = _wgrad(s["ca"], dya, "conv_out", 4)
    g["w_attn_out"] = _unpad_heads(_wgrad(s["o_att"], dyb, "attn_out", 4), NH, 64, 1)
    g["w_hgrn_out"] = _wgrad(s["oc"], dyc, "hgrn_out", 4)
    g["conv_ln_g"], g["conv_ln_b"], g["conv_b"], g["hgrn_norm_g"] = dlng.sum(0), dlnb.sum(0), dcb.sum(0), dng.sum(0)
    da, dg, dcw = _conv_bwd(s["ag"], p["cw"], dz)
    g["conv_w"] = dcw[:CONV_K]
    dq, dk, dv = _attn_bwd(s["q"], s["k"], s["v"], s["o_att"], s["lse"], doa)
    dcq, dckv, dkr, dqraw, dkraw, dqag, dkvag, dqng, dkng = _mla_pre_bwd(
        dq, dk, dv, s["cq"], s["ckv"], s["kr"], p["qag"], p["wuq"], p["wuq_t"], p["kvag"], p["wk"], p["wk_t"],
        p["wv_t"], p["qng"], p["kng"], *rope)
    g["w_uq"] = _unpad_heads(_wgrad(s["cqn"], dqraw, "uq", 4), 2, QK_DIM, 2)
    dwk = _unpad_heads(_wgrad(s["ckvn"], dkraw, "uk"), NH, 64, 1).reshape(128, NH, 64)
    dwv = _unpad_heads(_wgrad(s["ckvn"], dv, "uv"), NH, 64, 1).reshape(128, NH, 64)
    g["w_ukv"] = jnp.concatenate([dwk, dwv], axis=2).reshape(128, 4, 256).transpose(1, 0, 2)
    g["q_a_norm_g"], g["kv_a_norm_g"] = dqag.sum(0), dkvag.sum(0)
    g["q_norm_g"], g["k_norm_g"] = dqng.sum(0)[:QK_DIM], dkng.sum(0)[:QK_DIM]
    dhq, dhf, dhi, dlb = _hgrn_bwd(s["h4"], lb, doh, s["states"])
    mm = lambda a: a.astype(_MM)
    du = jnp.concatenate([dgt, mm(da), mm(dg), mm(dhq), mm(dhf), mm(dhi), dhg, dcq, dckv, dkr], axis=1)
    dx, dg1 = _in_proj_bwd(du, s["x"], dx1, p["g1"], p["w_in_t"], t_end)
    g["norm1_g"] = dg1.sum(0)
    g["w_in"] = _w_in_grad_to_chips(_wgrad(s["hb"], du, "in"))
    return dx, g, dlb.sum(0)


def _device_step(x, target, pieces, small):
    s_real = x.shape[0]
    t_end = ROW0 + s_real
    t = -(-t_end // LANE) * LANE
    zrow = lambda n: jnp.zeros((n, D), F32)
    xp = jnp.concatenate([zrow(FRONT), small["meta"].astype(F32), x, zrow(t - t_end)], axis=0)
    tp = jnp.concatenate([zrow(ROW0), target, zrow(t - t_end)], axis=0)
    rope = _rope_tables(t)
    logits = small["hgrn_lb_logits"].astype(F32)
    lbs = _lower_bounds_fwd(logits)
    prm = [_prep_layer(pieces, small, l) for l in range(DEPTH)]
    saved = []
    h = xp
    for l in range(DEPTH):
        h, sv = _layer_fwd(h, prm[l], lbs[l:l + 1], rope, t_end)
        saved.append(sv)
    dh, lsum = _loss_head(h, tp, t_end)
    loss = jnp.sum(lsum)
    grads = [None] * DEPTH
    dlbs = [None] * DEPTH
    for l in reversed(range(DEPTH)):
        dh, grads[l], dlbs[l] = _layer_bwd(dh, saved[l], prm[l], lbs[l:l + 1], rope, t_end)
    dlogits = _lower_bounds_bwd(logits, jnp.stack(dlbs))
    for l in range(DEPTH):
        grads[l]["hgrn_lb_logits"] = dlogits[l]
    return loss, dh[ROW0:t_end], grads, dh[FRONT:ROW0]


MESH = pl.DeviceIdType.MESH
_ANY = pl.BlockSpec(memory_space=pl.ANY)
SMALL_ROWS = 64
SMALL_LEN = SMALL_ROWS * 1024


def _mesh_pos():
    return lax.axis_index("x"), lax.axis_index("y"), lax.axis_index("c")


def _other_chips(x, y):
    return [(1 - x, y), (x, 1 - y), (1 - x, 1 - y)]


def _all_gather_weights(shards):
    nw = len(shards)

    def body(*refs):
        srcs, outs = refs[:nw], refs[nw:2 * nw]
        send_sems, recv_sems = refs[2 * nw:]
        x, y, c = _mesh_pos()
        me, sib = (x, y, c), (x, y, 1 - c)
        chips = _other_chips(x, y)

        def copy(w, k, blk, to, from_src=False):
            px, py, pc = blk
            slot = outs[w].at[2 * px + py, pc]
            return pltpu.make_async_remote_copy(
                src_ref=srcs[w].at[c] if from_src else slot, dst_ref=slot,
                send_sem=send_sems.at[w, k], recv_sem=recv_sems.at[w, k], device_id=to, device_id_type=MESH)

        first = []
        for j, ch in enumerate(chips):
            for w in range(nw):
                first.append(copy(w, j, me, (*ch, c), True))
        for cp in first:
            cp.start()
        passed = []
        for j, ch in enumerate(chips):
            for w in range(nw):
                copy(w, j, (*ch, c), me).wait_recv()
                fwd = copy(w, 3 + j, (*ch, c), sib)
                fwd.start()
                passed.append(fwd)
        for j, ch in enumerate(chips):
            for w in range(nw):
                copy(w, 3 + j, (*ch, 1 - c), me).wait_recv()
        for cp in first + passed:
            cp.wait_send()

    return pl.pallas_call(
        body, name="all_gather_weights", in_specs=[_ANY] * nw, out_specs=[_ANY] * nw,
        out_shape=[jax.ShapeDtypeStruct((4,) + a.shape, a.dtype) for a in shards],
        scratch_shapes=[pltpu.SemaphoreType.DMA((nw, 6)), pltpu.SemaphoreType.DMA((nw, 6))],
    )(*shards)


def _sibling_swap(g0, g1):
    nw = len(g0)

    def body(*refs):
        a0, a1, outs = refs[:nw], refs[nw:2 * nw], refs[2 * nw:3 * nw]
        send_sems, recv_sems = refs[3 * nw:]
        x, y, c = _mesh_pos()

        def copy(w, src):
            return pltpu.make_async_remote_copy(src_ref=src, dst_ref=outs[w], send_sem=send_sems.at[w],
                                                recv_sem=recv_sems.at[w], device_id=(x, y, 1 - c),
                                                device_id_type=MESH)

        @pl.when(c == 0)
        def _():
            for w in range(nw):
                copy(w, a1[w]).start()

        @pl.when(c == 1)
        def _():
            for w in range(nw):
                copy(w, a0[w]).start()

        for w in range(nw):
            copy(w, a0[w]).wait()

    return pl.pallas_call(
        body, name="sibling_swap", in_specs=[_ANY] * (2 * nw), out_specs=[_ANY] * nw,
        out_shape=[jax.ShapeDtypeStruct(a.shape, a.dtype) for a in g0],
        scratch_shapes=[pltpu.SemaphoreType.DMA((nw,)), pltpu.SemaphoreType.DMA((nw,))],
    )(*g0, *g1)


def _chip_scatter(parts):
    nw = len(parts)

    def body(*refs):
        srcs, outs = refs[:nw], refs[nw:2 * nw]
        send_sems, recv_sems = refs[2 * nw:]
        x, y, c = _mesh_pos()
        me = 2 * x + y
        chips = _other_chips(x, y)
        cps = []
        for j, (px, py) in enumerate(chips):
            for w in range(nw):
                cps.append(pltpu.make_async_remote_copy(
                    src_ref=srcs[w].at[2 * px + py], dst_ref=outs[w].at[me], send_sem=send_sems.at[w, j],
                    recv_sem=recv_sems.at[w, j], device_id=(px, py, c), device_id_type=MESH))
        for cp in cps:
            cp.start()
        for j, (px, py) in enumerate(chips):
            for w in range(nw):
                pltpu.make_async_remote_copy(
                    src_ref=srcs[w].at[me], dst_ref=outs[w].at[2 * px + py], send_sem=send_sems.at[w, j],
                    recv_sem=recv_sems.at[w, j], device_id=(x, y, c), device_id_type=MESH).wait_recv()
        for cp in cps:
            cp.wait_send()

    return pl.pallas_call(
        body, name="chip_scatter", in_specs=[_ANY] * nw, out_specs=[_ANY] * nw,
        out_shape=[jax.ShapeDtypeStruct(a.shape, a.dtype) for a in parts],
        scratch_shapes=[pltpu.SemaphoreType.DMA((nw, 3)), pltpu.SemaphoreType.DMA((nw, 3))],
    )(*parts)


def _sibling_exchange(reds):
    nw = len(reds)

    def body(*refs):
        srcs, outs = refs[:nw], refs[nw:2 * nw]
        send_sems, recv_sems = refs[2 * nw:]
        x, y, c = _mesh_pos()
        cps = [pltpu.make_async_remote_copy(src_ref=srcs[w], dst_ref=outs[w], send_sem=send_sems.at[w],
                                            recv_sem=recv_sems.at[w], device_id=(x, y, 1 - c), device_id_type=MESH)
               for w in range(nw)]
        for cp in cps:
            cp.start()
        for cp in cps:
            cp.wait()

    return pl.pallas_call(
        body, name="sibling_exchange", in_specs=[_ANY] * nw, out_specs=[_ANY] * nw,
        out_shape=[jax.ShapeDtypeStruct(a.shape, a.dtype) for a in reds],
        scratch_shapes=[pltpu.SemaphoreType.DMA((nw,)), pltpu.SemaphoreType.DMA((nw,))],
    )(*reds)


def _all_reduce_small(v, name):
    rows, cols = v.shape

    def body(v_ref, o_ref, slots, send_sems, recv_sems):
        x, y, c = _mesh_pos()
        me = 4 * x + 2 * y + c
        slots[me] = v_ref[...]
        peers = []
        for rel in range(1, 8):
            fx, fy, fc = (rel >> 2) & 1, (rel >> 1) & 1, rel & 1
            px = 1 - x if fx else x
            py = 1 - y if fy else y
            pc = 1 - c if fc else c
            peers.append((px, py, pc))
        cps = [pltpu.make_async_remote_copy(src_ref=v_ref, dst_ref=slots.at[me], send_sem=send_sems.at[k],
                                            recv_sem=recv_sems.at[k], device_id=peer, device_id_type=MESH)
               for k, peer in enumerate(peers)]
        for cp in cps:
            cp.start()
        for k, (px, py, pc) in enumerate(peers):
            pltpu.make_async_remote_copy(src_ref=v_ref, dst_ref=slots.at[4 * px + 2 * py + pc],
                                         send_sem=send_sems.at[k], recv_sem=recv_sems.at[k], device_id=(x, y, c),
                                         device_id_type=MESH).wait_recv()
        for cp in cps:
            cp.wait_send()
        acc = slots[0]
        for d in range(1, 8):
            acc = acc + slots[d]
        o_ref[...] = acc

    vm = pl.BlockSpec(memory_space=pltpu.VMEM)
    return pl.pallas_call(
        body, name=name, in_specs=[vm], out_specs=vm,
        out_shape=jax.ShapeDtypeStruct((rows, cols), F32),
        scratch_shapes=[pltpu.VMEM((8, rows, cols), F32), pltpu.SemaphoreType.DMA((7,)),
                        pltpu.SemaphoreType.DMA((7,))],
    )(v)


def _add_to_wire(a, b, name):
    n4, r, c_ = a.shape
    rows = n4 * r
    tr = _ew_tile(rows, c_, 3)

    def body(a_ref, b_ref, o_ref):
        o_ref[...] = (a_ref[...] + b_ref[...]).astype(o_ref.dtype)

    spec = pl.BlockSpec((tr, c_), lambda i: (i, 0))
    out = pl.pallas_call(
        body, name="add_to_wire_" + name, grid=(rows // tr,), in_specs=[spec, spec], out_specs=spec,
        out_shape=jax.ShapeDtypeStruct((rows, c_), jnp.bfloat16), compiler_params=_cp("parallel"),
    )(a.reshape(rows, c_), b.reshape(rows, c_))
    return out.reshape(n4, r, c_)


def _sum_chips(recv, own, name):
    _, r, c_ = recv.shape
    tr = _ew_tile(r, c_, 4)

    def body(r_ref, own_ref, o_ref):
        chip = 2 * lax.axis_index("x") + lax.axis_index("y")
        own_v = own_ref[...].astype(F32)
        acc = None
        for s in range(4):
            term = jnp.where(chip == s, own_v, r_ref[s].astype(F32))
            acc = term if acc is None else acc + term
        o_ref[...] = acc

    return pl.pallas_call(
        body, name="sum_chips_" + name, grid=(r // tr,),
        in_specs=[pl.BlockSpec((4, tr, c_), lambda i: (0, i, 0)), pl.BlockSpec((tr, c_), lambda i: (i, 0))],
        out_specs=pl.BlockSpec((tr, c_), lambda i: (i, 0)),
        out_shape=jax.ShapeDtypeStruct((r, c_), F32),
        compiler_params=_cp("parallel"),
    )(recv, own)


def _pack_small(vals, meta_full, conv_w_full):
    flat = jnp.concatenate([vals[k].reshape(-1) for k in SMALL] + [meta_full.reshape(-1), conv_w_full.reshape(-1)])
    return jnp.pad(flat, (0, SMALL_LEN - flat.shape[0])).reshape(SMALL_ROWS, 1024)


def _unpack_small(buf):
    flat = buf.reshape(-1)
    out, off = {}, 0
    for k in SMALL:
        n = DEPTH * SMALL_SIZES[k]
        out[k] = flat[off:off + n].reshape(DEPTH, SMALL_SIZES[k])
        off += n
    meta = flat[off:off + N_META * D].reshape(N_META, D)
    off += N_META * D
    conv_w = flat[off:off + DEPTH * CONV_K * CONV_DIM].reshape(DEPTH, CONV_K, CONV_DIM)
    return out, meta, conv_w


def kernel(x, meta, norm1_g, w_in, conv_w, conv_b, conv_ln_g, conv_ln_b, w_conv_out, q_a_norm_g, w_uq, kv_a_norm_g, w_ukv, q_norm_g, k_norm_g, w_attn_out, hgrn_lb_logits, hgrn_norm_g, w_hgrn_out, w_out, norm2_g, w_ff1, w_ff2, loss_target, m_meta, m_norm1_g, m_w_in, m_conv_w, m_conv_b, m_conv_ln_g, m_conv_ln_b, m_w_conv_out, m_q_a_norm_g, m_w_uq, m_kv_a_norm_g, m_w_ukv, m_q_norm_g, m_k_norm_g, m_w_attn_out, m_hgrn_lb_logits, m_hgrn_norm_g, m_w_hgrn_out, m_w_out, m_norm2_g, m_w_ff1, m_w_ff2, v_meta, v_norm1_g, v_w_in, v_conv_w, v_conv_b, v_conv_ln_g, v_conv_ln_b, v_w_conv_out, v_q_a_norm_g, v_w_uq, v_kv_a_norm_g, v_w_ukv, v_q_norm_g, v_k_norm_g, v_w_attn_out, v_hgrn_lb_logits, v_hgrn_norm_g, v_w_hgrn_out, v_w_out, v_norm2_g, v_w_ff1, v_w_ff2):
    names = ("meta", "norm1_g", "w_in", "conv_w", "conv_b", "conv_ln_g", "conv_ln_b", "w_conv_out", "q_a_norm_g",
             "w_uq", "kv_a_norm_g", "w_ukv", "q_norm_g", "k_norm_g", "w_attn_out", "hgrn_lb_logits", "hgrn_norm_g",
             "w_hgrn_out", "w_out", "norm2_g", "w_ff1", "w_ff2")
    w = dict(zip(names, (meta, norm1_g, w_in, conv_w, conv_b, conv_ln_g, conv_ln_b, w_conv_out, q_a_norm_g, w_uq,
                         kv_a_norm_g, w_ukv, q_norm_g, k_norm_g, w_attn_out, hgrn_lb_logits, hgrn_norm_g, w_hgrn_out,
                         w_out, norm2_g, w_ff1, w_ff2)))
    m = dict(zip(names, (m_meta, m_norm1_g, m_w_in, m_conv_w, m_conv_b, m_conv_ln_g, m_conv_ln_b, m_w_conv_out,
                         m_q_a_norm_g, m_w_uq, m_kv_a_norm_g, m_w_ukv, m_q_norm_g, m_k_norm_g, m_w_attn_out,
                         m_hgrn_lb_logits, m_hgrn_norm_g, m_w_hgrn_out, m_w_out, m_norm2_g, m_w_ff1, m_w_ff2)))
    v = dict(zip(names, (v_meta, v_norm1_g, v_w_in, v_conv_w, v_conv_b, v_conv_ln_g, v_conv_ln_b, v_w_conv_out,
                         v_q_a_norm_g, v_w_uq, v_kv_a_norm_g, v_w_ukv, v_q_norm_g, v_k_norm_g, v_w_attn_out,
                         v_hgrn_lb_logits, v_hgrn_norm_g, v_w_hgrn_out, v_w_out, v_norm2_g, v_w_ff1, v_w_ff2)))
    cx, cy, cc = _mesh_pos()
    chip = 2 * cx + cy
    zero = jnp.zeros((), jnp.int32)

    own = [w[k].astype(_MM) for k in BIG]
    gathered = _all_gather_weights(own)
    pieces = {k: [[jnp.where(chip == s, o[l], g[s, l]) for l in range(DEPTH)] for s in range(4)]
              for k, o, g in zip(BIG, own, gathered)}
    meta_slab = lax.dynamic_update_slice(jnp.zeros((N_META, D), F32), meta, (zero, chip * (D // 4)))
    convw_slab = lax.dynamic_update_slice(jnp.zeros((DEPTH, CONV_K, CONV_DIM), F32), conv_w,
                                          (zero, zero, chip * (CONV_DIM // 4)))
    zsmall = {k: jnp.zeros((DEPTH, SMALL_SIZES[k]), F32) for k in SMALL}
    south = (cc == 0).astype(F32)
    _, meta_full, convw_full = _unpack_small(
        _all_reduce_small(_pack_small(zsmall, meta_slab, convw_slab) * south, "gather_small"))
    small = {k: w[k] for k in SMALL}
    small["meta"] = meta_full
    small["conv_w"] = convw_full

    loss_share, grad_x, gl, g_meta = _device_step(x[0], loss_target[0], pieces, small)
    loss = lax.psum(loss_share, ("x", "y", "c"))

    g0 = [gl[0][k] for k in BIG]
    g1 = [gl[1][k] for k in BIG]
    from_sibling = _sibling_swap(g0, g1)

    def chip_sums(mine):
        return [_add_to_wire(a, b, k) for k, a, b in zip(BIG, mine, from_sibling)]

    wire = lax.cond(cc == 0, lambda: chip_sums(g0), lambda: chip_sums(g1))
    reds = [_sum_chips(r, lax.dynamic_index_in_dim(s, chip, 0, keepdims=False), k)
            for k, r, s in zip(BIG, _chip_scatter(wire), wire)]
    reds_sibling = _sibling_exchange(reds)
    grads, delta, new_m, new_v = {}, {}, {}, {}
    for k, mine, theirs in zip(BIG, reds, reds_sibling):
        grads[k], delta[k], new_m[k], new_v[k] = _adamw_layers(w[k], m[k], v[k], mine, theirs, k)

    g_small_local = {k: jnp.stack([gl[l][k] for l in range(DEPTH)]) for k in SMALL}
    g_convw_local = jnp.stack([gl[l]["conv_w"] for l in range(DEPTH)])
    g_small, g_meta_full, g_convw_full = _unpack_small(
        _all_reduce_small(_pack_small(g_small_local, g_meta, g_convw_local), "reduce_small"))
    grads.update(g_small)
    grads["meta"] = lax.dynamic_slice(g_meta_full, (zero, chip * (D // 4)), (N_META, D // 4))
    grads["conv_w"] = lax.dynamic_slice(g_convw_full, (zero, zero, chip * (CONV_DIM // 4)),
                                        (DEPTH, CONV_K, CONV_DIM // 4))

    def small_pack(src):
        return _pack_small(src, jnp.pad(src["meta"], ((0, 0), (0, D - D // 4))),
                           jnp.pad(src["conv_w"], ((0, 0), (0, 0), (0, CONV_DIM - CONV_DIM // 4))))

    def small_unpack(buf):
        out, meta_p, convw_p = _unpack_small(buf)
        out["meta"] = meta_p[:, :D // 4]
        out["conv_w"] = convw_p[:, :, :CONV_DIM // 4]
        return out

    d_s, m_s, v_s = [small_unpack(a) for a in _adamw(small_pack(w), small_pack(grads), small_pack(m),
                                                     small_pack(v), "small")]
    delta.update(d_s)
    new_m.update(m_s)
    new_v.update(v_s)
    return (loss, grad_x[None], *[grads[k] for k in names], *[delta[k] for k in names],
            *[new_m[k] for k in names], *[new_v[k] for k in names])
```

```python
import functools

import jax
import jax.numpy as jnp
from jax import lax
from jax.experimental import pallas as pl
from jax.experimental.pallas import tpu as pltpu

F32 = jnp.float32
_MM = jnp.bfloat16

D = 1024
N_META = 16
FRONT = 48
ROW0 = FRONT + N_META
EPS = 1e-6
GATE_CLAMP = 1.0 - 1e-6
CONV_K = 31
CONV_DIM = 512
NH = 8
QK_DIM = 96
ATT_SCALE = QK_DIM ** -0.5
HH = 4
CHUNK = 64
SUB = 16
EXP_CLIP = 60.0
NEG = -1e30
LANE = 128

SEG_GATES = (0, 3072)
SEG_AG = (3072, 4096)
SEG_H4 = (4096, 6144)
SEG_CQ = (6144, 6400)
SEG_CKV = (6400, 6528)
SEG_KR = (6528, 6656)
N_IN_P = 6656

ADAM_LR = 0.001
ADAM_B1 = 0.9
ADAM_B2 = 0.999
ADAM_EPS = 1e-08
ADAM_WD = 0.01
ADAM_STEP = 10

VMEM_LIMIT = 56 * 1024 * 1024


def _tile(n, pref):
    best = 64
    for t in range(64, pref + 1, 64):
        if n % t == 0:
            best = t
    return best


def _cp(*sem):
    return pltpu.CompilerParams(dimension_semantics=tuple(sem), vmem_limit_bytes=VMEM_LIMIT)


def _row(tm, n, col=0):
    return pl.BlockSpec((tm, n), lambda i: (i, col))


def _full(shape):
    return pl.BlockSpec(shape, lambda i: (0,) * len(shape))


def _mm(a, b):
    return jnp.dot(a.astype(_MM), b.astype(_MM), preferred_element_type=F32)


def _mm_nt(a, b):
    return lax.dot_general(a.astype(_MM), b.astype(_MM), (((1,), (1,)), ((), ())), preferred_element_type=F32)


def _mm_tn(a, b):
    return lax.dot_general(a.astype(_MM), b.astype(_MM), (((0,), (0,)), ((), ())), preferred_element_type=F32)


def _split3(x):
    hi = x.astype(jnp.bfloat16)
    return hi, (x - hi.astype(F32)).astype(jnp.bfloat16)


def _dot3(a, b, dims):
    ah, al = _split3(a)
    bh, bl = _split3(b)
    dg = lambda u, v: lax.dot_general(u, v, (dims, ((), ())), preferred_element_type=F32)
    return dg(ah, bh) + (dg(ah, bl) + dg(al, bh))


def _hmm(a, b):
    return _dot3(a, b, ((1,), (0,)))


def _hmm_nt(a, b):
    return _dot3(a, b, ((1,), (1,)))


def _hmm_tn(a, b):
    return _dot3(a, b, ((0,), (0,)))


def _sigmoid(x):
    return 1.0 / (1.0 + jnp.exp(-x))


def _rstd(x, n=None):
    n = x.shape[-1] if n is None else n
    return lax.rsqrt(jnp.sum(x * x, axis=-1, keepdims=True) * (1.0 / n) + EPS)


def _rms_bwd(dy, x, rstd, g, n=None):
    n = x.shape[-1] if n is None else n
    xh = x * rstd
    dxh = dy * g
    dx = rstd * (dxh - xh * (jnp.sum(dxh * xh, axis=-1, keepdims=True) * (1.0 / n)))
    return dx, dy * xh


def _valid_rows(i, tm, t_valid_end):
    r = i * tm + lax.broadcasted_iota(jnp.int32, (tm, 1), 0)
    return ((r >= FRONT) & (r < t_valid_end)).astype(F32)


def _colsum8(x):
    n, c = x.shape
    return jnp.sum(x.reshape(n // 8, 8, c), axis=0)


def _in_proj_fwd(x, g1, w):
    t = x.shape[0]
    tm = _tile(t, 192)
    segs = (SEG_GATES, SEG_AG, SEG_H4, SEG_CQ, SEG_CKV, SEG_KR)

    def body(x_ref, g_ref, w_ref, gates_ref, ag_ref, h4_ref, cq_ref, ckv_ref, kr_ref, hb_ref):
        xv = x_ref[...]
        hb = (xv * _rstd(xv) * g_ref[...]).astype(_MM)
        hb_ref[...] = hb
        for ref, (a, b) in zip((gates_ref, ag_ref, h4_ref, cq_ref, ckv_ref, kr_ref), segs):
            ref[...] = jnp.dot(hb, w_ref[:, a:b], preferred_element_type=F32)

    outs = [jax.ShapeDtypeStruct((t, b - a), F32) for a, b in segs] + [jax.ShapeDtypeStruct((t, D), _MM)]
    return pl.pallas_call(
        body, name="in_proj_fwd", grid=(t // tm,),
        in_specs=[_row(tm, D), _full((1, D)), _full((D, N_IN_P))],
        out_specs=[_row(tm, b - a) for a, b in segs] + [_row(tm, D)],
        out_shape=outs, compiler_params=_cp("parallel"),
    )(x, g1, w)


def _in_proj_bwd(du, x, dx1, g1, wt, t_end):
    t = x.shape[0]
    tm = _tile(t, 192)

    def body(du_ref, x_ref, dx1_ref, g_ref, wt_ref, dx_ref, dg_ref):
        i = pl.program_id(0)
        dh = _mm_nt(du_ref[...], wt_ref[...])
        xv = x_ref[...]
        dxn, dgrow = _rms_bwd(dh, xv, _rstd(xv), g_ref[...])
        dx_ref[...] = _valid_rows(i, tm, t_end) * (dx1_ref[...] + dxn)

        @pl.when(i == 0)
        def _():
            dg_ref[...] = jnp.zeros_like(dg_ref)
        dg_ref[...] += _colsum8(dgrow)

    return pl.pallas_call(
        body, name="in_proj_bwd", grid=(t // tm,),
        in_specs=[_row(tm, N_IN_P), _row(tm, D), _row(tm, D), _full((1, D)), _full((D, N_IN_P))],
        out_specs=[_row(tm, D), _full((8, D))],
        out_shape=[jax.ShapeDtypeStruct((t, D), F32), jax.ShapeDtypeStruct((8, D), F32)],
        compiler_params=_cp("arbitrary"),
    )(du, x, dx1, g1, wt)


CONV_CH = 128


def _conv_fwd(ag, cw, cb):
    t = ag.shape[0]
    n = t // CONV_CH

    def body(a_ref, g_ref, w_ref, b_ref, z_ref, hp):
        hp[0:32, :] = jnp.zeros((32, LANE), F32)

        def fill(i, c):
            r = pl.multiple_of(i * CONV_CH, CONV_CH)
            hp[pl.ds(32 + r, CONV_CH), :] = a_ref[pl.ds(r, CONV_CH), :] * _sigmoid(g_ref[pl.ds(r, CONV_CH), :])
            return c
        lax.fori_loop(0, n, fill, 0)

        def conv(i, c):
            r = pl.multiple_of(i * CONV_CH, CONV_CH)
            acc = jnp.broadcast_to(b_ref[...], (CONV_CH, LANE))
            for k in range(CONV_K):
                acc = acc + w_ref[k:k + 1, :] * hp[pl.ds(r + (k + 2), CONV_CH), :]
            z_ref[pl.ds(r, CONV_CH), :] = acc
            return c
        lax.fori_loop(0, n, conv, 0)

    nb = CONV_DIM // LANE
    return pl.pallas_call(
        body, name="conv_fwd", grid=(nb,),
        in_specs=[pl.BlockSpec((t, LANE), lambda j: (0, j)), pl.BlockSpec((t, LANE), lambda j: (0, nb + j)),
                  pl.BlockSpec((32, LANE), lambda j: (0, j)), pl.BlockSpec((1, LANE), lambda j: (0, j))],
        out_specs=pl.BlockSpec((t, LANE), lambda j: (0, j)),
        out_shape=jax.ShapeDtypeStruct((t, CONV_DIM), F32),
        scratch_shapes=[pltpu.VMEM((t + 32, LANE), F32)],
        compiler_params=_cp("parallel"),
    )(ag, ag, cw, cb)


def _conv_bwd(ag, cw, dz):
    t = ag.shape[0]
    n = t // CONV_CH

    def body(a_ref, g_ref, w_ref, dz_ref, da_ref, dg_ref, dcw_ref, hp, dzp, accw):
        hp[0:32, :] = jnp.zeros((32, LANE), F32)
        dzp[pl.ds(t, 32), :] = jnp.zeros((32, LANE), F32)
        accw[...] = jnp.zeros_like(accw)

        def fill(i, c):
            r = pl.multiple_of(i * CONV_CH, CONV_CH)
            hp[pl.ds(32 + r, CONV_CH), :] = a_ref[pl.ds(r, CONV_CH), :] * _sigmoid(g_ref[pl.ds(r, CONV_CH), :])
            dzp[pl.ds(r, CONV_CH), :] = dz_ref[pl.ds(r, CONV_CH), :]
            return c
        lax.fori_loop(0, n, fill, 0)

        def step(i, c):
            r = pl.multiple_of(i * CONV_CH, CONV_CH)
            dzc = dz_ref[pl.ds(r, CONV_CH), :]
            dh = jnp.zeros((CONV_CH, LANE), F32)
            for k in range(CONV_K):
                dh = dh + w_ref[k:k + 1, :] * dzp[pl.ds(r + (CONV_K - 1 - k), CONV_CH), :]
                accw[8 * k:8 * k + 8, :] += _colsum8(dzc * hp[pl.ds(r + (k + 2), CONV_CH), :])
            a = a_ref[pl.ds(r, CONV_CH), :]
            sg = _sigmoid(g_ref[pl.ds(r, CONV_CH), :])
            da_ref[pl.ds(r, CONV_CH), :] = dh * sg
            dg_ref[pl.ds(r, CONV_CH), :] = dh * a * sg * (1.0 - sg)
            return c
        lax.fori_loop(0, n, step, 0)

        for k in range(CONV_K):
            dcw_ref[k:k + 1, :] = jnp.sum(accw[8 * k:8 * k + 8, :], axis=0, keepdims=True)
        dcw_ref[CONV_K:32, :] = jnp.zeros((32 - CONV_K, LANE), F32)

    nb = CONV_DIM // LANE
    colspec = pl.BlockSpec((t, LANE), lambda j: (0, j))
    return pl.pallas_call(
        body, name="conv_bwd", grid=(nb,),
        in_specs=[colspec, pl.BlockSpec((t, LANE), lambda j: (0, nb + j)),
                  pl.BlockSpec((32, LANE), lambda j: (0, j)), colspec],
        out_specs=[colspec, colspec, pl.BlockSpec((32, LANE), lambda j: (0, j))],
        out_shape=[jax.ShapeDtypeStruct((t, CONV_DIM), F32), jax.ShapeDtypeStruct((t, CONV_DIM), F32),
                   jax.ShapeDtypeStruct((32, CONV_DIM), F32)],
        scratch_shapes=[pltpu.VMEM((t + 32, LANE), F32), pltpu.VMEM((t + 32, LANE), F32),
                        pltpu.VMEM((8 * 32, LANE), F32)],
        compiler_params=_cp("parallel"),
    )(ag, ag, cw, dz)


def _rope(x, c, s1, s2):
    return x * c + pltpu.roll(x, LANE - 16, 1) * s1 + pltpu.roll(x, 16, 1) * s2


def _rope_t(dy, c, s1, s2):
    return dy * c + pltpu.roll(dy * s1, 16, 1) + pltpu.roll(dy * s2, LANE - 16, 1)


def _mla_pre_fwd(cq, ckv, kr, qag, wuq, kvag, wk, wv, qng, kng, rc, rs1, rs2):
    t = cq.shape[0]
    tm = _tile(t, 384)

    def body(cq_ref, ckv_ref, kr_ref, qag_ref, wuq_ref, kvag_ref, wk_ref, wv_ref, qng_ref, kng_ref,
             c_ref, s1_ref, s2_ref, q_ref, k_ref, v_ref, cqn_ref, ckvn_ref):
        cqv = cq_ref[...]
        cqn = (cqv * _rstd(cqv) * qag_ref[...]).astype(_MM)
        cqn_ref[...] = cqn
        ckvv = ckv_ref[...]
        ckvn = (ckvv * _rstd(ckvv) * kvag_ref[...]).astype(_MM)
        ckvn_ref[...] = ckvn
        qraw = jnp.dot(cqn, wuq_ref[...], preferred_element_type=F32)
        kraw = jnp.dot(ckvn, wk_ref[...], preferred_element_type=F32)
        v_ref[...] = jnp.dot(ckvn, wv_ref[...], preferred_element_type=F32).astype(_MM)
        krv = kr_ref[...]
        c, s1, s2 = c_ref[...], s1_ref[...], s2_ref[...]
        for h in range(NH):
            sl = slice(LANE * h, LANE * (h + 1))
            qh = qraw[:, sl]
            qn = qh * _rstd(qh, QK_DIM) * qng_ref[...]
            q_ref[:, sl] = (_rope(qn, c, s1, s2) * ATT_SCALE).astype(_MM)
            kh = kraw[:, sl] + krv
            kn = kh * _rstd(kh, QK_DIM) * kng_ref[...]
            k_ref[:, sl] = _rope(kn, c, s1, s2).astype(_MM)

    hd = NH * LANE
    return pl.pallas_call(
        body, name="mla_pre_fwd", grid=(t // tm,),
        in_specs=[_row(tm, 256), _row(tm, 128), _row(tm, 128), _full((1, 256)), _full((256, hd)),
                  _full((1, 128)), _full((128, hd)), _full((128, hd)), _full((1, LANE)), _full((1, LANE)),
                  _row(tm, LANE), _row(tm, LANE), _row(tm, LANE)],
        out_specs=[_row(tm, hd), _row(tm, hd), _row(tm, hd), _row(tm, 256), _row(tm, 128)],
        out_shape=[jax.ShapeDtypeStruct((t, hd), _MM)] * 3 + [jax.ShapeDtypeStruct((t, 256), _MM),
                                                              jax.ShapeDtypeStruct((t, 128), _MM)],
        compiler_params=_cp("parallel"),
    )(cq, ckv, kr, qag, wuq, kvag, wk, wv, qng, kng, rc, rs1, rs2)


def _mla_pre_bwd(dq, dk, dv, cq, ckv, kr, qag, wuq, kvag, wk, wv, qng, kng, rc, rs1, rs2):
    t = cq.shape[0]
    tm = _tile(t, 192)
    hd = NH * LANE

    def body(dq_ref, dk_ref, dv_ref, cq_ref, ckv_ref, kr_ref, qag_ref, wuq_ref, kvag_ref, wk_ref,
             wv_ref, qng_ref, kng_ref, c_ref, s1_ref, s2_ref,
             dcq_ref, dckv_ref, dkr_ref, dqraw_ref, dkraw_ref, dqag_ref, dkvag_ref, dqng_ref, dkng_ref):
        i = pl.program_id(0)
        cqv = cq_ref[...]
        rq_in = _rstd(cqv)
        cqn = (cqv * rq_in * qag_ref[...]).astype(_MM)
        ckvv = ckv_ref[...]
        rkv_in = _rstd(ckvv)
        ckvn = (ckvv * rkv_in * kvag_ref[...]).astype(_MM)
        qraw = jnp.dot(cqn, wuq_ref[...], preferred_element_type=F32)
        kraw = jnp.dot(ckvn, wk_ref[...], preferred_element_type=F32)
        krv = kr_ref[...]
        c, s1, s2 = c_ref[...], s1_ref[...], s2_ref[...]
        dkr = jnp.zeros((tm, LANE), F32)
        dqng = jnp.zeros((8, LANE), F32)
        dkng = jnp.zeros((8, LANE), F32)
        for h in range(NH):
            sl = slice(LANE * h, LANE * (h + 1))
            qh = qraw[:, sl]
            dqn = _rope_t(dq_ref[:, sl] * ATT_SCALE, c, s1, s2)
            dqh, gq = _rms_bwd(dqn, qh, _rstd(qh, QK_DIM), qng_ref[...], QK_DIM)
            dqraw_ref[:, sl] = dqh.astype(_MM)
            dqng = dqng + _colsum8(gq)
            kh = kraw[:, sl] + krv
            dkn = _rope_t(dk_ref[:, sl], c, s1, s2)
            dkh, gk = _rms_bwd(dkn, kh, _rstd(kh, QK_DIM), kng_ref[...], QK_DIM)
            dkraw_ref[:, sl] = dkh.astype(_MM)
            dkr = dkr + dkh
            dkng = dkng + _colsum8(gk)
        dkr_ref[...] = dkr.astype(_MM)
        dcqn = _mm_nt(dqraw_ref[...], wuq_ref[...])
        dcq, gqa = _rms_bwd(dcqn, cqv, rq_in, qag_ref[...])
        dcq_ref[...] = dcq.astype(_MM)
        dckvn = _mm_nt(dkraw_ref[...], wk_ref[...]) + _mm_nt(dv_ref[...], wv_ref[...])
        dckv, gkva = _rms_bwd(dckvn, ckvv, rkv_in, kvag_ref[...])
        dckv_ref[...] = dckv.astype(_MM)

        @pl.when(i == 0)
        def _():
            dqag_ref[...] = jnp.zeros_like(dqag_ref)
            dkvag_ref[...] = jnp.zeros_like(dkvag_ref)
            dqng_ref[...] = jnp.zeros_like(dqng_ref)
            dkng_ref[...] = jnp.zeros_like(dkng_ref)
        dqag_ref[...] += _colsum8(gqa)
        dkvag_ref[...] += _colsum8(gkva)
        dqng_ref[...] += dqng
        dkng_ref[...] += dkng

    return pl.pallas_call(
        body, name="mla_pre_bwd", grid=(t // tm,),
        in_specs=[_row(tm, hd), _row(tm, hd), _row(tm, hd), _row(tm, 256), _row(tm, 128), _row(tm, 128),
                  _full((1, 256)), _full((256, hd)), _full((1, 128)), _full((128, hd)),
                  _full((128, hd)), _full((1, LANE)), _full((1, LANE)),
                  _row(tm, LANE), _row(tm, LANE), _row(tm, LANE)],
        out_specs=[_row(tm, 256), _row(tm, 128), _row(tm, 128), _row(tm, hd), _row(tm, hd),
                   _full((8, 256)), _full((8, 128)), _full((8, LANE)), _full((8, LANE))],
        out_shape=[jax.ShapeDtypeStruct((t, 256), _MM), jax.ShapeDtypeStruct((t, 128), _MM),
                   jax.ShapeDtypeStruct((t, 128), _MM), jax.ShapeDtypeStruct((t, hd), _MM),
                   jax.ShapeDtypeStruct((t, hd), _MM), jax.ShapeDtypeStruct((8, 256), F32),
                   jax.ShapeDtypeStruct((8, 128), F32), jax.ShapeDtypeStruct((8, LANE), F32),
                   jax.ShapeDtypeStruct((8, LANE), F32)],
        compiler_params=_cp("arbitrary"),
    )(dq, dk, dv, cq, ckv, kr, qag, wuq, kvag, wk, wv, qng, kng, rc, rs1, rs2)


ATT_TILE = 384


def _attn_tile(t):
    return _tile(t, ATT_TILE)


def _to_head_blocks(a, tq):
    t = a.shape[0]
    return a.reshape(t // tq, tq, NH, LANE).transpose(2, 0, 3, 1)


def _from_head_blocks(a):
    nh, nq, d, tq = a.shape
    return a.transpose(1, 3, 0, 2).reshape(nq * tq, nh * d)


def _attn_mask_t(r0, c0, tq):
    keys = c0 + lax.broadcasted_iota(jnp.int32, (tq, 1), 0)
    qrys = r0 + lax.broadcasted_iota(jnp.int32, (1, tq), 1)
    return (keys <= qrys) & (keys >= FRONT)


def _attn_fwd(q, k, vt):
    t = q.shape[0]
    tq = _attn_tile(t)
    nq = t // tq

    def body(q_ref, k_ref, vt_ref, ot_ref, lse_ref):
        def qloop(qi, carry):
            r0 = pl.multiple_of(qi * tq, tq)
            qb = q_ref[pl.ds(r0, tq), :]

            def kstep(kj, st, masked):
                m, l, acc = st
                c0 = pl.multiple_of(kj * tq, tq)
                s = _mm_nt(k_ref[pl.ds(c0, tq), :], qb)
                if masked:
                    s = jnp.where(_attn_mask_t(r0, c0, tq), s, NEG)
                m2 = jnp.maximum(m, jnp.max(s, axis=0, keepdims=True))
                p = jnp.exp(s - m2)
                a = jnp.exp(m - m2)
                l = a * l + jnp.sum(p, axis=0, keepdims=True)
                acc = a * acc + _mm(vt_ref[kj], p)
                return m2, l, acc

            st = kstep(0, (jnp.full((1, tq), NEG, F32), jnp.zeros((1, tq), F32), jnp.zeros((LANE, tq), F32)), True)
            st = lax.fori_loop(1, qi, lambda kj, s_: kstep(kj, s_, False), st)
            m, l, acc = lax.cond(qi > 0, lambda s_: kstep(qi, s_, True), lambda s_: s_, st)
            ot_ref[qi] = acc / l
            lse_ref[qi] = m + jnp.log(l)
            return carry
        lax.fori_loop(0, nq, qloop, 0)

    hs = pl.BlockSpec((t, LANE), lambda h: (0, h))
    hb = pl.BlockSpec((None, nq, LANE, tq), lambda h: (h, 0, 0, 0))
    hr = pl.BlockSpec((None, nq, 1, tq), lambda h: (h, 0, 0, 0))
    return pl.pallas_call(
        body, name="attn_fwd", grid=(NH,),
        in_specs=[hs, hs, hb], out_specs=[hb, hr],
        out_shape=[jax.ShapeDtypeStruct((NH, nq, LANE, tq), F32), jax.ShapeDtypeStruct((NH, nq, 1, tq), F32)],
        compiler_params=_cp("parallel"),
    )(q, k, vt)


def _attn_bwd(q, k, v, do, qt, kt, dot, ot, lse):
    t = q.shape[0]
    tq = _attn_tile(t)
    nq = t // tq

    def body(q_ref, k_ref, v_ref, do_ref, qt_ref, kt_ref, dot_ref, ot_ref, lse_ref,
             dqt_ref, dkt_ref, dvt_ref, delta):
        def prep(i, c):
            delta[i] = jnp.sum(dot_ref[i] * ot_ref[i], axis=0, keepdims=True)
            dqt_ref[i] = jnp.zeros((LANE, tq), F32)
            return c
        lax.fori_loop(0, nq, prep, 0)

        def kloop(kj, carry):
            c0 = pl.multiple_of(kj * tq, tq)
            kb = k_ref[pl.ds(c0, tq), :]
            vb = v_ref[pl.ds(c0, tq), :]
            ktb = kt_ref[kj]

            def qstep(qi, st, masked):
                dkt, dvt = st
                r0 = pl.multiple_of(qi * tq, tq)
                s = _mm_nt(kb, q_ref[pl.ds(r0, tq), :])
                if masked:
                    s = jnp.where(_attn_mask_t(r0, c0, tq), s, NEG)
                p = jnp.exp(s - lse_ref[qi])
                dvt = dvt + _mm_nt(dot_ref[qi], p)
                dp = _mm_nt(vb, do_ref[pl.ds(r0, tq), :])
                ds = (p * (dp - delta[qi])).astype(_MM)
                dkt = dkt + _mm_nt(qt_ref[qi], ds)
                dqt_ref[qi] += _mm(ktb, ds)
                return dkt, dvt

            st = qstep(kj, (jnp.zeros((LANE, tq), F32), jnp.zeros((LANE, tq), F32)), True)
            dkt, dvt = lax.cond(
                kj == 0,
                lambda s_: lax.fori_loop(kj + 1, nq, lambda qi, t_: qstep(qi, t_, True), s_),
                lambda s_: lax.fori_loop(kj + 1, nq, lambda qi, t_: qstep(qi, t_, False), s_), st)
            dkt_ref[kj] = dkt
            dvt_ref[kj] = dvt
            return carry
        lax.fori_loop(0, nq, kloop, 0)

    hs = pl.BlockSpec((t, LANE), lambda h: (0, h))
    hb = pl.BlockSpec((None, nq, LANE, tq), lambda h: (h, 0, 0, 0))
    hr = pl.BlockSpec((None, nq, 1, tq), lambda h: (h, 0, 0, 0))
    return pl.pallas_call(
        body, name="attn_bwd", grid=(NH,),
        in_specs=[hs, hs, hs, hs, hb, hb, hb, hb, hr],
        out_specs=[hb, hb, hb],
        out_shape=[jax.ShapeDtypeStruct((NH, nq, LANE, tq), F32)] * 3,
        scratch_shapes=[pltpu.VMEM((nq, 1, tq), F32)],
        compiler_params=_cp("parallel"),
    )(q, k, v, do, qt, kt, dot, ot, lse)


def _cumsum_rows(x):
    n = x.shape[0]
    rows = lax.broadcasted_iota(jnp.int32, (n, 1), 0)
    d = 1
    while d < n:
        x = x + jnp.where(rows >= d, pltpu.roll(x, d, 0), 0.0)
        d *= 2
    return x


def _revcumsum_rows(x):
    n = x.shape[0]
    rows = lax.broadcasted_iota(jnp.int32, (n, 1), 0)
    d = 1
    while d < n:
        x = x + jnp.where(rows < n - d, pltpu.roll(x, n - d, 0), 0.0)
        d *= 2
    return x


def _hgrn_gates(f, lb):
    sneg = _sigmoid(-f)
    kk = (1.0 - lb) * sneg
    lf = jnp.log1p(-jnp.minimum(kk, GATE_CLAMP))
    return kk, lf, sneg


def _silu(x):
    return x * _sigmoid(x)


def _dsilu(x):
    s = _sigmoid(x)
    return s * (1.0 + x * (1.0 - s))


def _hgrn_intra(q, kk, b):
    parts = []
    for blk in range(CHUNK // SUB):
        lo = blk * SUB
        ref = jnp.zeros((1, LANE), F32) if blk == 0 else b[lo - 1:lo, :]
        eq = jnp.exp(b[lo:lo + SUB, :] - ref)
        ek = jnp.exp(jnp.minimum(ref - b, EXP_CLIP))
        parts.append((q[lo:lo + SUB, :] * eq, kk * ek, eq, ek))
    return parts


def _chunk_causal():
    return lax.broadcasted_iota(jnp.int32, (CHUNK, CHUNK), 1) <= lax.broadcasted_iota(jnp.int32, (CHUNK, CHUNK), 0)


def _hgrn_fwd(h4, lb):
    t = h4.shape[0]
    nc = t // CHUNK

    def body(q_ref, f_ref, i_ref, lb_ref, o_ref, s_ref, st):
        st[...] = jnp.zeros_like(st)
        causal = _chunk_causal()

        def chunk(c, carry):
            r0 = pl.multiple_of(c * CHUNK, CHUNK)
            q = q_ref[pl.ds(r0, CHUNK), :]
            kk, lf, _ = _hgrn_gates(f_ref[pl.ds(r0, CHUNK), :], lb_ref[...])
            v = _silu(i_ref[pl.ds(r0, CHUNK), :])
            b = _cumsum_rows(lf)
            s_prev = st[...]
            s_ref[c] = s_prev
            o = _hmm_nt(q * jnp.exp(b), s_prev)
            a = jnp.concatenate([_hmm_nt(qs, ks) for qs, ks, _, _ in _hgrn_intra(q, kk, b)], axis=0)
            a = jnp.where(causal, a, 0.0)
            o_ref[pl.ds(r0, CHUNK), :] = o + _hmm(a, v)
            bl = b[CHUNK - 1:CHUNK, :]
            st[...] = s_prev * jnp.exp(bl) + _hmm_tn(v, kk * jnp.exp(bl - b))
            return carry
        lax.fori_loop(0, nc, chunk, 0, unroll=2)

    def col(j):
        return pl.BlockSpec((t, LANE), lambda h: (0, HH * j + h))
    return pl.pallas_call(
        body, name="hgrn_fwd", grid=(HH,),
        in_specs=[col(0), col(1), col(2), pl.BlockSpec((1, LANE), lambda h: (0, h))],
        out_specs=[pl.BlockSpec((t, LANE), lambda h: (0, h)),
                   pl.BlockSpec((None, nc, LANE, LANE), lambda h: (h, 0, 0, 0))],
        out_shape=[jax.ShapeDtypeStruct((t, HH * LANE), F32), jax.ShapeDtypeStruct((HH, nc, LANE, LANE), F32)],
        scratch_shapes=[pltpu.VMEM((LANE, LANE), F32)],
        compiler_params=_cp("parallel"),
    )(h4, h4, h4, lb)


def _hgrn_bwd(h4, lb, do, states):
    t = h4.shape[0]
    nc = t // CHUNK

    def body(q_ref, f_ref, i_ref, lb_ref, do_ref, s_ref, dq_ref, df_ref, di_ref, dlb_ref, dst, carry):
        dst[...] = jnp.zeros_like(dst)
        carry[...] = jnp.zeros_like(carry)
        dlb_ref[...] = jnp.zeros_like(dlb_ref)
        causal = _chunk_causal()

        def chunk(cc, cr):
            c = nc - 1 - cc
            r0 = pl.multiple_of(c * CHUNK, CHUNK)
            q = q_ref[pl.ds(r0, CHUNK), :]
            lbv = lb_ref[...]
            kk, lf, sneg = _hgrn_gates(f_ref[pl.ds(r0, CHUNK), :], lbv)
            iv = i_ref[pl.ds(r0, CHUNK), :]
            v = _silu(iv)
            b = _cumsum_rows(lf)
            s_prev = s_ref[c]
            ds_new = dst[...]
            dob = do_ref[pl.ds(r0, CHUNK), :]
            e = jnp.exp(b)
            qe = q * e
            bl = b[CHUNK - 1:CHUNK, :]
            etail = jnp.exp(bl - b)
            kd = kk * etail
            dq_inter = _hmm(dob, s_prev) * e
            dv = _hmm_nt(kd, ds_new)
            dkk = _hmm(v, ds_new) * etail
            parts = _hgrn_intra(q, kk, b)
            a = jnp.where(causal, jnp.concatenate([_hmm_nt(qs, ks) for qs, ks, _, _ in parts], axis=0), 0.0)
            da = jnp.where(causal, _hmm_nt(dob, v), 0.0)
            dv = dv + _hmm_tn(a, dob)
            dq_rows = []
            for blk, (qs, ks, eq, ek) in enumerate(parts):
                da_blk = da[blk * SUB:(blk + 1) * SUB, :]
                dq_rows.append(_hmm(da_blk, ks) * eq)
                dkk = dkk + _hmm_tn(da_blk, qs) * ek
            dq = dq_inter + jnp.concatenate(dq_rows, axis=0)
            dst[...] = ds_new * jnp.exp(bl) + _hmm_tn(dob, qe)
            g = q * dq - kk * dkk
            dlf = _revcumsum_rows(g) + carry[0:1, :]
            carry[0:1, :] += jnp.sum(g, axis=0, keepdims=True)
            dkk_tot = dkk + dlf * jnp.where(kk < GATE_CLAMP, -1.0 / (1.0 - kk), 0.0)
            dq_ref[pl.ds(r0, CHUNK), :] = dq
            df_ref[pl.ds(r0, CHUNK), :] = dkk_tot * (1.0 - lbv) * (-sneg * (1.0 - sneg))
            di_ref[pl.ds(r0, CHUNK), :] = dv * _dsilu(iv)
            dlb_ref[...] += _colsum8(dkk_tot * (-sneg))
            return cr
        lax.fori_loop(0, nc, chunk, 0, unroll=2)

    def col(j):
        return pl.BlockSpec((t, LANE), lambda h: (0, HH * j + h))
    hs = pl.BlockSpec((t, LANE), lambda h: (0, h))
    return pl.pallas_call(
        body, name="hgrn_bwd", grid=(HH,),
        in_specs=[col(0), col(1), col(2), pl.BlockSpec((1, LANE), lambda h: (0, h)), hs,
                  pl.BlockSpec((None, nc, LANE, LANE), lambda h: (h, 0, 0, 0))],
        out_specs=[hs, hs, hs, pl.BlockSpec((8, LANE), lambda h: (0, h))],
        out_shape=[jax.ShapeDtypeStruct((t, HH * LANE), F32)] * 3 + [jax.ShapeDtypeStruct((8, HH * LANE), F32)],
        scratch_shapes=[pltpu.VMEM((LANE, LANE), F32), pltpu.VMEM((8, LANE), F32)],
        compiler_params=_cp("parallel"),
    )(h4, h4, h4, lb, do, states)


def _ln_fwd(z, g, b):
    mu = jnp.mean(z, axis=-1, keepdims=True)
    zc = z - mu
    rstd = lax.rsqrt(jnp.mean(zc * zc, axis=-1, keepdims=True) + EPS)
    zh = zc * rstd
    return zh * g + b, zh, rstd


def _mix_fwd(x, z, o_att, o_h, h4, gates, lng, lnb, wco, wao, ng, who, wout, t_end):
    t = x.shape[0]
    tm = _tile(t, 192)

    def body(x_ref, z_ref, oa_ref, oh_ref, hg_ref, gt_ref, lng_ref, lnb_ref, wco_ref, wao_ref, ng_ref, who_ref,
             wout_ref, x1_ref, mix_ref, ca_ref, oc_ref, ya_ref, yb_ref, yc_ref):
        i = pl.program_id(0)
        ln, _, _ = _ln_fwd(z_ref[...], lng_ref[...], lnb_ref[...])
        ca = _silu(ln).astype(_MM)
        ca_ref[...] = ca
        ya = jnp.dot(ca, wco_ref[...], preferred_element_type=F32)
        yb = _mm(oa_ref[...], wao_ref[...])
        hg = hg_ref[...]
        for h in range(HH):
            sl = slice(LANE * h, LANE * (h + 1))
            oh = oh_ref[:, sl]
            oc_ref[:, sl] = (oh * _rstd(oh) * ng_ref[:, sl] * _silu(hg[:, sl])).astype(_MM)
        yc = jnp.dot(oc_ref[...], who_ref[...], preferred_element_type=F32)
        ya_ref[...] = ya
        yb_ref[...] = yb
        yc_ref[...] = yc
        mix = (_sigmoid(gt_ref[:, 0:D]) * ya + _sigmoid(gt_ref[:, D:2 * D]) * yb
               + _sigmoid(gt_ref[:, 2 * D:3 * D]) * yc).astype(_MM)
        mix_ref[...] = mix
        x1_ref[...] = x_ref[...] + _valid_rows(i, tm, t_end) * jnp.dot(mix, wout_ref[...],
                                                                       preferred_element_type=F32)

    hd = NH * LANE
    return pl.pallas_call(
        body, name="mix_fwd", grid=(t // tm,),
        in_specs=[_row(tm, D), _row(tm, CONV_DIM), _row(tm, hd), _row(tm, 512), _row(tm, 512, 3), _row(tm, 3 * D),
                  _full((1, 512)), _full((1, 512)), _full((512, D)), _full((hd, D)), _full((1, 512)),
                  _full((512, D)), _full((D, D))],
        out_specs=[_row(tm, D), _row(tm, D), _row(tm, 512), _row(tm, 512), _row(tm, D), _row(tm, D), _row(tm, D)],
        out_shape=[jax.ShapeDtypeStruct((t, D), F32), jax.ShapeDtypeStruct((t, D), _MM),
                   jax.ShapeDtypeStruct((t, 512), _MM), jax.ShapeDtypeStruct((t, 512), _MM),
                   jax.ShapeDtypeStruct((t, D), F32), jax.ShapeDtypeStruct((t, D), F32),
                   jax.ShapeDtypeStruct((t, D), F32)],
        compiler_params=_cp("parallel"),
    )(x, z, o_att, o_h, h4, gates, lng, lnb, wco, wao, ng, who, wout)


def _mix_bwd(dx1, ya, yb, yc, gates, z, o_h, h4, lng, lnb, ng, wout, wco, wao, who):
    t = dx1.shape[0]
    tm = _tile(t, 192)
    hd = NH * LANE

    def body(dx1_ref, ya_ref, yb_ref, yc_ref, gt_ref, z_ref, oh_ref, hg_ref, lng_ref, lnb_ref, ng_ref,
             wout_ref, wco_ref, wao_ref, who_ref,
             dgt_ref, dya_ref, dyb_ref, dyc_ref, dz_ref, doa_ref, doh_ref, dhg_ref,
             dlng_ref, dlnb_ref, dcb_ref, dng_ref):
        i = pl.program_id(0)
        dmix = _mm_nt(dx1_ref[...], wout_ref[...])
        dys = []
        for j, y_ref in enumerate((ya_ref, yb_ref, yc_ref)):
            sg = _sigmoid(gt_ref[:, j * D:(j + 1) * D])
            dgt_ref[:, j * D:(j + 1) * D] = (dmix * y_ref[...] * sg * (1.0 - sg)).astype(_MM)
            dys.append((dmix * sg).astype(_MM))
        dya_ref[...], dyb_ref[...], dyc_ref[...] = dys
        dca = _mm_nt(dys[0], wco_ref[...])
        ln, zh, rstd = _ln_fwd(z_ref[...], lng_ref[...], lnb_ref[...])
        dln = dca * _dsilu(ln)
        dzh = dln * lng_ref[...]
        dz = rstd * (dzh - jnp.mean(dzh, axis=-1, keepdims=True)
                     - zh * jnp.mean(dzh * zh, axis=-1, keepdims=True))
        dz_ref[...] = dz
        doa_ref[...] = _mm_nt(dys[1], wao_ref[...])
        doc = _mm_nt(dys[2], who_ref[...])
        hg = hg_ref[...]
        dng_rows = []
        for h in range(HH):
            sl = slice(LANE * h, LANE * (h + 1))
            oh = oh_ref[:, sl]
            r = _rstd(oh)
            don = doc[:, sl] * _silu(hg[:, sl])
            dhg_ref[:, sl] = (doc[:, sl] * oh * r * ng_ref[:, sl] * _dsilu(hg[:, sl])).astype(_MM)
            doh, gn = _rms_bwd(don, oh, r, ng_ref[:, sl])
            doh_ref[:, sl] = doh
            dng_rows.append(_colsum8(gn))

        @pl.when(i == 0)
        def _():
            dlng_ref[...] = jnp.zeros_like(dlng_ref)
            dlnb_ref[...] = jnp.zeros_like(dlnb_ref)
            dcb_ref[...] = jnp.zeros_like(dcb_ref)
            dng_ref[...] = jnp.zeros_like(dng_ref)
        dlng_ref[...] += _colsum8(dln * zh)
        dlnb_ref[...] += _colsum8(dln)
        dcb_ref[...] += _colsum8(dz)
        dng_ref[...] += jnp.concatenate(dng_rows, axis=1)

    return pl.pallas_call(
        body, name="mix_bwd", grid=(t // tm,),
        in_specs=[_row(tm, D), _row(tm, D), _row(tm, D), _row(tm, D), _row(tm, 3 * D), _row(tm, 512), _row(tm, 512),
                  _row(tm, 512, 3), _full((1, 512)), _full((1, 512)), _full((1, 512)),
                  _full((D, D)), _full((512, D)), _full((hd, D)), _full((512, D))],
        out_specs=[_row(tm, 3 * D), _row(tm, D), _row(tm, D), _row(tm, D), _row(tm, 512), _row(tm, hd),
                   _row(tm, 512), _row(tm, 512), _full((8, 512)), _full((8, 512)), _full((8, 512)), _full((8, 512))],
        out_shape=[jax.ShapeDtypeStruct((t, 3 * D), _MM), jax.ShapeDtypeStruct((t, D), _MM),
                   jax.ShapeDtypeStruct((t, D), _MM), jax.ShapeDtypeStruct((t, D), _MM),
                   jax.ShapeDtypeStruct((t, 512), F32), jax.ShapeDtypeStruct((t, hd), F32),
                   jax.ShapeDtypeStruct((t, 512), F32), jax.ShapeDtypeStruct((t, 512), _MM)]
        + [jax.ShapeDtypeStruct((8, 512), F32)] * 4,
        compiler_params=_cp("arbitrary"),
    )(dx1, ya, yb, yc, gates, z, o_h, h4, lng, lnb, ng, wout, wco, wao, who)


D_FF = 4096


def _ffn_fwd(x1, g2, w1, w2):
    t = x1.shape[0]
    tm = _tile(t, 192)

    def body(x1_ref, g_ref, w1_ref, w2_ref, x2_ref, p_ref):
        xv = x1_ref[...]
        h2 = (xv * _rstd(xv) * g_ref[...]).astype(_MM)
        p = jnp.dot(h2, w1_ref[...], preferred_element_type=F32)
        p_ref[...] = p
        r = jnp.maximum(p, 0.0)
        x2_ref[...] = xv + jnp.dot((r * r).astype(_MM), w2_ref[...], preferred_element_type=F32)

    return pl.pallas_call(
        body, name="ffn_fwd", grid=(t // tm,),
        in_specs=[_row(tm, D), _full((1, D)), _full((D, D_FF)), _full((D_FF, D))],
        out_specs=[_row(tm, D), _row(tm, D_FF)],
        out_shape=[jax.ShapeDtypeStruct((t, D), F32), jax.ShapeDtypeStruct((t, D_FF), F32)],
        compiler_params=_cp("parallel"),
    )(x1, g2, w1, w2)


def _ffn_bwd(dx2, x1, p, g2, w1, w2):
    t = x1.shape[0]
    tm = _tile(t, 192)

    def body(dx2_ref, x1_ref, p_ref, g_ref, w1_ref, w2_ref, dx1_ref, h2_ref, act_ref, dp_ref, dg_ref):
        i = pl.program_id(0)
        xv = x1_ref[...]
        rstd = _rstd(xv)
        h2_ref[...] = (xv * rstd * g_ref[...]).astype(_MM)
        r = jnp.maximum(p_ref[...], 0.0)
        act_ref[...] = (r * r).astype(_MM)
        dx2 = dx2_ref[...]
        da = _mm_nt(dx2, w2_ref[...])
        dp = (2.0 * r * da).astype(_MM)
        dp_ref[...] = dp
        dh2 = _mm_nt(dp, w1_ref[...])
        dxn, dgrow = _rms_bwd(dh2, xv, rstd, g_ref[...])
        dx1_ref[...] = dx2 + dxn

        @pl.when(i == 0)
        def _():
            dg_ref[...] = jnp.zeros_like(dg_ref)
        dg_ref[...] += _colsum8(dgrow)

    return pl.pallas_call(
        body, name="ffn_bwd", grid=(t // tm,),
        in_specs=[_row(tm, D), _row(tm, D), _row(tm, D_FF), _full((1, D)), _full((D, D_FF)), _full((D_FF, D))],
        out_specs=[_row(tm, D), _row(tm, D), _row(tm, D_FF), _row(tm, D_FF), _full((8, D))],
        out_shape=[jax.ShapeDtypeStruct((t, D), F32), jax.ShapeDtypeStruct((t, D), _MM),
                   jax.ShapeDtypeStruct((t, D_FF), _MM), jax.ShapeDtypeStruct((t, D_FF), _MM),
                   jax.ShapeDtypeStruct((8, D), F32)],
        compiler_params=_cp("arbitrary"),
    )(dx2, x1, p, g2, w1, w2)


def _wgrad(a, b, name, chips=1):
    t, ka = a.shape
    nb = b.shape[1]
    tm = _tile(t, 384)
    cs = nb // chips
    tn = max(d for d in range(LANE, cs + 1, LANE) if cs % d == 0 and (ka * d * 4 <= 16 * 1024 * 1024 or d == LANE))
    per = cs // tn

    def body(a_ref, b_ref, o_ref):
        @pl.when(pl.program_id(1) == 0)
        def _():
            o_ref[...] = jnp.zeros_like(o_ref)
        o_ref[...] += _mm_tn(a_ref[...], b_ref[...])

    if chips == 1:
        out_spec = pl.BlockSpec((ka, tn), lambda n, i: (0, n))
        out_shape = jax.ShapeDtypeStruct((ka, nb), F32)
    else:
        out_spec = pl.BlockSpec((None, ka, tn), lambda n, i: (n // per, 0, n % per))
        out_shape = jax.ShapeDtypeStruct((chips, ka, cs), F32)
    return pl.pallas_call(
        body, name="wgrad_" + name, grid=(nb // tn, t // tm),
        in_specs=[pl.BlockSpec((tm, ka), lambda n, i: (i, 0)), pl.BlockSpec((tm, tn), lambda n, i: (i, n))],
        out_specs=out_spec, out_shape=out_shape,
        compiler_params=_cp("parallel", "arbitrary"),
    )(a, b)


def _loss_head(y, target, t_end):
    t = y.shape[0]
    tm = _tile(t, 384)

    def body(y_ref, tg_ref, dy_ref, l_ref):
        i = pl.program_id(0)
        r = i * tm + lax.broadcasted_iota(jnp.int32, (tm, 1), 0)
        real = ((r >= ROW0) & (r < t_end)).astype(F32)
        diff = (y_ref[...] - tg_ref[...]) * real
        dy_ref[...] = diff * (1.0 / D)

        @pl.when(i == 0)
        def _():
            l_ref[...] = jnp.zeros_like(l_ref)
        sq = _colsum8(diff * diff)
        part = sq[:, 0:LANE]
        for j in range(1, D // LANE):
            part = part + sq[:, j * LANE:(j + 1) * LANE]
        l_ref[...] += part * (0.5 / D)

    return pl.pallas_call(
        body, name="loss_head", grid=(t // tm,),
        in_specs=[_row(tm, D), _row(tm, D)],
        out_specs=[_row(tm, D), _full((8, LANE))],
        out_shape=[jax.ShapeDtypeStruct((t, D), F32), jax.ShapeDtypeStruct((8, LANE), F32)],
        compiler_params=_cp("arbitrary"),
    )(y, target)


def _lower_bounds_fwd(logits):
    depth, n = logits.shape

    def body(l_ref, lb_ref):
        lg = l_ref[...]
        m = jnp.max(lg, axis=0, keepdims=True)
        e = jnp.exp(lg - m)
        p = e / jnp.sum(e, axis=0, keepdims=True)
        acc = jnp.zeros((1, n), F32)
        for l in range(depth):
            if l > 0:
                acc = acc + p[l:l + 1, :]
            lb_ref[l:l + 1, :] = acc

    return pl.pallas_call(body, name="lower_bounds_fwd", out_shape=jax.ShapeDtypeStruct((depth, n), F32))(logits)


def _lower_bounds_bwd(logits, dlb):
    depth, n = logits.shape

    def body(l_ref, dlb_ref, dl_ref):
        lg = l_ref[...]
        m = jnp.max(lg, axis=0, keepdims=True)
        e = jnp.exp(lg - m)
        p = e / jnp.sum(e, axis=0, keepdims=True)
        dps = [jnp.zeros((1, n), F32)]
        for j in range(1, depth):
            acc = jnp.zeros((1, n), F32)
            for l in range(j, depth):
                acc = acc + dlb_ref[l:l + 1, :]
            dps.append(acc)
        dot = jnp.zeros((1, n), F32)
        for j in range(depth):
            dot = dot + p[j:j + 1, :] * dps[j]
        for j in range(depth):
            dl_ref[j:j + 1, :] = p[j:j + 1, :] * (dps[j] - dot)

    return pl.pallas_call(body, name="lower_bounds_bwd", out_shape=jax.ShapeDtypeStruct((depth, n), F32))(logits, dlb)


def _ew_tile(rows, cols, n_arrays):
    cap = max(16, (24 * 1024 * 1024) // (8 * n_arrays * cols))
    best = None
    for t in range(16, rows + 1, 16):
        if rows % t == 0 and t <= cap:
            best = t
    return rows if best is None else best


def _adamw_math(w, g, m, v):
    mn = ADAM_B1 * m + (1.0 - ADAM_B1) * g
    vn = ADAM_B2 * v + (1.0 - ADAM_B2) * (g * g)
    m_hat = mn / (1.0 - ADAM_B1 ** ADAM_STEP)
    v_hat = vn / (1.0 - ADAM_B2 ** ADAM_STEP)
    return -ADAM_LR * (m_hat / (jnp.sqrt(v_hat) + ADAM_EPS) + ADAM_WD * w), mn, vn


def _adamw_layers(w, m, v, g_mine, g_sibling, name):
    _, r, c_ = w.shape
    tr = _ew_tile(r, c_, 9)

    def body(w_ref, m_ref, v_ref, gm_ref, gs_ref, g_ref, d_ref, mo_ref, vo_ref):
        mine = pl.program_id(0) == lax.axis_index("c")
        g = jnp.where(mine, gm_ref[...], gs_ref[...])
        g_ref[...] = g
        d_ref[...], mo_ref[...], vo_ref[...] = _adamw_math(w_ref[...], g, m_ref[...], v_ref[...])

    lay = pl.BlockSpec((None, tr, c_), lambda l, i: (l, i, 0))
    flat = pl.BlockSpec((tr, c_), lambda l, i: (i, 0))
    return pl.pallas_call(
        body, name="adamw_" + name, grid=(2, r // tr),
        in_specs=[lay, lay, lay, flat, flat], out_specs=[lay] * 4,
        out_shape=[jax.ShapeDtypeStruct(w.shape, F32)] * 4,
        compiler_params=_cp("parallel", "parallel"),
    )(w, m, v, g_mine, g_sibling)


def _adamw(w, g, m, v, name):
    rows, cols = w.shape
    tr = _ew_tile(rows, cols, 7)

    def body(w_ref, g_ref, m_ref, v_ref, d_ref, mo_ref, vo_ref):
        d_ref[...], mo_ref[...], vo_ref[...] = _adamw_math(w_ref[...], g_ref[...], m_ref[...], v_ref[...])

    spec = pl.BlockSpec((tr, cols), lambda i: (i, 0))
    return pl.pallas_call(
        body, name="adamw_" + name, grid=(rows // tr,),
        in_specs=[spec] * 4, out_specs=[spec] * 3,
        out_shape=[jax.ShapeDtypeStruct((rows, cols), F32)] * 3,
        compiler_params=_cp("parallel"),
    )(w, g, m, v)


DEPTH = 2
BIG_SHAPES = {"w_in": ((1024, 6560), 1), "w_conv_out": ((512, 1024), 1), "w_uq": ((256, 768), 1),
              "w_ukv": ((128, 1024), 1), "w_attn_out": ((512, 1024), 1), "w_hgrn_out": ((512, 1024), 1),
              "w_out": ((1024, 1024), 0), "w_ff1": ((1024, 4096), 1), "w_ff2": ((4096, 1024), 0)}
BIG = tuple(BIG_SHAPES)
SMALL_SIZES = {"norm1_g": 1024, "conv_b": 512, "conv_ln_g": 512, "conv_ln_b": 512, "q_a_norm_g": 256,
               "kv_a_norm_g": 128, "q_norm_g": 96, "k_norm_g": 96, "hgrn_lb_logits": 512, "hgrn_norm_g": 512,
               "norm2_g": 1024}
SMALL = tuple(SMALL_SIZES)
W_IN_COLS = 6560
W_IN_SHARD = W_IN_COLS // 4
W_IN_SEGS = ((0, 1024, SEG_AG[0]), (1024, 1280, SEG_CQ[0]), (1280, 1408, SEG_CKV[0]), (1408, 1440, SEG_KR[0] + 64),
             (1440, 3488, SEG_H4[0]), (3488, 6560, SEG_GATES[0]))


def _pad_heads(w, nh, used, axis):
    shp = w.shape
    w = w.reshape(shp[:axis] + (nh, used) + shp[axis + 1:])
    pad = [(0, 0)] * w.ndim
    pad[axis + 1] = (0, LANE - used)
    w = jnp.pad(w, pad)
    return w.reshape(shp[:axis] + (nh * LANE,) + shp[axis + 1:])


def _unpad_heads(w, nh, used, axis):
    shp = w.shape
    w = w.reshape(shp[:axis] + (nh, LANE) + shp[axis + 1:])
    w = lax.slice_in_dim(w, 0, used, axis=axis + 1)
    return w.reshape(shp[:axis] + (nh * used,) + shp[axis + 1:])


def _w_in_from_chips(p4):
    def orig(a, b):
        out = []
        while a < b:
            s = a // W_IN_SHARD
            e = min(b, (s + 1) * W_IN_SHARD)
            out.append(p4[s][:, a - W_IN_SHARD * s:e - W_IN_SHARD * s])
            a = e
        return out
    zc = lambda n: jnp.zeros((D, n), p4[0].dtype)
    parts = (orig(3488, 6560) + orig(0, 1024) + orig(1440, 3488) + orig(1024, 1280) + orig(1280, 1408)
             + [zc(64)] + orig(1408, 1440) + [zc(32)])
    return jnp.concatenate(parts, axis=1)


def _w_in_grad_to_chips(dw):
    chips = []
    for s in range(4):
        a, b = W_IN_SHARD * s, W_IN_SHARD * (s + 1)
        parts = []
        for o0, o1, p0 in W_IN_SEGS:
            lo, hi = max(a, o0), min(b, o1)
            if lo < hi:
                parts.append(dw[:, p0 + lo - o0:p0 + hi - o0])
        chips.append(jnp.concatenate(parts, axis=1))
    return jnp.stack(chips)


def _cat_chips(p4, axis):
    return jnp.concatenate([p4[s] for s in range(4)], axis=axis)


def _prep_layer(pieces, small, l):
    mm = lambda a: a.astype(_MM)
    pc = lambda k: [mm(pieces[k][s][l]) for s in range(4)]
    w_in_p = _w_in_from_chips(pc("w_in"))
    wuq = jnp.concatenate([_pad_heads(pc("w_uq")[s], 2, QK_DIM, 1) for s in range(4)], axis=1)
    wukv = _cat_chips(pc("w_ukv"), 1).reshape(128, NH, 128)
    wk = _pad_heads(wukv[:, :, :64].reshape(128, NH * 64), NH, 64, 1)
    wv = _pad_heads(wukv[:, :, 64:].reshape(128, NH * 64), NH, 64, 1)
    wao = _pad_heads(_cat_chips(pc("w_attn_out"), 1), NH, 64, 0)
    row = lambda a: a.astype(F32).reshape(1, -1)
    p = dict(
        w_in=w_in_p, wuq=wuq, wk=wk, wv=wv, wao=wao, wco=_cat_chips(pc("w_conv_out"), 1), who=_cat_chips(pc("w_hgrn_out"), 1),
        wout=_cat_chips(pc("w_out"), 0), w1=_cat_chips(pc("w_ff1"), 1), w2=_cat_chips(pc("w_ff2"), 0),
        g1=row(small["norm1_g"][l]), g2=row(small["norm2_g"][l]),
        cw=jnp.pad(small["conv_w"][l].astype(F32), ((0, 1), (0, 0))), cb=row(small["conv_b"][l]),
        lng=row(small["conv_ln_g"][l]), lnb=row(small["conv_ln_b"][l]),
        qag=row(small["q_a_norm_g"][l]), kvag=row(small["kv_a_norm_g"][l]),
        qng=jnp.pad(row(small["q_norm_g"][l]), ((0, 0), (0, LANE - QK_DIM))),
        kng=jnp.pad(row(small["k_norm_g"][l]), ((0, 0), (0, LANE - QK_DIM))),
        ng=row(small["hgrn_norm_g"][l]),
    )
    return p


def _rope_tables(t):
    pos = (jnp.arange(t, dtype=jnp.int32) - FRONT).astype(F32)
    inv_freq = 10000.0 ** (-jnp.arange(16, dtype=F32) / 16)
    ang = pos[:, None] * inv_freq[None, :]
    cos, sin = jnp.cos(ang), jnp.sin(ang)
    one = jnp.ones((t, 64), F32)
    z16, z32, z64 = jnp.zeros((t, 16), F32), jnp.zeros((t, 32), F32), jnp.zeros((t, 64), F32)
    c = jnp.concatenate([one, cos, cos, z32], axis=1)
    s1 = jnp.concatenate([z64, -sin, z16, z32], axis=1)
    s2 = jnp.concatenate([z64, z16, sin, z32], axis=1)
    return c, s1, s2


def _layer_fwd(x, p, lb, rope, t_end):
    gates, ag, h4, cq, ckv, kr, hb = _in_proj_fwd(x, p["g1"], p["w_in"])
    z = _conv_fwd(ag, p["cw"], p["cb"])
    q, k, v, cqn, ckvn = _mla_pre_fwd(cq, ckv, kr, p["qag"], p["wuq"], p["kvag"], p["wk"], p["wv"], p["qng"],
                                      p["kng"], *rope)
    tq = _attn_tile(x.shape[0])
    ot, lse = _attn_fwd(q, k, _to_head_blocks(v, tq))
    o_att = _from_head_blocks(ot)
    o_h, states = _hgrn_fwd(h4, lb)
    x1, mix, ca, oc, ya, yb, yc = _mix_fwd(x, z, o_att, o_h, h4, gates, p["lng"], p["lnb"], p["wco"], p["wao"],
                                           p["ng"], p["who"], p["wout"], t_end)
    x2, pre = _ffn_fwd(x1, p["g2"], p["w1"], p["w2"])
    saved = dict(x=x, gates=gates, ag=ag, h4=h4, cq=cq, ckv=ckv, kr=kr, hb=hb, z=z, q=q, k=k, v=v, cqn=cqn,
                 ckvn=ckvn, o_att=o_att, ot=ot, lse=lse, o_h=o_h, states=states, x1=x1, mix=mix, ca=ca, oc=oc,
                 ya=ya, yb=yb, yc=yc, pre=pre)
    return x2, saved


def _layer_bwd(dx2, s, p, lb, rope, t_end):
    dx1, h2, act, dp, dg2 = _ffn_bwd(dx2, s["x1"], s["pre"], p["g2"], p["w1"], p["w2"])
    g = {"w_ff1": _wgrad(h2, dp, "ff1", 4), "w_ff2": _wgrad(act, dx2, "ff2").reshape(4, D_FF // 4, D),
         "norm2_g": dg2.sum(0)}
    (dgt, dya, dyb, dyc, dz, doa, doh, dhg, dlng, dlnb, dcb, dng) = _mix_bwd(
        dx1, s["ya"], s["yb"], s["yc"], s["gates"], s["z"], s["o_h"], s["h4"], p["lng"], p["lnb"], p["ng"],
        p["wout"], p["wco"], p["wao"], p["who"])
    g["w_out"] = _wgrad(s["mix"], dx1, "out").reshape(4, D // 4, D)
    g["w_conv_out"] = _wgrad(s["ca"], dya, "conv_out", 4)
    g["w_attn_out"] = _unpad_heads(_wgrad(s["o_att"], dyb, "attn_out", 4), NH, 64, 1)
    g["w_hgrn_out"] = _wgrad(s["oc"], dyc, "hgrn_out", 4)
    g["conv_ln_g"], g["conv_ln_b"], g["conv_b"], g["hgrn_norm_g"] = dlng.sum(0), dlnb.sum(0), dcb.sum(0), dng.sum(0)
    da, dg, dcw = _conv_bwd(s["ag"], p["cw"], dz)
    g["conv_w"] = dcw[:CONV_K]
    tq = _attn_tile(dx2.shape[0])
    dq, dk, dv = [_from_head_blocks(a) for a in _attn_bwd(
        s["q"], s["k"], s["v"], doa.astype(_MM), _to_head_blocks(s["q"], tq), _to_head_blocks(s["k"], tq),
        _to_head_blocks(doa, tq), s["ot"], s["lse"])]
    dcq, dckv, dkr, dqraw, dkraw, dqag, dkvag, dqng, dkng = _mla_pre_bwd(
        dq, dk, dv, s["cq"], s["ckv"], s["kr"], p["qag"], p["wuq"], p["kvag"], p["wk"], p["wv"], p["qng"],
        p["kng"], *rope)
    g["w_uq"] = _unpad_heads(_wgrad(s["cqn"], dqraw, "uq", 4), 2, QK_DIM, 2)
    dwk = _unpad_heads(_wgrad(s["ckvn"], dkraw, "uk"), NH, 64, 1).reshape(128, NH, 64)
    dwv = _unpad_heads(_wgrad(s["ckvn"], dv, "uv"), NH, 64, 1).reshape(128, NH, 64)
    g["w_ukv"] = jnp.concatenate([dwk, dwv], axis=2).reshape(128, 4, 256).transpose(1, 0, 2)
    g["q_a_norm_g"], g["kv_a_norm_g"] = dqag.sum(0), dkvag.sum(0)
    g["q_norm_g"], g["k_norm_g"] = dqng.sum(0)[:QK_DIM], dkng.sum(0)[:QK_DIM]
    dhq, dhf, dhi, dlb = _hgrn_bwd(s["h4"], lb, doh, s["states"])
    mm = lambda a: a.astype(_MM)
    du = jnp.concatenate([dgt, mm(da), mm(dg), mm(dhq), mm(dhf), mm(dhi), dhg, dcq, dckv, dkr], axis=1)
    dx, dg1 = _in_proj_bwd(du, s["x"], dx1, p["g1"], p["w_in"], t_end)
    g["norm1_g"] = dg1.sum(0)
    g["w_in"] = _w_in_grad_to_chips(_wgrad(s["hb"], du, "in"))
    return dx, g, dlb.sum(0)


def _device_step(x, target, pieces, small):
    s_real = x.shape[0]
    t_end = ROW0 + s_real
    t = -(-t_end // LANE) * LANE
    zrow = lambda n: jnp.zeros((n, D), F32)
    xp = jnp.concatenate([zrow(FRONT), small["meta"].astype(F32), x, zrow(t - t_end)], axis=0)
    tp = jnp.concatenate([zrow(ROW0), target, zrow(t - t_end)], axis=0)
    rope = _rope_tables(t)
    logits = small["hgrn_lb_logits"].astype(F32)
    lbs = _lower_bounds_fwd(logits)
    prm = [_prep_layer(pieces, small, l) for l in range(DEPTH)]
    saved = []
    h = xp
    for l in range(DEPTH):
        h, sv = _layer_fwd(h, prm[l], lbs[l:l + 1], rope, t_end)
        saved.append(sv)
    dh, lsum = _loss_head(h, tp, t_end)
    loss = jnp.sum(lsum)
    grads = [None] * DEPTH
    dlbs = [None] * DEPTH
    for l in reversed(range(DEPTH)):
        dh, grads[l], dlbs[l] = _layer_bwd(dh, saved[l], prm[l], lbs[l:l + 1], rope, t_end)
    dlogits = _lower_bounds_bwd(logits, jnp.stack(dlbs))
    for l in range(DEPTH):
        grads[l]["hgrn_lb_logits"] = dlogits[l]
    return loss, dh[ROW0:t_end], grads, dh[FRONT:ROW0]


MESH = pl.DeviceIdType.MESH
_ANY = pl.BlockSpec(memory_space=pl.ANY)
SMALL_ROWS = 64
SMALL_LEN = SMALL_ROWS * 1024


def _mesh_pos():
    return lax.axis_index("x"), lax.axis_index("y"), lax.axis_index("c")


def _other_chips(x, y):
    return [(1 - x, y), (x, 1 - y), (1 - x, 1 - y)]


def _all_gather_weights(shards):
    nw = len(shards)

    def body(*refs):
        srcs, outs = refs[:nw], refs[nw:2 * nw]
        send_sems, recv_sems = refs[2 * nw:]
        x, y, c = _mesh_pos()
        me, sib = (x, y, c), (x, y, 1 - c)
        chips = _other_chips(x, y)

        def copy(w, k, blk, to, from_src=False):
            px, py, pc = blk
            slot = outs[w].at[2 * px + py, pc]
            return pltpu.make_async_remote_copy(
                src_ref=srcs[w].at[c] if from_src else slot, dst_ref=slot,
                send_sem=send_sems.at[w, k], recv_sem=recv_sems.at[w, k], device_id=to, device_id_type=MESH)

        first = []
        for j, ch in enumerate(chips):
            for w in range(nw):
                first.append(copy(w, j, me, (*ch, c), True))
        for cp in first:
            cp.start()
        passed = []
        for j, ch in enumerate(chips):
            for w in range(nw):
                copy(w, j, (*ch, c), me).wait_recv()
                fwd = copy(w, 3 + j, (*ch, c), sib)
                fwd.start()
                passed.append(fwd)
        for j, ch in enumerate(chips):
            for w in range(nw):
                copy(w, 3 + j, (*ch, 1 - c), me).wait_recv()
        for cp in first + passed:
            cp.wait_send()

    return pl.pallas_call(
        body, name="all_gather_weights", in_specs=[_ANY] * nw, out_specs=[_ANY] * nw,
        out_shape=[jax.ShapeDtypeStruct((4,) + a.shape, a.dtype) for a in shards],
        scratch_shapes=[pltpu.SemaphoreType.DMA((nw, 6)), pltpu.SemaphoreType.DMA((nw, 6))],
    )(*shards)


def _sibling_swap(g0, g1):
    nw = len(g0)

    def body(*refs):
        a0, a1, outs = refs[:nw], refs[nw:2 * nw], refs[2 * nw:3 * nw]
        send_sems, recv_sems = refs[3 * nw:]
        x, y, c = _mesh_pos()

        def copy(w, src):
            return pltpu.make_async_remote_copy(src_ref=src, dst_ref=outs[w], send_sem=send_sems.at[w],
                                                recv_sem=recv_sems.at[w], device_id=(x, y, 1 - c),
                                                device_id_type=MESH)

        @pl.when(c == 0)
        def _():
            for w in range(nw):
                copy(w, a1[w]).start()

        @pl.when(c == 1)
        def _():
            for w in range(nw):
                copy(w, a0[w]).start()

        for w in range(nw):
            copy(w, a0[w]).wait()

    return pl.pallas_call(
        body, name="sibling_swap", in_specs=[_ANY] * (2 * nw), out_specs=[_ANY] * nw,
        out_shape=[jax.ShapeDtypeStruct(a.shape, a.dtype) for a in g0],
        scratch_shapes=[pltpu.SemaphoreType.DMA((nw,)), pltpu.SemaphoreType.DMA((nw,))],
    )(*g0, *g1)


def _chip_scatter(parts):
    nw = len(parts)

    def body(*refs):
        srcs, outs = refs[:nw], refs[nw:2 * nw]
        send_sems, recv_sems = refs[2 * nw:]
        x, y, c = _mesh_pos()
        me = 2 * x + y
        chips = _other_chips(x, y)
        cps = []
        for j, (px, py) in enumerate(chips):
            for w in range(nw):
                cps.append(pltpu.make_async_remote_copy(
                    src_ref=srcs[w].at[2 * px + py], dst_ref=outs[w].at[me], send_sem=send_sems.at[w, j],
                    recv_sem=recv_sems.at[w, j], device_id=(px, py, c), device_id_type=MESH))
        for cp in cps:
            cp.start()
        for j, (px, py) in enumerate(chips):
            for w in range(nw):
                pltpu.make_async_remote_copy(
                    src_ref=srcs[w].at[me], dst_ref=outs[w].at[2 * px + py], send_sem=send_sems.at[w, j],
                    recv_sem=recv_sems.at[w, j], device_id=(x, y, c), device_id_type=MESH).wait_recv()
        for cp in cps:
            cp.wait_send()

    return pl.pallas_call(
        body, name="chip_scatter", in_specs=[_ANY] * nw, out_specs=[_ANY] * nw,
        out_shape=[jax.ShapeDtypeStruct(a.shape, a.dtype) for a in parts],
        scratch_shapes=[pltpu.SemaphoreType.DMA((nw, 3)), pltpu.SemaphoreType.DMA((nw, 3))],
    )(*parts)


def _sibling_exchange(reds):
    nw = len(reds)

    def body(*refs):
        srcs, outs = refs[:nw], refs[nw:2 * nw]
        send_sems, recv_sems = refs[2 * nw:]
        x, y, c = _mesh_pos()
        cps = [pltpu.make_async_remote_copy(src_ref=srcs[w], dst_ref=outs[w], send_sem=send_sems.at[w],
                                            recv_sem=recv_sems.at[w], device_id=(x, y, 1 - c), device_id_type=MESH)
               for w in range(nw)]
        for cp in cps:
            cp.start()
        for cp in cps:
            cp.wait()

    return pl.pallas_call(
        body, name="sibling_exchange", in_specs=[_ANY] * nw, out_specs=[_ANY] * nw,
        out_shape=[jax.ShapeDtypeStruct(a.shape, a.dtype) for a in reds],
        scratch_shapes=[pltpu.SemaphoreType.DMA((nw,)), pltpu.SemaphoreType.DMA((nw,))],
    )(*reds)


def _all_reduce_small(v, name):
    rows, cols = v.shape

    def body(v_ref, o_ref, slots, send_sems, recv_sems):
        x, y, c = _mesh_pos()
        me = 4 * x + 2 * y + c
        slots[me] = v_ref[...]
        peers = []
        for rel in range(1, 8):
            fx, fy, fc = (rel >> 2) & 1, (rel >> 1) & 1, rel & 1
            px = 1 - x if fx else x
            py = 1 - y if fy else y
            pc = 1 - c if fc else c
            peers.append((px, py, pc))
        cps = [pltpu.make_async_remote_copy(src_ref=v_ref, dst_ref=slots.at[me], send_sem=send_sems.at[k],
                                            recv_sem=recv_sems.at[k], device_id=peer, device_id_type=MESH)
               for k, peer in enumerate(peers)]
        for cp in cps:
            cp.start()
        for k, (px, py, pc) in enumerate(peers):
            pltpu.make_async_remote_copy(src_ref=v_ref, dst_ref=slots.at[4 * px + 2 * py + pc],
                                         send_sem=send_sems.at[k], recv_sem=recv_sems.at[k], device_id=(x, y, c),
                                         device_id_type=MESH).wait_recv()
        for cp in cps:
            cp.wait_send()
        acc = slots[0]
        for d in range(1, 8):
            acc = acc + slots[d]
        o_ref[...] = acc

    vm = pl.BlockSpec(memory_space=pltpu.VMEM)
    return pl.pallas_call(
        body, name=name, in_specs=[vm], out_specs=vm,
        out_shape=jax.ShapeDtypeStruct((rows, cols), F32),
        scratch_shapes=[pltpu.VMEM((8, rows, cols), F32), pltpu.SemaphoreType.DMA((7,)),
                        pltpu.SemaphoreType.DMA((7,))],
    )(v)


def _add_to_wire(a, b, name):
    n4, r, c_ = a.shape
    rows = n4 * r
    tr = _ew_tile(rows, c_, 3)

    def body(a_ref, b_ref, o_ref):
        o_ref[...] = (a_ref[...] + b_ref[...]).astype(o_ref.dtype)

    spec = pl.BlockSpec((tr, c_), lambda i: (i, 0))
    out = pl.pallas_call(
        body, name="add_to_wire_" + name, grid=(rows // tr,), in_specs=[spec, spec], out_specs=spec,
        out_shape=jax.ShapeDtypeStruct((rows, c_), jnp.bfloat16), compiler_params=_cp("parallel"),
    )(a.reshape(rows, c_), b.reshape(rows, c_))
    return out.reshape(n4, r, c_)


def _sum_chips(recv, own, name):
    _, r, c_ = recv.shape
    tr = _ew_tile(r, c_, 4)

    def body(r_ref, own_ref, o_ref):
        chip = 2 * lax.axis_index("x") + lax.axis_index("y")
        own_v = own_ref[...].astype(F32)
        acc = None
        for s in range(4):
            term = jnp.where(chip == s, own_v, r_ref[s].astype(F32))
            acc = term if acc is None else acc + term
        o_ref[...] = acc

    return pl.pallas_call(
        body, name="sum_chips_" + name, grid=(r // tr,),
        in_specs=[pl.BlockSpec((4, tr, c_), lambda i: (0, i, 0)), pl.BlockSpec((tr, c_), lambda i: (i, 0))],
        out_specs=pl.BlockSpec((tr, c_), lambda i: (i, 0)),
        out_shape=jax.ShapeDtypeStruct((r, c_), F32),
        compiler_params=_cp("parallel"),
    )(recv, own)


def _pack_small(vals, meta_full, conv_w_full):
    flat = jnp.concatenate([vals[k].reshape(-1) for k in SMALL] + [meta_full.reshape(-1), conv_w_full.reshape(-1)])
    return jnp.pad(flat, (0, SMALL_LEN - flat.shape[0])).reshape(SMALL_ROWS, 1024)


def _unpack_small(buf):
    flat = buf.reshape(-1)
    out, off = {}, 0
    for k in SMALL:
        n = DEPTH * SMALL_SIZES[k]
        out[k] = flat[off:off + n].reshape(DEPTH, SMALL_SIZES[k])
        off += n
    meta = flat[off:off + N_META * D].reshape(N_META, D)
    off += N_META * D
    conv_w = flat[off:off + DEPTH * CONV_K * CONV_DIM].reshape(DEPTH, CONV_K, CONV_DIM)
    return out, meta, conv_w


def kernel(x, meta, norm1_g, w_in, conv_w, conv_b, conv_ln_g, conv_ln_b, w_conv_out, q_a_norm_g, w_uq, kv_a_norm_g, w_ukv, q_norm_g, k_norm_g, w_attn_out, hgrn_lb_logits, hgrn_norm_g, w_hgrn_out, w_out, norm2_g, w_ff1, w_ff2, loss_target, m_meta, m_norm1_g, m_w_in, m_conv_w, m_conv_b, m_conv_ln_g, m_conv_ln_b, m_w_conv_out, m_q_a_norm_g, m_w_uq, m_kv_a_norm_g, m_w_ukv, m_q_norm_g, m_k_norm_g, m_w_attn_out, m_hgrn_lb_logits, m_hgrn_norm_g, m_w_hgrn_out, m_w_out, m_norm2_g, m_w_ff1, m_w_ff2, v_meta, v_norm1_g, v_w_in, v_conv_w, v_conv_b, v_conv_ln_g, v_conv_ln_b, v_w_conv_out, v_q_a_norm_g, v_w_uq, v_kv_a_norm_g, v_w_ukv, v_q_norm_g, v_k_norm_g, v_w_attn_out, v_hgrn_lb_logits, v_hgrn_norm_g, v_w_hgrn_out, v_w_out, v_norm2_g, v_w_ff1, v_w_ff2):
    names = ("meta", "norm1_g", "w_in", "conv_w", "conv_b", "conv_ln_g", "conv_ln_b", "w_conv_out", "q_a_norm_g",
             "w_uq", "kv_a_norm_g", "w_ukv", "q_norm_g", "k_norm_g", "w_attn_out", "hgrn_lb_logits", "hgrn_norm_g",
             "w_hgrn_out", "w_out", "norm2_g", "w_ff1", "w_ff2")
    w = dict(zip(names, (meta, norm1_g, w_in, conv_w, conv_b, conv_ln_g, conv_ln_b, w_conv_out, q_a_norm_g, w_uq,
                         kv_a_norm_g, w_ukv, q_norm_g, k_norm_g, w_attn_out, hgrn_lb_logits, hgrn_norm_g, w_hgrn_out,
                         w_out, norm2_g, w_ff1, w_ff2)))
    m = dict(zip(names, (m_meta, m_norm1_g, m_w_in, m_conv_w, m_conv_b, m_conv_ln_g, m_conv_ln_b, m_w_conv_out,
                         m_q_a_norm_g, m_w_uq, m_kv_a_norm_g, m_w_ukv, m_q_norm_g, m_k_norm_g, m_w_attn_out,
                         m_hgrn_lb_logits, m_hgrn_norm_g, m_w_hgrn_out, m_w_out, m_norm2_g, m_w_ff1, m_w_ff2)))
    v = dict(zip(names, (v_meta, v_norm1_g, v_w_in, v_conv_w, v_conv_b, v_conv_ln_g, v_conv_ln_b, v_w_conv_out,
                         v_q_a_norm_g, v_w_uq, v_kv_a_norm_g, v_w_ukv, v_q_norm_g, v_k_norm_g, v_w_attn_out,
                         v_hgrn_lb_logits, v_hgrn_norm_g, v_w_hgrn_out, v_w_out, v_norm2_g, v_w_ff1, v_w_ff2)))
    cx, cy, cc = _mesh_pos()
    chip = 2 * cx + cy
    zero = jnp.zeros((), jnp.int32)

    own = [w[k].astype(_MM) for k in BIG]
    gathered = _all_gather_weights(own)
    pieces = {k: [[jnp.where(chip == s, o[l], g[s, l]) for l in range(DEPTH)] for s in range(4)]
              for k, o, g in zip(BIG, own, gathered)}
    meta_slab = lax.dynamic_update_slice(jnp.zeros((N_META, D), F32), meta, (zero, chip * (D // 4)))
    convw_slab = lax.dynamic_update_slice(jnp.zeros((DEPTH, CONV_K, CONV_DIM), F32), conv_w,
                                          (zero, zero, chip * (CONV_DIM // 4)))
    zsmall = {k: jnp.zeros((DEPTH, SMALL_SIZES[k]), F32) for k in SMALL}
    south = (cc == 0).astype(F32)
    _, meta_full, convw_full = _unpack_small(
        _all_reduce_small(_pack_small(zsmall, meta_slab, convw_slab) * south, "gather_small"))
    small = {k: w[k] for k in SMALL}
    small["meta"] = meta_full
    small["conv_w"] = convw_full

    loss_share, grad_x, gl, g_meta = _device_step(x[0], loss_target[0], pieces, small)
    loss = lax.psum(loss_share, ("x", "y", "c"))

    g0 = [gl[0][k] for k in BIG]
    g1 = [gl[1][k] for k in BIG]
    from_sibling = _sibling_swap(g0, g1)

    def chip_sums(mine):
        return [_add_to_wire(a, b, k) for k, a, b in zip(BIG, mine, from_sibling)]

    wire = lax.cond(cc == 0, lambda: chip_sums(g0), lambda: chip_sums(g1))
    reds = [_sum_chips(r, lax.dynamic_index_in_dim(s, chip, 0, keepdims=False), k)
            for k, r, s in zip(BIG, _chip_scatter(wire), wire)]
    reds_sibling = _sibling_exchange(reds)
    grads, delta, new_m, new_v = {}, {}, {}, {}
    for k, mine, theirs in zip(BIG, reds, reds_sibling):
        grads[k], delta[k], new_m[k], new_v[k] = _adamw_layers(w[k], m[k], v[k], mine, theirs, k)

    g_small_local = {k: jnp.stack([gl[l][k] for l in range(DEPTH)]) for k in SMALL}
    g_convw_local = jnp.stack([gl[l]["conv_w"] for l in range(DEPTH)])
    g_small, g_meta_full, g_convw_full = _unpack_small(
        _all_reduce_small(_pack_small(g_small_local, g_meta, g_convw_local), "reduce_small"))
    grads.update(g_small)
    grads["meta"] = lax.dynamic_slice(g_meta_full, (zero, chip * (D // 4)), (N_META, D // 4))
    grads["conv_w"] = lax.dynamic_slice(g_convw_full, (zero, zero, chip * (CONV_DIM // 4)),
                                        (DEPTH, CONV_K, CONV_DIM // 4))

    def small_pack(src):
        return _pack_small(src, jnp.pad(src["meta"], ((0, 0), (0, D - D // 4))),
                           jnp.pad(src["conv_w"], ((0, 0), (0, 0), (0, CONV_DIM - CONV_DIM // 4))))

    def small_unpack(buf):
        out, meta_p, convw_p = _unpack_small(buf)
        out["meta"] = meta_p[:, :D // 4]
        out["conv_w"] = convw_p[:, :, :CONV_DIM // 4]
        return out

    d_s, m_s, v_s = [small_unpack(a) for a in _adamw(small_pack(w), small_pack(grads), small_pack(m),
                                                     small_pack(v), "small")]
    delta.update(d_s)
    new_m.update(m_s)
    new_v.update(v_s)
    return (loss, grad_x[None], *[grads[k] for k in names], *[delta[k] for k in names],
            *[new_m[k] for k in names], *[new_v[k] for k in names])
```

```python
import functools

import jax
import jax.numpy as jnp
from jax import lax
from jax.experimental import pallas as pl
from jax.experimental.pallas import tpu as pltpu

F32 = jnp.float32
_MM = jnp.bfloat16

D = 1024
N_META = 16
FRONT = 48
ROW0 = FRONT + N_META
EPS = 1e-6
GATE_CLAMP = 1.0 - 1e-6
CONV_K = 31
CONV_DIM = 512
NH = 8
QK_DIM = 96
ATT_SCALE = QK_DIM ** -0.5
HH = 4
CHUNK = 64
SUB = 16
EXP_CLIP = 60.0
NEG = -1e30
LANE = 128

SEG_GATES = (0, 3072)
SEG_AG = (3072, 4096)
SEG_H4 = (4096, 6144)
SEG_CQ = (6144, 6400)
SEG_CKV = (6400, 6528)
SEG_KR = (6528, 6656)
N_IN_P = 6656

ADAM_LR = 0.001
ADAM_B1 = 0.9
ADAM_B2 = 0.999
ADAM_EPS = 1e-08
ADAM_WD = 0.01
ADAM_STEP = 10

VMEM_LIMIT = 56 * 1024 * 1024


def _tile(n, pref):
    best = 64
    for t in range(64, pref + 1, 64):
        if n % t == 0:
            best = t
    return best


def _cp(*sem):
    return pltpu.CompilerParams(dimension_semantics=tuple(sem), vmem_limit_bytes=VMEM_LIMIT)


def _row(tm, n, col=0):
    return pl.BlockSpec((tm, n), lambda i: (i, col))


def _full(shape):
    return pl.BlockSpec(shape, lambda i: (0,) * len(shape))


def _mm(a, b):
    return jnp.dot(a.astype(_MM), b.astype(_MM), preferred_element_type=F32)


def _mm_nt(a, b):
    return lax.dot_general(a.astype(_MM), b.astype(_MM), (((1,), (1,)), ((), ())), preferred_element_type=F32)


def _mm_tn(a, b):
    return lax.dot_general(a.astype(_MM), b.astype(_MM), (((0,), (0,)), ((), ())), preferred_element_type=F32)


def _split3(x):
    hi = x.astype(jnp.bfloat16)
    return hi, (x - hi.astype(F32)).astype(jnp.bfloat16)


def _dot3(a, b, dims):
    ah, al = _split3(a)
    bh, bl = _split3(b)
    dg = lambda u, v: lax.dot_general(u, v, (dims, ((), ())), preferred_element_type=F32)
    return dg(ah, bh) + (dg(ah, bl) + dg(al, bh))


def _hmm(a, b):
    return _dot3(a, b, ((1,), (0,)))


def _hmm_nt(a, b):
    return _dot3(a, b, ((1,), (1,)))


def _hmm_tn(a, b):
    return _dot3(a, b, ((0,), (0,)))


def _sigmoid(x):
    return 1.0 / (1.0 + jnp.exp(-x))


def _rstd(x, n=None):
    n = x.shape[-1] if n is None else n
    return lax.rsqrt(jnp.sum(x * x, axis=-1, keepdims=True) * (1.0 / n) + EPS)


def _rms_bwd(dy, x, rstd, g, n=None):
    n = x.shape[-1] if n is None else n
    xh = x * rstd
    dxh = dy * g
    dx = rstd * (dxh - xh * (jnp.sum(dxh * xh, axis=-1, keepdims=True) * (1.0 / n)))
    return dx, dy * xh


def _valid_rows(i, tm, t_valid_end):
    r = i * tm + lax.broadcasted_iota(jnp.int32, (tm, 1), 0)
    return ((r >= FRONT) & (r < t_valid_end)).astype(F32)


def _colsum8(x):
    n, c = x.shape
    return jnp.sum(x.reshape(n // 8, 8, c), axis=0)


def _in_proj_fwd(x, g1, w):
    t = x.shape[0]
    tm = _tile(t, 192)
    segs = (SEG_GATES, SEG_AG, SEG_H4, SEG_CQ, SEG_CKV, SEG_KR)

    def body(x_ref, g_ref, w_ref, gates_ref, ag_ref, h4_ref, cq_ref, ckv_ref, kr_ref, hb_ref):
        xv = x_ref[...]
        hb = (xv * _rstd(xv) * g_ref[...]).astype(_MM)
        hb_ref[...] = hb
        for ref, (a, b) in zip((gates_ref, ag_ref, h4_ref, cq_ref, ckv_ref, kr_ref), segs):
            ref[...] = jnp.dot(hb, w_ref[:, a:b], preferred_element_type=F32)

    outs = [jax.ShapeDtypeStruct((t, b - a), F32) for a, b in segs] + [jax.ShapeDtypeStruct((t, D), _MM)]
    return pl.pallas_call(
        body, name="in_proj_fwd", grid=(t // tm,),
        in_specs=[_row(tm, D), _full((1, D)), _full((D, N_IN_P))],
        out_specs=[_row(tm, b - a) for a, b in segs] + [_row(tm, D)],
        out_shape=outs, compiler_params=_cp("parallel"),
    )(x, g1, w)


def _in_proj_bwd(du, x, dx1, g1, wt, t_end):
    t = x.shape[0]
    tm = _tile(t, 192)

    def body(du_ref, x_ref, dx1_ref, g_ref, wt_ref, dx_ref, dg_ref):
        i = pl.program_id(0)
        dh = jnp.dot(du_ref[...], wt_ref[...], preferred_element_type=F32)
        xv = x_ref[...]
        dxn, dgrow = _rms_bwd(dh, xv, _rstd(xv), g_ref[...])
        dx_ref[...] = _valid_rows(i, tm, t_end) * (dx1_ref[...] + dxn)

        @pl.when(i == 0)
        def _():
            dg_ref[...] = jnp.zeros_like(dg_ref)
        dg_ref[...] += _colsum8(dgrow)

    return pl.pallas_call(
        body, name="in_proj_bwd", grid=(t // tm,),
        in_specs=[_row(tm, N_IN_P), _row(tm, D), _row(tm, D), _full((1, D)), _full((N_IN_P, D))],
        out_specs=[_row(tm, D), _full((8, D))],
        out_shape=[jax.ShapeDtypeStruct((t, D), F32), jax.ShapeDtypeStruct((8, D), F32)],
        compiler_params=_cp("arbitrary"),
    )(du, x, dx1, g1, wt)


CONV_CH = 128


def _conv_fwd(ag, cw, cb):
    t = ag.shape[0]
    n = t // CONV_CH

    def body(a_ref, g_ref, w_ref, b_ref, z_ref, hp):
        hp[0:32, :] = jnp.zeros((32, LANE), F32)

        def fill(i, c):
            r = pl.multiple_of(i * CONV_CH, CONV_CH)
            hp[pl.ds(32 + r, CONV_CH), :] = a_ref[pl.ds(r, CONV_CH), :] * _sigmoid(g_ref[pl.ds(r, CONV_CH), :])
            return c
        lax.fori_loop(0, n, fill, 0)

        def conv(i, c):
            r = pl.multiple_of(i * CONV_CH, CONV_CH)
            acc = jnp.broadcast_to(b_ref[...], (CONV_CH, LANE))
            for k in range(CONV_K):
                acc = acc + w_ref[k:k + 1, :] * hp[pl.ds(r + (k + 2), CONV_CH), :]
            z_ref[pl.ds(r, CONV_CH), :] = acc
            return c
        lax.fori_loop(0, n, conv, 0)

    nb = CONV_DIM // LANE
    return pl.pallas_call(
        body, name="conv_fwd", grid=(nb,),
        in_specs=[pl.BlockSpec((t, LANE), lambda j: (0, j)), pl.BlockSpec((t, LANE), lambda j: (0, nb + j)),
                  pl.BlockSpec((32, LANE), lambda j: (0, j)), pl.BlockSpec((1, LANE), lambda j: (0, j))],
        out_specs=pl.BlockSpec((t, LANE), lambda j: (0, j)),
        out_shape=jax.ShapeDtypeStruct((t, CONV_DIM), F32),
        scratch_shapes=[pltpu.VMEM((t + 32, LANE), F32)],
        compiler_params=_cp("parallel"),
    )(ag, ag, cw, cb)


def _conv_bwd(ag, cw, dz):
    t = ag.shape[0]
    n = t // CONV_CH

    def body(a_ref, g_ref, w_ref, dz_ref, da_ref, dg_ref, dcw_ref, hp, dzp, accw):
        hp[0:32, :] = jnp.zeros((32, LANE), F32)
        dzp[pl.ds(t, 32), :] = jnp.zeros((32, LANE), F32)
        accw[...] = jnp.zeros_like(accw)

        def fill(i, c):
            r = pl.multiple_of(i * CONV_CH, CONV_CH)
            hp[pl.ds(32 + r, CONV_CH), :] = a_ref[pl.ds(r, CONV_CH), :] * _sigmoid(g_ref[pl.ds(r, CONV_CH), :])
            dzp[pl.ds(r, CONV_CH), :] = dz_ref[pl.ds(r, CONV_CH), :]
            return c
        lax.fori_loop(0, n, fill, 0)

        def step(i, c):
            r = pl.multiple_of(i * CONV_CH, CONV_CH)
            dzc = dz_ref[pl.ds(r, CONV_CH), :]
            dh = jnp.zeros((CONV_CH, LANE), F32)
            for k in range(CONV_K):
                dh = dh + w_ref[k:k + 1, :] * dzp[pl.ds(r + (CONV_K - 1 - k), CONV_CH), :]
                accw[8 * k:8 * k + 8, :] += _colsum8(dzc * hp[pl.ds(r + (k + 2), CONV_CH), :])
            a = a_ref[pl.ds(r, CONV_CH), :]
            sg = _sigmoid(g_ref[pl.ds(r, CONV_CH), :])
            da_ref[pl.ds(r, CONV_CH), :] = dh * sg
            dg_ref[pl.ds(r, CONV_CH), :] = dh * a * sg * (1.0 - sg)
            return c
        lax.fori_loop(0, n, step, 0)

        for k in range(CONV_K):
            dcw_ref[k:k + 1, :] = jnp.sum(accw[8 * k:8 * k + 8, :], axis=0, keepdims=True)
        dcw_ref[CONV_K:32, :] = jnp.zeros((32 - CONV_K, LANE), F32)

    nb = CONV_DIM // LANE
    colspec = pl.BlockSpec((t, LANE), lambda j: (0, j))
    return pl.pallas_call(
        body, name="conv_bwd", grid=(nb,),
        in_specs=[colspec, pl.BlockSpec((t, LANE), lambda j: (0, nb + j)),
                  pl.BlockSpec((32, LANE), lambda j: (0, j)), colspec],
        out_specs=[colspec, colspec, pl.BlockSpec((32, LANE), lambda j: (0, j))],
        out_shape=[jax.ShapeDtypeStruct((t, CONV_DIM), F32), jax.ShapeDtypeStruct((t, CONV_DIM), F32),
                   jax.ShapeDtypeStruct((32, CONV_DIM), F32)],
        scratch_shapes=[pltpu.VMEM((t + 32, LANE), F32), pltpu.VMEM((t + 32, LANE), F32),
                        pltpu.VMEM((8 * 32, LANE), F32)],
        compiler_params=_cp("parallel"),
    )(ag, ag, cw, dz)


def _rope(x, c, s1, s2):
    return x * c + pltpu.roll(x, LANE - 16, 1) * s1 + pltpu.roll(x, 16, 1) * s2


def _rope_t(dy, c, s1, s2):
    return dy * c + pltpu.roll(dy * s1, 16, 1) + pltpu.roll(dy * s2, LANE - 16, 1)


def _mla_pre_fwd(cq, ckv, kr, qag, wuq, kvag, wk, wv, qng, kng, rc, rs1, rs2):
    t = cq.shape[0]
    tm = _tile(t, 384)

    def body(cq_ref, ckv_ref, kr_ref, qag_ref, wuq_ref, kvag_ref, wk_ref, wv_ref, qng_ref, kng_ref,
             c_ref, s1_ref, s2_ref, q_ref, k_ref, v_ref, cqn_ref, ckvn_ref):
        cqv = cq_ref[...]
        cqn = (cqv * _rstd(cqv) * qag_ref[...]).astype(_MM)
        cqn_ref[...] = cqn
        ckvv = ckv_ref[...]
        ckvn = (ckvv * _rstd(ckvv) * kvag_ref[...]).astype(_MM)
        ckvn_ref[...] = ckvn
        qraw = jnp.dot(cqn, wuq_ref[...], preferred_element_type=F32)
        kraw = jnp.dot(ckvn, wk_ref[...], preferred_element_type=F32)
        v_ref[...] = jnp.dot(ckvn, wv_ref[...], preferred_element_type=F32).astype(_MM)
        krv = kr_ref[...]
        c, s1, s2 = c_ref[...], s1_ref[...], s2_ref[...]
        for h in range(NH):
            sl = slice(LANE * h, LANE * (h + 1))
            qh = qraw[:, sl]
            qn = qh * _rstd(qh, QK_DIM) * qng_ref[...]
            q_ref[:, sl] = (_rope(qn, c, s1, s2) * ATT_SCALE).astype(_MM)
            kh = kraw[:, sl] + krv
            kn = kh * _rstd(kh, QK_DIM) * kng_ref[...]
            k_ref[:, sl] = _rope(kn, c, s1, s2).astype(_MM)

    hd = NH * LANE
    return pl.pallas_call(
        body, name="mla_pre_fwd", grid=(t // tm,),
        in_specs=[_row(tm, 256), _row(tm, 128), _row(tm, 128), _full((1, 256)), _full((256, hd)),
                  _full((1, 128)), _full((128, hd)), _full((128, hd)), _full((1, LANE)), _full((1, LANE)),
                  _row(tm, LANE), _row(tm, LANE), _row(tm, LANE)],
        out_specs=[_row(tm, hd), _row(tm, hd), _row(tm, hd), _row(tm, 256), _row(tm, 128)],
        out_shape=[jax.ShapeDtypeStruct((t, hd), _MM)] * 3 + [jax.ShapeDtypeStruct((t, 256), _MM),
                                                              jax.ShapeDtypeStruct((t, 128), _MM)],
        compiler_params=_cp("parallel"),
    )(cq, ckv, kr, qag, wuq, kvag, wk, wv, qng, kng, rc, rs1, rs2)


def _mla_pre_bwd(dq, dk, dv, cq, ckv, kr, qag, wuq, kvag, wk, wv, qng, kng, rc, rs1, rs2):
    t = cq.shape[0]
    tm = _tile(t, 192)
    hd = NH * LANE

    def body(dq_ref, dk_ref, dv_ref, cq_ref, ckv_ref, kr_ref, qag_ref, wuq_ref, kvag_ref, wk_ref,
             wv_ref, qng_ref, kng_ref, c_ref, s1_ref, s2_ref,
             dcq_ref, dckv_ref, dkr_ref, dqraw_ref, dkraw_ref, dqag_ref, dkvag_ref, dqng_ref, dkng_ref):
        i = pl.program_id(0)
        cqv = cq_ref[...]
        rq_in = _rstd(cqv)
        cqn = (cqv * rq_in * qag_ref[...]).astype(_MM)
        ckvv = ckv_ref[...]
        rkv_in = _rstd(ckvv)
        ckvn = (ckvv * rkv_in * kvag_ref[...]).astype(_MM)
        qraw = jnp.dot(cqn, wuq_ref[...], preferred_element_type=F32)
        kraw = jnp.dot(ckvn, wk_ref[...], preferred_element_type=F32)
        krv = kr_ref[...]
        c, s1, s2 = c_ref[...], s1_ref[...], s2_ref[...]
        dkr = jnp.zeros((tm, LANE), F32)
        dqng = jnp.zeros((8, LANE), F32)
        dkng = jnp.zeros((8, LANE), F32)
        for h in range(NH):
            sl = slice(LANE * h, LANE * (h + 1))
            qh = qraw[:, sl]
            dqn = _rope_t(dq_ref[:, sl] * ATT_SCALE, c, s1, s2)
            dqh, gq = _rms_bwd(dqn, qh, _rstd(qh, QK_DIM), qng_ref[...], QK_DIM)
            dqraw_ref[:, sl] = dqh.astype(_MM)
            dqng = dqng + _colsum8(gq)
            kh = kraw[:, sl] + krv
            dkn = _rope_t(dk_ref[:, sl], c, s1, s2)
            dkh, gk = _rms_bwd(dkn, kh, _rstd(kh, QK_DIM), kng_ref[...], QK_DIM)
            dkraw_ref[:, sl] = dkh.astype(_MM)
            dkr = dkr + dkh
            dkng = dkng + _colsum8(gk)
        dkr_ref[...] = dkr.astype(_MM)
        dcqn = _mm_nt(dqraw_ref[...], wuq_ref[...])
        dcq, gqa = _rms_bwd(dcqn, cqv, rq_in, qag_ref[...])
        dcq_ref[...] = dcq.astype(_MM)
        dckvn = _mm_nt(dkraw_ref[...], wk_ref[...]) + _mm_nt(dv_ref[...], wv_ref[...])
        dckv, gkva = _rms_bwd(dckvn, ckvv, rkv_in, kvag_ref[...])
        dckv_ref[...] = dckv.astype(_MM)

        @pl.when(i == 0)
        def _():
            dqag_ref[...] = jnp.zeros_like(dqag_ref)
            dkvag_ref[...] = jnp.zeros_like(dkvag_ref)
            dqng_ref[...] = jnp.zeros_like(dqng_ref)
            dkng_ref[...] = jnp.zeros_like(dkng_ref)
        dqag_ref[...] += _colsum8(gqa)
        dkvag_ref[...] += _colsum8(gkva)
        dqng_ref[...] += dqng
        dkng_ref[...] += dkng

    return pl.pallas_call(
        body, name="mla_pre_bwd", grid=(t // tm,),
        in_specs=[_row(tm, hd), _row(tm, hd), _row(tm, hd), _row(tm, 256), _row(tm, 128), _row(tm, 128),
                  _full((1, 256)), _full((256, hd)), _full((1, 128)), _full((128, hd)),
                  _full((128, hd)), _full((1, LANE)), _full((1, LANE)),
                  _row(tm, LANE), _row(tm, LANE), _row(tm, LANE)],
        out_specs=[_row(tm, 256), _row(tm, 128), _row(tm, 128), _row(tm, hd), _row(tm, hd),
                   _full((8, 256)), _full((8, 128)), _full((8, LANE)), _full((8, LANE))],
        out_shape=[jax.ShapeDtypeStruct((t, 256), _MM), jax.ShapeDtypeStruct((t, 128), _MM),
                   jax.ShapeDtypeStruct((t, 128), _MM), jax.ShapeDtypeStruct((t, hd), _MM),
                   jax.ShapeDtypeStruct((t, hd), _MM), jax.ShapeDtypeStruct((8, 256), F32),
                   jax.ShapeDtypeStruct((8, 128), F32), jax.ShapeDtypeStruct((8, LANE), F32),
                   jax.ShapeDtypeStruct((8, LANE), F32)],
        compiler_params=_cp("arbitrary"),
    )(dq, dk, dv, cq, ckv, kr, qag, wuq, kvag, wk, wv, qng, kng, rc, rs1, rs2)


def _attn_mask(r0, c0, tq):
    rows = r0 + lax.broadcasted_iota(jnp.int32, (tq, 1), 0)
    cols = c0 + lax.broadcasted_iota(jnp.int32, (1, tq), 1)
    return (cols <= rows) & (cols >= FRONT)


def _attn_fwd(q, k, v, plan=None):
    t = q.shape[0]
    tq = _tile(t, 384)
    nq = t // tq
    p_args, p_in, p_out, p_shape, p_sem = _plan_specs(plan)

    def body(*refs):
        ((q_ref, k_ref, v_ref), (o_ref, lse_ref), _), rider = _host_refs(refs, 3, 2, 0, plan)
        done = _ride(plan, rider, pl.program_id(0), NH - 1)

        def qloop(qi, carry):
            r0 = pl.multiple_of(qi * tq, tq)
            qb = q_ref[pl.ds(r0, tq), :]

            def kstep(kj, st, masked):
                m, l, acc = st
                c0 = pl.multiple_of(kj * tq, tq)
                s = _mm_nt(qb, k_ref[pl.ds(c0, tq), :])
                if masked:
                    s = jnp.where(_attn_mask(r0, c0, tq), s, NEG)
                m2 = jnp.maximum(m, jnp.max(s, axis=-1, keepdims=True))
                p = jnp.exp(s - m2)
                a = jnp.exp(m - m2)
                l = a * l + jnp.sum(p, axis=-1, keepdims=True)
                acc = a * acc + _mm(p, v_ref[pl.ds(c0, tq), :])
                return m2, l, acc

            st = kstep(0, (jnp.full((tq, 1), NEG, F32), jnp.zeros((tq, 1), F32), jnp.zeros((tq, LANE), F32)), True)
            st = lax.fori_loop(1, qi, lambda kj, s_: kstep(kj, s_, False), st)
            m, l, acc = lax.cond(qi > 0, lambda s_: kstep(qi, s_, True), lambda s_: s_, st)
            o_ref[pl.ds(r0, tq), :] = acc / l
            lse_ref[pl.ds(r0, tq), :] = m + jnp.log(l)
            return carry
        lax.fori_loop(0, nq, qloop, 0)
        done()

    hs = pl.BlockSpec((t, LANE), lambda h: (0, h))
    res = pl.pallas_call(
        body, name="attn_fwd", grid=(NH,),
        in_specs=[hs, hs, hs] + p_in,
        out_specs=[hs, pl.BlockSpec((None, t, 1), lambda h: (h, 0, 0))] + p_out,
        out_shape=[jax.ShapeDtypeStruct((t, NH * LANE), F32), jax.ShapeDtypeStruct((NH, t, 1), F32)] + p_shape,
        scratch_shapes=p_sem,
        compiler_params=_cp("parallel" if plan is None else "arbitrary"),
    )(q, k, v, *p_args)
    return res[:2], res[2:]


def _attn_bwd(q, k, v, o, lse, do, plan=None):
    t = q.shape[0]
    tq = _tile(t, 384)
    nq = t // tq
    p_args, p_in, p_out, p_shape, p_sem = _plan_specs(plan)

    def body(*refs):
        (ins, (dq_ref, dk_ref, dv_ref), (delta,)), rider = _host_refs(refs, 6, 3, 1, plan)
        q_ref, k_ref, v_ref, o_ref, lse_ref, do_ref = ins
        done = _ride(plan, rider, pl.program_id(0), NH - 1)

        def prep(i, c):
            r0 = pl.multiple_of(i * tq, tq)
            delta[pl.ds(r0, tq), :] = jnp.sum(do_ref[pl.ds(r0, tq), :] * o_ref[pl.ds(r0, tq), :], axis=-1,
                                              keepdims=True)
            dq_ref[pl.ds(r0, tq), :] = jnp.zeros((tq, LANE), F32)
            return c
        lax.fori_loop(0, nq, prep, 0)

        def kloop(kj, carry):
            c0 = pl.multiple_of(kj * tq, tq)
            kb = k_ref[pl.ds(c0, tq), :]
            vb = v_ref[pl.ds(c0, tq), :]

            def qstep(qi, st, masked):
                dkb, dvb = st
                r0 = pl.multiple_of(qi * tq, tq)
                qb = q_ref[pl.ds(r0, tq), :]
                dob = do_ref[pl.ds(r0, tq), :].astype(_MM)
                s = _mm_nt(qb, kb)
                if masked:
                    s = jnp.where(_attn_mask(r0, c0, tq), s, NEG)
                p = jnp.exp(s - lse_ref[pl.ds(r0, tq), :])
                dvb = dvb + _mm_tn(p, dob)
                dp = _mm_nt(dob, vb)
                ds = (p * (dp - delta[pl.ds(r0, tq), :])).astype(_MM)
                dkb = dkb + _mm_tn(ds, qb)
                dq_ref[pl.ds(r0, tq), :] += _mm(ds, kb)
                return dkb, dvb

            st = qstep(kj, (jnp.zeros((tq, LANE), F32), jnp.zeros((tq, LANE), F32)), True)
            dkb, dvb = lax.cond(
                kj == 0,
                lambda s_: lax.fori_loop(kj + 1, nq, lambda qi, t_: qstep(qi, t_, True), s_),
                lambda s_: lax.fori_loop(kj + 1, nq, lambda qi, t_: qstep(qi, t_, False), s_), st)
            dk_ref[pl.ds(c0, tq), :] = dkb
            dv_ref[pl.ds(c0, tq), :] = dvb
            return carry
        lax.fori_loop(0, nq, kloop, 0)
        done()

    hs = pl.BlockSpec((t, LANE), lambda h: (0, h))
    res = pl.pallas_call(
        body, name="attn_bwd", grid=(NH,),
        in_specs=[hs, hs, hs, hs, pl.BlockSpec((None, t, 1), lambda h: (h, 0, 0)), hs] + p_in,
        out_specs=[hs, hs, hs] + p_out,
        out_shape=[jax.ShapeDtypeStruct((t, NH * LANE), F32)] * 3 + p_shape,
        scratch_shapes=[pltpu.VMEM((t, 1), F32)] + p_sem,
        compiler_params=_cp("parallel" if plan is None else "arbitrary"),
    )(q, k, v, o, lse, do, *p_args)
    return res[:3], res[3:]


def _cumsum_rows(x):
    n = x.shape[0]
    rows = lax.broadcasted_iota(jnp.int32, (n, 1), 0)
    d = 1
    while d < n:
        x = x + jnp.where(rows >= d, pltpu.roll(x, d, 0), 0.0)
        d *= 2
    return x


def _revcumsum_rows(x):
    n = x.shape[0]
    rows = lax.broadcasted_iota(jnp.int32, (n, 1), 0)
    d = 1
    while d < n:
        x = x + jnp.where(rows < n - d, pltpu.roll(x, n - d, 0), 0.0)
        d *= 2
    return x


def _hgrn_gates(f, lb):
    sneg = _sigmoid(-f)
    kk = (1.0 - lb) * sneg
    lf = jnp.log1p(-jnp.minimum(kk, GATE_CLAMP))
    return kk, lf, sneg


def _silu(x):
    return x * _sigmoid(x)


def _dsilu(x):
    s = _sigmoid(x)
    return s * (1.0 + x * (1.0 - s))


def _hgrn_intra(q, kk, b):
    parts = []
    for blk in range(CHUNK // SUB):
        lo = blk * SUB
        ref = jnp.zeros((1, LANE), F32) if blk == 0 else b[lo - 1:lo, :]
        eq = jnp.exp(b[lo:lo + SUB, :] - ref)
        ek = jnp.exp(jnp.minimum(ref - b, EXP_CLIP))
        parts.append((q[lo:lo + SUB, :] * eq, kk * ek, eq, ek))
    return parts


def _chunk_causal():
    return lax.broadcasted_iota(jnp.int32, (CHUNK, CHUNK), 1) <= lax.broadcasted_iota(jnp.int32, (CHUNK, CHUNK), 0)


def _hgrn_fwd(h4, lb):
    t = h4.shape[0]
    nc = t // CHUNK

    def body(q_ref, f_ref, i_ref, lb_ref, o_ref, s_ref, st):
        st[...] = jnp.zeros_like(st)
        causal = _chunk_causal()

        def chunk(c, carry):
            r0 = pl.multiple_of(c * CHUNK, CHUNK)
            q = q_ref[pl.ds(r0, CHUNK), :]
            kk, lf, _ = _hgrn_gates(f_ref[pl.ds(r0, CHUNK), :], lb_ref[...])
            v = _silu(i_ref[pl.ds(r0, CHUNK), :])
            b = _cumsum_rows(lf)
            s_prev = st[...]
            s_ref[c] = s_prev
            o = _hmm_nt(q * jnp.exp(b), s_prev)
            a = jnp.concatenate([_hmm_nt(qs, ks) for qs, ks, _, _ in _hgrn_intra(q, kk, b)], axis=0)
            a = jnp.where(causal, a, 0.0)
            o_ref[pl.ds(r0, CHUNK), :] = o + _hmm(a, v)
            bl = b[CHUNK - 1:CHUNK, :]
            st[...] = s_prev * jnp.exp(bl) + _hmm_tn(v, kk * jnp.exp(bl - b))
            return carry
        lax.fori_loop(0, nc, chunk, 0, unroll=2)

    def col(j):
        return pl.BlockSpec((t, LANE), lambda h: (0, HH * j + h))
    return pl.pallas_call(
        body, name="hgrn_fwd", grid=(HH,),
        in_specs=[col(0), col(1), col(2), pl.BlockSpec((1, LANE), lambda h: (0, h))],
        out_specs=[pl.BlockSpec((t, LANE), lambda h: (0, h)),
                   pl.BlockSpec((None, nc, LANE, LANE), lambda h: (h, 0, 0, 0))],
        out_shape=[jax.ShapeDtypeStruct((t, HH * LANE), F32), jax.ShapeDtypeStruct((HH, nc, LANE, LANE), F32)],
        scratch_shapes=[pltpu.VMEM((LANE, LANE), F32)],
        compiler_params=_cp("parallel"),
    )(h4, h4, h4, lb)


def _hgrn_bwd(h4, lb, do, states):
    t = h4.shape[0]
    nc = t // CHUNK

    def body(q_ref, f_ref, i_ref, lb_ref, do_ref, s_ref, dq_ref, df_ref, di_ref, dlb_ref, dst, carry):
        dst[...] = jnp.zeros_like(dst)
        carry[...] = jnp.zeros_like(carry)
        dlb_ref[...] = jnp.zeros_like(dlb_ref)
        causal = _chunk_causal()

        def chunk(cc, cr):
            c = nc - 1 - cc
            r0 = pl.multiple_of(c * CHUNK, CHUNK)
            q = q_ref[pl.ds(r0, CHUNK), :]
            lbv = lb_ref[...]
            kk, lf, sneg = _hgrn_gates(f_ref[pl.ds(r0, CHUNK), :], lbv)
            iv = i_ref[pl.ds(r0, CHUNK), :]
            v = _silu(iv)
            b = _cumsum_rows(lf)
            s_prev = s_ref[c]
            ds_new = dst[...]
            dob = do_ref[pl.ds(r0, CHUNK), :]
            e = jnp.exp(b)
            qe = q * e
            bl = b[CHUNK - 1:CHUNK, :]
            etail = jnp.exp(bl - b)
            kd = kk * etail
            dq_inter = _hmm(dob, s_prev) * e
            dv = _hmm_nt(kd, ds_new)
            dkk = _hmm(v, ds_new) * etail
            parts = _hgrn_intra(q, kk, b)
            a = jnp.where(causal, jnp.concatenate([_hmm_nt(qs, ks) for qs, ks, _, _ in parts], axis=0), 0.0)
            da = jnp.where(causal, _hmm_nt(dob, v), 0.0)
            dv = dv + _hmm_tn(a, dob)
            dq_rows = []
            for blk, (qs, ks, eq, ek) in enumerate(parts):
                da_blk = da[blk * SUB:(blk + 1) * SUB, :]
                dq_rows.append(_hmm(da_blk, ks) * eq)
                dkk = dkk + _hmm_tn(da_blk, qs) * ek
            dq = dq_inter + jnp.concatenate(dq_rows, axis=0)
            dst[...] = ds_new * jnp.exp(bl) + _hmm_tn(dob, qe)
            g = q * dq - kk * dkk
            dlf = _revcumsum_rows(g) + carry[0:1, :]
            carry[0:1, :] += jnp.sum(g, axis=0, keepdims=True)
            dkk_tot = dkk + dlf * jnp.where(kk < GATE_CLAMP, -1.0 / (1.0 - kk), 0.0)
            dq_ref[pl.ds(r0, CHUNK), :] = dq
            df_ref[pl.ds(r0, CHUNK), :] = dkk_tot * (1.0 - lbv) * (-sneg * (1.0 - sneg))
            di_ref[pl.ds(r0, CHUNK), :] = dv * _dsilu(iv)
            dlb_ref[...] += _colsum8(dkk_tot * (-sneg))
            return cr
        lax.fori_loop(0, nc, chunk, 0, unroll=2)

    def col(j):
        return pl.BlockSpec((t, LANE), lambda h: (0, HH * j + h))
    hs = pl.BlockSpec((t, LANE), lambda h: (0, h))
    return pl.pallas_call(
        body, name="hgrn_bwd", grid=(HH,),
        in_specs=[col(0), col(1), col(2), pl.BlockSpec((1, LANE), lambda h: (0, h)), hs,
                  pl.BlockSpec((None, nc, LANE, LANE), lambda h: (h, 0, 0, 0))],
        out_specs=[hs, hs, hs, pl.BlockSpec((8, LANE), lambda h: (0, h))],
        out_shape=[jax.ShapeDtypeStruct((t, HH * LANE), F32)] * 3 + [jax.ShapeDtypeStruct((8, HH * LANE), F32)],
        scratch_shapes=[pltpu.VMEM((LANE, LANE), F32), pltpu.VMEM((8, LANE), F32)],
        compiler_params=_cp("parallel"),
    )(h4, h4, h4, lb, do, states)


def _ln_fwd(z, g, b):
    mu = jnp.mean(z, axis=-1, keepdims=True)
    zc = z - mu
    rstd = lax.rsqrt(jnp.mean(zc * zc, axis=-1, keepdims=True) + EPS)
    zh = zc * rstd
    return zh * g + b, zh, rstd


def _mix_fwd(x, z, o_att, o_h, h4, gates, lng, lnb, wco, wao, ng, who, wout, t_end):
    t = x.shape[0]
    tm = _tile(t, 192)

    def body(x_ref, z_ref, oa_ref, oh_ref, hg_ref, gt_ref, lng_ref, lnb_ref, wco_ref, wao_ref, ng_ref, who_ref,
             wout_ref, x1_ref, mix_ref, ca_ref, oc_ref, ya_ref, yb_ref, yc_ref):
        i = pl.program_id(0)
        ln, _, _ = _ln_fwd(z_ref[...], lng_ref[...], lnb_ref[...])
        ca = _silu(ln).astype(_MM)
        ca_ref[...] = ca
        ya = jnp.dot(ca, wco_ref[...], preferred_element_type=F32)
        yb = _mm(oa_ref[...], wao_ref[...])
        hg = hg_ref[...]
        for h in range(HH):
            sl = slice(LANE * h, LANE * (h + 1))
            oh = oh_ref[:, sl]
            oc_ref[:, sl] = (oh * _rstd(oh) * ng_ref[:, sl] * _silu(hg[:, sl])).astype(_MM)
        yc = jnp.dot(oc_ref[...], who_ref[...], preferred_element_type=F32)
        ya_ref[...] = ya
        yb_ref[...] = yb
        yc_ref[...] = yc
        mix = (_sigmoid(gt_ref[:, 0:D]) * ya + _sigmoid(gt_ref[:, D:2 * D]) * yb
               + _sigmoid(gt_ref[:, 2 * D:3 * D]) * yc).astype(_MM)
        mix_ref[...] = mix
        x1_ref[...] = x_ref[...] + _valid_rows(i, tm, t_end) * jnp.dot(mix, wout_ref[...],
                                                                       preferred_element_type=F32)

    hd = NH * LANE
    return pl.pallas_call(
        body, name="mix_fwd", grid=(t // tm,),
        in_specs=[_row(tm, D), _row(tm, CONV_DIM), _row(tm, hd), _row(tm, 512), _row(tm, 512, 3), _row(tm, 3 * D),
                  _full((1, 512)), _full((1, 512)), _full((512, D)), _full((hd, D)), _full((1, 512)),
                  _full((512, D)), _full((D, D))],
        out_specs=[_row(tm, D), _row(tm, D), _row(tm, 512), _row(tm, 512), _row(tm, D), _row(tm, D), _row(tm, D)],
        out_shape=[jax.ShapeDtypeStruct((t, D), F32), jax.ShapeDtypeStruct((t, D), _MM),
                   jax.ShapeDtypeStruct((t, 512), _MM), jax.ShapeDtypeStruct((t, 512), _MM),
                   jax.ShapeDtypeStruct((t, D), F32), jax.ShapeDtypeStruct((t, D), F32),
                   jax.ShapeDtypeStruct((t, D), F32)],
        compiler_params=_cp("parallel"),
    )(x, z, o_att, o_h, h4, gates, lng, lnb, wco, wao, ng, who, wout)


def _mix_bwd(dx1, ya, yb, yc, gates, z, o_h, h4, lng, lnb, ng, wout, wco, wao, who):
    t = dx1.shape[0]
    tm = _tile(t, 192)
    hd = NH * LANE

    def body(dx1_ref, ya_ref, yb_ref, yc_ref, gt_ref, z_ref, oh_ref, hg_ref, lng_ref, lnb_ref, ng_ref,
             wout_ref, wco_ref, wao_ref, who_ref,
             dgt_ref, dya_ref, dyb_ref, dyc_ref, dz_ref, doa_ref, doh_ref, dhg_ref,
             dlng_ref, dlnb_ref, dcb_ref, dng_ref):
        i = pl.program_id(0)
        dmix = _mm_nt(dx1_ref[...], wout_ref[...])
        dys = []
        for j, y_ref in enumerate((ya_ref, yb_ref, yc_ref)):
            sg = _sigmoid(gt_ref[:, j * D:(j + 1) * D])
            dgt_ref[:, j * D:(j + 1) * D] = (dmix * y_ref[...] * sg * (1.0 - sg)).astype(_MM)
            dys.append((dmix * sg).astype(_MM))
        dya_ref[...], dyb_ref[...], dyc_ref[...] = dys
        dca = _mm_nt(dys[0], wco_ref[...])
        ln, zh, rstd = _ln_fwd(z_ref[...], lng_ref[...], lnb_ref[...])
        dln = dca * _dsilu(ln)
        dzh = dln * lng_ref[...]
        dz = rstd * (dzh - jnp.mean(dzh, axis=-1, keepdims=True)
                     - zh * jnp.mean(dzh * zh, axis=-1, keepdims=True))
        dz_ref[...] = dz
        doa_ref[...] = _mm_nt(dys[1], wao_ref[...])
        doc = _mm_nt(dys[2], who_ref[...])
        hg = hg_ref[...]
        dng_rows = []
        for h in range(HH):
            sl = slice(LANE * h, LANE * (h + 1))
            oh = oh_ref[:, sl]
            r = _rstd(oh)
            don = doc[:, sl] * _silu(hg[:, sl])
            dhg_ref[:, sl] = (doc[:, sl] * oh * r * ng_ref[:, sl] * _dsilu(hg[:, sl])).astype(_MM)
            doh, gn = _rms_bwd(don, oh, r, ng_ref[:, sl])
            doh_ref[:, sl] = doh
            dng_rows.append(_colsum8(gn))

        @pl.when(i == 0)
        def _():
            dlng_ref[...] = jnp.zeros_like(dlng_ref)
            dlnb_ref[...] = jnp.zeros_like(dlnb_ref)
            dcb_ref[...] = jnp.zeros_like(dcb_ref)
            dng_ref[...] = jnp.zeros_like(dng_ref)
        dlng_ref[...] += _colsum8(dln * zh)
        dlnb_ref[...] += _colsum8(dln)
        dcb_ref[...] += _colsum8(dz)
        dng_ref[...] += jnp.concatenate(dng_rows, axis=1)

    return pl.pallas_call(
        body, name="mix_bwd", grid=(t // tm,),
        in_specs=[_row(tm, D), _row(tm, D), _row(tm, D), _row(tm, D), _row(tm, 3 * D), _row(tm, 512), _row(tm, 512),
                  _row(tm, 512, 3), _full((1, 512)), _full((1, 512)), _full((1, 512)),
                  _full((D, D)), _full((512, D)), _full((hd, D)), _full((512, D))],
        out_specs=[_row(tm, 3 * D), _row(tm, D), _row(tm, D), _row(tm, D), _row(tm, 512), _row(tm, hd),
                   _row(tm, 512), _row(tm, 512), _full((8, 512)), _full((8, 512)), _full((8, 512)), _full((8, 512))],
        out_shape=[jax.ShapeDtypeStruct((t, 3 * D), _MM), jax.ShapeDtypeStruct((t, D), _MM),
                   jax.ShapeDtypeStruct((t, D), _MM), jax.ShapeDtypeStruct((t, D), _MM),
                   jax.ShapeDtypeStruct((t, 512), F32), jax.ShapeDtypeStruct((t, hd), F32),
                   jax.ShapeDtypeStruct((t, 512), F32), jax.ShapeDtypeStruct((t, 512), _MM)]
        + [jax.ShapeDtypeStruct((8, 512), F32)] * 4,
        compiler_params=_cp("arbitrary"),
    )(dx1, ya, yb, yc, gates, z, o_h, h4, lng, lnb, ng, wout, wco, wao, who)


D_FF = 4096


def _ffn_fwd(x1, g2, w1, w2):
    t = x1.shape[0]
    tm = _tile(t, 192)

    def body(x1_ref, g_ref, w1_ref, w2_ref, x2_ref, p_ref):
        xv = x1_ref[...]
        h2 = (xv * _rstd(xv) * g_ref[...]).astype(_MM)
        p = jnp.dot(h2, w1_ref[...], preferred_element_type=F32)
        p_ref[...] = p
        r = jnp.maximum(p, 0.0)
        x2_ref[...] = xv + jnp.dot((r * r).astype(_MM), w2_ref[...], preferred_element_type=F32)

    return pl.pallas_call(
        body, name="ffn_fwd", grid=(t // tm,),
        in_specs=[_row(tm, D), _full((1, D)), _full((D, D_FF)), _full((D_FF, D))],
        out_specs=[_row(tm, D), _row(tm, D_FF)],
        out_shape=[jax.ShapeDtypeStruct((t, D), F32), jax.ShapeDtypeStruct((t, D_FF), F32)],
        compiler_params=_cp("parallel"),
    )(x1, g2, w1, w2)


def _ffn_bwd(dx2, x1, p, g2, w1t, w2t, plan=None):
    t = x1.shape[0]
    tm = _tile(t, 192)
    p_args, p_in, p_out, p_shape, p_sem = _plan_specs(plan)

    def body(*refs):
        (ins, outs, _), rider = _host_refs(refs, 6, 5, 0, plan)
        dx2_ref, x1_ref, p_ref, g_ref, w1t_ref, w2t_ref = ins
        dx1_ref, h2_ref, act_ref, dp_ref, dg_ref = outs
        i = pl.program_id(0)
        done = _ride(plan, rider, i, t // tm - 1)
        xv = x1_ref[...]
        rstd = _rstd(xv)
        h2_ref[...] = (xv * rstd * g_ref[...]).astype(_MM)
        r = jnp.maximum(p_ref[...], 0.0)
        act_ref[...] = (r * r).astype(_MM)
        dx2 = dx2_ref[...]
        da = _mm(dx2, w2t_ref[...])
        dp = (2.0 * r * da).astype(_MM)
        dp_ref[...] = dp
        dh2 = jnp.dot(dp, w1t_ref[...], preferred_element_type=F32)
        dxn, dgrow = _rms_bwd(dh2, xv, rstd, g_ref[...])
        dx1_ref[...] = dx2 + dxn

        @pl.when(i == 0)
        def _():
            dg_ref[...] = jnp.zeros_like(dg_ref)
        dg_ref[...] += _colsum8(dgrow)
        done()

    res = pl.pallas_call(
        body, name="ffn_bwd", grid=(t // tm,),
        in_specs=[_row(tm, D), _row(tm, D), _row(tm, D_FF), _full((1, D)), _full((D_FF, D)),
                  _full((D, D_FF))] + p_in,
        out_specs=[_row(tm, D), _row(tm, D), _row(tm, D_FF), _row(tm, D_FF), _full((8, D))] + p_out,
        out_shape=[jax.ShapeDtypeStruct((t, D), F32), jax.ShapeDtypeStruct((t, D), _MM),
                   jax.ShapeDtypeStruct((t, D_FF), _MM), jax.ShapeDtypeStruct((t, D_FF), _MM),
                   jax.ShapeDtypeStruct((8, D), F32)] + p_shape,
        scratch_shapes=p_sem,
        compiler_params=_cp("arbitrary"),
    )(dx2, x1, p, g2, w1t, w2t, *p_args)
    return res[:5], res[5:]


def _wgrad(a, b, name, chips=1):
    t, ka = a.shape
    nb = b.shape[1]
    tm = _tile(t, 384)
    cs = nb // chips
    tn = max(d for d in range(LANE, cs + 1, LANE) if cs % d == 0 and (ka * d * 4 <= 16 * 1024 * 1024 or d == LANE))
    per = cs // tn

    def body(a_ref, b_ref, o_ref):
        @pl.when(pl.program_id(1) == 0)
        def _():
            o_ref[...] = jnp.zeros_like(o_ref)
        o_ref[...] += _mm_tn(a_ref[...], b_ref[...])

    if chips == 1:
        out_spec = pl.BlockSpec((ka, tn), lambda n, i: (0, n))
        out_shape = jax.ShapeDtypeStruct((ka, nb), F32)
    else:
        out_spec = pl.BlockSpec((None, ka, tn), lambda n, i: (n // per, 0, n % per))
        out_shape = jax.ShapeDtypeStruct((chips, ka, cs), F32)
    return pl.pallas_call(
        body, name="wgrad_" + name, grid=(nb // tn, t // tm),
        in_specs=[pl.BlockSpec((tm, ka), lambda n, i: (i, 0)), pl.BlockSpec((tm, tn), lambda n, i: (i, n))],
        out_specs=out_spec, out_shape=out_shape,
        compiler_params=_cp("parallel", "arbitrary"),
    )(a, b)


def _loss_head(y, target, t_end):
    t = y.shape[0]
    tm = _tile(t, 384)

    def body(y_ref, tg_ref, dy_ref, l_ref):
        i = pl.program_id(0)
        r = i * tm + lax.broadcasted_iota(jnp.int32, (tm, 1), 0)
        real = ((r >= ROW0) & (r < t_end)).astype(F32)
        diff = (y_ref[...] - tg_ref[...]) * real
        dy_ref[...] = diff * (1.0 / D)

        @pl.when(i == 0)
        def _():
            l_ref[...] = jnp.zeros_like(l_ref)
        sq = _colsum8(diff * diff)
        part = sq[:, 0:LANE]
        for j in range(1, D // LANE):
            part = part + sq[:, j * LANE:(j + 1) * LANE]
        l_ref[...] += part * (0.5 / D)

    return pl.pallas_call(
        body, name="loss_head", grid=(t // tm,),
        in_specs=[_row(tm, D), _row(tm, D)],
        out_specs=[_row(tm, D), _full((8, LANE))],
        out_shape=[jax.ShapeDtypeStruct((t, D), F32), jax.ShapeDtypeStruct((8, LANE), F32)],
        compiler_params=_cp("arbitrary"),
    )(y, target)


def _lower_bounds_fwd(logits):
    depth, n = logits.shape

    def body(l_ref, lb_ref):
        lg = l_ref[...]
        m = jnp.max(lg, axis=0, keepdims=True)
        e = jnp.exp(lg - m)
        p = e / jnp.sum(e, axis=0, keepdims=True)
        acc = jnp.zeros((1, n), F32)
        for l in range(depth):
            if l > 0:
                acc = acc + p[l:l + 1, :]
            lb_ref[l:l + 1, :] = acc

    return pl.pallas_call(body, name="lower_bounds_fwd", out_shape=jax.ShapeDtypeStruct((depth, n), F32))(logits)


def _lower_bounds_bwd(logits, dlb):
    depth, n = logits.shape

    def body(l_ref, dlb_ref, dl_ref):
        lg = l_ref[...]
        m = jnp.max(lg, axis=0, keepdims=True)
        e = jnp.exp(lg - m)
        p = e / jnp.sum(e, axis=0, keepdims=True)
        dps = [jnp.zeros((1, n), F32)]
        for j in range(1, depth):
            acc = jnp.zeros((1, n), F32)
            for l in range(j, depth):
                acc = acc + dlb_ref[l:l + 1, :]
            dps.append(acc)
        dot = jnp.zeros((1, n), F32)
        for j in range(depth):
            dot = dot + p[j:j + 1, :] * dps[j]
        for j in range(depth):
            dl_ref[j:j + 1, :] = p[j:j + 1, :] * (dps[j] - dot)

    return pl.pallas_call(body, name="lower_bounds_bwd", out_shape=jax.ShapeDtypeStruct((depth, n), F32))(logits, dlb)


def _ew_tile(rows, cols, n_arrays):
    cap = max(16, (24 * 1024 * 1024) // (8 * n_arrays * cols))
    best = None
    for t in range(16, rows + 1, 16):
        if rows % t == 0 and t <= cap:
            best = t
    return rows if best is None else best


def _adamw_math(w, g, m, v):
    mn = ADAM_B1 * m + (1.0 - ADAM_B1) * g
    vn = ADAM_B2 * v + (1.0 - ADAM_B2) * (g * g)
    m_hat = mn / (1.0 - ADAM_B1 ** ADAM_STEP)
    v_hat = vn / (1.0 - ADAM_B2 ** ADAM_STEP)
    return -ADAM_LR * (m_hat / (jnp.sqrt(v_hat) + ADAM_EPS) + ADAM_WD * w), mn, vn


def _adamw_layers(w, m, v, g0, g1, g_sibling, name):
    _, r, c_ = w.shape
    tr = _ew_tile(r, c_, 10)

    def body(w_ref, m_ref, v_ref, g0_ref, g1_ref, gs_ref, g_ref, d_ref, mo_ref, vo_ref):
        layer = pl.program_id(0)
        own = jnp.where(layer == 0, g0_ref[...], g1_ref[...])
        g = jnp.where(layer == lax.axis_index("c"), own, gs_ref[...])
        g_ref[...] = g
        d_ref[...], mo_ref[...], vo_ref[...] = _adamw_math(w_ref[...], g, m_ref[...], v_ref[...])

    lay = pl.BlockSpec((None, tr, c_), lambda l, i: (l, i, 0))
    flat = pl.BlockSpec((tr, c_), lambda l, i: (i, 0))
    return pl.pallas_call(
        body, name="adamw_" + name, grid=(2, r // tr),
        in_specs=[lay, lay, lay, flat, flat, flat], out_specs=[lay] * 4,
        out_shape=[jax.ShapeDtypeStruct(w.shape, F32)] * 4,
        compiler_params=_cp("parallel", "parallel"),
    )(w, m, v, g0, g1, g_sibling)


def _adamw(w, g, m, v, name):
    rows, cols = w.shape
    tr = _ew_tile(rows, cols, 7)

    def body(w_ref, g_ref, m_ref, v_ref, d_ref, mo_ref, vo_ref):
        d_ref[...], mo_ref[...], vo_ref[...] = _adamw_math(w_ref[...], g_ref[...], m_ref[...], v_ref[...])

    spec = pl.BlockSpec((tr, cols), lambda i: (i, 0))
    return pl.pallas_call(
        body, name="adamw_" + name, grid=(rows // tr,),
        in_specs=[spec] * 4, out_specs=[spec] * 3,
        out_shape=[jax.ShapeDtypeStruct((rows, cols), F32)] * 3,
        compiler_params=_cp("parallel"),
    )(w, g, m, v)


DEPTH = 2
BIG_SHAPES = {"w_in": ((1024, 6560), 1), "w_conv_out": ((512, 1024), 1), "w_uq": ((256, 768), 1),
              "w_ukv": ((128, 1024), 1), "w_attn_out": ((512, 1024), 1), "w_hgrn_out": ((512, 1024), 1),
              "w_out": ((1024, 1024), 0), "w_ff1": ((1024, 4096), 1), "w_ff2": ((4096, 1024), 0)}
BIG = tuple(BIG_SHAPES)
SMALL_SIZES = {"norm1_g": 1024, "conv_b": 512, "conv_ln_g": 512, "conv_ln_b": 512, "q_a_norm_g": 256,
               "kv_a_norm_g": 128, "q_norm_g": 96, "k_norm_g": 96, "hgrn_lb_logits": 512, "hgrn_norm_g": 512,
               "norm2_g": 1024}
SMALL = tuple(SMALL_SIZES)
W_IN_COLS = 6560
W_IN_SHARD = W_IN_COLS // 4
W_IN_SEGS = ((0, 1024, SEG_AG[0]), (1024, 1280, SEG_CQ[0]), (1280, 1408, SEG_CKV[0]), (1408, 1440, SEG_KR[0] + 64),
             (1440, 3488, SEG_H4[0]), (3488, 6560, SEG_GATES[0]))


def _pad_heads(w, nh, used, axis):
    shp = w.shape
    w = w.reshape(shp[:axis] + (nh, used) + shp[axis + 1:])
    pad = [(0, 0)] * w.ndim
    pad[axis + 1] = (0, LANE - used)
    w = jnp.pad(w, pad)
    return w.reshape(shp[:axis] + (nh * LANE,) + shp[axis + 1:])


def _unpad_heads(w, nh, used, axis):
    shp = w.shape
    w = w.reshape(shp[:axis] + (nh, LANE) + shp[axis + 1:])
    w = lax.slice_in_dim(w, 0, used, axis=axis + 1)
    return w.reshape(shp[:axis] + (nh * used,) + shp[axis + 1:])


def _w_in_from_chips(p4):
    def orig(a, b):
        out = []
        while a < b:
            s = a // W_IN_SHARD
            e = min(b, (s + 1) * W_IN_SHARD)
            out.append(p4[s][:, a - W_IN_SHARD * s:e - W_IN_SHARD * s])
            a = e
        return out
    zc = lambda n: jnp.zeros((D, n), p4[0].dtype)
    parts = (orig(3488, 6560) + orig(0, 1024) + orig(1440, 3488) + orig(1024, 1280) + orig(1280, 1408)
             + [zc(64)] + orig(1408, 1440) + [zc(32)])
    return jnp.concatenate(parts, axis=1)


def _w_in_grad_to_chips(dw):
    chips = []
    for s in range(4):
        a, b = W_IN_SHARD * s, W_IN_SHARD * (s + 1)
        parts = []
        for o0, o1, p0 in W_IN_SEGS:
            lo, hi = max(a, o0), min(b, o1)
            if lo < hi:
                parts.append(dw[:, p0 + lo - o0:p0 + hi - o0])
        chips.append(jnp.concatenate(parts, axis=1))
    return jnp.stack(chips)


def _cat_chips(p4, axis):
    return jnp.concatenate([p4[s] for s in range(4)], axis=axis)


def _prep_layer(pieces, small, l):
    mm = lambda a: a.astype(_MM)
    pc = lambda k: [mm(pieces[k][s]) for s in range(4)]
    w_in_p = _w_in_from_chips(pc("w_in"))
    wuq = jnp.concatenate([_pad_heads(pc("w_uq")[s], 2, QK_DIM, 1) for s in range(4)], axis=1)
    wukv = _cat_chips(pc("w_ukv"), 1).reshape(128, NH, 128)
    wk = _pad_heads(wukv[:, :, :64].reshape(128, NH * 64), NH, 64, 1)
    wv = _pad_heads(wukv[:, :, 64:].reshape(128, NH * 64), NH, 64, 1)
    wao = _pad_heads(_cat_chips(pc("w_attn_out"), 1), NH, 64, 0)
    row = lambda a: a.astype(F32).reshape(1, -1)
    p = dict(
        w_in=w_in_p, w_in_t=w_in_p.T, wuq=wuq, wk=wk, wv=wv, wao=wao, wco=_cat_chips(pc("w_conv_out"), 1), who=_cat_chips(pc("w_hgrn_out"), 1),
        wout=_cat_chips(pc("w_out"), 0), w1=_cat_chips(pc("w_ff1"), 1), w2=_cat_chips(pc("w_ff2"), 0),
        g1=row(small["norm1_g"][l]), g2=row(small["norm2_g"][l]),
        cw=jnp.pad(small["conv_w"][l].astype(F32), ((0, 1), (0, 0))), cb=row(small["conv_b"][l]),
        lng=row(small["conv_ln_g"][l]), lnb=row(small["conv_ln_b"][l]),
        qag=row(small["q_a_norm_g"][l]), kvag=row(small["kv_a_norm_g"][l]),
        qng=jnp.pad(row(small["q_norm_g"][l]), ((0, 0), (0, LANE - QK_DIM))),
        kng=jnp.pad(row(small["k_norm_g"][l]), ((0, 0), (0, LANE - QK_DIM))),
        ng=row(small["hgrn_norm_g"][l]),
    )
    return p


def _rope_tables(t):
    pos = (jnp.arange(t, dtype=jnp.int32) - FRONT).astype(F32)
    inv_freq = 10000.0 ** (-jnp.arange(16, dtype=F32) / 16)
    ang = pos[:, None] * inv_freq[None, :]
    cos, sin = jnp.cos(ang), jnp.sin(ang)
    one = jnp.ones((t, 64), F32)
    z16, z32, z64 = jnp.zeros((t, 16), F32), jnp.zeros((t, 32), F32), jnp.zeros((t, 64), F32)
    c = jnp.concatenate([one, cos, cos, z32], axis=1)
    s1 = jnp.concatenate([z64, -sin, z16, z32], axis=1)
    s2 = jnp.concatenate([z64, z16, sin, z32], axis=1)
    return c, s1, s2


def _layer_fwd(x, p, lb, rope, t_end, plan=None):
    gates, ag, h4, cq, ckv, kr, hb = _in_proj_fwd(x, p["g1"], p["w_in"])
    z = _conv_fwd(ag, p["cw"], p["cb"])
    q, k, v, cqn, ckvn = _mla_pre_fwd(cq, ckv, kr, p["qag"], p["wuq"], p["kvag"], p["wk"], p["wv"], p["qng"],
                                      p["kng"], *rope)
    (o_att, lse), rode = _attn_fwd(q, k, v, plan)
    o_h, states = _hgrn_fwd(h4, lb)
    x1, mix, ca, oc, ya, yb, yc = _mix_fwd(x, z, o_att, o_h, h4, gates, p["lng"], p["lnb"], p["wco"], p["wao"],
                                           p["ng"], p["who"], p["wout"], t_end)
    x2, pre = _ffn_fwd(x1, p["g2"], p["w1"], p["w2"])
    saved = dict(x=x, gates=gates, ag=ag, h4=h4, cq=cq, ckv=ckv, kr=kr, hb=hb, z=z, q=q, k=k, v=v, cqn=cqn,
                 ckvn=ckvn, o_att=o_att, lse=lse, o_h=o_h, states=states, x1=x1, mix=mix, ca=ca, oc=oc,
                 ya=ya, yb=yb, yc=yc, pre=pre)
    return x2, saved, rode


def _layer_bwd(dx2, s, p, lb, rope, t_end, plan_ffn=None, plan_attn_fn=None):
    (dx1, h2, act, dp, dg2), rode_ffn = _ffn_bwd(dx2, s["x1"], s["pre"], p["g2"], p["w1"].T, p["w2"].T, plan_ffn)
    plan_attn = None if plan_attn_fn is None else plan_attn_fn(rode_ffn)
    g = {"w_ff1": _wgrad(h2, dp, "ff1", 4), "w_ff2": _wgrad(act, dx2, "ff2").reshape(4, D_FF // 4, D),
         "norm2_g": dg2.sum(0)}
    (dgt, dya, dyb, dyc, dz, doa, doh, dhg, dlng, dlnb, dcb, dng) = _mix_bwd(
        dx1, s["ya"], s["yb"], s["yc"], s["gates"], s["z"], s["o_h"], s["h4"], p["lng"], p["lnb"], p["ng"],
        p["wout"], p["wco"], p["wao"], p["who"])
    g["w_out"] = _wgrad(s["mix"], dx1, "out").reshape(4, D // 4, D)
    g["w_conv_out"] = _wgrad(s["ca"], dya, "conv_out", 4)
    g["w_attn_out"] = _unpad_heads(_wgrad(s["o_att"], dyb, "attn_out", 4), NH, 64, 1)
    g["w_hgrn_out"] = _wgrad(s["oc"], dyc, "hgrn_out", 4)
    g["conv_ln_g"], g["conv_ln_b"], g["conv_b"], g["hgrn_norm_g"] = dlng.sum(0), dlnb.sum(0), dcb.sum(0), dng.sum(0)
    da, dg, dcw = _conv_bwd(s["ag"], p["cw"], dz)
    g["conv_w"] = dcw[:CONV_K]
    (dq, dk, dv), rode_attn = _attn_bwd(s["q"], s["k"], s["v"], s["o_att"], s["lse"], doa, plan_attn)
    dcq, dckv, dkr, dqraw, dkraw, dqag, dkvag, dqng, dkng = _mla_pre_bwd(
        dq, dk, dv, s["cq"], s["ckv"], s["kr"], p["qag"], p["wuq"], p["kvag"], p["wk"], p["wv"], p["qng"],
        p["kng"], *rope)
    g["w_uq"] = _unpad_heads(_wgrad(s["cqn"], dqraw, "uq", 4), 2, QK_DIM, 2)
    dwk = _unpad_heads(_wgrad(s["ckvn"], dkraw, "uk"), NH, 64, 1).reshape(128, NH, 64)
    dwv = _unpad_heads(_wgrad(s["ckvn"], dv, "uv"), NH, 64, 1).reshape(128, NH, 64)
    g["w_ukv"] = jnp.concatenate([dwk, dwv], axis=2).reshape(128, 4, 256).transpose(1, 0, 2)
    g["q_a_norm_g"], g["kv_a_norm_g"] = dqag.sum(0), dkvag.sum(0)
    g["q_norm_g"], g["k_norm_g"] = dqng.sum(0)[:QK_DIM], dkng.sum(0)[:QK_DIM]
    dhq, dhf, dhi, dlb = _hgrn_bwd(s["h4"], lb, doh, s["states"])
    mm = lambda a: a.astype(_MM)
    du = jnp.concatenate([dgt, mm(da), mm(dg), mm(dhq), mm(dhf), mm(dhi), dhg, dcq, dckv, dkr], axis=1)
    dx, dg1 = _in_proj_bwd(du, s["x"], dx1, p["g1"], p["w_in_t"], t_end)
    g["norm1_g"] = dg1.sum(0)
    g["w_in"] = _w_in_grad_to_chips(_wgrad(s["hb"], du, "in"))
    return dx, g, dlb.sum(0), rode_attn


def _device_step(x, target, small, pieces0, pieces1=None, gather1=None, reduce1=None):
    s_real = x.shape[0]
    t_end = ROW0 + s_real
    t = -(-t_end // LANE) * LANE
    zrow = lambda n: jnp.zeros((n, D), F32)
    xp = jnp.concatenate([zrow(FRONT), small["meta"].astype(F32), x, zrow(t - t_end)], axis=0)
    tp = jnp.concatenate([zrow(ROW0), target, zrow(t - t_end)], axis=0)
    rope = _rope_tables(t)
    logits = small["hgrn_lb_logits"].astype(F32)
    lbs = _lower_bounds_fwd(logits)
    prm0 = _prep_layer(pieces0, small, 0)
    h, sv0, rode = _layer_fwd(xp, prm0, lbs[0:1], rope, t_end, None if gather1 is None else gather1[0])
    if gather1 is not None:
        pieces1 = gather1[1](rode)
    prm1 = _prep_layer(pieces1, small, 1)
    h, sv1, _ = _layer_fwd(h, prm1, lbs[1:2], rope, t_end)
    dh, lsum = _loss_head(h, tp, t_end)
    loss = jnp.sum(lsum)
    dh, g1, dlb1, _ = _layer_bwd(dh, sv1, prm1, lbs[1:2], rope, t_end)
    plan_ffn, plan_attn_fn = (None, None) if reduce1 is None else (reduce1[0](g1), reduce1[1])
    dh, g0, dlb0, rode = _layer_bwd(dh, sv0, prm0, lbs[0:1], rope, t_end, plan_ffn, plan_attn_fn)
    dlogits = _lower_bounds_bwd(logits, jnp.stack([dlb0, dlb1]))
    grads = [g0, g1]
    for l in range(DEPTH):
        grads[l]["hgrn_lb_logits"] = dlogits[l]
    return loss, dh[ROW0:t_end], grads, dh[FRONT:ROW0], rode


MESH = pl.DeviceIdType.MESH
_ANY = pl.BlockSpec(memory_space=pl.ANY)
SMALL_ROWS = 64
SMALL_LEN = SMALL_ROWS * 1024


def _mesh_pos():
    return lax.axis_index("x"), lax.axis_index("y"), lax.axis_index("c")


def _other_chips(x, y):
    return [(1 - x, y), (x, 1 - y), (1 - x, 1 - y)]


class _Plan:
    def __init__(self, name, ins, out_shapes, sems, start, finish):
        self.name, self.ins, self.out_shapes, self.sems = name, list(ins), list(out_shapes), list(sems)
        self.start, self.finish = start, finish


def _run_plan(plan):
    ni, no = len(plan.ins), len(plan.out_shapes)

    def body(*refs):
        ins, outs, sems = refs[:ni], refs[ni:ni + no], refs[ni + no:]
        plan.start(ins, outs, sems)
        plan.finish(ins, outs, sems)

    return pl.pallas_call(body, name=plan.name, in_specs=[_ANY] * ni, out_specs=[_ANY] * no,
                          out_shape=plan.out_shapes, scratch_shapes=plan.sems)(*plan.ins)


def _plan_specs(plan):
    if plan is None:
        return [], [], [], [], []
    return plan.ins, [_ANY] * len(plan.ins), [_ANY] * len(plan.out_shapes), plan.out_shapes, plan.sems


def _host_refs(refs, n_in, n_out, n_scratch, plan):
    ni = 0 if plan is None else len(plan.ins)
    no = 0 if plan is None else len(plan.out_shapes)
    o0 = n_in + ni
    s0 = o0 + n_out + no
    own = (refs[:n_in], refs[o0:o0 + n_out], refs[s0:s0 + n_scratch])
    rider = (refs[n_in:o0], refs[o0 + n_out:s0], refs[s0 + n_scratch:])
    return own, rider


def _ride(plan, rider, step, last):
    if plan is None:
        return lambda: None

    @pl.when(step == 0)
    def _():
        plan.start(*rider)

    def done():
        @pl.when(step == last)
        def _():
            plan.finish(*rider)
    return done


def _plan_gather(own, layer):
    nw = len(own)

    def copies(ins, outs, sems):
        send_sems, recv_sems = sems

        def over_ici(w, j, chip_of_data, to):
            return pltpu.make_async_remote_copy(
                src_ref=ins[w].at[layer], dst_ref=outs[w].at[chip_of_data], send_sem=send_sems.at[w, j],
                recv_sem=recv_sems.at[w, j], device_id=to, device_id_type=MESH)

        def over_d2d(w, j, chip_of_data, to):
            return pltpu.make_async_remote_copy(
                src_ref=outs[w].at[chip_of_data], dst_ref=outs[w].at[chip_of_data], send_sem=send_sems.at[w, 3 + j],
                recv_sem=recv_sems.at[w, 3 + j], device_id=to, device_id_type=MESH)
        return over_ici, over_d2d

    def start(ins, outs, sems):
        x, y, c = _mesh_pos()
        over_ici, _ = copies(ins, outs, sems)

        @pl.when(c == layer)
        def _():
            for j, (px, py) in enumerate(_other_chips(x, y)):
                for w in range(nw):
                    over_ici(w, j, 2 * x + y, (px, py, layer)).start()

    def finish(ins, outs, sems):
        x, y, c = _mesh_pos()
        over_ici, over_d2d = copies(ins, outs, sems)
        chips = _other_chips(x, y)

        @pl.when(c == layer)
        def _():
            for j, (px, py) in enumerate(chips):
                for w in range(nw):
                    over_ici(w, j, 2 * px + py, (x, y, c)).wait_recv()
                    over_d2d(w, j, 2 * px + py, (x, y, 1 - layer)).start()
            for j, (px, py) in enumerate(chips):
                for w in range(nw):
                    over_ici(w, j, 2 * x + y, (px, py, layer)).wait_send()
                    over_d2d(w, j, 2 * px + py, (x, y, 1 - layer)).wait_send()

        @pl.when(c != layer)
        def _():
            for j, (px, py) in enumerate(chips):
                for w in range(nw):
                    over_d2d(w, j, 2 * px + py, (x, y, c)).wait_recv()

    return _Plan("gather_weights_l%d" % layer, own,
                 [jax.ShapeDtypeStruct((4,) + a.shape[1:], a.dtype) for a in own],
                 [pltpu.SemaphoreType.DMA((nw, 6)), pltpu.SemaphoreType.DMA((nw, 6))], start, finish)


def _plan_to_sibling(arrs, layer, name):
    nw = len(arrs)

    def copy(ins, outs, sems, w):
        x, y, _ = _mesh_pos()
        return pltpu.make_async_remote_copy(src_ref=ins[w], dst_ref=outs[w], send_sem=sems[0].at[w],
                                            recv_sem=sems[1].at[w], device_id=(x, y, layer), device_id_type=MESH)

    def start(ins, outs, sems):
        @pl.when(lax.axis_index("c") != layer)
        def _():
            for w in range(nw):
                copy(ins, outs, sems, w).start()

    def finish(ins, outs, sems):
        c = lax.axis_index("c")

        @pl.when(c != layer)
        def _():
            for w in range(nw):
                copy(ins, outs, sems, w).wait_send()

        @pl.when(c == layer)
        def _():
            for w in range(nw):
                copy(ins, outs, sems, w).wait_recv()

    return _Plan(name, arrs, [jax.ShapeDtypeStruct(a.shape, a.dtype) for a in arrs],
                 [pltpu.SemaphoreType.DMA((nw,)), pltpu.SemaphoreType.DMA((nw,))], start, finish)


def _plan_scatter(parts, layer):
    nw = len(parts)

    def start(ins, outs, sems):
        x, y, c = _mesh_pos()

        @pl.when(c == layer)
        def _():
            for j, (px, py) in enumerate(_other_chips(x, y)):
                for w in range(nw):
                    pltpu.make_async_remote_copy(
                        src_ref=ins[w].at[2 * px + py], dst_ref=outs[w].at[2 * x + y], send_sem=sems[0].at[w, j],
                        recv_sem=sems[1].at[w, j], device_id=(px, py, layer), device_id_type=MESH).start()

    def finish(ins, outs, sems):
        x, y, c = _mesh_pos()

        @pl.when(c == layer)
        def _():
            for j, (px, py) in enumerate(_other_chips(x, y)):
                for w in range(nw):
                    pltpu.make_async_remote_copy(
                        src_ref=ins[w].at[2 * px + py], dst_ref=outs[w].at[2 * px + py], send_sem=sems[0].at[w, j],
                        recv_sem=sems[1].at[w, j], device_id=(x, y, c), device_id_type=MESH).wait()

    return _Plan("scatter_grads_l%d" % layer, parts, [jax.ShapeDtypeStruct(a.shape, a.dtype) for a in parts],
                 [pltpu.SemaphoreType.DMA((nw, 3)), pltpu.SemaphoreType.DMA((nw, 3))], start, finish)


def _sibling_exchange(reds0, reds1):
    nw = len(reds0)

    def body(*refs):
        a0, a1, outs = refs[:nw], refs[nw:2 * nw], refs[2 * nw:3 * nw]
        send_sems, recv_sems = refs[3 * nw:]
        x, y, c = _mesh_pos()

        def copy(w, src):
            return pltpu.make_async_remote_copy(src_ref=src, dst_ref=outs[w], send_sem=send_sems.at[w],
                                                recv_sem=recv_sems.at[w], device_id=(x, y, 1 - c),
                                                device_id_type=MESH)

        @pl.when(c == 0)
        def _():
            for w in range(nw):
                copy(w, a0[w]).start()

        @pl.when(c == 1)
        def _():
            for w in range(nw):
                copy(w, a1[w]).start()

        for w in range(nw):
            copy(w, a0[w]).wait()

    return pl.pallas_call(
        body, name="sibling_exchange", in_specs=[_ANY] * (2 * nw), out_specs=[_ANY] * nw,
        out_shape=[jax.ShapeDtypeStruct(a.shape, a.dtype) for a in reds0],
        scratch_shapes=[pltpu.SemaphoreType.DMA((nw,)), pltpu.SemaphoreType.DMA((nw,))],
    )(*reds0, *reds1)


def _all_reduce_small(v, name):
    rows, cols = v.shape

    def body(v_ref, o_ref, slots, send_sems, recv_sems):
        x, y, c = _mesh_pos()
        me = 4 * x + 2 * y + c
        slots[me] = v_ref[...]
        peers = []
        for rel in range(1, 8):
            fx, fy, fc = (rel >> 2) & 1, (rel >> 1) & 1, rel & 1
            px = 1 - x if fx else x
            py = 1 - y if fy else y
            pc = 1 - c if fc else c
            peers.append((px, py, pc))
        cps = [pltpu.make_async_remote_copy(src_ref=v_ref, dst_ref=slots.at[me], send_sem=send_sems.at[k],
                                            recv_sem=recv_sems.at[k], device_id=peer, device_id_type=MESH)
               for k, peer in enumerate(peers)]
        for cp in cps:
            cp.start()
        for k, (px, py, pc) in enumerate(peers):
            pltpu.make_async_remote_copy(src_ref=v_ref, dst_ref=slots.at[4 * px + 2 * py + pc],
                                         send_sem=send_sems.at[k], recv_sem=recv_sems.at[k], device_id=(x, y, c),
                                         device_id_type=MESH).wait_recv()
        for cp in cps:
            cp.wait_send()
        acc = slots[0]
        for d in range(1, 8):
            acc = acc + slots[d]
        o_ref[...] = acc

    vm = pl.BlockSpec(memory_space=pltpu.VMEM)
    return pl.pallas_call(
        body, name=name, in_specs=[vm], out_specs=vm,
        out_shape=jax.ShapeDtypeStruct((rows, cols), F32),
        scratch_shapes=[pltpu.VMEM((8, rows, cols), F32), pltpu.SemaphoreType.DMA((7,)),
                        pltpu.SemaphoreType.DMA((7,))],
    )(v)


def _add_to_wire(a, b, name):
    n4, r, c_ = a.shape
    rows = n4 * r
    tr = _ew_tile(rows, c_, 3)

    def body(a_ref, b_ref, o_ref):
        o_ref[...] = (a_ref[...] + b_ref[...]).astype(o_ref.dtype)

    spec = pl.BlockSpec((tr, c_), lambda i: (i, 0))
    out = pl.pallas_call(
        body, name="add_to_wire_" + name, grid=(rows // tr,), in_specs=[spec, spec], out_specs=spec,
        out_shape=jax.ShapeDtypeStruct((rows, c_), jnp.bfloat16), compiler_params=_cp("parallel"),
    )(a.reshape(rows, c_), b.reshape(rows, c_))
    return out.reshape(n4, r, c_)


def _sum_chips(recv, own, name):
    _, r, c_ = recv.shape
    tr = _ew_tile(r, c_, 4)

    def body(r_ref, own_ref, o_ref):
        chip = 2 * lax.axis_index("x") + lax.axis_index("y")
        own_v = own_ref[...].astype(F32)
        acc = None
        for s in range(4):
            term = jnp.where(chip == s, own_v, r_ref[s].astype(F32))
            acc = term if acc is None else acc + term
        o_ref[...] = acc

    return pl.pallas_call(
        body, name="sum_chips_" + name, grid=(r // tr,),
        in_specs=[pl.BlockSpec((4, tr, c_), lambda i: (0, i, 0)), pl.BlockSpec((tr, c_), lambda i: (i, 0))],
        out_specs=pl.BlockSpec((tr, c_), lambda i: (i, 0)),
        out_shape=jax.ShapeDtypeStruct((r, c_), F32),
        compiler_params=_cp("parallel"),
    )(recv, own)


def _pack_small(vals, meta_full, conv_w_full):
    flat = jnp.concatenate([vals[k].reshape(-1) for k in SMALL] + [meta_full.reshape(-1), conv_w_full.reshape(-1)])
    return jnp.pad(flat, (0, SMALL_LEN - flat.shape[0])).reshape(SMALL_ROWS, 1024)


def _unpack_small(buf):
    flat = buf.reshape(-1)
    out, off = {}, 0
    for k in SMALL:
        n = DEPTH * SMALL_SIZES[k]
        out[k] = flat[off:off + n].reshape(DEPTH, SMALL_SIZES[k])
        off += n
    meta = flat[off:off + N_META * D].reshape(N_META, D)
    off += N_META * D
    conv_w = flat[off:off + DEPTH * CONV_K * CONV_DIM].reshape(DEPTH, CONV_K, CONV_DIM)
    return out, meta, conv_w


def kernel(x, meta, norm1_g, w_in, conv_w, conv_b, conv_ln_g, conv_ln_b, w_conv_out, q_a_norm_g, w_uq, kv_a_norm_g, w_ukv, q_norm_g, k_norm_g, w_attn_out, hgrn_lb_logits, hgrn_norm_g, w_hgrn_out, w_out, norm2_g, w_ff1, w_ff2, loss_target, m_meta, m_norm1_g, m_w_in, m_conv_w, m_conv_b, m_conv_ln_g, m_conv_ln_b, m_w_conv_out, m_q_a_norm_g, m_w_uq, m_kv_a_norm_g, m_w_ukv, m_q_norm_g, m_k_norm_g, m_w_attn_out, m_hgrn_lb_logits, m_hgrn_norm_g, m_w_hgrn_out, m_w_out, m_norm2_g, m_w_ff1, m_w_ff2, v_meta, v_norm1_g, v_w_in, v_conv_w, v_conv_b, v_conv_ln_g, v_conv_ln_b, v_w_conv_out, v_q_a_norm_g, v_w_uq, v_kv_a_norm_g, v_w_ukv, v_q_norm_g, v_k_norm_g, v_w_attn_out, v_hgrn_lb_logits, v_hgrn_norm_g, v_w_hgrn_out, v_w_out, v_norm2_g, v_w_ff1, v_w_ff2):
    names = ("meta", "norm1_g", "w_in", "conv_w", "conv_b", "conv_ln_g", "conv_ln_b", "w_conv_out", "q_a_norm_g",
             "w_uq", "kv_a_norm_g", "w_ukv", "q_norm_g", "k_norm_g", "w_attn_out", "hgrn_lb_logits", "hgrn_norm_g",
             "w_hgrn_out", "w_out", "norm2_g", "w_ff1", "w_ff2")
    w = dict(zip(names, (meta, norm1_g, w_in, conv_w, conv_b, conv_ln_g, conv_ln_b, w_conv_out, q_a_norm_g, w_uq,
                         kv_a_norm_g, w_ukv, q_norm_g, k_norm_g, w_attn_out, hgrn_lb_logits, hgrn_norm_g, w_hgrn_out,
                         w_out, norm2_g, w_ff1, w_ff2)))
    m = dict(zip(names, (m_meta, m_norm1_g, m_w_in, m_conv_w, m_conv_b, m_conv_ln_g, m_conv_ln_b, m_w_conv_out,
                         m_q_a_norm_g, m_w_uq, m_kv_a_norm_g, m_w_ukv, m_q_norm_g, m_k_norm_g, m_w_attn_out,
                         m_hgrn_lb_logits, m_hgrn_norm_g, m_w_hgrn_out, m_w_out, m_norm2_g, m_w_ff1, m_w_ff2)))
    v = dict(zip(names, (v_meta, v_norm1_g, v_w_in, v_conv_w, v_conv_b, v_conv_ln_g, v_conv_ln_b, v_w_conv_out,
                         v_q_a_norm_g, v_w_uq, v_kv_a_norm_g, v_w_ukv, v_q_norm_g, v_k_norm_g, v_w_attn_out,
                         v_hgrn_lb_logits, v_hgrn_norm_g, v_w_hgrn_out, v_w_out, v_norm2_g, v_w_ff1, v_w_ff2)))
    cx, cy, cc = _mesh_pos()
    chip = 2 * cx + cy
    zero = jnp.zeros((), jnp.int32)

    own = [w[k].astype(_MM) for k in BIG]

    def as_pieces(gathered, layer):
        return {k: [jnp.where(chip == s, o[layer], g[s]) for s in range(4)] for k, o, g in zip(BIG, own, gathered)}

    pieces0 = as_pieces(_run_plan(_plan_gather(own, 0)), 0)
    gather1 = (_plan_gather(own, 1), lambda got: as_pieces(got, 1))
    meta_slab = lax.dynamic_update_slice(jnp.zeros((N_META, D), F32), meta, (zero, chip * (D // 4)))
    convw_slab = lax.dynamic_update_slice(jnp.zeros((DEPTH, CONV_K, CONV_DIM), F32), conv_w,
                                          (zero, zero, chip * (CONV_DIM // 4)))
    zsmall = {k: jnp.zeros((DEPTH, SMALL_SIZES[k]), F32) for k in SMALL}
    south = (cc == 0).astype(F32)
    _, meta_full, convw_full = _unpack_small(
        _all_reduce_small(_pack_small(zsmall, meta_slab, convw_slab) * south, "gather_small"))
    small = {k: w[k] for k in SMALL}
    small["meta"] = meta_full
    small["conv_w"] = convw_full

    held = {}

    def to_wire(layer, mine, from_sibling):
        return [_add_to_wire(a, b, "%s_l%d" % (k, layer)) for k, a, b in zip(BIG, mine, from_sibling)]

    def swap1(g1):
        held["g1"] = [g1[k] for k in BIG]
        return _plan_to_sibling(held["g1"], 1, "swap_grads_l1")

    def scatter1(from_sibling):
        held["wire1"] = to_wire(1, held["g1"], from_sibling)
        return _plan_scatter(held["wire1"], 1)

    loss_share, grad_x, gl, g_meta, got1 = _device_step(x[0], loss_target[0], small, pieces0, None, gather1,
                                                        (swap1, scatter1))
    loss = lax.psum(loss_share, ("x", "y", "c"))

    def chip_sum(layer, got, wire):
        return [_sum_chips(r, lax.dynamic_index_in_dim(s, chip, 0, keepdims=False), "%s_l%d" % (k, layer))
                for k, r, s in zip(BIG, got, wire)]

    reds1 = chip_sum(1, got1, held["wire1"])
    g0 = [gl[0][k] for k in BIG]
    wire0 = to_wire(0, g0, _run_plan(_plan_to_sibling(g0, 0, "swap_grads_l0")))
    reds0 = chip_sum(0, _run_plan(_plan_scatter(wire0, 0)), wire0)
    reds_sibling = _sibling_exchange(reds0, reds1)
    grads, delta, new_m, new_v = {}, {}, {}, {}
    for k, r0, r1, theirs in zip(BIG, reds0, reds1, reds_sibling):
        grads[k], delta[k], new_m[k], new_v[k] = _adamw_layers(w[k], m[k], v[k], r0, r1, theirs, k)

    g_small_local = {k: jnp.stack([gl[l][k] for l in range(DEPTH)]) for k in SMALL}
    g_convw_local = jnp.stack([gl[l]["conv_w"] for l in range(DEPTH)])
    g_small, g_meta_full, g_convw_full = _unpack_small(
        _all_reduce_small(_pack_small(g_small_local, g_meta, g_convw_local), "reduce_small"))
    grads.update(g_small)
    grads["meta"] = lax.dynamic_slice(g_meta_full, (zero, chip * (D // 4)), (N_META, D // 4))
    grads["conv_w"] = lax.dynamic_slice(g_convw_full, (zero, zero, chip * (CONV_DIM // 4)),
                                        (DEPTH, CONV_K, CONV_DIM // 4))

    def small_pack(src):
        return _pack_small(src, jnp.pad(src["meta"], ((0, 0), (0, D - D // 4))),
                           jnp.pad(src["conv_w"], ((0, 0), (0, 0), (0, CONV_DIM - CONV_DIM // 4))))

    def small_unpack(buf):
        out, meta_p, convw_p = _unpack_small(buf)
        out["meta"] = meta_p[:, :D // 4]
        out["conv_w"] = convw_p[:, :, :CONV_DIM // 4]
        return out

    d_s, m_s, v_s = [small_unpack(a) for a in _adamw(small_pack(w), small_pack(grads), small_pack(m),
                                                     small_pack(v), "small")]
    delta.update(d_s)
    new_m.update(m_s)
    new_v.update(v_s)
    return (loss, grad_x[None], *[grads[k] for k in names], *[delta[k] for k in names],
            *[new_m[k] for k in names], *[new_v[k] for k in names])
```

```python
import functools

import jax
import jax.numpy as jnp
from jax import lax
from jax.experimental import pallas as pl
from jax.experimental.pallas import tpu as pltpu

F32 = jnp.float32
_MM = jnp.bfloat16

D = 1024
N_META = 16
FRONT = 48
ROW0 = FRONT + N_META
EPS = 1e-6
GATE_CLAMP = 1.0 - 1e-6
CONV_K = 31
CONV_DIM = 512
NH = 8
QK_DIM = 96
ATT_SCALE = QK_DIM ** -0.5
HH = 4
CHUNK = 64
SUB = 16
EXP_CLIP = 60.0
NEG = -1e30
LANE = 128

SEG_GATES = (0, 3072)
SEG_AG = (3072, 4096)
SEG_H4 = (4096, 6144)
SEG_CQ = (6144, 6400)
SEG_CKV = (6400, 6528)
SEG_KR = (6528, 6656)
N_IN_P = 6656

ADAM_LR = 0.001
ADAM_B1 = 0.9
ADAM_B2 = 0.999
ADAM_EPS = 1e-08
ADAM_WD = 0.01
ADAM_STEP = 10

VMEM_LIMIT = 56 * 1024 * 1024


def _tile(n, pref):
    best = 64
    for t in range(64, pref + 1, 64):
        if n % t == 0:
            best = t
    return best


def _cp(*sem):
    return pltpu.CompilerParams(dimension_semantics=tuple(sem), vmem_limit_bytes=VMEM_LIMIT)


def _row(tm, n, col=0):
    return pl.BlockSpec((tm, n), lambda i: (i, col))


def _full(shape):
    return pl.BlockSpec(shape, lambda i: (0,) * len(shape))


def _mm(a, b):
    return jnp.dot(a.astype(_MM), b.astype(_MM), preferred_element_type=F32)


def _mm_nt(a, b):
    return lax.dot_general(a.astype(_MM), b.astype(_MM), (((1,), (1,)), ((), ())), preferred_element_type=F32)


def _mm_tn(a, b):
    return lax.dot_general(a.astype(_MM), b.astype(_MM), (((0,), (0,)), ((), ())), preferred_element_type=F32)


def _split3(x):
    hi = x.astype(jnp.bfloat16)
    return hi, (x - hi.astype(F32)).astype(jnp.bfloat16)


def _dot3(a, b, dims):
    ah, al = _split3(a)
    bh, bl = _split3(b)
    dg = lambda u, v: lax.dot_general(u, v, (dims, ((), ())), preferred_element_type=F32)
    return dg(ah, bh) + (dg(ah, bl) + dg(al, bh))


def _hmm(a, b):
    return _dot3(a, b, ((1,), (0,)))


def _hmm_nt(a, b):
    return _dot3(a, b, ((1,), (1,)))


def _hmm_tn(a, b):
    return _dot3(a, b, ((0,), (0,)))


def _sigmoid(x):
    return 1.0 / (1.0 + jnp.exp(-x))


def _rstd(x, n=None):
    n = x.shape[-1] if n is None else n
    return lax.rsqrt(jnp.sum(x * x, axis=-1, keepdims=True) * (1.0 / n) + EPS)


def _rms_bwd(dy, x, rstd, g, n=None):
    n = x.shape[-1] if n is None else n
    xh = x * rstd
    dxh = dy * g
    dx = rstd * (dxh - xh * (jnp.sum(dxh * xh, axis=-1, keepdims=True) * (1.0 / n)))
    return dx, dy * xh


def _valid_rows(i, tm, t_valid_end):
    r = i * tm + lax.broadcasted_iota(jnp.int32, (tm, 1), 0)
    return ((r >= FRONT) & (r < t_valid_end)).astype(F32)


def _colsum8(x):
    n, c = x.shape
    return jnp.sum(x.reshape(n // 8, 8, c), axis=0)


def _in_proj_fwd(x, g1, w):
    t = x.shape[0]
    tm = _tile(t, 192)
    segs = (SEG_GATES, SEG_AG, SEG_H4, SEG_CQ, SEG_CKV, SEG_KR)

    def body(x_ref, g_ref, w_ref, gates_ref, ag_ref, h4_ref, cq_ref, ckv_ref, kr_ref, hb_ref):
        xv = x_ref[...]
        hb = (xv * _rstd(xv) * g_ref[...]).astype(_MM)
        hb_ref[...] = hb
        for ref, (a, b) in zip((gates_ref, ag_ref, h4_ref, cq_ref, ckv_ref, kr_ref), segs):
            ref[...] = jnp.dot(hb, w_ref[:, a:b], preferred_element_type=F32)

    outs = [jax.ShapeDtypeStruct((t, b - a), F32) for a, b in segs] + [jax.ShapeDtypeStruct((t, D), _MM)]
    return pl.pallas_call(
        body, name="in_proj_fwd", grid=(t // tm,),
        in_specs=[_row(tm, D), _full((1, D)), _full((D, N_IN_P))],
        out_specs=[_row(tm, b - a) for a, b in segs] + [_row(tm, D)],
        out_shape=outs, compiler_params=_cp("parallel"),
    )(x, g1, w)


def _in_proj_bwd(du, x, dx1, g1, wt, t_end):
    t = x.shape[0]
    tm = _tile(t, 192)

    def body(du_ref, x_ref, dx1_ref, g_ref, wt_ref, dx_ref, dg_ref):
        i = pl.program_id(0)
        dh = jnp.dot(du_ref[...], wt_ref[...], preferred_element_type=F32)
        xv = x_ref[...]
        dxn, dgrow = _rms_bwd(dh, xv, _rstd(xv), g_ref[...])
        dx_ref[...] = _valid_rows(i, tm, t_end) * (dx1_ref[...] + dxn)

        @pl.when(i == 0)
        def _():
            dg_ref[...] = jnp.zeros_like(dg_ref)
        dg_ref[...] += _colsum8(dgrow)

    return pl.pallas_call(
        body, name="in_proj_bwd", grid=(t // tm,),
        in_specs=[_row(tm, N_IN_P), _row(tm, D), _row(tm, D), _full((1, D)), _full((N_IN_P, D))],
        out_specs=[_row(tm, D), _full((8, D))],
        out_shape=[jax.ShapeDtypeStruct((t, D), F32), jax.ShapeDtypeStruct((8, D), F32)],
        compiler_params=_cp("arbitrary"),
    )(du, x, dx1, g1, wt)


CONV_CH = 128


def _conv_fwd(ag, cw, cb):
    t = ag.shape[0]
    n = t // CONV_CH

    def body(a_ref, g_ref, w_ref, b_ref, z_ref, hp):
        hp[0:32, :] = jnp.zeros((32, LANE), F32)

        def fill(i, c):
            r = pl.multiple_of(i * CONV_CH, CONV_CH)
            hp[pl.ds(32 + r, CONV_CH), :] = a_ref[pl.ds(r, CONV_CH), :] * _sigmoid(g_ref[pl.ds(r, CONV_CH), :])
            return c
        lax.fori_loop(0, n, fill, 0)

        def conv(i, c):
            r = pl.multiple_of(i * CONV_CH, CONV_CH)
            acc = jnp.broadcast_to(b_ref[...], (CONV_CH, LANE))
            for k in range(CONV_K):
                acc = acc + w_ref[k:k + 1, :] * hp[pl.ds(r + (k + 2), CONV_CH), :]
            z_ref[pl.ds(r, CONV_CH), :] = acc
            return c
        lax.fori_loop(0, n, conv, 0)

    nb = CONV_DIM // LANE
    return pl.pallas_call(
        body, name="conv_fwd", grid=(nb,),
        in_specs=[pl.BlockSpec((t, LANE), lambda j: (0, j)), pl.BlockSpec((t, LANE), lambda j: (0, nb + j)),
                  pl.BlockSpec((32, LANE), lambda j: (0, j)), pl.BlockSpec((1, LANE), lambda j: (0, j))],
        out_specs=pl.BlockSpec((t, LANE), lambda j: (0, j)),
        out_shape=jax.ShapeDtypeStruct((t, CONV_DIM), F32),
        scratch_shapes=[pltpu.VMEM((t + 32, LANE), F32)],
        compiler_params=_cp("parallel"),
    )(ag, ag, cw, cb)


def _conv_bwd(ag, cw, dz):
    t = ag.shape[0]
    n = t // CONV_CH

    def body(a_ref, g_ref, w_ref, dz_ref, da_ref, dg_ref, dcw_ref, hp, dzp, accw):
        hp[0:32, :] = jnp.zeros((32, LANE), F32)
        dzp[pl.ds(t, 32), :] = jnp.zeros((32, LANE), F32)
        accw[...] = jnp.zeros_like(accw)

        def fill(i, c):
            r = pl.multiple_of(i * CONV_CH, CONV_CH)
            hp[pl.ds(32 + r, CONV_CH), :] = a_ref[pl.ds(r, CONV_CH), :] * _sigmoid(g_ref[pl.ds(r, CONV_CH), :])
            dzp[pl.ds(r, CONV_CH), :] = dz_ref[pl.ds(r, CONV_CH), :]
            return c
        lax.fori_loop(0, n, fill, 0)

        def step(i, c):
            r = pl.multiple_of(i * CONV_CH, CONV_CH)
            dzc = dz_ref[pl.ds(r, CONV_CH), :]
            dh = jnp.zeros((CONV_CH, LANE), F32)
            for k in range(CONV_K):
                dh = dh + w_ref[k:k + 1, :] * dzp[pl.ds(r + (CONV_K - 1 - k), CONV_CH), :]
                accw[8 * k:8 * k + 8, :] += _colsum8(dzc * hp[pl.ds(r + (k + 2), CONV_CH), :])
            a = a_ref[pl.ds(r, CONV_CH), :]
            sg = _sigmoid(g_ref[pl.ds(r, CONV_CH), :])
            da_ref[pl.ds(r, CONV_CH), :] = dh * sg
            dg_ref[pl.ds(r, CONV_CH), :] = dh * a * sg * (1.0 - sg)
            return c
        lax.fori_loop(0, n, step, 0)

        for k in range(CONV_K):
            dcw_ref[k:k + 1, :] = jnp.sum(accw[8 * k:8 * k + 8, :], axis=0, keepdims=True)
        dcw_ref[CONV_K:32, :] = jnp.zeros((32 - CONV_K, LANE), F32)

    nb = CONV_DIM // LANE
    colspec = pl.BlockSpec((t, LANE), lambda j: (0, j))
    return pl.pallas_call(
        body, name="conv_bwd", grid=(nb,),
        in_specs=[colspec, pl.BlockSpec((t, LANE), lambda j: (0, nb + j)),
                  pl.BlockSpec((32, LANE), lambda j: (0, j)), colspec],
        out_specs=[colspec, colspec, pl.BlockSpec((32, LANE), lambda j: (0, j))],
        out_shape=[jax.ShapeDtypeStruct((t, CONV_DIM), F32), jax.ShapeDtypeStruct((t, CONV_DIM), F32),
                   jax.ShapeDtypeStruct((32, CONV_DIM), F32)],
        scratch_shapes=[pltpu.VMEM((t + 32, LANE), F32), pltpu.VMEM((t + 32, LANE), F32),
                        pltpu.VMEM((8 * 32, LANE), F32)],
        compiler_params=_cp("parallel"),
    )(ag, ag, cw, dz)


def _rope(x, c, s1, s2):
    return x * c + pltpu.roll(x, LANE - 16, 1) * s1 + pltpu.roll(x, 16, 1) * s2


def _rope_t(dy, c, s1, s2):
    return dy * c + pltpu.roll(dy * s1, 16, 1) + pltpu.roll(dy * s2, LANE - 16, 1)


def _mla_pre_fwd(cq, ckv, kr, qag, wuq, kvag, wk, wv, qng, kng, rc, rs1, rs2):
    t = cq.shape[0]
    tm = _tile(t, 384)

    def body(cq_ref, ckv_ref, kr_ref, qag_ref, wuq_ref, kvag_ref, wk_ref, wv_ref, qng_ref, kng_ref,
             c_ref, s1_ref, s2_ref, q_ref, k_ref, v_ref, cqn_ref, ckvn_ref):
        cqv = cq_ref[...]
        cqn = (cqv * _rstd(cqv) * qag_ref[...]).astype(_MM)
        cqn_ref[...] = cqn
        ckvv = ckv_ref[...]
        ckvn = (ckvv * _rstd(ckvv) * kvag_ref[...]).astype(_MM)
        ckvn_ref[...] = ckvn
        qraw = jnp.dot(cqn, wuq_ref[...], preferred_element_type=F32)
        kraw = jnp.dot(ckvn, wk_ref[...], preferred_element_type=F32)
        v_ref[...] = jnp.dot(ckvn, wv_ref[...], preferred_element_type=F32).astype(_MM)
        krv = kr_ref[...]
        c, s1, s2 = c_ref[...], s1_ref[...], s2_ref[...]
        for h in range(NH):
            sl = slice(LANE * h, LANE * (h + 1))
            qh = qraw[:, sl]
            qn = qh * _rstd(qh, QK_DIM) * qng_ref[...]
            q_ref[:, sl] = (_rope(qn, c, s1, s2) * ATT_SCALE).astype(_MM)
            kh = kraw[:, sl] + krv
            kn = kh * _rstd(kh, QK_DIM) * kng_ref[...]
            k_ref[:, sl] = _rope(kn, c, s1, s2).astype(_MM)

    hd = NH * LANE
    return pl.pallas_call(
        body, name="mla_pre_fwd", grid=(t // tm,),
        in_specs=[_row(tm, 256), _row(tm, 128), _row(tm, 128), _full((1, 256)), _full((256, hd)),
                  _full((1, 128)), _full((128, hd)), _full((128, hd)), _full((1, LANE)), _full((1, LANE)),
                  _row(tm, LANE), _row(tm, LANE), _row(tm, LANE)],
        out_specs=[_row(tm, hd), _row(tm, hd), _row(tm, hd), _row(tm, 256), _row(tm, 128)],
        out_shape=[jax.ShapeDtypeStruct((t, hd), _MM)] * 3 + [jax.ShapeDtypeStruct((t, 256), _MM),
                                                              jax.ShapeDtypeStruct((t, 128), _MM)],
        compiler_params=_cp("parallel"),
    )(cq, ckv, kr, qag, wuq, kvag, wk, wv, qng, kng, rc, rs1, rs2)


def _mla_pre_bwd(dq, dk, dv, cq, ckv, kr, qag, wuq, kvag, wk, wv, qng, kng, rc, rs1, rs2):
    t = cq.shape[0]
    tm = _tile(t, 192)
    hd = NH * LANE

    def body(dq_ref, dk_ref, dv_ref, cq_ref, ckv_ref, kr_ref, qag_ref, wuq_ref, kvag_ref, wk_ref,
             wv_ref, qng_ref, kng_ref, c_ref, s1_ref, s2_ref,
             dcq_ref, dckv_ref, dkr_ref, dqraw_ref, dkraw_ref, dqag_ref, dkvag_ref, dqng_ref, dkng_ref):
        i = pl.program_id(0)
        cqv = cq_ref[...]
        rq_in = _rstd(cqv)
        cqn = (cqv * rq_in * qag_ref[...]).astype(_MM)
        ckvv = ckv_ref[...]
        rkv_in = _rstd(ckvv)
        ckvn = (ckvv * rkv_in * kvag_ref[...]).astype(_MM)
        qraw = jnp.dot(cqn, wuq_ref[...], preferred_element_type=F32)
        kraw = jnp.dot(ckvn, wk_ref[...], preferred_element_type=F32)
        krv = kr_ref[...]
        c, s1, s2 = c_ref[...], s1_ref[...], s2_ref[...]
        dkr = jnp.zeros((tm, LANE), F32)
        dqng = jnp.zeros((8, LANE), F32)
        dkng = jnp.zeros((8, LANE), F32)
        for h in range(NH):
            sl = slice(LANE * h, LANE * (h + 1))
            qh = qraw[:, sl]
            dqn = _rope_t(dq_ref[:, sl] * ATT_SCALE, c, s1, s2)
            dqh, gq = _rms_bwd(dqn, qh, _rstd(qh, QK_DIM), qng_ref[...], QK_DIM)
            dqraw_ref[:, sl] = dqh.astype(_MM)
            dqng = dqng + _colsum8(gq)
            kh = kraw[:, sl] + krv
            dkn = _rope_t(dk_ref[:, sl], c, s1, s2)
            dkh, gk = _rms_bwd(dkn, kh, _rstd(kh, QK_DIM), kng_ref[...], QK_DIM)
            dkraw_ref[:, sl] = dkh.astype(_MM)
            dkr = dkr + dkh
            dkng = dkng + _colsum8(gk)
        dkr_ref[...] = dkr.astype(_MM)
        dcqn = _mm_nt(dqraw_ref[...], wuq_ref[...])
        dcq, gqa = _rms_bwd(dcqn, cqv, rq_in, qag_ref[...])
        dcq_ref[...] = dcq.astype(_MM)
        dckvn = _mm_nt(dkraw_ref[...], wk_ref[...]) + _mm_nt(dv_ref[...], wv_ref[...])
        dckv, gkva = _rms_bwd(dckvn, ckvv, rkv_in, kvag_ref[...])
        dckv_ref[...] = dckv.astype(_MM)

        @pl.when(i == 0)
        def _():
            dqag_ref[...] = jnp.zeros_like(dqag_ref)
            dkvag_ref[...] = jnp.zeros_like(dkvag_ref)
            dqng_ref[...] = jnp.zeros_like(dqng_ref)
            dkng_ref[...] = jnp.zeros_like(dkng_ref)
        dqag_ref[...] += _colsum8(gqa)
        dkvag_ref[...] += _colsum8(gkva)
        dqng_ref[...] += dqng
        dkng_ref[...] += dkng

    return pl.pallas_call(
        body, name="mla_pre_bwd", grid=(t // tm,),
        in_specs=[_row(tm, hd), _row(tm, hd), _row(tm, hd), _row(tm, 256), _row(tm, 128), _row(tm, 128),
                  _full((1, 256)), _full((256, hd)), _full((1, 128)), _full((128, hd)),
                  _full((128, hd)), _full((1, LANE)), _full((1, LANE)),
                  _row(tm, LANE), _row(tm, LANE), _row(tm, LANE)],
        out_specs=[_row(tm, 256), _row(tm, 128), _row(tm, 128), _row(tm, hd), _row(tm, hd),
                   _full((8, 256)), _full((8, 128)), _full((8, LANE)), _full((8, LANE))],
        out_shape=[jax.ShapeDtypeStruct((t, 256), _MM), jax.ShapeDtypeStruct((t, 128), _MM),
                   jax.ShapeDtypeStruct((t, 128), _MM), jax.ShapeDtypeStruct((t, hd), _MM),
                   jax.ShapeDtypeStruct((t, hd), _MM), jax.ShapeDtypeStruct((8, 256), F32),
                   jax.ShapeDtypeStruct((8, 128), F32), jax.ShapeDtypeStruct((8, LANE), F32),
                   jax.ShapeDtypeStruct((8, LANE), F32)],
        compiler_params=_cp("arbitrary"),
    )(dq, dk, dv, cq, ckv, kr, qag, wuq, kvag, wk, wv, qng, kng, rc, rs1, rs2)


def _attn_mask(r0, c0, tq):
    rows = r0 + lax.broadcasted_iota(jnp.int32, (tq, 1), 0)
    cols = c0 + lax.broadcasted_iota(jnp.int32, (1, tq), 1)
    return (cols <= rows) & (cols >= FRONT)


def _attn_fwd(q, k, v, plan=None):
    t = q.shape[0]
    tq = _tile(t, 384)
    nq = t // tq
    p_args, p_in, p_out, p_shape, p_sem = _plan_specs(plan)

    def body(*refs):
        ((q_ref, k_ref, v_ref), (o_ref, lse_ref), _), rider = _host_refs(refs, 3, 2, 0, plan)
        done = _ride(plan, rider, pl.program_id(0), NH - 1)

        def qloop(qi, carry):
            r0 = pl.multiple_of(qi * tq, tq)
            qb = q_ref[pl.ds(r0, tq), :]

            def kstep(kj, st, masked):
                m, l, acc = st
                c0 = pl.multiple_of(kj * tq, tq)
                s = _mm_nt(qb, k_ref[pl.ds(c0, tq), :])
                if masked:
                    s = jnp.where(_attn_mask(r0, c0, tq), s, NEG)
                m2 = jnp.maximum(m, jnp.max(s, axis=-1, keepdims=True))
                p = jnp.exp(s - m2)
                a = jnp.exp(m - m2)
                l = a * l + jnp.sum(p, axis=-1, keepdims=True)
                acc = a * acc + _mm(p, v_ref[pl.ds(c0, tq), :])
                return m2, l, acc

            st = kstep(0, (jnp.full((tq, 1), NEG, F32), jnp.zeros((tq, 1), F32), jnp.zeros((tq, LANE), F32)), True)
            st = lax.fori_loop(1, qi, lambda kj, s_: kstep(kj, s_, False), st)
            m, l, acc = lax.cond(qi > 0, lambda s_: kstep(qi, s_, True), lambda s_: s_, st)
            o_ref[pl.ds(r0, tq), :] = acc / l
            lse_ref[pl.ds(r0, tq), :] = m + jnp.log(l)
            return carry
        lax.fori_loop(0, nq, qloop, 0)
        done()

    hs = pl.BlockSpec((t, LANE), lambda h: (0, h))
    res = pl.pallas_call(
        body, name="attn_fwd", grid=(NH,),
        in_specs=[hs, hs, hs] + p_in,
        out_specs=[hs, pl.BlockSpec((None, t, 1), lambda h: (h, 0, 0))] + p_out,
        out_shape=[jax.ShapeDtypeStruct((t, NH * LANE), F32), jax.ShapeDtypeStruct((NH, t, 1), F32)] + p_shape,
        scratch_shapes=p_sem,
        compiler_params=_cp("parallel" if plan is None else "arbitrary"),
    )(q, k, v, *p_args)
    return res[:2], res[2:]


def _attn_bwd(q, k, v, o, lse, do, plan=None):
    t = q.shape[0]
    tq = _tile(t, 384)
    nq = t // tq
    p_args, p_in, p_out, p_shape, p_sem = _plan_specs(plan)

    def body(*refs):
        (ins, (dq_ref, dk_ref, dv_ref), (delta,)), rider = _host_refs(refs, 6, 3, 1, plan)
        q_ref, k_ref, v_ref, o_ref, lse_ref, do_ref = ins
        done = _ride(plan, rider, pl.program_id(0), NH - 1)

        def prep(i, c):
            r0 = pl.multiple_of(i * tq, tq)
            delta[pl.ds(r0, tq), :] = jnp.sum(do_ref[pl.ds(r0, tq), :] * o_ref[pl.ds(r0, tq), :], axis=-1,
                                              keepdims=True)
            dq_ref[pl.ds(r0, tq), :] = jnp.zeros((tq, LANE), F32)
            return c
        lax.fori_loop(0, nq, prep, 0)

        def kloop(kj, carry):
            c0 = pl.multiple_of(kj * tq, tq)
            kb = k_ref[pl.ds(c0, tq), :]
            vb = v_ref[pl.ds(c0, tq), :]

            def qstep(qi, st, masked):
                dkb, dvb = st
                r0 = pl.multiple_of(qi * tq, tq)
                qb = q_ref[pl.ds(r0, tq), :]
                dob = do_ref[pl.ds(r0, tq), :].astype(_MM)
                s = _mm_nt(qb, kb)
                if masked:
                    s = jnp.where(_attn_mask(r0, c0, tq), s, NEG)
                p = jnp.exp(s - lse_ref[pl.ds(r0, tq), :])
                dvb = dvb + _mm_tn(p, dob)
                dp = _mm_nt(dob, vb)
                ds = (p * (dp - delta[pl.ds(r0, tq), :])).astype(_MM)
                dkb = dkb + _mm_tn(ds, qb)
                dq_ref[pl.ds(r0, tq), :] += _mm(ds, kb)
                return dkb, dvb

            st = qstep(kj, (jnp.zeros((tq, LANE), F32), jnp.zeros((tq, LANE), F32)), True)
            dkb, dvb = lax.cond(
                kj == 0,
                lambda s_: lax.fori_loop(kj + 1, nq, lambda qi, t_: qstep(qi, t_, True), s_),
                lambda s_: lax.fori_loop(kj + 1, nq, lambda qi, t_: qstep(qi, t_, False), s_), st)
            dk_ref[pl.ds(c0, tq), :] = dkb
            dv_ref[pl.ds(c0, tq), :] = dvb
            return carry
        lax.fori_loop(0, nq, kloop, 0)
        done()

    hs = pl.BlockSpec((t, LANE), lambda h: (0, h))
    res = pl.pallas_call(
        body, name="attn_bwd", grid=(NH,),
        in_specs=[hs, hs, hs, hs, pl.BlockSpec((None, t, 1), lambda h: (h, 0, 0)), hs] + p_in,
        out_specs=[hs, hs, hs] + p_out,
        out_shape=[jax.ShapeDtypeStruct((t, NH * LANE), F32)] * 3 + p_shape,
        scratch_shapes=[pltpu.VMEM((t, 1), F32)] + p_sem,
        compiler_params=_cp("parallel" if plan is None else "arbitrary"),
    )(q, k, v, o, lse, do, *p_args)
    return res[:3], res[3:]


def _cumsum_rows(x):
    n = x.shape[0]
    rows = lax.broadcasted_iota(jnp.int32, (n, 1), 0)
    d = 1
    while d < n:
        x = x + jnp.where(rows >= d, pltpu.roll(x, d, 0), 0.0)
        d *= 2
    return x


def _revcumsum_rows(x):
    n = x.shape[0]
    rows = lax.broadcasted_iota(jnp.int32, (n, 1), 0)
    d = 1
    while d < n:
        x = x + jnp.where(rows < n - d, pltpu.roll(x, n - d, 0), 0.0)
        d *= 2
    return x


def _hgrn_gates(f, lb):
    sneg = _sigmoid(-f)
    kk = (1.0 - lb) * sneg
    lf = jnp.log1p(-jnp.minimum(kk, GATE_CLAMP))
    return kk, lf, sneg


def _silu(x):
    return x * _sigmoid(x)


def _dsilu(x):
    s = _sigmoid(x)
    return s * (1.0 + x * (1.0 - s))


def _hgrn_intra(q, kk, b):
    parts = []
    for blk in range(CHUNK // SUB):
        lo = blk * SUB
        ref = jnp.zeros((1, LANE), F32) if blk == 0 else b[lo - 1:lo, :]
        eq = jnp.exp(b[lo:lo + SUB, :] - ref)
        ek = jnp.exp(jnp.minimum(ref - b, EXP_CLIP))
        parts.append((q[lo:lo + SUB, :] * eq, kk * ek, eq, ek))
    return parts


def _chunk_causal():
    return lax.broadcasted_iota(jnp.int32, (CHUNK, CHUNK), 1) <= lax.broadcasted_iota(jnp.int32, (CHUNK, CHUNK), 0)


def _hgrn_fwd(h4, lb):
    t = h4.shape[0]
    nc = t // CHUNK

    def body(q_ref, f_ref, i_ref, lb_ref, o_ref, s_ref, st):
        st[...] = jnp.zeros_like(st)
        causal = _chunk_causal()

        def chunk(c, carry):
            r0 = pl.multiple_of(c * CHUNK, CHUNK)
            q = q_ref[pl.ds(r0, CHUNK), :]
            kk, lf, _ = _hgrn_gates(f_ref[pl.ds(r0, CHUNK), :], lb_ref[...])
            v = _silu(i_ref[pl.ds(r0, CHUNK), :])
            b = _cumsum_rows(lf)
            s_prev = st[...]
            s_ref[c] = s_prev
            o = _hmm_nt(q * jnp.exp(b), s_prev)
            a = jnp.concatenate([_hmm_nt(qs, ks) for qs, ks, _, _ in _hgrn_intra(q, kk, b)], axis=0)
            a = jnp.where(causal, a, 0.0)
            o_ref[pl.ds(r0, CHUNK), :] = o + _hmm(a, v)
            bl = b[CHUNK - 1:CHUNK, :]
            st[...] = s_prev * jnp.exp(bl) + _hmm_tn(v, kk * jnp.exp(bl - b))
            return carry
        lax.fori_loop(0, nc, chunk, 0, unroll=2)

    def col(j):
        return pl.BlockSpec((t, LANE), lambda h: (0, HH * j + h))
    return pl.pallas_call(
        body, name="hgrn_fwd", grid=(HH,),
        in_specs=[col(0), col(1), col(2), pl.BlockSpec((1, LANE), lambda h: (0, h))],
        out_specs=[pl.BlockSpec((t, LANE), lambda h: (0, h)),
                   pl.BlockSpec((None, nc, LANE, LANE), lambda h: (h, 0, 0, 0))],
        out_shape=[jax.ShapeDtypeStruct((t, HH * LANE), F32), jax.ShapeDtypeStruct((HH, nc, LANE, LANE), F32)],
        scratch_shapes=[pltpu.VMEM((LANE, LANE), F32)],
        compiler_params=_cp("parallel"),
    )(h4, h4, h4, lb)


def _hgrn_bwd(h4, lb, do, states):
    t = h4.shape[0]
    nc = t // CHUNK

    def body(q_ref, f_ref, i_ref, lb_ref, do_ref, s_ref, dq_ref, df_ref, di_ref, dlb_ref, dst, carry):
        dst[...] = jnp.zeros_like(dst)
        carry[...] = jnp.zeros_like(carry)
        dlb_ref[...] = jnp.zeros_like(dlb_ref)
        causal = _chunk_causal()

        def chunk(cc, cr):
            c = nc - 1 - cc
            r0 = pl.multiple_of(c * CHUNK, CHUNK)
            q = q_ref[pl.ds(r0, CHUNK), :]
            lbv = lb_ref[...]
            kk, lf, sneg = _hgrn_gates(f_ref[pl.ds(r0, CHUNK), :], lbv)
            iv = i_ref[pl.ds(r0, CHUNK), :]
            v = _silu(iv)
            b = _cumsum_rows(lf)
            s_prev = s_ref[c]
            ds_new = dst[...]
            dob = do_ref[pl.ds(r0, CHUNK), :]
            e = jnp.exp(b)
            qe = q * e
            bl = b[CHUNK - 1:CHUNK, :]
            etail = jnp.exp(bl - b)
            kd = kk * etail
            dq_inter = _hmm(dob, s_prev) * e
            dv = _hmm_nt(kd, ds_new)
            dkk = _hmm(v, ds_new) * etail
            parts = _hgrn_intra(q, kk, b)
            a = jnp.where(causal, jnp.concatenate([_hmm_nt(qs, ks) for qs, ks, _, _ in parts], axis=0), 0.0)
            da = jnp.where(causal, _hmm_nt(dob, v), 0.0)
            dv = dv + _hmm_tn(a, dob)
            dq_rows = []
            for blk, (qs, ks, eq, ek) in enumerate(parts):
                da_blk = da[blk * SUB:(blk + 1) * SUB, :]
                dq_rows.append(_hmm(da_blk, ks) * eq)
                dkk = dkk + _hmm_tn(da_blk, qs) * ek
            dq = dq_inter + jnp.concatenate(dq_rows, axis=0)
            dst[...] = ds_new * jnp.exp(bl) + _hmm_tn(dob, qe)
            g = q * dq - kk * dkk
            dlf = _revcumsum_rows(g) + carry[0:1, :]
            carry[0:1, :] += jnp.sum(g, axis=0, keepdims=True)
            dkk_tot = dkk + dlf * jnp.where(kk < GATE_CLAMP, -1.0 / (1.0 - kk), 0.0)
            dq_ref[pl.ds(r0, CHUNK), :] = dq
            df_ref[pl.ds(r0, CHUNK), :] = dkk_tot * (1.0 - lbv) * (-sneg * (1.0 - sneg))
            di_ref[pl.ds(r0, CHUNK), :] = dv * _dsilu(iv)
            dlb_ref[...] += _colsum8(dkk_tot * (-sneg))
            return cr
        lax.fori_loop(0, nc, chunk, 0, unroll=2)

    def col(j):
        return pl.BlockSpec((t, LANE), lambda h: (0, HH * j + h))
    hs = pl.BlockSpec((t, LANE), lambda h: (0, h))
    return pl.pallas_call(
        body, name="hgrn_bwd", grid=(HH,),
        in_specs=[col(0), col(1), col(2), pl.BlockSpec((1, LANE), lambda h: (0, h)), hs,
                  pl.BlockSpec((None, nc, LANE, LANE), lambda h: (h, 0, 0, 0))],
        out_specs=[hs, hs, hs, pl.BlockSpec((8, LANE), lambda h: (0, h))],
        out_shape=[jax.ShapeDtypeStruct((t, HH * LANE), F32)] * 3 + [jax.ShapeDtypeStruct((8, HH * LANE), F32)],
        scratch_shapes=[pltpu.VMEM((LANE, LANE), F32), pltpu.VMEM((8, LANE), F32)],
        compiler_params=_cp("parallel"),
    )(h4, h4, h4, lb, do, states)


def _ln_fwd(z, g, b):
    mu = jnp.mean(z, axis=-1, keepdims=True)
    zc = z - mu
    rstd = lax.rsqrt(jnp.mean(zc * zc, axis=-1, keepdims=True) + EPS)
    zh = zc * rstd
    return zh * g + b, zh, rstd


def _mix_fwd(x, z, o_att, o_h, h4, gates, lng, lnb, wco, wao, ng, who, wout, t_end):
    t = x.shape[0]
    tm = _tile(t, 192)

    def body(x_ref, z_ref, oa_ref, oh_ref, hg_ref, gt_ref, lng_ref, lnb_ref, wco_ref, wao_ref, ng_ref, who_ref,
             wout_ref, x1_ref, mix_ref, ca_ref, oc_ref, ya_ref, yb_ref, yc_ref):
        i = pl.program_id(0)
        ln, _, _ = _ln_fwd(z_ref[...], lng_ref[...], lnb_ref[...])
        ca = _silu(ln).astype(_MM)
        ca_ref[...] = ca
        ya = jnp.dot(ca, wco_ref[...], preferred_element_type=F32)
        yb = _mm(oa_ref[...], wao_ref[...])
        hg = hg_ref[...]
        for h in range(HH):
            sl = slice(LANE * h, LANE * (h + 1))
            oh = oh_ref[:, sl]
            oc_ref[:, sl] = (oh * _rstd(oh) * ng_ref[:, sl] * _silu(hg[:, sl])).astype(_MM)
        yc = jnp.dot(oc_ref[...], who_ref[...], preferred_element_type=F32)
        ya_ref[...] = ya
        yb_ref[...] = yb
        yc_ref[...] = yc
        mix = (_sigmoid(gt_ref[:, 0:D]) * ya + _sigmoid(gt_ref[:, D:2 * D]) * yb
               + _sigmoid(gt_ref[:, 2 * D:3 * D]) * yc).astype(_MM)
        mix_ref[...] = mix
        x1_ref[...] = x_ref[...] + _valid_rows(i, tm, t_end) * jnp.dot(mix, wout_ref[...],
                                                                       preferred_element_type=F32)

    hd = NH * LANE
    return pl.pallas_call(
        body, name="mix_fwd", grid=(t // tm,),
        in_specs=[_row(tm, D), _row(tm, CONV_DIM), _row(tm, hd), _row(tm, 512), _row(tm, 512, 3), _row(tm, 3 * D),
                  _full((1, 512)), _full((1, 512)), _full((512, D)), _full((hd, D)), _full((1, 512)),
                  _full((512, D)), _full((D, D))],
        out_specs=[_row(tm, D), _row(tm, D), _row(tm, 512), _row(tm, 512), _row(tm, D), _row(tm, D), _row(tm, D)],
        out_shape=[jax.ShapeDtypeStruct((t, D), F32), jax.ShapeDtypeStruct((t, D), _MM),
                   jax.ShapeDtypeStruct((t, 512), _MM), jax.ShapeDtypeStruct((t, 512), _MM),
                   jax.ShapeDtypeStruct((t, D), F32), jax.ShapeDtypeStruct((t, D), F32),
                   jax.ShapeDtypeStruct((t, D), F32)],
        compiler_params=_cp("parallel"),
    )(x, z, o_att, o_h, h4, gates, lng, lnb, wco, wao, ng, who, wout)


def _mix_bwd(dx1, ya, yb, yc, gates, z, o_h, h4, lng, lnb, ng, wout, wco, wao, who, plan=None):
    t = dx1.shape[0]
    tm = _tile(t, 192)
    hd = NH * LANE
    p_args, p_in, p_out, p_shape, p_sem = _plan_specs(plan)

    def body(*refs):
        (ins, outs, _), rider = _host_refs(refs, 15, 12, 0, plan)
        (dx1_ref, ya_ref, yb_ref, yc_ref, gt_ref, z_ref, oh_ref, hg_ref, lng_ref, lnb_ref, ng_ref,
         wout_ref, wco_ref, wao_ref, who_ref) = ins
        (dgt_ref, dya_ref, dyb_ref, dyc_ref, dz_ref, doa_ref, doh_ref, dhg_ref,
         dlng_ref, dlnb_ref, dcb_ref, dng_ref) = outs
        i = pl.program_id(0)
        done = _ride(plan, rider, i, t // tm - 1)
        dmix = _mm_nt(dx1_ref[...], wout_ref[...])
        dys = []
        for j, y_ref in enumerate((ya_ref, yb_ref, yc_ref)):
            sg = _sigmoid(gt_ref[:, j * D:(j + 1) * D])
            dgt_ref[:, j * D:(j + 1) * D] = (dmix * y_ref[...] * sg * (1.0 - sg)).astype(_MM)
            dys.append((dmix * sg).astype(_MM))
        dya_ref[...], dyb_ref[...], dyc_ref[...] = dys
        dca = _mm_nt(dys[0], wco_ref[...])
        ln, zh, rstd = _ln_fwd(z_ref[...], lng_ref[...], lnb_ref[...])
        dln = dca * _dsilu(ln)
        dzh = dln * lng_ref[...]
        dz = rstd * (dzh - jnp.mean(dzh, axis=-1, keepdims=True)
                     - zh * jnp.mean(dzh * zh, axis=-1, keepdims=True))
        dz_ref[...] = dz
        doa_ref[...] = _mm_nt(dys[1], wao_ref[...])
        doc = _mm_nt(dys[2], who_ref[...])
        hg = hg_ref[...]
        dng_rows = []
        for h in range(HH):
            sl = slice(LANE * h, LANE * (h + 1))
            oh = oh_ref[:, sl]
            r = _rstd(oh)
            don = doc[:, sl] * _silu(hg[:, sl])
            dhg_ref[:, sl] = (doc[:, sl] * oh * r * ng_ref[:, sl] * _dsilu(hg[:, sl])).astype(_MM)
            doh, gn = _rms_bwd(don, oh, r, ng_ref[:, sl])
            doh_ref[:, sl] = doh
            dng_rows.append(_colsum8(gn))

        @pl.when(i == 0)
        def _():
            dlng_ref[...] = jnp.zeros_like(dlng_ref)
            dlnb_ref[...] = jnp.zeros_like(dlnb_ref)
            dcb_ref[...] = jnp.zeros_like(dcb_ref)
            dng_ref[...] = jnp.zeros_like(dng_ref)
        dlng_ref[...] += _colsum8(dln * zh)
        dlnb_ref[...] += _colsum8(dln)
        dcb_ref[...] += _colsum8(dz)
        dng_ref[...] += jnp.concatenate(dng_rows, axis=1)
        done()

    res = pl.pallas_call(
        body, name="mix_bwd", grid=(t // tm,),
        in_specs=[_row(tm, D), _row(tm, D), _row(tm, D), _row(tm, D), _row(tm, 3 * D), _row(tm, 512), _row(tm, 512),
                  _row(tm, 512, 3), _full((1, 512)), _full((1, 512)), _full((1, 512)),
                  _full((D, D)), _full((512, D)), _full((hd, D)), _full((512, D))] + p_in,
        out_specs=[_row(tm, 3 * D), _row(tm, D), _row(tm, D), _row(tm, D), _row(tm, 512), _row(tm, hd),
                   _row(tm, 512), _row(tm, 512), _full((8, 512)), _full((8, 512)), _full((8, 512)),
                   _full((8, 512))] + p_out,
        out_shape=[jax.ShapeDtypeStruct((t, 3 * D), _MM), jax.ShapeDtypeStruct((t, D), _MM),
                   jax.ShapeDtypeStruct((t, D), _MM), jax.ShapeDtypeStruct((t, D), _MM),
                   jax.ShapeDtypeStruct((t, 512), F32), jax.ShapeDtypeStruct((t, hd), F32),
                   jax.ShapeDtypeStruct((t, 512), F32), jax.ShapeDtypeStruct((t, 512), _MM)]
        + [jax.ShapeDtypeStruct((8, 512), F32)] * 4 + p_shape,
        scratch_shapes=p_sem,
        compiler_params=_cp("arbitrary"),
    )(dx1, ya, yb, yc, gates, z, o_h, h4, lng, lnb, ng, wout, wco, wao, who, *p_args)
    return res[:12], res[12:]


D_FF = 4096


def _ffn_fwd(x1, g2, w1, w2):
    t = x1.shape[0]
    tm = _tile(t, 192)

    def body(x1_ref, g_ref, w1_ref, w2_ref, x2_ref, p_ref):
        xv = x1_ref[...]
        h2 = (xv * _rstd(xv) * g_ref[...]).astype(_MM)
        p = jnp.dot(h2, w1_ref[...], preferred_element_type=F32)
        p_ref[...] = p
        r = jnp.maximum(p, 0.0)
        x2_ref[...] = xv + jnp.dot((r * r).astype(_MM), w2_ref[...], preferred_element_type=F32)

    return pl.pallas_call(
        body, name="ffn_fwd", grid=(t // tm,),
        in_specs=[_row(tm, D), _full((1, D)), _full((D, D_FF)), _full((D_FF, D))],
        out_specs=[_row(tm, D), _row(tm, D_FF)],
        out_shape=[jax.ShapeDtypeStruct((t, D), F32), jax.ShapeDtypeStruct((t, D_FF), F32)],
        compiler_params=_cp("parallel"),
    )(x1, g2, w1, w2)


def _ffn_bwd(dx2, x1, p, g2, w1t, w2t, plan=None):
    t = x1.shape[0]
    tm = _tile(t, 192)
    p_args, p_in, p_out, p_shape, p_sem = _plan_specs(plan)

    def body(*refs):
        (ins, outs, _), rider = _host_refs(refs, 6, 5, 0, plan)
        dx2_ref, x1_ref, p_ref, g_ref, w1t_ref, w2t_ref = ins
        dx1_ref, h2_ref, act_ref, dp_ref, dg_ref = outs
        i = pl.program_id(0)
        done = _ride(plan, rider, i, t // tm - 1)
        xv = x1_ref[...]
        rstd = _rstd(xv)
        h2_ref[...] = (xv * rstd * g_ref[...]).astype(_MM)
        r = jnp.maximum(p_ref[...], 0.0)
        act_ref[...] = (r * r).astype(_MM)
        dx2 = dx2_ref[...]
        da = _mm(dx2, w2t_ref[...])
        dp = (2.0 * r * da).astype(_MM)
        dp_ref[...] = dp
        dh2 = jnp.dot(dp, w1t_ref[...], preferred_element_type=F32)
        dxn, dgrow = _rms_bwd(dh2, xv, rstd, g_ref[...])
        dx1_ref[...] = dx2 + dxn

        @pl.when(i == 0)
        def _():
            dg_ref[...] = jnp.zeros_like(dg_ref)
        dg_ref[...] += _colsum8(dgrow)
        done()

    res = pl.pallas_call(
        body, name="ffn_bwd", grid=(t // tm,),
        in_specs=[_row(tm, D), _row(tm, D), _row(tm, D_FF), _full((1, D)), _full((D_FF, D)),
                  _full((D, D_FF))] + p_in,
        out_specs=[_row(tm, D), _row(tm, D), _row(tm, D_FF), _row(tm, D_FF), _full((8, D))] + p_out,
        out_shape=[jax.ShapeDtypeStruct((t, D), F32), jax.ShapeDtypeStruct((t, D), _MM),
                   jax.ShapeDtypeStruct((t, D_FF), _MM), jax.ShapeDtypeStruct((t, D_FF), _MM),
                   jax.ShapeDtypeStruct((8, D), F32)] + p_shape,
        scratch_shapes=p_sem,
        compiler_params=_cp("arbitrary"),
    )(dx2, x1, p, g2, w1t, w2t, *p_args)
    return res[:5], res[5:]


def _wgrad(a, b, name, chips=1):
    t, ka = a.shape
    nb = b.shape[1]
    tm = _tile(t, 384)
    cs = nb // chips
    tn = max(d for d in range(LANE, cs + 1, LANE) if cs % d == 0 and (ka * d * 4 <= 16 * 1024 * 1024 or d == LANE))
    per = cs // tn

    def body(a_ref, b_ref, o_ref):
        @pl.when(pl.program_id(1) == 0)
        def _():
            o_ref[...] = jnp.zeros_like(o_ref)
        o_ref[...] += _mm_tn(a_ref[...], b_ref[...])

    if chips == 1:
        out_spec = pl.BlockSpec((ka, tn), lambda n, i: (0, n))
        out_shape = jax.ShapeDtypeStruct((ka, nb), F32)
    else:
        out_spec = pl.BlockSpec((None, ka, tn), lambda n, i: (n // per, 0, n % per))
        out_shape = jax.ShapeDtypeStruct((chips, ka, cs), F32)
    return pl.pallas_call(
        body, name="wgrad_" + name, grid=(nb // tn, t // tm),
        in_specs=[pl.BlockSpec((tm, ka), lambda n, i: (i, 0)), pl.BlockSpec((tm, tn), lambda n, i: (i, n))],
        out_specs=out_spec, out_shape=out_shape,
        compiler_params=_cp("parallel", "arbitrary"),
    )(a, b)


def _loss_head(y, target, t_end):
    t = y.shape[0]
    tm = _tile(t, 384)

    def body(y_ref, tg_ref, dy_ref, l_ref):
        i = pl.program_id(0)
        r = i * tm + lax.broadcasted_iota(jnp.int32, (tm, 1), 0)
        real = ((r >= ROW0) & (r < t_end)).astype(F32)
        diff = (y_ref[...] - tg_ref[...]) * real
        dy_ref[...] = diff * (1.0 / D)

        @pl.when(i == 0)
        def _():
            l_ref[...] = jnp.zeros_like(l_ref)
        sq = _colsum8(diff * diff)
        part = sq[:, 0:LANE]
        for j in range(1, D // LANE):
            part = part + sq[:, j * LANE:(j + 1) * LANE]
        l_ref[...] += part * (0.5 / D)

    return pl.pallas_call(
        body, name="loss_head", grid=(t // tm,),
        in_specs=[_row(tm, D), _row(tm, D)],
        out_specs=[_row(tm, D), _full((8, LANE))],
        out_shape=[jax.ShapeDtypeStruct((t, D), F32), jax.ShapeDtypeStruct((8, LANE), F32)],
        compiler_params=_cp("arbitrary"),
    )(y, target)


def _lower_bounds_fwd(logits):
    depth, n = logits.shape

    def body(l_ref, lb_ref):
        lg = l_ref[...]
        m = jnp.max(lg, axis=0, keepdims=True)
        e = jnp.exp(lg - m)
        p = e / jnp.sum(e, axis=0, keepdims=True)
        acc = jnp.zeros((1, n), F32)
        for l in range(depth):
            if l > 0:
                acc = acc + p[l:l + 1, :]
            lb_ref[l:l + 1, :] = acc

    return pl.pallas_call(body, name="lower_bounds_fwd", out_shape=jax.ShapeDtypeStruct((depth, n), F32))(logits)


def _lower_bounds_bwd(logits, dlb):
    depth, n = logits.shape

    def body(l_ref, dlb_ref, dl_ref):
        lg = l_ref[...]
        m = jnp.max(lg, axis=0, keepdims=True)
        e = jnp.exp(lg - m)
        p = e / jnp.sum(e, axis=0, keepdims=True)
        dps = [jnp.zeros((1, n), F32)]
        for j in range(1, depth):
            acc = jnp.zeros((1, n), F32)
            for l in range(j, depth):
                acc = acc + dlb_ref[l:l + 1, :]
            dps.append(acc)
        dot = jnp.zeros((1, n), F32)
        for j in range(depth):
            dot = dot + p[j:j + 1, :] * dps[j]
        for j in range(depth):
            dl_ref[j:j + 1, :] = p[j:j + 1, :] * (dps[j] - dot)

    return pl.pallas_call(body, name="lower_bounds_bwd", out_shape=jax.ShapeDtypeStruct((depth, n), F32))(logits, dlb)


def _ew_tile(rows, cols, n_arrays):
    cap = max(16, (24 * 1024 * 1024) // (8 * n_arrays * cols))
    best = None
    for t in range(16, rows + 1, 16):
        if rows % t == 0 and t <= cap:
            best = t
    return rows if best is None else best


def _adamw_math(w, g, m, v):
    mn = ADAM_B1 * m + (1.0 - ADAM_B1) * g
    vn = ADAM_B2 * v + (1.0 - ADAM_B2) * (g * g)
    m_hat = mn / (1.0 - ADAM_B1 ** ADAM_STEP)
    v_hat = vn / (1.0 - ADAM_B2 ** ADAM_STEP)
    return -ADAM_LR * (m_hat / (jnp.sqrt(v_hat) + ADAM_EPS) + ADAM_WD * w), mn, vn


def _adamw_layers(w, m, v, g0, g1, g_sibling, name):
    _, r, c_ = w.shape
    tr = _ew_tile(r, c_, 10)

    def body(w_ref, m_ref, v_ref, g0_ref, g1_ref, gs_ref, g_ref, d_ref, mo_ref, vo_ref):
        layer = pl.program_id(0)
        own = jnp.where(layer == 0, g0_ref[...], g1_ref[...])
        g = jnp.where(layer == lax.axis_index("c"), own, gs_ref[...])
        g_ref[...] = g
        d_ref[...], mo_ref[...], vo_ref[...] = _adamw_math(w_ref[...], g, m_ref[...], v_ref[...])

    lay = pl.BlockSpec((None, tr, c_), lambda l, i: (l, i, 0))
    flat = pl.BlockSpec((tr, c_), lambda l, i: (i, 0))
    return pl.pallas_call(
        body, name="adamw_" + name, grid=(2, r // tr),
        in_specs=[lay, lay, lay, flat, flat, flat], out_specs=[lay] * 4,
        out_shape=[jax.ShapeDtypeStruct(w.shape, F32)] * 4,
        compiler_params=_cp("parallel", "parallel"),
    )(w, m, v, g0, g1, g_sibling)


def _adamw(w, g, m, v, name):
    rows, cols = w.shape
    tr = _ew_tile(rows, cols, 7)

    def body(w_ref, g_ref, m_ref, v_ref, d_ref, mo_ref, vo_ref):
        d_ref[...], mo_ref[...], vo_ref[...] = _adamw_math(w_ref[...], g_ref[...], m_ref[...], v_ref[...])

    spec = pl.BlockSpec((tr, cols), lambda i: (i, 0))
    return pl.pallas_call(
        body, name="adamw_" + name, grid=(rows // tr,),
        in_specs=[spec] * 4, out_specs=[spec] * 3,
        out_shape=[jax.ShapeDtypeStruct((rows, cols), F32)] * 3,
        compiler_params=_cp("parallel"),
    )(w, g, m, v)


DEPTH = 2
BIG_SHAPES = {"w_in": ((1024, 6560), 1), "w_conv_out": ((512, 1024), 1), "w_uq": ((256, 768), 1),
              "w_ukv": ((128, 1024), 1), "w_attn_out": ((512, 1024), 1), "w_hgrn_out": ((512, 1024), 1),
              "w_out": ((1024, 1024), 0), "w_ff1": ((1024, 4096), 1), "w_ff2": ((4096, 1024), 0)}
BIG = tuple(BIG_SHAPES)
SMALL_SIZES = {"norm1_g": 1024, "conv_b": 512, "conv_ln_g": 512, "conv_ln_b": 512, "q_a_norm_g": 256,
               "kv_a_norm_g": 128, "q_norm_g": 96, "k_norm_g": 96, "hgrn_lb_logits": 512, "hgrn_norm_g": 512,
               "norm2_g": 1024}
SMALL = tuple(SMALL_SIZES)
W_IN_COLS = 6560
W_IN_SHARD = W_IN_COLS // 4
W_IN_SEGS = ((0, 1024, SEG_AG[0]), (1024, 1280, SEG_CQ[0]), (1280, 1408, SEG_CKV[0]), (1408, 1440, SEG_KR[0] + 64),
             (1440, 3488, SEG_H4[0]), (3488, 6560, SEG_GATES[0]))


def _pad_heads(w, nh, used, axis):
    shp = w.shape
    w = w.reshape(shp[:axis] + (nh, used) + shp[axis + 1:])
    pad = [(0, 0)] * w.ndim
    pad[axis + 1] = (0, LANE - used)
    w = jnp.pad(w, pad)
    return w.reshape(shp[:axis] + (nh * LANE,) + shp[axis + 1:])


def _unpad_heads(w, nh, used, axis):
    shp = w.shape
    w = w.reshape(shp[:axis] + (nh, LANE) + shp[axis + 1:])
    w = lax.slice_in_dim(w, 0, used, axis=axis + 1)
    return w.reshape(shp[:axis] + (nh * used,) + shp[axis + 1:])


def _w_in_from_chips(p4):
    def orig(a, b):
        out = []
        while a < b:
            s = a // W_IN_SHARD
            e = min(b, (s + 1) * W_IN_SHARD)
            out.append(p4[s][:, a - W_IN_SHARD * s:e - W_IN_SHARD * s])
            a = e
        return out
    zc = lambda n: jnp.zeros((D, n), p4[0].dtype)
    parts = (orig(3488, 6560) + orig(0, 1024) + orig(1440, 3488) + orig(1024, 1280) + orig(1280, 1408)
             + [zc(64)] + orig(1408, 1440) + [zc(32)])
    return jnp.concatenate(parts, axis=1)


def _w_in_grad_to_chips(dw):
    chips = []
    for s in range(4):
        a, b = W_IN_SHARD * s, W_IN_SHARD * (s + 1)
        parts = []
        for o0, o1, p0 in W_IN_SEGS:
            lo, hi = max(a, o0), min(b, o1)
            if lo < hi:
                parts.append(dw[:, p0 + lo - o0:p0 + hi - o0])
        chips.append(jnp.concatenate(parts, axis=1))
    return jnp.stack(chips)


def _cat_chips(p4, axis):
    return jnp.concatenate([p4[s] for s in range(4)], axis=axis)


EARLY = ("w_in", "w_uq", "w_ukv")
LATE = tuple(k for k in BIG if k not in EARLY)


def _prep_late(pieces):
    pc = lambda k: [pieces[k][s].astype(_MM) for s in range(4)]
    return dict(wao=_pad_heads(_cat_chips(pc("w_attn_out"), 1), NH, 64, 0), wco=_cat_chips(pc("w_conv_out"), 1),
                who=_cat_chips(pc("w_hgrn_out"), 1), wout=_cat_chips(pc("w_out"), 0),
                w1=_cat_chips(pc("w_ff1"), 1), w2=_cat_chips(pc("w_ff2"), 0))


def _prep_early(pieces, small, l):
    mm = lambda a: a.astype(_MM)
    pc = lambda k: [mm(pieces[k][s]) for s in range(4)]
    w_in_p = _w_in_from_chips(pc("w_in"))
    wuq = jnp.concatenate([_pad_heads(pc("w_uq")[s], 2, QK_DIM, 1) for s in range(4)], axis=1)
    wukv = _cat_chips(pc("w_ukv"), 1).reshape(128, NH, 128)
    wk = _pad_heads(wukv[:, :, :64].reshape(128, NH * 64), NH, 64, 1)
    wv = _pad_heads(wukv[:, :, 64:].reshape(128, NH * 64), NH, 64, 1)
    row = lambda a: a.astype(F32).reshape(1, -1)
    p = dict(
        w_in=w_in_p, w_in_t=w_in_p.T, wuq=wuq, wk=wk, wv=wv,
        g1=row(small["norm1_g"][l]), g2=row(small["norm2_g"][l]),
        cw=jnp.pad(small["conv_w"][l].astype(F32), ((0, 1), (0, 0))), cb=row(small["conv_b"][l]),
        lng=row(small["conv_ln_g"][l]), lnb=row(small["conv_ln_b"][l]),
        qag=row(small["q_a_norm_g"][l]), kvag=row(small["kv_a_norm_g"][l]),
        qng=jnp.pad(row(small["q_norm_g"][l]), ((0, 0), (0, LANE - QK_DIM))),
        kng=jnp.pad(row(small["k_norm_g"][l]), ((0, 0), (0, LANE - QK_DIM))),
        ng=row(small["hgrn_norm_g"][l]),
    )
    return p


def _rope_tables(t):
    pos = (jnp.arange(t, dtype=jnp.int32) - FRONT).astype(F32)
    inv_freq = 10000.0 ** (-jnp.arange(16, dtype=F32) / 16)
    ang = pos[:, None] * inv_freq[None, :]
    cos, sin = jnp.cos(ang), jnp.sin(ang)
    one = jnp.ones((t, 64), F32)
    z16, z32, z64 = jnp.zeros((t, 16), F32), jnp.zeros((t, 32), F32), jnp.zeros((t, 64), F32)
    c = jnp.concatenate([one, cos, cos, z32], axis=1)
    s1 = jnp.concatenate([z64, -sin, z16, z32], axis=1)
    s2 = jnp.concatenate([z64, z16, sin, z32], axis=1)
    return c, s1, s2


def _layer_fwd(x, p, lb, rope, t_end, plan=None, on_rode=None):
    gates, ag, h4, cq, ckv, kr, hb = _in_proj_fwd(x, p["g1"], p["w_in"])
    z = _conv_fwd(ag, p["cw"], p["cb"])
    q, k, v, cqn, ckvn = _mla_pre_fwd(cq, ckv, kr, p["qag"], p["wuq"], p["kvag"], p["wk"], p["wv"], p["qng"],
                                      p["kng"], *rope)
    (o_att, lse), rode = _attn_fwd(q, k, v, plan)
    if on_rode is not None:
        on_rode(rode)
    o_h, states = _hgrn_fwd(h4, lb)
    x1, mix, ca, oc, ya, yb, yc = _mix_fwd(x, z, o_att, o_h, h4, gates, p["lng"], p["lnb"], p["wco"], p["wao"],
                                           p["ng"], p["who"], p["wout"], t_end)
    x2, pre = _ffn_fwd(x1, p["g2"], p["w1"], p["w2"])
    saved = dict(x=x, gates=gates, ag=ag, h4=h4, cq=cq, ckv=ckv, kr=kr, hb=hb, z=z, q=q, k=k, v=v, cqn=cqn,
                 ckvn=ckvn, o_att=o_att, lse=lse, o_h=o_h, states=states, x1=x1, mix=mix, ca=ca, oc=oc,
                 ya=ya, yb=yb, yc=yc, pre=pre)
    return x2, saved


def _layer_bwd(dx2, s, p, lb, rope, t_end, rides=None):
    rides = rides or {}
    (dx1, h2, act, dp, dg2), rode = _ffn_bwd(dx2, s["x1"], s["pre"], p["g2"], p["w1"].T, p["w2"].T,
                                             rides.get("ffn"))
    g = {"w_ff1": _wgrad(h2, dp, "ff1", 4), "w_ff2": _wgrad(act, dx2, "ff2").reshape(4, D_FF // 4, D),
         "norm2_g": dg2.sum(0)}
    plan_mix = rides["mix"](rode, g) if "mix" in rides else None
    (dgt, dya, dyb, dyc, dz, doa, doh, dhg, dlng, dlnb, dcb, dng), rode = _mix_bwd(
        dx1, s["ya"], s["yb"], s["yc"], s["gates"], s["z"], s["o_h"], s["h4"], p["lng"], p["lnb"], p["ng"],
        p["wout"], p["wco"], p["wao"], p["who"], plan_mix)
    plan_attn = rides["attn"](rode) if "attn" in rides else None
    g["w_out"] = _wgrad(s["mix"], dx1, "out").reshape(4, D // 4, D)
    g["w_conv_out"] = _wgrad(s["ca"], dya, "conv_out", 4)
    g["w_attn_out"] = _unpad_heads(_wgrad(s["o_att"], dyb, "attn_out", 4), NH, 64, 1)
    g["w_hgrn_out"] = _wgrad(s["oc"], dyc, "hgrn_out", 4)
    g["conv_ln_g"], g["conv_ln_b"], g["conv_b"], g["hgrn_norm_g"] = dlng.sum(0), dlnb.sum(0), dcb.sum(0), dng.sum(0)
    da, dg, dcw = _conv_bwd(s["ag"], p["cw"], dz)
    g["conv_w"] = dcw[:CONV_K]
    (dq, dk, dv), rode_attn = _attn_bwd(s["q"], s["k"], s["v"], s["o_att"], s["lse"], doa, plan_attn)
    dcq, dckv, dkr, dqraw, dkraw, dqag, dkvag, dqng, dkng = _mla_pre_bwd(
        dq, dk, dv, s["cq"], s["ckv"], s["kr"], p["qag"], p["wuq"], p["kvag"], p["wk"], p["wv"], p["qng"],
        p["kng"], *rope)
    g["w_uq"] = _unpad_heads(_wgrad(s["cqn"], dqraw, "uq", 4), 2, QK_DIM, 2)
    dwk = _unpad_heads(_wgrad(s["ckvn"], dkraw, "uk"), NH, 64, 1).reshape(128, NH, 64)
    dwv = _unpad_heads(_wgrad(s["ckvn"], dv, "uv"), NH, 64, 1).reshape(128, NH, 64)
    g["w_ukv"] = jnp.concatenate([dwk, dwv], axis=2).reshape(128, 4, 256).transpose(1, 0, 2)
    g["q_a_norm_g"], g["kv_a_norm_g"] = dqag.sum(0), dkvag.sum(0)
    g["q_norm_g"], g["k_norm_g"] = dqng.sum(0)[:QK_DIM], dkng.sum(0)[:QK_DIM]
    dhq, dhf, dhi, dlb = _hgrn_bwd(s["h4"], lb, doh, s["states"])
    mm = lambda a: a.astype(_MM)
    du = jnp.concatenate([dgt, mm(da), mm(dg), mm(dhq), mm(dhf), mm(dhi), dhg, dcq, dckv, dkr], axis=1)
    dx, dg1 = _in_proj_bwd(du, s["x"], dx1, p["g1"], p["w_in_t"], t_end)
    g["norm1_g"] = dg1.sum(0)
    g["w_in"] = _w_in_grad_to_chips(_wgrad(s["hb"], du, "in"))
    return dx, g, dlb.sum(0), rode_attn


def _device_step(x, target, small, pieces0, pieces1=None, fwd_ride=None, bwd_rides=None):
    s_real = x.shape[0]
    t_end = ROW0 + s_real
    t = -(-t_end // LANE) * LANE
    zrow = lambda n: jnp.zeros((n, D), F32)
    xp = jnp.concatenate([zrow(FRONT), small["meta"].astype(F32), x, zrow(t - t_end)], axis=0)
    tp = jnp.concatenate([zrow(ROW0), target, zrow(t - t_end)], axis=0)
    rope = _rope_tables(t)
    logits = small["hgrn_lb_logits"].astype(F32)
    lbs = _lower_bounds_fwd(logits)
    prm0 = _prep_early(pieces0, small, 0)
    got = {}
    if fwd_ride is None:
        prm0.update(_prep_late(pieces0))
        h, sv0 = _layer_fwd(xp, prm0, lbs[0:1], rope, t_end)
    else:
        def on_rode(rode):
            late0, got["pieces1"] = fwd_ride[1](rode)
            prm0.update(_prep_late(late0))
        h, sv0 = _layer_fwd(xp, prm0, lbs[0:1], rope, t_end, fwd_ride[0], on_rode)
        pieces1 = got["pieces1"]
    prm1 = _prep_early(pieces1, small, 1)
    prm1.update(_prep_late(pieces1))
    h, sv1 = _layer_fwd(h, prm1, lbs[1:2], rope, t_end)
    dh, lsum = _loss_head(h, tp, t_end)
    loss = jnp.sum(lsum)
    dh, g1, dlb1, _ = _layer_bwd(dh, sv1, prm1, lbs[1:2], rope, t_end)
    dh, g0, dlb0, rode = _layer_bwd(dh, sv0, prm0, lbs[0:1], rope, t_end,
                                    None if bwd_rides is None else bwd_rides(g1))
    dlogits = _lower_bounds_bwd(logits, jnp.stack([dlb0, dlb1]))
    grads = [g0, g1]
    for l in range(DEPTH):
        grads[l]["hgrn_lb_logits"] = dlogits[l]
    return loss, dh[ROW0:t_end], grads, dh[FRONT:ROW0], rode


MESH = pl.DeviceIdType.MESH
_ANY = pl.BlockSpec(memory_space=pl.ANY)
SMALL_ROWS = 64
SMALL_LEN = SMALL_ROWS * 1024


def _mesh_pos():
    return lax.axis_index("x"), lax.axis_index("y"), lax.axis_index("c")


def _other_chips(x, y):
    return [(1 - x, y), (x, 1 - y), (1 - x, 1 - y)]


class _Plan:
    def __init__(self, name, ins, out_shapes, sems, start, finish):
        self.name, self.ins, self.out_shapes, self.sems = name, list(ins), list(out_shapes), list(sems)
        self.start, self.finish = start, finish


def _run_plan(plan):
    ni, no = len(plan.ins), len(plan.out_shapes)

    def body(*refs):
        ins, outs, sems = refs[:ni], refs[ni:ni + no], refs[ni + no:]
        plan.start(ins, outs, sems)
        plan.finish(ins, outs, sems)

    return pl.pallas_call(body, name=plan.name, in_specs=[_ANY] * ni, out_specs=[_ANY] * no,
                          out_shape=plan.out_shapes, scratch_shapes=plan.sems)(*plan.ins)


def _plan_specs(plan):
    if plan is None:
        return [], [], [], [], []
    return plan.ins, [_ANY] * len(plan.ins), [_ANY] * len(plan.out_shapes), plan.out_shapes, plan.sems


def _host_refs(refs, n_in, n_out, n_scratch, plan):
    ni = 0 if plan is None else len(plan.ins)
    no = 0 if plan is None else len(plan.out_shapes)
    o0 = n_in + ni
    s0 = o0 + n_out + no
    own = (refs[:n_in], refs[o0:o0 + n_out], refs[s0:s0 + n_scratch])
    rider = (refs[n_in:o0], refs[o0 + n_out:s0], refs[s0 + n_scratch:])
    return own, rider


def _ride(plan, rider, step, last):
    if plan is None:
        return lambda: None

    @pl.when(step == 0)
    def _():
        plan.start(*rider)

    def done():
        @pl.when(step == last)
        def _():
            plan.finish(*rider)
    return done


def _merge_plans(name, plans):
    def parts(ins, outs, sems):
        i = o = s = 0
        for p in plans:
            ni, no, ns = len(p.ins), len(p.out_shapes), len(p.sems)
            yield p, (ins[i:i + ni], outs[o:o + no], sems[s:s + ns])
            i, o, s = i + ni, o + no, s + ns

    def start(ins, outs, sems):
        for p, refs in parts(ins, outs, sems):
            p.start(*refs)

    def finish(ins, outs, sems):
        for p, refs in parts(ins, outs, sems):
            p.finish(*refs)

    return _Plan(name, [a for p in plans for a in p.ins], [a for p in plans for a in p.out_shapes],
                 [a for p in plans for a in p.sems], start, finish)


def _plan_gather(own, layer, name):
    nw = len(own)

    def copies(ins, outs, sems):
        send_sems, recv_sems = sems

        def over_ici(w, j, chip_of_data, to):
            return pltpu.make_async_remote_copy(
                src_ref=ins[w].at[layer], dst_ref=outs[w].at[chip_of_data], send_sem=send_sems.at[w, j],
                recv_sem=recv_sems.at[w, j], device_id=to, device_id_type=MESH)

        def over_d2d(w, j, chip_of_data, to):
            return pltpu.make_async_remote_copy(
                src_ref=outs[w].at[chip_of_data], dst_ref=outs[w].at[chip_of_data], send_sem=send_sems.at[w, 3 + j],
                recv_sem=recv_sems.at[w, 3 + j], device_id=to, device_id_type=MESH)
        return over_ici, over_d2d

    def start(ins, outs, sems):
        x, y, c = _mesh_pos()
        over_ici, _ = copies(ins, outs, sems)

        @pl.when(c == layer)
        def _():
            for j, (px, py) in enumerate(_other_chips(x, y)):
                for w in range(nw):
                    over_ici(w, j, 2 * x + y, (px, py, layer)).start()

    def finish(ins, outs, sems):
        x, y, c = _mesh_pos()
        over_ici, over_d2d = copies(ins, outs, sems)
        chips = _other_chips(x, y)

        @pl.when(c == layer)
        def _():
            for j, (px, py) in enumerate(chips):
                for w in range(nw):
                    over_ici(w, j, 2 * px + py, (x, y, c)).wait_recv()
                    over_d2d(w, j, 2 * px + py, (x, y, 1 - layer)).start()
            for j, (px, py) in enumerate(chips):
                for w in range(nw):
                    over_ici(w, j, 2 * x + y, (px, py, layer)).wait_send()
                    over_d2d(w, j, 2 * px + py, (x, y, 1 - layer)).wait_send()

        @pl.when(c != layer)
        def _():
            for j, (px, py) in enumerate(chips):
                for w in range(nw):
                    over_d2d(w, j, 2 * px + py, (x, y, c)).wait_recv()

    return _Plan(name, own,
                 [jax.ShapeDtypeStruct((4,) + a.shape[1:], a.dtype) for a in own],
                 [pltpu.SemaphoreType.DMA((nw, 6)), pltpu.SemaphoreType.DMA((nw, 6))], start, finish)


def _plan_to_sibling(arrs, layer, name):
    nw = len(arrs)

    def copy(ins, outs, sems, w):
        x, y, _ = _mesh_pos()
        return pltpu.make_async_remote_copy(src_ref=ins[w], dst_ref=outs[w], send_sem=sems[0].at[w],
                                            recv_sem=sems[1].at[w], device_id=(x, y, layer), device_id_type=MESH)

    def start(ins, outs, sems):
        @pl.when(lax.axis_index("c") != layer)
        def _():
            for w in range(nw):
                copy(ins, outs, sems, w).start()

    def finish(ins, outs, sems):
        c = lax.axis_index("c")

        @pl.when(c != layer)
        def _():
            for w in range(nw):
                copy(ins, outs, sems, w).wait_send()

        @pl.when(c == layer)
        def _():
            for w in range(nw):
                copy(ins, outs, sems, w).wait_recv()

    return _Plan(name, arrs, [jax.ShapeDtypeStruct(a.shape, a.dtype) for a in arrs],
                 [pltpu.SemaphoreType.DMA((nw,)), pltpu.SemaphoreType.DMA((nw,))], start, finish)


def _plan_scatter(parts, layer, name):
    nw = len(parts)

    def start(ins, outs, sems):
        x, y, c = _mesh_pos()

        @pl.when(c == layer)
        def _():
            for j, (px, py) in enumerate(_other_chips(x, y)):
                for w in range(nw):
                    pltpu.make_async_remote_copy(
                        src_ref=ins[w].at[2 * px + py], dst_ref=outs[w].at[2 * x + y], send_sem=sems[0].at[w, j],
                        recv_sem=sems[1].at[w, j], device_id=(px, py, layer), device_id_type=MESH).start()

    def finish(ins, outs, sems):
        x, y, c = _mesh_pos()

        @pl.when(c == layer)
        def _():
            for j, (px, py) in enumerate(_other_chips(x, y)):
                for w in range(nw):
                    pltpu.make_async_remote_copy(
                        src_ref=ins[w].at[2 * px + py], dst_ref=outs[w].at[2 * px + py], send_sem=sems[0].at[w, j],
                        recv_sem=sems[1].at[w, j], device_id=(x, y, c), device_id_type=MESH).wait()

    return _Plan(name, parts, [jax.ShapeDtypeStruct(a.shape, a.dtype) for a in parts],
                 [pltpu.SemaphoreType.DMA((nw, 3)), pltpu.SemaphoreType.DMA((nw, 3))], start, finish)


def _sibling_exchange(reds0, reds1):
    nw = len(reds0)

    def body(*refs):
        a0, a1, outs = refs[:nw], refs[nw:2 * nw], refs[2 * nw:3 * nw]
        send_sems, recv_sems = refs[3 * nw:]
        x, y, c = _mesh_pos()

        def copy(w, src):
            return pltpu.make_async_remote_copy(src_ref=src, dst_ref=outs[w], send_sem=send_sems.at[w],
                                                recv_sem=recv_sems.at[w], device_id=(x, y, 1 - c),
                                                device_id_type=MESH)

        @pl.when(c == 0)
        def _():
            for w in range(nw):
                copy(w, a0[w]).start()

        @pl.when(c == 1)
        def _():
            for w in range(nw):
                copy(w, a1[w]).start()

        for w in range(nw):
            copy(w, a0[w]).wait()

    return pl.pallas_call(
        body, name="sibling_exchange", in_specs=[_ANY] * (2 * nw), out_specs=[_ANY] * nw,
        out_shape=[jax.ShapeDtypeStruct(a.shape, a.dtype) for a in reds0],
        scratch_shapes=[pltpu.SemaphoreType.DMA((nw,)), pltpu.SemaphoreType.DMA((nw,))],
    )(*reds0, *reds1)


def _all_reduce_small(v, name):
    rows, cols = v.shape

    def body(v_ref, o_ref, slots, send_sems, recv_sems):
        x, y, c = _mesh_pos()
        me = 4 * x + 2 * y + c
        slots[me] = v_ref[...]
        peers = []
        for rel in range(1, 8):
            fx, fy, fc = (rel >> 2) & 1, (rel >> 1) & 1, rel & 1
            px = 1 - x if fx else x
            py = 1 - y if fy else y
            pc = 1 - c if fc else c
            peers.append((px, py, pc))
        cps = [pltpu.make_async_remote_copy(src_ref=v_ref, dst_ref=slots.at[me], send_sem=send_sems.at[k],
                                            recv_sem=recv_sems.at[k], device_id=peer, device_id_type=MESH)
               for k, peer in enumerate(peers)]
        for cp in cps:
            cp.start()
        for k, (px, py, pc) in enumerate(peers):
            pltpu.make_async_remote_copy(src_ref=v_ref, dst_ref=slots.at[4 * px + 2 * py + pc],
                                         send_sem=send_sems.at[k], recv_sem=recv_sems.at[k], device_id=(x, y, c),
                                         device_id_type=MESH).wait_recv()
        for cp in cps:
            cp.wait_send()
        acc = slots[0]
        for d in range(1, 8):
            acc = acc + slots[d]
        o_ref[...] = acc

    vm = pl.BlockSpec(memory_space=pltpu.VMEM)
    return pl.pallas_call(
        body, name=name, in_specs=[vm], out_specs=vm,
        out_shape=jax.ShapeDtypeStruct((rows, cols), F32),
        scratch_shapes=[pltpu.VMEM((8, rows, cols), F32), pltpu.SemaphoreType.DMA((7,)),
                        pltpu.SemaphoreType.DMA((7,))],
    )(v)


def _add_to_wire(a, b, name):
    n4, r, c_ = a.shape
    rows = n4 * r
    tr = _ew_tile(rows, c_, 3)

    def body(a_ref, b_ref, o_ref):
        o_ref[...] = (a_ref[...] + b_ref[...]).astype(o_ref.dtype)

    spec = pl.BlockSpec((tr, c_), lambda i: (i, 0))
    out = pl.pallas_call(
        body, name="add_to_wire_" + name, grid=(rows // tr,), in_specs=[spec, spec], out_specs=spec,
        out_shape=jax.ShapeDtypeStruct((rows, c_), jnp.bfloat16), compiler_params=_cp("parallel"),
    )(a.reshape(rows, c_), b.reshape(rows, c_))
    return out.reshape(n4, r, c_)


def _sum_chips(recv, own, name):
    _, r, c_ = recv.shape
    tr = _ew_tile(r, c_, 4)

    def body(r_ref, own_ref, o_ref):
        chip = 2 * lax.axis_index("x") + lax.axis_index("y")
        own_v = own_ref[...].astype(F32)
        acc = None
        for s in range(4):
            term = jnp.where(chip == s, own_v, r_ref[s].astype(F32))
            acc = term if acc is None else acc + term
        o_ref[...] = acc

    return pl.pallas_call(
        body, name="sum_chips_" + name, grid=(r // tr,),
        in_specs=[pl.BlockSpec((4, tr, c_), lambda i: (0, i, 0)), pl.BlockSpec((tr, c_), lambda i: (i, 0))],
        out_specs=pl.BlockSpec((tr, c_), lambda i: (i, 0)),
        out_shape=jax.ShapeDtypeStruct((r, c_), F32),
        compiler_params=_cp("parallel"),
    )(recv, own)


def _pack_small(vals, meta_full, conv_w_full):
    flat = jnp.concatenate([vals[k].reshape(-1) for k in SMALL] + [meta_full.reshape(-1), conv_w_full.reshape(-1)])
    return jnp.pad(flat, (0, SMALL_LEN - flat.shape[0])).reshape(SMALL_ROWS, 1024)


def _unpack_small(buf):
    flat = buf.reshape(-1)
    out, off = {}, 0
    for k in SMALL:
        n = DEPTH * SMALL_SIZES[k]
        out[k] = flat[off:off + n].reshape(DEPTH, SMALL_SIZES[k])
        off += n
    meta = flat[off:off + N_META * D].reshape(N_META, D)
    off += N_META * D
    conv_w = flat[off:off + DEPTH * CONV_K * CONV_DIM].reshape(DEPTH, CONV_K, CONV_DIM)
    return out, meta, conv_w


def kernel(x, meta, norm1_g, w_in, conv_w, conv_b, conv_ln_g, conv_ln_b, w_conv_out, q_a_norm_g, w_uq, kv_a_norm_g, w_ukv, q_norm_g, k_norm_g, w_attn_out, hgrn_lb_logits, hgrn_norm_g, w_hgrn_out, w_out, norm2_g, w_ff1, w_ff2, loss_target, m_meta, m_norm1_g, m_w_in, m_conv_w, m_conv_b, m_conv_ln_g, m_conv_ln_b, m_w_conv_out, m_q_a_norm_g, m_w_uq, m_kv_a_norm_g, m_w_ukv, m_q_norm_g, m_k_norm_g, m_w_attn_out, m_hgrn_lb_logits, m_hgrn_norm_g, m_w_hgrn_out, m_w_out, m_norm2_g, m_w_ff1, m_w_ff2, v_meta, v_norm1_g, v_w_in, v_conv_w, v_conv_b, v_conv_ln_g, v_conv_ln_b, v_w_conv_out, v_q_a_norm_g, v_w_uq, v_kv_a_norm_g, v_w_ukv, v_q_norm_g, v_k_norm_g, v_w_attn_out, v_hgrn_lb_logits, v_hgrn_norm_g, v_w_hgrn_out, v_w_out, v_norm2_g, v_w_ff1, v_w_ff2):
    names = ("meta", "norm1_g", "w_in", "conv_w", "conv_b", "conv_ln_g", "conv_ln_b", "w_conv_out", "q_a_norm_g",
             "w_uq", "kv_a_norm_g", "w_ukv", "q_norm_g", "k_norm_g", "w_attn_out", "hgrn_lb_logits", "hgrn_norm_g",
             "w_hgrn_out", "w_out", "norm2_g", "w_ff1", "w_ff2")
    w = dict(zip(names, (meta, norm1_g, w_in, conv_w, conv_b, conv_ln_g, conv_ln_b, w_conv_out, q_a_norm_g, w_uq,
                         kv_a_norm_g, w_ukv, q_norm_g, k_norm_g, w_attn_out, hgrn_lb_logits, hgrn_norm_g, w_hgrn_out,
                         w_out, norm2_g, w_ff1, w_ff2)))
    m = dict(zip(names, (m_meta, m_norm1_g, m_w_in, m_conv_w, m_conv_b, m_conv_ln_g, m_conv_ln_b, m_w_conv_out,
                         m_q_a_norm_g, m_w_uq, m_kv_a_norm_g, m_w_ukv, m_q_norm_g, m_k_norm_g, m_w_attn_out,
                         m_hgrn_lb_logits, m_hgrn_norm_g, m_w_hgrn_out, m_w_out, m_norm2_g, m_w_ff1, m_w_ff2)))
    v = dict(zip(names, (v_meta, v_norm1_g, v_w_in, v_conv_w, v_conv_b, v_conv_ln_g, v_conv_ln_b, v_w_conv_out,
                         v_q_a_norm_g, v_w_uq, v_kv_a_norm_g, v_w_ukv, v_q_norm_g, v_k_norm_g, v_w_attn_out,
                         v_hgrn_lb_logits, v_hgrn_norm_g, v_w_hgrn_out, v_w_out, v_norm2_g, v_w_ff1, v_w_ff2)))
    cx, cy, cc = _mesh_pos()
    chip = 2 * cx + cy
    zero = jnp.zeros((), jnp.int32)

    own = {k: w[k].astype(_MM) for k in BIG}

    def as_pieces(names, gathered, layer):
        return {k: [jnp.where(chip == s, own[k][layer], g[s]) for s in range(4)] for k, g in zip(names, gathered)}

    pieces0 = as_pieces(EARLY, _run_plan(_plan_gather([own[k] for k in EARLY], 0, "gather_l0_early")), 0)
    fwd_ride = (_merge_plans("gather_rest", [_plan_gather([own[k] for k in LATE], 0, "gather_l0_late"),
                                             _plan_gather([own[k] for k in BIG], 1, "gather_l1")]),
                lambda got: (as_pieces(LATE, got[:len(LATE)], 0), as_pieces(BIG, got[len(LATE):], 1)))
    meta_slab = lax.dynamic_update_slice(jnp.zeros((N_META, D), F32), meta, (zero, chip * (D // 4)))
    convw_slab = lax.dynamic_update_slice(jnp.zeros((DEPTH, CONV_K, CONV_DIM), F32), conv_w,
                                          (zero, zero, chip * (CONV_DIM // 4)))
    zsmall = {k: jnp.zeros((DEPTH, SMALL_SIZES[k]), F32) for k in SMALL}
    south = (cc == 0).astype(F32)
    _, meta_full, convw_full = _unpack_small(
        _all_reduce_small(_pack_small(zsmall, meta_slab, convw_slab) * south, "gather_small"))
    small = {k: w[k] for k in SMALL}
    small["meta"] = meta_full
    small["conv_w"] = convw_full

    FFN = ("w_ff1", "w_ff2")
    REST = tuple(k for k in BIG if k not in FFN)
    held = {}

    def to_wire(names, layer, mine, from_sibling):
        return [_add_to_wire(a, b, "%s_l%d" % (k, layer)) for k, a, b in zip(names, mine, from_sibling)]

    def chip_sum(names, layer, got, wire):
        return [_sum_chips(r, lax.dynamic_index_in_dim(s, chip, 0, keepdims=False), "%s_l%d" % (k, layer))
                for k, r, s in zip(names, got, wire)]

    def bwd_rides(g1):
        held["g1"] = [g1[k] for k in BIG]

        def ride_mix(from_sibling1, g0_ffn):
            held["wire1"] = to_wire(BIG, 1, held["g1"], from_sibling1)
            held["g0_ffn"] = [g0_ffn[k] for k in FFN]
            return _plan_to_sibling(held["g0_ffn"], 0, "swap_grads_l0_ffn")

        def ride_attn(from_sibling0):
            held["wire0_ffn"] = to_wire(FFN, 0, held["g0_ffn"], from_sibling0)
            return _merge_plans("scatter_grads_early", [_plan_scatter(held["wire1"], 1, "scatter_grads_l1"),
                                                        _plan_scatter(held["wire0_ffn"], 0, "scatter_grads_l0_ffn")])

        return {"ffn": _plan_to_sibling(held["g1"], 1, "swap_grads_l1"), "mix": ride_mix, "attn": ride_attn}

    loss_share, grad_x, gl, g_meta, got = _device_step(x[0], loss_target[0], small, pieces0, None, fwd_ride,
                                                       bwd_rides)
    loss = lax.psum(loss_share, ("x", "y", "c"))

    reds1 = chip_sum(BIG, 1, got[:len(BIG)], held["wire1"])
    reds0 = dict(zip(FFN, chip_sum(FFN, 0, got[len(BIG):], held["wire0_ffn"])))
    g0_rest = [gl[0][k] for k in REST]
    wire0 = to_wire(REST, 0, g0_rest, _run_plan(_plan_to_sibling(g0_rest, 0, "swap_grads_l0_rest")))
    reds0.update(zip(REST, chip_sum(REST, 0, _run_plan(_plan_scatter(wire0, 0, "scatter_grads_l0_rest")), wire0)))
    reds0 = [reds0[k] for k in BIG]
    reds_sibling = _sibling_exchange(reds0, reds1)
    grads, delta, new_m, new_v = {}, {}, {}, {}
    for k, r0, r1, theirs in zip(BIG, reds0, reds1, reds_sibling):
        grads[k], delta[k], new_m[k], new_v[k] = _adamw_layers(w[k], m[k], v[k], r0, r1, theirs, k)

    g_small_local = {k: jnp.stack([gl[l][k] for l in range(DEPTH)]) for k in SMALL}
    g_convw_local = jnp.stack([gl[l]["conv_w"] for l in range(DEPTH)])
    g_small, g_meta_full, g_convw_full = _unpack_small(
        _all_reduce_small(_pack_small(g_small_local, g_meta, g_convw_local), "reduce_small"))
    grads.update(g_small)
    grads["meta"] = lax.dynamic_slice(g_meta_full, (zero, chip * (D // 4)), (N_META, D // 4))
    grads["conv_w"] = lax.dynamic_slice(g_convw_full, (zero, zero, chip * (CONV_DIM // 4)),
                                        (DEPTH, CONV_K, CONV_DIM // 4))

    def small_pack(src):
        return _pack_small(src, jnp.pad(src["meta"], ((0, 0), (0, D - D // 4))),
                           jnp.pad(src["conv_w"], ((0, 0), (0, 0), (0, CONV_DIM - CONV_DIM // 4))))

    def small_unpack(buf):
        out, meta_p, convw_p = _unpack_small(buf)
        out["meta"] = meta_p[:, :D // 4]
        out["conv_w"] = convw_p[:, :, :CONV_DIM // 4]
        return out

    d_s, m_s, v_s = [small_unpack(a) for a in _adamw(small_pack(w), small_pack(grads), small_pack(m),
                                                     small_pack(v), "small")]
    delta.update(d_s)
    new_m.update(m_s)
    new_v.update(v_s)
    return (loss, grad_x[None], *[grads[k] for k in names], *[delta[k] for k in names],
            *[new_m[k] for k in names], *[new_v[k] for k in names])
```

```python
import functools

import jax
import jax.numpy as jnp
from jax import lax
from jax.experimental import pallas as pl
from jax.experimental.pallas import tpu as pltpu

F32 = jnp.float32
_MM = jnp.bfloat16

D = 1024
N_META = 16
FRONT = 48
ROW0 = FRONT + N_META
EPS = 1e-6
GATE_CLAMP = 1.0 - 1e-6
CONV_K = 31
CONV_DIM = 512
NH = 8
QK_DIM = 96
ATT_SCALE = QK_DIM ** -0.5
HH = 4
CHUNK = 64
SUB = 16
EXP_CLIP = 60.0
NEG = -1e30
LANE = 128

SEG_GATES = (0, 3072)
SEG_AG = (3072, 4096)
SEG_H4 = (4096, 6144)
SEG_CQ = (6144, 6400)
SEG_CKV = (6400, 6528)
SEG_KR = (6528, 6656)
N_IN_P = 6656

ADAM_LR = 0.001
ADAM_B1 = 0.9
ADAM_B2 = 0.999
ADAM_EPS = 1e-08
ADAM_WD = 0.01
ADAM_STEP = 10

VMEM_LIMIT = 56 * 1024 * 1024


def _tile(n, pref):
    best = 64
    for t in range(64, pref + 1, 64):
        if n % t == 0:
            best = t
    return best


def _cp(*sem):
    return pltpu.CompilerParams(dimension_semantics=tuple(sem), vmem_limit_bytes=VMEM_LIMIT)


def _row(tm, n, col=0):
    return pl.BlockSpec((tm, n), lambda i: (i, col))


def _full(shape):
    return pl.BlockSpec(shape, lambda i: (0,) * len(shape))


def _mm(a, b):
    return jnp.dot(a.astype(_MM), b.astype(_MM), preferred_element_type=F32)


def _mm_nt(a, b):
    return lax.dot_general(a.astype(_MM), b.astype(_MM), (((1,), (1,)), ((), ())), preferred_element_type=F32)


def _mm_tn(a, b):
    return lax.dot_general(a.astype(_MM), b.astype(_MM), (((0,), (0,)), ((), ())), preferred_element_type=F32)


def _split3(x):
    hi = x.astype(jnp.bfloat16)
    return hi, (x - hi.astype(F32)).astype(jnp.bfloat16)


def _dot3(a, b, dims):
    ah, al = _split3(a)
    bh, bl = _split3(b)
    dg = lambda u, v: lax.dot_general(u, v, (dims, ((), ())), preferred_element_type=F32)
    return dg(ah, bh) + (dg(ah, bl) + dg(al, bh))


def _hmm(a, b):
    return _dot3(a, b, ((1,), (0,)))


def _hmm_nt(a, b):
    return _dot3(a, b, ((1,), (1,)))


def _hmm_tn(a, b):
    return _dot3(a, b, ((0,), (0,)))


def _sigmoid(x):
    return 1.0 / (1.0 + jnp.exp(-x))


def _rstd(x, n=None):
    n = x.shape[-1] if n is None else n
    return lax.rsqrt(jnp.sum(x * x, axis=-1, keepdims=True) * (1.0 / n) + EPS)


def _rms_bwd(dy, x, rstd, g, n=None):
    n = x.shape[-1] if n is None else n
    xh = x * rstd
    dxh = dy * g
    dx = rstd * (dxh - xh * (jnp.sum(dxh * xh, axis=-1, keepdims=True) * (1.0 / n)))
    return dx, dy * xh


def _valid_rows(i, tm, t_valid_end):
    r = i * tm + lax.broadcasted_iota(jnp.int32, (tm, 1), 0)
    return ((r >= FRONT) & (r < t_valid_end)).astype(F32)


def _colsum8(x):
    n, c = x.shape
    return jnp.sum(x.reshape(n // 8, 8, c), axis=0)


def _in_proj_fwd(x, g1, w):
    t = x.shape[0]
    tm = _tile(t, 192)
    segs = (SEG_GATES, SEG_AG, SEG_H4, SEG_CQ, SEG_CKV, SEG_KR)

    def body(x_ref, g_ref, w_ref, gates_ref, ag_ref, h4_ref, cq_ref, ckv_ref, kr_ref, hb_ref):
        xv = x_ref[...]
        hb = (xv * _rstd(xv) * g_ref[...]).astype(_MM)
        hb_ref[...] = hb
        for ref, (a, b) in zip((gates_ref, ag_ref, h4_ref, cq_ref, ckv_ref, kr_ref), segs):
            ref[...] = jnp.dot(hb, w_ref[:, a:b], preferred_element_type=F32)

    outs = [jax.ShapeDtypeStruct((t, b - a), F32) for a, b in segs] + [jax.ShapeDtypeStruct((t, D), _MM)]
    return pl.pallas_call(
        body, name="in_proj_fwd", grid=(t // tm,),
        in_specs=[_row(tm, D), _full((1, D)), _full((D, N_IN_P))],
        out_specs=[_row(tm, b - a) for a, b in segs] + [_row(tm, D)],
        out_shape=outs, compiler_params=_cp("parallel"),
    )(x, g1, w)


def _in_proj_bwd(du, x, dx1, g1, wt, t_end):
    t = x.shape[0]
    tm = _tile(t, 192)

    def body(du_ref, x_ref, dx1_ref, g_ref, wt_ref, dx_ref, dg_ref):
        i = pl.program_id(0)
        dh = jnp.dot(du_ref[...], wt_ref[...], preferred_element_type=F32)
        xv = x_ref[...]
        dxn, dgrow = _rms_bwd(dh, xv, _rstd(xv), g_ref[...])
        dx_ref[...] = _valid_rows(i, tm, t_end) * (dx1_ref[...] + dxn)

        @pl.when(i == 0)
        def _():
            dg_ref[...] = jnp.zeros_like(dg_ref)
        dg_ref[...] += _colsum8(dgrow)

    return pl.pallas_call(
        body, name="in_proj_bwd", grid=(t // tm,),
        in_specs=[_row(tm, N_IN_P), _row(tm, D), _row(tm, D), _full((1, D)), _full((N_IN_P, D))],
        out_specs=[_row(tm, D), _full((8, D))],
        out_shape=[jax.ShapeDtypeStruct((t, D), F32), jax.ShapeDtypeStruct((8, D), F32)],
        compiler_params=_cp("arbitrary"),
    )(du, x, dx1, g1, wt)


CONV_CH = 128


def _conv_fwd(ag, cw, cb):
    t = ag.shape[0]
    n = t // CONV_CH

    def body(a_ref, g_ref, w_ref, b_ref, z_ref, hp):
        hp[0:32, :] = jnp.zeros((32, LANE), F32)

        def fill(i, c):
            r = pl.multiple_of(i * CONV_CH, CONV_CH)
            hp[pl.ds(32 + r, CONV_CH), :] = a_ref[pl.ds(r, CONV_CH), :] * _sigmoid(g_ref[pl.ds(r, CONV_CH), :])
            return c
        lax.fori_loop(0, n, fill, 0)

        def conv(i, c):
            r = pl.multiple_of(i * CONV_CH, CONV_CH)
            acc = jnp.broadcast_to(b_ref[...], (CONV_CH, LANE))
            for k in range(CONV_K):
                acc = acc + w_ref[k:k + 1, :] * hp[pl.ds(r + (k + 2), CONV_CH), :]
            z_ref[pl.ds(r, CONV_CH), :] = acc
            return c
        lax.fori_loop(0, n, conv, 0)

    nb = CONV_DIM // LANE
    return pl.pallas_call(
        body, name="conv_fwd", grid=(nb,),
        in_specs=[pl.BlockSpec((t, LANE), lambda j: (0, j)), pl.BlockSpec((t, LANE), lambda j: (0, nb + j)),
                  pl.BlockSpec((32, LANE), lambda j: (0, j)), pl.BlockSpec((1, LANE), lambda j: (0, j))],
        out_specs=pl.BlockSpec((t, LANE), lambda j: (0, j)),
        out_shape=jax.ShapeDtypeStruct((t, CONV_DIM), F32),
        scratch_shapes=[pltpu.VMEM((t + 32, LANE), F32)],
        compiler_params=_cp("parallel"),
    )(ag, ag, cw, cb)


def _conv_bwd(ag, cw, dz):
    t = ag.shape[0]
    n = t // CONV_CH

    def body(a_ref, g_ref, w_ref, dz_ref, da_ref, dg_ref, dcw_ref, hp, dzp, accw):
        hp[0:32, :] = jnp.zeros((32, LANE), F32)
        dzp[pl.ds(t, 32), :] = jnp.zeros((32, LANE), F32)
        accw[...] = jnp.zeros_like(accw)

        def fill(i, c):
            r = pl.multiple_of(i * CONV_CH, CONV_CH)
            hp[pl.ds(32 + r, CONV_CH), :] = a_ref[pl.ds(r, CONV_CH), :] * _sigmoid(g_ref[pl.ds(r, CONV_CH), :])
            dzp[pl.ds(r, CONV_CH), :] = dz_ref[pl.ds(r, CONV_CH), :]
            return c
        lax.fori_loop(0, n, fill, 0)

        def step(i, c):
            r = pl.multiple_of(i * CONV_CH, CONV_CH)
            dzc = dz_ref[pl.ds(r, CONV_CH), :]
            dh = jnp.zeros((CONV_CH, LANE), F32)
            for k in range(CONV_K):
                dh = dh + w_ref[k:k + 1, :] * dzp[pl.ds(r + (CONV_K - 1 - k), CONV_CH), :]
                accw[8 * k:8 * k + 8, :] += _colsum8(dzc * hp[pl.ds(r + (k + 2), CONV_CH), :])
            a = a_ref[pl.ds(r, CONV_CH), :]
            sg = _sigmoid(g_ref[pl.ds(r, CONV_CH), :])
            da_ref[pl.ds(r, CONV_CH), :] = dh * sg
            dg_ref[pl.ds(r, CONV_CH), :] = dh * a * sg * (1.0 - sg)
            return c
        lax.fori_loop(0, n, step, 0)

        for k in range(CONV_K):
            dcw_ref[k:k + 1, :] = jnp.sum(accw[8 * k:8 * k + 8, :], axis=0, keepdims=True)
        dcw_ref[CONV_K:32, :] = jnp.zeros((32 - CONV_K, LANE), F32)

    nb = CONV_DIM // LANE
    colspec = pl.BlockSpec((t, LANE), lambda j: (0, j))
    return pl.pallas_call(
        body, name="conv_bwd", grid=(nb,),
        in_specs=[colspec, pl.BlockSpec((t, LANE), lambda j: (0, nb + j)),
                  pl.BlockSpec((32, LANE), lambda j: (0, j)), colspec],
        out_specs=[colspec, colspec, pl.BlockSpec((32, LANE), lambda j: (0, j))],
        out_shape=[jax.ShapeDtypeStruct((t, CONV_DIM), F32), jax.ShapeDtypeStruct((t, CONV_DIM), F32),
                   jax.ShapeDtypeStruct((32, CONV_DIM), F32)],
        scratch_shapes=[pltpu.VMEM((t + 32, LANE), F32), pltpu.VMEM((t + 32, LANE), F32),
                        pltpu.VMEM((8 * 32, LANE), F32)],
        compiler_params=_cp("parallel"),
    )(ag, ag, cw, dz)


def _rope(x, c, s1, s2):
    return x * c + pltpu.roll(x, LANE - 16, 1) * s1 + pltpu.roll(x, 16, 1) * s2


def _rope_t(dy, c, s1, s2):
    return dy * c + pltpu.roll(dy * s1, 16, 1) + pltpu.roll(dy * s2, LANE - 16, 1)


def _mla_pre_fwd(cq, ckv, kr, qag, wuq, kvag, wk, wv, qng, kng, rc, rs1, rs2):
    t = cq.shape[0]
    tm = _tile(t, 384)

    def body(cq_ref, ckv_ref, kr_ref, qag_ref, wuq_ref, kvag_ref, wk_ref, wv_ref, qng_ref, kng_ref,
             c_ref, s1_ref, s2_ref, q_ref, k_ref, v_ref, cqn_ref, ckvn_ref):
        cqv = cq_ref[...]
        cqn = (cqv * _rstd(cqv) * qag_ref[...]).astype(_MM)
        cqn_ref[...] = cqn
        ckvv = ckv_ref[...]
        ckvn = (ckvv * _rstd(ckvv) * kvag_ref[...]).astype(_MM)
        ckvn_ref[...] = ckvn
        qraw = jnp.dot(cqn, wuq_ref[...], preferred_element_type=F32)
        kraw = jnp.dot(ckvn, wk_ref[...], preferred_element_type=F32)
        v_ref[...] = jnp.dot(ckvn, wv_ref[...], preferred_element_type=F32).astype(_MM)
        krv = kr_ref[...]
        c, s1, s2 = c_ref[...], s1_ref[...], s2_ref[...]
        for h in range(NH):
            sl = slice(LANE * h, LANE * (h + 1))
            qh = qraw[:, sl]
            qn = qh * _rstd(qh, QK_DIM) * qng_ref[...]
            q_ref[:, sl] = (_rope(qn, c, s1, s2) * ATT_SCALE).astype(_MM)
            kh = kraw[:, sl] + krv
            kn = kh * _rstd(kh, QK_DIM) * kng_ref[...]
            k_ref[:, sl] = _rope(kn, c, s1, s2).astype(_MM)

    hd = NH * LANE
    return pl.pallas_call(
        body, name="mla_pre_fwd", grid=(t // tm,),
        in_specs=[_row(tm, 256), _row(tm, 128), _row(tm, 128), _full((1, 256)), _full((256, hd)),
                  _full((1, 128)), _full((128, hd)), _full((128, hd)), _full((1, LANE)), _full((1, LANE)),
                  _row(tm, LANE), _row(tm, LANE), _row(tm, LANE)],
        out_specs=[_row(tm, hd), _row(tm, hd), _row(tm, hd), _row(tm, 256), _row(tm, 128)],
        out_shape=[jax.ShapeDtypeStruct((t, hd), _MM)] * 3 + [jax.ShapeDtypeStruct((t, 256), _MM),
                                                              jax.ShapeDtypeStruct((t, 128), _MM)],
        compiler_params=_cp("parallel"),
    )(cq, ckv, kr, qag, wuq, kvag, wk, wv, qng, kng, rc, rs1, rs2)


def _mla_pre_bwd(dq, dk, dv, cq, ckv, kr, qag, wuq, kvag, wk, wv, qng, kng, rc, rs1, rs2):
    t = cq.shape[0]
    tm = _tile(t, 192)
    hd = NH * LANE

    def body(dq_ref, dk_ref, dv_ref, cq_ref, ckv_ref, kr_ref, qag_ref, wuq_ref, kvag_ref, wk_ref,
             wv_ref, qng_ref, kng_ref, c_ref, s1_ref, s2_ref,
             dcq_ref, dckv_ref, dkr_ref, dqraw_ref, dkraw_ref, dqag_ref, dkvag_ref, dqng_ref, dkng_ref):
        i = pl.program_id(0)
        cqv = cq_ref[...]
        rq_in = _rstd(cqv)
        cqn = (cqv * rq_in * qag_ref[...]).astype(_MM)
        ckvv = ckv_ref[...]
        rkv_in = _rstd(ckvv)
        ckvn = (ckvv * rkv_in * kvag_ref[...]).astype(_MM)
        qraw = jnp.dot(cqn, wuq_ref[...], preferred_element_type=F32)
        kraw = jnp.dot(ckvn, wk_ref[...], preferred_element_type=F32)
        krv = kr_ref[...]
        c, s1, s2 = c_ref[...], s1_ref[...], s2_ref[...]
        dkr = jnp.zeros((tm, LANE), F32)
        dqng = jnp.zeros((8, LANE), F32)
        dkng = jnp.zeros((8, LANE), F32)
        for h in range(NH):
            sl = slice(LANE * h, LANE * (h + 1))
            qh = qraw[:, sl]
            dqn = _rope_t(dq_ref[:, sl] * ATT_SCALE, c, s1, s2)
            dqh, gq = _rms_bwd(dqn, qh, _rstd(qh, QK_DIM), qng_ref[...], QK_DIM)
            dqraw_ref[:, sl] = dqh.astype(_MM)
            dqng = dqng + _colsum8(gq)
            kh = kraw[:, sl] + krv
            dkn = _rope_t(dk_ref[:, sl], c, s1, s2)
            dkh, gk = _rms_bwd(dkn, kh, _rstd(kh, QK_DIM), kng_ref[...], QK_DIM)
            dkraw_ref[:, sl] = dkh.astype(_MM)
            dkr = dkr + dkh
            dkng = dkng + _colsum8(gk)
        dkr_ref[...] = dkr.astype(_MM)
        dcqn = _mm_nt(dqraw_ref[...], wuq_ref[...])
        dcq, gqa = _rms_bwd(dcqn, cqv, rq_in, qag_ref[...])
        dcq_ref[...] = dcq.astype(_MM)
        dckvn = _mm_nt(dkraw_ref[...], wk_ref[...]) + _mm_nt(dv_ref[...], wv_ref[...])
        dckv, gkva = _rms_bwd(dckvn, ckvv, rkv_in, kvag_ref[...])
        dckv_ref[...] = dckv.astype(_MM)

        @pl.when(i == 0)
        def _():
            dqag_ref[...] = jnp.zeros_like(dqag_ref)
            dkvag_ref[...] = jnp.zeros_like(dkvag_ref)
            dqng_ref[...] = jnp.zeros_like(dqng_ref)
            dkng_ref[...] = jnp.zeros_like(dkng_ref)
        dqag_ref[...] += _colsum8(gqa)
        dkvag_ref[...] += _colsum8(gkva)
        dqng_ref[...] += dqng
        dkng_ref[...] += dkng

    return pl.pallas_call(
        body, name="mla_pre_bwd", grid=(t // tm,),
        in_specs=[_row(tm, hd), _row(tm, hd), _row(tm, hd), _row(tm, 256), _row(tm, 128), _row(tm, 128),
                  _full((1, 256)), _full((256, hd)), _full((1, 128)), _full((128, hd)),
                  _full((128, hd)), _full((1, LANE)), _full((1, LANE)),
                  _row(tm, LANE), _row(tm, LANE), _row(tm, LANE)],
        out_specs=[_row(tm, 256), _row(tm, 128), _row(tm, 128), _row(tm, hd), _row(tm, hd),
                   _full((8, 256)), _full((8, 128)), _full((8, LANE)), _full((8, LANE))],
        out_shape=[jax.ShapeDtypeStruct((t, 256), _MM), jax.ShapeDtypeStruct((t, 128), _MM),
                   jax.ShapeDtypeStruct((t, 128), _MM), jax.ShapeDtypeStruct((t, hd), _MM),
                   jax.ShapeDtypeStruct((t, hd), _MM), jax.ShapeDtypeStruct((8, 256), F32),
                   jax.ShapeDtypeStruct((8, 128), F32), jax.ShapeDtypeStruct((8, LANE), F32),
                   jax.ShapeDtypeStruct((8, LANE), F32)],
        compiler_params=_cp("arbitrary"),
    )(dq, dk, dv, cq, ckv, kr, qag, wuq, kvag, wk, wv, qng, kng, rc, rs1, rs2)


def _attn_mask(r0, c0, tq):
    rows = r0 + lax.broadcasted_iota(jnp.int32, (tq, 1), 0)
    cols = c0 + lax.broadcasted_iota(jnp.int32, (1, tq), 1)
    return (cols <= rows) & (cols >= FRONT)


def _attn_fwd(q, k, v, plan=None):
    t = q.shape[0]
    tq = _tile(t, 384)
    nq = t // tq
    p_args, p_in, p_out, p_shape, p_sem = _plan_specs(plan)

    def body(*refs):
        ((q_ref, k_ref, v_ref), (o_ref, lse_ref), _), rider = _host_refs(refs, 3, 2, 0, plan)
        done = _ride(plan, rider, pl.program_id(0), NH - 1)

        def qloop(qi, carry):
            r0 = pl.multiple_of(qi * tq, tq)
            qb = q_ref[pl.ds(r0, tq), :]

            def kstep(kj, st, masked):
                m, l, acc = st
                c0 = pl.multiple_of(kj * tq, tq)
                s = _mm_nt(qb, k_ref[pl.ds(c0, tq), :])
                if masked:
                    s = jnp.where(_attn_mask(r0, c0, tq), s, NEG)
                m2 = jnp.maximum(m, jnp.max(s, axis=-1, keepdims=True))
                p = jnp.exp(s - m2)
                a = jnp.exp(m - m2)
                l = a * l + jnp.sum(p, axis=-1, keepdims=True)
                acc = a * acc + _mm(p, v_ref[pl.ds(c0, tq), :])
                return m2, l, acc

            st = kstep(0, (jnp.full((tq, 1), NEG, F32), jnp.zeros((tq, 1), F32), jnp.zeros((tq, LANE), F32)), True)
            st = lax.fori_loop(1, qi, lambda kj, s_: kstep(kj, s_, False), st)
            m, l, acc = lax.cond(qi > 0, lambda s_: kstep(qi, s_, True), lambda s_: s_, st)
            o_ref[pl.ds(r0, tq), :] = acc / l
            lse_ref[pl.ds(r0, tq), :] = m + jnp.log(l)
            return carry
        lax.fori_loop(0, nq, qloop, 0)
        done()

    hs = pl.BlockSpec((t, LANE), lambda h: (0, h))
    res = pl.pallas_call(
        body, name="attn_fwd", grid=(NH,),
        in_specs=[hs, hs, hs] + p_in,
        out_specs=[hs, pl.BlockSpec((None, t, 1), lambda h: (h, 0, 0))] + p_out,
        out_shape=[jax.ShapeDtypeStruct((t, NH * LANE), F32), jax.ShapeDtypeStruct((NH, t, 1), F32)] + p_shape,
        scratch_shapes=p_sem,
        compiler_params=_cp("parallel" if plan is None else "arbitrary"),
    )(q, k, v, *p_args)
    return res[:2], res[2:]


def _attn_bwd(q, k, v, o, lse, do, plan=None):
    t = q.shape[0]
    tq = _tile(t, 384)
    nq = t // tq
    p_args, p_in, p_out, p_shape, p_sem = _plan_specs(plan)

    def body(*refs):
        (ins, (dq_ref, dk_ref, dv_ref), (delta,)), rider = _host_refs(refs, 6, 3, 1, plan)
        q_ref, k_ref, v_ref, o_ref, lse_ref, do_ref = ins
        done = _ride(plan, rider, pl.program_id(0), NH - 1)

        def prep(i, c):
            r0 = pl.multiple_of(i * tq, tq)
            delta[pl.ds(r0, tq), :] = jnp.sum(do_ref[pl.ds(r0, tq), :] * o_ref[pl.ds(r0, tq), :], axis=-1,
                                              keepdims=True)
            dq_ref[pl.ds(r0, tq), :] = jnp.zeros((tq, LANE), F32)
            return c
        lax.fori_loop(0, nq, prep, 0)

        def kloop(kj, carry):
            c0 = pl.multiple_of(kj * tq, tq)
            kb = k_ref[pl.ds(c0, tq), :]
            vb = v_ref[pl.ds(c0, tq), :]

            def qstep(qi, st, masked):
                dkb, dvb = st
                r0 = pl.multiple_of(qi * tq, tq)
                qb = q_ref[pl.ds(r0, tq), :]
                dob = do_ref[pl.ds(r0, tq), :].astype(_MM)
                s = _mm_nt(qb, kb)
                if masked:
                    s = jnp.where(_attn_mask(r0, c0, tq), s, NEG)
                p = jnp.exp(s - lse_ref[pl.ds(r0, tq), :])
                dvb = dvb + _mm_tn(p, dob)
                dp = _mm_nt(dob, vb)
                ds = (p * (dp - delta[pl.ds(r0, tq), :])).astype(_MM)
                dkb = dkb + _mm_tn(ds, qb)
                dq_ref[pl.ds(r0, tq), :] += _mm(ds, kb)
                return dkb, dvb

            st = qstep(kj, (jnp.zeros((tq, LANE), F32), jnp.zeros((tq, LANE), F32)), True)
            dkb, dvb = lax.cond(
                kj == 0,
                lambda s_: lax.fori_loop(kj + 1, nq, lambda qi, t_: qstep(qi, t_, True), s_),
                lambda s_: lax.fori_loop(kj + 1, nq, lambda qi, t_: qstep(qi, t_, False), s_), st)
            dk_ref[pl.ds(c0, tq), :] = dkb
            dv_ref[pl.ds(c0, tq), :] = dvb
            return carry
        lax.fori_loop(0, nq, kloop, 0)
        done()

    hs = pl.BlockSpec((t, LANE), lambda h: (0, h))
    res = pl.pallas_call(
        body, name="attn_bwd", grid=(NH,),
        in_specs=[hs, hs, hs, hs, pl.BlockSpec((None, t, 1), lambda h: (h, 0, 0)), hs] + p_in,
        out_specs=[hs, hs, hs] + p_out,
        out_shape=[jax.ShapeDtypeStruct((t, NH * LANE), F32)] * 3 + p_shape,
        scratch_shapes=[pltpu.VMEM((t, 1), F32)] + p_sem,
        compiler_params=_cp("parallel" if plan is None else "arbitrary"),
    )(q, k, v, o, lse, do, *p_args)
    return res[:3], res[3:]


def _cumsum_rows(x):
    n = x.shape[0]
    rows = lax.broadcasted_iota(jnp.int32, (n, 1), 0)
    d = 1
    while d < n:
        x = x + jnp.where(rows >= d, pltpu.roll(x, d, 0), 0.0)
        d *= 2
    return x


def _revcumsum_rows(x):
    n = x.shape[0]
    rows = lax.broadcasted_iota(jnp.int32, (n, 1), 0)
    d = 1
    while d < n:
        x = x + jnp.where(rows < n - d, pltpu.roll(x, n - d, 0), 0.0)
        d *= 2
    return x


def _hgrn_gates(f, lb):
    sneg = _sigmoid(-f)
    kk = (1.0 - lb) * sneg
    lf = jnp.log1p(-jnp.minimum(kk, GATE_CLAMP))
    return kk, lf, sneg


def _silu(x):
    return x * _sigmoid(x)


def _dsilu(x):
    s = _sigmoid(x)
    return s * (1.0 + x * (1.0 - s))


def _hgrn_intra(q, kk, b):
    parts = []
    for blk in range(CHUNK // SUB):
        lo = blk * SUB
        ref = jnp.zeros((1, LANE), F32) if blk == 0 else b[lo - 1:lo, :]
        eq = jnp.exp(b[lo:lo + SUB, :] - ref)
        ek = jnp.exp(jnp.minimum(ref - b, EXP_CLIP))
        parts.append((q[lo:lo + SUB, :] * eq, kk * ek, eq, ek))
    return parts


def _chunk_causal():
    return lax.broadcasted_iota(jnp.int32, (CHUNK, CHUNK), 1) <= lax.broadcasted_iota(jnp.int32, (CHUNK, CHUNK), 0)


def _hgrn_fwd(h4, lb):
    t = h4.shape[0]
    nc = t // CHUNK

    def body(q_ref, f_ref, i_ref, lb_ref, o_ref, s_ref, st):
        st[...] = jnp.zeros_like(st)
        causal = _chunk_causal()

        def chunk(c, carry):
            r0 = pl.multiple_of(c * CHUNK, CHUNK)
            q = q_ref[pl.ds(r0, CHUNK), :]
            kk, lf, _ = _hgrn_gates(f_ref[pl.ds(r0, CHUNK), :], lb_ref[...])
            v = _silu(i_ref[pl.ds(r0, CHUNK), :])
            b = _cumsum_rows(lf)
            s_prev = st[...]
            s_ref[c] = s_prev
            o = _hmm_nt(q * jnp.exp(b), s_prev)
            a = jnp.concatenate([_hmm_nt(qs, ks) for qs, ks, _, _ in _hgrn_intra(q, kk, b)], axis=0)
            a = jnp.where(causal, a, 0.0)
            o_ref[pl.ds(r0, CHUNK), :] = o + _hmm(a, v)
            bl = b[CHUNK - 1:CHUNK, :]
            st[...] = s_prev * jnp.exp(bl) + _hmm_tn(v, kk * jnp.exp(bl - b))
            return carry
        lax.fori_loop(0, nc, chunk, 0, unroll=2)

    def col(j):
        return pl.BlockSpec((t, LANE), lambda h: (0, HH * j + h))
    return pl.pallas_call(
        body, name="hgrn_fwd", grid=(HH,),
        in_specs=[col(0), col(1), col(2), pl.BlockSpec((1, LANE), lambda h: (0, h))],
        out_specs=[pl.BlockSpec((t, LANE), lambda h: (0, h)),
                   pl.BlockSpec((None, nc, LANE, LANE), lambda h: (h, 0, 0, 0))],
        out_shape=[jax.ShapeDtypeStruct((t, HH * LANE), F32), jax.ShapeDtypeStruct((HH, nc, LANE, LANE), F32)],
        scratch_shapes=[pltpu.VMEM((LANE, LANE), F32)],
        compiler_params=_cp("parallel"),
    )(h4, h4, h4, lb)


def _hgrn_bwd(h4, lb, do, states):
    t = h4.shape[0]
    nc = t // CHUNK

    def body(q_ref, f_ref, i_ref, lb_ref, do_ref, s_ref, dq_ref, df_ref, di_ref, dlb_ref, dst, carry):
        dst[...] = jnp.zeros_like(dst)
        carry[...] = jnp.zeros_like(carry)
        dlb_ref[...] = jnp.zeros_like(dlb_ref)
        causal = _chunk_causal()

        def chunk(cc, cr):
            c = nc - 1 - cc
            r0 = pl.multiple_of(c * CHUNK, CHUNK)
            q = q_ref[pl.ds(r0, CHUNK), :]
            lbv = lb_ref[...]
            kk, lf, sneg = _hgrn_gates(f_ref[pl.ds(r0, CHUNK), :], lbv)
            iv = i_ref[pl.ds(r0, CHUNK), :]
            v = _silu(iv)
            b = _cumsum_rows(lf)
            s_prev = s_ref[c]
            ds_new = dst[...]
            dob = do_ref[pl.ds(r0, CHUNK), :]
            e = jnp.exp(b)
            qe = q * e
            bl = b[CHUNK - 1:CHUNK, :]
            etail = jnp.exp(bl - b)
            kd = kk * etail
            dq_inter = _hmm(dob, s_prev) * e
            dv = _hmm_nt(kd, ds_new)
            dkk = _hmm(v, ds_new) * etail
            parts = _hgrn_intra(q, kk, b)
            a = jnp.where(causal, jnp.concatenate([_hmm_nt(qs, ks) for qs, ks, _, _ in parts], axis=0), 0.0)
            da = jnp.where(causal, _hmm_nt(dob, v), 0.0)
            dv = dv + _hmm_tn(a, dob)
            dq_rows = []
            for blk, (qs, ks, eq, ek) in enumerate(parts):
                da_blk = da[blk * SUB:(blk + 1) * SUB, :]
                dq_rows.append(_hmm(da_blk, ks) * eq)
                dkk = dkk + _hmm_tn(da_blk, qs) * ek
            dq = dq_inter + jnp.concatenate(dq_rows, axis=0)
            dst[...] = ds_new * jnp.exp(bl) + _hmm_tn(dob, qe)
            g = q * dq - kk * dkk
            dlf = _revcumsum_rows(g) + carry[0:1, :]
            carry[0:1, :] += jnp.sum(g, axis=0, keepdims=True)
            dkk_tot = dkk + dlf * jnp.where(kk < GATE_CLAMP, -1.0 / (1.0 - kk), 0.0)
            dq_ref[pl.ds(r0, CHUNK), :] = dq
            df_ref[pl.ds(r0, CHUNK), :] = dkk_tot * (1.0 - lbv) * (-sneg * (1.0 - sneg))
            di_ref[pl.ds(r0, CHUNK), :] = dv * _dsilu(iv)
            dlb_ref[...] += _colsum8(dkk_tot * (-sneg))
            return cr
        lax.fori_loop(0, nc, chunk, 0, unroll=2)

    def col(j):
        return pl.BlockSpec((t, LANE), lambda h: (0, HH * j + h))
    hs = pl.BlockSpec((t, LANE), lambda h: (0, h))
    return pl.pallas_call(
        body, name="hgrn_bwd", grid=(HH,),
        in_specs=[col(0), col(1), col(2), pl.BlockSpec((1, LANE), lambda h: (0, h)), hs,
                  pl.BlockSpec((None, nc, LANE, LANE), lambda h: (h, 0, 0, 0))],
        out_specs=[hs, hs, hs, pl.BlockSpec((8, LANE), lambda h: (0, h))],
        out_shape=[jax.ShapeDtypeStruct((t, HH * LANE), F32)] * 3 + [jax.ShapeDtypeStruct((8, HH * LANE), F32)],
        scratch_shapes=[pltpu.VMEM((LANE, LANE), F32), pltpu.VMEM((8, LANE), F32)],
        compiler_params=_cp("parallel"),
    )(h4, h4, h4, lb, do, states)


def _ln_fwd(z, g, b):
    mu = jnp.mean(z, axis=-1, keepdims=True)
    zc = z - mu
    rstd = lax.rsqrt(jnp.mean(zc * zc, axis=-1, keepdims=True) + EPS)
    zh = zc * rstd
    return zh * g + b, zh, rstd


def _mix_fwd(x, z, o_att, o_h, h4, gates, lng, lnb, wco, wao, ng, who, wout, t_end):
    t = x.shape[0]
    tm = _tile(t, 192)

    def body(x_ref, z_ref, oa_ref, oh_ref, hg_ref, gt_ref, lng_ref, lnb_ref, wco_ref, wao_ref, ng_ref, who_ref,
             wout_ref, x1_ref, mix_ref, ca_ref, oc_ref, ya_ref, yb_ref, yc_ref):
        i = pl.program_id(0)
        ln, _, _ = _ln_fwd(z_ref[...], lng_ref[...], lnb_ref[...])
        ca = _silu(ln).astype(_MM)
        ca_ref[...] = ca
        ya = jnp.dot(ca, wco_ref[...], preferred_element_type=F32)
        yb = _mm(oa_ref[...], wao_ref[...])
        hg = hg_ref[...]
        for h in range(HH):
            sl = slice(LANE * h, LANE * (h + 1))
            oh = oh_ref[:, sl]
            oc_ref[:, sl] = (oh * _rstd(oh) * ng_ref[:, sl] * _silu(hg[:, sl])).astype(_MM)
        yc = jnp.dot(oc_ref[...], who_ref[...], preferred_element_type=F32)
        ya_ref[...] = ya
        yb_ref[...] = yb
        yc_ref[...] = yc
        mix = (_sigmoid(gt_ref[:, 0:D]) * ya + _sigmoid(gt_ref[:, D:2 * D]) * yb
               + _sigmoid(gt_ref[:, 2 * D:3 * D]) * yc).astype(_MM)
        mix_ref[...] = mix
        x1_ref[...] = x_ref[...] + _valid_rows(i, tm, t_end) * jnp.dot(mix, wout_ref[...],
                                                                       preferred_element_type=F32)

    hd = NH * LANE
    return pl.pallas_call(
        body, name="mix_fwd", grid=(t // tm,),
        in_specs=[_row(tm, D), _row(tm, CONV_DIM), _row(tm, hd), _row(tm, 512), _row(tm, 512, 3), _row(tm, 3 * D),
                  _full((1, 512)), _full((1, 512)), _full((512, D)), _full((hd, D)), _full((1, 512)),
                  _full((512, D)), _full((D, D))],
        out_specs=[_row(tm, D), _row(tm, D), _row(tm, 512), _row(tm, 512), _row(tm, D), _row(tm, D), _row(tm, D)],
        out_shape=[jax.ShapeDtypeStruct((t, D), F32), jax.ShapeDtypeStruct((t, D), _MM),
                   jax.ShapeDtypeStruct((t, 512), _MM), jax.ShapeDtypeStruct((t, 512), _MM),
                   jax.ShapeDtypeStruct((t, D), F32), jax.ShapeDtypeStruct((t, D), F32),
                   jax.ShapeDtypeStruct((t, D), F32)],
        compiler_params=_cp("parallel"),
    )(x, z, o_att, o_h, h4, gates, lng, lnb, wco, wao, ng, who, wout)


def _mix_bwd(dx1, ya, yb, yc, gates, z, o_h, h4, lng, lnb, ng, wout, wco, wao, who, plan=None):
    t = dx1.shape[0]
    tm = _tile(t, 192)
    hd = NH * LANE
    p_args, p_in, p_out, p_shape, p_sem = _plan_specs(plan)

    def body(*refs):
        (ins, outs, _), rider = _host_refs(refs, 15, 12, 0, plan)
        (dx1_ref, ya_ref, yb_ref, yc_ref, gt_ref, z_ref, oh_ref, hg_ref, lng_ref, lnb_ref, ng_ref,
         wout_ref, wco_ref, wao_ref, who_ref) = ins
        (dgt_ref, dya_ref, dyb_ref, dyc_ref, dz_ref, doa_ref, doh_ref, dhg_ref,
         dlng_ref, dlnb_ref, dcb_ref, dng_ref) = outs
        i = pl.program_id(0)
        done = _ride(plan, rider, i, t // tm - 1)
        dmix = _mm_nt(dx1_ref[...], wout_ref[...])
        dys = []
        for j, y_ref in enumerate((ya_ref, yb_ref, yc_ref)):
            sg = _sigmoid(gt_ref[:, j * D:(j + 1) * D])
            dgt_ref[:, j * D:(j + 1) * D] = (dmix * y_ref[...] * sg * (1.0 - sg)).astype(_MM)
            dys.append((dmix * sg).astype(_MM))
        dya_ref[...], dyb_ref[...], dyc_ref[...] = dys
        dca = _mm_nt(dys[0], wco_ref[...])
        ln, zh, rstd = _ln_fwd(z_ref[...], lng_ref[...], lnb_ref[...])
        dln = dca * _dsilu(ln)
        dzh = dln * lng_ref[...]
        dz = rstd * (dzh - jnp.mean(dzh, axis=-1, keepdims=True)
                     - zh * jnp.mean(dzh * zh, axis=-1, keepdims=True))
        dz_ref[...] = dz
        doa_ref[...] = _mm_nt(dys[1], wao_ref[...])
        doc = _mm_nt(dys[2], who_ref[...])
        hg = hg_ref[...]
        dng_rows = []
        for h in range(HH):
            sl = slice(LANE * h, LANE * (h + 1))
            oh = oh_ref[:, sl]
            r = _rstd(oh)
            don = doc[:, sl] * _silu(hg[:, sl])
            dhg_ref[:, sl] = (doc[:, sl] * oh * r * ng_ref[:, sl] * _dsilu(hg[:, sl])).astype(_MM)
            doh, gn = _rms_bwd(don, oh, r, ng_ref[:, sl])
            doh_ref[:, sl] = doh
            dng_rows.append(_colsum8(gn))

        @pl.when(i == 0)
        def _():
            dlng_ref[...] = jnp.zeros_like(dlng_ref)
            dlnb_ref[...] = jnp.zeros_like(dlnb_ref)
            dcb_ref[...] = jnp.zeros_like(dcb_ref)
            dng_ref[...] = jnp.zeros_like(dng_ref)
        dlng_ref[...] += _colsum8(dln * zh)
        dlnb_ref[...] += _colsum8(dln)
        dcb_ref[...] += _colsum8(dz)
        dng_ref[...] += jnp.concatenate(dng_rows, axis=1)
        done()

    res = pl.pallas_call(
        body, name="mix_bwd", grid=(t // tm,),
        in_specs=[_row(tm, D), _row(tm, D), _row(tm, D), _row(tm, D), _row(tm, 3 * D), _row(tm, 512), _row(tm, 512),
                  _row(tm, 512, 3), _full((1, 512)), _full((1, 512)), _full((1, 512)),
                  _full((D, D)), _full((512, D)), _full((hd, D)), _full((512, D))] + p_in,
        out_specs=[_row(tm, 3 * D), _row(tm, D), _row(tm, D), _row(tm, D), _row(tm, 512), _row(tm, hd),
                   _row(tm, 512), _row(tm, 512), _full((8, 512)), _full((8, 512)), _full((8, 512)),
                   _full((8, 512))] + p_out,
        out_shape=[jax.ShapeDtypeStruct((t, 3 * D), _MM), jax.ShapeDtypeStruct((t, D), _MM),
                   jax.ShapeDtypeStruct((t, D), _MM), jax.ShapeDtypeStruct((t, D), _MM),
                   jax.ShapeDtypeStruct((t, 512), F32), jax.ShapeDtypeStruct((t, hd), F32),
                   jax.ShapeDtypeStruct((t, 512), F32), jax.ShapeDtypeStruct((t, 512), _MM)]
        + [jax.ShapeDtypeStruct((8, 512), F32)] * 4 + p_shape,
        scratch_shapes=p_sem,
        compiler_params=_cp("arbitrary"),
    )(dx1, ya, yb, yc, gates, z, o_h, h4, lng, lnb, ng, wout, wco, wao, who, *p_args)
    return res[:12], res[12:]


D_FF = 4096


def _ffn_fwd(x1, g2, w1, w2):
    t = x1.shape[0]
    tm = _tile(t, 192)

    def body(x1_ref, g_ref, w1_ref, w2_ref, x2_ref, p_ref):
        xv = x1_ref[...]
        h2 = (xv * _rstd(xv) * g_ref[...]).astype(_MM)
        p = jnp.dot(h2, w1_ref[...], preferred_element_type=F32)
        p_ref[...] = p
        r = jnp.maximum(p, 0.0)
        x2_ref[...] = xv + jnp.dot((r * r).astype(_MM), w2_ref[...], preferred_element_type=F32)

    return pl.pallas_call(
        body, name="ffn_fwd", grid=(t // tm,),
        in_specs=[_row(tm, D), _full((1, D)), _full((D, D_FF)), _full((D_FF, D))],
        out_specs=[_row(tm, D), _row(tm, D_FF)],
        out_shape=[jax.ShapeDtypeStruct((t, D), F32), jax.ShapeDtypeStruct((t, D_FF), F32)],
        compiler_params=_cp("parallel"),
    )(x1, g2, w1, w2)


def _ffn_bwd(dx2, x1, p, g2, w1t, w2t, plan=None):
    t = x1.shape[0]
    tm = _tile(t, 192)
    p_args, p_in, p_out, p_shape, p_sem = _plan_specs(plan)

    def body(*refs):
        (ins, outs, _), rider = _host_refs(refs, 6, 5, 0, plan)
        dx2_ref, x1_ref, p_ref, g_ref, w1t_ref, w2t_ref = ins
        dx1_ref, h2_ref, act_ref, dp_ref, dg_ref = outs
        i = pl.program_id(0)
        done = _ride(plan, rider, i, t // tm - 1)
        xv = x1_ref[...]
        rstd = _rstd(xv)
        h2_ref[...] = (xv * rstd * g_ref[...]).astype(_MM)
        r = jnp.maximum(p_ref[...], 0.0)
        act_ref[...] = (r * r).astype(_MM)
        dx2 = dx2_ref[...]
        da = _mm(dx2, w2t_ref[...])
        dp = (2.0 * r * da).astype(_MM)
        dp_ref[...] = dp
        dh2 = jnp.dot(dp, w1t_ref[...], preferred_element_type=F32)
        dxn, dgrow = _rms_bwd(dh2, xv, rstd, g_ref[...])
        dx1_ref[...] = dx2 + dxn

        @pl.when(i == 0)
        def _():
            dg_ref[...] = jnp.zeros_like(dg_ref)
        dg_ref[...] += _colsum8(dgrow)
        done()

    res = pl.pallas_call(
        body, name="ffn_bwd", grid=(t // tm,),
        in_specs=[_row(tm, D), _row(tm, D), _row(tm, D_FF), _full((1, D)), _full((D_FF, D)),
                  _full((D, D_FF))] + p_in,
        out_specs=[_row(tm, D), _row(tm, D), _row(tm, D_FF), _row(tm, D_FF), _full((8, D))] + p_out,
        out_shape=[jax.ShapeDtypeStruct((t, D), F32), jax.ShapeDtypeStruct((t, D), _MM),
                   jax.ShapeDtypeStruct((t, D_FF), _MM), jax.ShapeDtypeStruct((t, D_FF), _MM),
                   jax.ShapeDtypeStruct((8, D), F32)] + p_shape,
        scratch_shapes=p_sem,
        compiler_params=_cp("arbitrary"),
    )(dx2, x1, p, g2, w1t, w2t, *p_args)
    return res[:5], res[5:]


WGRAD_VMEM = 40 * 1024 * 1024


def _wgrad(a, b, name, chips=1):
    t, ka = a.shape
    nb = b.shape[1]
    cs = nb // chips
    widths = [d for d in range(cs, 0, -LANE) if cs % d == 0 and d % LANE == 0] or [cs]
    tn, tm = widths[-1], 64
    for d in widths:
        room = WGRAD_VMEM - 2 * ka * d * 4
        row_bytes = 2 * (ka * a.dtype.itemsize + d * b.dtype.itemsize) + 4 * ka
        fit = [r for r in range(64, t + 1, 64) if t % r == 0 and r * row_bytes <= room]
        if ka * d * 4 <= 16 * 1024 * 1024 and fit and (max(fit) >= 384 or d == widths[-1]):
            tn, tm = d, max(fit)
            break
    per = cs // tn

    def body(a_ref, b_ref, o_ref):
        @pl.when(pl.program_id(1) == 0)
        def _():
            o_ref[...] = jnp.zeros_like(o_ref)
        o_ref[...] += _mm_tn(a_ref[...], b_ref[...])

    if chips == 1:
        out_spec = pl.BlockSpec((ka, tn), lambda n, i: (0, n))
        out_shape = jax.ShapeDtypeStruct((ka, nb), F32)
    else:
        out_spec = pl.BlockSpec((None, ka, tn), lambda n, i: (n // per, 0, n % per))
        out_shape = jax.ShapeDtypeStruct((chips, ka, cs), F32)
    return pl.pallas_call(
        body, name="wgrad_" + name, grid=(nb // tn, t // tm),
        in_specs=[pl.BlockSpec((tm, ka), lambda n, i: (i, 0)), pl.BlockSpec((tm, tn), lambda n, i: (i, n))],
        out_specs=out_spec, out_shape=out_shape,
        compiler_params=_cp("parallel", "arbitrary"),
    )(a, b)


def _loss_head(y, target, t_end):
    t = y.shape[0]
    tm = _tile(t, 384)

    def body(y_ref, tg_ref, dy_ref, l_ref):
        i = pl.program_id(0)
        r = i * tm + lax.broadcasted_iota(jnp.int32, (tm, 1), 0)
        real = ((r >= ROW0) & (r < t_end)).astype(F32)
        diff = (y_ref[...] - tg_ref[...]) * real
        dy_ref[...] = diff * (1.0 / D)

        @pl.when(i == 0)
        def _():
            l_ref[...] = jnp.zeros_like(l_ref)
        sq = _colsum8(diff * diff)
        part = sq[:, 0:LANE]
        for j in range(1, D // LANE):
            part = part + sq[:, j * LANE:(j + 1) * LANE]
        l_ref[...] += part * (0.5 / D)

    return pl.pallas_call(
        body, name="loss_head", grid=(t // tm,),
        in_specs=[_row(tm, D), _row(tm, D)],
        out_specs=[_row(tm, D), _full((8, LANE))],
        out_shape=[jax.ShapeDtypeStruct((t, D), F32), jax.ShapeDtypeStruct((8, LANE), F32)],
        compiler_params=_cp("arbitrary"),
    )(y, target)


def _lower_bounds_fwd(logits):
    depth, n = logits.shape

    def body(l_ref, lb_ref):
        lg = l_ref[...]
        m = jnp.max(lg, axis=0, keepdims=True)
        e = jnp.exp(lg - m)
        p = e / jnp.sum(e, axis=0, keepdims=True)
        acc = jnp.zeros((1, n), F32)
        for l in range(depth):
            if l > 0:
                acc = acc + p[l:l + 1, :]
            lb_ref[l:l + 1, :] = acc

    return pl.pallas_call(body, name="lower_bounds_fwd", out_shape=jax.ShapeDtypeStruct((depth, n), F32))(logits)


def _lower_bounds_bwd(logits, dlb):
    depth, n = logits.shape

    def body(l_ref, dlb_ref, dl_ref):
        lg = l_ref[...]
        m = jnp.max(lg, axis=0, keepdims=True)
        e = jnp.exp(lg - m)
        p = e / jnp.sum(e, axis=0, keepdims=True)
        dps = [jnp.zeros((1, n), F32)]
        for j in range(1, depth):
            acc = jnp.zeros((1, n), F32)
            for l in range(j, depth):
                acc = acc + dlb_ref[l:l + 1, :]
            dps.append(acc)
        dot = jnp.zeros((1, n), F32)
        for j in range(depth):
            dot = dot + p[j:j + 1, :] * dps[j]
        for j in range(depth):
            dl_ref[j:j + 1, :] = p[j:j + 1, :] * (dps[j] - dot)

    return pl.pallas_call(body, name="lower_bounds_bwd", out_shape=jax.ShapeDtypeStruct((depth, n), F32))(logits, dlb)


def _ew_tile(rows, cols, n_arrays):
    cap = max(16, (24 * 1024 * 1024) // (8 * n_arrays * cols))
    best = None
    for t in range(16, rows + 1, 16):
        if rows % t == 0 and t <= cap:
            best = t
    return rows if best is None else best


def _adamw_math(w, g, m, v):
    mn = ADAM_B1 * m + (1.0 - ADAM_B1) * g
    vn = ADAM_B2 * v + (1.0 - ADAM_B2) * (g * g)
    m_hat = mn / (1.0 - ADAM_B1 ** ADAM_STEP)
    v_hat = vn / (1.0 - ADAM_B2 ** ADAM_STEP)
    return -ADAM_LR * (m_hat / (jnp.sqrt(v_hat) + ADAM_EPS) + ADAM_WD * w), mn, vn


def _adamw_layers(w, m, v, g0, g1, g_sibling, name):
    _, r, c_ = w.shape
    tr = _ew_tile(r, c_, 10)

    def body(w_ref, m_ref, v_ref, g0_ref, g1_ref, gs_ref, g_ref, d_ref, mo_ref, vo_ref):
        layer = pl.program_id(0)
        own = jnp.where(layer == 0, g0_ref[...], g1_ref[...])
        g = jnp.where(layer == lax.axis_index("c"), own, gs_ref[...])
        g_ref[...] = g
        d_ref[...], mo_ref[...], vo_ref[...] = _adamw_math(w_ref[...], g, m_ref[...], v_ref[...])

    lay = pl.BlockSpec((None, tr, c_), lambda l, i: (l, i, 0))
    flat = pl.BlockSpec((tr, c_), lambda l, i: (i, 0))
    return pl.pallas_call(
        body, name="adamw_" + name, grid=(2, r // tr),
        in_specs=[lay, lay, lay, flat, flat, flat], out_specs=[lay] * 4,
        out_shape=[jax.ShapeDtypeStruct(w.shape, F32)] * 4,
        compiler_params=_cp("parallel", "parallel"),
    )(w, m, v, g0, g1, g_sibling)


def _adamw(w, g, m, v, name):
    rows, cols = w.shape
    tr = _ew_tile(rows, cols, 7)

    def body(w_ref, g_ref, m_ref, v_ref, d_ref, mo_ref, vo_ref):
        d_ref[...], mo_ref[...], vo_ref[...] = _adamw_math(w_ref[...], g_ref[...], m_ref[...], v_ref[...])

    spec = pl.BlockSpec((tr, cols), lambda i: (i, 0))
    return pl.pallas_call(
        body, name="adamw_" + name, grid=(rows // tr,),
        in_specs=[spec] * 4, out_specs=[spec] * 3,
        out_shape=[jax.ShapeDtypeStruct((rows, cols), F32)] * 3,
        compiler_params=_cp("parallel"),
    )(w, g, m, v)


DEPTH = 2
BIG_SHAPES = {"w_in": ((1024, 6560), 1), "w_conv_out": ((512, 1024), 1), "w_uq": ((256, 768), 1),
              "w_ukv": ((128, 1024), 1), "w_attn_out": ((512, 1024), 1), "w_hgrn_out": ((512, 1024), 1),
              "w_out": ((1024, 1024), 0), "w_ff1": ((1024, 4096), 1), "w_ff2": ((4096, 1024), 0)}
BIG = tuple(BIG_SHAPES)
SMALL_SIZES = {"norm1_g": 1024, "conv_b": 512, "conv_ln_g": 512, "conv_ln_b": 512, "q_a_norm_g": 256,
               "kv_a_norm_g": 128, "q_norm_g": 96, "k_norm_g": 96, "hgrn_lb_logits": 512, "hgrn_norm_g": 512,
               "norm2_g": 1024}
SMALL = tuple(SMALL_SIZES)
W_IN_COLS = 6560
W_IN_SHARD = W_IN_COLS // 4
W_IN_SEGS = ((0, 1024, SEG_AG[0]), (1024, 1280, SEG_CQ[0]), (1280, 1408, SEG_CKV[0]), (1408, 1440, SEG_KR[0] + 64),
             (1440, 3488, SEG_H4[0]), (3488, 6560, SEG_GATES[0]))


def _pad_heads(w, nh, used, axis):
    shp = w.shape
    w = w.reshape(shp[:axis] + (nh, used) + shp[axis + 1:])
    pad = [(0, 0)] * w.ndim
    pad[axis + 1] = (0, LANE - used)
    w = jnp.pad(w, pad)
    return w.reshape(shp[:axis] + (nh * LANE,) + shp[axis + 1:])


def _unpad_heads(w, nh, used, axis):
    shp = w.shape
    w = w.reshape(shp[:axis] + (nh, LANE) + shp[axis + 1:])
    w = lax.slice_in_dim(w, 0, used, axis=axis + 1)
    return w.reshape(shp[:axis] + (nh * used,) + shp[axis + 1:])


def _w_in_from_chips(p4):
    def orig(a, b):
        out = []
        while a < b:
            s = a // W_IN_SHARD
            e = min(b, (s + 1) * W_IN_SHARD)
            out.append(p4[s][:, a - W_IN_SHARD * s:e - W_IN_SHARD * s])
            a = e
        return out
    zc = lambda n: jnp.zeros((D, n), p4[0].dtype)
    parts = (orig(3488, 6560) + orig(0, 1024) + orig(1440, 3488) + orig(1024, 1280) + orig(1280, 1408)
             + [zc(64)] + orig(1408, 1440) + [zc(32)])
    return jnp.concatenate(parts, axis=1)


def _w_in_grad_to_chips(dw):
    chips = []
    for s in range(4):
        a, b = W_IN_SHARD * s, W_IN_SHARD * (s + 1)
        parts = []
        for o0, o1, p0 in W_IN_SEGS:
            lo, hi = max(a, o0), min(b, o1)
            if lo < hi:
                parts.append(dw[:, p0 + lo - o0:p0 + hi - o0])
        chips.append(jnp.concatenate(parts, axis=1))
    return jnp.stack(chips)


def _cat_chips(p4, axis):
    return jnp.concatenate([p4[s] for s in range(4)], axis=axis)


EARLY = ("w_in", "w_uq", "w_ukv")
LATE = tuple(k for k in BIG if k not in EARLY)


def _prep_late(pieces):
    pc = lambda k: [pieces[k][s].astype(_MM) for s in range(4)]
    return dict(wao=_pad_heads(_cat_chips(pc("w_attn_out"), 1), NH, 64, 0), wco=_cat_chips(pc("w_conv_out"), 1),
                who=_cat_chips(pc("w_hgrn_out"), 1), wout=_cat_chips(pc("w_out"), 0),
                w1=_cat_chips(pc("w_ff1"), 1), w2=_cat_chips(pc("w_ff2"), 0))


def _prep_early(pieces, small, l):
    mm = lambda a: a.astype(_MM)
    pc = lambda k: [mm(pieces[k][s]) for s in range(4)]
    w_in_p = _w_in_from_chips(pc("w_in"))
    wuq = jnp.concatenate([_pad_heads(pc("w_uq")[s], 2, QK_DIM, 1) for s in range(4)], axis=1)
    wukv = _cat_chips(pc("w_ukv"), 1).reshape(128, NH, 128)
    wk = _pad_heads(wukv[:, :, :64].reshape(128, NH * 64), NH, 64, 1)
    wv = _pad_heads(wukv[:, :, 64:].reshape(128, NH * 64), NH, 64, 1)
    row = lambda a: a.astype(F32).reshape(1, -1)
    p = dict(
        w_in=w_in_p, w_in_t=w_in_p.T, wuq=wuq, wk=wk, wv=wv,
        g1=row(small["norm1_g"][l]), g2=row(small["norm2_g"][l]),
        cw=jnp.pad(small["conv_w"][l].astype(F32), ((0, 1), (0, 0))), cb=row(small["conv_b"][l]),
        lng=row(small["conv_ln_g"][l]), lnb=row(small["conv_ln_b"][l]),
        qag=row(small["q_a_norm_g"][l]), kvag=row(small["kv_a_norm_g"][l]),
        qng=jnp.pad(row(small["q_norm_g"][l]), ((0, 0), (0, LANE - QK_DIM))),
        kng=jnp.pad(row(small["k_norm_g"][l]), ((0, 0), (0, LANE - QK_DIM))),
        ng=row(small["hgrn_norm_g"][l]),
    )
    return p


def _rope_tables(t):
    pos = (jnp.arange(t, dtype=jnp.int32) - FRONT).astype(F32)
    inv_freq = 10000.0 ** (-jnp.arange(16, dtype=F32) / 16)
    ang = pos[:, None] * inv_freq[None, :]
    cos, sin = jnp.cos(ang), jnp.sin(ang)
    one = jnp.ones((t, 64), F32)
    z16, z32, z64 = jnp.zeros((t, 16), F32), jnp.zeros((t, 32), F32), jnp.zeros((t, 64), F32)
    c = jnp.concatenate([one, cos, cos, z32], axis=1)
    s1 = jnp.concatenate([z64, -sin, z16, z32], axis=1)
    s2 = jnp.concatenate([z64, z16, sin, z32], axis=1)
    return c, s1, s2


def _layer_fwd(x, p, lb, rope, t_end, plan=None, on_rode=None):
    gates, ag, h4, cq, ckv, kr, hb = _in_proj_fwd(x, p["g1"], p["w_in"])
    z = _conv_fwd(ag, p["cw"], p["cb"])
    q, k, v, cqn, ckvn = _mla_pre_fwd(cq, ckv, kr, p["qag"], p["wuq"], p["kvag"], p["wk"], p["wv"], p["qng"],
                                      p["kng"], *rope)
    (o_att, lse), rode = _attn_fwd(q, k, v, plan)
    if on_rode is not None:
        on_rode(rode)
    o_h, states = _hgrn_fwd(h4, lb)
    x1, mix, ca, oc, ya, yb, yc = _mix_fwd(x, z, o_att, o_h, h4, gates, p["lng"], p["lnb"], p["wco"], p["wao"],
                                           p["ng"], p["who"], p["wout"], t_end)
    x2, pre = _ffn_fwd(x1, p["g2"], p["w1"], p["w2"])
    saved = dict(x=x, gates=gates, ag=ag, h4=h4, cq=cq, ckv=ckv, kr=kr, hb=hb, z=z, q=q, k=k, v=v, cqn=cqn,
                 ckvn=ckvn, o_att=o_att, lse=lse, o_h=o_h, states=states, x1=x1, mix=mix, ca=ca, oc=oc,
                 ya=ya, yb=yb, yc=yc, pre=pre)
    return x2, saved


def _layer_bwd(dx2, s, p, lb, rope, t_end, rides=None):
    rides = rides or {}
    (dx1, h2, act, dp, dg2), rode = _ffn_bwd(dx2, s["x1"], s["pre"], p["g2"], p["w1"].T, p["w2"].T,
                                             rides.get("ffn"))
    g = {"w_ff1": _wgrad(h2, dp, "ff1", 4), "w_ff2": _wgrad(act, dx2, "ff2").reshape(4, D_FF // 4, D),
         "norm2_g": dg2.sum(0)}
    plan_mix = rides["mix"](rode, g) if "mix" in rides else None
    (dgt, dya, dyb, dyc, dz, doa, doh, dhg, dlng, dlnb, dcb, dng), rode = _mix_bwd(
        dx1, s["ya"], s["yb"], s["yc"], s["gates"], s["z"], s["o_h"], s["h4"], p["lng"], p["lnb"], p["ng"],
        p["wout"], p["wco"], p["wao"], p["who"], plan_mix)
    plan_attn = rides["attn"](rode) if "attn" in rides else None
    g["w_out"] = _wgrad(s["mix"], dx1, "out").reshape(4, D // 4, D)
    g["w_conv_out"] = _wgrad(s["ca"], dya, "conv_out", 4)
    g["w_attn_out"] = _unpad_heads(_wgrad(s["o_att"], dyb, "attn_out", 4), NH, 64, 1)
    g["w_hgrn_out"] = _wgrad(s["oc"], dyc, "hgrn_out", 4)
    g["conv_ln_g"], g["conv_ln_b"], g["conv_b"], g["hgrn_norm_g"] = dlng.sum(0), dlnb.sum(0), dcb.sum(0), dng.sum(0)
    da, dg, dcw = _conv_bwd(s["ag"], p["cw"], dz)
    g["conv_w"] = dcw[:CONV_K]
    (dq, dk, dv), rode_attn = _attn_bwd(s["q"], s["k"], s["v"], s["o_att"], s["lse"], doa, plan_attn)
    dcq, dckv, dkr, dqraw, dkraw, dqag, dkvag, dqng, dkng = _mla_pre_bwd(
        dq, dk, dv, s["cq"], s["ckv"], s["kr"], p["qag"], p["wuq"], p["kvag"], p["wk"], p["wv"], p["qng"],
        p["kng"], *rope)
    g["w_uq"] = _unpad_heads(_wgrad(s["cqn"], dqraw, "uq", 4), 2, QK_DIM, 2)
    dwk = _unpad_heads(_wgrad(s["ckvn"], dkraw, "uk"), NH, 64, 1).reshape(128, NH, 64)
    dwv = _unpad_heads(_wgrad(s["ckvn"], dv, "uv"), NH, 64, 1).reshape(128, NH, 64)
    g["w_ukv"] = jnp.concatenate([dwk, dwv], axis=2).reshape(128, 4, 256).transpose(1, 0, 2)
    g["q_a_norm_g"], g["kv_a_norm_g"] = dqag.sum(0), dkvag.sum(0)
    g["q_norm_g"], g["k_norm_g"] = dqng.sum(0)[:QK_DIM], dkng.sum(0)[:QK_DIM]
    dhq, dhf, dhi, dlb = _hgrn_bwd(s["h4"], lb, doh, s["states"])
    mm = lambda a: a.astype(_MM)
    du = jnp.concatenate([dgt, mm(da), mm(dg), mm(dhq), mm(dhf), mm(dhi), dhg, dcq, dckv, dkr], axis=1)
    dx, dg1 = _in_proj_bwd(du, s["x"], dx1, p["g1"], p["w_in_t"], t_end)
    g["norm1_g"] = dg1.sum(0)
    g["w_in"] = _w_in_grad_to_chips(_wgrad(s["hb"], du, "in"))
    return dx, g, dlb.sum(0), rode_attn


def _device_step(x, target, small, pieces0, pieces1=None, fwd_ride=None, bwd_rides=None):
    s_real = x.shape[0]
    t_end = ROW0 + s_real
    t = -(-t_end // LANE) * LANE
    zrow = lambda n: jnp.zeros((n, D), F32)
    xp = jnp.concatenate([zrow(FRONT), small["meta"].astype(F32), x, zrow(t - t_end)], axis=0)
    tp = jnp.concatenate([zrow(ROW0), target, zrow(t - t_end)], axis=0)
    rope = _rope_tables(t)
    logits = small["hgrn_lb_logits"].astype(F32)
    lbs = _lower_bounds_fwd(logits)
    prm0 = _prep_early(pieces0, small, 0)
    got = {}
    if fwd_ride is None:
        prm0.update(_prep_late(pieces0))
        h, sv0 = _layer_fwd(xp, prm0, lbs[0:1], rope, t_end)
    else:
        def on_rode(rode):
            late0, got["pieces1"] = fwd_ride[1](rode)
            prm0.update(_prep_late(late0))
        h, sv0 = _layer_fwd(xp, prm0, lbs[0:1], rope, t_end, fwd_ride[0], on_rode)
        pieces1 = got["pieces1"]
    prm1 = _prep_early(pieces1, small, 1)
    prm1.update(_prep_late(pieces1))
    h, sv1 = _layer_fwd(h, prm1, lbs[1:2], rope, t_end)
    dh, lsum = _loss_head(h, tp, t_end)
    loss = jnp.sum(lsum)
    dh, g1, dlb1, _ = _layer_bwd(dh, sv1, prm1, lbs[1:2], rope, t_end)
    dh, g0, dlb0, rode = _layer_bwd(dh, sv0, prm0, lbs[0:1], rope, t_end,
                                    None if bwd_rides is None else bwd_rides(g1))
    dlogits = _lower_bounds_bwd(logits, jnp.stack([dlb0, dlb1]))
    grads = [g0, g1]
    for l in range(DEPTH):
        grads[l]["hgrn_lb_logits"] = dlogits[l]
    return loss, dh[ROW0:t_end], grads, dh[FRONT:ROW0], rode


MESH = pl.DeviceIdType.MESH
_ANY = pl.BlockSpec(memory_space=pl.ANY)
SMALL_ROWS = 64
SMALL_LEN = SMALL_ROWS * 1024


def _mesh_pos():
    return lax.axis_index("x"), lax.axis_index("y"), lax.axis_index("c")


def _other_chips(x, y):
    return [(1 - x, y), (x, 1 - y), (1 - x, 1 - y)]


class _Plan:
    def __init__(self, name, ins, out_shapes, sems, start, finish):
        self.name, self.ins, self.out_shapes, self.sems = name, list(ins), list(out_shapes), list(sems)
        self.start, self.finish = start, finish


def _run_plan(plan):
    ni, no = len(plan.ins), len(plan.out_shapes)

    def body(*refs):
        ins, outs, sems = refs[:ni], refs[ni:ni + no], refs[ni + no:]
        plan.start(ins, outs, sems)
        plan.finish(ins, outs, sems)

    return pl.pallas_call(body, name=plan.name, in_specs=[_ANY] * ni, out_specs=[_ANY] * no,
                          out_shape=plan.out_shapes, scratch_shapes=plan.sems)(*plan.ins)


def _plan_specs(plan):
    if plan is None:
        return [], [], [], [], []
    return plan.ins, [_ANY] * len(plan.ins), [_ANY] * len(plan.out_shapes), plan.out_shapes, plan.sems


def _host_refs(refs, n_in, n_out, n_scratch, plan):
    ni = 0 if plan is None else len(plan.ins)
    no = 0 if plan is None else len(plan.out_shapes)
    o0 = n_in + ni
    s0 = o0 + n_out + no
    own = (refs[:n_in], refs[o0:o0 + n_out], refs[s0:s0 + n_scratch])
    rider = (refs[n_in:o0], refs[o0 + n_out:s0], refs[s0 + n_scratch:])
    return own, rider


def _ride(plan, rider, step, last):
    if plan is None:
        return lambda: None

    @pl.when(step == 0)
    def _():
        plan.start(*rider)

    def done():
        @pl.when(step == last)
        def _():
            plan.finish(*rider)
    return done


def _merge_plans(name, plans):
    def parts(ins, outs, sems):
        i = o = s = 0
        for p in plans:
            ni, no, ns = len(p.ins), len(p.out_shapes), len(p.sems)
            yield p, (ins[i:i + ni], outs[o:o + no], sems[s:s + ns])
            i, o, s = i + ni, o + no, s + ns

    def start(ins, outs, sems):
        for p, refs in parts(ins, outs, sems):
            p.start(*refs)

    def finish(ins, outs, sems):
        for p, refs in parts(ins, outs, sems):
            p.finish(*refs)

    return _Plan(name, [a for p in plans for a in p.ins], [a for p in plans for a in p.out_shapes],
                 [a for p in plans for a in p.sems], start, finish)


def _plan_gather(own, layer, name):
    nw = len(own)

    def copies(ins, outs, sems):
        send_sems, recv_sems = sems

        def over_ici(w, j, chip_of_data, to):
            return pltpu.make_async_remote_copy(
                src_ref=ins[w].at[layer], dst_ref=outs[w].at[chip_of_data], send_sem=send_sems.at[w, j],
                recv_sem=recv_sems.at[w, j], device_id=to, device_id_type=MESH)

        def over_d2d(w, j, chip_of_data, to):
            return pltpu.make_async_remote_copy(
                src_ref=outs[w].at[chip_of_data], dst_ref=outs[w].at[chip_of_data], send_sem=send_sems.at[w, 3 + j],
                recv_sem=recv_sems.at[w, 3 + j], device_id=to, device_id_type=MESH)
        return over_ici, over_d2d

    def start(ins, outs, sems):
        x, y, c = _mesh_pos()
        over_ici, _ = copies(ins, outs, sems)

        @pl.when(c == layer)
        def _():
            for j, (px, py) in enumerate(_other_chips(x, y)):
                for w in range(nw):
                    over_ici(w, j, 2 * x + y, (px, py, layer)).start()

    def finish(ins, outs, sems):
        x, y, c = _mesh_pos()
        over_ici, over_d2d = copies(ins, outs, sems)
        chips = _other_chips(x, y)

        @pl.when(c == layer)
        def _():
            for j, (px, py) in enumerate(chips):
                for w in range(nw):
                    over_ici(w, j, 2 * px + py, (x, y, c)).wait_recv()
                    over_d2d(w, j, 2 * px + py, (x, y, 1 - layer)).start()
            for j, (px, py) in enumerate(chips):
                for w in range(nw):
                    over_ici(w, j, 2 * x + y, (px, py, layer)).wait_send()
                    over_d2d(w, j, 2 * px + py, (x, y, 1 - layer)).wait_send()

        @pl.when(c != layer)
        def _():
            for j, (px, py) in enumerate(chips):
                for w in range(nw):
                    over_d2d(w, j, 2 * px + py, (x, y, c)).wait_recv()

    return _Plan(name, own,
                 [jax.ShapeDtypeStruct((4,) + a.shape[1:], a.dtype) for a in own],
                 [pltpu.SemaphoreType.DMA((nw, 6)), pltpu.SemaphoreType.DMA((nw, 6))], start, finish)


def _plan_to_sibling(arrs, layer, name):
    nw = len(arrs)

    def copy(ins, outs, sems, w):
        x, y, _ = _mesh_pos()
        return pltpu.make_async_remote_copy(src_ref=ins[w], dst_ref=outs[w], send_sem=sems[0].at[w],
                                            recv_sem=sems[1].at[w], device_id=(x, y, layer), device_id_type=MESH)

    def start(ins, outs, sems):
        @pl.when(lax.axis_index("c") != layer)
        def _():
            for w in range(nw):
                copy(ins, outs, sems, w).start()

    def finish(ins, outs, sems):
        c = lax.axis_index("c")

        @pl.when(c != layer)
        def _():
            for w in range(nw):
                copy(ins, outs, sems, w).wait_send()

        @pl.when(c == layer)
        def _():
            for w in range(nw):
                copy(ins, outs, sems, w).wait_recv()

    return _Plan(name, arrs, [jax.ShapeDtypeStruct(a.shape, a.dtype) for a in arrs],
                 [pltpu.SemaphoreType.DMA((nw,)), pltpu.SemaphoreType.DMA((nw,))], start, finish)


def _plan_scatter(parts, layer, name):
    nw = len(parts)

    def start(ins, outs, sems):
        x, y, c = _mesh_pos()

        @pl.when(c == layer)
        def _():
            for j, (px, py) in enumerate(_other_chips(x, y)):
                for w in range(nw):
                    pltpu.make_async_remote_copy(
                        src_ref=ins[w].at[2 * px + py], dst_ref=outs[w].at[2 * x + y], send_sem=sems[0].at[w, j],
                        recv_sem=sems[1].at[w, j], device_id=(px, py, layer), device_id_type=MESH).start()

    def finish(ins, outs, sems):
        x, y, c = _mesh_pos()

        @pl.when(c == layer)
        def _():
            for j, (px, py) in enumerate(_other_chips(x, y)):
                for w in range(nw):
                    pltpu.make_async_remote_copy(
                        src_ref=ins[w].at[2 * px + py], dst_ref=outs[w].at[2 * px + py], send_sem=sems[0].at[w, j],
                        recv_sem=sems[1].at[w, j], device_id=(x, y, c), device_id_type=MESH).wait()

    return _Plan(name, parts, [jax.ShapeDtypeStruct(a.shape, a.dtype) for a in parts],
                 [pltpu.SemaphoreType.DMA((nw, 3)), pltpu.SemaphoreType.DMA((nw, 3))], start, finish)


def _sibling_exchange(reds0, reds1):
    nw = len(reds0)

    def body(*refs):
        a0, a1, outs = refs[:nw], refs[nw:2 * nw], refs[2 * nw:3 * nw]
        send_sems, recv_sems = refs[3 * nw:]
        x, y, c = _mesh_pos()

        def copy(w, src):
            return pltpu.make_async_remote_copy(src_ref=src, dst_ref=outs[w], send_sem=send_sems.at[w],
                                                recv_sem=recv_sems.at[w], device_id=(x, y, 1 - c),
                                                device_id_type=MESH)

        @pl.when(c == 0)
        def _():
            for w in range(nw):
                copy(w, a0[w]).start()

        @pl.when(c == 1)
        def _():
            for w in range(nw):
                copy(w, a1[w]).start()

        for w in range(nw):
            copy(w, a0[w]).wait()

    return pl.pallas_call(
        body, name="sibling_exchange", in_specs=[_ANY] * (2 * nw), out_specs=[_ANY] * nw,
        out_shape=[jax.ShapeDtypeStruct(a.shape, a.dtype) for a in reds0],
        scratch_shapes=[pltpu.SemaphoreType.DMA((nw,)), pltpu.SemaphoreType.DMA((nw,))],
    )(*reds0, *reds1)


def _all_reduce_small(v, name):
    rows, cols = v.shape

    def body(v_ref, o_ref, slots, send_sems, recv_sems):
        x, y, c = _mesh_pos()
        me = 4 * x + 2 * y + c
        slots[me] = v_ref[...]
        peers = []
        for rel in range(1, 8):
            fx, fy, fc = (rel >> 2) & 1, (rel >> 1) & 1, rel & 1
            px = 1 - x if fx else x
            py = 1 - y if fy else y
            pc = 1 - c if fc else c
            peers.append((px, py, pc))
        cps = [pltpu.make_async_remote_copy(src_ref=v_ref, dst_ref=slots.at[me], send_sem=send_sems.at[k],
                                            recv_sem=recv_sems.at[k], device_id=peer, device_id_type=MESH)
               for k, peer in enumerate(peers)]
        for cp in cps:
            cp.start()
        for k, (px, py, pc) in enumerate(peers):
            pltpu.make_async_remote_copy(src_ref=v_ref, dst_ref=slots.at[4 * px + 2 * py + pc],
                                         send_sem=send_sems.at[k], recv_sem=recv_sems.at[k], device_id=(x, y, c),
                                         device_id_type=MESH).wait_recv()
        for cp in cps:
            cp.wait_send()
        acc = slots[0]
        for d in range(1, 8):
            acc = acc + slots[d]
        o_ref[...] = acc

    vm = pl.BlockSpec(memory_space=pltpu.VMEM)
    return pl.pallas_call(
        body, name=name, in_specs=[vm], out_specs=vm,
        out_shape=jax.ShapeDtypeStruct((rows, cols), F32),
        scratch_shapes=[pltpu.VMEM((8, rows, cols), F32), pltpu.SemaphoreType.DMA((7,)),
                        pltpu.SemaphoreType.DMA((7,))],
    )(v)


def _add_to_wire(a, b, name):
    n4, r, c_ = a.shape
    rows = n4 * r
    tr = _ew_tile(rows, c_, 3)

    def body(a_ref, b_ref, o_ref):
        o_ref[...] = (a_ref[...] + b_ref[...]).astype(o_ref.dtype)

    spec = pl.BlockSpec((tr, c_), lambda i: (i, 0))
    out = pl.pallas_call(
        body, name="add_to_wire_" + name, grid=(rows // tr,), in_specs=[spec, spec], out_specs=spec,
        out_shape=jax.ShapeDtypeStruct((rows, c_), jnp.bfloat16), compiler_params=_cp("parallel"),
    )(a.reshape(rows, c_), b.reshape(rows, c_))
    return out.reshape(n4, r, c_)


def _sum_chips(recv, own, name):
    _, r, c_ = recv.shape
    tr = _ew_tile(r, c_, 4)

    def body(r_ref, own_ref, o_ref):
        chip = 2 * lax.axis_index("x") + lax.axis_index("y")
        own_v = own_ref[...].astype(F32)
        acc = None
        for s in range(4):
            term = jnp.where(chip == s, own_v, r_ref[s].astype(F32))
            acc = term if acc is None else acc + term
        o_ref[...] = acc

    return pl.pallas_call(
        body, name="sum_chips_" + name, grid=(r // tr,),
        in_specs=[pl.BlockSpec((4, tr, c_), lambda i: (0, i, 0)), pl.BlockSpec((tr, c_), lambda i: (i, 0))],
        out_specs=pl.BlockSpec((tr, c_), lambda i: (i, 0)),
        out_shape=jax.ShapeDtypeStruct((r, c_), F32),
        compiler_params=_cp("parallel"),
    )(recv, own)


def _pack_small(vals, meta_full, conv_w_full):
    flat = jnp.concatenate([vals[k].reshape(-1) for k in SMALL] + [meta_full.reshape(-1), conv_w_full.reshape(-1)])
    return jnp.pad(flat, (0, SMALL_LEN - flat.shape[0])).reshape(SMALL_ROWS, 1024)


def _unpack_small(buf):
    flat = buf.reshape(-1)
    out, off = {}, 0
    for k in SMALL:
        n = DEPTH * SMALL_SIZES[k]
        out[k] = flat[off:off + n].reshape(DEPTH, SMALL_SIZES[k])
        off += n
    meta = flat[off:off + N_META * D].reshape(N_META, D)
    off += N_META * D
    conv_w = flat[off:off + DEPTH * CONV_K * CONV_DIM].reshape(DEPTH, CONV_K, CONV_DIM)
    return out, meta, conv_w


def kernel(x, meta, norm1_g, w_in, conv_w, conv_b, conv_ln_g, conv_ln_b, w_conv_out, q_a_norm_g, w_uq, kv_a_norm_g, w_ukv, q_norm_g, k_norm_g, w_attn_out, hgrn_lb_logits, hgrn_norm_g, w_hgrn_out, w_out, norm2_g, w_ff1, w_ff2, loss_target, m_meta, m_norm1_g, m_w_in, m_conv_w, m_conv_b, m_conv_ln_g, m_conv_ln_b, m_w_conv_out, m_q_a_norm_g, m_w_uq, m_kv_a_norm_g, m_w_ukv, m_q_norm_g, m_k_norm_g, m_w_attn_out, m_hgrn_lb_logits, m_hgrn_norm_g, m_w_hgrn_out, m_w_out, m_norm2_g, m_w_ff1, m_w_ff2, v_meta, v_norm1_g, v_w_in, v_conv_w, v_conv_b, v_conv_ln_g, v_conv_ln_b, v_w_conv_out, v_q_a_norm_g, v_w_uq, v_kv_a_norm_g, v_w_ukv, v_q_norm_g, v_k_norm_g, v_w_attn_out, v_hgrn_lb_logits, v_hgrn_norm_g, v_w_hgrn_out, v_w_out, v_norm2_g, v_w_ff1, v_w_ff2):
    names = ("meta", "norm1_g", "w_in", "conv_w", "conv_b", "conv_ln_g", "conv_ln_b", "w_conv_out", "q_a_norm_g",
             "w_uq", "kv_a_norm_g", "w_ukv", "q_norm_g", "k_norm_g", "w_attn_out", "hgrn_lb_logits", "hgrn_norm_g",
             "w_hgrn_out", "w_out", "norm2_g", "w_ff1", "w_ff2")
    w = dict(zip(names, (meta, norm1_g, w_in, conv_w, conv_b, conv_ln_g, conv_ln_b, w_conv_out, q_a_norm_g, w_uq,
                         kv_a_norm_g, w_ukv, q_norm_g, k_norm_g, w_attn_out, hgrn_lb_logits, hgrn_norm_g, w_hgrn_out,
                         w_out, norm2_g, w_ff1, w_ff2)))
    m = dict(zip(names, (m_meta, m_norm1_g, m_w_in, m_conv_w, m_conv_b, m_conv_ln_g, m_conv_ln_b, m_w_conv_out,
                         m_q_a_norm_g, m_w_uq, m_kv_a_norm_g, m_w_ukv, m_q_norm_g, m_k_norm_g, m_w_attn_out,
                         m_hgrn_lb_logits, m_hgrn_norm_g, m_w_hgrn_out, m_w_out, m_norm2_g, m_w_ff1, m_w_ff2)))
    v = dict(zip(names, (v_meta, v_norm1_g, v_w_in, v_conv_w, v_conv_b, v_conv_ln_g, v_conv_ln_b, v_w_conv_out,
                         v_q_a_norm_g, v_w_uq, v_kv_a_norm_g, v_w_ukv, v_q_norm_g, v_k_norm_g, v_w_attn_out,
                         v_hgrn_lb_logits, v_hgrn_norm_g, v_w_hgrn_out, v_w_out, v_norm2_g, v_w_ff1, v_w_ff2)))
    cx, cy, cc = _mesh_pos()
    chip = 2 * cx + cy
    zero = jnp.zeros((), jnp.int32)

    own = {k: w[k].astype(_MM) for k in BIG}

    def as_pieces(names, gathered, layer):
        return {k: [jnp.where(chip == s, own[k][layer], g[s]) for s in range(4)] for k, g in zip(names, gathered)}

    pieces0 = as_pieces(EARLY, _run_plan(_plan_gather([own[k] for k in EARLY], 0, "gather_l0_early")), 0)
    fwd_ride = (_merge_plans("gather_rest", [_plan_gather([own[k] for k in LATE], 0, "gather_l0_late"),
                                             _plan_gather([own[k] for k in BIG], 1, "gather_l1")]),
                lambda got: (as_pieces(LATE, got[:len(LATE)], 0), as_pieces(BIG, got[len(LATE):], 1)))
    meta_slab = lax.dynamic_update_slice(jnp.zeros((N_META, D), F32), meta, (zero, chip * (D // 4)))
    convw_slab = lax.dynamic_update_slice(jnp.zeros((DEPTH, CONV_K, CONV_DIM), F32), conv_w,
                                          (zero, zero, chip * (CONV_DIM // 4)))
    zsmall = {k: jnp.zeros((DEPTH, SMALL_SIZES[k]), F32) for k in SMALL}
    south = (cc == 0).astype(F32)
    _, meta_full, convw_full = _unpack_small(
        _all_reduce_small(_pack_small(zsmall, meta_slab, convw_slab) * south, "gather_small"))
    small = {k: w[k] for k in SMALL}
    small["meta"] = meta_full
    small["conv_w"] = convw_full

    FFN = ("w_ff1", "w_ff2")
    REST = tuple(k for k in BIG if k not in FFN)
    held = {}

    def to_wire(names, layer, mine, from_sibling):
        return lax.cond(
            cc == layer,
            lambda: [_add_to_wire(a, b, "%s_l%d" % (k, layer)) for k, a, b in zip(names, mine, from_sibling)],
            lambda: [jnp.zeros(a.shape, jnp.bfloat16) for a in mine])

    def chip_sum(names, layer, got, wire):
        return lax.cond(
            cc == layer,
            lambda: [_sum_chips(r, lax.dynamic_index_in_dim(s, chip, 0, keepdims=False), "%s_l%d" % (k, layer))
                     for k, r, s in zip(names, got, wire)],
            lambda: [jnp.zeros(s.shape[1:], F32) for s in wire])

    def bwd_rides(g1):
        held["g1"] = [g1[k] for k in BIG]

        def ride_mix(from_sibling1, g0_ffn):
            held["wire1"] = to_wire(BIG, 1, held["g1"], from_sibling1)
            held["g0_ffn"] = [g0_ffn[k] for k in FFN]
            return _plan_to_sibling(held["g0_ffn"], 0, "swap_grads_l0_ffn")

        def ride_attn(from_sibling0):
            held["wire0_ffn"] = to_wire(FFN, 0, held["g0_ffn"], from_sibling0)
            return _merge_plans("scatter_grads_early", [_plan_scatter(held["wire1"], 1, "scatter_grads_l1"),
                                                        _plan_scatter(held["wire0_ffn"], 0, "scatter_grads_l0_ffn")])

        return {"ffn": _plan_to_sibling(held["g1"], 1, "swap_grads_l1"), "mix": ride_mix, "attn": ride_attn}

    loss_share, grad_x, gl, g_meta, got = _device_step(x[0], loss_target[0], small, pieces0, None, fwd_ride,
                                                       bwd_rides)

    reds1 = chip_sum(BIG, 1, got[:len(BIG)], held["wire1"])
    reds0 = dict(zip(FFN, chip_sum(FFN, 0, got[len(BIG):], held["wire0_ffn"])))
    g0_rest = [gl[0][k] for k in REST]
    wire0 = to_wire(REST, 0, g0_rest, _run_plan(_plan_to_sibling(g0_rest, 0, "swap_grads_l0_rest")))
    reds0.update(zip(REST, chip_sum(REST, 0, _run_plan(_plan_scatter(wire0, 0, "scatter_grads_l0_rest")), wire0)))
    reds0 = [reds0[k] for k in BIG]
    reds_sibling = _sibling_exchange(reds0, reds1)
    grads, delta, new_m, new_v = {}, {}, {}, {}
    for k, r0, r1, theirs in zip(BIG, reds0, reds1, reds_sibling):
        grads[k], delta[k], new_m[k], new_v[k] = _adamw_layers(w[k], m[k], v[k], r0, r1, theirs, k)

    g_small_local = {k: jnp.stack([gl[l][k] for l in range(DEPTH)]) for k in SMALL}
    g_convw_local = jnp.stack([gl[l]["conv_w"] for l in range(DEPTH)])
    reduced = _all_reduce_small(
        _pack_small(g_small_local, g_meta, g_convw_local).at[SMALL_ROWS - 1, 1023].set(loss_share), "reduce_small")
    loss = reduced[SMALL_ROWS - 1, 1023]
    g_small, g_meta_full, g_convw_full = _unpack_small(reduced)
    grads.update(g_small)
    grads["meta"] = lax.dynamic_slice(g_meta_full, (zero, chip * (D // 4)), (N_META, D // 4))
    grads["conv_w"] = lax.dynamic_slice(g_convw_full, (zero, zero, chip * (CONV_DIM // 4)),
                                        (DEPTH, CONV_K, CONV_DIM // 4))

    def small_pack(src):
        return _pack_small(src, jnp.pad(src["meta"], ((0, 0), (0, D - D // 4))),
                           jnp.pad(src["conv_w"], ((0, 0), (0, 0), (0, CONV_DIM - CONV_DIM // 4))))

    def small_unpack(buf):
        out, meta_p, convw_p = _unpack_small(buf)
        out["meta"] = meta_p[:, :D // 4]
        out["conv_w"] = convw_p[:, :, :CONV_DIM // 4]
        return out

    d_s, m_s, v_s = [small_unpack(a) for a in _adamw(small_pack(w), small_pack(grads), small_pack(m),
                                                     small_pack(v), "small")]
    delta.update(d_s)
    new_m.update(m_s)
    new_v.update(v_s)
    return (loss, grad_x[None], *[grads[k] for k in names], *[delta[k] for k in names],
            *[new_m[k] for k in names], *[new_v[k] for k in names])
```

```python
import functools

import jax
import jax.numpy as jnp
from jax import lax
from jax.experimental import pallas as pl
from jax.experimental.pallas import tpu as pltpu

F32 = jnp.float32
_MM = jnp.bfloat16

D = 1024
N_META = 16
FRONT = 48
ROW0 = FRONT + N_META
EPS = 1e-6
GATE_CLAMP = 1.0 - 1e-6
CONV_K = 31
CONV_DIM = 512
NH = 8
QK_DIM = 96
ATT_SCALE = QK_DIM ** -0.5
HH = 4
CHUNK = 64
SUB = 16
EXP_CLIP = 60.0
NEG = -1e30
LANE = 128

SEG_GATES = (0, 3072)
SEG_AG = (3072, 4096)
SEG_H4 = (4096, 6144)
SEG_CQ = (6144, 6400)
SEG_CKV = (6400, 6528)
SEG_KR = (6528, 6656)
N_IN_P = 6656

ADAM_LR = 0.001
ADAM_B1 = 0.9
ADAM_B2 = 0.999
ADAM_EPS = 1e-08
ADAM_WD = 0.01
ADAM_STEP = 10

VMEM_LIMIT = 56 * 1024 * 1024


def _tile(n, pref):
    best = 64
    for t in range(64, pref + 1, 64):
        if n % t == 0:
            best = t
    return best


def _cp(*sem):
    return pltpu.CompilerParams(dimension_semantics=tuple(sem), vmem_limit_bytes=VMEM_LIMIT)


def _row(tm, n, col=0):
    return pl.BlockSpec((tm, n), lambda i: (i, col))


def _full(shape):
    return pl.BlockSpec(shape, lambda i: (0,) * len(shape))


def _mm(a, b):
    return jnp.dot(a.astype(_MM), b.astype(_MM), preferred_element_type=F32)


def _mm_nt(a, b):
    return lax.dot_general(a.astype(_MM), b.astype(_MM), (((1,), (1,)), ((), ())), preferred_element_type=F32)


def _mm_tn(a, b):
    return lax.dot_general(a.astype(_MM), b.astype(_MM), (((0,), (0,)), ((), ())), preferred_element_type=F32)


def _split3(x):
    hi = x.astype(jnp.bfloat16)
    return hi, (x - hi.astype(F32)).astype(jnp.bfloat16)


def _dot3(a, b, dims):
    ah, al = _split3(a)
    bh, bl = _split3(b)
    dg = lambda u, v: lax.dot_general(u, v, (dims, ((), ())), preferred_element_type=F32)
    return dg(ah, bh) + (dg(ah, bl) + dg(al, bh))


def _hmm(a, b):
    return _dot3(a, b, ((1,), (0,)))


def _hmm_nt(a, b):
    return _dot3(a, b, ((1,), (1,)))


def _hmm_tn(a, b):
    return _dot3(a, b, ((0,), (0,)))


def _sigmoid(x):
    return 1.0 / (1.0 + jnp.exp(-x))


def _rstd(x, n=None):
    n = x.shape[-1] if n is None else n
    return lax.rsqrt(jnp.sum(x * x, axis=-1, keepdims=True) * (1.0 / n) + EPS)


def _rms_bwd(dy, x, rstd, g, n=None):
    n = x.shape[-1] if n is None else n
    xh = x * rstd
    dxh = dy * g
    dx = rstd * (dxh - xh * (jnp.sum(dxh * xh, axis=-1, keepdims=True) * (1.0 / n)))
    return dx, dy * xh


def _valid_rows(i, tm, t_valid_end):
    r = i * tm + lax.broadcasted_iota(jnp.int32, (tm, 1), 0)
    return ((r >= FRONT) & (r < t_valid_end)).astype(F32)


def _colsum8(x):
    n, c = x.shape
    return jnp.sum(x.reshape(n // 8, 8, c), axis=0)


def _in_proj_fwd(x, g1, w):
    t = x.shape[0]
    tm = _tile(t, 192)
    segs = (SEG_GATES, SEG_AG, SEG_H4, SEG_CQ, SEG_CKV, SEG_KR)

    def body(x_ref, g_ref, w_ref, gates_ref, ag_ref, h4_ref, cq_ref, ckv_ref, kr_ref, hb_ref):
        xv = x_ref[...]
        hb = (xv * _rstd(xv) * g_ref[...]).astype(_MM)
        hb_ref[...] = hb
        for ref, (a, b) in zip((gates_ref, ag_ref, h4_ref, cq_ref, ckv_ref, kr_ref), segs):
            ref[...] = jnp.dot(hb, w_ref[:, a:b], preferred_element_type=F32)

    outs = [jax.ShapeDtypeStruct((t, b - a), F32) for a, b in segs] + [jax.ShapeDtypeStruct((t, D), _MM)]
    return pl.pallas_call(
        body, name="in_proj_fwd", grid=(t // tm,),
        in_specs=[_row(tm, D), _full((1, D)), _full((D, N_IN_P))],
        out_specs=[_row(tm, b - a) for a, b in segs] + [_row(tm, D)],
        out_shape=outs, compiler_params=_cp("parallel"),
    )(x, g1, w)


def _in_proj_bwd(du, x, dx1, g1, wt, t_end):
    t = x.shape[0]
    tm = _tile(t, 192)

    def body(du_ref, x_ref, dx1_ref, g_ref, wt_ref, dx_ref, dg_ref):
        i = pl.program_id(0)
        dh = jnp.dot(du_ref[...], wt_ref[...], preferred_element_type=F32)
        xv = x_ref[...]
        dxn, dgrow = _rms_bwd(dh, xv, _rstd(xv), g_ref[...])
        dx_ref[...] = _valid_rows(i, tm, t_end) * (dx1_ref[...] + dxn)

        @pl.when(i == 0)
        def _():
            dg_ref[...] = jnp.zeros_like(dg_ref)
        dg_ref[...] += _colsum8(dgrow)

    return pl.pallas_call(
        body, name="in_proj_bwd", grid=(t // tm,),
        in_specs=[_row(tm, N_IN_P), _row(tm, D), _row(tm, D), _full((1, D)), _full((N_IN_P, D))],
        out_specs=[_row(tm, D), _full((8, D))],
        out_shape=[jax.ShapeDtypeStruct((t, D), F32), jax.ShapeDtypeStruct((8, D), F32)],
        compiler_params=_cp("arbitrary"),
    )(du, x, dx1, g1, wt)


CONV_CH = 128


def _conv_fwd(ag, cw, cb):
    t = ag.shape[0]
    n = t // CONV_CH

    def body(a_ref, g_ref, w_ref, b_ref, z_ref, hp):
        hp[0:32, :] = jnp.zeros((32, LANE), F32)

        def fill(i, c):
            r = pl.multiple_of(i * CONV_CH, CONV_CH)
            hp[pl.ds(32 + r, CONV_CH), :] = a_ref[pl.ds(r, CONV_CH), :] * _sigmoid(g_ref[pl.ds(r, CONV_CH), :])
            return c
        lax.fori_loop(0, n, fill, 0)

        def conv(i, c):
            r = pl.multiple_of(i * CONV_CH, CONV_CH)
            acc = jnp.broadcast_to(b_ref[...], (CONV_CH, LANE))
            for k in range(CONV_K):
                acc = acc + w_ref[k:k + 1, :] * hp[pl.ds(r + (k + 2), CONV_CH), :]
            z_ref[pl.ds(r, CONV_CH), :] = acc
            return c
        lax.fori_loop(0, n, conv, 0)

    nb = CONV_DIM // LANE
    return pl.pallas_call(
        body, name="conv_fwd", grid=(nb,),
        in_specs=[pl.BlockSpec((t, LANE), lambda j: (0, j)), pl.BlockSpec((t, LANE), lambda j: (0, nb + j)),
                  pl.BlockSpec((32, LANE), lambda j: (0, j)), pl.BlockSpec((1, LANE), lambda j: (0, j))],
        out_specs=pl.BlockSpec((t, LANE), lambda j: (0, j)),
        out_shape=jax.ShapeDtypeStruct((t, CONV_DIM), F32),
        scratch_shapes=[pltpu.VMEM((t + 32, LANE), F32)],
        compiler_params=_cp("parallel"),
    )(ag, ag, cw, cb)


def _conv_bwd(ag, cw, dz):
    t = ag.shape[0]
    n = t // CONV_CH

    def body(a_ref, g_ref, w_ref, dz_ref, da_ref, dg_ref, dcw_ref, hp, dzp, accw):
        hp[0:32, :] = jnp.zeros((32, LANE), F32)
        dzp[pl.ds(t, 32), :] = jnp.zeros((32, LANE), F32)
        accw[...] = jnp.zeros_like(accw)

        def fill(i, c):
            r = pl.multiple_of(i * CONV_CH, CONV_CH)
            hp[pl.ds(32 + r, CONV_CH), :] = a_ref[pl.ds(r, CONV_CH), :] * _sigmoid(g_ref[pl.ds(r, CONV_CH), :])
            dzp[pl.ds(r, CONV_CH), :] = dz_ref[pl.ds(r, CONV_CH), :]
            return c
        lax.fori_loop(0, n, fill, 0)

        def step(i, c):
            r = pl.multiple_of(i * CONV_CH, CONV_CH)
            dzc = dz_ref[pl.ds(r, CONV_CH), :]
            dh = jnp.zeros((CONV_CH, LANE), F32)
            for k in range(CONV_K):
                dh = dh + w_ref[k:k + 1, :] * dzp[pl.ds(r + (CONV_K - 1 - k), CONV_CH), :]
                accw[8 * k:8 * k + 8, :] += _colsum8(dzc * hp[pl.ds(r + (k + 2), CONV_CH), :])
            a = a_ref[pl.ds(r, CONV_CH), :]
            sg = _sigmoid(g_ref[pl.ds(r, CONV_CH), :])
            da_ref[pl.ds(r, CONV_CH), :] = dh * sg
            dg_ref[pl.ds(r, CONV_CH), :] = dh * a * sg * (1.0 - sg)
            return c
        lax.fori_loop(0, n, step, 0)

        for k in range(CONV_K):
            dcw_ref[k:k + 1, :] = jnp.sum(accw[8 * k:8 * k + 8, :], axis=0, keepdims=True)
        dcw_ref[CONV_K:32, :] = jnp.zeros((32 - CONV_K, LANE), F32)

    nb = CONV_DIM // LANE
    colspec = pl.BlockSpec((t, LANE), lambda j: (0, j))
    return pl.pallas_call(
        body, name="conv_bwd", grid=(nb,),
        in_specs=[colspec, pl.BlockSpec((t, LANE), lambda j: (0, nb + j)),
                  pl.BlockSpec((32, LANE), lambda j: (0, j)), colspec],
        out_specs=[colspec, colspec, pl.BlockSpec((32, LANE), lambda j: (0, j))],
        out_shape=[jax.ShapeDtypeStruct((t, CONV_DIM), F32), jax.ShapeDtypeStruct((t, CONV_DIM), F32),
                   jax.ShapeDtypeStruct((32, CONV_DIM), F32)],
        scratch_shapes=[pltpu.VMEM((t + 32, LANE), F32), pltpu.VMEM((t + 32, LANE), F32),
                        pltpu.VMEM((8 * 32, LANE), F32)],
        compiler_params=_cp("parallel"),
    )(ag, ag, cw, dz)


def _rope(x, c, s1, s2):
    return x * c + pltpu.roll(x, LANE - 16, 1) * s1 + pltpu.roll(x, 16, 1) * s2


def _rope_t(dy, c, s1, s2):
    return dy * c + pltpu.roll(dy * s1, 16, 1) + pltpu.roll(dy * s2, LANE - 16, 1)


def _mla_pre_fwd(cq, ckv, kr, qag, wuq, kvag, wk, wv, qng, kng, rc, rs1, rs2):
    t = cq.shape[0]
    tm = _tile(t, 384)

    def body(cq_ref, ckv_ref, kr_ref, qag_ref, wuq_ref, kvag_ref, wk_ref, wv_ref, qng_ref, kng_ref,
             c_ref, s1_ref, s2_ref, q_ref, k_ref, v_ref, cqn_ref, ckvn_ref):
        cqv = cq_ref[...]
        cqn = (cqv * _rstd(cqv) * qag_ref[...]).astype(_MM)
        cqn_ref[...] = cqn
        ckvv = ckv_ref[...]
        ckvn = (ckvv * _rstd(ckvv) * kvag_ref[...]).astype(_MM)
        ckvn_ref[...] = ckvn
        qraw = jnp.dot(cqn, wuq_ref[...], preferred_element_type=F32)
        kraw = jnp.dot(ckvn, wk_ref[...], preferred_element_type=F32)
        v_ref[...] = jnp.dot(ckvn, wv_ref[...], preferred_element_type=F32).astype(_MM)
        krv = kr_ref[...]
        c, s1, s2 = c_ref[...], s1_ref[...], s2_ref[...]
        for h in range(NH):
            sl = slice(LANE * h, LANE * (h + 1))
            qh = qraw[:, sl]
            qn = qh * _rstd(qh, QK_DIM) * qng_ref[...]
            q_ref[:, sl] = (_rope(qn, c, s1, s2) * ATT_SCALE).astype(_MM)
            kh = kraw[:, sl] + krv
            kn = kh * _rstd(kh, QK_DIM) * kng_ref[...]
            k_ref[:, sl] = _rope(kn, c, s1, s2).astype(_MM)

    hd = NH * LANE
    return pl.pallas_call(
        body, name="mla_pre_fwd", grid=(t // tm,),
        in_specs=[_row(tm, 256), _row(tm, 128), _row(tm, 128), _full((1, 256)), _full((256, hd)),
                  _full((1, 128)), _full((128, hd)), _full((128, hd)), _full((1, LANE)), _full((1, LANE)),
                  _row(tm, LANE), _row(tm, LANE), _row(tm, LANE)],
        out_specs=[_row(tm, hd), _row(tm, hd), _row(tm, hd), _row(tm, 256), _row(tm, 128)],
        out_shape=[jax.ShapeDtypeStruct((t, hd), _MM)] * 3 + [jax.ShapeDtypeStruct((t, 256), _MM),
                                                              jax.ShapeDtypeStruct((t, 128), _MM)],
        compiler_params=_cp("parallel"),
    )(cq, ckv, kr, qag, wuq, kvag, wk, wv, qng, kng, rc, rs1, rs2)


def _mla_pre_bwd(dq, dk, dv, cq, ckv, kr, qag, wuq, kvag, wk, wv, qng, kng, rc, rs1, rs2):
    t = cq.shape[0]
    tm = _tile(t, 192)
    hd = NH * LANE

    def body(dq_ref, dk_ref, dv_ref, cq_ref, ckv_ref, kr_ref, qag_ref, wuq_ref, kvag_ref, wk_ref,
             wv_ref, qng_ref, kng_ref, c_ref, s1_ref, s2_ref,
             dcq_ref, dckv_ref, dkr_ref, dqraw_ref, dkraw_ref, dqag_ref, dkvag_ref, dqng_ref, dkng_ref):
        i = pl.program_id(0)
        cqv = cq_ref[...]
        rq_in = _rstd(cqv)
        cqn = (cqv * rq_in * qag_ref[...]).astype(_MM)
        ckvv = ckv_ref[...]
        rkv_in = _rstd(ckvv)
        ckvn = (ckvv * rkv_in * kvag_ref[...]).astype(_MM)
        qraw = jnp.dot(cqn, wuq_ref[...], preferred_element_type=F32)
        kraw = jnp.dot(ckvn, wk_ref[...], preferred_element_type=F32)
        krv = kr_ref[...]
        c, s1, s2 = c_ref[...], s1_ref[...], s2_ref[...]
        dkr = jnp.zeros((tm, LANE), F32)
        dqng = jnp.zeros((8, LANE), F32)
        dkng = jnp.zeros((8, LANE), F32)
        for h in range(NH):
            sl = slice(LANE * h, LANE * (h + 1))
            qh = qraw[:, sl]
            dqn = _rope_t(dq_ref[:, sl] * ATT_SCALE, c, s1, s2)
            dqh, gq = _rms_bwd(dqn, qh, _rstd(qh, QK_DIM), qng_ref[...], QK_DIM)
            dqraw_ref[:, sl] = dqh.astype(_MM)
            dqng = dqng + _colsum8(gq)
            kh = kraw[:, sl] + krv
            dkn = _rope_t(dk_ref[:, sl], c, s1, s2)
            dkh, gk = _rms_bwd(dkn, kh, _rstd(kh, QK_DIM), kng_ref[...], QK_DIM)
            dkraw_ref[:, sl] = dkh.astype(_MM)
            dkr = dkr + dkh
            dkng = dkng + _colsum8(gk)
        dkr_ref[...] = dkr.astype(_MM)
        dcqn = _mm_nt(dqraw_ref[...], wuq_ref[...])
        dcq, gqa = _rms_bwd(dcqn, cqv, rq_in, qag_ref[...])
        dcq_ref[...] = dcq.astype(_MM)
        dckvn = _mm_nt(dkraw_ref[...], wk_ref[...]) + _mm_nt(dv_ref[...], wv_ref[...])
        dckv, gkva = _rms_bwd(dckvn, ckvv, rkv_in, kvag_ref[...])
        dckv_ref[...] = dckv.astype(_MM)

        @pl.when(i == 0)
        def _():
            dqag_ref[...] = jnp.zeros_like(dqag_ref)
            dkvag_ref[...] = jnp.zeros_like(dkvag_ref)
            dqng_ref[...] = jnp.zeros_like(dqng_ref)
            dkng_ref[...] = jnp.zeros_like(dkng_ref)
        dqag_ref[...] += _colsum8(gqa)
        dkvag_ref[...] += _colsum8(gkva)
        dqng_ref[...] += dqng
        dkng_ref[...] += dkng

    return pl.pallas_call(
        body, name="mla_pre_bwd", grid=(t // tm,),
        in_specs=[_row(tm, hd), _row(tm, hd), _row(tm, hd), _row(tm, 256), _row(tm, 128), _row(tm, 128),
                  _full((1, 256)), _full((256, hd)), _full((1, 128)), _full((128, hd)),
                  _full((128, hd)), _full((1, LANE)), _full((1, LANE)),
                  _row(tm, LANE), _row(tm, LANE), _row(tm, LANE)],
        out_specs=[_row(tm, 256), _row(tm, 128), _row(tm, 128), _row(tm, hd), _row(tm, hd),
                   _full((8, 256)), _full((8, 128)), _full((8, LANE)), _full((8, LANE))],
        out_shape=[jax.ShapeDtypeStruct((t, 256), _MM), jax.ShapeDtypeStruct((t, 128), _MM),
                   jax.ShapeDtypeStruct((t, 128), _MM), jax.ShapeDtypeStruct((t, hd), _MM),
                   jax.ShapeDtypeStruct((t, hd), _MM), jax.ShapeDtypeStruct((8, 256), F32),
                   jax.ShapeDtypeStruct((8, 128), F32), jax.ShapeDtypeStruct((8, LANE), F32),
                   jax.ShapeDtypeStruct((8, LANE), F32)],
        compiler_params=_cp("arbitrary"),
    )(dq, dk, dv, cq, ckv, kr, qag, wuq, kvag, wk, wv, qng, kng, rc, rs1, rs2)


ATT_TILE = 704


def _attn_mask(r0, c0, tq):
    rows = r0 + lax.broadcasted_iota(jnp.int32, (tq, 1), 0)
    cols = c0 + lax.broadcasted_iota(jnp.int32, (1, tq), 1)
    return (cols <= rows) & (cols >= FRONT)


def _attn_fwd(q, k, v, plan=None):
    t = q.shape[0]
    tq = _tile(t, ATT_TILE)
    nq = t // tq
    p_args, p_in, p_out, p_shape, p_sem = _plan_specs(plan)

    def body(*refs):
        ((q_ref, k_ref, v_ref), (o_ref, lse_ref), _), rider = _host_refs(refs, 3, 2, 0, plan)
        done = _ride(plan, rider, pl.program_id(0), NH - 1)

        def qloop(qi, carry):
            r0 = pl.multiple_of(qi * tq, tq)
            qb = q_ref[pl.ds(r0, tq), :]

            def kstep(kj, st, masked):
                m, l, acc = st
                c0 = pl.multiple_of(kj * tq, tq)
                s = _mm_nt(qb, k_ref[pl.ds(c0, tq), :])
                if masked:
                    s = jnp.where(_attn_mask(r0, c0, tq), s, NEG)
                m2 = jnp.maximum(m, jnp.max(s, axis=-1, keepdims=True))
                p = jnp.exp(s - m2)
                a = jnp.exp(m - m2)
                l = a * l + jnp.sum(p, axis=-1, keepdims=True)
                acc = a * acc + _mm(p, v_ref[pl.ds(c0, tq), :])
                return m2, l, acc

            st = kstep(0, (jnp.full((tq, 1), NEG, F32), jnp.zeros((tq, 1), F32), jnp.zeros((tq, LANE), F32)), True)
            st = lax.fori_loop(1, qi, lambda kj, s_: kstep(kj, s_, False), st)
            m, l, acc = lax.cond(qi > 0, lambda s_: kstep(qi, s_, True), lambda s_: s_, st)
            o_ref[pl.ds(r0, tq), :] = acc / l
            lse_ref[pl.ds(r0, tq), :] = m + jnp.log(l)
            return carry
        lax.fori_loop(0, nq, qloop, 0)
        done()

    hs = pl.BlockSpec((t, LANE), lambda h: (0, h))
    res = pl.pallas_call(
        body, name="attn_fwd", grid=(NH,),
        in_specs=[hs, hs, hs] + p_in,
        out_specs=[hs, pl.BlockSpec((None, t, 1), lambda h: (h, 0, 0))] + p_out,
        out_shape=[jax.ShapeDtypeStruct((t, NH * LANE), F32), jax.ShapeDtypeStruct((NH, t, 1), F32)] + p_shape,
        scratch_shapes=p_sem,
        compiler_params=_cp("parallel" if plan is None else "arbitrary"),
    )(q, k, v, *p_args)
    return res[:2], res[2:]


def _attn_bwd(q, k, v, o, lse, do, plan=None):
    t = q.shape[0]
    tq = _tile(t, ATT_TILE)
    nq = t // tq
    p_args, p_in, p_out, p_shape, p_sem = _plan_specs(plan)

    def body(*refs):
        (ins, (dq_ref, dk_ref, dv_ref), (delta,)), rider = _host_refs(refs, 6, 3, 1, plan)
        q_ref, k_ref, v_ref, o_ref, lse_ref, do_ref = ins
        done = _ride(plan, rider, pl.program_id(0), NH - 1)

        def prep(i, c):
            r0 = pl.multiple_of(i * tq, tq)
            delta[pl.ds(r0, tq), :] = jnp.sum(do_ref[pl.ds(r0, tq), :] * o_ref[pl.ds(r0, tq), :], axis=-1,
                                              keepdims=True)
            dq_ref[pl.ds(r0, tq), :] = jnp.zeros((tq, LANE), F32)
            return c
        lax.fori_loop(0, nq, prep, 0)

        def kloop(kj, carry):
            c0 = pl.multiple_of(kj * tq, tq)
            kb = k_ref[pl.ds(c0, tq), :]
            vb = v_ref[pl.ds(c0, tq), :]

            def qstep(qi, st, masked):
                dkb, dvb = st
                r0 = pl.multiple_of(qi * tq, tq)
                qb = q_ref[pl.ds(r0, tq), :]
                dob = do_ref[pl.ds(r0, tq), :].astype(_MM)
                s = _mm_nt(qb, kb)
                if masked:
                    s = jnp.where(_attn_mask(r0, c0, tq), s, NEG)
                p = jnp.exp(s - lse_ref[pl.ds(r0, tq), :])
                dvb = dvb + _mm_tn(p, dob)
                dp = _mm_nt(dob, vb)
                ds = (p * (dp - delta[pl.ds(r0, tq), :])).astype(_MM)
                dkb = dkb + _mm_tn(ds, qb)
                dq_ref[pl.ds(r0, tq), :] += _mm(ds, kb)
                return dkb, dvb

            st = qstep(kj, (jnp.zeros((tq, LANE), F32), jnp.zeros((tq, LANE), F32)), True)
            dkb, dvb = lax.cond(
                kj == 0,
                lambda s_: lax.fori_loop(kj + 1, nq, lambda qi, t_: qstep(qi, t_, True), s_),
                lambda s_: lax.fori_loop(kj + 1, nq, lambda qi, t_: qstep(qi, t_, False), s_), st)
            dk_ref[pl.ds(c0, tq), :] = dkb
            dv_ref[pl.ds(c0, tq), :] = dvb
            return carry
        lax.fori_loop(0, nq, kloop, 0)
        done()

    hs = pl.BlockSpec((t, LANE), lambda h: (0, h))
    res = pl.pallas_call(
        body, name="attn_bwd", grid=(NH,),
        in_specs=[hs, hs, hs, hs, pl.BlockSpec((None, t, 1), lambda h: (h, 0, 0)), hs] + p_in,
        out_specs=[hs, hs, hs] + p_out,
        out_shape=[jax.ShapeDtypeStruct((t, NH * LANE), F32)] * 3 + p_shape,
        scratch_shapes=[pltpu.VMEM((t, 1), F32)] + p_sem,
        compiler_params=_cp("parallel" if plan is None else "arbitrary"),
    )(q, k, v, o, lse, do, *p_args)
    return res[:3], res[3:]


def _cumsum_rows(x):
    n = x.shape[0]
    rows = lax.broadcasted_iota(jnp.int32, (n, 1), 0)
    d = 1
    while d < n:
        x = x + jnp.where(rows >= d, pltpu.roll(x, d, 0), 0.0)
        d *= 2
    return x


def _revcumsum_rows(x):
    n = x.shape[0]
    rows = lax.broadcasted_iota(jnp.int32, (n, 1), 0)
    d = 1
    while d < n:
        x = x + jnp.where(rows < n - d, pltpu.roll(x, n - d, 0), 0.0)
        d *= 2
    return x


def _hgrn_gates(f, lb):
    sneg = _sigmoid(-f)
    kk = (1.0 - lb) * sneg
    lf = jnp.log1p(-jnp.minimum(kk, GATE_CLAMP))
    return kk, lf, sneg


def _silu(x):
    return x * _sigmoid(x)


def _dsilu(x):
    s = _sigmoid(x)
    return s * (1.0 + x * (1.0 - s))


def _hgrn_intra(q, kk, b):
    parts = []
    for blk in range(CHUNK // SUB):
        lo = blk * SUB
        ref = jnp.zeros((1, LANE), F32) if blk == 0 else b[lo - 1:lo, :]
        eq = jnp.exp(b[lo:lo + SUB, :] - ref)
        ek = jnp.exp(jnp.minimum(ref - b, EXP_CLIP))
        parts.append((q[lo:lo + SUB, :] * eq, kk * ek, eq, ek))
    return parts


def _chunk_causal():
    return lax.broadcasted_iota(jnp.int32, (CHUNK, CHUNK), 1) <= lax.broadcasted_iota(jnp.int32, (CHUNK, CHUNK), 0)


def _hgrn_fwd(h4, lb):
    t = h4.shape[0]
    nc = t // CHUNK

    def body(q_ref, f_ref, i_ref, lb_ref, o_ref, s_ref, st):
        st[...] = jnp.zeros_like(st)
        causal = _chunk_causal()

        def chunk(c, carry):
            r0 = pl.multiple_of(c * CHUNK, CHUNK)
            q = q_ref[pl.ds(r0, CHUNK), :]
            kk, lf, _ = _hgrn_gates(f_ref[pl.ds(r0, CHUNK), :], lb_ref[...])
            v = _silu(i_ref[pl.ds(r0, CHUNK), :])
            b = _cumsum_rows(lf)
            s_prev = st[...]
            s_ref[c] = s_prev
            o = _hmm_nt(q * jnp.exp(b), s_prev)
            a = jnp.concatenate([_hmm_nt(qs, ks) for qs, ks, _, _ in _hgrn_intra(q, kk, b)], axis=0)
            a = jnp.where(causal, a, 0.0)
            o_ref[pl.ds(r0, CHUNK), :] = o + _hmm(a, v)
            bl = b[CHUNK - 1:CHUNK, :]
            st[...] = s_prev * jnp.exp(bl) + _hmm_tn(v, kk * jnp.exp(bl - b))
            return carry
        lax.fori_loop(0, nc, chunk, 0, unroll=2)

    def col(j):
        return pl.BlockSpec((t, LANE), lambda h: (0, HH * j + h))
    return pl.pallas_call(
        body, name="hgrn_fwd", grid=(HH,),
        in_specs=[col(0), col(1), col(2), pl.BlockSpec((1, LANE), lambda h: (0, h))],
        out_specs=[pl.BlockSpec((t, LANE), lambda h: (0, h)),
                   pl.BlockSpec((None, nc, LANE, LANE), lambda h: (h, 0, 0, 0))],
        out_shape=[jax.ShapeDtypeStruct((t, HH * LANE), F32), jax.ShapeDtypeStruct((HH, nc, LANE, LANE), F32)],
        scratch_shapes=[pltpu.VMEM((LANE, LANE), F32)],
        compiler_params=_cp("parallel"),
    )(h4, h4, h4, lb)


def _hgrn_bwd(h4, lb, do, states):
    t = h4.shape[0]
    nc = t // CHUNK

    def body(q_ref, f_ref, i_ref, lb_ref, do_ref, s_ref, dq_ref, df_ref, di_ref, dlb_ref, dst, carry):
        dst[...] = jnp.zeros_like(dst)
        carry[...] = jnp.zeros_like(carry)
        dlb_ref[...] = jnp.zeros_like(dlb_ref)
        causal = _chunk_causal()

        def chunk(cc, cr):
            c = nc - 1 - cc
            r0 = pl.multiple_of(c * CHUNK, CHUNK)
            q = q_ref[pl.ds(r0, CHUNK), :]
            lbv = lb_ref[...]
            kk, lf, sneg = _hgrn_gates(f_ref[pl.ds(r0, CHUNK), :], lbv)
            iv = i_ref[pl.ds(r0, CHUNK), :]
            v = _silu(iv)
            b = _cumsum_rows(lf)
            s_prev = s_ref[c]
            ds_new = dst[...]
            dob = do_ref[pl.ds(r0, CHUNK), :]
            e = jnp.exp(b)
            qe = q * e
            bl = b[CHUNK - 1:CHUNK, :]
            etail = jnp.exp(bl - b)
            kd = kk * etail
            dq_inter = _hmm(dob, s_prev) * e
            dv = _hmm_nt(kd, ds_new)
            dkk = _hmm(v, ds_new) * etail
            parts = _hgrn_intra(q, kk, b)
            a = jnp.where(causal, jnp.concatenate([_hmm_nt(qs, ks) for qs, ks, _, _ in parts], axis=0), 0.0)
            da = jnp.where(causal, _hmm_nt(dob, v), 0.0)
            dv = dv + _hmm_tn(a, dob)
            dq_rows = []
            for blk, (qs, ks, eq, ek) in enumerate(parts):
                da_blk = da[blk * SUB:(blk + 1) * SUB, :]
                dq_rows.append(_hmm(da_blk, ks) * eq)
                dkk = dkk + _hmm_tn(da_blk, qs) * ek
            dq = dq_inter + jnp.concatenate(dq_rows, axis=0)
            dst[...] = ds_new * jnp.exp(bl) + _hmm_tn(dob, qe)
            g = q * dq - kk * dkk
            dlf = _revcumsum_rows(g) + carry[0:1, :]
            carry[0:1, :] += jnp.sum(g, axis=0, keepdims=True)
            dkk_tot = dkk + dlf * jnp.where(kk < GATE_CLAMP, -1.0 / (1.0 - kk), 0.0)
            dq_ref[pl.ds(r0, CHUNK), :] = dq
            df_ref[pl.ds(r0, CHUNK), :] = dkk_tot * (1.0 - lbv) * (-sneg * (1.0 - sneg))
            di_ref[pl.ds(r0, CHUNK), :] = dv * _dsilu(iv)
            dlb_ref[...] += _colsum8(dkk_tot * (-sneg))
            return cr
        lax.fori_loop(0, nc, chunk, 0, unroll=2)

    def col(j):
        return pl.BlockSpec((t, LANE), lambda h: (0, HH * j + h))
    hs = pl.BlockSpec((t, LANE), lambda h: (0, h))
    return pl.pallas_call(
        body, name="hgrn_bwd", grid=(HH,),
        in_specs=[col(0), col(1), col(2), pl.BlockSpec((1, LANE), lambda h: (0, h)), hs,
                  pl.BlockSpec((None, nc, LANE, LANE), lambda h: (h, 0, 0, 0))],
        out_specs=[hs, hs, hs, pl.BlockSpec((8, LANE), lambda h: (0, h))],
        out_shape=[jax.ShapeDtypeStruct((t, HH * LANE), F32)] * 3 + [jax.ShapeDtypeStruct((8, HH * LANE), F32)],
        scratch_shapes=[pltpu.VMEM((LANE, LANE), F32), pltpu.VMEM((8, LANE), F32)],
        compiler_params=_cp("parallel"),
    )(h4, h4, h4, lb, do, states)


def _ln_fwd(z, g, b):
    mu = jnp.mean(z, axis=-1, keepdims=True)
    zc = z - mu
    rstd = lax.rsqrt(jnp.mean(zc * zc, axis=-1, keepdims=True) + EPS)
    zh = zc * rstd
    return zh * g + b, zh, rstd


def _mix_fwd(x, z, o_att, o_h, h4, gates, lng, lnb, wco, wao, ng, who, wout, t_end):
    t = x.shape[0]
    tm = _tile(t, 192)

    def body(x_ref, z_ref, oa_ref, oh_ref, hg_ref, gt_ref, lng_ref, lnb_ref, wco_ref, wao_ref, ng_ref, who_ref,
             wout_ref, x1_ref, mix_ref, ca_ref, oc_ref, ya_ref, yb_ref, yc_ref):
        i = pl.program_id(0)
        ln, _, _ = _ln_fwd(z_ref[...], lng_ref[...], lnb_ref[...])
        ca = _silu(ln).astype(_MM)
        ca_ref[...] = ca
        ya = jnp.dot(ca, wco_ref[...], preferred_element_type=F32)
        yb = _mm(oa_ref[...], wao_ref[...])
        hg = hg_ref[...]
        for h in range(HH):
            sl = slice(LANE * h, LANE * (h + 1))
            oh = oh_ref[:, sl]
            oc_ref[:, sl] = (oh * _rstd(oh) * ng_ref[:, sl] * _silu(hg[:, sl])).astype(_MM)
        yc = jnp.dot(oc_ref[...], who_ref[...], preferred_element_type=F32)
        ya_ref[...] = ya
        yb_ref[...] = yb
        yc_ref[...] = yc
        mix = (_sigmoid(gt_ref[:, 0:D]) * ya + _sigmoid(gt_ref[:, D:2 * D]) * yb
               + _sigmoid(gt_ref[:, 2 * D:3 * D]) * yc).astype(_MM)
        mix_ref[...] = mix
        x1_ref[...] = x_ref[...] + _valid_rows(i, tm, t_end) * jnp.dot(mix, wout_ref[...],
                                                                       preferred_element_type=F32)

    hd = NH * LANE
    return pl.pallas_call(
        body, name="mix_fwd", grid=(t // tm,),
        in_specs=[_row(tm, D), _row(tm, CONV_DIM), _row(tm, hd), _row(tm, 512), _row(tm, 512, 3), _row(tm, 3 * D),
                  _full((1, 512)), _full((1, 512)), _full((512, D)), _full((hd, D)), _full((1, 512)),
                  _full((512, D)), _full((D, D))],
        out_specs=[_row(tm, D), _row(tm, D), _row(tm, 512), _row(tm, 512), _row(tm, D), _row(tm, D), _row(tm, D)],
        out_shape=[jax.ShapeDtypeStruct((t, D), F32), jax.ShapeDtypeStruct((t, D), _MM),
                   jax.ShapeDtypeStruct((t, 512), _MM), jax.ShapeDtypeStruct((t, 512), _MM),
                   jax.ShapeDtypeStruct((t, D), F32), jax.ShapeDtypeStruct((t, D), F32),
                   jax.ShapeDtypeStruct((t, D), F32)],
        compiler_params=_cp("parallel"),
    )(x, z, o_att, o_h, h4, gates, lng, lnb, wco, wao, ng, who, wout)


def _mix_bwd(dx1, ya, yb, yc, gates, z, o_h, h4, lng, lnb, ng, wout, wco, wao, who, plan=None):
    t = dx1.shape[0]
    tm = _tile(t, 192)
    hd = NH * LANE
    p_args, p_in, p_out, p_shape, p_sem = _plan_specs(plan)

    def body(*refs):
        (ins, outs, _), rider = _host_refs(refs, 15, 12, 0, plan)
        (dx1_ref, ya_ref, yb_ref, yc_ref, gt_ref, z_ref, oh_ref, hg_ref, lng_ref, lnb_ref, ng_ref,
         wout_ref, wco_ref, wao_ref, who_ref) = ins
        (dgt_ref, dya_ref, dyb_ref, dyc_ref, dz_ref, doa_ref, doh_ref, dhg_ref,
         dlng_ref, dlnb_ref, dcb_ref, dng_ref) = outs
        i = pl.program_id(0)
        done = _ride(plan, rider, i, t // tm - 1)
        dmix = _mm_nt(dx1_ref[...], wout_ref[...])
        dys = []
        for j, y_ref in enumerate((ya_ref, yb_ref, yc_ref)):
            sg = _sigmoid(gt_ref[:, j * D:(j + 1) * D])
            dgt_ref[:, j * D:(j + 1) * D] = (dmix * y_ref[...] * sg * (1.0 - sg)).astype(_MM)
            dys.append((dmix * sg).astype(_MM))
        dya_ref[...], dyb_ref[...], dyc_ref[...] = dys
        dca = _mm_nt(dys[0], wco_ref[...])
        ln, zh, rstd = _ln_fwd(z_ref[...], lng_ref[...], lnb_ref[...])
        dln = dca * _dsilu(ln)
        dzh = dln * lng_ref[...]
        dz = rstd * (dzh - jnp.mean(dzh, axis=-1, keepdims=True)
                     - zh * jnp.mean(dzh * zh, axis=-1, keepdims=True))
        dz_ref[...] = dz
        doa_ref[...] = _mm_nt(dys[1], wao_ref[...])
        doc = _mm_nt(dys[2], who_ref[...])
        hg = hg_ref[...]
        dng_rows = []
        for h in range(HH):
            sl = slice(LANE * h, LANE * (h + 1))
            oh = oh_ref[:, sl]
            r = _rstd(oh)
            don = doc[:, sl] * _silu(hg[:, sl])
            dhg_ref[:, sl] = (doc[:, sl] * oh * r * ng_ref[:, sl] * _dsilu(hg[:, sl])).astype(_MM)
            doh, gn = _rms_bwd(don, oh, r, ng_ref[:, sl])
            doh_ref[:, sl] = doh
            dng_rows.append(_colsum8(gn))

        @pl.when(i == 0)
        def _():
            dlng_ref[...] = jnp.zeros_like(dlng_ref)
            dlnb_ref[...] = jnp.zeros_like(dlnb_ref)
            dcb_ref[...] = jnp.zeros_like(dcb_ref)
            dng_ref[...] = jnp.zeros_like(dng_ref)
        dlng_ref[...] += _colsum8(dln * zh)
        dlnb_ref[...] += _colsum8(dln)
        dcb_ref[...] += _colsum8(dz)
        dng_ref[...] += jnp.concatenate(dng_rows, axis=1)
        done()

    res = pl.pallas_call(
        body, name="mix_bwd", grid=(t // tm,),
        in_specs=[_row(tm, D), _row(tm, D), _row(tm, D), _row(tm, D), _row(tm, 3 * D), _row(tm, 512), _row(tm, 512),
                  _row(tm, 512, 3), _full((1, 512)), _full((1, 512)), _full((1, 512)),
                  _full((D, D)), _full((512, D)), _full((hd, D)), _full((512, D))] + p_in,
        out_specs=[_row(tm, 3 * D), _row(tm, D), _row(tm, D), _row(tm, D), _row(tm, 512), _row(tm, hd),
                   _row(tm, 512), _row(tm, 512), _full((8, 512)), _full((8, 512)), _full((8, 512)),
                   _full((8, 512))] + p_out,
        out_shape=[jax.ShapeDtypeStruct((t, 3 * D), _MM), jax.ShapeDtypeStruct((t, D), _MM),
                   jax.ShapeDtypeStruct((t, D), _MM), jax.ShapeDtypeStruct((t, D), _MM),
                   jax.ShapeDtypeStruct((t, 512), F32), jax.ShapeDtypeStruct((t, hd), F32),
                   jax.ShapeDtypeStruct((t, 512), F32), jax.ShapeDtypeStruct((t, 512), _MM)]
        + [jax.ShapeDtypeStruct((8, 512), F32)] * 4 + p_shape,
        scratch_shapes=p_sem,
        compiler_params=_cp("arbitrary"),
    )(dx1, ya, yb, yc, gates, z, o_h, h4, lng, lnb, ng, wout, wco, wao, who, *p_args)
    return res[:12], res[12:]


D_FF = 4096


def _ffn_fwd(x1, g2, w1, w2):
    t = x1.shape[0]
    tm = _tile(t, 192)

    def body(x1_ref, g_ref, w1_ref, w2_ref, x2_ref, p_ref):
        xv = x1_ref[...]
        h2 = (xv * _rstd(xv) * g_ref[...]).astype(_MM)
        p = jnp.dot(h2, w1_ref[...], preferred_element_type=F32)
        p_ref[...] = p
        r = jnp.maximum(p, 0.0)
        x2_ref[...] = xv + jnp.dot((r * r).astype(_MM), w2_ref[...], preferred_element_type=F32)

    return pl.pallas_call(
        body, name="ffn_fwd", grid=(t // tm,),
        in_specs=[_row(tm, D), _full((1, D)), _full((D, D_FF)), _full((D_FF, D))],
        out_specs=[_row(tm, D), _row(tm, D_FF)],
        out_shape=[jax.ShapeDtypeStruct((t, D), F32), jax.ShapeDtypeStruct((t, D_FF), F32)],
        compiler_params=_cp("parallel"),
    )(x1, g2, w1, w2)


def _ffn_bwd(dx2, x1, p, g2, w1t, w2t, plan=None):
    t = x1.shape[0]
    tm = _tile(t, 192)
    p_args, p_in, p_out, p_shape, p_sem = _plan_specs(plan)

    def body(*refs):
        (ins, outs, _), rider = _host_refs(refs, 6, 5, 0, plan)
        dx2_ref, x1_ref, p_ref, g_ref, w1t_ref, w2t_ref = ins
        dx1_ref, h2_ref, act_ref, dp_ref, dg_ref = outs
        i = pl.program_id(0)
        done = _ride(plan, rider, i, t // tm - 1)
        xv = x1_ref[...]
        rstd = _rstd(xv)
        h2_ref[...] = (xv * rstd * g_ref[...]).astype(_MM)
        r = jnp.maximum(p_ref[...], 0.0)
        act_ref[...] = (r * r).astype(_MM)
        dx2 = dx2_ref[...]
        da = _mm(dx2, w2t_ref[...])
        dp = (2.0 * r * da).astype(_MM)
        dp_ref[...] = dp
        dh2 = jnp.dot(dp, w1t_ref[...], preferred_element_type=F32)
        dxn, dgrow = _rms_bwd(dh2, xv, rstd, g_ref[...])
        dx1_ref[...] = dx2 + dxn

        @pl.when(i == 0)
        def _():
            dg_ref[...] = jnp.zeros_like(dg_ref)
        dg_ref[...] += _colsum8(dgrow)
        done()

    res = pl.pallas_call(
        body, name="ffn_bwd", grid=(t // tm,),
        in_specs=[_row(tm, D), _row(tm, D), _row(tm, D_FF), _full((1, D)), _full((D_FF, D)),
                  _full((D, D_FF))] + p_in,
        out_specs=[_row(tm, D), _row(tm, D), _row(tm, D_FF), _row(tm, D_FF), _full((8, D))] + p_out,
        out_shape=[jax.ShapeDtypeStruct((t, D), F32), jax.ShapeDtypeStruct((t, D), _MM),
                   jax.ShapeDtypeStruct((t, D_FF), _MM), jax.ShapeDtypeStruct((t, D_FF), _MM),
                   jax.ShapeDtypeStruct((8, D), F32)] + p_shape,
        scratch_shapes=p_sem,
        compiler_params=_cp("arbitrary"),
    )(dx2, x1, p, g2, w1t, w2t, *p_args)
    return res[:5], res[5:]


WGRAD_VMEM = 40 * 1024 * 1024


def _wgrad(a, b, name, chips=1):
    t, ka = a.shape
    nb = b.shape[1]
    cs = nb // chips
    widths = [d for d in range(cs, 0, -LANE) if cs % d == 0 and d % LANE == 0] or [cs]
    tn, tm = widths[-1], 64
    for d in widths:
        room = WGRAD_VMEM - 2 * ka * d * 4
        row_bytes = 2 * (ka * a.dtype.itemsize + d * b.dtype.itemsize) + 4 * ka
        fit = [r for r in range(64, t + 1, 64) if t % r == 0 and r * row_bytes <= room]
        if ka * d * 4 <= 16 * 1024 * 1024 and fit and (max(fit) >= 384 or d == widths[-1]):
            tn, tm = d, max(fit)
            break
    per = cs // tn

    def body(a_ref, b_ref, o_ref):
        @pl.when(pl.program_id(1) == 0)
        def _():
            o_ref[...] = jnp.zeros_like(o_ref)
        o_ref[...] += _mm_tn(a_ref[...], b_ref[...])

    if chips == 1:
        out_spec = pl.BlockSpec((ka, tn), lambda n, i: (0, n))
        out_shape = jax.ShapeDtypeStruct((ka, nb), F32)
    else:
        out_spec = pl.BlockSpec((None, ka, tn), lambda n, i: (n // per, 0, n % per))
        out_shape = jax.ShapeDtypeStruct((chips, ka, cs), F32)
    return pl.pallas_call(
        body, name="wgrad_" + name, grid=(nb // tn, t // tm),
        in_specs=[pl.BlockSpec((tm, ka), lambda n, i: (i, 0)), pl.BlockSpec((tm, tn), lambda n, i: (i, n))],
        out_specs=out_spec, out_shape=out_shape,
        compiler_params=_cp("parallel", "arbitrary"),
    )(a, b)


def _loss_head(y, target, t_end):
    t = y.shape[0]
    tm = _tile(t, 384)

    def body(y_ref, tg_ref, dy_ref, l_ref):
        i = pl.program_id(0)
        r = i * tm + lax.broadcasted_iota(jnp.int32, (tm, 1), 0)
        real = ((r >= ROW0) & (r < t_end)).astype(F32)
        diff = (y_ref[...] - tg_ref[...]) * real
        dy_ref[...] = diff * (1.0 / D)

        @pl.when(i == 0)
        def _():
            l_ref[...] = jnp.zeros_like(l_ref)
        sq = _colsum8(diff * diff)
        part = sq[:, 0:LANE]
        for j in range(1, D // LANE):
            part = part + sq[:, j * LANE:(j + 1) * LANE]
        l_ref[...] += part * (0.5 / D)

    return pl.pallas_call(
        body, name="loss_head", grid=(t // tm,),
        in_specs=[_row(tm, D), _row(tm, D)],
        out_specs=[_row(tm, D), _full((8, LANE))],
        out_shape=[jax.ShapeDtypeStruct((t, D), F32), jax.ShapeDtypeStruct((8, LANE), F32)],
        compiler_params=_cp("arbitrary"),
    )(y, target)


def _lower_bounds_fwd(logits):
    depth, n = logits.shape

    def body(l_ref, lb_ref):
        lg = l_ref[...]
        m = jnp.max(lg, axis=0, keepdims=True)
        e = jnp.exp(lg - m)
        p = e / jnp.sum(e, axis=0, keepdims=True)
        acc = jnp.zeros((1, n), F32)
        for l in range(depth):
            if l > 0:
                acc = acc + p[l:l + 1, :]
            lb_ref[l:l + 1, :] = acc

    return pl.pallas_call(body, name="lower_bounds_fwd", out_shape=jax.ShapeDtypeStruct((depth, n), F32))(logits)


def _lower_bounds_bwd(logits, dlb):
    depth, n = logits.shape

    def body(l_ref, dlb_ref, dl_ref):
        lg = l_ref[...]
        m = jnp.max(lg, axis=0, keepdims=True)
        e = jnp.exp(lg - m)
        p = e / jnp.sum(e, axis=0, keepdims=True)
        dps = [jnp.zeros((1, n), F32)]
        for j in range(1, depth):
            acc = jnp.zeros((1, n), F32)
            for l in range(j, depth):
                acc = acc + dlb_ref[l:l + 1, :]
            dps.append(acc)
        dot = jnp.zeros((1, n), F32)
        for j in range(depth):
            dot = dot + p[j:j + 1, :] * dps[j]
        for j in range(depth):
            dl_ref[j:j + 1, :] = p[j:j + 1, :] * (dps[j] - dot)

    return pl.pallas_call(body, name="lower_bounds_bwd", out_shape=jax.ShapeDtypeStruct((depth, n), F32))(logits, dlb)


def _ew_tile(rows, cols, n_arrays):
    cap = max(16, (24 * 1024 * 1024) // (8 * n_arrays * cols))
    best = None
    for t in range(16, rows + 1, 16):
        if rows % t == 0 and t <= cap:
            best = t
    return rows if best is None else best


def _adamw_math(w, g, m, v):
    mn = ADAM_B1 * m + (1.0 - ADAM_B1) * g
    vn = ADAM_B2 * v + (1.0 - ADAM_B2) * (g * g)
    m_hat = mn / (1.0 - ADAM_B1 ** ADAM_STEP)
    v_hat = vn / (1.0 - ADAM_B2 ** ADAM_STEP)
    return -ADAM_LR * (m_hat / (jnp.sqrt(v_hat) + ADAM_EPS) + ADAM_WD * w), mn, vn


def _adamw_layers(w, m, v, g0, g1, g_sibling, name):
    _, r, c_ = w.shape
    tr = _ew_tile(r, c_, 10)

    def body(w_ref, m_ref, v_ref, g0_ref, g1_ref, gs_ref, g_ref, d_ref, mo_ref, vo_ref):
        layer = pl.program_id(0)
        own = jnp.where(layer == 0, g0_ref[...], g1_ref[...])
        g = jnp.where(layer == lax.axis_index("c"), own, gs_ref[...])
        g_ref[...] = g
        d_ref[...], mo_ref[...], vo_ref[...] = _adamw_math(w_ref[...], g, m_ref[...], v_ref[...])

    lay = pl.BlockSpec((None, tr, c_), lambda l, i: (l, i, 0))
    flat = pl.BlockSpec((tr, c_), lambda l, i: (i, 0))
    return pl.pallas_call(
        body, name="adamw_" + name, grid=(2, r // tr),
        in_specs=[lay, lay, lay, flat, flat, flat], out_specs=[lay] * 4,
        out_shape=[jax.ShapeDtypeStruct(w.shape, F32)] * 4,
        compiler_params=_cp("parallel", "parallel"),
    )(w, m, v, g0, g1, g_sibling)


def _adamw(w, g, m, v, name):
    rows, cols = w.shape
    tr = _ew_tile(rows, cols, 7)

    def body(w_ref, g_ref, m_ref, v_ref, d_ref, mo_ref, vo_ref):
        d_ref[...], mo_ref[...], vo_ref[...] = _adamw_math(w_ref[...], g_ref[...], m_ref[...], v_ref[...])

    spec = pl.BlockSpec((tr, cols), lambda i: (i, 0))
    return pl.pallas_call(
        body, name="adamw_" + name, grid=(rows // tr,),
        in_specs=[spec] * 4, out_specs=[spec] * 3,
        out_shape=[jax.ShapeDtypeStruct((rows, cols), F32)] * 3,
        compiler_params=_cp("parallel"),
    )(w, g, m, v)


DEPTH = 2
BIG_SHAPES = {"w_in": ((1024, 6560), 1), "w_conv_out": ((512, 1024), 1), "w_uq": ((256, 768), 1),
              "w_ukv": ((128, 1024), 1), "w_attn_out": ((512, 1024), 1), "w_hgrn_out": ((512, 1024), 1),
              "w_out": ((1024, 1024), 0), "w_ff1": ((1024, 4096), 1), "w_ff2": ((4096, 1024), 0)}
BIG = tuple(BIG_SHAPES)
SMALL_SIZES = {"norm1_g": 1024, "conv_b": 512, "conv_ln_g": 512, "conv_ln_b": 512, "q_a_norm_g": 256,
               "kv_a_norm_g": 128, "q_norm_g": 96, "k_norm_g": 96, "hgrn_lb_logits": 512, "hgrn_norm_g": 512,
               "norm2_g": 1024}
SMALL = tuple(SMALL_SIZES)
W_IN_COLS = 6560
W_IN_SHARD = W_IN_COLS // 4
W_IN_SEGS = ((0, 1024, SEG_AG[0]), (1024, 1280, SEG_CQ[0]), (1280, 1408, SEG_CKV[0]), (1408, 1440, SEG_KR[0] + 64),
             (1440, 3488, SEG_H4[0]), (3488, 6560, SEG_GATES[0]))


def _pad_heads(w, nh, used, axis):
    shp = w.shape
    w = w.reshape(shp[:axis] + (nh, used) + shp[axis + 1:])
    pad = [(0, 0)] * w.ndim
    pad[axis + 1] = (0, LANE - used)
    w = jnp.pad(w, pad)
    return w.reshape(shp[:axis] + (nh * LANE,) + shp[axis + 1:])


def _unpad_heads(w, nh, used, axis):
    shp = w.shape
    w = w.reshape(shp[:axis] + (nh, LANE) + shp[axis + 1:])
    w = lax.slice_in_dim(w, 0, used, axis=axis + 1)
    return w.reshape(shp[:axis] + (nh * used,) + shp[axis + 1:])


def _w_in_from_chips(p4):
    def orig(a, b):
        out = []
        while a < b:
            s = a // W_IN_SHARD
            e = min(b, (s + 1) * W_IN_SHARD)
            out.append(p4[s][:, a - W_IN_SHARD * s:e - W_IN_SHARD * s])
            a = e
        return out
    zc = lambda n: jnp.zeros((D, n), p4[0].dtype)
    parts = (orig(3488, 6560) + orig(0, 1024) + orig(1440, 3488) + orig(1024, 1280) + orig(1280, 1408)
             + [zc(64)] + orig(1408, 1440) + [zc(32)])
    return jnp.concatenate(parts, axis=1)


def _w_in_grad_to_chips(dw):
    chips = []
    for s in range(4):
        a, b = W_IN_SHARD * s, W_IN_SHARD * (s + 1)
        parts = []
        for o0, o1, p0 in W_IN_SEGS:
            lo, hi = max(a, o0), min(b, o1)
            if lo < hi:
                parts.append(dw[:, p0 + lo - o0:p0 + hi - o0])
        chips.append(jnp.concatenate(parts, axis=1))
    return jnp.stack(chips)


def _cat_chips(p4, axis):
    return jnp.concatenate([p4[s] for s in range(4)], axis=axis)


EARLY = ("w_in", "w_uq", "w_ukv")
LATE = tuple(k for k in BIG if k not in EARLY)


def _prep_late(pieces):
    pc = lambda k: [pieces[k][s].astype(_MM) for s in range(4)]
    return dict(wao=_pad_heads(_cat_chips(pc("w_attn_out"), 1), NH, 64, 0), wco=_cat_chips(pc("w_conv_out"), 1),
                who=_cat_chips(pc("w_hgrn_out"), 1), wout=_cat_chips(pc("w_out"), 0),
                w1=_cat_chips(pc("w_ff1"), 1), w2=_cat_chips(pc("w_ff2"), 0))


def _prep_early(pieces, small, l):
    mm = lambda a: a.astype(_MM)
    pc = lambda k: [mm(pieces[k][s]) for s in range(4)]
    w_in_p = _w_in_from_chips(pc("w_in"))
    wuq = jnp.concatenate([_pad_heads(pc("w_uq")[s], 2, QK_DIM, 1) for s in range(4)], axis=1)
    wukv = _cat_chips(pc("w_ukv"), 1).reshape(128, NH, 128)
    wk = _pad_heads(wukv[:, :, :64].reshape(128, NH * 64), NH, 64, 1)
    wv = _pad_heads(wukv[:, :, 64:].reshape(128, NH * 64), NH, 64, 1)
    row = lambda a: a.astype(F32).reshape(1, -1)
    p = dict(
        w_in=w_in_p, w_in_t=w_in_p.T, wuq=wuq, wk=wk, wv=wv,
        g1=row(small["norm1_g"][l]), g2=row(small["norm2_g"][l]),
        cw=jnp.pad(small["conv_w"][l].astype(F32), ((0, 1), (0, 0))), cb=row(small["conv_b"][l]),
        lng=row(small["conv_ln_g"][l]), lnb=row(small["conv_ln_b"][l]),
        qag=row(small["q_a_norm_g"][l]), kvag=row(small["kv_a_norm_g"][l]),
        qng=jnp.pad(row(small["q_norm_g"][l]), ((0, 0), (0, LANE - QK_DIM))),
        kng=jnp.pad(row(small["k_norm_g"][l]), ((0, 0), (0, LANE - QK_DIM))),
        ng=row(small["hgrn_norm_g"][l]),
    )
    return p


def _rope_tables(t):
    pos = (jnp.arange(t, dtype=jnp.int32) - FRONT).astype(F32)
    inv_freq = 10000.0 ** (-jnp.arange(16, dtype=F32) / 16)
    ang = pos[:, None] * inv_freq[None, :]
    cos, sin = jnp.cos(ang), jnp.sin(ang)
    one = jnp.ones((t, 64), F32)
    z16, z32, z64 = jnp.zeros((t, 16), F32), jnp.zeros((t, 32), F32), jnp.zeros((t, 64), F32)
    c = jnp.concatenate([one, cos, cos, z32], axis=1)
    s1 = jnp.concatenate([z64, -sin, z16, z32], axis=1)
    s2 = jnp.concatenate([z64, z16, sin, z32], axis=1)
    return c, s1, s2


def _layer_fwd(x, p, lb, rope, t_end, plan=None, on_rode=None):
    gates, ag, h4, cq, ckv, kr, hb = _in_proj_fwd(x, p["g1"], p["w_in"])
    z = _conv_fwd(ag, p["cw"], p["cb"])
    q, k, v, cqn, ckvn = _mla_pre_fwd(cq, ckv, kr, p["qag"], p["wuq"], p["kvag"], p["wk"], p["wv"], p["qng"],
                                      p["kng"], *rope)
    (o_att, lse), rode = _attn_fwd(q, k, v, plan)
    if on_rode is not None:
        on_rode(rode)
    o_h, states = _hgrn_fwd(h4, lb)
    x1, mix, ca, oc, ya, yb, yc = _mix_fwd(x, z, o_att, o_h, h4, gates, p["lng"], p["lnb"], p["wco"], p["wao"],
                                           p["ng"], p["who"], p["wout"], t_end)
    x2, pre = _ffn_fwd(x1, p["g2"], p["w1"], p["w2"])
    saved = dict(x=x, gates=gates, ag=ag, h4=h4, cq=cq, ckv=ckv, kr=kr, hb=hb, z=z, q=q, k=k, v=v, cqn=cqn,
                 ckvn=ckvn, o_att=o_att, lse=lse, o_h=o_h, states=states, x1=x1, mix=mix, ca=ca, oc=oc,
                 ya=ya, yb=yb, yc=yc, pre=pre)
    return x2, saved


def _layer_bwd(dx2, s, p, lb, rope, t_end, rides=None):
    rides = rides or {}
    (dx1, h2, act, dp, dg2), rode = _ffn_bwd(dx2, s["x1"], s["pre"], p["g2"], p["w1"].T, p["w2"].T,
                                             rides.get("ffn"))
    g = {"w_ff1": _wgrad(h2, dp, "ff1", 4), "w_ff2": _wgrad(act, dx2, "ff2").reshape(4, D_FF // 4, D),
         "norm2_g": dg2.sum(0)}
    plan_mix = rides["mix"](rode, g) if "mix" in rides else None
    (dgt, dya, dyb, dyc, dz, doa, doh, dhg, dlng, dlnb, dcb, dng), rode = _mix_bwd(
        dx1, s["ya"], s["yb"], s["yc"], s["gates"], s["z"], s["o_h"], s["h4"], p["lng"], p["lnb"], p["ng"],
        p["wout"], p["wco"], p["wao"], p["who"], plan_mix)
    plan_attn = rides["attn"](rode) if "attn" in rides else None
    g["w_out"] = _wgrad(s["mix"], dx1, "out").reshape(4, D // 4, D)
    g["w_conv_out"] = _wgrad(s["ca"], dya, "conv_out", 4)
    g["w_attn_out"] = _unpad_heads(_wgrad(s["o_att"], dyb, "attn_out", 4), NH, 64, 1)
    g["w_hgrn_out"] = _wgrad(s["oc"], dyc, "hgrn_out", 4)
    g["conv_ln_g"], g["conv_ln_b"], g["conv_b"], g["hgrn_norm_g"] = dlng.sum(0), dlnb.sum(0), dcb.sum(0), dng.sum(0)
    da, dg, dcw = _conv_bwd(s["ag"], p["cw"], dz)
    g["conv_w"] = dcw[:CONV_K]
    (dq, dk, dv), rode_attn = _attn_bwd(s["q"], s["k"], s["v"], s["o_att"], s["lse"], doa, plan_attn)
    dcq, dckv, dkr, dqraw, dkraw, dqag, dkvag, dqng, dkng = _mla_pre_bwd(
        dq, dk, dv, s["cq"], s["ckv"], s["kr"], p["qag"], p["wuq"], p["kvag"], p["wk"], p["wv"], p["qng"],
        p["kng"], *rope)
    g["w_uq"] = _unpad_heads(_wgrad(s["cqn"], dqraw, "uq", 4), 2, QK_DIM, 2)
    dwk = _unpad_heads(_wgrad(s["ckvn"], dkraw, "uk"), NH, 64, 1).reshape(128, NH, 64)
    dwv = _unpad_heads(_wgrad(s["ckvn"], dv, "uv"), NH, 64, 1).reshape(128, NH, 64)
    g["w_ukv"] = jnp.concatenate([dwk, dwv], axis=2).reshape(128, 4, 256).transpose(1, 0, 2)
    g["q_a_norm_g"], g["kv_a_norm_g"] = dqag.sum(0), dkvag.sum(0)
    g["q_norm_g"], g["k_norm_g"] = dqng.sum(0)[:QK_DIM], dkng.sum(0)[:QK_DIM]
    dhq, dhf, dhi, dlb = _hgrn_bwd(s["h4"], lb, doh, s["states"])
    mm = lambda a: a.astype(_MM)
    du = jnp.concatenate([dgt, mm(da), mm(dg), mm(dhq), mm(dhf), mm(dhi), dhg, dcq, dckv, dkr], axis=1)
    dx, dg1 = _in_proj_bwd(du, s["x"], dx1, p["g1"], p["w_in_t"], t_end)
    g["norm1_g"] = dg1.sum(0)
    g["w_in"] = _w_in_grad_to_chips(_wgrad(s["hb"], du, "in"))
    return dx, g, dlb.sum(0), rode_attn


def _device_step(x, target, small, pieces0, pieces1=None, fwd_ride=None, bwd_rides=None):
    s_real = x.shape[0]
    t_end = ROW0 + s_real
    t = -(-t_end // LANE) * LANE
    zrow = lambda n: jnp.zeros((n, D), F32)
    xp = jnp.concatenate([zrow(FRONT), small["meta"].astype(F32), x, zrow(t - t_end)], axis=0)
    tp = jnp.concatenate([zrow(ROW0), target, zrow(t - t_end)], axis=0)
    rope = _rope_tables(t)
    logits = small["hgrn_lb_logits"].astype(F32)
    lbs = _lower_bounds_fwd(logits)
    prm0 = _prep_early(pieces0, small, 0)
    got = {}
    if fwd_ride is None:
        prm0.update(_prep_late(pieces0))
        h, sv0 = _layer_fwd(xp, prm0, lbs[0:1], rope, t_end)
    else:
        def on_rode(rode):
            late0, got["pieces1"] = fwd_ride[1](rode)
            prm0.update(_prep_late(late0))
        h, sv0 = _layer_fwd(xp, prm0, lbs[0:1], rope, t_end, fwd_ride[0], on_rode)
        pieces1 = got["pieces1"]
    prm1 = _prep_early(pieces1, small, 1)
    prm1.update(_prep_late(pieces1))
    h, sv1 = _layer_fwd(h, prm1, lbs[1:2], rope, t_end)
    dh, lsum = _loss_head(h, tp, t_end)
    loss = jnp.sum(lsum)
    dh, g1, dlb1, _ = _layer_bwd(dh, sv1, prm1, lbs[1:2], rope, t_end)
    dh, g0, dlb0, rode = _layer_bwd(dh, sv0, prm0, lbs[0:1], rope, t_end,
                                    None if bwd_rides is None else bwd_rides(g1))
    dlogits = _lower_bounds_bwd(logits, jnp.stack([dlb0, dlb1]))
    grads = [g0, g1]
    for l in range(DEPTH):
        grads[l]["hgrn_lb_logits"] = dlogits[l]
    return loss, dh[ROW0:t_end], grads, dh[FRONT:ROW0], rode


MESH = pl.DeviceIdType.MESH
_ANY = pl.BlockSpec(memory_space=pl.ANY)
SMALL_ROWS = 64
SMALL_LEN = SMALL_ROWS * 1024


def _mesh_pos():
    return lax.axis_index("x"), lax.axis_index("y"), lax.axis_index("c")


def _other_chips(x, y):
    return [(1 - x, y), (x, 1 - y), (1 - x, 1 - y)]


class _Plan:
    def __init__(self, name, ins, out_shapes, sems, start, finish):
        self.name, self.ins, self.out_shapes, self.sems = name, list(ins), list(out_shapes), list(sems)
        self.start, self.finish = start, finish


def _run_plan(plan):
    ni, no = len(plan.ins), len(plan.out_shapes)

    def body(*refs):
        ins, outs, sems = refs[:ni], refs[ni:ni + no], refs[ni + no:]
        plan.start(ins, outs, sems)
        plan.finish(ins, outs, sems)

    return pl.pallas_call(body, name=plan.name, in_specs=[_ANY] * ni, out_specs=[_ANY] * no,
                          out_shape=plan.out_shapes, scratch_shapes=plan.sems)(*plan.ins)


def _plan_specs(plan):
    if plan is None:
        return [], [], [], [], []
    return plan.ins, [_ANY] * len(plan.ins), [_ANY] * len(plan.out_shapes), plan.out_shapes, plan.sems


def _host_refs(refs, n_in, n_out, n_scratch, plan):
    ni = 0 if plan is None else len(plan.ins)
    no = 0 if plan is None else len(plan.out_shapes)
    o0 = n_in + ni
    s0 = o0 + n_out + no
    own = (refs[:n_in], refs[o0:o0 + n_out], refs[s0:s0 + n_scratch])
    rider = (refs[n_in:o0], refs[o0 + n_out:s0], refs[s0 + n_scratch:])
    return own, rider


def _ride(plan, rider, step, last):
    if plan is None:
        return lambda: None

    @pl.when(step == 0)
    def _():
        plan.start(*rider)

    def done():
        @pl.when(step == last)
        def _():
            plan.finish(*rider)
    return done


def _merge_plans(name, plans):
    def parts(ins, outs, sems):
        i = o = s = 0
        for p in plans:
            ni, no, ns = len(p.ins), len(p.out_shapes), len(p.sems)
            yield p, (ins[i:i + ni], outs[o:o + no], sems[s:s + ns])
            i, o, s = i + ni, o + no, s + ns

    def start(ins, outs, sems):
        for p, refs in parts(ins, outs, sems):
            p.start(*refs)

    def finish(ins, outs, sems):
        for p, refs in parts(ins, outs, sems):
            p.finish(*refs)

    return _Plan(name, [a for p in plans for a in p.ins], [a for p in plans for a in p.out_shapes],
                 [a for p in plans for a in p.sems], start, finish)


def _plan_gather(own, layer, name):
    nw = len(own)

    def copies(ins, outs, sems):
        send_sems, recv_sems = sems

        def over_ici(w, j, chip_of_data, to):
            return pltpu.make_async_remote_copy(
                src_ref=ins[w].at[layer], dst_ref=outs[w].at[chip_of_data], send_sem=send_sems.at[w, j],
                recv_sem=recv_sems.at[w, j], device_id=to, device_id_type=MESH)

        def over_d2d(w, j, chip_of_data, to):
            return pltpu.make_async_remote_copy(
                src_ref=outs[w].at[chip_of_data], dst_ref=outs[w].at[chip_of_data], send_sem=send_sems.at[w, 3 + j],
                recv_sem=recv_sems.at[w, 3 + j], device_id=to, device_id_type=MESH)
        return over_ici, over_d2d

    def start(ins, outs, sems):
        x, y, c = _mesh_pos()
        over_ici, _ = copies(ins, outs, sems)

        @pl.when(c == layer)
        def _():
            for j, (px, py) in enumerate(_other_chips(x, y)):
                for w in range(nw):
                    over_ici(w, j, 2 * x + y, (px, py, layer)).start()

    def finish(ins, outs, sems):
        x, y, c = _mesh_pos()
        over_ici, over_d2d = copies(ins, outs, sems)
        chips = _other_chips(x, y)

        @pl.when(c == layer)
        def _():
            for j, (px, py) in enumerate(chips):
                for w in range(nw):
                    over_ici(w, j, 2 * px + py, (x, y, c)).wait_recv()
                    over_d2d(w, j, 2 * px + py, (x, y, 1 - layer)).start()
            for j, (px, py) in enumerate(chips):
                for w in range(nw):
                    over_ici(w, j, 2 * x + y, (px, py, layer)).wait_send()
                    over_d2d(w, j, 2 * px + py, (x, y, 1 - layer)).wait_send()

        @pl.when(c != layer)
        def _():
            for j, (px, py) in enumerate(chips):
                for w in range(nw):
                    over_d2d(w, j, 2 * px + py, (x, y, c)).wait_recv()

    return _Plan(name, own,
                 [jax.ShapeDtypeStruct((4,) + a.shape[1:], a.dtype) for a in own],
                 [pltpu.SemaphoreType.DMA((nw, 6)), pltpu.SemaphoreType.DMA((nw, 6))], start, finish)


def _plan_to_sibling(arrs, layer, name):
    nw = len(arrs)

    def copy(ins, outs, sems, w):
        x, y, _ = _mesh_pos()
        return pltpu.make_async_remote_copy(src_ref=ins[w], dst_ref=outs[w], send_sem=sems[0].at[w],
                                            recv_sem=sems[1].at[w], device_id=(x, y, layer), device_id_type=MESH)

    def start(ins, outs, sems):
        @pl.when(lax.axis_index("c") != layer)
        def _():
            for w in range(nw):
                copy(ins, outs, sems, w).start()

    def finish(ins, outs, sems):
        c = lax.axis_index("c")

        @pl.when(c != layer)
        def _():
            for w in range(nw):
                copy(ins, outs, sems, w).wait_send()

        @pl.when(c == layer)
        def _():
            for w in range(nw):
                copy(ins, outs, sems, w).wait_recv()

    return _Plan(name, arrs, [jax.ShapeDtypeStruct(a.shape, a.dtype) for a in arrs],
                 [pltpu.SemaphoreType.DMA((nw,)), pltpu.SemaphoreType.DMA((nw,))], start, finish)


def _plan_scatter(parts, layer, name):
    nw = len(parts)

    def start(ins, outs, sems):
        x, y, c = _mesh_pos()

        @pl.when(c == layer)
        def _():
            for j, (px, py) in enumerate(_other_chips(x, y)):
                for w in range(nw):
                    pltpu.make_async_remote_copy(
                        src_ref=ins[w].at[2 * px + py], dst_ref=outs[w].at[2 * x + y], send_sem=sems[0].at[w, j],
                        recv_sem=sems[1].at[w, j], device_id=(px, py, layer), device_id_type=MESH).start()

    def finish(ins, outs, sems):
        x, y, c = _mesh_pos()

        @pl.when(c == layer)
        def _():
            for j, (px, py) in enumerate(_other_chips(x, y)):
                for w in range(nw):
                    pltpu.make_async_remote_copy(
                        src_ref=ins[w].at[2 * px + py], dst_ref=outs[w].at[2 * px + py], send_sem=sems[0].at[w, j],
                        recv_sem=sems[1].at[w, j], device_id=(x, y, c), device_id_type=MESH).wait()

    return _Plan(name, parts, [jax.ShapeDtypeStruct(a.shape, a.dtype) for a in parts],
                 [pltpu.SemaphoreType.DMA((nw, 3)), pltpu.SemaphoreType.DMA((nw, 3))], start, finish)


def _sibling_exchange(reds0, reds1):
    nw = len(reds0)

    def body(*refs):
        a0, a1, outs = refs[:nw], refs[nw:2 * nw], refs[2 * nw:3 * nw]
        send_sems, recv_sems = refs[3 * nw:]
        x, y, c = _mesh_pos()

        def copy(w, src):
            return pltpu.make_async_remote_copy(src_ref=src, dst_ref=outs[w], send_sem=send_sems.at[w],
                                                recv_sem=recv_sems.at[w], device_id=(x, y, 1 - c),
                                                device_id_type=MESH)

        @pl.when(c == 0)
        def _():
            for w in range(nw):
                copy(w, a0[w]).start()

        @pl.when(c == 1)
        def _():
            for w in range(nw):
                copy(w, a1[w]).start()

        for w in range(nw):
            copy(w, a0[w]).wait()

    return pl.pallas_call(
        body, name="sibling_exchange", in_specs=[_ANY] * (2 * nw), out_specs=[_ANY] * nw,
        out_shape=[jax.ShapeDtypeStruct(a.shape, a.dtype) for a in reds0],
        scratch_shapes=[pltpu.SemaphoreType.DMA((nw,)), pltpu.SemaphoreType.DMA((nw,))],
    )(*reds0, *reds1)


def _all_reduce_small(v, name):
    rows, cols = v.shape

    def body(v_ref, o_ref, slots, send_sems, recv_sems):
        x, y, c = _mesh_pos()
        me = 4 * x + 2 * y + c
        slots[me] = v_ref[...]
        peers = []
        for rel in range(1, 8):
            fx, fy, fc = (rel >> 2) & 1, (rel >> 1) & 1, rel & 1
            px = 1 - x if fx else x
            py = 1 - y if fy else y
            pc = 1 - c if fc else c
            peers.append((px, py, pc))
        cps = [pltpu.make_async_remote_copy(src_ref=v_ref, dst_ref=slots.at[me], send_sem=send_sems.at[k],
                                            recv_sem=recv_sems.at[k], device_id=peer, device_id_type=MESH)
               for k, peer in enumerate(peers)]
        for cp in cps:
            cp.start()
        for k, (px, py, pc) in enumerate(peers):
            pltpu.make_async_remote_copy(src_ref=v_ref, dst_ref=slots.at[4 * px + 2 * py + pc],
                                         send_sem=send_sems.at[k], recv_sem=recv_sems.at[k], device_id=(x, y, c),
                                         device_id_type=MESH).wait_recv()
        for cp in cps:
            cp.wait_send()
        acc = slots[0]
        for d in range(1, 8):
            acc = acc + slots[d]
        o_ref[...] = acc

    vm = pl.BlockSpec(memory_space=pltpu.VMEM)
    return pl.pallas_call(
        body, name=name, in_specs=[vm], out_specs=vm,
        out_shape=jax.ShapeDtypeStruct((rows, cols), F32),
        scratch_shapes=[pltpu.VMEM((8, rows, cols), F32), pltpu.SemaphoreType.DMA((7,)),
                        pltpu.SemaphoreType.DMA((7,))],
    )(v)


def _add_to_wire(a, b, name):
    n4, r, c_ = a.shape
    rows = n4 * r
    tr = _ew_tile(rows, c_, 3)

    def body(a_ref, b_ref, o_ref):
        o_ref[...] = (a_ref[...] + b_ref[...]).astype(o_ref.dtype)

    spec = pl.BlockSpec((tr, c_), lambda i: (i, 0))
    out = pl.pallas_call(
        body, name="add_to_wire_" + name, grid=(rows // tr,), in_specs=[spec, spec], out_specs=spec,
        out_shape=jax.ShapeDtypeStruct((rows, c_), jnp.bfloat16), compiler_params=_cp("parallel"),
    )(a.reshape(rows, c_), b.reshape(rows, c_))
    return out.reshape(n4, r, c_)


def _sum_chips(recv, own, name):
    _, r, c_ = recv.shape
    tr = _ew_tile(r, c_, 4)

    def body(r_ref, own_ref, o_ref):
        chip = 2 * lax.axis_index("x") + lax.axis_index("y")
        own_v = own_ref[...].astype(F32)
        acc = None
        for s in range(4):
            term = jnp.where(chip == s, own_v, r_ref[s].astype(F32))
            acc = term if acc is None else acc + term
        o_ref[...] = acc

    return pl.pallas_call(
        body, name="sum_chips_" + name, grid=(r // tr,),
        in_specs=[pl.BlockSpec((4, tr, c_), lambda i: (0, i, 0)), pl.BlockSpec((tr, c_), lambda i: (i, 0))],
        out_specs=pl.BlockSpec((tr, c_), lambda i: (i, 0)),
        out_shape=jax.ShapeDtypeStruct((r, c_), F32),
        compiler_params=_cp("parallel"),
    )(recv, own)


def _pack_small(vals, meta_full, conv_w_full):
    flat = jnp.concatenate([vals[k].reshape(-1) for k in SMALL] + [meta_full.reshape(-1), conv_w_full.reshape(-1)])
    return jnp.pad(flat, (0, SMALL_LEN - flat.shape[0])).reshape(SMALL_ROWS, 1024)


def _unpack_small(buf):
    flat = buf.reshape(-1)
    out, off = {}, 0
    for k in SMALL:
        n = DEPTH * SMALL_SIZES[k]
        out[k] = flat[off:off + n].reshape(DEPTH, SMALL_SIZES[k])
        off += n
    meta = flat[off:off + N_META * D].reshape(N_META, D)
    off += N_META * D
    conv_w = flat[off:off + DEPTH * CONV_K * CONV_DIM].reshape(DEPTH, CONV_K, CONV_DIM)
    return out, meta, conv_w


def kernel(x, meta, norm1_g, w_in, conv_w, conv_b, conv_ln_g, conv_ln_b, w_conv_out, q_a_norm_g, w_uq, kv_a_norm_g, w_ukv, q_norm_g, k_norm_g, w_attn_out, hgrn_lb_logits, hgrn_norm_g, w_hgrn_out, w_out, norm2_g, w_ff1, w_ff2, loss_target, m_meta, m_norm1_g, m_w_in, m_conv_w, m_conv_b, m_conv_ln_g, m_conv_ln_b, m_w_conv_out, m_q_a_norm_g, m_w_uq, m_kv_a_norm_g, m_w_ukv, m_q_norm_g, m_k_norm_g, m_w_attn_out, m_hgrn_lb_logits, m_hgrn_norm_g, m_w_hgrn_out, m_w_out, m_norm2_g, m_w_ff1, m_w_ff2, v_meta, v_norm1_g, v_w_in, v_conv_w, v_conv_b, v_conv_ln_g, v_conv_ln_b, v_w_conv_out, v_q_a_norm_g, v_w_uq, v_kv_a_norm_g, v_w_ukv, v_q_norm_g, v_k_norm_g, v_w_attn_out, v_hgrn_lb_logits, v_hgrn_norm_g, v_w_hgrn_out, v_w_out, v_norm2_g, v_w_ff1, v_w_ff2):
    names = ("meta", "norm1_g", "w_in", "conv_w", "conv_b", "conv_ln_g", "conv_ln_b", "w_conv_out", "q_a_norm_g",
             "w_uq", "kv_a_norm_g", "w_ukv", "q_norm_g", "k_norm_g", "w_attn_out", "hgrn_lb_logits", "hgrn_norm_g",
             "w_hgrn_out", "w_out", "norm2_g", "w_ff1", "w_ff2")
    w = dict(zip(names, (meta, norm1_g, w_in, conv_w, conv_b, conv_ln_g, conv_ln_b, w_conv_out, q_a_norm_g, w_uq,
                         kv_a_norm_g, w_ukv, q_norm_g, k_norm_g, w_attn_out, hgrn_lb_logits, hgrn_norm_g, w_hgrn_out,
                         w_out, norm2_g, w_ff1, w_ff2)))
    m = dict(zip(names, (m_meta, m_norm1_g, m_w_in, m_conv_w, m_conv_b, m_conv_ln_g, m_conv_ln_b, m_w_conv_out,
                         m_q_a_norm_g, m_w_uq, m_kv_a_norm_g, m_w_ukv, m_q_norm_g, m_k_norm_g, m_w_attn_out,
                         m_hgrn_lb_logits, m_hgrn_norm_g, m_w_hgrn_out, m_w_out, m_norm2_g, m_w_ff1, m_w_ff2)))
    v = dict(zip(names, (v_meta, v_norm1_g, v_w_in, v_conv_w, v_conv_b, v_conv_ln_g, v_conv_ln_b, v_w_conv_out,
                         v_q_a_norm_g, v_w_uq, v_kv_a_norm_g, v_w_ukv, v_q_norm_g, v_k_norm_g, v_w_attn_out,
                         v_hgrn_lb_logits, v_hgrn_norm_g, v_w_hgrn_out, v_w_out, v_norm2_g, v_w_ff1, v_w_ff2)))
    cx, cy, cc = _mesh_pos()
    chip = 2 * cx + cy
    zero = jnp.zeros((), jnp.int32)

    own = {k: w[k].astype(_MM) for k in BIG}

    def as_pieces(names, gathered, layer):
        return {k: [jnp.where(chip == s, own[k][layer], g[s]) for s in range(4)] for k, g in zip(names, gathered)}

    pieces0 = as_pieces(EARLY, _run_plan(_plan_gather([own[k] for k in EARLY], 0, "gather_l0_early")), 0)
    fwd_ride = (_merge_plans("gather_rest", [_plan_gather([own[k] for k in LATE], 0, "gather_l0_late"),
                                             _plan_gather([own[k] for k in BIG], 1, "gather_l1")]),
                lambda got: (as_pieces(LATE, got[:len(LATE)], 0), as_pieces(BIG, got[len(LATE):], 1)))
    meta_slab = lax.dynamic_update_slice(jnp.zeros((N_META, D), F32), meta, (zero, chip * (D // 4)))
    convw_slab = lax.dynamic_update_slice(jnp.zeros((DEPTH, CONV_K, CONV_DIM), F32), conv_w,
                                          (zero, zero, chip * (CONV_DIM // 4)))
    zsmall = {k: jnp.zeros((DEPTH, SMALL_SIZES[k]), F32) for k in SMALL}
    south = (cc == 0).astype(F32)
    _, meta_full, convw_full = _unpack_small(
        _all_reduce_small(_pack_small(zsmall, meta_slab, convw_slab) * south, "gather_small"))
    small = {k: w[k] for k in SMALL}
    small["meta"] = meta_full
    small["conv_w"] = convw_full

    FFN = ("w_ff1", "w_ff2")
    REST = tuple(k for k in BIG if k not in FFN)
    held = {}

    def to_wire(names, layer, mine, from_sibling):
        return lax.cond(
            cc == layer,
            lambda: [_add_to_wire(a, b, "%s_l%d" % (k, layer)) for k, a, b in zip(names, mine, from_sibling)],
            lambda: [jnp.zeros(a.shape, jnp.bfloat16) for a in mine])

    def chip_sum(names, layer, got, wire):
        return lax.cond(
            cc == layer,
            lambda: [_sum_chips(r, lax.dynamic_index_in_dim(s, chip, 0, keepdims=False), "%s_l%d" % (k, layer))
                     for k, r, s in zip(names, got, wire)],
            lambda: [jnp.zeros(s.shape[1:], F32) for s in wire])

    def bwd_rides(g1):
        held["g1"] = [g1[k] for k in BIG]

        def ride_mix(from_sibling1, g0_ffn):
            held["wire1"] = to_wire(BIG, 1, held["g1"], from_sibling1)
            held["g0_ffn"] = [g0_ffn[k] for k in FFN]
            return _plan_to_sibling(held["g0_ffn"], 0, "swap_grads_l0_ffn")

        def ride_attn(from_sibling0):
            held["wire0_ffn"] = to_wire(FFN, 0, held["g0_ffn"], from_sibling0)
            return _merge_plans("scatter_grads_early", [_plan_scatter(held["wire1"], 1, "scatter_grads_l1"),
                                                        _plan_scatter(held["wire0_ffn"], 0, "scatter_grads_l0_ffn")])

        return {"ffn": _plan_to_sibling(held["g1"], 1, "swap_grads_l1"), "mix": ride_mix, "attn": ride_attn}

    loss_share, grad_x, gl, g_meta, got = _device_step(x[0], loss_target[0], small, pieces0, None, fwd_ride,
                                                       bwd_rides)

    reds1 = chip_sum(BIG, 1, got[:len(BIG)], held["wire1"])
    reds0 = dict(zip(FFN, chip_sum(FFN, 0, got[len(BIG):], held["wire0_ffn"])))
    g0_rest = [gl[0][k] for k in REST]
    wire0 = to_wire(REST, 0, g0_rest, _run_plan(_plan_to_sibling(g0_rest, 0, "swap_grads_l0_rest")))
    reds0.update(zip(REST, chip_sum(REST, 0, _run_plan(_plan_scatter(wire0, 0, "scatter_grads_l0_rest")), wire0)))
    reds0 = [reds0[k] for k in BIG]
    reds_sibling = _sibling_exchange(reds0, reds1)
    grads, delta, new_m, new_v = {}, {}, {}, {}
    for k, r0, r1, theirs in zip(BIG, reds0, reds1, reds_sibling):
        grads[k], delta[k], new_m[k], new_v[k] = _adamw_layers(w[k], m[k], v[k], r0, r1, theirs, k)

    g_small_local = {k: jnp.stack([gl[l][k] for l in range(DEPTH)]) for k in SMALL}
    g_convw_local = jnp.stack([gl[l]["conv_w"] for l in range(DEPTH)])
    reduced = _all_reduce_small(
        _pack_small(g_small_local, g_meta, g_convw_local).at[SMALL_ROWS - 1, 1023].set(loss_share), "reduce_small")
    loss = reduced[SMALL_ROWS - 1, 1023]
    g_small, g_meta_full, g_convw_full = _unpack_small(reduced)
    grads.update(g_small)
    grads["meta"] = lax.dynamic_slice(g_meta_full, (zero, chip * (D // 4)), (N_META, D // 4))
    grads["conv_w"] = lax.dynamic_slice(g_convw_full, (zero, zero, chip * (CONV_DIM // 4)),
                                        (DEPTH, CONV_K, CONV_DIM // 4))

    def small_pack(src):
        return _pack_small(src, jnp.pad(src["meta"], ((0, 0), (0, D - D // 4))),
                           jnp.pad(src["conv_w"], ((0, 0), (0, 0), (0, CONV_DIM - CONV_DIM // 4))))

    def small_unpack(buf):
        out, meta_p, convw_p = _unpack_small(buf)
        out["meta"] = meta_p[:, :D // 4]
        out["conv_w"] = convw_p[:, :, :CONV_DIM // 4]
        return out

    d_s, m_s, v_s = [small_unpack(a) for a in _adamw(small_pack(w), small_pack(grads), small_pack(m),
                                                     small_pack(v), "small")]
    delta.update(d_s)
    new_m.update(m_s)
    new_v.update(v_s)
    return (loss, grad_x[None], *[grads[k] for k in names], *[delta[k] for k in names],
            *[new_m[k] for k in names], *[new_v[k] for k in names])
```

```python
import functools

import jax
import jax.numpy as jnp
from jax import lax
from jax.experimental import pallas as pl
from jax.experimental.pallas import tpu as pltpu

F32 = jnp.float32
_MM = jnp.bfloat16

D = 1024
N_META = 16
FRONT = 48
ROW0 = FRONT + N_META
EPS = 1e-6
GATE_CLAMP = 1.0 - 1e-6
CONV_K = 31
CONV_DIM = 512
NH = 8
QK_DIM = 96
ATT_SCALE = QK_DIM ** -0.5
HH = 4
CHUNK = 64
SUB = 16
EXP_CLIP = 60.0
NEG = -1e30
LANE = 128

SEG_GATES = (0, 3072)
SEG_AG = (3072, 4096)
SEG_H4 = (4096, 6144)
SEG_CQ = (6144, 6400)
SEG_CKV = (6400, 6528)
SEG_KR = (6528, 6656)
N_IN_P = 6656

ADAM_LR = 0.001
ADAM_B1 = 0.9
ADAM_B2 = 0.999
ADAM_EPS = 1e-08
ADAM_WD = 0.01
ADAM_STEP = 10

VMEM_LIMIT = 56 * 1024 * 1024


def _tile(n, pref):
    best = 64
    for t in range(64, pref + 1, 64):
        if n % t == 0:
            best = t
    return best


def _cp(*sem):
    return pltpu.CompilerParams(dimension_semantics=tuple(sem), vmem_limit_bytes=VMEM_LIMIT)


def _row(tm, n, col=0):
    return pl.BlockSpec((tm, n), lambda i: (i, col))


def _full(shape):
    return pl.BlockSpec(shape, lambda i: (0,) * len(shape))


def _mm(a, b):
    return jnp.dot(a.astype(_MM), b.astype(_MM), preferred_element_type=F32)


def _mm_nt(a, b):
    return lax.dot_general(a.astype(_MM), b.astype(_MM), (((1,), (1,)), ((), ())), preferred_element_type=F32)


def _mm_tn(a, b):
    return lax.dot_general(a.astype(_MM), b.astype(_MM), (((0,), (0,)), ((), ())), preferred_element_type=F32)


def _split3(x):
    hi = x.astype(jnp.bfloat16)
    return hi, (x - hi.astype(F32)).astype(jnp.bfloat16)


def _dot3(a, b, dims):
    ah, al = _split3(a)
    bh, bl = _split3(b)
    dg = lambda u, v: lax.dot_general(u, v, (dims, ((), ())), preferred_element_type=F32)
    return dg(ah, bh) + (dg(ah, bl) + dg(al, bh))


def _hmm(a, b):
    return _dot3(a, b, ((1,), (0,)))


def _hmm_nt(a, b):
    return _dot3(a, b, ((1,), (1,)))


def _hmm_tn(a, b):
    return _dot3(a, b, ((0,), (0,)))


def _sigmoid(x):
    return 1.0 / (1.0 + jnp.exp(-x))


def _rstd(x, n=None):
    n = x.shape[-1] if n is None else n
    return lax.rsqrt(jnp.sum(x * x, axis=-1, keepdims=True) * (1.0 / n) + EPS)


def _rms_bwd(dy, x, rstd, g, n=None):
    n = x.shape[-1] if n is None else n
    xh = x * rstd
    dxh = dy * g
    dx = rstd * (dxh - xh * (jnp.sum(dxh * xh, axis=-1, keepdims=True) * (1.0 / n)))
    return dx, dy * xh


def _valid_rows(i, tm, t_valid_end):
    r = i * tm + lax.broadcasted_iota(jnp.int32, (tm, 1), 0)
    return ((r >= FRONT) & (r < t_valid_end)).astype(F32)


def _colsum8(x):
    n, c = x.shape
    return jnp.sum(x.reshape(n // 8, 8, c), axis=0)


def _in_proj_fwd(x, g1, w):
    t = x.shape[0]
    tm = _tile(t, 192)
    segs = (SEG_GATES, SEG_AG, SEG_H4, SEG_CQ, SEG_CKV, SEG_KR)

    def body(x_ref, g_ref, w_ref, gates_ref, ag_ref, h4_ref, cq_ref, ckv_ref, kr_ref, hb_ref):
        xv = x_ref[...]
        hb = (xv * _rstd(xv) * g_ref[...]).astype(_MM)
        hb_ref[...] = hb
        for ref, (a, b) in zip((gates_ref, ag_ref, h4_ref, cq_ref, ckv_ref, kr_ref), segs):
            ref[...] = jnp.dot(hb, w_ref[:, a:b], preferred_element_type=F32)

    outs = [jax.ShapeDtypeStruct((t, b - a), F32) for a, b in segs] + [jax.ShapeDtypeStruct((t, D), _MM)]
    return pl.pallas_call(
        body, name="in_proj_fwd", grid=(t // tm,),
        in_specs=[_row(tm, D), _full((1, D)), _full((D, N_IN_P))],
        out_specs=[_row(tm, b - a) for a, b in segs] + [_row(tm, D)],
        out_shape=outs, compiler_params=_cp("parallel"),
    )(x, g1, w)


def _in_proj_bwd(du, x, dx1, g1, wt, t_end):
    t = x.shape[0]
    tm = _tile(t, 192)

    def body(du_ref, x_ref, dx1_ref, g_ref, wt_ref, dx_ref, dg_ref):
        i = pl.program_id(0)
        dh = jnp.dot(du_ref[...], wt_ref[...], preferred_element_type=F32)
        xv = x_ref[...]
        dxn, dgrow = _rms_bwd(dh, xv, _rstd(xv), g_ref[...])
        dx_ref[...] = _valid_rows(i, tm, t_end) * (dx1_ref[...] + dxn)

        @pl.when(i == 0)
        def _():
            dg_ref[...] = jnp.zeros_like(dg_ref)
        dg_ref[...] += _colsum8(dgrow)

    return pl.pallas_call(
        body, name="in_proj_bwd", grid=(t // tm,),
        in_specs=[_row(tm, N_IN_P), _row(tm, D), _row(tm, D), _full((1, D)), _full((N_IN_P, D))],
        out_specs=[_row(tm, D), _full((8, D))],
        out_shape=[jax.ShapeDtypeStruct((t, D), F32), jax.ShapeDtypeStruct((8, D), F32)],
        compiler_params=_cp("arbitrary"),
    )(du, x, dx1, g1, wt)


CONV_CH = 128


def _conv_fwd(ag, cw, cb):
    t = ag.shape[0]
    n = t // CONV_CH

    def body(a_ref, g_ref, w_ref, b_ref, z_ref, hp):
        hp[0:32, :] = jnp.zeros((32, LANE), F32)

        def fill(i, c):
            r = pl.multiple_of(i * CONV_CH, CONV_CH)
            hp[pl.ds(32 + r, CONV_CH), :] = a_ref[pl.ds(r, CONV_CH), :] * _sigmoid(g_ref[pl.ds(r, CONV_CH), :])
            return c
        lax.fori_loop(0, n, fill, 0)

        def conv(i, c):
            r = pl.multiple_of(i * CONV_CH, CONV_CH)
            acc = jnp.broadcast_to(b_ref[...], (CONV_CH, LANE))
            for k in range(CONV_K):
                acc = acc + w_ref[k:k + 1, :] * hp[pl.ds(r + (k + 2), CONV_CH), :]
            z_ref[pl.ds(r, CONV_CH), :] = acc
            return c
        lax.fori_loop(0, n, conv, 0)

    nb = CONV_DIM // LANE
    return pl.pallas_call(
        body, name="conv_fwd", grid=(nb,),
        in_specs=[pl.BlockSpec((t, LANE), lambda j: (0, j)), pl.BlockSpec((t, LANE), lambda j: (0, nb + j)),
                  pl.BlockSpec((32, LANE), lambda j: (0, j)), pl.BlockSpec((1, LANE), lambda j: (0, j))],
        out_specs=pl.BlockSpec((t, LANE), lambda j: (0, j)),
        out_shape=jax.ShapeDtypeStruct((t, CONV_DIM), F32),
        scratch_shapes=[pltpu.VMEM((t + 32, LANE), F32)],
        compiler_params=_cp("parallel"),
    )(ag, ag, cw, cb)


def _conv_bwd(ag, cw, dz):
    t = ag.shape[0]
    n = t // CONV_CH

    def body(a_ref, g_ref, w_ref, dz_ref, da_ref, dg_ref, dcw_ref, hp, dzp, accw):
        hp[0:32, :] = jnp.zeros((32, LANE), F32)
        dzp[pl.ds(t, 32), :] = jnp.zeros((32, LANE), F32)
        accw[...] = jnp.zeros_like(accw)

        def fill(i, c):
            r = pl.multiple_of(i * CONV_CH, CONV_CH)
            hp[pl.ds(32 + r, CONV_CH), :] = a_ref[pl.ds(r, CONV_CH), :] * _sigmoid(g_ref[pl.ds(r, CONV_CH), :])
            dzp[pl.ds(r, CONV_CH), :] = dz_ref[pl.ds(r, CONV_CH), :]
            return c
        lax.fori_loop(0, n, fill, 0)

        def step(i, c):
            r = pl.multiple_of(i * CONV_CH, CONV_CH)
            dzc = dz_ref[pl.ds(r, CONV_CH), :]
            dh = jnp.zeros((CONV_CH, LANE), F32)
            for k in range(CONV_K):
                dh = dh + w_ref[k:k + 1, :] * dzp[pl.ds(r + (CONV_K - 1 - k), CONV_CH), :]
                accw[8 * k:8 * k + 8, :] += _colsum8(dzc * hp[pl.ds(r + (k + 2), CONV_CH), :])
            a = a_ref[pl.ds(r, CONV_CH), :]
            sg = _sigmoid(g_ref[pl.ds(r, CONV_CH), :])
            da_ref[pl.ds(r, CONV_CH), :] = dh * sg
            dg_ref[pl.ds(r, CONV_CH), :] = dh * a * sg * (1.0 - sg)
            return c
        lax.fori_loop(0, n, step, 0)

        for k in range(CONV_K):
            dcw_ref[k:k + 1, :] = jnp.sum(accw[8 * k:8 * k + 8, :], axis=0, keepdims=True)
        dcw_ref[CONV_K:32, :] = jnp.zeros((32 - CONV_K, LANE), F32)

    nb = CONV_DIM // LANE
    colspec = pl.BlockSpec((t, LANE), lambda j: (0, j))
    return pl.pallas_call(
        body, name="conv_bwd", grid=(nb,),
        in_specs=[colspec, pl.BlockSpec((t, LANE), lambda j: (0, nb + j)),
                  pl.BlockSpec((32, LANE), lambda j: (0, j)), colspec],
        out_specs=[colspec, colspec, pl.BlockSpec((32, LANE), lambda j: (0, j))],
        out_shape=[jax.ShapeDtypeStruct((t, CONV_DIM), F32), jax.ShapeDtypeStruct((t, CONV_DIM), F32),
                   jax.ShapeDtypeStruct((32, CONV_DIM), F32)],
        scratch_shapes=[pltpu.VMEM((t + 32, LANE), F32), pltpu.VMEM((t + 32, LANE), F32),
                        pltpu.VMEM((8 * 32, LANE), F32)],
        compiler_params=_cp("parallel"),
    )(ag, ag, cw, dz)


def _rope(x, c, s1, s2):
    return x * c + pltpu.roll(x, LANE - 16, 1) * s1 + pltpu.roll(x, 16, 1) * s2


def _rope_t(dy, c, s1, s2):
    return dy * c + pltpu.roll(dy * s1, 16, 1) + pltpu.roll(dy * s2, LANE - 16, 1)


def _mla_pre_fwd(cq, ckv, kr, qag, wuq, kvag, wk, wv, qng, kng, rc, rs1, rs2):
    t = cq.shape[0]
    tm = _tile(t, 384)

    def body(cq_ref, ckv_ref, kr_ref, qag_ref, wuq_ref, kvag_ref, wk_ref, wv_ref, qng_ref, kng_ref,
             c_ref, s1_ref, s2_ref, q_ref, k_ref, v_ref, cqn_ref, ckvn_ref):
        cqv = cq_ref[...]
        cqn = (cqv * _rstd(cqv) * qag_ref[...]).astype(_MM)
        cqn_ref[...] = cqn
        ckvv = ckv_ref[...]
        ckvn = (ckvv * _rstd(ckvv) * kvag_ref[...]).astype(_MM)
        ckvn_ref[...] = ckvn
        qraw = jnp.dot(cqn, wuq_ref[...], preferred_element_type=F32)
        kraw = jnp.dot(ckvn, wk_ref[...], preferred_element_type=F32)
        v_ref[...] = jnp.dot(ckvn, wv_ref[...], preferred_element_type=F32).astype(_MM)
        krv = kr_ref[...]
        c, s1, s2 = c_ref[...], s1_ref[...], s2_ref[...]
        for h in range(NH):
            sl = slice(LANE * h, LANE * (h + 1))
            qh = qraw[:, sl]
            qn = qh * _rstd(qh, QK_DIM) * qng_ref[...]
            q_ref[:, sl] = (_rope(qn, c, s1, s2) * ATT_SCALE).astype(_MM)
            kh = kraw[:, sl] + krv
            kn = kh * _rstd(kh, QK_DIM) * kng_ref[...]
            k_ref[:, sl] = _rope(kn, c, s1, s2).astype(_MM)

    hd = NH * LANE
    return pl.pallas_call(
        body, name="mla_pre_fwd", grid=(t // tm,),
        in_specs=[_row(tm, 256), _row(tm, 128), _row(tm, 128), _full((1, 256)), _full((256, hd)),
                  _full((1, 128)), _full((128, hd)), _full((128, hd)), _full((1, LANE)), _full((1, LANE)),
                  _row(tm, LANE), _row(tm, LANE), _row(tm, LANE)],
        out_specs=[_row(tm, hd), _row(tm, hd), _row(tm, hd), _row(tm, 256), _row(tm, 128)],
        out_shape=[jax.ShapeDtypeStruct((t, hd), _MM)] * 3 + [jax.ShapeDtypeStruct((t, 256), _MM),
                                                              jax.ShapeDtypeStruct((t, 128), _MM)],
        compiler_params=_cp("parallel"),
    )(cq, ckv, kr, qag, wuq, kvag, wk, wv, qng, kng, rc, rs1, rs2)


def _mla_pre_bwd(dq, dk, dv, cq, ckv, kr, qag, wuq, kvag, wk, wv, qng, kng, rc, rs1, rs2):
    t = cq.shape[0]
    tm = _tile(t, 192)
    hd = NH * LANE

    def body(dq_ref, dk_ref, dv_ref, cq_ref, ckv_ref, kr_ref, qag_ref, wuq_ref, kvag_ref, wk_ref,
             wv_ref, qng_ref, kng_ref, c_ref, s1_ref, s2_ref,
             dcq_ref, dckv_ref, dkr_ref, dqraw_ref, dkraw_ref, dqag_ref, dkvag_ref, dqng_ref, dkng_ref):
        i = pl.program_id(0)
        cqv = cq_ref[...]
        rq_in = _rstd(cqv)
        cqn = (cqv * rq_in * qag_ref[...]).astype(_MM)
        ckvv = ckv_ref[...]
        rkv_in = _rstd(ckvv)
        ckvn = (ckvv * rkv_in * kvag_ref[...]).astype(_MM)
        qraw = jnp.dot(cqn, wuq_ref[...], preferred_element_type=F32)
        kraw = jnp.dot(ckvn, wk_ref[...], preferred_element_type=F32)
        krv = kr_ref[...]
        c, s1, s2 = c_ref[...], s1_ref[...], s2_ref[...]
        dkr = jnp.zeros((tm, LANE), F32)
        dqng = jnp.zeros((8, LANE), F32)
        dkng = jnp.zeros((8, LANE), F32)
        for h in range(NH):
            sl = slice(LANE * h, LANE * (h + 1))
            qh = qraw[:, sl]
            dqn = _rope_t(dq_ref[:, sl] * ATT_SCALE, c, s1, s2)
            dqh, gq = _rms_bwd(dqn, qh, _rstd(qh, QK_DIM), qng_ref[...], QK_DIM)
            dqraw_ref[:, sl] = dqh.astype(_MM)
            dqng = dqng + _colsum8(gq)
            kh = kraw[:, sl] + krv
            dkn = _rope_t(dk_ref[:, sl], c, s1, s2)
            dkh, gk = _rms_bwd(dkn, kh, _rstd(kh, QK_DIM), kng_ref[...], QK_DIM)
            dkraw_ref[:, sl] = dkh.astype(_MM)
            dkr = dkr + dkh
            dkng = dkng + _colsum8(gk)
        dkr_ref[...] = dkr.astype(_MM)
        dcqn = _mm_nt(dqraw_ref[...], wuq_ref[...])
        dcq, gqa = _rms_bwd(dcqn, cqv, rq_in, qag_ref[...])
        dcq_ref[...] = dcq.astype(_MM)
        dckvn = _mm_nt(dkraw_ref[...], wk_ref[...]) + _mm_nt(dv_ref[...], wv_ref[...])
        dckv, gkva = _rms_bwd(dckvn, ckvv, rkv_in, kvag_ref[...])
        dckv_ref[...] = dckv.astype(_MM)

        @pl.when(i == 0)
        def _():
            dqag_ref[...] = jnp.zeros_like(dqag_ref)
            dkvag_ref[...] = jnp.zeros_like(dkvag_ref)
            dqng_ref[...] = jnp.zeros_like(dqng_ref)
            dkng_ref[...] = jnp.zeros_like(dkng_ref)
        dqag_ref[...] += _colsum8(gqa)
        dkvag_ref[...] += _colsum8(gkva)
        dqng_ref[...] += dqng
        dkng_ref[...] += dkng

    return pl.pallas_call(
        body, name="mla_pre_bwd", grid=(t // tm,),
        in_specs=[_row(tm, hd), _row(tm, hd), _row(tm, hd), _row(tm, 256), _row(tm, 128), _row(tm, 128),
                  _full((1, 256)), _full((256, hd)), _full((1, 128)), _full((128, hd)),
                  _full((128, hd)), _full((1, LANE)), _full((1, LANE)),
                  _row(tm, LANE), _row(tm, LANE), _row(tm, LANE)],
        out_specs=[_row(tm, 256), _row(tm, 128), _row(tm, 128), _row(tm, hd), _row(tm, hd),
                   _full((8, 256)), _full((8, 128)), _full((8, LANE)), _full((8, LANE))],
        out_shape=[jax.ShapeDtypeStruct((t, 256), _MM), jax.ShapeDtypeStruct((t, 128), _MM),
                   jax.ShapeDtypeStruct((t, 128), _MM), jax.ShapeDtypeStruct((t, hd), _MM),
                   jax.ShapeDtypeStruct((t, hd), _MM), jax.ShapeDtypeStruct((8, 256), F32),
                   jax.ShapeDtypeStruct((8, 128), F32), jax.ShapeDtypeStruct((8, LANE), F32),
                   jax.ShapeDtypeStruct((8, LANE), F32)],
        compiler_params=_cp("arbitrary"),
    )(dq, dk, dv, cq, ckv, kr, qag, wuq, kvag, wk, wv, qng, kng, rc, rs1, rs2)


ATT_TILE = 704


def _attn_mask(r0, c0, tq):
    rows = r0 + lax.broadcasted_iota(jnp.int32, (tq, 1), 0)
    cols = c0 + lax.broadcasted_iota(jnp.int32, (1, tq), 1)
    return (cols <= rows) & (cols >= FRONT)


def _attn_fwd(q, k, v, plan=None):
    t = q.shape[0]
    tq = _tile(t, ATT_TILE)
    nq = t // tq
    p_args, p_in, p_out, p_shape, p_sem = _plan_specs(plan)

    def body(*refs):
        ((q_ref, k_ref, v_ref), (o_ref, lse_ref), _), rider = _host_refs(refs, 3, 2, 0, plan)
        done = _ride(plan, rider, pl.program_id(0), NH - 1)

        def qloop(qi, carry):
            r0 = pl.multiple_of(qi * tq, tq)
            qb = q_ref[pl.ds(r0, tq), :]

            def kstep(kj, st, masked):
                m, l, acc = st
                c0 = pl.multiple_of(kj * tq, tq)
                s = _mm_nt(qb, k_ref[pl.ds(c0, tq), :])
                if masked:
                    s = jnp.where(_attn_mask(r0, c0, tq), s, NEG)
                m2 = jnp.maximum(m, jnp.max(s, axis=-1, keepdims=True))
                p = jnp.exp(s - m2)
                a = jnp.exp(m - m2)
                l = a * l + jnp.sum(p, axis=-1, keepdims=True)
                acc = a * acc + _mm(p, v_ref[pl.ds(c0, tq), :])
                return m2, l, acc

            st = kstep(0, (jnp.full((tq, 1), NEG, F32), jnp.zeros((tq, 1), F32), jnp.zeros((tq, LANE), F32)), True)
            st = lax.fori_loop(1, qi, lambda kj, s_: kstep(kj, s_, False), st)
            m, l, acc = lax.cond(qi > 0, lambda s_: kstep(qi, s_, True), lambda s_: s_, st)
            o_ref[pl.ds(r0, tq), :] = acc / l
            lse_ref[pl.ds(r0, tq), :] = m + jnp.log(l)
            return carry
        lax.fori_loop(0, nq, qloop, 0)
        done()

    hs = pl.BlockSpec((t, LANE), lambda h: (0, h))
    res = pl.pallas_call(
        body, name="attn_fwd", grid=(NH,),
        in_specs=[hs, hs, hs] + p_in,
        out_specs=[hs, pl.BlockSpec((None, t, 1), lambda h: (h, 0, 0))] + p_out,
        out_shape=[jax.ShapeDtypeStruct((t, NH * LANE), F32), jax.ShapeDtypeStruct((NH, t, 1), F32)] + p_shape,
        scratch_shapes=p_sem,
        compiler_params=_cp("parallel" if plan is None else "arbitrary"),
    )(q, k, v, *p_args)
    return res[:2], res[2:]


def _attn_bwd(q, k, v, o, lse, do, plan=None):
    t = q.shape[0]
    tq = _tile(t, ATT_TILE)
    nq = t // tq
    p_args, p_in, p_out, p_shape, p_sem = _plan_specs(plan)

    def body(*refs):
        (ins, (dq_ref, dk_ref, dv_ref), (delta,)), rider = _host_refs(refs, 6, 3, 1, plan)
        q_ref, k_ref, v_ref, o_ref, lse_ref, do_ref = ins
        done = _ride(plan, rider, pl.program_id(0), NH - 1)

        def prep(i, c):
            r0 = pl.multiple_of(i * tq, tq)
            delta[pl.ds(r0, tq), :] = jnp.sum(do_ref[pl.ds(r0, tq), :] * o_ref[pl.ds(r0, tq), :], axis=-1,
                                              keepdims=True)
            dq_ref[pl.ds(r0, tq), :] = jnp.zeros((tq, LANE), F32)
            return c
        lax.fori_loop(0, nq, prep, 0)

        def kloop(kj, carry):
            c0 = pl.multiple_of(kj * tq, tq)
            kb = k_ref[pl.ds(c0, tq), :]
            vb = v_ref[pl.ds(c0, tq), :]

            def qstep(qi, st, masked):
                dkb, dvb = st
                r0 = pl.multiple_of(qi * tq, tq)
                qb = q_ref[pl.ds(r0, tq), :]
                dob = do_ref[pl.ds(r0, tq), :].astype(_MM)
                s = _mm_nt(qb, kb)
                if masked:
                    s = jnp.where(_attn_mask(r0, c0, tq), s, NEG)
                p = jnp.exp(s - lse_ref[pl.ds(r0, tq), :])
                dvb = dvb + _mm_tn(p, dob)
                dp = _mm_nt(dob, vb)
                ds = (p * (dp - delta[pl.ds(r0, tq), :])).astype(_MM)
                dkb = dkb + _mm_tn(ds, qb)
                dq_ref[pl.ds(r0, tq), :] += _mm(ds, kb)
                return dkb, dvb

            st = qstep(kj, (jnp.zeros((tq, LANE), F32), jnp.zeros((tq, LANE), F32)), True)
            dkb, dvb = lax.cond(
                kj == 0,
                lambda s_: lax.fori_loop(kj + 1, nq, lambda qi, t_: qstep(qi, t_, True), s_),
                lambda s_: lax.fori_loop(kj + 1, nq, lambda qi, t_: qstep(qi, t_, False), s_), st)
            dk_ref[pl.ds(c0, tq), :] = dkb
            dv_ref[pl.ds(c0, tq), :] = dvb
            return carry
        lax.fori_loop(0, nq, kloop, 0)
        done()

    hs = pl.BlockSpec((t, LANE), lambda h: (0, h))
    res = pl.pallas_call(
        body, name="attn_bwd", grid=(NH,),
        in_specs=[hs, hs, hs, hs, pl.BlockSpec((None, t, 1), lambda h: (h, 0, 0)), hs] + p_in,
        out_specs=[hs, hs, hs] + p_out,
        out_shape=[jax.ShapeDtypeStruct((t, NH * LANE), F32)] * 3 + p_shape,
        scratch_shapes=[pltpu.VMEM((t, 1), F32)] + p_sem,
        compiler_params=_cp("parallel" if plan is None else "arbitrary"),
    )(q, k, v, o, lse, do, *p_args)
    return res[:3], res[3:]


def _cumsum_rows(x):
    n = x.shape[0]
    rows = lax.broadcasted_iota(jnp.int32, (n, 1), 0)
    d = 1
    while d < n:
        x = x + jnp.where(rows >= d, pltpu.roll(x, d, 0), 0.0)
        d *= 2
    return x


def _revcumsum_rows(x):
    n = x.shape[0]
    rows = lax.broadcasted_iota(jnp.int32, (n, 1), 0)
    d = 1
    while d < n:
        x = x + jnp.where(rows < n - d, pltpu.roll(x, n - d, 0), 0.0)
        d *= 2
    return x


def _hgrn_gates(f, lb):
    sneg = _sigmoid(-f)
    kk = (1.0 - lb) * sneg
    lf = jnp.log1p(-jnp.minimum(kk, GATE_CLAMP))
    return kk, lf, sneg


def _silu(x):
    return x * _sigmoid(x)


def _dsilu(x):
    s = _sigmoid(x)
    return s * (1.0 + x * (1.0 - s))


def _hgrn_intra(q, kk, b):
    parts = []
    for blk in range(CHUNK // SUB):
        lo = blk * SUB
        ref = jnp.zeros((1, LANE), F32) if blk == 0 else b[lo - 1:lo, :]
        eq = jnp.exp(b[lo:lo + SUB, :] - ref)
        ek = jnp.exp(jnp.minimum(ref - b, EXP_CLIP))
        parts.append((q[lo:lo + SUB, :] * eq, kk * ek, eq, ek))
    return parts


def _chunk_causal():
    return lax.broadcasted_iota(jnp.int32, (CHUNK, CHUNK), 1) <= lax.broadcasted_iota(jnp.int32, (CHUNK, CHUNK), 0)


def _hgrn_fwd(h4, lb):
    t = h4.shape[0]
    nc = t // CHUNK

    def body(q_ref, f_ref, i_ref, lb_ref, o_ref, s_ref, st):
        st[...] = jnp.zeros_like(st)
        causal = _chunk_causal()

        def chunk(c, carry):
            r0 = pl.multiple_of(c * CHUNK, CHUNK)
            q = q_ref[pl.ds(r0, CHUNK), :]
            kk, lf, _ = _hgrn_gates(f_ref[pl.ds(r0, CHUNK), :], lb_ref[...])
            v = _silu(i_ref[pl.ds(r0, CHUNK), :])
            b = _cumsum_rows(lf)
            s_prev = st[...]
            s_ref[c] = s_prev
            o = _hmm_nt(q * jnp.exp(b), s_prev)
            a = jnp.concatenate([_hmm_nt(qs, ks) for qs, ks, _, _ in _hgrn_intra(q, kk, b)], axis=0)
            a = jnp.where(causal, a, 0.0)
            o_ref[pl.ds(r0, CHUNK), :] = o + _hmm(a, v)
            bl = b[CHUNK - 1:CHUNK, :]
            st[...] = s_prev * jnp.exp(bl) + _hmm_tn(v, kk * jnp.exp(bl - b))
            return carry
        lax.fori_loop(0, nc, chunk, 0, unroll=2)

    def col(j):
        return pl.BlockSpec((t, LANE), lambda h: (0, HH * j + h))
    return pl.pallas_call(
        body, name="hgrn_fwd", grid=(HH,),
        in_specs=[col(0), col(1), col(2), pl.BlockSpec((1, LANE), lambda h: (0, h))],
        out_specs=[pl.BlockSpec((t, LANE), lambda h: (0, h)),
                   pl.BlockSpec((None, nc, LANE, LANE), lambda h: (h, 0, 0, 0))],
        out_shape=[jax.ShapeDtypeStruct((t, HH * LANE), F32), jax.ShapeDtypeStruct((HH, nc, LANE, LANE), F32)],
        scratch_shapes=[pltpu.VMEM((LANE, LANE), F32)],
        compiler_params=_cp("parallel"),
    )(h4, h4, h4, lb)


def _hgrn_bwd(h4, lb, do, states, plan=None):
    t = h4.shape[0]
    nc = t // CHUNK
    p_args, p_in, p_out, p_shape, p_sem = _plan_specs(plan)

    def body(*refs):
        (ins, outs, (dst, carry)), rider = _host_refs(refs, 6, 4, 2, plan)
        q_ref, f_ref, i_ref, lb_ref, do_ref, s_ref = ins
        dq_ref, df_ref, di_ref, dlb_ref = outs
        done = _ride(plan, rider, pl.program_id(0), HH - 1)
        dst[...] = jnp.zeros_like(dst)
        carry[...] = jnp.zeros_like(carry)
        dlb_ref[...] = jnp.zeros_like(dlb_ref)
        causal = _chunk_causal()

        def chunk(cc, cr):
            c = nc - 1 - cc
            r0 = pl.multiple_of(c * CHUNK, CHUNK)
            q = q_ref[pl.ds(r0, CHUNK), :]
            lbv = lb_ref[...]
            kk, lf, sneg = _hgrn_gates(f_ref[pl.ds(r0, CHUNK), :], lbv)
            iv = i_ref[pl.ds(r0, CHUNK), :]
            v = _silu(iv)
            b = _cumsum_rows(lf)
            s_prev = s_ref[c]
            ds_new = dst[...]
            dob = do_ref[pl.ds(r0, CHUNK), :]
            e = jnp.exp(b)
            qe = q * e
            bl = b[CHUNK - 1:CHUNK, :]
            etail = jnp.exp(bl - b)
            kd = kk * etail
            dq_inter = _hmm(dob, s_prev) * e
            dv = _hmm_nt(kd, ds_new)
            dkk = _hmm(v, ds_new) * etail
            parts = _hgrn_intra(q, kk, b)
            a = jnp.where(causal, jnp.concatenate([_hmm_nt(qs, ks) for qs, ks, _, _ in parts], axis=0), 0.0)
            da = jnp.where(causal, _hmm_nt(dob, v), 0.0)
            dv = dv + _hmm_tn(a, dob)
            dq_rows = []
            for blk, (qs, ks, eq, ek) in enumerate(parts):
                da_blk = da[blk * SUB:(blk + 1) * SUB, :]
                dq_rows.append(_hmm(da_blk, ks) * eq)
                dkk = dkk + _hmm_tn(da_blk, qs) * ek
            dq = dq_inter + jnp.concatenate(dq_rows, axis=0)
            dst[...] = ds_new * jnp.exp(bl) + _hmm_tn(dob, qe)
            g = q * dq - kk * dkk
            dlf = _revcumsum_rows(g) + carry[0:1, :]
            carry[0:1, :] += jnp.sum(g, axis=0, keepdims=True)
            dkk_tot = dkk + dlf * jnp.where(kk < GATE_CLAMP, -1.0 / (1.0 - kk), 0.0)
            dq_ref[pl.ds(r0, CHUNK), :] = dq
            df_ref[pl.ds(r0, CHUNK), :] = dkk_tot * (1.0 - lbv) * (-sneg * (1.0 - sneg))
            di_ref[pl.ds(r0, CHUNK), :] = dv * _dsilu(iv)
            dlb_ref[...] += _colsum8(dkk_tot * (-sneg))
            return cr
        lax.fori_loop(0, nc, chunk, 0, unroll=2)
        done()

    def col(j):
        return pl.BlockSpec((t, LANE), lambda h: (0, HH * j + h))
    hs = pl.BlockSpec((t, LANE), lambda h: (0, h))
    res = pl.pallas_call(
        body, name="hgrn_bwd", grid=(HH,),
        in_specs=[col(0), col(1), col(2), pl.BlockSpec((1, LANE), lambda h: (0, h)), hs,
                  pl.BlockSpec((None, nc, LANE, LANE), lambda h: (h, 0, 0, 0))] + p_in,
        out_specs=[hs, hs, hs, pl.BlockSpec((8, LANE), lambda h: (0, h))] + p_out,
        out_shape=[jax.ShapeDtypeStruct((t, HH * LANE), F32)] * 3 + [jax.ShapeDtypeStruct((8, HH * LANE), F32)]
        + p_shape,
        scratch_shapes=[pltpu.VMEM((LANE, LANE), F32), pltpu.VMEM((8, LANE), F32)] + p_sem,
        compiler_params=_cp("parallel" if plan is None else "arbitrary"),
    )(h4, h4, h4, lb, do, states, *p_args)
    return res[:4], res[4:]


def _ln_fwd(z, g, b):
    mu = jnp.mean(z, axis=-1, keepdims=True)
    zc = z - mu
    rstd = lax.rsqrt(jnp.mean(zc * zc, axis=-1, keepdims=True) + EPS)
    zh = zc * rstd
    return zh * g + b, zh, rstd


def _mix_fwd(x, z, o_att, o_h, h4, gates, lng, lnb, wco, wao, ng, who, wout, t_end):
    t = x.shape[0]
    tm = _tile(t, 192)

    def body(x_ref, z_ref, oa_ref, oh_ref, hg_ref, gt_ref, lng_ref, lnb_ref, wco_ref, wao_ref, ng_ref, who_ref,
             wout_ref, x1_ref, mix_ref, ca_ref, oc_ref, ya_ref, yb_ref, yc_ref):
        i = pl.program_id(0)
        ln, _, _ = _ln_fwd(z_ref[...], lng_ref[...], lnb_ref[...])
        ca = _silu(ln).astype(_MM)
        ca_ref[...] = ca
        ya = jnp.dot(ca, wco_ref[...], preferred_element_type=F32)
        yb = _mm(oa_ref[...], wao_ref[...])
        hg = hg_ref[...]
        for h in range(HH):
            sl = slice(LANE * h, LANE * (h + 1))
            oh = oh_ref[:, sl]
            oc_ref[:, sl] = (oh * _rstd(oh) * ng_ref[:, sl] * _silu(hg[:, sl])).astype(_MM)
        yc = jnp.dot(oc_ref[...], who_ref[...], preferred_element_type=F32)
        ya_ref[...] = ya
        yb_ref[...] = yb
        yc_ref[...] = yc
        mix = (_sigmoid(gt_ref[:, 0:D]) * ya + _sigmoid(gt_ref[:, D:2 * D]) * yb
               + _sigmoid(gt_ref[:, 2 * D:3 * D]) * yc).astype(_MM)
        mix_ref[...] = mix
        x1_ref[...] = x_ref[...] + _valid_rows(i, tm, t_end) * jnp.dot(mix, wout_ref[...],
                                                                       preferred_element_type=F32)

    hd = NH * LANE
    return pl.pallas_call(
        body, name="mix_fwd", grid=(t // tm,),
        in_specs=[_row(tm, D), _row(tm, CONV_DIM), _row(tm, hd), _row(tm, 512), _row(tm, 512, 3), _row(tm, 3 * D),
                  _full((1, 512)), _full((1, 512)), _full((512, D)), _full((hd, D)), _full((1, 512)),
                  _full((512, D)), _full((D, D))],
        out_specs=[_row(tm, D), _row(tm, D), _row(tm, 512), _row(tm, 512), _row(tm, D), _row(tm, D), _row(tm, D)],
        out_shape=[jax.ShapeDtypeStruct((t, D), F32), jax.ShapeDtypeStruct((t, D), _MM),
                   jax.ShapeDtypeStruct((t, 512), _MM), jax.ShapeDtypeStruct((t, 512), _MM),
                   jax.ShapeDtypeStruct((t, D), F32), jax.ShapeDtypeStruct((t, D), F32),
                   jax.ShapeDtypeStruct((t, D), F32)],
        compiler_params=_cp("parallel"),
    )(x, z, o_att, o_h, h4, gates, lng, lnb, wco, wao, ng, who, wout)


def _mix_bwd(dx1, ya, yb, yc, gates, z, o_h, h4, lng, lnb, ng, wout, wco, wao, who, plan=None):
    t = dx1.shape[0]
    tm = _tile(t, 192)
    hd = NH * LANE
    p_args, p_in, p_out, p_shape, p_sem = _plan_specs(plan)

    def body(*refs):
        (ins, outs, _), rider = _host_refs(refs, 15, 12, 0, plan)
        (dx1_ref, ya_ref, yb_ref, yc_ref, gt_ref, z_ref, oh_ref, hg_ref, lng_ref, lnb_ref, ng_ref,
         wout_ref, wco_ref, wao_ref, who_ref) = ins
        (dgt_ref, dya_ref, dyb_ref, dyc_ref, dz_ref, doa_ref, doh_ref, dhg_ref,
         dlng_ref, dlnb_ref, dcb_ref, dng_ref) = outs
        i = pl.program_id(0)
        done = _ride(plan, rider, i, t // tm - 1)
        dmix = _mm_nt(dx1_ref[...], wout_ref[...])
        dys = []
        for j, y_ref in enumerate((ya_ref, yb_ref, yc_ref)):
            sg = _sigmoid(gt_ref[:, j * D:(j + 1) * D])
            dgt_ref[:, j * D:(j + 1) * D] = (dmix * y_ref[...] * sg * (1.0 - sg)).astype(_MM)
            dys.append((dmix * sg).astype(_MM))
        dya_ref[...], dyb_ref[...], dyc_ref[...] = dys
        dca = _mm_nt(dys[0], wco_ref[...])
        ln, zh, rstd = _ln_fwd(z_ref[...], lng_ref[...], lnb_ref[...])
        dln = dca * _dsilu(ln)
        dzh = dln * lng_ref[...]
        dz = rstd * (dzh - jnp.mean(dzh, axis=-1, keepdims=True)
                     - zh * jnp.mean(dzh * zh, axis=-1, keepdims=True))
        dz_ref[...] = dz
        doa_ref[...] = _mm_nt(dys[1], wao_ref[...])
        doc = _mm_nt(dys[2], who_ref[...])
        hg = hg_ref[...]
        dng_rows = []
        for h in range(HH):
            sl = slice(LANE * h, LANE * (h + 1))
            oh = oh_ref[:, sl]
            r = _rstd(oh)
            don = doc[:, sl] * _silu(hg[:, sl])
            dhg_ref[:, sl] = (doc[:, sl] * oh * r * ng_ref[:, sl] * _dsilu(hg[:, sl])).astype(_MM)
            doh, gn = _rms_bwd(don, oh, r, ng_ref[:, sl])
            doh_ref[:, sl] = doh
            dng_rows.append(_colsum8(gn))

        @pl.when(i == 0)
        def _():
            dlng_ref[...] = jnp.zeros_like(dlng_ref)
            dlnb_ref[...] = jnp.zeros_like(dlnb_ref)
            dcb_ref[...] = jnp.zeros_like(dcb_ref)
            dng_ref[...] = jnp.zeros_like(dng_ref)
        dlng_ref[...] += _colsum8(dln * zh)
        dlnb_ref[...] += _colsum8(dln)
        dcb_ref[...] += _colsum8(dz)
        dng_ref[...] += jnp.concatenate(dng_rows, axis=1)
        done()

    res = pl.pallas_call(
        body, name="mix_bwd", grid=(t // tm,),
        in_specs=[_row(tm, D), _row(tm, D), _row(tm, D), _row(tm, D), _row(tm, 3 * D), _row(tm, 512), _row(tm, 512),
                  _row(tm, 512, 3), _full((1, 512)), _full((1, 512)), _full((1, 512)),
                  _full((D, D)), _full((512, D)), _full((hd, D)), _full((512, D))] + p_in,
        out_specs=[_row(tm, 3 * D), _row(tm, D), _row(tm, D), _row(tm, D), _row(tm, 512), _row(tm, hd),
                   _row(tm, 512), _row(tm, 512), _full((8, 512)), _full((8, 512)), _full((8, 512)),
                   _full((8, 512))] + p_out,
        out_shape=[jax.ShapeDtypeStruct((t, 3 * D), _MM), jax.ShapeDtypeStruct((t, D), _MM),
                   jax.ShapeDtypeStruct((t, D), _MM), jax.ShapeDtypeStruct((t, D), _MM),
                   jax.ShapeDtypeStruct((t, 512), F32), jax.ShapeDtypeStruct((t, hd), F32),
                   jax.ShapeDtypeStruct((t, 512), F32), jax.ShapeDtypeStruct((t, 512), _MM)]
        + [jax.ShapeDtypeStruct((8, 512), F32)] * 4 + p_shape,
        scratch_shapes=p_sem,
        compiler_params=_cp("arbitrary"),
    )(dx1, ya, yb, yc, gates, z, o_h, h4, lng, lnb, ng, wout, wco, wao, who, *p_args)
    return res[:12], res[12:]


D_FF = 4096


def _ffn_fwd(x1, g2, w1, w2):
    t = x1.shape[0]
    tm = _tile(t, 192)

    def body(x1_ref, g_ref, w1_ref, w2_ref, x2_ref, p_ref):
        xv = x1_ref[...]
        h2 = (xv * _rstd(xv) * g_ref[...]).astype(_MM)
        p = jnp.dot(h2, w1_ref[...], preferred_element_type=F32)
        p_ref[...] = p
        r = jnp.maximum(p, 0.0)
        x2_ref[...] = xv + jnp.dot((r * r).astype(_MM), w2_ref[...], preferred_element_type=F32)

    return pl.pallas_call(
        body, name="ffn_fwd", grid=(t // tm,),
        in_specs=[_row(tm, D), _full((1, D)), _full((D, D_FF)), _full((D_FF, D))],
        out_specs=[_row(tm, D), _row(tm, D_FF)],
        out_shape=[jax.ShapeDtypeStruct((t, D), F32), jax.ShapeDtypeStruct((t, D_FF), F32)],
        compiler_params=_cp("parallel"),
    )(x1, g2, w1, w2)


def _ffn_bwd(dx2, x1, p, g2, w1t, w2t, plan=None):
    t = x1.shape[0]
    tm = _tile(t, 192)
    p_args, p_in, p_out, p_shape, p_sem = _plan_specs(plan)

    def body(*refs):
        (ins, outs, _), rider = _host_refs(refs, 6, 5, 0, plan)
        dx2_ref, x1_ref, p_ref, g_ref, w1t_ref, w2t_ref = ins
        dx1_ref, h2_ref, act_ref, dp_ref, dg_ref = outs
        i = pl.program_id(0)
        done = _ride(plan, rider, i, t // tm - 1)
        xv = x1_ref[...]
        rstd = _rstd(xv)
        h2_ref[...] = (xv * rstd * g_ref[...]).astype(_MM)
        r = jnp.maximum(p_ref[...], 0.0)
        act_ref[...] = (r * r).astype(_MM)
        dx2 = dx2_ref[...]
        da = _mm(dx2, w2t_ref[...])
        dp = (2.0 * r * da).astype(_MM)
        dp_ref[...] = dp
        dh2 = jnp.dot(dp, w1t_ref[...], preferred_element_type=F32)
        dxn, dgrow = _rms_bwd(dh2, xv, rstd, g_ref[...])
        dx1_ref[...] = dx2 + dxn

        @pl.when(i == 0)
        def _():
            dg_ref[...] = jnp.zeros_like(dg_ref)
        dg_ref[...] += _colsum8(dgrow)
        done()

    res = pl.pallas_call(
        body, name="ffn_bwd", grid=(t // tm,),
        in_specs=[_row(tm, D), _row(tm, D), _row(tm, D_FF), _full((1, D)), _full((D_FF, D)),
                  _full((D, D_FF))] + p_in,
        out_specs=[_row(tm, D), _row(tm, D), _row(tm, D_FF), _row(tm, D_FF), _full((8, D))] + p_out,
        out_shape=[jax.ShapeDtypeStruct((t, D), F32), jax.ShapeDtypeStruct((t, D), _MM),
                   jax.ShapeDtypeStruct((t, D_FF), _MM), jax.ShapeDtypeStruct((t, D_FF), _MM),
                   jax.ShapeDtypeStruct((8, D), F32)] + p_shape,
        scratch_shapes=p_sem,
        compiler_params=_cp("arbitrary"),
    )(dx2, x1, p, g2, w1t, w2t, *p_args)
    return res[:5], res[5:]


WGRAD_VMEM = 40 * 1024 * 1024


def _wgrad(a, b, name, chips=1):
    t, ka = a.shape
    nb = b.shape[1]
    cs = nb // chips
    widths = [d for d in range(cs, 0, -LANE) if cs % d == 0 and d % LANE == 0] or [cs]
    tn, tm = widths[-1], 64
    for d in widths:
        room = WGRAD_VMEM - 2 * ka * d * 4
        row_bytes = 2 * (ka * a.dtype.itemsize + d * b.dtype.itemsize) + 4 * ka
        fit = [r for r in range(64, t + 1, 64) if t % r == 0 and r * row_bytes <= room]
        if ka * d * 4 <= 16 * 1024 * 1024 and fit and (max(fit) >= 384 or d == widths[-1]):
            tn, tm = d, max(fit)
            break
    per = cs // tn

    def body(a_ref, b_ref, o_ref):
        @pl.when(pl.program_id(1) == 0)
        def _():
            o_ref[...] = jnp.zeros_like(o_ref)
        o_ref[...] += _mm_tn(a_ref[...], b_ref[...])

    if chips == 1:
        out_spec = pl.BlockSpec((ka, tn), lambda n, i: (0, n))
        out_shape = jax.ShapeDtypeStruct((ka, nb), F32)
    else:
        out_spec = pl.BlockSpec((None, ka, tn), lambda n, i: (n // per, 0, n % per))
        out_shape = jax.ShapeDtypeStruct((chips, ka, cs), F32)
    return pl.pallas_call(
        body, name="wgrad_" + name, grid=(nb // tn, t // tm),
        in_specs=[pl.BlockSpec((tm, ka), lambda n, i: (i, 0)), pl.BlockSpec((tm, tn), lambda n, i: (i, n))],
        out_specs=out_spec, out_shape=out_shape,
        compiler_params=_cp("parallel", "arbitrary"),
    )(a, b)


def _loss_head(y, target, t_end):
    t = y.shape[0]
    tm = _tile(t, 384)

    def body(y_ref, tg_ref, dy_ref, l_ref):
        i = pl.program_id(0)
        r = i * tm + lax.broadcasted_iota(jnp.int32, (tm, 1), 0)
        real = ((r >= ROW0) & (r < t_end)).astype(F32)
        diff = (y_ref[...] - tg_ref[...]) * real
        dy_ref[...] = diff * (1.0 / D)

        @pl.when(i == 0)
        def _():
            l_ref[...] = jnp.zeros_like(l_ref)
        sq = _colsum8(diff * diff)
        part = sq[:, 0:LANE]
        for j in range(1, D // LANE):
            part = part + sq[:, j * LANE:(j + 1) * LANE]
        l_ref[...] += part * (0.5 / D)

    return pl.pallas_call(
        body, name="loss_head", grid=(t // tm,),
        in_specs=[_row(tm, D), _row(tm, D)],
        out_specs=[_row(tm, D), _full((8, LANE))],
        out_shape=[jax.ShapeDtypeStruct((t, D), F32), jax.ShapeDtypeStruct((8, LANE), F32)],
        compiler_params=_cp("arbitrary"),
    )(y, target)


def _lower_bounds_fwd(logits):
    depth, n = logits.shape

    def body(l_ref, lb_ref):
        lg = l_ref[...]
        m = jnp.max(lg, axis=0, keepdims=True)
        e = jnp.exp(lg - m)
        p = e / jnp.sum(e, axis=0, keepdims=True)
        acc = jnp.zeros((1, n), F32)
        for l in range(depth):
            if l > 0:
                acc = acc + p[l:l + 1, :]
            lb_ref[l:l + 1, :] = acc

    return pl.pallas_call(body, name="lower_bounds_fwd", out_shape=jax.ShapeDtypeStruct((depth, n), F32))(logits)


def _lower_bounds_bwd(logits, dlb):
    depth, n = logits.shape

    def body(l_ref, dlb_ref, dl_ref):
        lg = l_ref[...]
        m = jnp.max(lg, axis=0, keepdims=True)
        e = jnp.exp(lg - m)
        p = e / jnp.sum(e, axis=0, keepdims=True)
        dps = [jnp.zeros((1, n), F32)]
        for j in range(1, depth):
            acc = jnp.zeros((1, n), F32)
            for l in range(j, depth):
                acc = acc + dlb_ref[l:l + 1, :]
            dps.append(acc)
        dot = jnp.zeros((1, n), F32)
        for j in range(depth):
            dot = dot + p[j:j + 1, :] * dps[j]
        for j in range(depth):
            dl_ref[j:j + 1, :] = p[j:j + 1, :] * (dps[j] - dot)

    return pl.pallas_call(body, name="lower_bounds_bwd", out_shape=jax.ShapeDtypeStruct((depth, n), F32))(logits, dlb)


def _ew_tile(rows, cols, n_arrays):
    cap = max(16, (24 * 1024 * 1024) // (8 * n_arrays * cols))
    best = None
    for t in range(16, rows + 1, 16):
        if rows % t == 0 and t <= cap:
            best = t
    return rows if best is None else best


def _adamw_math(w, g, m, v):
    mn = ADAM_B1 * m + (1.0 - ADAM_B1) * g
    vn = ADAM_B2 * v + (1.0 - ADAM_B2) * (g * g)
    m_hat = mn / (1.0 - ADAM_B1 ** ADAM_STEP)
    v_hat = vn / (1.0 - ADAM_B2 ** ADAM_STEP)
    return -ADAM_LR * (m_hat / (jnp.sqrt(v_hat) + ADAM_EPS) + ADAM_WD * w), mn, vn


def _adamw_layers(w, m, v, g0, g1, g_sibling, name):
    _, r, c_ = w.shape
    tr = _ew_tile(r, c_, 10)

    def body(w_ref, m_ref, v_ref, g0_ref, g1_ref, gs_ref, g_ref, d_ref, mo_ref, vo_ref):
        layer = pl.program_id(0)
        own = jnp.where(layer == 0, g0_ref[...], g1_ref[...])
        g = jnp.where(layer == lax.axis_index("c"), own, gs_ref[...])
        g_ref[...] = g
        d_ref[...], mo_ref[...], vo_ref[...] = _adamw_math(w_ref[...], g, m_ref[...], v_ref[...])

    lay = pl.BlockSpec((None, tr, c_), lambda l, i: (l, i, 0))
    flat = pl.BlockSpec((tr, c_), lambda l, i: (i, 0))
    return pl.pallas_call(
        body, name="adamw_" + name, grid=(2, r // tr),
        in_specs=[lay, lay, lay, flat, flat, flat], out_specs=[lay] * 4,
        out_shape=[jax.ShapeDtypeStruct(w.shape, F32)] * 4,
        compiler_params=_cp("parallel", "parallel"),
    )(w, m, v, g0, g1, g_sibling)


def _adamw(w, g, m, v, name):
    rows, cols = w.shape
    tr = _ew_tile(rows, cols, 7)

    def body(w_ref, g_ref, m_ref, v_ref, d_ref, mo_ref, vo_ref):
        d_ref[...], mo_ref[...], vo_ref[...] = _adamw_math(w_ref[...], g_ref[...], m_ref[...], v_ref[...])

    spec = pl.BlockSpec((tr, cols), lambda i: (i, 0))
    return pl.pallas_call(
        body, name="adamw_" + name, grid=(rows // tr,),
        in_specs=[spec] * 4, out_specs=[spec] * 3,
        out_shape=[jax.ShapeDtypeStruct((rows, cols), F32)] * 3,
        compiler_params=_cp("parallel"),
    )(w, g, m, v)


DEPTH = 2
BIG_SHAPES = {"w_in": ((1024, 6560), 1), "w_conv_out": ((512, 1024), 1), "w_uq": ((256, 768), 1),
              "w_ukv": ((128, 1024), 1), "w_attn_out": ((512, 1024), 1), "w_hgrn_out": ((512, 1024), 1),
              "w_out": ((1024, 1024), 0), "w_ff1": ((1024, 4096), 1), "w_ff2": ((4096, 1024), 0)}
BIG = tuple(BIG_SHAPES)
SMALL_SIZES = {"norm1_g": 1024, "conv_b": 512, "conv_ln_g": 512, "conv_ln_b": 512, "q_a_norm_g": 256,
               "kv_a_norm_g": 128, "q_norm_g": 96, "k_norm_g": 96, "hgrn_lb_logits": 512, "hgrn_norm_g": 512,
               "norm2_g": 1024}
SMALL = tuple(SMALL_SIZES)
W_IN_COLS = 6560
W_IN_SHARD = W_IN_COLS // 4
W_IN_SEGS = ((0, 1024, SEG_AG[0]), (1024, 1280, SEG_CQ[0]), (1280, 1408, SEG_CKV[0]), (1408, 1440, SEG_KR[0] + 64),
             (1440, 3488, SEG_H4[0]), (3488, 6560, SEG_GATES[0]))


def _pad_heads(w, nh, used, axis):
    shp = w.shape
    w = w.reshape(shp[:axis] + (nh, used) + shp[axis + 1:])
    pad = [(0, 0)] * w.ndim
    pad[axis + 1] = (0, LANE - used)
    w = jnp.pad(w, pad)
    return w.reshape(shp[:axis] + (nh * LANE,) + shp[axis + 1:])


def _unpad_heads(w, nh, used, axis):
    shp = w.shape
    w = w.reshape(shp[:axis] + (nh, LANE) + shp[axis + 1:])
    w = lax.slice_in_dim(w, 0, used, axis=axis + 1)
    return w.reshape(shp[:axis] + (nh * used,) + shp[axis + 1:])


def _w_in_from_chips(p4):
    def orig(a, b):
        out = []
        while a < b:
            s = a // W_IN_SHARD
            e = min(b, (s + 1) * W_IN_SHARD)
            out.append(p4[s][:, a - W_IN_SHARD * s:e - W_IN_SHARD * s])
            a = e
        return out
    zc = lambda n: jnp.zeros((D, n), p4[0].dtype)
    parts = (orig(3488, 6560) + orig(0, 1024) + orig(1440, 3488) + orig(1024, 1280) + orig(1280, 1408)
             + [zc(64)] + orig(1408, 1440) + [zc(32)])
    return jnp.concatenate(parts, axis=1)


def _w_in_grad_to_chips(dw):
    chips = []
    for s in range(4):
        a, b = W_IN_SHARD * s, W_IN_SHARD * (s + 1)
        parts = []
        for o0, o1, p0 in W_IN_SEGS:
            lo, hi = max(a, o0), min(b, o1)
            if lo < hi:
                parts.append(dw[:, p0 + lo - o0:p0 + hi - o0])
        chips.append(jnp.concatenate(parts, axis=1))
    return jnp.stack(chips)


def _cat_chips(p4, axis):
    return jnp.concatenate([p4[s] for s in range(4)], axis=axis)


EARLY = ("w_in", "w_uq", "w_ukv")
LATE = tuple(k for k in BIG if k not in EARLY)


def _prep_late(pieces):
    pc = lambda k: [pieces[k][s].astype(_MM) for s in range(4)]
    return dict(wao=_pad_heads(_cat_chips(pc("w_attn_out"), 1), NH, 64, 0), wco=_cat_chips(pc("w_conv_out"), 1),
                who=_cat_chips(pc("w_hgrn_out"), 1), wout=_cat_chips(pc("w_out"), 0),
                w1=_cat_chips(pc("w_ff1"), 1), w2=_cat_chips(pc("w_ff2"), 0))


def _prep_early(pieces, small, l):
    mm = lambda a: a.astype(_MM)
    pc = lambda k: [mm(pieces[k][s]) for s in range(4)]
    w_in_p = _w_in_from_chips(pc("w_in"))
    wuq = jnp.concatenate([_pad_heads(pc("w_uq")[s], 2, QK_DIM, 1) for s in range(4)], axis=1)
    wukv = _cat_chips(pc("w_ukv"), 1).reshape(128, NH, 128)
    wk = _pad_heads(wukv[:, :, :64].reshape(128, NH * 64), NH, 64, 1)
    wv = _pad_heads(wukv[:, :, 64:].reshape(128, NH * 64), NH, 64, 1)
    row = lambda a: a.astype(F32).reshape(1, -1)
    p = dict(
        w_in=w_in_p, w_in_t=w_in_p.T, wuq=wuq, wk=wk, wv=wv,
        g1=row(small["norm1_g"][l]), g2=row(small["norm2_g"][l]),
        cw=jnp.pad(small["conv_w"][l].astype(F32), ((0, 1), (0, 0))), cb=row(small["conv_b"][l]),
        lng=row(small["conv_ln_g"][l]), lnb=row(small["conv_ln_b"][l]),
        qag=row(small["q_a_norm_g"][l]), kvag=row(small["kv_a_norm_g"][l]),
        qng=jnp.pad(row(small["q_norm_g"][l]), ((0, 0), (0, LANE - QK_DIM))),
        kng=jnp.pad(row(small["k_norm_g"][l]), ((0, 0), (0, LANE - QK_DIM))),
        ng=row(small["hgrn_norm_g"][l]),
    )
    return p


def _rope_tables(t):
    pos = (jnp.arange(t, dtype=jnp.int32) - FRONT).astype(F32)
    inv_freq = 10000.0 ** (-jnp.arange(16, dtype=F32) / 16)
    ang = pos[:, None] * inv_freq[None, :]
    cos, sin = jnp.cos(ang), jnp.sin(ang)
    one = jnp.ones((t, 64), F32)
    z16, z32, z64 = jnp.zeros((t, 16), F32), jnp.zeros((t, 32), F32), jnp.zeros((t, 64), F32)
    c = jnp.concatenate([one, cos, cos, z32], axis=1)
    s1 = jnp.concatenate([z64, -sin, z16, z32], axis=1)
    s2 = jnp.concatenate([z64, z16, sin, z32], axis=1)
    return c, s1, s2


def _layer_fwd(x, p, lb, rope, t_end, plan=None, on_rode=None):
    gates, ag, h4, cq, ckv, kr, hb = _in_proj_fwd(x, p["g1"], p["w_in"])
    z = _conv_fwd(ag, p["cw"], p["cb"])
    q, k, v, cqn, ckvn = _mla_pre_fwd(cq, ckv, kr, p["qag"], p["wuq"], p["kvag"], p["wk"], p["wv"], p["qng"],
                                      p["kng"], *rope)
    (o_att, lse), rode = _attn_fwd(q, k, v, plan)
    if on_rode is not None:
        on_rode(rode)
    o_h, states = _hgrn_fwd(h4, lb)
    x1, mix, ca, oc, ya, yb, yc = _mix_fwd(x, z, o_att, o_h, h4, gates, p["lng"], p["lnb"], p["wco"], p["wao"],
                                           p["ng"], p["who"], p["wout"], t_end)
    x2, pre = _ffn_fwd(x1, p["g2"], p["w1"], p["w2"])
    saved = dict(x=x, gates=gates, ag=ag, h4=h4, cq=cq, ckv=ckv, kr=kr, hb=hb, z=z, q=q, k=k, v=v, cqn=cqn,
                 ckvn=ckvn, o_att=o_att, lse=lse, o_h=o_h, states=states, x1=x1, mix=mix, ca=ca, oc=oc,
                 ya=ya, yb=yb, yc=yc, pre=pre)
    return x2, saved


def _layer_bwd(dx2, s, p, lb, rope, t_end, rides=None):
    rides = rides or {}
    (dx1, h2, act, dp, dg2), rode = _ffn_bwd(dx2, s["x1"], s["pre"], p["g2"], p["w1"].T, p["w2"].T,
                                             rides.get("ffn"))
    g = {"w_ff1": _wgrad(h2, dp, "ff1", 4), "w_ff2": _wgrad(act, dx2, "ff2").reshape(4, D_FF // 4, D),
         "norm2_g": dg2.sum(0)}
    plan_mix = rides["mix"](rode, g) if "mix" in rides else None
    (dgt, dya, dyb, dyc, dz, doa, doh, dhg, dlng, dlnb, dcb, dng), rode = _mix_bwd(
        dx1, s["ya"], s["yb"], s["yc"], s["gates"], s["z"], s["o_h"], s["h4"], p["lng"], p["lnb"], p["ng"],
        p["wout"], p["wco"], p["wao"], p["who"], plan_mix)
    g["w_out"] = _wgrad(s["mix"], dx1, "out").reshape(4, D // 4, D)
    g["w_conv_out"] = _wgrad(s["ca"], dya, "conv_out", 4)
    g["w_attn_out"] = _unpad_heads(_wgrad(s["o_att"], dyb, "attn_out", 4), NH, 64, 1)
    g["w_hgrn_out"] = _wgrad(s["oc"], dyc, "hgrn_out", 4)
    g["conv_ln_g"], g["conv_ln_b"], g["conv_b"], g["hgrn_norm_g"] = dlng.sum(0), dlnb.sum(0), dcb.sum(0), dng.sum(0)
    da, dg, dcw = _conv_bwd(s["ag"], p["cw"], dz)
    g["conv_w"] = dcw[:CONV_K]
    plan_attn = rides["attn"](rode, g) if "attn" in rides else None
    (dq, dk, dv), rode_attn = _attn_bwd(s["q"], s["k"], s["v"], s["o_att"], s["lse"], doa, plan_attn)
    dcq, dckv, dkr, dqraw, dkraw, dqag, dkvag, dqng, dkng = _mla_pre_bwd(
        dq, dk, dv, s["cq"], s["ckv"], s["kr"], p["qag"], p["wuq"], p["kvag"], p["wk"], p["wv"], p["qng"],
        p["kng"], *rope)
    g["w_uq"] = _unpad_heads(_wgrad(s["cqn"], dqraw, "uq", 4), 2, QK_DIM, 2)
    dwk = _unpad_heads(_wgrad(s["ckvn"], dkraw, "uk"), NH, 64, 1).reshape(128, NH, 64)
    dwv = _unpad_heads(_wgrad(s["ckvn"], dv, "uv"), NH, 64, 1).reshape(128, NH, 64)
    g["w_ukv"] = jnp.concatenate([dwk, dwv], axis=2).reshape(128, 4, 256).transpose(1, 0, 2)
    g["q_a_norm_g"], g["kv_a_norm_g"] = dqag.sum(0), dkvag.sum(0)
    g["q_norm_g"], g["k_norm_g"] = dqng.sum(0)[:QK_DIM], dkng.sum(0)[:QK_DIM]
    plan_hgrn = rides["hgrn"](rode_attn) if "hgrn" in rides else None
    (dhq, dhf, dhi, dlb), rode_hgrn = _hgrn_bwd(s["h4"], lb, doh, s["states"], plan_hgrn)
    mm = lambda a: a.astype(_MM)
    du = jnp.concatenate([dgt, mm(da), mm(dg), mm(dhq), mm(dhf), mm(dhi), dhg, dcq, dckv, dkr], axis=1)
    dx, dg1 = _in_proj_bwd(du, s["x"], dx1, p["g1"], p["w_in_t"], t_end)
    g["norm1_g"] = dg1.sum(0)
    g["w_in"] = _w_in_grad_to_chips(_wgrad(s["hb"], du, "in"))
    return dx, g, dlb.sum(0), (rode_attn, rode_hgrn)


def _device_step(x, target, small, pieces0, pieces1=None, fwd_ride=None, bwd_rides=None):
    s_real = x.shape[0]
    t_end = ROW0 + s_real
    t = -(-t_end // LANE) * LANE
    zrow = lambda n: jnp.zeros((n, D), F32)
    xp = jnp.concatenate([zrow(FRONT), small["meta"].astype(F32), x, zrow(t - t_end)], axis=0)
    tp = jnp.concatenate([zrow(ROW0), target, zrow(t - t_end)], axis=0)
    rope = _rope_tables(t)
    logits = small["hgrn_lb_logits"].astype(F32)
    lbs = _lower_bounds_fwd(logits)
    prm0 = _prep_early(pieces0, small, 0)
    got = {}
    if fwd_ride is None:
        prm0.update(_prep_late(pieces0))
        h, sv0 = _layer_fwd(xp, prm0, lbs[0:1], rope, t_end)
    else:
        def on_rode(rode):
            late0, got["pieces1"] = fwd_ride[1](rode)
            prm0.update(_prep_late(late0))
        h, sv0 = _layer_fwd(xp, prm0, lbs[0:1], rope, t_end, fwd_ride[0], on_rode)
        pieces1 = got["pieces1"]
    prm1 = _prep_early(pieces1, small, 1)
    if fwd_ride is None:
        prm1.update(_prep_late(pieces1))
        h, sv1 = _layer_fwd(h, prm1, lbs[1:2], rope, t_end)
    else:
        h, sv1 = _layer_fwd(h, prm1, lbs[1:2], rope, t_end, fwd_ride[2],
                            lambda rode: prm1.update(_prep_late(fwd_ride[3](rode))))
    dh, lsum = _loss_head(h, tp, t_end)
    loss = jnp.sum(lsum)
    dh, g1, dlb1, _ = _layer_bwd(dh, sv1, prm1, lbs[1:2], rope, t_end)
    dh, g0, dlb0, rode = _layer_bwd(dh, sv0, prm0, lbs[0:1], rope, t_end,
                                    None if bwd_rides is None else bwd_rides(g1))
    dlogits = _lower_bounds_bwd(logits, jnp.stack([dlb0, dlb1]))
    grads = [g0, g1]
    for l in range(DEPTH):
        grads[l]["hgrn_lb_logits"] = dlogits[l]
    return loss, dh[ROW0:t_end], grads, dh[FRONT:ROW0], rode


MESH = pl.DeviceIdType.MESH
_ANY = pl.BlockSpec(memory_space=pl.ANY)
SMALL_ROWS = 64
SMALL_LEN = SMALL_ROWS * 1024


def _mesh_pos():
    return lax.axis_index("x"), lax.axis_index("y"), lax.axis_index("c")


def _other_chips(x, y):
    return [(1 - x, y), (x, 1 - y), (1 - x, 1 - y)]


class _Plan:
    def __init__(self, name, ins, out_shapes, sems, start, finish):
        self.name, self.ins, self.out_shapes, self.sems = name, list(ins), list(out_shapes), list(sems)
        self.start, self.finish = start, finish


def _run_plan(plan):
    ni, no = len(plan.ins), len(plan.out_shapes)

    def body(*refs):
        ins, outs, sems = refs[:ni], refs[ni:ni + no], refs[ni + no:]
        plan.start(ins, outs, sems)
        plan.finish(ins, outs, sems)

    return pl.pallas_call(body, name=plan.name, in_specs=[_ANY] * ni, out_specs=[_ANY] * no,
                          out_shape=plan.out_shapes, scratch_shapes=plan.sems)(*plan.ins)


def _plan_specs(plan):
    if plan is None:
        return [], [], [], [], []
    return plan.ins, [_ANY] * len(plan.ins), [_ANY] * len(plan.out_shapes), plan.out_shapes, plan.sems


def _host_refs(refs, n_in, n_out, n_scratch, plan):
    ni = 0 if plan is None else len(plan.ins)
    no = 0 if plan is None else len(plan.out_shapes)
    o0 = n_in + ni
    s0 = o0 + n_out + no
    own = (refs[:n_in], refs[o0:o0 + n_out], refs[s0:s0 + n_scratch])
    rider = (refs[n_in:o0], refs[o0 + n_out:s0], refs[s0 + n_scratch:])
    return own, rider


def _ride(plan, rider, step, last):
    if plan is None:
        return lambda: None

    @pl.when(step == 0)
    def _():
        plan.start(*rider)

    def done():
        @pl.when(step == last)
        def _():
            plan.finish(*rider)
    return done


def _merge_plans(name, plans):
    def parts(ins, outs, sems):
        i = o = s = 0
        for p in plans:
            ni, no, ns = len(p.ins), len(p.out_shapes), len(p.sems)
            yield p, (ins[i:i + ni], outs[o:o + no], sems[s:s + ns])
            i, o, s = i + ni, o + no, s + ns

    def start(ins, outs, sems):
        for p, refs in parts(ins, outs, sems):
            p.start(*refs)

    def finish(ins, outs, sems):
        for p, refs in parts(ins, outs, sems):
            p.finish(*refs)

    return _Plan(name, [a for p in plans for a in p.ins], [a for p in plans for a in p.out_shapes],
                 [a for p in plans for a in p.sems], start, finish)


def _plan_gather(own, layer, name):
    nw = len(own)

    def copies(ins, outs, sems):
        send_sems, recv_sems = sems

        def over_ici(w, j, chip_of_data, to):
            return pltpu.make_async_remote_copy(
                src_ref=ins[w].at[layer], dst_ref=outs[w].at[chip_of_data], send_sem=send_sems.at[w, j],
                recv_sem=recv_sems.at[w, j], device_id=to, device_id_type=MESH)

        def over_d2d(w, j, chip_of_data, to):
            return pltpu.make_async_remote_copy(
                src_ref=outs[w].at[chip_of_data], dst_ref=outs[w].at[chip_of_data], send_sem=send_sems.at[w, 3 + j],
                recv_sem=recv_sems.at[w, 3 + j], device_id=to, device_id_type=MESH)
        return over_ici, over_d2d

    def start(ins, outs, sems):
        x, y, c = _mesh_pos()
        over_ici, _ = copies(ins, outs, sems)

        @pl.when(c == layer)
        def _():
            for j, (px, py) in enumerate(_other_chips(x, y)):
                for w in range(nw):
                    over_ici(w, j, 2 * x + y, (px, py, layer)).start()

    def finish(ins, outs, sems):
        x, y, c = _mesh_pos()
        over_ici, over_d2d = copies(ins, outs, sems)
        chips = _other_chips(x, y)

        @pl.when(c == layer)
        def _():
            for j, (px, py) in enumerate(chips):
                for w in range(nw):
                    over_ici(w, j, 2 * px + py, (x, y, c)).wait_recv()
                    over_d2d(w, j, 2 * px + py, (x, y, 1 - layer)).start()
            for j, (px, py) in enumerate(chips):
                for w in range(nw):
                    over_ici(w, j, 2 * x + y, (px, py, layer)).wait_send()
                    over_d2d(w, j, 2 * px + py, (x, y, 1 - layer)).wait_send()

        @pl.when(c != layer)
        def _():
            for j, (px, py) in enumerate(chips):
                for w in range(nw):
                    over_d2d(w, j, 2 * px + py, (x, y, c)).wait_recv()

    return _Plan(name, own,
                 [jax.ShapeDtypeStruct((4,) + a.shape[1:], a.dtype) for a in own],
                 [pltpu.SemaphoreType.DMA((nw, 6)), pltpu.SemaphoreType.DMA((nw, 6))], start, finish)


def _plan_to_sibling(arrs, layer, name):
    nw = len(arrs)

    def copy(ins, outs, sems, w):
        x, y, _ = _mesh_pos()
        return pltpu.make_async_remote_copy(src_ref=ins[w], dst_ref=outs[w], send_sem=sems[0].at[w],
                                            recv_sem=sems[1].at[w], device_id=(x, y, layer), device_id_type=MESH)

    def start(ins, outs, sems):
        @pl.when(lax.axis_index("c") != layer)
        def _():
            for w in range(nw):
                copy(ins, outs, sems, w).start()

    def finish(ins, outs, sems):
        c = lax.axis_index("c")

        @pl.when(c != layer)
        def _():
            for w in range(nw):
                copy(ins, outs, sems, w).wait_send()

        @pl.when(c == layer)
        def _():
            for w in range(nw):
                copy(ins, outs, sems, w).wait_recv()

    return _Plan(name, arrs, [jax.ShapeDtypeStruct(a.shape, a.dtype) for a in arrs],
                 [pltpu.SemaphoreType.DMA((nw,)), pltpu.SemaphoreType.DMA((nw,))], start, finish)


def _plan_scatter(parts, layer, name):
    nw = len(parts)

    def start(ins, outs, sems):
        x, y, c = _mesh_pos()

        @pl.when(c == layer)
        def _():
            for j, (px, py) in enumerate(_other_chips(x, y)):
                for w in range(nw):
                    pltpu.make_async_remote_copy(
                        src_ref=ins[w].at[2 * px + py], dst_ref=outs[w].at[2 * x + y], send_sem=sems[0].at[w, j],
                        recv_sem=sems[1].at[w, j], device_id=(px, py, layer), device_id_type=MESH).start()

    def finish(ins, outs, sems):
        x, y, c = _mesh_pos()

        @pl.when(c == layer)
        def _():
            for j, (px, py) in enumerate(_other_chips(x, y)):
                for w in range(nw):
                    pltpu.make_async_remote_copy(
                        src_ref=ins[w].at[2 * px + py], dst_ref=outs[w].at[2 * px + py], send_sem=sems[0].at[w, j],
                        recv_sem=sems[1].at[w, j], device_id=(x, y, c), device_id_type=MESH).wait()

    return _Plan(name, parts, [jax.ShapeDtypeStruct(a.shape, a.dtype) for a in parts],
                 [pltpu.SemaphoreType.DMA((nw, 3)), pltpu.SemaphoreType.DMA((nw, 3))], start, finish)


def _sibling_exchange(reds0, reds1):
    nw = len(reds0)

    def body(*refs):
        a0, a1, outs = refs[:nw], refs[nw:2 * nw], refs[2 * nw:3 * nw]
        send_sems, recv_sems = refs[3 * nw:]
        x, y, c = _mesh_pos()

        def copy(w, src):
            return pltpu.make_async_remote_copy(src_ref=src, dst_ref=outs[w], send_sem=send_sems.at[w],
                                                recv_sem=recv_sems.at[w], device_id=(x, y, 1 - c),
                                                device_id_type=MESH)

        @pl.when(c == 0)
        def _():
            for w in range(nw):
                copy(w, a0[w]).start()

        @pl.when(c == 1)
        def _():
            for w in range(nw):
                copy(w, a1[w]).start()

        for w in range(nw):
            copy(w, a0[w]).wait()

    return pl.pallas_call(
        body, name="sibling_exchange", in_specs=[_ANY] * (2 * nw), out_specs=[_ANY] * nw,
        out_shape=[jax.ShapeDtypeStruct(a.shape, a.dtype) for a in reds0],
        scratch_shapes=[pltpu.SemaphoreType.DMA((nw,)), pltpu.SemaphoreType.DMA((nw,))],
    )(*reds0, *reds1)


def _all_reduce_small(v, name):
    rows, cols = v.shape

    def body(v_ref, o_ref, slots, send_sems, recv_sems):
        x, y, c = _mesh_pos()
        me = 4 * x + 2 * y + c
        slots[me] = v_ref[...]
        peers = []
        for rel in range(1, 8):
            fx, fy, fc = (rel >> 2) & 1, (rel >> 1) & 1, rel & 1
            px = 1 - x if fx else x
            py = 1 - y if fy else y
            pc = 1 - c if fc else c
            peers.append((px, py, pc))
        cps = [pltpu.make_async_remote_copy(src_ref=v_ref, dst_ref=slots.at[me], send_sem=send_sems.at[k],
                                            recv_sem=recv_sems.at[k], device_id=peer, device_id_type=MESH)
               for k, peer in enumerate(peers)]
        for cp in cps:
            cp.start()
        for k, (px, py, pc) in enumerate(peers):
            pltpu.make_async_remote_copy(src_ref=v_ref, dst_ref=slots.at[4 * px + 2 * py + pc],
                                         send_sem=send_sems.at[k], recv_sem=recv_sems.at[k], device_id=(x, y, c),
                                         device_id_type=MESH).wait_recv()
        for cp in cps:
            cp.wait_send()
        acc = slots[0]
        for d in range(1, 8):
            acc = acc + slots[d]
        o_ref[...] = acc

    vm = pl.BlockSpec(memory_space=pltpu.VMEM)
    return pl.pallas_call(
        body, name=name, in_specs=[vm], out_specs=vm,
        out_shape=jax.ShapeDtypeStruct((rows, cols), F32),
        scratch_shapes=[pltpu.VMEM((8, rows, cols), F32), pltpu.SemaphoreType.DMA((7,)),
                        pltpu.SemaphoreType.DMA((7,))],
    )(v)


def _add_to_wire(a, b, name):
    n4, r, c_ = a.shape
    rows = n4 * r
    tr = _ew_tile(rows, c_, 3)

    def body(a_ref, b_ref, o_ref):
        o_ref[...] = (a_ref[...] + b_ref[...]).astype(o_ref.dtype)

    spec = pl.BlockSpec((tr, c_), lambda i: (i, 0))
    out = pl.pallas_call(
        body, name="add_to_wire_" + name, grid=(rows // tr,), in_specs=[spec, spec], out_specs=spec,
        out_shape=jax.ShapeDtypeStruct((rows, c_), jnp.bfloat16), compiler_params=_cp("parallel"),
    )(a.reshape(rows, c_), b.reshape(rows, c_))
    return out.reshape(n4, r, c_)


def _sum_chips(recv, own, name):
    _, r, c_ = recv.shape
    tr = _ew_tile(r, c_, 4)

    def body(r_ref, own_ref, o_ref):
        chip = 2 * lax.axis_index("x") + lax.axis_index("y")
        own_v = own_ref[...].astype(F32)
        acc = None
        for s in range(4):
            term = jnp.where(chip == s, own_v, r_ref[s].astype(F32))
            acc = term if acc is None else acc + term
        o_ref[...] = acc

    return pl.pallas_call(
        body, name="sum_chips_" + name, grid=(r // tr,),
        in_specs=[pl.BlockSpec((4, tr, c_), lambda i: (0, i, 0)), pl.BlockSpec((tr, c_), lambda i: (i, 0))],
        out_specs=pl.BlockSpec((tr, c_), lambda i: (i, 0)),
        out_shape=jax.ShapeDtypeStruct((r, c_), F32),
        compiler_params=_cp("parallel"),
    )(recv, own)


def _pack_small(vals, meta_full, conv_w_full):
    flat = jnp.concatenate([vals[k].reshape(-1) for k in SMALL] + [meta_full.reshape(-1), conv_w_full.reshape(-1)])
    return jnp.pad(flat, (0, SMALL_LEN - flat.shape[0])).reshape(SMALL_ROWS, 1024)


def _unpack_small(buf):
    flat = buf.reshape(-1)
    out, off = {}, 0
    for k in SMALL:
        n = DEPTH * SMALL_SIZES[k]
        out[k] = flat[off:off + n].reshape(DEPTH, SMALL_SIZES[k])
        off += n
    meta = flat[off:off + N_META * D].reshape(N_META, D)
    off += N_META * D
    conv_w = flat[off:off + DEPTH * CONV_K * CONV_DIM].reshape(DEPTH, CONV_K, CONV_DIM)
    return out, meta, conv_w


def kernel(x, meta, norm1_g, w_in, conv_w, conv_b, conv_ln_g, conv_ln_b, w_conv_out, q_a_norm_g, w_uq, kv_a_norm_g, w_ukv, q_norm_g, k_norm_g, w_attn_out, hgrn_lb_logits, hgrn_norm_g, w_hgrn_out, w_out, norm2_g, w_ff1, w_ff2, loss_target, m_meta, m_norm1_g, m_w_in, m_conv_w, m_conv_b, m_conv_ln_g, m_conv_ln_b, m_w_conv_out, m_q_a_norm_g, m_w_uq, m_kv_a_norm_g, m_w_ukv, m_q_norm_g, m_k_norm_g, m_w_attn_out, m_hgrn_lb_logits, m_hgrn_norm_g, m_w_hgrn_out, m_w_out, m_norm2_g, m_w_ff1, m_w_ff2, v_meta, v_norm1_g, v_w_in, v_conv_w, v_conv_b, v_conv_ln_g, v_conv_ln_b, v_w_conv_out, v_q_a_norm_g, v_w_uq, v_kv_a_norm_g, v_w_ukv, v_q_norm_g, v_k_norm_g, v_w_attn_out, v_hgrn_lb_logits, v_hgrn_norm_g, v_w_hgrn_out, v_w_out, v_norm2_g, v_w_ff1, v_w_ff2):
    names = ("meta", "norm1_g", "w_in", "conv_w", "conv_b", "conv_ln_g", "conv_ln_b", "w_conv_out", "q_a_norm_g",
             "w_uq", "kv_a_norm_g", "w_ukv", "q_norm_g", "k_norm_g", "w_attn_out", "hgrn_lb_logits", "hgrn_norm_g",
             "w_hgrn_out", "w_out", "norm2_g", "w_ff1", "w_ff2")
    w = dict(zip(names, (meta, norm1_g, w_in, conv_w, conv_b, conv_ln_g, conv_ln_b, w_conv_out, q_a_norm_g, w_uq,
                         kv_a_norm_g, w_ukv, q_norm_g, k_norm_g, w_attn_out, hgrn_lb_logits, hgrn_norm_g, w_hgrn_out,
                         w_out, norm2_g, w_ff1, w_ff2)))
    m = dict(zip(names, (m_meta, m_norm1_g, m_w_in, m_conv_w, m_conv_b, m_conv_ln_g, m_conv_ln_b, m_w_conv_out,
                         m_q_a_norm_g, m_w_uq, m_kv_a_norm_g, m_w_ukv, m_q_norm_g, m_k_norm_g, m_w_attn_out,
                         m_hgrn_lb_logits, m_hgrn_norm_g, m_w_hgrn_out, m_w_out, m_norm2_g, m_w_ff1, m_w_ff2)))
    v = dict(zip(names, (v_meta, v_norm1_g, v_w_in, v_conv_w, v_conv_b, v_conv_ln_g, v_conv_ln_b, v_w_conv_out,
                         v_q_a_norm_g, v_w_uq, v_kv_a_norm_g, v_w_ukv, v_q_norm_g, v_k_norm_g, v_w_attn_out,
                         v_hgrn_lb_logits, v_hgrn_norm_g, v_w_hgrn_out, v_w_out, v_norm2_g, v_w_ff1, v_w_ff2)))
    cx, cy, cc = _mesh_pos()
    chip = 2 * cx + cy
    zero = jnp.zeros((), jnp.int32)

    own = {k: w[k].astype(_MM) for k in BIG}

    def as_pieces(names, gathered, layer):
        return {k: [jnp.where(chip == s, own[k][layer], g[s]) for s in range(4)] for k, g in zip(names, gathered)}

    pieces0 = as_pieces(EARLY, _run_plan(_plan_gather([own[k] for k in EARLY], 0, "gather_l0_early")), 0)
    fwd_ride = (_merge_plans("gather_mid", [_plan_gather([own[k] for k in LATE], 0, "gather_l0_late"),
                                            _plan_gather([own[k] for k in EARLY], 1, "gather_l1_early")]),
                lambda got: (as_pieces(LATE, got[:len(LATE)], 0), as_pieces(EARLY, got[len(LATE):], 1)),
                _plan_gather([own[k] for k in LATE], 1, "gather_l1_late"),
                lambda got: as_pieces(LATE, got, 1))
    meta_slab = lax.dynamic_update_slice(jnp.zeros((N_META, D), F32), meta, (zero, chip * (D // 4)))
    convw_slab = lax.dynamic_update_slice(jnp.zeros((DEPTH, CONV_K, CONV_DIM), F32), conv_w,
                                          (zero, zero, chip * (CONV_DIM // 4)))
    zsmall = {k: jnp.zeros((DEPTH, SMALL_SIZES[k]), F32) for k in SMALL}
    south = (cc == 0).astype(F32)
    _, meta_full, convw_full = _unpack_small(
        _all_reduce_small(_pack_small(zsmall, meta_slab, convw_slab) * south, "gather_small"))
    small = {k: w[k] for k in SMALL}
    small["meta"] = meta_full
    small["conv_w"] = convw_full

    FFN = ("w_ff1", "w_ff2")
    MID = ("w_out", "w_conv_out", "w_attn_out", "w_hgrn_out")
    REST = tuple(k for k in BIG if k not in FFN + MID)
    held = {}

    def to_wire(names, layer, mine, from_sibling):
        return lax.cond(
            cc == layer,
            lambda: [_add_to_wire(a, b, "%s_l%d" % (k, layer)) for k, a, b in zip(names, mine, from_sibling)],
            lambda: [jnp.zeros(a.shape, jnp.bfloat16) for a in mine])

    def chip_sum(names, layer, got, wire):
        return lax.cond(
            cc == layer,
            lambda: [_sum_chips(r, lax.dynamic_index_in_dim(s, chip, 0, keepdims=False), "%s_l%d" % (k, layer))
                     for k, r, s in zip(names, got, wire)],
            lambda: [jnp.zeros(s.shape[1:], F32) for s in wire])

    def bwd_rides(g1):
        held["g1"] = [g1[k] for k in BIG]

        def ride_mix(from_sibling1, g0_ffn):
            held["wire1"] = to_wire(BIG, 1, held["g1"], from_sibling1)
            held["g0_ffn"] = [g0_ffn[k] for k in FFN]
            return _plan_to_sibling(held["g0_ffn"], 0, "swap_grads_l0_ffn")

        def ride_attn(from_sibling0, g0):
            held["wire0_ffn"] = to_wire(FFN, 0, held["g0_ffn"], from_sibling0)
            held["g0_mid"] = [g0[k] for k in MID]
            return _merge_plans("exchange_grads_mid", [
                _plan_scatter(held["wire1"], 1, "scatter_grads_l1"),
                _plan_scatter(held["wire0_ffn"], 0, "scatter_grads_l0_ffn"),
                _plan_to_sibling(held["g0_mid"], 0, "swap_grads_l0_mid")])

        def ride_hgrn(rode_attn):
            held["wire0_mid"] = to_wire(MID, 0, held["g0_mid"], rode_attn[len(BIG) + len(FFN):])
            return _plan_scatter(held["wire0_mid"], 0, "scatter_grads_l0_mid")

        return {"ffn": _plan_to_sibling(held["g1"], 1, "swap_grads_l1"), "mix": ride_mix, "attn": ride_attn,
                "hgrn": ride_hgrn}

    loss_share, grad_x, gl, g_meta, (got, got_mid) = _device_step(x[0], loss_target[0], small, pieces0, None,
                                                                  fwd_ride, bwd_rides)

    reds1 = chip_sum(BIG, 1, got[:len(BIG)], held["wire1"])
    reds0 = dict(zip(FFN, chip_sum(FFN, 0, got[len(BIG):len(BIG) + len(FFN)], held["wire0_ffn"])))
    reds0.update(zip(MID, chip_sum(MID, 0, got_mid, held["wire0_mid"])))
    g0_rest = [gl[0][k] for k in REST]
    wire0 = to_wire(REST, 0, g0_rest, _run_plan(_plan_to_sibling(g0_rest, 0, "swap_grads_l0_rest")))
    reds0.update(zip(REST, chip_sum(REST, 0, _run_plan(_plan_scatter(wire0, 0, "scatter_grads_l0_rest")), wire0)))
    reds0 = [reds0[k] for k in BIG]
    reds_sibling = _sibling_exchange(reds0, reds1)
    grads, delta, new_m, new_v = {}, {}, {}, {}
    for k, r0, r1, theirs in zip(BIG, reds0, reds1, reds_sibling):
        grads[k], delta[k], new_m[k], new_v[k] = _adamw_layers(w[k], m[k], v[k], r0, r1, theirs, k)

    g_small_local = {k: jnp.stack([gl[l][k] for l in range(DEPTH)]) for k in SMALL}
    g_convw_local = jnp.stack([gl[l]["conv_w"] for l in range(DEPTH)])
    reduced = _all_reduce_small(
        _pack_small(g_small_local, g_meta, g_convw_local).at[SMALL_ROWS - 1, 1023].set(loss_share), "reduce_small")
    loss = reduced[SMALL_ROWS - 1, 1023]
    g_small, g_meta_full, g_convw_full = _unpack_small(reduced)
    grads.update(g_small)
    grads["meta"] = lax.dynamic_slice(g_meta_full, (zero, chip * (D // 4)), (N_META, D // 4))
    grads["conv_w"] = lax.dynamic_slice(g_convw_full, (zero, zero, chip * (CONV_DIM // 4)),
                                        (DEPTH, CONV_K, CONV_DIM // 4))

    def small_pack(src):
        return _pack_small(src, jnp.pad(src["meta"], ((0, 0), (0, D - D // 4))),
                           jnp.pad(src["conv_w"], ((0, 0), (0, 0), (0, CONV_DIM - CONV_DIM // 4))))

    def small_unpack(buf):
        out, meta_p, convw_p = _unpack_small(buf)
        out["meta"] = meta_p[:, :D // 4]
        out["conv_w"] = convw_p[:, :, :CONV_DIM // 4]
        return out

    d_s, m_s, v_s = [small_unpack(a) for a in _adamw(small_pack(w), small_pack(grads), small_pack(m),
                                                     small_pack(v), "small")]
    delta.update(d_s)
    new_m.update(m_s)
    new_v.update(v_s)
    return (loss, grad_x[None], *[grads[k] for k in names], *[delta[k] for k in names],
            *[new_m[k] for k in names], *[new_v[k] for k in names])
```

```python
import functools

import jax
import jax.numpy as jnp
from jax import lax
from jax.experimental import pallas as pl
from jax.experimental.pallas import tpu as pltpu

F32 = jnp.float32
_MM = jnp.bfloat16

D = 1024
N_META = 16
FRONT = 48
ROW0 = FRONT + N_META
EPS = 1e-6
GATE_CLAMP = 1.0 - 1e-6
CONV_K = 31
CONV_DIM = 512
NH = 8
QK_DIM = 96
ATT_SCALE = QK_DIM ** -0.5
HH = 4
CHUNK = 64
SUB = 16
EXP_CLIP = 60.0
NEG = -1e30
LANE = 128

SEG_GATES = (0, 3072)
SEG_AG = (3072, 4096)
SEG_H4 = (4096, 6144)
SEG_CQ = (6144, 6400)
SEG_CKV = (6400, 6528)
SEG_KR = (6528, 6656)
N_IN_P = 6656

ADAM_LR = 0.001
ADAM_B1 = 0.9
ADAM_B2 = 0.999
ADAM_EPS = 1e-08
ADAM_WD = 0.01
ADAM_STEP = 10

VMEM_LIMIT = 56 * 1024 * 1024


def _tile(n, pref):
    best = 64
    for t in range(64, pref + 1, 64):
        if n % t == 0:
            best = t
    return best


def _cp(*sem):
    return pltpu.CompilerParams(dimension_semantics=tuple(sem), vmem_limit_bytes=VMEM_LIMIT)


def _row(tm, n, col=0):
    return pl.BlockSpec((tm, n), lambda i: (i, col))


def _full(shape):
    return pl.BlockSpec(shape, lambda i: (0,) * len(shape))


def _mm(a, b):
    return jnp.dot(a.astype(_MM), b.astype(_MM), preferred_element_type=F32)


def _mm_nt(a, b):
    return lax.dot_general(a.astype(_MM), b.astype(_MM), (((1,), (1,)), ((), ())), preferred_element_type=F32)


def _mm_tn(a, b):
    return lax.dot_general(a.astype(_MM), b.astype(_MM), (((0,), (0,)), ((), ())), preferred_element_type=F32)


def _split3(x):
    hi = x.astype(jnp.bfloat16)
    return hi, (x - hi.astype(F32)).astype(jnp.bfloat16)


def _dot3(a, b, dims):
    ah, al = _split3(a)
    bh, bl = _split3(b)
    dg = lambda u, v: lax.dot_general(u, v, (dims, ((), ())), preferred_element_type=F32)
    return dg(ah, bh) + (dg(ah, bl) + dg(al, bh))


def _hmm(a, b):
    return _dot3(a, b, ((1,), (0,)))


def _hmm_nt(a, b):
    return _dot3(a, b, ((1,), (1,)))


def _hmm_tn(a, b):
    return _dot3(a, b, ((0,), (0,)))


def _sigmoid(x):
    return 1.0 / (1.0 + jnp.exp(-x))


def _rstd(x, n=None):
    n = x.shape[-1] if n is None else n
    return lax.rsqrt(jnp.sum(x * x, axis=-1, keepdims=True) * (1.0 / n) + EPS)


def _rms_bwd(dy, x, rstd, g, n=None):
    n = x.shape[-1] if n is None else n
    xh = x * rstd
    dxh = dy * g
    dx = rstd * (dxh - xh * (jnp.sum(dxh * xh, axis=-1, keepdims=True) * (1.0 / n)))
    return dx, dy * xh


def _valid_rows(i, tm, t_valid_end):
    r = i * tm + lax.broadcasted_iota(jnp.int32, (tm, 1), 0)
    return ((r >= FRONT) & (r < t_valid_end)).astype(F32)


def _colsum8(x):
    n, c = x.shape
    return jnp.sum(x.reshape(n // 8, 8, c), axis=0)


def _in_proj_fwd(x, g1, w):
    t = x.shape[0]
    tm = _tile(t, 192)
    segs = (SEG_GATES, SEG_AG, SEG_H4, SEG_CQ, SEG_CKV, SEG_KR)

    def body(x_ref, g_ref, w_ref, gates_ref, ag_ref, h4_ref, cq_ref, ckv_ref, kr_ref, hb_ref):
        xv = x_ref[...]
        hb = (xv * _rstd(xv) * g_ref[...]).astype(_MM)
        hb_ref[...] = hb
        for ref, (a, b) in zip((gates_ref, ag_ref, h4_ref, cq_ref, ckv_ref, kr_ref), segs):
            ref[...] = jnp.dot(hb, w_ref[:, a:b], preferred_element_type=F32)

    outs = [jax.ShapeDtypeStruct((t, b - a), F32) for a, b in segs] + [jax.ShapeDtypeStruct((t, D), _MM)]
    return pl.pallas_call(
        body, name="in_proj_fwd", grid=(t // tm,),
        in_specs=[_row(tm, D), _full((1, D)), _full((D, N_IN_P))],
        out_specs=[_row(tm, b - a) for a, b in segs] + [_row(tm, D)],
        out_shape=outs, compiler_params=_cp("parallel"),
    )(x, g1, w)


def _in_proj_bwd(du, x, dx1, g1, wt, t_end):
    t = x.shape[0]
    tm = _tile(t, 192)

    def body(du_ref, x_ref, dx1_ref, g_ref, wt_ref, dx_ref, dg_ref):
        i = pl.program_id(0)
        dh = jnp.dot(du_ref[...], wt_ref[...], preferred_element_type=F32)
        xv = x_ref[...]
        dxn, dgrow = _rms_bwd(dh, xv, _rstd(xv), g_ref[...])
        dx_ref[...] = _valid_rows(i, tm, t_end) * (dx1_ref[...] + dxn)

        @pl.when(i == 0)
        def _():
            dg_ref[...] = jnp.zeros_like(dg_ref)
        dg_ref[...] += _colsum8(dgrow)

    return pl.pallas_call(
        body, name="in_proj_bwd", grid=(t // tm,),
        in_specs=[_row(tm, N_IN_P), _row(tm, D), _row(tm, D), _full((1, D)), _full((N_IN_P, D))],
        out_specs=[_row(tm, D), _full((8, D))],
        out_shape=[jax.ShapeDtypeStruct((t, D), F32), jax.ShapeDtypeStruct((8, D), F32)],
        compiler_params=_cp("arbitrary"),
    )(du, x, dx1, g1, wt)


CONV_CH = 128


def _conv_fwd(ag, cw, cb):
    t = ag.shape[0]
    n = t // CONV_CH

    def body(a_ref, g_ref, w_ref, b_ref, z_ref, hp):
        hp[0:32, :] = jnp.zeros((32, LANE), F32)

        def fill(i, c):
            r = pl.multiple_of(i * CONV_CH, CONV_CH)
            hp[pl.ds(32 + r, CONV_CH), :] = a_ref[pl.ds(r, CONV_CH), :] * _sigmoid(g_ref[pl.ds(r, CONV_CH), :])
            return c
        lax.fori_loop(0, n, fill, 0)

        def conv(i, c):
            r = pl.multiple_of(i * CONV_CH, CONV_CH)
            acc = jnp.broadcast_to(b_ref[...], (CONV_CH, LANE))
            for k in range(CONV_K):
                acc = acc + w_ref[k:k + 1, :] * hp[pl.ds(r + (k + 2), CONV_CH), :]
            z_ref[pl.ds(r, CONV_CH), :] = acc
            return c
        lax.fori_loop(0, n, conv, 0)

    nb = CONV_DIM // LANE
    return pl.pallas_call(
        body, name="conv_fwd", grid=(nb,),
        in_specs=[pl.BlockSpec((t, LANE), lambda j: (0, j)), pl.BlockSpec((t, LANE), lambda j: (0, nb + j)),
                  pl.BlockSpec((32, LANE), lambda j: (0, j)), pl.BlockSpec((1, LANE), lambda j: (0, j))],
        out_specs=pl.BlockSpec((t, LANE), lambda j: (0, j)),
        out_shape=jax.ShapeDtypeStruct((t, CONV_DIM), F32),
        scratch_shapes=[pltpu.VMEM((t + 32, LANE), F32)],
        compiler_params=_cp("parallel"),
    )(ag, ag, cw, cb)


def _conv_bwd(ag, cw, dz):
    t = ag.shape[0]
    n = t // CONV_CH

    def body(a_ref, g_ref, w_ref, dz_ref, da_ref, dg_ref, dcw_ref, hp, dzp, accw):
        hp[0:32, :] = jnp.zeros((32, LANE), F32)
        dzp[pl.ds(t, 32), :] = jnp.zeros((32, LANE), F32)
        accw[...] = jnp.zeros_like(accw)

        def fill(i, c):
            r = pl.multiple_of(i * CONV_CH, CONV_CH)
            hp[pl.ds(32 + r, CONV_CH), :] = a_ref[pl.ds(r, CONV_CH), :] * _sigmoid(g_ref[pl.ds(r, CONV_CH), :])
            dzp[pl.ds(r, CONV_CH), :] = dz_ref[pl.ds(r, CONV_CH), :]
            return c
        lax.fori_loop(0, n, fill, 0)

        def step(i, c):
            r = pl.multiple_of(i * CONV_CH, CONV_CH)
            dzc = dz_ref[pl.ds(r, CONV_CH), :]
            dh = jnp.zeros((CONV_CH, LANE), F32)
            for k in range(CONV_K):
                dh = dh + w_ref[k:k + 1, :] * dzp[pl.ds(r + (CONV_K - 1 - k), CONV_CH), :]
                accw[8 * k:8 * k + 8, :] += _colsum8(dzc * hp[pl.ds(r + (k + 2), CONV_CH), :])
            a = a_ref[pl.ds(r, CONV_CH), :]
            sg = _sigmoid(g_ref[pl.ds(r, CONV_CH), :])
            da_ref[pl.ds(r, CONV_CH), :] = dh * sg
            dg_ref[pl.ds(r, CONV_CH), :] = dh * a * sg * (1.0 - sg)
            return c
        lax.fori_loop(0, n, step, 0)

        for k in range(CONV_K):
            dcw_ref[k:k + 1, :] = jnp.sum(accw[8 * k:8 * k + 8, :], axis=0, keepdims=True)
        dcw_ref[CONV_K:32, :] = jnp.zeros((32 - CONV_K, LANE), F32)

    nb = CONV_DIM // LANE
    colspec = pl.BlockSpec((t, LANE), lambda j: (0, j))
    return pl.pallas_call(
        body, name="conv_bwd", grid=(nb,),
        in_specs=[colspec, pl.BlockSpec((t, LANE), lambda j: (0, nb + j)),
                  pl.BlockSpec((32, LANE), lambda j: (0, j)), colspec],
        out_specs=[colspec, colspec, pl.BlockSpec((32, LANE), lambda j: (0, j))],
        out_shape=[jax.ShapeDtypeStruct((t, CONV_DIM), F32), jax.ShapeDtypeStruct((t, CONV_DIM), F32),
                   jax.ShapeDtypeStruct((32, CONV_DIM), F32)],
        scratch_shapes=[pltpu.VMEM((t + 32, LANE), F32), pltpu.VMEM((t + 32, LANE), F32),
                        pltpu.VMEM((8 * 32, LANE), F32)],
        compiler_params=_cp("parallel"),
    )(ag, ag, cw, dz)


def _rope(x, c, s1, s2):
    return x * c + pltpu.roll(x, LANE - 16, 1) * s1 + pltpu.roll(x, 16, 1) * s2


def _rope_t(dy, c, s1, s2):
    return dy * c + pltpu.roll(dy * s1, 16, 1) + pltpu.roll(dy * s2, LANE - 16, 1)


def _mla_pre_fwd(cq, ckv, kr, qag, wuq, kvag, wk, wv, qng, kng, rc, rs1, rs2):
    t = cq.shape[0]
    tm = _tile(t, 384)

    def body(cq_ref, ckv_ref, kr_ref, qag_ref, wuq_ref, kvag_ref, wk_ref, wv_ref, qng_ref, kng_ref,
             c_ref, s1_ref, s2_ref, q_ref, k_ref, v_ref, cqn_ref, ckvn_ref):
        cqv = cq_ref[...]
        cqn = (cqv * _rstd(cqv) * qag_ref[...]).astype(_MM)
        cqn_ref[...] = cqn
        ckvv = ckv_ref[...]
        ckvn = (ckvv * _rstd(ckvv) * kvag_ref[...]).astype(_MM)
        ckvn_ref[...] = ckvn
        qraw = jnp.dot(cqn, wuq_ref[...], preferred_element_type=F32)
        kraw = jnp.dot(ckvn, wk_ref[...], preferred_element_type=F32)
        v_ref[...] = jnp.dot(ckvn, wv_ref[...], preferred_element_type=F32).astype(_MM)
        krv = kr_ref[...]
        c, s1, s2 = c_ref[...], s1_ref[...], s2_ref[...]
        for h in range(NH):
            sl = slice(LANE * h, LANE * (h + 1))
            qh = qraw[:, sl]
            qn = qh * _rstd(qh, QK_DIM) * qng_ref[...]
            q_ref[:, sl] = (_rope(qn, c, s1, s2) * ATT_SCALE).astype(_MM)
            kh = kraw[:, sl] + krv
            kn = kh * _rstd(kh, QK_DIM) * kng_ref[...]
            k_ref[:, sl] = _rope(kn, c, s1, s2).astype(_MM)

    hd = NH * LANE
    return pl.pallas_call(
        body, name="mla_pre_fwd", grid=(t // tm,),
        in_specs=[_row(tm, 256), _row(tm, 128), _row(tm, 128), _full((1, 256)), _full((256, hd)),
                  _full((1, 128)), _full((128, hd)), _full((128, hd)), _full((1, LANE)), _full((1, LANE)),
                  _row(tm, LANE), _row(tm, LANE), _row(tm, LANE)],
        out_specs=[_row(tm, hd), _row(tm, hd), _row(tm, hd), _row(tm, 256), _row(tm, 128)],
        out_shape=[jax.ShapeDtypeStruct((t, hd), _MM)] * 3 + [jax.ShapeDtypeStruct((t, 256), _MM),
                                                              jax.ShapeDtypeStruct((t, 128), _MM)],
        compiler_params=_cp("parallel"),
    )(cq, ckv, kr, qag, wuq, kvag, wk, wv, qng, kng, rc, rs1, rs2)


def _mla_pre_bwd(dq, dk, dv, cq, ckv, kr, qag, wuq, kvag, wk, wv, qng, kng, rc, rs1, rs2):
    t = cq.shape[0]
    tm = _tile(t, 192)
    hd = NH * LANE

    def body(dq_ref, dk_ref, dv_ref, cq_ref, ckv_ref, kr_ref, qag_ref, wuq_ref, kvag_ref, wk_ref,
             wv_ref, qng_ref, kng_ref, c_ref, s1_ref, s2_ref,
             dcq_ref, dckv_ref, dkr_ref, dqraw_ref, dkraw_ref, dqag_ref, dkvag_ref, dqng_ref, dkng_ref):
        i = pl.program_id(0)
        cqv = cq_ref[...]
        rq_in = _rstd(cqv)
        cqn = (cqv * rq_in * qag_ref[...]).astype(_MM)
        ckvv = ckv_ref[...]
        rkv_in = _rstd(ckvv)
        ckvn = (ckvv * rkv_in * kvag_ref[...]).astype(_MM)
        qraw = jnp.dot(cqn, wuq_ref[...], preferred_element_type=F32)
        kraw = jnp.dot(ckvn, wk_ref[...], preferred_element_type=F32)
        krv = kr_ref[...]
        c, s1, s2 = c_ref[...], s1_ref[...], s2_ref[...]
        dkr = jnp.zeros((tm, LANE), F32)
        dqng = jnp.zeros((8, LANE), F32)
        dkng = jnp.zeros((8, LANE), F32)
        for h in range(NH):
            sl = slice(LANE * h, LANE * (h + 1))
            qh = qraw[:, sl]
            dqn = _rope_t(dq_ref[:, sl] * ATT_SCALE, c, s1, s2)
            dqh, gq = _rms_bwd(dqn, qh, _rstd(qh, QK_DIM), qng_ref[...], QK_DIM)
            dqraw_ref[:, sl] = dqh.astype(_MM)
            dqng = dqng + _colsum8(gq)
            kh = kraw[:, sl] + krv
            dkn = _rope_t(dk_ref[:, sl], c, s1, s2)
            dkh, gk = _rms_bwd(dkn, kh, _rstd(kh, QK_DIM), kng_ref[...], QK_DIM)
            dkraw_ref[:, sl] = dkh.astype(_MM)
            dkr = dkr + dkh
            dkng = dkng + _colsum8(gk)
        dkr_ref[...] = dkr.astype(_MM)
        dcqn = _mm_nt(dqraw_ref[...], wuq_ref[...])
        dcq, gqa = _rms_bwd(dcqn, cqv, rq_in, qag_ref[...])
        dcq_ref[...] = dcq.astype(_MM)
        dckvn = _mm_nt(dkraw_ref[...], wk_ref[...]) + _mm_nt(dv_ref[...], wv_ref[...])
        dckv, gkva = _rms_bwd(dckvn, ckvv, rkv_in, kvag_ref[...])
        dckv_ref[...] = dckv.astype(_MM)

        @pl.when(i == 0)
        def _():
            dqag_ref[...] = jnp.zeros_like(dqag_ref)
            dkvag_ref[...] = jnp.zeros_like(dkvag_ref)
            dqng_ref[...] = jnp.zeros_like(dqng_ref)
            dkng_ref[...] = jnp.zeros_like(dkng_ref)
        dqag_ref[...] += _colsum8(gqa)
        dkvag_ref[...] += _colsum8(gkva)
        dqng_ref[...] += dqng
        dkng_ref[...] += dkng

    return pl.pallas_call(
        body, name="mla_pre_bwd", grid=(t // tm,),
        in_specs=[_row(tm, hd), _row(tm, hd), _row(tm, hd), _row(tm, 256), _row(tm, 128), _row(tm, 128),
                  _full((1, 256)), _full((256, hd)), _full((1, 128)), _full((128, hd)),
                  _full((128, hd)), _full((1, LANE)), _full((1, LANE)),
                  _row(tm, LANE), _row(tm, LANE), _row(tm, LANE)],
        out_specs=[_row(tm, 256), _row(tm, 128), _row(tm, 128), _row(tm, hd), _row(tm, hd),
                   _full((8, 256)), _full((8, 128)), _full((8, LANE)), _full((8, LANE))],
        out_shape=[jax.ShapeDtypeStruct((t, 256), _MM), jax.ShapeDtypeStruct((t, 128), _MM),
                   jax.ShapeDtypeStruct((t, 128), _MM), jax.ShapeDtypeStruct((t, hd), _MM),
                   jax.ShapeDtypeStruct((t, hd), _MM), jax.ShapeDtypeStruct((8, 256), F32),
                   jax.ShapeDtypeStruct((8, 128), F32), jax.ShapeDtypeStruct((8, LANE), F32),
                   jax.ShapeDtypeStruct((8, LANE), F32)],
        compiler_params=_cp("arbitrary"),
    )(dq, dk, dv, cq, ckv, kr, qag, wuq, kvag, wk, wv, qng, kng, rc, rs1, rs2)


ATT_TILE = 704


def _attn_mask(r0, c0, tq):
    rows = r0 + lax.broadcasted_iota(jnp.int32, (tq, 1), 0)
    cols = c0 + lax.broadcasted_iota(jnp.int32, (1, tq), 1)
    return (cols <= rows) & (cols >= FRONT)


def _attn_fwd(q, k, v, plan=None):
    t = q.shape[0]
    tq = _tile(t, ATT_TILE)
    nq = t // tq
    p_args, p_in, p_out, p_shape, p_sem = _plan_specs(plan)

    def body(*refs):
        ((q_ref, k_ref, v_ref), (o_ref, lse_ref), _), rider = _host_refs(refs, 3, 2, 0, plan)
        done = _ride(plan, rider, pl.program_id(0), NH - 1)

        def qloop(qi, carry):
            r0 = pl.multiple_of(qi * tq, tq)
            qb = q_ref[pl.ds(r0, tq), :]

            def kstep(kj, st, masked):
                m, l, acc = st
                c0 = pl.multiple_of(kj * tq, tq)
                s = _mm_nt(qb, k_ref[pl.ds(c0, tq), :])
                if masked:
                    s = jnp.where(_attn_mask(r0, c0, tq), s, NEG)
                m2 = jnp.maximum(m, jnp.max(s, axis=-1, keepdims=True))
                p = jnp.exp(s - m2)
                a = jnp.exp(m - m2)
                l = a * l + jnp.sum(p, axis=-1, keepdims=True)
                acc = a * acc + _mm(p, v_ref[pl.ds(c0, tq), :])
                return m2, l, acc

            st = kstep(0, (jnp.full((tq, 1), NEG, F32), jnp.zeros((tq, 1), F32), jnp.zeros((tq, LANE), F32)), True)
            st = lax.fori_loop(1, qi, lambda kj, s_: kstep(kj, s_, False), st)
            m, l, acc = lax.cond(qi > 0, lambda s_: kstep(qi, s_, True), lambda s_: s_, st)
            o_ref[pl.ds(r0, tq), :] = acc / l
            lse_ref[pl.ds(r0, tq), :] = m + jnp.log(l)
            return carry
        lax.fori_loop(0, nq, qloop, 0)
        done()

    hs = pl.BlockSpec((t, LANE), lambda h: (0, h))
    res = pl.pallas_call(
        body, name="attn_fwd", grid=(NH,),
        in_specs=[hs, hs, hs] + p_in,
        out_specs=[hs, pl.BlockSpec((None, t, 1), lambda h: (h, 0, 0))] + p_out,
        out_shape=[jax.ShapeDtypeStruct((t, NH * LANE), F32), jax.ShapeDtypeStruct((NH, t, 1), F32)] + p_shape,
        scratch_shapes=p_sem,
        compiler_params=_cp("parallel" if plan is None else "arbitrary"),
    )(q, k, v, *p_args)
    return res[:2], res[2:]


def _attn_bwd(q, k, v, o, lse, do, plan=None):
    t = q.shape[0]
    tq = _tile(t, ATT_TILE)
    nq = t // tq
    p_args, p_in, p_out, p_shape, p_sem = _plan_specs(plan)

    def body(*refs):
        (ins, (dq_ref, dk_ref, dv_ref), (delta,)), rider = _host_refs(refs, 6, 3, 1, plan)
        q_ref, k_ref, v_ref, o_ref, lse_ref, do_ref = ins
        done = _ride(plan, rider, pl.program_id(0), NH - 1)

        def prep(i, c):
            r0 = pl.multiple_of(i * tq, tq)
            delta[pl.ds(r0, tq), :] = jnp.sum(do_ref[pl.ds(r0, tq), :] * o_ref[pl.ds(r0, tq), :], axis=-1,
                                              keepdims=True)
            dq_ref[pl.ds(r0, tq), :] = jnp.zeros((tq, LANE), F32)
            return c
        lax.fori_loop(0, nq, prep, 0)

        def kloop(kj, carry):
            c0 = pl.multiple_of(kj * tq, tq)
            kb = k_ref[pl.ds(c0, tq), :]
            vb = v_ref[pl.ds(c0, tq), :]

            def qstep(qi, st, masked):
                dkb, dvb = st
                r0 = pl.multiple_of(qi * tq, tq)
                qb = q_ref[pl.ds(r0, tq), :]
                dob = do_ref[pl.ds(r0, tq), :].astype(_MM)
                s = _mm_nt(qb, kb)
                if masked:
                    s = jnp.where(_attn_mask(r0, c0, tq), s, NEG)
                p = jnp.exp(s - lse_ref[pl.ds(r0, tq), :])
                dvb = dvb + _mm_tn(p, dob)
                dp = _mm_nt(dob, vb)
                ds = (p * (dp - delta[pl.ds(r0, tq), :])).astype(_MM)
                dkb = dkb + _mm_tn(ds, qb)
                dq_ref[pl.ds(r0, tq), :] += _mm(ds, kb)
                return dkb, dvb

            st = qstep(kj, (jnp.zeros((tq, LANE), F32), jnp.zeros((tq, LANE), F32)), True)
            dkb, dvb = lax.cond(
                kj == 0,
                lambda s_: lax.fori_loop(kj + 1, nq, lambda qi, t_: qstep(qi, t_, True), s_),
                lambda s_: lax.fori_loop(kj + 1, nq, lambda qi, t_: qstep(qi, t_, False), s_), st)
            dk_ref[pl.ds(c0, tq), :] = dkb
            dv_ref[pl.ds(c0, tq), :] = dvb
            return carry
        lax.fori_loop(0, nq, kloop, 0)
        done()

    hs = pl.BlockSpec((t, LANE), lambda h: (0, h))
    res = pl.pallas_call(
        body, name="attn_bwd", grid=(NH,),
        in_specs=[hs, hs, hs, hs, pl.BlockSpec((None, t, 1), lambda h: (h, 0, 0)), hs] + p_in,
        out_specs=[hs, hs, hs] + p_out,
        out_shape=[jax.ShapeDtypeStruct((t, NH * LANE), F32)] * 3 + p_shape,
        scratch_shapes=[pltpu.VMEM((t, 1), F32)] + p_sem,
        compiler_params=_cp("parallel" if plan is None else "arbitrary"),
    )(q, k, v, o, lse, do, *p_args)
    return res[:3], res[3:]


def _cumsum_rows(x):
    n = x.shape[0]
    rows = lax.broadcasted_iota(jnp.int32, (n, 1), 0)
    d = 1
    while d < n:
        x = x + jnp.where(rows >= d, pltpu.roll(x, d, 0), 0.0)
        d *= 2
    return x


def _revcumsum_rows(x):
    n = x.shape[0]
    rows = lax.broadcasted_iota(jnp.int32, (n, 1), 0)
    d = 1
    while d < n:
        x = x + jnp.where(rows < n - d, pltpu.roll(x, n - d, 0), 0.0)
        d *= 2
    return x


def _hgrn_gates(f, lb):
    sneg = _sigmoid(-f)
    kk = (1.0 - lb) * sneg
    lf = jnp.log1p(-jnp.minimum(kk, GATE_CLAMP))
    return kk, lf, sneg


def _silu(x):
    return x * _sigmoid(x)


def _dsilu(x):
    s = _sigmoid(x)
    return s * (1.0 + x * (1.0 - s))


def _hgrn_intra(q, kk, b):
    parts = []
    for blk in range(CHUNK // SUB):
        lo = blk * SUB
        ref = jnp.zeros((1, LANE), F32) if blk == 0 else b[lo - 1:lo, :]
        eq = jnp.exp(b[lo:lo + SUB, :] - ref)
        ek = jnp.exp(jnp.minimum(ref - b, EXP_CLIP))
        parts.append((q[lo:lo + SUB, :] * eq, kk * ek, eq, ek))
    return parts


def _chunk_causal():
    return lax.broadcasted_iota(jnp.int32, (CHUNK, CHUNK), 1) <= lax.broadcasted_iota(jnp.int32, (CHUNK, CHUNK), 0)


def _hgrn_fwd(h4, lb):
    t = h4.shape[0]
    nc = t // CHUNK

    def body(q_ref, f_ref, i_ref, lb_ref, o_ref, s_ref, st):
        st[...] = jnp.zeros_like(st)
        causal = _chunk_causal()

        def chunk(c, carry):
            r0 = pl.multiple_of(c * CHUNK, CHUNK)
            q = q_ref[pl.ds(r0, CHUNK), :]
            kk, lf, _ = _hgrn_gates(f_ref[pl.ds(r0, CHUNK), :], lb_ref[...])
            v = _silu(i_ref[pl.ds(r0, CHUNK), :])
            b = _cumsum_rows(lf)
            s_prev = st[...]
            s_ref[c] = s_prev
            o = _hmm_nt(q * jnp.exp(b), s_prev)
            a = jnp.concatenate([_hmm_nt(qs, ks) for qs, ks, _, _ in _hgrn_intra(q, kk, b)], axis=0)
            a = jnp.where(causal, a, 0.0)
            o_ref[pl.ds(r0, CHUNK), :] = o + _hmm(a, v)
            bl = b[CHUNK - 1:CHUNK, :]
            st[...] = s_prev * jnp.exp(bl) + _hmm_tn(v, kk * jnp.exp(bl - b))
            return carry
        lax.fori_loop(0, nc, chunk, 0, unroll=2)

    def col(j):
        return pl.BlockSpec((t, LANE), lambda h: (0, HH * j + h))
    return pl.pallas_call(
        body, name="hgrn_fwd", grid=(HH,),
        in_specs=[col(0), col(1), col(2), pl.BlockSpec((1, LANE), lambda h: (0, h))],
        out_specs=[pl.BlockSpec((t, LANE), lambda h: (0, h)),
                   pl.BlockSpec((None, nc, LANE, LANE), lambda h: (h, 0, 0, 0))],
        out_shape=[jax.ShapeDtypeStruct((t, HH * LANE), F32), jax.ShapeDtypeStruct((HH, nc, LANE, LANE), F32)],
        scratch_shapes=[pltpu.VMEM((LANE, LANE), F32)],
        compiler_params=_cp("parallel"),
    )(h4, h4, h4, lb)


def _hgrn_bwd(h4, lb, do, states, plan=None):
    t = h4.shape[0]
    nc = t // CHUNK
    p_args, p_in, p_out, p_shape, p_sem = _plan_specs(plan)

    def body(*refs):
        (ins, outs, (dst, carry)), rider = _host_refs(refs, 6, 4, 2, plan)
        q_ref, f_ref, i_ref, lb_ref, do_ref, s_ref = ins
        dq_ref, df_ref, di_ref, dlb_ref = outs
        done = _ride(plan, rider, pl.program_id(0), HH - 1)
        dst[...] = jnp.zeros_like(dst)
        carry[...] = jnp.zeros_like(carry)
        dlb_ref[...] = jnp.zeros_like(dlb_ref)
        causal = _chunk_causal()

        def chunk(cc, cr):
            c = nc - 1 - cc
            r0 = pl.multiple_of(c * CHUNK, CHUNK)
            q = q_ref[pl.ds(r0, CHUNK), :]
            lbv = lb_ref[...]
            kk, lf, sneg = _hgrn_gates(f_ref[pl.ds(r0, CHUNK), :], lbv)
            iv = i_ref[pl.ds(r0, CHUNK), :]
            v = _silu(iv)
            b = _cumsum_rows(lf)
            s_prev = s_ref[c]
            ds_new = dst[...]
            dob = do_ref[pl.ds(r0, CHUNK), :]
            e = jnp.exp(b)
            qe = q * e
            bl = b[CHUNK - 1:CHUNK, :]
            etail = jnp.exp(bl - b)
            kd = kk * etail
            dq_inter = _hmm(dob, s_prev) * e
            dv = _hmm_nt(kd, ds_new)
            dkk = _hmm(v, ds_new) * etail
            parts = _hgrn_intra(q, kk, b)
            a = jnp.where(causal, jnp.concatenate([_hmm_nt(qs, ks) for qs, ks, _, _ in parts], axis=0), 0.0)
            da = jnp.where(causal, _hmm_nt(dob, v), 0.0)
            dv = dv + _hmm_tn(a, dob)
            dq_rows = []
            for blk, (qs, ks, eq, ek) in enumerate(parts):
                da_blk = da[blk * SUB:(blk + 1) * SUB, :]
                dq_rows.append(_hmm(da_blk, ks) * eq)
                dkk = dkk + _hmm_tn(da_blk, qs) * ek
            dq = dq_inter + jnp.concatenate(dq_rows, axis=0)
            dst[...] = ds_new * jnp.exp(bl) + _hmm_tn(dob, qe)
            g = q * dq - kk * dkk
            dlf = _revcumsum_rows(g) + carry[0:1, :]
            carry[0:1, :] += jnp.sum(g, axis=0, keepdims=True)
            dkk_tot = dkk + dlf * jnp.where(kk < GATE_CLAMP, -1.0 / (1.0 - kk), 0.0)
            dq_ref[pl.ds(r0, CHUNK), :] = dq
            df_ref[pl.ds(r0, CHUNK), :] = dkk_tot * (1.0 - lbv) * (-sneg * (1.0 - sneg))
            di_ref[pl.ds(r0, CHUNK), :] = dv * _dsilu(iv)
            dlb_ref[...] += _colsum8(dkk_tot * (-sneg))
            return cr
        lax.fori_loop(0, nc, chunk, 0, unroll=2)
        done()

    def col(j):
        return pl.BlockSpec((t, LANE), lambda h: (0, HH * j + h))
    hs = pl.BlockSpec((t, LANE), lambda h: (0, h))
    res = pl.pallas_call(
        body, name="hgrn_bwd", grid=(HH,),
        in_specs=[col(0), col(1), col(2), pl.BlockSpec((1, LANE), lambda h: (0, h)), hs,
                  pl.BlockSpec((None, nc, LANE, LANE), lambda h: (h, 0, 0, 0))] + p_in,
        out_specs=[hs, hs, hs, pl.BlockSpec((8, LANE), lambda h: (0, h))] + p_out,
        out_shape=[jax.ShapeDtypeStruct((t, HH * LANE), F32)] * 3 + [jax.ShapeDtypeStruct((8, HH * LANE), F32)]
        + p_shape,
        scratch_shapes=[pltpu.VMEM((LANE, LANE), F32), pltpu.VMEM((8, LANE), F32)] + p_sem,
        compiler_params=_cp("parallel" if plan is None else "arbitrary"),
    )(h4, h4, h4, lb, do, states, *p_args)
    return res[:4], res[4:]


def _ln_fwd(z, g, b):
    mu = jnp.mean(z, axis=-1, keepdims=True)
    zc = z - mu
    rstd = lax.rsqrt(jnp.mean(zc * zc, axis=-1, keepdims=True) + EPS)
    zh = zc * rstd
    return zh * g + b, zh, rstd


def _mix_fwd(x, z, o_att, o_h, h4, gates, lng, lnb, wco, wao, ng, who, wout, t_end):
    t = x.shape[0]
    tm = _tile(t, 192)

    def body(x_ref, z_ref, oa_ref, oh_ref, hg_ref, gt_ref, lng_ref, lnb_ref, wco_ref, wao_ref, ng_ref, who_ref,
             wout_ref, x1_ref, mix_ref, ca_ref, oc_ref, ya_ref, yb_ref, yc_ref):
        i = pl.program_id(0)
        ln, _, _ = _ln_fwd(z_ref[...], lng_ref[...], lnb_ref[...])
        ca = _silu(ln).astype(_MM)
        ca_ref[...] = ca
        ya = jnp.dot(ca, wco_ref[...], preferred_element_type=F32)
        yb = _mm(oa_ref[...], wao_ref[...])
        hg = hg_ref[...]
        for h in range(HH):
            sl = slice(LANE * h, LANE * (h + 1))
            oh = oh_ref[:, sl]
            oc_ref[:, sl] = (oh * _rstd(oh) * ng_ref[:, sl] * _silu(hg[:, sl])).astype(_MM)
        yc = jnp.dot(oc_ref[...], who_ref[...], preferred_element_type=F32)
        ya_ref[...] = ya
        yb_ref[...] = yb
        yc_ref[...] = yc
        mix = (_sigmoid(gt_ref[:, 0:D]) * ya + _sigmoid(gt_ref[:, D:2 * D]) * yb
               + _sigmoid(gt_ref[:, 2 * D:3 * D]) * yc).astype(_MM)
        mix_ref[...] = mix
        x1_ref[...] = x_ref[...] + _valid_rows(i, tm, t_end) * jnp.dot(mix, wout_ref[...],
                                                                       preferred_element_type=F32)

    hd = NH * LANE
    return pl.pallas_call(
        body, name="mix_fwd", grid=(t // tm,),
        in_specs=[_row(tm, D), _row(tm, CONV_DIM), _row(tm, hd), _row(tm, 512), _row(tm, 512, 3), _row(tm, 3 * D),
                  _full((1, 512)), _full((1, 512)), _full((512, D)), _full((hd, D)), _full((1, 512)),
                  _full((512, D)), _full((D, D))],
        out_specs=[_row(tm, D), _row(tm, D), _row(tm, 512), _row(tm, 512), _row(tm, D), _row(tm, D), _row(tm, D)],
        out_shape=[jax.ShapeDtypeStruct((t, D), F32), jax.ShapeDtypeStruct((t, D), _MM),
                   jax.ShapeDtypeStruct((t, 512), _MM), jax.ShapeDtypeStruct((t, 512), _MM),
                   jax.ShapeDtypeStruct((t, D), F32), jax.ShapeDtypeStruct((t, D), F32),
                   jax.ShapeDtypeStruct((t, D), F32)],
        compiler_params=_cp("parallel"),
    )(x, z, o_att, o_h, h4, gates, lng, lnb, wco, wao, ng, who, wout)


def _mix_bwd(dx1, ya, yb, yc, gates, z, o_h, h4, lng, lnb, ng, wout, wco, wao, who, plan=None):
    t = dx1.shape[0]
    tm = _tile(t, 192)
    hd = NH * LANE
    p_args, p_in, p_out, p_shape, p_sem = _plan_specs(plan)

    def body(*refs):
        (ins, outs, _), rider = _host_refs(refs, 15, 12, 0, plan)
        (dx1_ref, ya_ref, yb_ref, yc_ref, gt_ref, z_ref, oh_ref, hg_ref, lng_ref, lnb_ref, ng_ref,
         wout_ref, wco_ref, wao_ref, who_ref) = ins
        (dgt_ref, dya_ref, dyb_ref, dyc_ref, dz_ref, doa_ref, doh_ref, dhg_ref,
         dlng_ref, dlnb_ref, dcb_ref, dng_ref) = outs
        i = pl.program_id(0)
        done = _ride(plan, rider, i, t // tm - 1)
        dmix = _mm_nt(dx1_ref[...], wout_ref[...])
        dys = []
        for j, y_ref in enumerate((ya_ref, yb_ref, yc_ref)):
            sg = _sigmoid(gt_ref[:, j * D:(j + 1) * D])
            dgt_ref[:, j * D:(j + 1) * D] = (dmix * y_ref[...] * sg * (1.0 - sg)).astype(_MM)
            dys.append((dmix * sg).astype(_MM))
        dya_ref[...], dyb_ref[...], dyc_ref[...] = dys
        dca = _mm_nt(dys[0], wco_ref[...])
        ln, zh, rstd = _ln_fwd(z_ref[...], lng_ref[...], lnb_ref[...])
        dln = dca * _dsilu(ln)
        dzh = dln * lng_ref[...]
        dz = rstd * (dzh - jnp.mean(dzh, axis=-1, keepdims=True)
                     - zh * jnp.mean(dzh * zh, axis=-1, keepdims=True))
        dz_ref[...] = dz
        doa_ref[...] = _mm_nt(dys[1], wao_ref[...])
        doc = _mm_nt(dys[2], who_ref[...])
        hg = hg_ref[...]
        dng_rows = []
        for h in range(HH):
            sl = slice(LANE * h, LANE * (h + 1))
            oh = oh_ref[:, sl]
            r = _rstd(oh)
            don = doc[:, sl] * _silu(hg[:, sl])
            dhg_ref[:, sl] = (doc[:, sl] * oh * r * ng_ref[:, sl] * _dsilu(hg[:, sl])).astype(_MM)
            doh, gn = _rms_bwd(don, oh, r, ng_ref[:, sl])
            doh_ref[:, sl] = doh
            dng_rows.append(_colsum8(gn))

        @pl.when(i == 0)
        def _():
            dlng_ref[...] = jnp.zeros_like(dlng_ref)
            dlnb_ref[...] = jnp.zeros_like(dlnb_ref)
            dcb_ref[...] = jnp.zeros_like(dcb_ref)
            dng_ref[...] = jnp.zeros_like(dng_ref)
        dlng_ref[...] += _colsum8(dln * zh)
        dlnb_ref[...] += _colsum8(dln)
        dcb_ref[...] += _colsum8(dz)
        dng_ref[...] += jnp.concatenate(dng_rows, axis=1)
        done()

    res = pl.pallas_call(
        body, name="mix_bwd", grid=(t // tm,),
        in_specs=[_row(tm, D), _row(tm, D), _row(tm, D), _row(tm, D), _row(tm, 3 * D), _row(tm, 512), _row(tm, 512),
                  _row(tm, 512, 3), _full((1, 512)), _full((1, 512)), _full((1, 512)),
                  _full((D, D)), _full((512, D)), _full((hd, D)), _full((512, D))] + p_in,
        out_specs=[_row(tm, 3 * D), _row(tm, D), _row(tm, D), _row(tm, D), _row(tm, 512), _row(tm, hd),
                   _row(tm, 512), _row(tm, 512), _full((8, 512)), _full((8, 512)), _full((8, 512)),
                   _full((8, 512))] + p_out,
        out_shape=[jax.ShapeDtypeStruct((t, 3 * D), _MM), jax.ShapeDtypeStruct((t, D), _MM),
                   jax.ShapeDtypeStruct((t, D), _MM), jax.ShapeDtypeStruct((t, D), _MM),
                   jax.ShapeDtypeStruct((t, 512), F32), jax.ShapeDtypeStruct((t, hd), F32),
                   jax.ShapeDtypeStruct((t, 512), F32), jax.ShapeDtypeStruct((t, 512), _MM)]
        + [jax.ShapeDtypeStruct((8, 512), F32)] * 4 + p_shape,
        scratch_shapes=p_sem,
        compiler_params=_cp("arbitrary"),
    )(dx1, ya, yb, yc, gates, z, o_h, h4, lng, lnb, ng, wout, wco, wao, who, *p_args)
    return res[:12], res[12:]


D_FF = 4096


def _ffn_fwd(x1, g2, w1, w2):
    t = x1.shape[0]
    tm = _tile(t, 192)

    def body(x1_ref, g_ref, w1_ref, w2_ref, x2_ref, p_ref):
        xv = x1_ref[...]
        h2 = (xv * _rstd(xv) * g_ref[...]).astype(_MM)
        p = jnp.dot(h2, w1_ref[...], preferred_element_type=F32)
        p_ref[...] = p
        r = jnp.maximum(p, 0.0)
        x2_ref[...] = xv + jnp.dot((r * r).astype(_MM), w2_ref[...], preferred_element_type=F32)

    return pl.pallas_call(
        body, name="ffn_fwd", grid=(t // tm,),
        in_specs=[_row(tm, D), _full((1, D)), _full((D, D_FF)), _full((D_FF, D))],
        out_specs=[_row(tm, D), _row(tm, D_FF)],
        out_shape=[jax.ShapeDtypeStruct((t, D), F32), jax.ShapeDtypeStruct((t, D_FF), F32)],
        compiler_params=_cp("parallel"),
    )(x1, g2, w1, w2)


def _ffn_bwd(dx2, x1, p, g2, w1t, w2t, plan=None):
    t = x1.shape[0]
    tm = _tile(t, 192)
    p_args, p_in, p_out, p_shape, p_sem = _plan_specs(plan)

    def body(*refs):
        (ins, outs, _), rider = _host_refs(refs, 6, 5, 0, plan)
        dx2_ref, x1_ref, p_ref, g_ref, w1t_ref, w2t_ref = ins
        dx1_ref, h2_ref, act_ref, dp_ref, dg_ref = outs
        i = pl.program_id(0)
        done = _ride(plan, rider, i, t // tm - 1)
        xv = x1_ref[...]
        rstd = _rstd(xv)
        h2_ref[...] = (xv * rstd * g_ref[...]).astype(_MM)
        r = jnp.maximum(p_ref[...], 0.0)
        act_ref[...] = (r * r).astype(_MM)
        dx2 = dx2_ref[...]
        da = _mm(dx2, w2t_ref[...])
        dp = (2.0 * r * da).astype(_MM)
        dp_ref[...] = dp
        dh2 = jnp.dot(dp, w1t_ref[...], preferred_element_type=F32)
        dxn, dgrow = _rms_bwd(dh2, xv, rstd, g_ref[...])
        dx1_ref[...] = dx2 + dxn

        @pl.when(i == 0)
        def _():
            dg_ref[...] = jnp.zeros_like(dg_ref)
        dg_ref[...] += _colsum8(dgrow)
        done()

    res = pl.pallas_call(
        body, name="ffn_bwd", grid=(t // tm,),
        in_specs=[_row(tm, D), _row(tm, D), _row(tm, D_FF), _full((1, D)), _full((D_FF, D)),
                  _full((D, D_FF))] + p_in,
        out_specs=[_row(tm, D), _row(tm, D), _row(tm, D_FF), _row(tm, D_FF), _full((8, D))] + p_out,
        out_shape=[jax.ShapeDtypeStruct((t, D), F32), jax.ShapeDtypeStruct((t, D), _MM),
                   jax.ShapeDtypeStruct((t, D_FF), _MM), jax.ShapeDtypeStruct((t, D_FF), _MM),
                   jax.ShapeDtypeStruct((8, D), F32)] + p_shape,
        scratch_shapes=p_sem,
        compiler_params=_cp("arbitrary"),
    )(dx2, x1, p, g2, w1t, w2t, *p_args)
    return res[:5], res[5:]


WGRAD_VMEM = 40 * 1024 * 1024


def _wgrad(a, b, name, chips=1):
    t, ka = a.shape
    nb = b.shape[1]
    cs = nb // chips
    widths = [d for d in range(cs, 0, -LANE) if cs % d == 0 and d % LANE == 0] or [cs]
    tn, tm = widths[-1], 64
    for d in widths:
        room = WGRAD_VMEM - 2 * ka * d * 4
        row_bytes = 2 * (ka * a.dtype.itemsize + d * b.dtype.itemsize) + 4 * ka
        fit = [r for r in range(64, t + 1, 64) if t % r == 0 and r * row_bytes <= room]
        if ka * d * 4 <= 16 * 1024 * 1024 and fit and (max(fit) >= 384 or d == widths[-1]):
            tn, tm = d, max(fit)
            break
    per = cs // tn

    def body(a_ref, b_ref, o_ref):
        @pl.when(pl.program_id(1) == 0)
        def _():
            o_ref[...] = jnp.zeros_like(o_ref)
        o_ref[...] += _mm_tn(a_ref[...], b_ref[...])

    if chips == 1:
        out_spec = pl.BlockSpec((ka, tn), lambda n, i: (0, n))
        out_shape = jax.ShapeDtypeStruct((ka, nb), F32)
    else:
        out_spec = pl.BlockSpec((None, ka, tn), lambda n, i: (n // per, 0, n % per))
        out_shape = jax.ShapeDtypeStruct((chips, ka, cs), F32)
    return pl.pallas_call(
        body, name="wgrad_" + name, grid=(nb // tn, t // tm),
        in_specs=[pl.BlockSpec((tm, ka), lambda n, i: (i, 0)), pl.BlockSpec((tm, tn), lambda n, i: (i, n))],
        out_specs=out_spec, out_shape=out_shape,
        compiler_params=_cp("parallel", "arbitrary"),
    )(a, b)


def _loss_head(y, target, t_end):
    t = y.shape[0]
    tm = _tile(t, 384)

    def body(y_ref, tg_ref, dy_ref, l_ref):
        i = pl.program_id(0)
        r = i * tm + lax.broadcasted_iota(jnp.int32, (tm, 1), 0)
        real = ((r >= ROW0) & (r < t_end)).astype(F32)
        diff = (y_ref[...] - tg_ref[...]) * real
        dy_ref[...] = diff * (1.0 / D)

        @pl.when(i == 0)
        def _():
            l_ref[...] = jnp.zeros_like(l_ref)
        sq = _colsum8(diff * diff)
        part = sq[:, 0:LANE]
        for j in range(1, D // LANE):
            part = part + sq[:, j * LANE:(j + 1) * LANE]
        l_ref[...] += part * (0.5 / D)

    return pl.pallas_call(
        body, name="loss_head", grid=(t // tm,),
        in_specs=[_row(tm, D), _row(tm, D)],
        out_specs=[_row(tm, D), _full((8, LANE))],
        out_shape=[jax.ShapeDtypeStruct((t, D), F32), jax.ShapeDtypeStruct((8, LANE), F32)],
        compiler_params=_cp("arbitrary"),
    )(y, target)


def _lower_bounds_fwd(logits):
    depth, n = logits.shape

    def body(l_ref, lb_ref):
        lg = l_ref[...]
        m = jnp.max(lg, axis=0, keepdims=True)
        e = jnp.exp(lg - m)
        p = e / jnp.sum(e, axis=0, keepdims=True)
        acc = jnp.zeros((1, n), F32)
        for l in range(depth):
            if l > 0:
                acc = acc + p[l:l + 1, :]
            lb_ref[l:l + 1, :] = acc

    return pl.pallas_call(body, name="lower_bounds_fwd", out_shape=jax.ShapeDtypeStruct((depth, n), F32))(logits)


def _lower_bounds_bwd(logits, dlb):
    depth, n = logits.shape

    def body(l_ref, dlb_ref, dl_ref):
        lg = l_ref[...]
        m = jnp.max(lg, axis=0, keepdims=True)
        e = jnp.exp(lg - m)
        p = e / jnp.sum(e, axis=0, keepdims=True)
        dps = [jnp.zeros((1, n), F32)]
        for j in range(1, depth):
            acc = jnp.zeros((1, n), F32)
            for l in range(j, depth):
                acc = acc + dlb_ref[l:l + 1, :]
            dps.append(acc)
        dot = jnp.zeros((1, n), F32)
        for j in range(depth):
            dot = dot + p[j:j + 1, :] * dps[j]
        for j in range(depth):
            dl_ref[j:j + 1, :] = p[j:j + 1, :] * (dps[j] - dot)

    return pl.pallas_call(body, name="lower_bounds_bwd", out_shape=jax.ShapeDtypeStruct((depth, n), F32))(logits, dlb)


def _ew_tile(rows, cols, n_arrays):
    cap = max(16, (24 * 1024 * 1024) // (8 * n_arrays * cols))
    best = None
    for t in range(16, rows + 1, 16):
        if rows % t == 0 and t <= cap:
            best = t
    return rows if best is None else best


def _adamw_math(w, g, m, v):
    mn = ADAM_B1 * m + (1.0 - ADAM_B1) * g
    vn = ADAM_B2 * v + (1.0 - ADAM_B2) * (g * g)
    m_hat = mn / (1.0 - ADAM_B1 ** ADAM_STEP)
    v_hat = vn / (1.0 - ADAM_B2 ** ADAM_STEP)
    return -ADAM_LR * (m_hat / (jnp.sqrt(v_hat) + ADAM_EPS) + ADAM_WD * w), mn, vn


def _adamw_layers(w, m, v, g0, g1, g_sibling, name):
    _, r, c_ = w.shape
    tr = _ew_tile(r, c_, 10)

    def body(w_ref, m_ref, v_ref, g0_ref, g1_ref, gs_ref, g_ref, d_ref, mo_ref, vo_ref):
        layer = pl.program_id(0)
        own = jnp.where(layer == 0, g0_ref[...], g1_ref[...])
        g = jnp.where(layer == lax.axis_index("c"), own, gs_ref[...])
        g_ref[...] = g
        d_ref[...], mo_ref[...], vo_ref[...] = _adamw_math(w_ref[...], g, m_ref[...], v_ref[...])

    lay = pl.BlockSpec((None, tr, c_), lambda l, i: (l, i, 0))
    flat = pl.BlockSpec((tr, c_), lambda l, i: (i, 0))
    return pl.pallas_call(
        body, name="adamw_" + name, grid=(2, r // tr),
        in_specs=[lay, lay, lay, flat, flat, flat], out_specs=[lay] * 4,
        out_shape=[jax.ShapeDtypeStruct(w.shape, F32)] * 4,
        compiler_params=_cp("parallel", "parallel"),
    )(w, m, v, g0, g1, g_sibling)


def _adamw(w, g, m, v, name):
    rows, cols = w.shape
    tr = _ew_tile(rows, cols, 7)

    def body(w_ref, g_ref, m_ref, v_ref, d_ref, mo_ref, vo_ref):
        d_ref[...], mo_ref[...], vo_ref[...] = _adamw_math(w_ref[...], g_ref[...], m_ref[...], v_ref[...])

    spec = pl.BlockSpec((tr, cols), lambda i: (i, 0))
    return pl.pallas_call(
        body, name="adamw_" + name, grid=(rows // tr,),
        in_specs=[spec] * 4, out_specs=[spec] * 3,
        out_shape=[jax.ShapeDtypeStruct((rows, cols), F32)] * 3,
        compiler_params=_cp("parallel"),
    )(w, g, m, v)


DEPTH = 2
BIG_SHAPES = {"w_in": ((1024, 6560), 1), "w_conv_out": ((512, 1024), 1), "w_uq": ((256, 768), 1),
              "w_ukv": ((128, 1024), 1), "w_attn_out": ((512, 1024), 1), "w_hgrn_out": ((512, 1024), 1),
              "w_out": ((1024, 1024), 0), "w_ff1": ((1024, 4096), 1), "w_ff2": ((4096, 1024), 0)}
BIG = tuple(BIG_SHAPES)
SMALL_SIZES = {"norm1_g": 1024, "conv_b": 512, "conv_ln_g": 512, "conv_ln_b": 512, "q_a_norm_g": 256,
               "kv_a_norm_g": 128, "q_norm_g": 96, "k_norm_g": 96, "hgrn_lb_logits": 512, "hgrn_norm_g": 512,
               "norm2_g": 1024}
SMALL = tuple(SMALL_SIZES)
W_IN_COLS = 6560
W_IN_SHARD = W_IN_COLS // 4
W_IN_SEGS = ((0, 1024, SEG_AG[0]), (1024, 1280, SEG_CQ[0]), (1280, 1408, SEG_CKV[0]), (1408, 1440, SEG_KR[0] + 64),
             (1440, 3488, SEG_H4[0]), (3488, 6560, SEG_GATES[0]))


def _pad_heads(w, nh, used, axis):
    shp = w.shape
    w = w.reshape(shp[:axis] + (nh, used) + shp[axis + 1:])
    pad = [(0, 0)] * w.ndim
    pad[axis + 1] = (0, LANE - used)
    w = jnp.pad(w, pad)
    return w.reshape(shp[:axis] + (nh * LANE,) + shp[axis + 1:])


def _unpad_heads(w, nh, used, axis):
    shp = w.shape
    w = w.reshape(shp[:axis] + (nh, LANE) + shp[axis + 1:])
    w = lax.slice_in_dim(w, 0, used, axis=axis + 1)
    return w.reshape(shp[:axis] + (nh * used,) + shp[axis + 1:])


def _w_in_from_chips(p4):
    def orig(a, b):
        out = []
        while a < b:
            s = a // W_IN_SHARD
            e = min(b, (s + 1) * W_IN_SHARD)
            out.append(p4[s][:, a - W_IN_SHARD * s:e - W_IN_SHARD * s])
            a = e
        return out
    zc = lambda n: jnp.zeros((D, n), p4[0].dtype)
    parts = (orig(3488, 6560) + orig(0, 1024) + orig(1440, 3488) + orig(1024, 1280) + orig(1280, 1408)
             + [zc(64)] + orig(1408, 1440) + [zc(32)])
    return jnp.concatenate(parts, axis=1)


def _w_in_grad_to_chips(dw):
    chips = []
    for s in range(4):
        a, b = W_IN_SHARD * s, W_IN_SHARD * (s + 1)
        parts = []
        for o0, o1, p0 in W_IN_SEGS:
            lo, hi = max(a, o0), min(b, o1)
            if lo < hi:
                parts.append(dw[:, p0 + lo - o0:p0 + hi - o0])
        chips.append(jnp.concatenate(parts, axis=1))
    return jnp.stack(chips)


def _cat_chips(p4, axis):
    return jnp.concatenate([p4[s] for s in range(4)], axis=axis)


EARLY = ("w_in", "w_uq", "w_ukv")
LATE = tuple(k for k in BIG if k not in EARLY)


def _prep_late(pieces):
    pc = lambda k: [pieces[k][s].astype(_MM) for s in range(4)]
    return dict(wao=_pad_heads(_cat_chips(pc("w_attn_out"), 1), NH, 64, 0), wco=_cat_chips(pc("w_conv_out"), 1),
                who=_cat_chips(pc("w_hgrn_out"), 1), wout=_cat_chips(pc("w_out"), 0),
                w1=_cat_chips(pc("w_ff1"), 1), w2=_cat_chips(pc("w_ff2"), 0))


def _prep_early(pieces, small, l):
    mm = lambda a: a.astype(_MM)
    pc = lambda k: [mm(pieces[k][s]) for s in range(4)]
    w_in_p = _w_in_from_chips(pc("w_in"))
    wuq = jnp.concatenate([_pad_heads(pc("w_uq")[s], 2, QK_DIM, 1) for s in range(4)], axis=1)
    wukv = _cat_chips(pc("w_ukv"), 1).reshape(128, NH, 128)
    wk = _pad_heads(wukv[:, :, :64].reshape(128, NH * 64), NH, 64, 1)
    wv = _pad_heads(wukv[:, :, 64:].reshape(128, NH * 64), NH, 64, 1)
    row = lambda a: a.astype(F32).reshape(1, -1)
    p = dict(
        w_in=w_in_p, w_in_t=w_in_p.T, wuq=wuq, wk=wk, wv=wv,
        g1=row(small["norm1_g"][l]), g2=row(small["norm2_g"][l]),
        cw=jnp.pad(small["conv_w"][l].astype(F32), ((0, 1), (0, 0))), cb=row(small["conv_b"][l]),
        lng=row(small["conv_ln_g"][l]), lnb=row(small["conv_ln_b"][l]),
        qag=row(small["q_a_norm_g"][l]), kvag=row(small["kv_a_norm_g"][l]),
        qng=jnp.pad(row(small["q_norm_g"][l]), ((0, 0), (0, LANE - QK_DIM))),
        kng=jnp.pad(row(small["k_norm_g"][l]), ((0, 0), (0, LANE - QK_DIM))),
        ng=row(small["hgrn_norm_g"][l]),
    )
    return p


def _rope_tables(t):
    pos = (jnp.arange(t, dtype=jnp.int32) - FRONT).astype(F32)
    inv_freq = 10000.0 ** (-jnp.arange(16, dtype=F32) / 16)
    ang = pos[:, None] * inv_freq[None, :]
    cos, sin = jnp.cos(ang), jnp.sin(ang)
    one = jnp.ones((t, 64), F32)
    z16, z32, z64 = jnp.zeros((t, 16), F32), jnp.zeros((t, 32), F32), jnp.zeros((t, 64), F32)
    c = jnp.concatenate([one, cos, cos, z32], axis=1)
    s1 = jnp.concatenate([z64, -sin, z16, z32], axis=1)
    s2 = jnp.concatenate([z64, z16, sin, z32], axis=1)
    return c, s1, s2


def _layer_fwd(x, p, lb, rope, t_end, plan=None, on_rode=None):
    gates, ag, h4, cq, ckv, kr, hb = _in_proj_fwd(x, p["g1"], p["w_in"])
    z = _conv_fwd(ag, p["cw"], p["cb"])
    q, k, v, cqn, ckvn = _mla_pre_fwd(cq, ckv, kr, p["qag"], p["wuq"], p["kvag"], p["wk"], p["wv"], p["qng"],
                                      p["kng"], *rope)
    (o_att, lse), rode = _attn_fwd(q, k, v, plan)
    if on_rode is not None:
        on_rode(rode)
    o_h, states = _hgrn_fwd(h4, lb)
    x1, mix, ca, oc, ya, yb, yc = _mix_fwd(x, z, o_att, o_h, h4, gates, p["lng"], p["lnb"], p["wco"], p["wao"],
                                           p["ng"], p["who"], p["wout"], t_end)
    x2, pre = _ffn_fwd(x1, p["g2"], p["w1"], p["w2"])
    saved = dict(x=x, gates=gates, ag=ag, h4=h4, cq=cq, ckv=ckv, kr=kr, hb=hb, z=z, q=q, k=k, v=v, cqn=cqn,
                 ckvn=ckvn, o_att=o_att, lse=lse, o_h=o_h, states=states, x1=x1, mix=mix, ca=ca, oc=oc,
                 ya=ya, yb=yb, yc=yc, pre=pre)
    return x2, saved


def _layer_bwd(dx2, s, p, lb, rope, t_end, rides=None):
    rides = rides or {}
    (dx1, h2, act, dp, dg2), rode = _ffn_bwd(dx2, s["x1"], s["pre"], p["g2"], p["w1"].T, p["w2"].T,
                                             rides.get("ffn"))
    g = {"w_ff1": _wgrad(h2, dp, "ff1", 4), "w_ff2": _wgrad(act, dx2, "ff2").reshape(4, D_FF // 4, D),
         "norm2_g": dg2.sum(0)}
    plan_mix = rides["mix"](rode, g) if "mix" in rides else None
    (dgt, dya, dyb, dyc, dz, doa, doh, dhg, dlng, dlnb, dcb, dng), rode = _mix_bwd(
        dx1, s["ya"], s["yb"], s["yc"], s["gates"], s["z"], s["o_h"], s["h4"], p["lng"], p["lnb"], p["ng"],
        p["wout"], p["wco"], p["wao"], p["who"], plan_mix)
    g["w_out"] = _wgrad(s["mix"], dx1, "out").reshape(4, D // 4, D)
    g["w_conv_out"] = _wgrad(s["ca"], dya, "conv_out", 4)
    g["w_attn_out"] = _unpad_heads(_wgrad(s["o_att"], dyb, "attn_out", 4), NH, 64, 1)
    g["w_hgrn_out"] = _wgrad(s["oc"], dyc, "hgrn_out", 4)
    g["conv_ln_g"], g["conv_ln_b"], g["conv_b"], g["hgrn_norm_g"] = dlng.sum(0), dlnb.sum(0), dcb.sum(0), dng.sum(0)
    da, dg, dcw = _conv_bwd(s["ag"], p["cw"], dz)
    g["conv_w"] = dcw[:CONV_K]
    plan_attn = rides["attn"](rode, g) if "attn" in rides else None
    (dq, dk, dv), rode_attn = _attn_bwd(s["q"], s["k"], s["v"], s["o_att"], s["lse"], doa, plan_attn)
    dcq, dckv, dkr, dqraw, dkraw, dqag, dkvag, dqng, dkng = _mla_pre_bwd(
        dq, dk, dv, s["cq"], s["ckv"], s["kr"], p["qag"], p["wuq"], p["kvag"], p["wk"], p["wv"], p["qng"],
        p["kng"], *rope)
    g["w_uq"] = _unpad_heads(_wgrad(s["cqn"], dqraw, "uq", 4), 2, QK_DIM, 2)
    dwk = _unpad_heads(_wgrad(s["ckvn"], dkraw, "uk"), NH, 64, 1).reshape(128, NH, 64)
    dwv = _unpad_heads(_wgrad(s["ckvn"], dv, "uv"), NH, 64, 1).reshape(128, NH, 64)
    g["w_ukv"] = jnp.concatenate([dwk, dwv], axis=2).reshape(128, 4, 256).transpose(1, 0, 2)
    g["q_a_norm_g"], g["kv_a_norm_g"] = dqag.sum(0), dkvag.sum(0)
    g["q_norm_g"], g["k_norm_g"] = dqng.sum(0)[:QK_DIM], dkng.sum(0)[:QK_DIM]
    plan_hgrn = rides["hgrn"](rode_attn) if "hgrn" in rides else None
    (dhq, dhf, dhi, dlb), rode_hgrn = _hgrn_bwd(s["h4"], lb, doh, s["states"], plan_hgrn)
    mm = lambda a: a.astype(_MM)
    du = jnp.concatenate([dgt, mm(da), mm(dg), mm(dhq), mm(dhf), mm(dhi), dhg, dcq, dckv, dkr], axis=1)
    dx, dg1 = _in_proj_bwd(du, s["x"], dx1, p["g1"], p["w_in_t"], t_end)
    g["norm1_g"] = dg1.sum(0)
    g["w_in"] = _w_in_grad_to_chips(_wgrad(s["hb"], du, "in"))
    return dx, g, dlb.sum(0), (rode_attn, rode_hgrn)


def _device_step(x, target, small, pieces0, pieces1=None, fwd_ride=None, bwd_rides=None):
    s_real = x.shape[0]
    t_end = ROW0 + s_real
    t = -(-t_end // LANE) * LANE
    zrow = lambda n: jnp.zeros((n, D), F32)
    xp = jnp.concatenate([zrow(FRONT), small["meta"].astype(F32), x, zrow(t - t_end)], axis=0)
    tp = jnp.concatenate([zrow(ROW0), target, zrow(t - t_end)], axis=0)
    rope = _rope_tables(t)
    logits = small["hgrn_lb_logits"].astype(F32)
    lbs = _lower_bounds_fwd(logits)
    prm0 = _prep_early(pieces0, small, 0)
    got = {}
    if fwd_ride is None:
        prm0.update(_prep_late(pieces0))
        h, sv0 = _layer_fwd(xp, prm0, lbs[0:1], rope, t_end)
    else:
        def on_rode(rode):
            late0, got["pieces1"] = fwd_ride[1](rode)
            prm0.update(_prep_late(late0))
        h, sv0 = _layer_fwd(xp, prm0, lbs[0:1], rope, t_end, fwd_ride[0], on_rode)
        pieces1 = got["pieces1"]
    prm1 = _prep_early(pieces1, small, 1)
    if fwd_ride is None:
        prm1.update(_prep_late(pieces1))
        h, sv1 = _layer_fwd(h, prm1, lbs[1:2], rope, t_end)
    else:
        h, sv1 = _layer_fwd(h, prm1, lbs[1:2], rope, t_end, fwd_ride[2],
                            lambda rode: prm1.update(_prep_late(fwd_ride[3](rode))))
    dh, lsum = _loss_head(h, tp, t_end)
    loss = jnp.sum(lsum)
    rides1, make_rides0 = (None, None) if bwd_rides is None else bwd_rides
    dh, g1, dlb1, rode1 = _layer_bwd(dh, sv1, prm1, lbs[1:2], rope, t_end, rides1)
    dh, g0, dlb0, rode = _layer_bwd(dh, sv0, prm0, lbs[0:1], rope, t_end,
                                    None if make_rides0 is None else make_rides0(g1, rode1))
    dlogits = _lower_bounds_bwd(logits, jnp.stack([dlb0, dlb1]))
    grads = [g0, g1]
    for l in range(DEPTH):
        grads[l]["hgrn_lb_logits"] = dlogits[l]
    return loss, dh[ROW0:t_end], grads, dh[FRONT:ROW0], rode


MESH = pl.DeviceIdType.MESH
_ANY = pl.BlockSpec(memory_space=pl.ANY)
SMALL_ROWS = 64
SMALL_LEN = SMALL_ROWS * 1024


def _mesh_pos():
    return lax.axis_index("x"), lax.axis_index("y"), lax.axis_index("c")


def _other_chips(x, y):
    return [(1 - x, y), (x, 1 - y), (1 - x, 1 - y)]


class _Plan:
    def __init__(self, name, ins, out_shapes, sems, start, finish, relay=None):
        self.name, self.ins, self.out_shapes, self.sems = name, list(ins), list(out_shapes), list(sems)
        self.start, self.finish, self.relay = start, finish, relay


def _run_plan(plan):
    ni, no = len(plan.ins), len(plan.out_shapes)

    def body(*refs):
        ins, outs, sems = refs[:ni], refs[ni:ni + no], refs[ni + no:]
        plan.start(ins, outs, sems)
        if plan.relay is not None:
            plan.relay(ins, outs, sems)
        plan.finish(ins, outs, sems)

    return pl.pallas_call(body, name=plan.name, in_specs=[_ANY] * ni, out_specs=[_ANY] * no,
                          out_shape=plan.out_shapes, scratch_shapes=plan.sems)(*plan.ins)


def _plan_specs(plan):
    if plan is None:
        return [], [], [], [], []
    return plan.ins, [_ANY] * len(plan.ins), [_ANY] * len(plan.out_shapes), plan.out_shapes, plan.sems


def _host_refs(refs, n_in, n_out, n_scratch, plan):
    ni = 0 if plan is None else len(plan.ins)
    no = 0 if plan is None else len(plan.out_shapes)
    o0 = n_in + ni
    s0 = o0 + n_out + no
    own = (refs[:n_in], refs[o0:o0 + n_out], refs[s0:s0 + n_scratch])
    rider = (refs[n_in:o0], refs[o0 + n_out:s0], refs[s0 + n_scratch:])
    return own, rider


def _ride(plan, rider, step, last):
    if plan is None:
        return lambda: None

    @pl.when(step == 0)
    def _():
        plan.start(*rider)

    def done():
        if plan.relay is not None:
            @pl.when(step == last - 1)
            def _():
                plan.relay(*rider)

        @pl.when(step == last)
        def _():
            plan.finish(*rider)
    return done


def _merge_plans(name, plans):
    def parts(ins, outs, sems):
        i = o = s = 0
        for p in plans:
            ni, no, ns = len(p.ins), len(p.out_shapes), len(p.sems)
            yield p, (ins[i:i + ni], outs[o:o + no], sems[s:s + ns])
            i, o, s = i + ni, o + no, s + ns

    def start(ins, outs, sems):
        for p, refs in parts(ins, outs, sems):
            p.start(*refs)

    def relay(ins, outs, sems):
        for p, refs in parts(ins, outs, sems):
            if p.relay is not None:
                p.relay(*refs)

    def finish(ins, outs, sems):
        for p, refs in parts(ins, outs, sems):
            p.finish(*refs)

    return _Plan(name, [a for p in plans for a in p.ins], [a for p in plans for a in p.out_shapes],
                 [a for p in plans for a in p.sems], start, finish, relay)


def _plan_gather(own, layer, name):
    nw = len(own)

    def copies(ins, outs, sems):
        send_sems, recv_sems = sems

        def over_ici(w, j, chip_of_data, to):
            return pltpu.make_async_remote_copy(
                src_ref=ins[w].at[layer], dst_ref=outs[w].at[chip_of_data], send_sem=send_sems.at[w, j],
                recv_sem=recv_sems.at[w, j], device_id=to, device_id_type=MESH)

        def over_d2d(w, j, chip_of_data, to):
            return pltpu.make_async_remote_copy(
                src_ref=outs[w].at[chip_of_data], dst_ref=outs[w].at[chip_of_data], send_sem=send_sems.at[w, 3 + j],
                recv_sem=recv_sems.at[w, 3 + j], device_id=to, device_id_type=MESH)
        return over_ici, over_d2d

    def start(ins, outs, sems):
        x, y, c = _mesh_pos()
        over_ici, _ = copies(ins, outs, sems)

        @pl.when(c == layer)
        def _():
            for j, (px, py) in enumerate(_other_chips(x, y)):
                for w in range(nw):
                    over_ici(w, j, 2 * x + y, (px, py, layer)).start()

    def relay(ins, outs, sems):
        x, y, c = _mesh_pos()
        over_ici, over_d2d = copies(ins, outs, sems)

        @pl.when(c == layer)
        def _():
            for j, (px, py) in enumerate(_other_chips(x, y)):
                for w in range(nw):
                    over_ici(w, j, 2 * px + py, (x, y, c)).wait_recv()
                    over_d2d(w, j, 2 * px + py, (x, y, 1 - layer)).start()

    def finish(ins, outs, sems):
        x, y, c = _mesh_pos()
        over_ici, over_d2d = copies(ins, outs, sems)
        chips = _other_chips(x, y)

        @pl.when(c == layer)
        def _():
            for j, (px, py) in enumerate(chips):
                for w in range(nw):
                    over_ici(w, j, 2 * x + y, (px, py, layer)).wait_send()
                    over_d2d(w, j, 2 * px + py, (x, y, 1 - layer)).wait_send()

        @pl.when(c != layer)
        def _():
            for j, (px, py) in enumerate(chips):
                for w in range(nw):
                    over_d2d(w, j, 2 * px + py, (x, y, c)).wait_recv()

    return _Plan(name, own,
                 [jax.ShapeDtypeStruct((4,) + a.shape[1:], a.dtype) for a in own],
                 [pltpu.SemaphoreType.DMA((nw, 6)), pltpu.SemaphoreType.DMA((nw, 6))], start, finish, relay)


def _plan_to_sibling(arrs, layer, name):
    nw = len(arrs)

    def copy(ins, outs, sems, w):
        x, y, _ = _mesh_pos()
        return pltpu.make_async_remote_copy(src_ref=ins[w], dst_ref=outs[w], send_sem=sems[0].at[w],
                                            recv_sem=sems[1].at[w], device_id=(x, y, layer), device_id_type=MESH)

    def start(ins, outs, sems):
        @pl.when(lax.axis_index("c") != layer)
        def _():
            for w in range(nw):
                copy(ins, outs, sems, w).start()

    def finish(ins, outs, sems):
        c = lax.axis_index("c")

        @pl.when(c != layer)
        def _():
            for w in range(nw):
                copy(ins, outs, sems, w).wait_send()

        @pl.when(c == layer)
        def _():
            for w in range(nw):
                copy(ins, outs, sems, w).wait_recv()

    return _Plan(name, arrs, [jax.ShapeDtypeStruct(a.shape, a.dtype) for a in arrs],
                 [pltpu.SemaphoreType.DMA((nw,)), pltpu.SemaphoreType.DMA((nw,))], start, finish)


def _plan_scatter(parts, layer, name):
    nw = len(parts)

    def start(ins, outs, sems):
        x, y, c = _mesh_pos()

        @pl.when(c == layer)
        def _():
            for j, (px, py) in enumerate(_other_chips(x, y)):
                for w in range(nw):
                    pltpu.make_async_remote_copy(
                        src_ref=ins[w].at[2 * px + py], dst_ref=outs[w].at[2 * x + y], send_sem=sems[0].at[w, j],
                        recv_sem=sems[1].at[w, j], device_id=(px, py, layer), device_id_type=MESH).start()

    def finish(ins, outs, sems):
        x, y, c = _mesh_pos()

        @pl.when(c == layer)
        def _():
            for j, (px, py) in enumerate(_other_chips(x, y)):
                for w in range(nw):
                    pltpu.make_async_remote_copy(
                        src_ref=ins[w].at[2 * px + py], dst_ref=outs[w].at[2 * px + py], send_sem=sems[0].at[w, j],
                        recv_sem=sems[1].at[w, j], device_id=(x, y, c), device_id_type=MESH).wait()

    return _Plan(name, parts, [jax.ShapeDtypeStruct(a.shape, a.dtype) for a in parts],
                 [pltpu.SemaphoreType.DMA((nw, 3)), pltpu.SemaphoreType.DMA((nw, 3))], start, finish)


def _sibling_exchange(reds0, reds1):
    nw = len(reds0)

    def body(*refs):
        a0, a1, outs = refs[:nw], refs[nw:2 * nw], refs[2 * nw:3 * nw]
        send_sems, recv_sems = refs[3 * nw:]
        x, y, c = _mesh_pos()

        def copy(w, src):
            return pltpu.make_async_remote_copy(src_ref=src, dst_ref=outs[w], send_sem=send_sems.at[w],
                                                recv_sem=recv_sems.at[w], device_id=(x, y, 1 - c),
                                                device_id_type=MESH)

        @pl.when(c == 0)
        def _():
            for w in range(nw):
                copy(w, a0[w]).start()

        @pl.when(c == 1)
        def _():
            for w in range(nw):
                copy(w, a1[w]).start()

        for w in range(nw):
            copy(w, a0[w]).wait()

    return pl.pallas_call(
        body, name="sibling_exchange", in_specs=[_ANY] * (2 * nw), out_specs=[_ANY] * nw,
        out_shape=[jax.ShapeDtypeStruct(a.shape, a.dtype) for a in reds0],
        scratch_shapes=[pltpu.SemaphoreType.DMA((nw,)), pltpu.SemaphoreType.DMA((nw,))],
    )(*reds0, *reds1)


def _all_reduce_small(v, name):
    rows, cols = v.shape

    def body(v_ref, o_ref, slots, send_sems, recv_sems):
        x, y, c = _mesh_pos()
        me = 4 * x + 2 * y + c
        slots[me] = v_ref[...]
        peers = []
        for rel in range(1, 8):
            fx, fy, fc = (rel >> 2) & 1, (rel >> 1) & 1, rel & 1
            px = 1 - x if fx else x
            py = 1 - y if fy else y
            pc = 1 - c if fc else c
            peers.append((px, py, pc))
        cps = [pltpu.make_async_remote_copy(src_ref=v_ref, dst_ref=slots.at[me], send_sem=send_sems.at[k],
                                            recv_sem=recv_sems.at[k], device_id=peer, device_id_type=MESH)
               for k, peer in enumerate(peers)]
        for cp in cps:
            cp.start()
        for k, (px, py, pc) in enumerate(peers):
            pltpu.make_async_remote_copy(src_ref=v_ref, dst_ref=slots.at[4 * px + 2 * py + pc],
                                         send_sem=send_sems.at[k], recv_sem=recv_sems.at[k], device_id=(x, y, c),
                                         device_id_type=MESH).wait_recv()
        for cp in cps:
            cp.wait_send()
        acc = slots[0]
        for d in range(1, 8):
            acc = acc + slots[d]
        o_ref[...] = acc

    vm = pl.BlockSpec(memory_space=pltpu.VMEM)
    return pl.pallas_call(
        body, name=name, in_specs=[vm], out_specs=vm,
        out_shape=jax.ShapeDtypeStruct((rows, cols), F32),
        scratch_shapes=[pltpu.VMEM((8, rows, cols), F32), pltpu.SemaphoreType.DMA((7,)),
                        pltpu.SemaphoreType.DMA((7,))],
    )(v)


def _add_to_wire(a, b, name):
    n4, r, c_ = a.shape
    rows = n4 * r
    tr = _ew_tile(rows, c_, 3)

    def body(a_ref, b_ref, o_ref):
        o_ref[...] = (a_ref[...] + b_ref[...]).astype(o_ref.dtype)

    spec = pl.BlockSpec((tr, c_), lambda i: (i, 0))
    out = pl.pallas_call(
        body, name="add_to_wire_" + name, grid=(rows // tr,), in_specs=[spec, spec], out_specs=spec,
        out_shape=jax.ShapeDtypeStruct((rows, c_), jnp.bfloat16), compiler_params=_cp("parallel"),
    )(a.reshape(rows, c_), b.reshape(rows, c_))
    return out.reshape(n4, r, c_)


def _sum_chips(recv, own, name):
    _, r, c_ = recv.shape
    tr = _ew_tile(r, c_, 4)

    def body(r_ref, own_ref, o_ref):
        chip = 2 * lax.axis_index("x") + lax.axis_index("y")
        own_v = own_ref[...].astype(F32)
        acc = None
        for s in range(4):
            term = jnp.where(chip == s, own_v, r_ref[s].astype(F32))
            acc = term if acc is None else acc + term
        o_ref[...] = acc

    return pl.pallas_call(
        body, name="sum_chips_" + name, grid=(r // tr,),
        in_specs=[pl.BlockSpec((4, tr, c_), lambda i: (0, i, 0)), pl.BlockSpec((tr, c_), lambda i: (i, 0))],
        out_specs=pl.BlockSpec((tr, c_), lambda i: (i, 0)),
        out_shape=jax.ShapeDtypeStruct((r, c_), F32),
        compiler_params=_cp("parallel"),
    )(recv, own)


def _pack_small(vals, meta_full, conv_w_full):
    flat = jnp.concatenate([vals[k].reshape(-1) for k in SMALL] + [meta_full.reshape(-1), conv_w_full.reshape(-1)])
    return jnp.pad(flat, (0, SMALL_LEN - flat.shape[0])).reshape(SMALL_ROWS, 1024)


def _unpack_small(buf):
    flat = buf.reshape(-1)
    out, off = {}, 0
    for k in SMALL:
        n = DEPTH * SMALL_SIZES[k]
        out[k] = flat[off:off + n].reshape(DEPTH, SMALL_SIZES[k])
        off += n
    meta = flat[off:off + N_META * D].reshape(N_META, D)
    off += N_META * D
    conv_w = flat[off:off + DEPTH * CONV_K * CONV_DIM].reshape(DEPTH, CONV_K, CONV_DIM)
    return out, meta, conv_w


def kernel(x, meta, norm1_g, w_in, conv_w, conv_b, conv_ln_g, conv_ln_b, w_conv_out, q_a_norm_g, w_uq, kv_a_norm_g, w_ukv, q_norm_g, k_norm_g, w_attn_out, hgrn_lb_logits, hgrn_norm_g, w_hgrn_out, w_out, norm2_g, w_ff1, w_ff2, loss_target, m_meta, m_norm1_g, m_w_in, m_conv_w, m_conv_b, m_conv_ln_g, m_conv_ln_b, m_w_conv_out, m_q_a_norm_g, m_w_uq, m_kv_a_norm_g, m_w_ukv, m_q_norm_g, m_k_norm_g, m_w_attn_out, m_hgrn_lb_logits, m_hgrn_norm_g, m_w_hgrn_out, m_w_out, m_norm2_g, m_w_ff1, m_w_ff2, v_meta, v_norm1_g, v_w_in, v_conv_w, v_conv_b, v_conv_ln_g, v_conv_ln_b, v_w_conv_out, v_q_a_norm_g, v_w_uq, v_kv_a_norm_g, v_w_ukv, v_q_norm_g, v_k_norm_g, v_w_attn_out, v_hgrn_lb_logits, v_hgrn_norm_g, v_w_hgrn_out, v_w_out, v_norm2_g, v_w_ff1, v_w_ff2):
    names = ("meta", "norm1_g", "w_in", "conv_w", "conv_b", "conv_ln_g", "conv_ln_b", "w_conv_out", "q_a_norm_g",
             "w_uq", "kv_a_norm_g", "w_ukv", "q_norm_g", "k_norm_g", "w_attn_out", "hgrn_lb_logits", "hgrn_norm_g",
             "w_hgrn_out", "w_out", "norm2_g", "w_ff1", "w_ff2")
    w = dict(zip(names, (meta, norm1_g, w_in, conv_w, conv_b, conv_ln_g, conv_ln_b, w_conv_out, q_a_norm_g, w_uq,
                         kv_a_norm_g, w_ukv, q_norm_g, k_norm_g, w_attn_out, hgrn_lb_logits, hgrn_norm_g, w_hgrn_out,
                         w_out, norm2_g, w_ff1, w_ff2)))
    m = dict(zip(names, (m_meta, m_norm1_g, m_w_in, m_conv_w, m_conv_b, m_conv_ln_g, m_conv_ln_b, m_w_conv_out,
                         m_q_a_norm_g, m_w_uq, m_kv_a_norm_g, m_w_ukv, m_q_norm_g, m_k_norm_g, m_w_attn_out,
                         m_hgrn_lb_logits, m_hgrn_norm_g, m_w_hgrn_out, m_w_out, m_norm2_g, m_w_ff1, m_w_ff2)))
    v = dict(zip(names, (v_meta, v_norm1_g, v_w_in, v_conv_w, v_conv_b, v_conv_ln_g, v_conv_ln_b, v_w_conv_out,
                         v_q_a_norm_g, v_w_uq, v_kv_a_norm_g, v_w_ukv, v_q_norm_g, v_k_norm_g, v_w_attn_out,
                         v_hgrn_lb_logits, v_hgrn_norm_g, v_w_hgrn_out, v_w_out, v_norm2_g, v_w_ff1, v_w_ff2)))
    cx, cy, cc = _mesh_pos()
    chip = 2 * cx + cy
    zero = jnp.zeros((), jnp.int32)

    own = {k: w[k].astype(_MM) for k in BIG}

    def as_pieces(names, gathered, layer):
        return {k: [jnp.where(chip == s, own[k][layer], g[s]) for s in range(4)] for k, g in zip(names, gathered)}

    pieces0 = as_pieces(EARLY, _run_plan(_plan_gather([own[k] for k in EARLY], 0, "gather_l0_early")), 0)
    fwd_ride = (_merge_plans("gather_mid", [_plan_gather([own[k] for k in LATE], 0, "gather_l0_late"),
                                            _plan_gather([own[k] for k in EARLY], 1, "gather_l1_early")]),
                lambda got: (as_pieces(LATE, got[:len(LATE)], 0), as_pieces(EARLY, got[len(LATE):], 1)),
                _plan_gather([own[k] for k in LATE], 1, "gather_l1_late"),
                lambda got: as_pieces(LATE, got, 1))
    meta_slab = lax.dynamic_update_slice(jnp.zeros((N_META, D), F32), meta, (zero, chip * (D // 4)))
    convw_slab = lax.dynamic_update_slice(jnp.zeros((DEPTH, CONV_K, CONV_DIM), F32), conv_w,
                                          (zero, zero, chip * (CONV_DIM // 4)))
    zsmall = {k: jnp.zeros((DEPTH, SMALL_SIZES[k]), F32) for k in SMALL}
    south = (cc == 0).astype(F32)
    _, meta_full, convw_full = _unpack_small(
        _all_reduce_small(_pack_small(zsmall, meta_slab, convw_slab) * south, "gather_small"))
    small = {k: w[k] for k in SMALL}
    small["meta"] = meta_full
    small["conv_w"] = convw_full

    FFN = ("w_ff1", "w_ff2")
    MID = ("w_out", "w_conv_out", "w_attn_out", "w_hgrn_out")
    REST = tuple(k for k in BIG if k not in FFN + MID)
    held = {}

    def to_wire(names, layer, mine, from_sibling):
        return lax.cond(
            cc == layer,
            lambda: [_add_to_wire(a, b, "%s_l%d" % (k, layer)) for k, a, b in zip(names, mine, from_sibling)],
            lambda: [jnp.zeros(a.shape, jnp.bfloat16) for a in mine])

    def chip_sum(names, layer, got, wire):
        return lax.cond(
            cc == layer,
            lambda: [_sum_chips(r, lax.dynamic_index_in_dim(s, chip, 0, keepdims=False), "%s_l%d" % (k, layer))
                     for k, r, s in zip(names, got, wire)],
            lambda: [jnp.zeros(s.shape[1:], F32) for s in wire])

    NONFFN = tuple(k for k in BIG if k not in FFN)

    def ride_attn_l1(_, g1):
        held["g1_ffn"] = [g1[k] for k in FFN]
        return _plan_to_sibling(held["g1_ffn"], 1, "swap_grads_l1_ffn")

    def rides_l0(g1, rode_l1):
        g1_rest = [g1[k] for k in NONFFN]

        def ride_mix(from_sibling1, g0_ffn):
            wire1 = dict(zip(FFN, to_wire(FFN, 1, held["g1_ffn"], rode_l1[0])))
            wire1.update(zip(NONFFN, to_wire(NONFFN, 1, g1_rest, from_sibling1)))
            held["wire1"] = [wire1[k] for k in BIG]
            held["g0_ffn"] = [g0_ffn[k] for k in FFN]
            return _plan_to_sibling(held["g0_ffn"], 0, "swap_grads_l0_ffn")

        def ride_attn(from_sibling0, g0):
            held["wire0_ffn"] = to_wire(FFN, 0, held["g0_ffn"], from_sibling0)
            held["g0_mid"] = [g0[k] for k in MID]
            return _merge_plans("exchange_grads_mid", [
                _plan_scatter(held["wire1"], 1, "scatter_grads_l1"),
                _plan_scatter(held["wire0_ffn"], 0, "scatter_grads_l0_ffn"),
                _plan_to_sibling(held["g0_mid"], 0, "swap_grads_l0_mid")])

        def ride_hgrn(rode_attn):
            held["wire0_mid"] = to_wire(MID, 0, held["g0_mid"], rode_attn[len(BIG) + len(FFN):])
            return _plan_scatter(held["wire0_mid"], 0, "scatter_grads_l0_mid")

        return {"ffn": _plan_to_sibling(g1_rest, 1, "swap_grads_l1_rest"), "mix": ride_mix, "attn": ride_attn,
                "hgrn": ride_hgrn}

    loss_share, grad_x, gl, g_meta, (got, got_mid) = _device_step(
        x[0], loss_target[0], small, pieces0, None, fwd_ride, ({"attn": ride_attn_l1}, rides_l0))

    reds1 = chip_sum(BIG, 1, got[:len(BIG)], held["wire1"])
    reds0 = dict(zip(FFN, chip_sum(FFN, 0, got[len(BIG):len(BIG) + len(FFN)], held["wire0_ffn"])))
    reds0.update(zip(MID, chip_sum(MID, 0, got_mid, held["wire0_mid"])))
    g0_rest = [gl[0][k] for k in REST]
    wire0 = to_wire(REST, 0, g0_rest, _run_plan(_plan_to_sibling(g0_rest, 0, "swap_grads_l0_rest")))
    reds0.update(zip(REST, chip_sum(REST, 0, _run_plan(_plan_scatter(wire0, 0, "scatter_grads_l0_rest")), wire0)))
    reds0 = [reds0[k] for k in BIG]
    reds_sibling = _sibling_exchange(reds0, reds1)
    grads, delta, new_m, new_v = {}, {}, {}, {}
    for k, r0, r1, theirs in zip(BIG, reds0, reds1, reds_sibling):
        grads[k], delta[k], new_m[k], new_v[k] = _adamw_layers(w[k], m[k], v[k], r0, r1, theirs, k)

    g_small_local = {k: jnp.stack([gl[l][k] for l in range(DEPTH)]) for k in SMALL}
    g_convw_local = jnp.stack([gl[l]["conv_w"] for l in range(DEPTH)])
    reduced = _all_reduce_small(
        _pack_small(g_small_local, g_meta, g_convw_local).at[SMALL_ROWS - 1, 1023].set(loss_share), "reduce_small")
    loss = reduced[SMALL_ROWS - 1, 1023]
    g_small, g_meta_full, g_convw_full = _unpack_small(reduced)
    grads.update(g_small)
    grads["meta"] = lax.dynamic_slice(g_meta_full, (zero, chip * (D // 4)), (N_META, D // 4))
    grads["conv_w"] = lax.dynamic_slice(g_convw_full, (zero, zero, chip * (CONV_DIM // 4)),
                                        (DEPTH, CONV_K, CONV_DIM // 4))

    def small_pack(src):
        return _pack_small(src, jnp.pad(src["meta"], ((0, 0), (0, D - D // 4))),
                           jnp.pad(src["conv_w"], ((0, 0), (0, 0), (0, CONV_DIM - CONV_DIM // 4))))

    def small_unpack(buf):
        out, meta_p, convw_p = _unpack_small(buf)
        out["meta"] = meta_p[:, :D // 4]
        out["conv_w"] = convw_p[:, :, :CONV_DIM // 4]
        return out

    d_s, m_s, v_s = [small_unpack(a) for a in _adamw(small_pack(w), small_pack(grads), small_pack(m),
                                                     small_pack(v), "small")]
    delta.update(d_s)
    new_m.update(m_s)
    new_v.update(v_s)
    return (loss, grad_x[None], *[grads[k] for k in names], *[delta[k] for k in names],
            *[new_m[k] for k in names], *[new_v[k] for k in names])
```

```python
import functools

import jax
import jax.numpy as jnp
from jax import lax
from jax.experimental import pallas as pl
from jax.experimental.pallas import tpu as pltpu

F32 = jnp.float32
_MM = jnp.bfloat16

D = 1024
N_META = 16
FRONT = 48
ROW0 = FRONT + N_META
EPS = 1e-6
GATE_CLAMP = 1.0 - 1e-6
CONV_K = 31
CONV_DIM = 512
NH = 8
QK_DIM = 96
ATT_SCALE = QK_DIM ** -0.5
HH = 4
CHUNK = 64
SUB = 16
EXP_CLIP = 60.0
NEG = -1e30
LANE = 128

SEG_GATES = (0, 3072)
SEG_AG = (3072, 4096)
SEG_H4 = (4096, 6144)
SEG_CQ = (6144, 6400)
SEG_CKV = (6400, 6528)
SEG_KR = (6528, 6656)
N_IN_P = 6656

ADAM_LR = 0.001
ADAM_B1 = 0.9
ADAM_B2 = 0.999
ADAM_EPS = 1e-08
ADAM_WD = 0.01
ADAM_STEP = 10

VMEM_LIMIT = 56 * 1024 * 1024


def _tile(n, pref):
    best = 64
    for t in range(64, pref + 1, 64):
        if n % t == 0:
            best = t
    return best


def _cp(*sem):
    return pltpu.CompilerParams(dimension_semantics=tuple(sem), vmem_limit_bytes=VMEM_LIMIT)


def _row(tm, n, col=0):
    return pl.BlockSpec((tm, n), lambda i: (i, col))


def _full(shape):
    return pl.BlockSpec(shape, lambda i: (0,) * len(shape))


def _mm(a, b):
    return jnp.dot(a.astype(_MM), b.astype(_MM), preferred_element_type=F32)


def _mm_nt(a, b):
    return lax.dot_general(a.astype(_MM), b.astype(_MM), (((1,), (1,)), ((), ())), preferred_element_type=F32)


def _mm_tn(a, b):
    return lax.dot_general(a.astype(_MM), b.astype(_MM), (((0,), (0,)), ((), ())), preferred_element_type=F32)


def _split3(x):
    hi = x.astype(jnp.bfloat16)
    return hi, (x - hi.astype(F32)).astype(jnp.bfloat16)


def _dot3(a, b, dims):
    ah, al = _split3(a)
    bh, bl = _split3(b)
    dg = lambda u, v: lax.dot_general(u, v, (dims, ((), ())), preferred_element_type=F32)
    return dg(ah, bh) + (dg(ah, bl) + dg(al, bh))


def _hmm(a, b):
    return _dot3(a, b, ((1,), (0,)))


def _hmm_nt(a, b):
    return _dot3(a, b, ((1,), (1,)))


def _hmm_tn(a, b):
    return _dot3(a, b, ((0,), (0,)))


def _sigmoid(x):
    return 1.0 / (1.0 + jnp.exp(-x))


def _rstd(x, n=None):
    n = x.shape[-1] if n is None else n
    return lax.rsqrt(jnp.sum(x * x, axis=-1, keepdims=True) * (1.0 / n) + EPS)


def _rms_bwd(dy, x, rstd, g, n=None):
    n = x.shape[-1] if n is None else n
    xh = x * rstd
    dxh = dy * g
    dx = rstd * (dxh - xh * (jnp.sum(dxh * xh, axis=-1, keepdims=True) * (1.0 / n)))
    return dx, dy * xh


def _valid_rows(i, tm, t_valid_end):
    r = i * tm + lax.broadcasted_iota(jnp.int32, (tm, 1), 0)
    return ((r >= FRONT) & (r < t_valid_end)).astype(F32)


def _colsum8(x):
    n, c = x.shape
    return jnp.sum(x.reshape(n // 8, 8, c), axis=0)


def _in_proj_fwd(x, g1, w):
    t = x.shape[0]
    tm = _tile(t, 192)
    segs = (SEG_GATES, SEG_AG, SEG_H4, SEG_CQ, SEG_CKV, SEG_KR)

    def body(x_ref, g_ref, w_ref, gates_ref, ag_ref, h4_ref, cq_ref, ckv_ref, kr_ref, hb_ref):
        xv = x_ref[...]
        hb = (xv * _rstd(xv) * g_ref[...]).astype(_MM)
        hb_ref[...] = hb
        for ref, (a, b) in zip((gates_ref, ag_ref, h4_ref, cq_ref, ckv_ref, kr_ref), segs):
            ref[...] = jnp.dot(hb, w_ref[:, a:b], preferred_element_type=F32)

    outs = [jax.ShapeDtypeStruct((t, b - a), F32) for a, b in segs] + [jax.ShapeDtypeStruct((t, D), _MM)]
    return pl.pallas_call(
        body, name="in_proj_fwd", grid=(t // tm,),
        in_specs=[_row(tm, D), _full((1, D)), _full((D, N_IN_P))],
        out_specs=[_row(tm, b - a) for a, b in segs] + [_row(tm, D)],
        out_shape=outs, compiler_params=_cp("parallel"),
    )(x, g1, w)


def _in_proj_bwd(du, x, dx1, g1, wt, t_end):
    t = x.shape[0]
    tm = _tile(t, 192)

    def body(du_ref, x_ref, dx1_ref, g_ref, wt_ref, dx_ref, dg_ref):
        i = pl.program_id(0)
        dh = jnp.dot(du_ref[...], wt_ref[...], preferred_element_type=F32)
        xv = x_ref[...]
        dxn, dgrow = _rms_bwd(dh, xv, _rstd(xv), g_ref[...])
        dx_ref[...] = _valid_rows(i, tm, t_end) * (dx1_ref[...] + dxn)

        @pl.when(i == 0)
        def _():
            dg_ref[...] = jnp.zeros_like(dg_ref)
        dg_ref[...] += _colsum8(dgrow)

    return pl.pallas_call(
        body, name="in_proj_bwd", grid=(t // tm,),
        in_specs=[_row(tm, N_IN_P), _row(tm, D), _row(tm, D), _full((1, D)), _full((N_IN_P, D))],
        out_specs=[_row(tm, D), _full((8, D))],
        out_shape=[jax.ShapeDtypeStruct((t, D), F32), jax.ShapeDtypeStruct((8, D), F32)],
        compiler_params=_cp("arbitrary"),
    )(du, x, dx1, g1, wt)


CONV_CH = 128


def _conv_fwd(ag, cw, cb):
    t = ag.shape[0]
    n = t // CONV_CH

    def body(a_ref, g_ref, w_ref, b_ref, z_ref, hp):
        hp[0:32, :] = jnp.zeros((32, LANE), F32)

        def fill(i, c):
            r = pl.multiple_of(i * CONV_CH, CONV_CH)
            hp[pl.ds(32 + r, CONV_CH), :] = a_ref[pl.ds(r, CONV_CH), :] * _sigmoid(g_ref[pl.ds(r, CONV_CH), :])
            return c
        lax.fori_loop(0, n, fill, 0)

        def conv(i, c):
            r = pl.multiple_of(i * CONV_CH, CONV_CH)
            acc = jnp.broadcast_to(b_ref[...], (CONV_CH, LANE))
            for k in range(CONV_K):
                acc = acc + w_ref[k:k + 1, :] * hp[pl.ds(r + (k + 2), CONV_CH), :]
            z_ref[pl.ds(r, CONV_CH), :] = acc
            return c
        lax.fori_loop(0, n, conv, 0)

    nb = CONV_DIM // LANE
    return pl.pallas_call(
        body, name="conv_fwd", grid=(nb,),
        in_specs=[pl.BlockSpec((t, LANE), lambda j: (0, j)), pl.BlockSpec((t, LANE), lambda j: (0, nb + j)),
                  pl.BlockSpec((32, LANE), lambda j: (0, j)), pl.BlockSpec((1, LANE), lambda j: (0, j))],
        out_specs=pl.BlockSpec((t, LANE), lambda j: (0, j)),
        out_shape=jax.ShapeDtypeStruct((t, CONV_DIM), F32),
        scratch_shapes=[pltpu.VMEM((t + 32, LANE), F32)],
        compiler_params=_cp("parallel"),
    )(ag, ag, cw, cb)


def _conv_bwd(ag, cw, dz):
    t = ag.shape[0]
    n = t // CONV_CH

    def body(a_ref, g_ref, w_ref, dz_ref, da_ref, dg_ref, dcw_ref, hp, dzp, accw):
        hp[0:32, :] = jnp.zeros((32, LANE), F32)
        dzp[pl.ds(t, 32), :] = jnp.zeros((32, LANE), F32)
        accw[...] = jnp.zeros_like(accw)

        def fill(i, c):
            r = pl.multiple_of(i * CONV_CH, CONV_CH)
            hp[pl.ds(32 + r, CONV_CH), :] = a_ref[pl.ds(r, CONV_CH), :] * _sigmoid(g_ref[pl.ds(r, CONV_CH), :])
            dzp[pl.ds(r, CONV_CH), :] = dz_ref[pl.ds(r, CONV_CH), :]
            return c
        lax.fori_loop(0, n, fill, 0)

        def step(i, c):
            r = pl.multiple_of(i * CONV_CH, CONV_CH)
            dzc = dz_ref[pl.ds(r, CONV_CH), :]
            dh = jnp.zeros((CONV_CH, LANE), F32)
            for k in range(CONV_K):
                dh = dh + w_ref[k:k + 1, :] * dzp[pl.ds(r + (CONV_K - 1 - k), CONV_CH), :]
                accw[8 * k:8 * k + 8, :] += _colsum8(dzc * hp[pl.ds(r + (k + 2), CONV_CH), :])
            a = a_ref[pl.ds(r, CONV_CH), :]
            sg = _sigmoid(g_ref[pl.ds(r, CONV_CH), :])
            da_ref[pl.ds(r, CONV_CH), :] = dh * sg
            dg_ref[pl.ds(r, CONV_CH), :] = dh * a * sg * (1.0 - sg)
            return c
        lax.fori_loop(0, n, step, 0)

        for k in range(CONV_K):
            dcw_ref[k:k + 1, :] = jnp.sum(accw[8 * k:8 * k + 8, :], axis=0, keepdims=True)
        dcw_ref[CONV_K:32, :] = jnp.zeros((32 - CONV_K, LANE), F32)

    nb = CONV_DIM // LANE
    colspec = pl.BlockSpec((t, LANE), lambda j: (0, j))
    return pl.pallas_call(
        body, name="conv_bwd", grid=(nb,),
        in_specs=[colspec, pl.BlockSpec((t, LANE), lambda j: (0, nb + j)),
                  pl.BlockSpec((32, LANE), lambda j: (0, j)), colspec],
        out_specs=[colspec, colspec, pl.BlockSpec((32, LANE), lambda j: (0, j))],
        out_shape=[jax.ShapeDtypeStruct((t, CONV_DIM), F32), jax.ShapeDtypeStruct((t, CONV_DIM), F32),
                   jax.ShapeDtypeStruct((32, CONV_DIM), F32)],
        scratch_shapes=[pltpu.VMEM((t + 32, LANE), F32), pltpu.VMEM((t + 32, LANE), F32),
                        pltpu.VMEM((8 * 32, LANE), F32)],
        compiler_params=_cp("parallel"),
    )(ag, ag, cw, dz)


def _rope(x, c, s1, s2):
    return x * c + pltpu.roll(x, LANE - 16, 1) * s1 + pltpu.roll(x, 16, 1) * s2


def _rope_t(dy, c, s1, s2):
    return dy * c + pltpu.roll(dy * s1, 16, 1) + pltpu.roll(dy * s2, LANE - 16, 1)


def _mla_pre_fwd(cq, ckv, kr, qag, wuq, kvag, wk, wv, qng, kng, rc, rs1, rs2):
    t = cq.shape[0]
    tm = _tile(t, 384)

    def body(cq_ref, ckv_ref, kr_ref, qag_ref, wuq_ref, kvag_ref, wk_ref, wv_ref, qng_ref, kng_ref,
             c_ref, s1_ref, s2_ref, q_ref, k_ref, v_ref, cqn_ref, ckvn_ref):
        cqv = cq_ref[...]
        cqn = (cqv * _rstd(cqv) * qag_ref[...]).astype(_MM)
        cqn_ref[...] = cqn
        ckvv = ckv_ref[...]
        ckvn = (ckvv * _rstd(ckvv) * kvag_ref[...]).astype(_MM)
        ckvn_ref[...] = ckvn
        qraw = jnp.dot(cqn, wuq_ref[...], preferred_element_type=F32)
        kraw = jnp.dot(ckvn, wk_ref[...], preferred_element_type=F32)
        v_ref[...] = jnp.dot(ckvn, wv_ref[...], preferred_element_type=F32).astype(_MM)
        krv = kr_ref[...]
        c, s1, s2 = c_ref[...], s1_ref[...], s2_ref[...]
        for h in range(NH):
            sl = slice(LANE * h, LANE * (h + 1))
            qh = qraw[:, sl]
            qn = qh * _rstd(qh, QK_DIM) * qng_ref[...]
            q_ref[:, sl] = (_rope(qn, c, s1, s2) * ATT_SCALE).astype(_MM)
            kh = kraw[:, sl] + krv
            kn = kh * _rstd(kh, QK_DIM) * kng_ref[...]
            k_ref[:, sl] = _rope(kn, c, s1, s2).astype(_MM)

    hd = NH * LANE
    return pl.pallas_call(
        body, name="mla_pre_fwd", grid=(t // tm,),
        in_specs=[_row(tm, 256), _row(tm, 128), _row(tm, 128), _full((1, 256)), _full((256, hd)),
                  _full((1, 128)), _full((128, hd)), _full((128, hd)), _full((1, LANE)), _full((1, LANE)),
                  _row(tm, LANE), _row(tm, LANE), _row(tm, LANE)],
        out_specs=[_row(tm, hd), _row(tm, hd), _row(tm, hd), _row(tm, 256), _row(tm, 128)],
        out_shape=[jax.ShapeDtypeStruct((t, hd), _MM)] * 3 + [jax.ShapeDtypeStruct((t, 256), _MM),
                                                              jax.ShapeDtypeStruct((t, 128), _MM)],
        compiler_params=_cp("parallel"),
    )(cq, ckv, kr, qag, wuq, kvag, wk, wv, qng, kng, rc, rs1, rs2)


def _mla_pre_bwd(dq, dk, dv, cq, ckv, kr, qag, wuq, kvag, wk, wv, qng, kng, rc, rs1, rs2):
    t = cq.shape[0]
    tm = _tile(t, 192)
    hd = NH * LANE

    def body(dq_ref, dk_ref, dv_ref, cq_ref, ckv_ref, kr_ref, qag_ref, wuq_ref, kvag_ref, wk_ref,
             wv_ref, qng_ref, kng_ref, c_ref, s1_ref, s2_ref,
             dcq_ref, dckv_ref, dkr_ref, dqraw_ref, dkraw_ref, dqag_ref, dkvag_ref, dqng_ref, dkng_ref):
        i = pl.program_id(0)
        cqv = cq_ref[...]
        rq_in = _rstd(cqv)
        cqn = (cqv * rq_in * qag_ref[...]).astype(_MM)
        ckvv = ckv_ref[...]
        rkv_in = _rstd(ckvv)
        ckvn = (ckvv * rkv_in * kvag_ref[...]).astype(_MM)
        qraw = jnp.dot(cqn, wuq_ref[...], preferred_element_type=F32)
        kraw = jnp.dot(ckvn, wk_ref[...], preferred_element_type=F32)
        krv = kr_ref[...]
        c, s1, s2 = c_ref[...], s1_ref[...], s2_ref[...]
        dkr = jnp.zeros((tm, LANE), F32)
        dqng = jnp.zeros((8, LANE), F32)
        dkng = jnp.zeros((8, LANE), F32)
        for h in range(NH):
            sl = slice(LANE * h, LANE * (h + 1))
            qh = qraw[:, sl]
            dqn = _rope_t(dq_ref[:, sl] * ATT_SCALE, c, s1, s2)
            dqh, gq = _rms_bwd(dqn, qh, _rstd(qh, QK_DIM), qng_ref[...], QK_DIM)
            dqraw_ref[:, sl] = dqh.astype(_MM)
            dqng = dqng + _colsum8(gq)
            kh = kraw[:, sl] + krv
            dkn = _rope_t(dk_ref[:, sl], c, s1, s2)
            dkh, gk = _rms_bwd(dkn, kh, _rstd(kh, QK_DIM), kng_ref[...], QK_DIM)
            dkraw_ref[:, sl] = dkh.astype(_MM)
            dkr = dkr + dkh
            dkng = dkng + _colsum8(gk)
        dkr_ref[...] = dkr.astype(_MM)
        dcqn = _mm_nt(dqraw_ref[...], wuq_ref[...])
        dcq, gqa = _rms_bwd(dcqn, cqv, rq_in, qag_ref[...])
        dcq_ref[...] = dcq.astype(_MM)
        dckvn = _mm_nt(dkraw_ref[...], wk_ref[...]) + _mm_nt(dv_ref[...], wv_ref[...])
        dckv, gkva = _rms_bwd(dckvn, ckvv, rkv_in, kvag_ref[...])
        dckv_ref[...] = dckv.astype(_MM)

        @pl.when(i == 0)
        def _():
            dqag_ref[...] = jnp.zeros_like(dqag_ref)
            dkvag_ref[...] = jnp.zeros_like(dkvag_ref)
            dqng_ref[...] = jnp.zeros_like(dqng_ref)
            dkng_ref[...] = jnp.zeros_like(dkng_ref)
        dqag_ref[...] += _colsum8(gqa)
        dkvag_ref[...] += _colsum8(gkva)
        dqng_ref[...] += dqng
        dkng_ref[...] += dkng

    return pl.pallas_call(
        body, name="mla_pre_bwd", grid=(t // tm,),
        in_specs=[_row(tm, hd), _row(tm, hd), _row(tm, hd), _row(tm, 256), _row(tm, 128), _row(tm, 128),
                  _full((1, 256)), _full((256, hd)), _full((1, 128)), _full((128, hd)),
                  _full((128, hd)), _full((1, LANE)), _full((1, LANE)),
                  _row(tm, LANE), _row(tm, LANE), _row(tm, LANE)],
        out_specs=[_row(tm, 256), _row(tm, 128), _row(tm, 128), _row(tm, hd), _row(tm, hd),
                   _full((8, 256)), _full((8, 128)), _full((8, LANE)), _full((8, LANE))],
        out_shape=[jax.ShapeDtypeStruct((t, 256), _MM), jax.ShapeDtypeStruct((t, 128), _MM),
                   jax.ShapeDtypeStruct((t, 128), _MM), jax.ShapeDtypeStruct((t, hd), _MM),
                   jax.ShapeDtypeStruct((t, hd), _MM), jax.ShapeDtypeStruct((8, 256), F32),
                   jax.ShapeDtypeStruct((8, 128), F32), jax.ShapeDtypeStruct((8, LANE), F32),
                   jax.ShapeDtypeStruct((8, LANE), F32)],
        compiler_params=_cp("arbitrary"),
    )(dq, dk, dv, cq, ckv, kr, qag, wuq, kvag, wk, wv, qng, kng, rc, rs1, rs2)


ATT_TILE = 704


def _attn_mask(r0, c0, tq):
    rows = r0 + lax.broadcasted_iota(jnp.int32, (tq, 1), 0)
    cols = c0 + lax.broadcasted_iota(jnp.int32, (1, tq), 1)
    return (cols <= rows) & (cols >= FRONT)


def _attn_fwd(q, k, v, plan=None):
    t = q.shape[0]
    tq = _tile(t, ATT_TILE)
    nq = t // tq
    p_args, p_in, p_out, p_shape, p_sem = _plan_specs(plan)

    def body(*refs):
        ((q_ref, k_ref, v_ref), (o_ref, lse_ref), _), rider = _host_refs(refs, 3, 2, 0, plan)
        done = _ride(plan, rider, pl.program_id(0), NH - 1)

        def qloop(qi, carry):
            r0 = pl.multiple_of(qi * tq, tq)
            qb = q_ref[pl.ds(r0, tq), :]

            def kstep(kj, st, masked):
                m, l, acc = st
                c0 = pl.multiple_of(kj * tq, tq)
                s = _mm_nt(qb, k_ref[pl.ds(c0, tq), :])
                if masked:
                    s = jnp.where(_attn_mask(r0, c0, tq), s, NEG)
                m2 = jnp.maximum(m, jnp.max(s, axis=-1, keepdims=True))
                p = jnp.exp(s - m2)
                a = jnp.exp(m - m2)
                l = a * l + jnp.sum(p, axis=-1, keepdims=True)
                acc = a * acc + _mm(p, v_ref[pl.ds(c0, tq), :])
                return m2, l, acc

            st = kstep(0, (jnp.full((tq, 1), NEG, F32), jnp.zeros((tq, 1), F32), jnp.zeros((tq, LANE), F32)), True)
            st = lax.fori_loop(1, qi, lambda kj, s_: kstep(kj, s_, False), st)
            m, l, acc = lax.cond(qi > 0, lambda s_: kstep(qi, s_, True), lambda s_: s_, st)
            o_ref[pl.ds(r0, tq), :] = acc / l
            lse_ref[pl.ds(r0, tq), :] = m + jnp.log(l)
            return carry
        lax.fori_loop(0, nq, qloop, 0)
        done()

    hs = pl.BlockSpec((t, LANE), lambda h: (0, h))
    res = pl.pallas_call(
        body, name="attn_fwd", grid=(NH,),
        in_specs=[hs, hs, hs] + p_in,
        out_specs=[hs, pl.BlockSpec((None, t, 1), lambda h: (h, 0, 0))] + p_out,
        out_shape=[jax.ShapeDtypeStruct((t, NH * LANE), F32), jax.ShapeDtypeStruct((NH, t, 1), F32)] + p_shape,
        scratch_shapes=p_sem,
        compiler_params=_cp("parallel" if plan is None else "arbitrary"),
    )(q, k, v, *p_args)
    return res[:2], res[2:]


def _attn_bwd(q, k, v, o, lse, do, plan=None):
    t = q.shape[0]
    tq = _tile(t, ATT_TILE)
    nq = t // tq
    p_args, p_in, p_out, p_shape, p_sem = _plan_specs(plan)

    def body(*refs):
        (ins, (dq_ref, dk_ref, dv_ref), (delta,)), rider = _host_refs(refs, 6, 3, 1, plan)
        q_ref, k_ref, v_ref, o_ref, lse_ref, do_ref = ins
        done = _ride(plan, rider, pl.program_id(0), NH - 1)

        def prep(i, c):
            r0 = pl.multiple_of(i * tq, tq)
            delta[pl.ds(r0, tq), :] = jnp.sum(do_ref[pl.ds(r0, tq), :] * o_ref[pl.ds(r0, tq), :], axis=-1,
                                              keepdims=True)
            dq_ref[pl.ds(r0, tq), :] = jnp.zeros((tq, LANE), F32)
            return c
        lax.fori_loop(0, nq, prep, 0)

        def kloop(kj, carry):
            c0 = pl.multiple_of(kj * tq, tq)
            kb = k_ref[pl.ds(c0, tq), :]
            vb = v_ref[pl.ds(c0, tq), :]

            def qstep(qi, st, masked):
                dkb, dvb = st
                r0 = pl.multiple_of(qi * tq, tq)
                qb = q_ref[pl.ds(r0, tq), :]
                dob = do_ref[pl.ds(r0, tq), :].astype(_MM)
                s = _mm_nt(qb, kb)
                if masked:
                    s = jnp.where(_attn_mask(r0, c0, tq), s, NEG)
                p = jnp.exp(s - lse_ref[pl.ds(r0, tq), :])
                dvb = dvb + _mm_tn(p, dob)
                dp = _mm_nt(dob, vb)
                ds = (p * (dp - delta[pl.ds(r0, tq), :])).astype(_MM)
                dkb = dkb + _mm_tn(ds, qb)
                dq_ref[pl.ds(r0, tq), :] += _mm(ds, kb)
                return dkb, dvb

            st = qstep(kj, (jnp.zeros((tq, LANE), F32), jnp.zeros((tq, LANE), F32)), True)
            dkb, dvb = lax.cond(
                kj == 0,
                lambda s_: lax.fori_loop(kj + 1, nq, lambda qi, t_: qstep(qi, t_, True), s_),
                lambda s_: lax.fori_loop(kj + 1, nq, lambda qi, t_: qstep(qi, t_, False), s_), st)
            dk_ref[pl.ds(c0, tq), :] = dkb
            dv_ref[pl.ds(c0, tq), :] = dvb
            return carry
        lax.fori_loop(0, nq, kloop, 0)
        done()

    hs = pl.BlockSpec((t, LANE), lambda h: (0, h))
    res = pl.pallas_call(
        body, name="attn_bwd", grid=(NH,),
        in_specs=[hs, hs, hs, hs, pl.BlockSpec((None, t, 1), lambda h: (h, 0, 0)), hs] + p_in,
        out_specs=[hs, hs, hs] + p_out,
        out_shape=[jax.ShapeDtypeStruct((t, NH * LANE), F32)] * 3 + p_shape,
        scratch_shapes=[pltpu.VMEM((t, 1), F32)] + p_sem,
        compiler_params=_cp("parallel" if plan is None else "arbitrary"),
    )(q, k, v, o, lse, do, *p_args)
    return res[:3], res[3:]


def _cumsum_rows(x):
    n = x.shape[0]
    rows = lax.broadcasted_iota(jnp.int32, (n, 1), 0)
    d = 1
    while d < n:
        x = x + jnp.where(rows >= d, pltpu.roll(x, d, 0), 0.0)
        d *= 2
    return x


def _revcumsum_rows(x):
    n = x.shape[0]
    rows = lax.broadcasted_iota(jnp.int32, (n, 1), 0)
    d = 1
    while d < n:
        x = x + jnp.where(rows < n - d, pltpu.roll(x, n - d, 0), 0.0)
        d *= 2
    return x


def _hgrn_gates(f, lb):
    sneg = _sigmoid(-f)
    kk = (1.0 - lb) * sneg
    lf = jnp.log1p(-jnp.minimum(kk, GATE_CLAMP))
    return kk, lf, sneg


def _silu(x):
    return x * _sigmoid(x)


def _dsilu(x):
    s = _sigmoid(x)
    return s * (1.0 + x * (1.0 - s))


def _hgrn_intra(q, kk, b):
    parts = []
    for blk in range(CHUNK // SUB):
        lo = blk * SUB
        ref = jnp.zeros((1, LANE), F32) if blk == 0 else b[lo - 1:lo, :]
        eq = jnp.exp(b[lo:lo + SUB, :] - ref)
        ek = jnp.exp(jnp.minimum(ref - b, EXP_CLIP))
        parts.append((q[lo:lo + SUB, :] * eq, kk * ek, eq, ek))
    return parts


def _chunk_causal():
    return lax.broadcasted_iota(jnp.int32, (CHUNK, CHUNK), 1) <= lax.broadcasted_iota(jnp.int32, (CHUNK, CHUNK), 0)


def _hgrn_fwd(h4, lb):
    t = h4.shape[0]
    nc = t // CHUNK

    def body(q_ref, f_ref, i_ref, lb_ref, o_ref, s_ref, st):
        st[...] = jnp.zeros_like(st)
        causal = _chunk_causal()

        def chunk(c, carry):
            r0 = pl.multiple_of(c * CHUNK, CHUNK)
            q = q_ref[pl.ds(r0, CHUNK), :]
            kk, lf, _ = _hgrn_gates(f_ref[pl.ds(r0, CHUNK), :], lb_ref[...])
            v = _silu(i_ref[pl.ds(r0, CHUNK), :])
            b = _cumsum_rows(lf)
            s_prev = st[...]
            s_ref[c] = s_prev
            o = _hmm_nt(q * jnp.exp(b), s_prev)
            a = jnp.concatenate([_hmm_nt(qs, ks) for qs, ks, _, _ in _hgrn_intra(q, kk, b)], axis=0)
            a = jnp.where(causal, a, 0.0)
            o_ref[pl.ds(r0, CHUNK), :] = o + _hmm(a, v)
            bl = b[CHUNK - 1:CHUNK, :]
            st[...] = s_prev * jnp.exp(bl) + _hmm_tn(v, kk * jnp.exp(bl - b))
            return carry
        lax.fori_loop(0, nc, chunk, 0, unroll=2)

    def col(j):
        return pl.BlockSpec((t, LANE), lambda h: (0, HH * j + h))
    return pl.pallas_call(
        body, name="hgrn_fwd", grid=(HH,),
        in_specs=[col(0), col(1), col(2), pl.BlockSpec((1, LANE), lambda h: (0, h))],
        out_specs=[pl.BlockSpec((t, LANE), lambda h: (0, h)),
                   pl.BlockSpec((None, nc, LANE, LANE), lambda h: (h, 0, 0, 0))],
        out_shape=[jax.ShapeDtypeStruct((t, HH * LANE), F32), jax.ShapeDtypeStruct((HH, nc, LANE, LANE), F32)],
        scratch_shapes=[pltpu.VMEM((LANE, LANE), F32)],
        compiler_params=_cp("parallel"),
    )(h4, h4, h4, lb)


def _hgrn_bwd(h4, lb, do, states, plan=None):
    t = h4.shape[0]
    nc = t // CHUNK
    p_args, p_in, p_out, p_shape, p_sem = _plan_specs(plan)

    def body(*refs):
        (ins, outs, (dst, carry)), rider = _host_refs(refs, 6, 4, 2, plan)
        q_ref, f_ref, i_ref, lb_ref, do_ref, s_ref = ins
        dq_ref, df_ref, di_ref, dlb_ref = outs
        done = _ride(plan, rider, pl.program_id(0), HH - 1)
        dst[...] = jnp.zeros_like(dst)
        carry[...] = jnp.zeros_like(carry)
        dlb_ref[...] = jnp.zeros_like(dlb_ref)
        causal = _chunk_causal()

        def chunk(cc, cr):
            c = nc - 1 - cc
            r0 = pl.multiple_of(c * CHUNK, CHUNK)
            q = q_ref[pl.ds(r0, CHUNK), :]
            lbv = lb_ref[...]
            kk, lf, sneg = _hgrn_gates(f_ref[pl.ds(r0, CHUNK), :], lbv)
            iv = i_ref[pl.ds(r0, CHUNK), :]
            v = _silu(iv)
            b = _cumsum_rows(lf)
            s_prev = s_ref[c]
            ds_new = dst[...]
            dob = do_ref[pl.ds(r0, CHUNK), :]
            e = jnp.exp(b)
            qe = q * e
            bl = b[CHUNK - 1:CHUNK, :]
            etail = jnp.exp(bl - b)
            kd = kk * etail
            dq_inter = _hmm(dob, s_prev) * e
            dv = _hmm_nt(kd, ds_new)
            dkk = _hmm(v, ds_new) * etail
            parts = _hgrn_intra(q, kk, b)
            a = jnp.where(causal, jnp.concatenate([_hmm_nt(qs, ks) for qs, ks, _, _ in parts], axis=0), 0.0)
            da = jnp.where(causal, _hmm_nt(dob, v), 0.0)
            dv = dv + _hmm_tn(a, dob)
            dq_rows = []
            for blk, (qs, ks, eq, ek) in enumerate(parts):
                da_blk = da[blk * SUB:(blk + 1) * SUB, :]
                dq_rows.append(_hmm(da_blk, ks) * eq)
                dkk = dkk + _hmm_tn(da_blk, qs) * ek
            dq = dq_inter + jnp.concatenate(dq_rows, axis=0)
            dst[...] = ds_new * jnp.exp(bl) + _hmm_tn(dob, qe)
            g = q * dq - kk * dkk
            dlf = _revcumsum_rows(g) + carry[0:1, :]
            carry[0:1, :] += jnp.sum(g, axis=0, keepdims=True)
            dkk_tot = dkk + dlf * jnp.where(kk < GATE_CLAMP, -1.0 / (1.0 - kk), 0.0)
            dq_ref[pl.ds(r0, CHUNK), :] = dq
            df_ref[pl.ds(r0, CHUNK), :] = dkk_tot * (1.0 - lbv) * (-sneg * (1.0 - sneg))
            di_ref[pl.ds(r0, CHUNK), :] = dv * _dsilu(iv)
            dlb_ref[...] += _colsum8(dkk_tot * (-sneg))
            return cr
        lax.fori_loop(0, nc, chunk, 0, unroll=2)
        done()

    def col(j):
        return pl.BlockSpec((t, LANE), lambda h: (0, HH * j + h))
    hs = pl.BlockSpec((t, LANE), lambda h: (0, h))
    res = pl.pallas_call(
        body, name="hgrn_bwd", grid=(HH,),
        in_specs=[col(0), col(1), col(2), pl.BlockSpec((1, LANE), lambda h: (0, h)), hs,
                  pl.BlockSpec((None, nc, LANE, LANE), lambda h: (h, 0, 0, 0))] + p_in,
        out_specs=[hs, hs, hs, pl.BlockSpec((8, LANE), lambda h: (0, h))] + p_out,
        out_shape=[jax.ShapeDtypeStruct((t, HH * LANE), F32)] * 3 + [jax.ShapeDtypeStruct((8, HH * LANE), F32)]
        + p_shape,
        scratch_shapes=[pltpu.VMEM((LANE, LANE), F32), pltpu.VMEM((8, LANE), F32)] + p_sem,
        compiler_params=_cp("parallel" if plan is None else "arbitrary"),
    )(h4, h4, h4, lb, do, states, *p_args)
    return res[:4], res[4:]


def _ln_fwd(z, g, b):
    mu = jnp.mean(z, axis=-1, keepdims=True)
    zc = z - mu
    rstd = lax.rsqrt(jnp.mean(zc * zc, axis=-1, keepdims=True) + EPS)
    zh = zc * rstd
    return zh * g + b, zh, rstd


def _mix_fwd(x, z, o_att, o_h, h4, gates, lng, lnb, wco, wao, ng, who, wout, t_end):
    t = x.shape[0]
    tm = _tile(t, 192)

    def body(x_ref, z_ref, oa_ref, oh_ref, hg_ref, gt_ref, lng_ref, lnb_ref, wco_ref, wao_ref, ng_ref, who_ref,
             wout_ref, x1_ref, mix_ref, ca_ref, oc_ref, ya_ref, yb_ref, yc_ref):
        i = pl.program_id(0)
        ln, _, _ = _ln_fwd(z_ref[...], lng_ref[...], lnb_ref[...])
        ca = _silu(ln).astype(_MM)
        ca_ref[...] = ca
        ya = jnp.dot(ca, wco_ref[...], preferred_element_type=F32)
        yb = _mm(oa_ref[...], wao_ref[...])
        hg = hg_ref[...]
        for h in range(HH):
            sl = slice(LANE * h, LANE * (h + 1))
            oh = oh_ref[:, sl]
            oc_ref[:, sl] = (oh * _rstd(oh) * ng_ref[:, sl] * _silu(hg[:, sl])).astype(_MM)
        yc = jnp.dot(oc_ref[...], who_ref[...], preferred_element_type=F32)
        ya_ref[...] = ya
        yb_ref[...] = yb
        yc_ref[...] = yc
        mix = (_sigmoid(gt_ref[:, 0:D]) * ya + _sigmoid(gt_ref[:, D:2 * D]) * yb
               + _sigmoid(gt_ref[:, 2 * D:3 * D]) * yc).astype(_MM)
        mix_ref[...] = mix
        x1_ref[...] = x_ref[...] + _valid_rows(i, tm, t_end) * jnp.dot(mix, wout_ref[...],
                                                                       preferred_element_type=F32)

    hd = NH * LANE
    return pl.pallas_call(
        body, name="mix_fwd", grid=(t // tm,),
        in_specs=[_row(tm, D), _row(tm, CONV_DIM), _row(tm, hd), _row(tm, 512), _row(tm, 512, 3), _row(tm, 3 * D),
                  _full((1, 512)), _full((1, 512)), _full((512, D)), _full((hd, D)), _full((1, 512)),
                  _full((512, D)), _full((D, D))],
        out_specs=[_row(tm, D), _row(tm, D), _row(tm, 512), _row(tm, 512), _row(tm, D), _row(tm, D), _row(tm, D)],
        out_shape=[jax.ShapeDtypeStruct((t, D), F32), jax.ShapeDtypeStruct((t, D), _MM),
                   jax.ShapeDtypeStruct((t, 512), _MM), jax.ShapeDtypeStruct((t, 512), _MM),
                   jax.ShapeDtypeStruct((t, D), F32), jax.ShapeDtypeStruct((t, D), F32),
                   jax.ShapeDtypeStruct((t, D), F32)],
        compiler_params=_cp("parallel"),
    )(x, z, o_att, o_h, h4, gates, lng, lnb, wco, wao, ng, who, wout)


def _mix_bwd(dx1, ya, yb, yc, gates, z, o_h, h4, lng, lnb, ng, wout, wco, wao, who, plan=None):
    t = dx1.shape[0]
    tm = _tile(t, 192)
    hd = NH * LANE
    p_args, p_in, p_out, p_shape, p_sem = _plan_specs(plan)

    def body(*refs):
        (ins, outs, _), rider = _host_refs(refs, 15, 12, 0, plan)
        (dx1_ref, ya_ref, yb_ref, yc_ref, gt_ref, z_ref, oh_ref, hg_ref, lng_ref, lnb_ref, ng_ref,
         wout_ref, wco_ref, wao_ref, who_ref) = ins
        (dgt_ref, dya_ref, dyb_ref, dyc_ref, dz_ref, doa_ref, doh_ref, dhg_ref,
         dlng_ref, dlnb_ref, dcb_ref, dng_ref) = outs
        i = pl.program_id(0)
        done = _ride(plan, rider, i, t // tm - 1)
        dmix = _mm_nt(dx1_ref[...], wout_ref[...])
        dys = []
        for j, y_ref in enumerate((ya_ref, yb_ref, yc_ref)):
            sg = _sigmoid(gt_ref[:, j * D:(j + 1) * D])
            dgt_ref[:, j * D:(j + 1) * D] = (dmix * y_ref[...] * sg * (1.0 - sg)).astype(_MM)
            dys.append((dmix * sg).astype(_MM))
        dya_ref[...], dyb_ref[...], dyc_ref[...] = dys
        dca = _mm_nt(dys[0], wco_ref[...])
        ln, zh, rstd = _ln_fwd(z_ref[...], lng_ref[...], lnb_ref[...])
        dln = dca * _dsilu(ln)
        dzh = dln * lng_ref[...]
        dz = rstd * (dzh - jnp.mean(dzh, axis=-1, keepdims=True)
                     - zh * jnp.mean(dzh * zh, axis=-1, keepdims=True))
        dz_ref[...] = dz
        doa_ref[...] = _mm_nt(dys[1], wao_ref[...])
        doc = _mm_nt(dys[2], who_ref[...])
        hg = hg_ref[...]
        dng_rows = []
        for h in range(HH):
            sl = slice(LANE * h, LANE * (h + 1))
            oh = oh_ref[:, sl]
            r = _rstd(oh)
            don = doc[:, sl] * _silu(hg[:, sl])
            dhg_ref[:, sl] = (doc[:, sl] * oh * r * ng_ref[:, sl] * _dsilu(hg[:, sl])).astype(_MM)
            doh, gn = _rms_bwd(don, oh, r, ng_ref[:, sl])
            doh_ref[:, sl] = doh
            dng_rows.append(_colsum8(gn))

        @pl.when(i == 0)
        def _():
            dlng_ref[...] = jnp.zeros_like(dlng_ref)
            dlnb_ref[...] = jnp.zeros_like(dlnb_ref)
            dcb_ref[...] = jnp.zeros_like(dcb_ref)
            dng_ref[...] = jnp.zeros_like(dng_ref)
        dlng_ref[...] += _colsum8(dln * zh)
        dlnb_ref[...] += _colsum8(dln)
        dcb_ref[...] += _colsum8(dz)
        dng_ref[...] += jnp.concatenate(dng_rows, axis=1)
        done()

    res = pl.pallas_call(
        body, name="mix_bwd", grid=(t // tm,),
        in_specs=[_row(tm, D), _row(tm, D), _row(tm, D), _row(tm, D), _row(tm, 3 * D), _row(tm, 512), _row(tm, 512),
                  _row(tm, 512, 3), _full((1, 512)), _full((1, 512)), _full((1, 512)),
                  _full((D, D)), _full((512, D)), _full((hd, D)), _full((512, D))] + p_in,
        out_specs=[_row(tm, 3 * D), _row(tm, D), _row(tm, D), _row(tm, D), _row(tm, 512), _row(tm, hd),
                   _row(tm, 512), _row(tm, 512), _full((8, 512)), _full((8, 512)), _full((8, 512)),
                   _full((8, 512))] + p_out,
        out_shape=[jax.ShapeDtypeStruct((t, 3 * D), _MM), jax.ShapeDtypeStruct((t, D), _MM),
                   jax.ShapeDtypeStruct((t, D), _MM), jax.ShapeDtypeStruct((t, D), _MM),
                   jax.ShapeDtypeStruct((t, 512), F32), jax.ShapeDtypeStruct((t, hd), F32),
                   jax.ShapeDtypeStruct((t, 512), F32), jax.ShapeDtypeStruct((t, 512), _MM)]
        + [jax.ShapeDtypeStruct((8, 512), F32)] * 4 + p_shape,
        scratch_shapes=p_sem,
        compiler_params=_cp("arbitrary"),
    )(dx1, ya, yb, yc, gates, z, o_h, h4, lng, lnb, ng, wout, wco, wao, who, *p_args)
    return res[:12], res[12:]


D_FF = 4096


def _ffn_fwd(x1, g2, w1, w2):
    t = x1.shape[0]
    tm = _tile(t, 192)

    def body(x1_ref, g_ref, w1_ref, w2_ref, x2_ref, p_ref):
        xv = x1_ref[...]
        h2 = (xv * _rstd(xv) * g_ref[...]).astype(_MM)
        p = jnp.dot(h2, w1_ref[...], preferred_element_type=F32)
        p_ref[...] = p
        r = jnp.maximum(p, 0.0)
        x2_ref[...] = xv + jnp.dot((r * r).astype(_MM), w2_ref[...], preferred_element_type=F32)

    return pl.pallas_call(
        body, name="ffn_fwd", grid=(t // tm,),
        in_specs=[_row(tm, D), _full((1, D)), _full((D, D_FF)), _full((D_FF, D))],
        out_specs=[_row(tm, D), _row(tm, D_FF)],
        out_shape=[jax.ShapeDtypeStruct((t, D), F32), jax.ShapeDtypeStruct((t, D_FF), F32)],
        compiler_params=_cp("parallel"),
    )(x1, g2, w1, w2)


def _ffn_bwd(dx2, x1, p, g2, w1t, w2t, plan=None):
    t = x1.shape[0]
    tm = _tile(t, 192)
    p_args, p_in, p_out, p_shape, p_sem = _plan_specs(plan)

    def body(*refs):
        (ins, outs, _), rider = _host_refs(refs, 6, 5, 0, plan)
        dx2_ref, x1_ref, p_ref, g_ref, w1t_ref, w2t_ref = ins
        dx1_ref, h2_ref, act_ref, dp_ref, dg_ref = outs
        i = pl.program_id(0)
        done = _ride(plan, rider, i, t // tm - 1)
        xv = x1_ref[...]
        rstd = _rstd(xv)
        h2_ref[...] = (xv * rstd * g_ref[...]).astype(_MM)
        r = jnp.maximum(p_ref[...], 0.0)
        act_ref[...] = (r * r).astype(_MM)
        dx2 = dx2_ref[...]
        da = _mm(dx2, w2t_ref[...])
        dp = (2.0 * r * da).astype(_MM)
        dp_ref[...] = dp
        dh2 = jnp.dot(dp, w1t_ref[...], preferred_element_type=F32)
        dxn, dgrow = _rms_bwd(dh2, xv, rstd, g_ref[...])
        dx1_ref[...] = dx2 + dxn

        @pl.when(i == 0)
        def _():
            dg_ref[...] = jnp.zeros_like(dg_ref)
        dg_ref[...] += _colsum8(dgrow)
        done()

    res = pl.pallas_call(
        body, name="ffn_bwd", grid=(t // tm,),
        in_specs=[_row(tm, D), _row(tm, D), _row(tm, D_FF), _full((1, D)), _full((D_FF, D)),
                  _full((D, D_FF))] + p_in,
        out_specs=[_row(tm, D), _row(tm, D), _row(tm, D_FF), _row(tm, D_FF), _full((8, D))] + p_out,
        out_shape=[jax.ShapeDtypeStruct((t, D), F32), jax.ShapeDtypeStruct((t, D), _MM),
                   jax.ShapeDtypeStruct((t, D_FF), _MM), jax.ShapeDtypeStruct((t, D_FF), _MM),
                   jax.ShapeDtypeStruct((8, D), F32)] + p_shape,
        scratch_shapes=p_sem,
        compiler_params=_cp("arbitrary"),
    )(dx2, x1, p, g2, w1t, w2t, *p_args)
    return res[:5], res[5:]


WGRAD_VMEM = 40 * 1024 * 1024


def _wgrad(a, b, name, chips=1):
    t, ka = a.shape
    nb = b.shape[1]
    cs = nb // chips
    widths = [d for d in range(cs, 0, -LANE) if cs % d == 0 and d % LANE == 0] or [cs]
    tn, tm = widths[-1], 64
    for d in widths:
        room = WGRAD_VMEM - 2 * ka * d * 4
        row_bytes = 2 * (ka * a.dtype.itemsize + d * b.dtype.itemsize) + 4 * ka
        fit = [r for r in range(64, t + 1, 64) if t % r == 0 and r * row_bytes <= room]
        if ka * d * 4 <= 16 * 1024 * 1024 and fit and (max(fit) >= 384 or d == widths[-1]):
            tn, tm = d, max(fit)
            break
    per = cs // tn

    def body(a_ref, b_ref, o_ref):
        @pl.when(pl.program_id(1) == 0)
        def _():
            o_ref[...] = jnp.zeros_like(o_ref)
        o_ref[...] += _mm_tn(a_ref[...], b_ref[...])

    if chips == 1:
        out_spec = pl.BlockSpec((ka, tn), lambda n, i: (0, n))
        out_shape = jax.ShapeDtypeStruct((ka, nb), F32)
    else:
        out_spec = pl.BlockSpec((None, ka, tn), lambda n, i: (n // per, 0, n % per))
        out_shape = jax.ShapeDtypeStruct((chips, ka, cs), F32)
    return pl.pallas_call(
        body, name="wgrad_" + name, grid=(nb // tn, t // tm),
        in_specs=[pl.BlockSpec((tm, ka), lambda n, i: (i, 0)), pl.BlockSpec((tm, tn), lambda n, i: (i, n))],
        out_specs=out_spec, out_shape=out_shape,
        compiler_params=_cp("parallel", "arbitrary"),
    )(a, b)


def _loss_head(y, target, t_end):
    t = y.shape[0]
    tm = _tile(t, 384)

    def body(y_ref, tg_ref, dy_ref, l_ref):
        i = pl.program_id(0)
        r = i * tm + lax.broadcasted_iota(jnp.int32, (tm, 1), 0)
        real = ((r >= ROW0) & (r < t_end)).astype(F32)
        diff = (y_ref[...] - tg_ref[...]) * real
        dy_ref[...] = diff * (1.0 / D)

        @pl.when(i == 0)
        def _():
            l_ref[...] = jnp.zeros_like(l_ref)
        sq = _colsum8(diff * diff)
        part = sq[:, 0:LANE]
        for j in range(1, D // LANE):
            part = part + sq[:, j * LANE:(j + 1) * LANE]
        l_ref[...] += part * (0.5 / D)

    return pl.pallas_call(
        body, name="loss_head", grid=(t // tm,),
        in_specs=[_row(tm, D), _row(tm, D)],
        out_specs=[_row(tm, D), _full((8, LANE))],
        out_shape=[jax.ShapeDtypeStruct((t, D), F32), jax.ShapeDtypeStruct((8, LANE), F32)],
        compiler_params=_cp("arbitrary"),
    )(y, target)


def _lower_bounds_fwd(logits):
    depth, n = logits.shape

    def body(l_ref, lb_ref):
        lg = l_ref[...]
        m = jnp.max(lg, axis=0, keepdims=True)
        e = jnp.exp(lg - m)
        p = e / jnp.sum(e, axis=0, keepdims=True)
        acc = jnp.zeros((1, n), F32)
        for l in range(depth):
            if l > 0:
                acc = acc + p[l:l + 1, :]
            lb_ref[l:l + 1, :] = acc

    return pl.pallas_call(body, name="lower_bounds_fwd", out_shape=jax.ShapeDtypeStruct((depth, n), F32))(logits)


def _lower_bounds_bwd(logits, dlb):
    depth, n = logits.shape

    def body(l_ref, dlb_ref, dl_ref):
        lg = l_ref[...]
        m = jnp.max(lg, axis=0, keepdims=True)
        e = jnp.exp(lg - m)
        p = e / jnp.sum(e, axis=0, keepdims=True)
        dps = [jnp.zeros((1, n), F32)]
        for j in range(1, depth):
            acc = jnp.zeros((1, n), F32)
            for l in range(j, depth):
                acc = acc + dlb_ref[l:l + 1, :]
            dps.append(acc)
        dot = jnp.zeros((1, n), F32)
        for j in range(depth):
            dot = dot + p[j:j + 1, :] * dps[j]
        for j in range(depth):
            dl_ref[j:j + 1, :] = p[j:j + 1, :] * (dps[j] - dot)

    return pl.pallas_call(body, name="lower_bounds_bwd", out_shape=jax.ShapeDtypeStruct((depth, n), F32))(logits, dlb)


def _ew_tile(rows, cols, n_arrays):
    cap = max(16, (32 * 1024 * 1024) // (8 * n_arrays * cols))
    for mult in (16, 8):
        fit = [t for t in range(mult, rows + 1, mult) if rows % t == 0 and t <= cap]
        if fit:
            return max(fit)
    return rows


def _adamw_math(w, g, m, v):
    mn = ADAM_B1 * m + (1.0 - ADAM_B1) * g
    vn = ADAM_B2 * v + (1.0 - ADAM_B2) * (g * g)
    m_hat = mn / (1.0 - ADAM_B1 ** ADAM_STEP)
    v_hat = vn / (1.0 - ADAM_B2 ** ADAM_STEP)
    return -ADAM_LR * (m_hat / (jnp.sqrt(v_hat) + ADAM_EPS) + ADAM_WD * w), mn, vn


def _adamw_layers(w, m, v, g0, g1, g_sibling, name):
    _, r, c_ = w.shape
    tr = _ew_tile(r, c_, 10)

    def body(w_ref, m_ref, v_ref, g0_ref, g1_ref, gs_ref, g_ref, d_ref, mo_ref, vo_ref):
        layer = pl.program_id(0)
        own = jnp.where(layer == 0, g0_ref[...], g1_ref[...])
        g = jnp.where(layer == lax.axis_index("c"), own, gs_ref[...])
        g_ref[...] = g
        d_ref[...], mo_ref[...], vo_ref[...] = _adamw_math(w_ref[...], g, m_ref[...], v_ref[...])

    lay = pl.BlockSpec((None, tr, c_), lambda l, i: (l, i, 0))
    flat = pl.BlockSpec((tr, c_), lambda l, i: (i, 0))
    return pl.pallas_call(
        body, name="adamw_" + name, grid=(2, r // tr),
        in_specs=[lay, lay, lay, flat, flat, flat], out_specs=[lay] * 4,
        out_shape=[jax.ShapeDtypeStruct(w.shape, F32)] * 4,
        compiler_params=_cp("parallel", "parallel"),
    )(w, m, v, g0, g1, g_sibling)


def _adamw(w, g, m, v, name):
    rows, cols = w.shape
    tr = _ew_tile(rows, cols, 7)

    def body(w_ref, g_ref, m_ref, v_ref, d_ref, mo_ref, vo_ref):
        d_ref[...], mo_ref[...], vo_ref[...] = _adamw_math(w_ref[...], g_ref[...], m_ref[...], v_ref[...])

    spec = pl.BlockSpec((tr, cols), lambda i: (i, 0))
    return pl.pallas_call(
        body, name="adamw_" + name, grid=(rows // tr,),
        in_specs=[spec] * 4, out_specs=[spec] * 3,
        out_shape=[jax.ShapeDtypeStruct((rows, cols), F32)] * 3,
        compiler_params=_cp("parallel"),
    )(w, g, m, v)


DEPTH = 2
BIG_SHAPES = {"w_in": ((1024, 6560), 1), "w_conv_out": ((512, 1024), 1), "w_uq": ((256, 768), 1),
              "w_ukv": ((128, 1024), 1), "w_attn_out": ((512, 1024), 1), "w_hgrn_out": ((512, 1024), 1),
              "w_out": ((1024, 1024), 0), "w_ff1": ((1024, 4096), 1), "w_ff2": ((4096, 1024), 0)}
BIG = tuple(BIG_SHAPES)
SMALL_SIZES = {"norm1_g": 1024, "conv_b": 512, "conv_ln_g": 512, "conv_ln_b": 512, "q_a_norm_g": 256,
               "kv_a_norm_g": 128, "q_norm_g": 96, "k_norm_g": 96, "hgrn_lb_logits": 512, "hgrn_norm_g": 512,
               "norm2_g": 1024}
SMALL = tuple(SMALL_SIZES)
W_IN_COLS = 6560
W_IN_SHARD = W_IN_COLS // 4
W_IN_SEGS = ((0, 1024, SEG_AG[0]), (1024, 1280, SEG_CQ[0]), (1280, 1408, SEG_CKV[0]), (1408, 1440, SEG_KR[0] + 64),
             (1440, 3488, SEG_H4[0]), (3488, 6560, SEG_GATES[0]))


def _pad_heads(w, nh, used, axis):
    shp = w.shape
    w = w.reshape(shp[:axis] + (nh, used) + shp[axis + 1:])
    pad = [(0, 0)] * w.ndim
    pad[axis + 1] = (0, LANE - used)
    w = jnp.pad(w, pad)
    return w.reshape(shp[:axis] + (nh * LANE,) + shp[axis + 1:])


def _unpad_heads(w, nh, used, axis):
    shp = w.shape
    w = w.reshape(shp[:axis] + (nh, LANE) + shp[axis + 1:])
    w = lax.slice_in_dim(w, 0, used, axis=axis + 1)
    return w.reshape(shp[:axis] + (nh * used,) + shp[axis + 1:])


def _w_in_from_chips(p4):
    def orig(a, b):
        out = []
        while a < b:
            s = a // W_IN_SHARD
            e = min(b, (s + 1) * W_IN_SHARD)
            out.append(p4[s][:, a - W_IN_SHARD * s:e - W_IN_SHARD * s])
            a = e
        return out
    zc = lambda n: jnp.zeros((D, n), p4[0].dtype)
    parts = (orig(3488, 6560) + orig(0, 1024) + orig(1440, 3488) + orig(1024, 1280) + orig(1280, 1408)
             + [zc(64)] + orig(1408, 1440) + [zc(32)])
    return jnp.concatenate(parts, axis=1)


def _w_in_grad_to_chips(dw):
    chips = []
    for s in range(4):
        a, b = W_IN_SHARD * s, W_IN_SHARD * (s + 1)
        parts = []
        for o0, o1, p0 in W_IN_SEGS:
            lo, hi = max(a, o0), min(b, o1)
            if lo < hi:
                parts.append(dw[:, p0 + lo - o0:p0 + hi - o0])
        chips.append(jnp.concatenate(parts, axis=1))
    return jnp.stack(chips)


def _cat_chips(p4, axis):
    return jnp.concatenate([p4[s] for s in range(4)], axis=axis)


EARLY = ("w_in", "w_uq", "w_ukv")
LATE = tuple(k for k in BIG if k not in EARLY)


def _prep_late(pieces):
    pc = lambda k: [pieces[k][s].astype(_MM) for s in range(4)]
    return dict(wao=_pad_heads(_cat_chips(pc("w_attn_out"), 1), NH, 64, 0), wco=_cat_chips(pc("w_conv_out"), 1),
                who=_cat_chips(pc("w_hgrn_out"), 1), wout=_cat_chips(pc("w_out"), 0),
                w1=_cat_chips(pc("w_ff1"), 1), w2=_cat_chips(pc("w_ff2"), 0))


def _prep_early(pieces, small, l):
    mm = lambda a: a.astype(_MM)
    pc = lambda k: [mm(pieces[k][s]) for s in range(4)]
    w_in_p = _w_in_from_chips(pc("w_in"))
    wuq = jnp.concatenate([_pad_heads(pc("w_uq")[s], 2, QK_DIM, 1) for s in range(4)], axis=1)
    wukv = _cat_chips(pc("w_ukv"), 1).reshape(128, NH, 128)
    wk = _pad_heads(wukv[:, :, :64].reshape(128, NH * 64), NH, 64, 1)
    wv = _pad_heads(wukv[:, :, 64:].reshape(128, NH * 64), NH, 64, 1)
    row = lambda a: a.astype(F32).reshape(1, -1)
    p = dict(
        w_in=w_in_p, w_in_t=w_in_p.T, wuq=wuq, wk=wk, wv=wv,
        g1=row(small["norm1_g"][l]), g2=row(small["norm2_g"][l]),
        cw=jnp.pad(small["conv_w"][l].astype(F32), ((0, 1), (0, 0))), cb=row(small["conv_b"][l]),
        lng=row(small["conv_ln_g"][l]), lnb=row(small["conv_ln_b"][l]),
        qag=row(small["q_a_norm_g"][l]), kvag=row(small["kv_a_norm_g"][l]),
        qng=jnp.pad(row(small["q_norm_g"][l]), ((0, 0), (0, LANE - QK_DIM))),
        kng=jnp.pad(row(small["k_norm_g"][l]), ((0, 0), (0, LANE - QK_DIM))),
        ng=row(small["hgrn_norm_g"][l]),
    )
    return p


def _rope_tables(t):
    pos = (jnp.arange(t, dtype=jnp.int32) - FRONT).astype(F32)
    inv_freq = 10000.0 ** (-jnp.arange(16, dtype=F32) / 16)
    ang = pos[:, None] * inv_freq[None, :]
    cos, sin = jnp.cos(ang), jnp.sin(ang)
    one = jnp.ones((t, 64), F32)
    z16, z32, z64 = jnp.zeros((t, 16), F32), jnp.zeros((t, 32), F32), jnp.zeros((t, 64), F32)
    c = jnp.concatenate([one, cos, cos, z32], axis=1)
    s1 = jnp.concatenate([z64, -sin, z16, z32], axis=1)
    s2 = jnp.concatenate([z64, z16, sin, z32], axis=1)
    return c, s1, s2


def _layer_fwd(x, p, lb, rope, t_end, plan=None, on_rode=None):
    gates, ag, h4, cq, ckv, kr, hb = _in_proj_fwd(x, p["g1"], p["w_in"])
    z = _conv_fwd(ag, p["cw"], p["cb"])
    q, k, v, cqn, ckvn = _mla_pre_fwd(cq, ckv, kr, p["qag"], p["wuq"], p["kvag"], p["wk"], p["wv"], p["qng"],
                                      p["kng"], *rope)
    (o_att, lse), rode = _attn_fwd(q, k, v, plan)
    if on_rode is not None:
        on_rode(rode)
    o_h, states = _hgrn_fwd(h4, lb)
    x1, mix, ca, oc, ya, yb, yc = _mix_fwd(x, z, o_att, o_h, h4, gates, p["lng"], p["lnb"], p["wco"], p["wao"],
                                           p["ng"], p["who"], p["wout"], t_end)
    x2, pre = _ffn_fwd(x1, p["g2"], p["w1"], p["w2"])
    saved = dict(x=x, gates=gates, ag=ag, h4=h4, cq=cq, ckv=ckv, kr=kr, hb=hb, z=z, q=q, k=k, v=v, cqn=cqn,
                 ckvn=ckvn, o_att=o_att, lse=lse, o_h=o_h, states=states, x1=x1, mix=mix, ca=ca, oc=oc,
                 ya=ya, yb=yb, yc=yc, pre=pre)
    return x2, saved


def _layer_bwd(dx2, s, p, lb, rope, t_end, rides=None):
    rides = rides or {}
    (dx1, h2, act, dp, dg2), rode = _ffn_bwd(dx2, s["x1"], s["pre"], p["g2"], p["w1"].T, p["w2"].T,
                                             rides.get("ffn"))
    g = {"w_ff1": _wgrad(h2, dp, "ff1", 4), "w_ff2": _wgrad(act, dx2, "ff2").reshape(4, D_FF // 4, D),
         "norm2_g": dg2.sum(0)}
    plan_mix = rides["mix"](rode, g) if "mix" in rides else None
    (dgt, dya, dyb, dyc, dz, doa, doh, dhg, dlng, dlnb, dcb, dng), rode = _mix_bwd(
        dx1, s["ya"], s["yb"], s["yc"], s["gates"], s["z"], s["o_h"], s["h4"], p["lng"], p["lnb"], p["ng"],
        p["wout"], p["wco"], p["wao"], p["who"], plan_mix)
    g["w_out"] = _wgrad(s["mix"], dx1, "out").reshape(4, D // 4, D)
    g["w_conv_out"] = _wgrad(s["ca"], dya, "conv_out", 4)
    g["w_attn_out"] = _unpad_heads(_wgrad(s["o_att"], dyb, "attn_out", 4), NH, 64, 1)
    g["w_hgrn_out"] = _wgrad(s["oc"], dyc, "hgrn_out", 4)
    g["conv_ln_g"], g["conv_ln_b"], g["conv_b"], g["hgrn_norm_g"] = dlng.sum(0), dlnb.sum(0), dcb.sum(0), dng.sum(0)
    da, dg, dcw = _conv_bwd(s["ag"], p["cw"], dz)
    g["conv_w"] = dcw[:CONV_K]
    plan_attn = rides["attn"](rode, g) if "attn" in rides else None
    (dq, dk, dv), rode_attn = _attn_bwd(s["q"], s["k"], s["v"], s["o_att"], s["lse"], doa, plan_attn)
    dcq, dckv, dkr, dqraw, dkraw, dqag, dkvag, dqng, dkng = _mla_pre_bwd(
        dq, dk, dv, s["cq"], s["ckv"], s["kr"], p["qag"], p["wuq"], p["kvag"], p["wk"], p["wv"], p["qng"],
        p["kng"], *rope)
    g["w_uq"] = _unpad_heads(_wgrad(s["cqn"], dqraw, "uq", 4), 2, QK_DIM, 2)
    dwk = _unpad_heads(_wgrad(s["ckvn"], dkraw, "uk"), NH, 64, 1).reshape(128, NH, 64)
    dwv = _unpad_heads(_wgrad(s["ckvn"], dv, "uv"), NH, 64, 1).reshape(128, NH, 64)
    g["w_ukv"] = jnp.concatenate([dwk, dwv], axis=2).reshape(128, 4, 256).transpose(1, 0, 2)
    g["q_a_norm_g"], g["kv_a_norm_g"] = dqag.sum(0), dkvag.sum(0)
    g["q_norm_g"], g["k_norm_g"] = dqng.sum(0)[:QK_DIM], dkng.sum(0)[:QK_DIM]
    plan_hgrn = rides["hgrn"](rode_attn) if "hgrn" in rides else None
    (dhq, dhf, dhi, dlb), rode_hgrn = _hgrn_bwd(s["h4"], lb, doh, s["states"], plan_hgrn)
    mm = lambda a: a.astype(_MM)
    du = jnp.concatenate([dgt, mm(da), mm(dg), mm(dhq), mm(dhf), mm(dhi), dhg, dcq, dckv, dkr], axis=1)
    dx, dg1 = _in_proj_bwd(du, s["x"], dx1, p["g1"], p["w_in_t"], t_end)
    g["norm1_g"] = dg1.sum(0)
    g["w_in"] = _w_in_grad_to_chips(_wgrad(s["hb"], du, "in"))
    return dx, g, dlb.sum(0), (rode_attn, rode_hgrn)


def _device_step(x, target, small, pieces0, pieces1=None, fwd_ride=None, bwd_rides=None):
    s_real = x.shape[0]
    t_end = ROW0 + s_real
    t = -(-t_end // LANE) * LANE
    zrow = lambda n: jnp.zeros((n, D), F32)
    xp = jnp.concatenate([zrow(FRONT), small["meta"].astype(F32), x, zrow(t - t_end)], axis=0)
    tp = jnp.concatenate([zrow(ROW0), target, zrow(t - t_end)], axis=0)
    rope = _rope_tables(t)
    logits = small["hgrn_lb_logits"].astype(F32)
    lbs = _lower_bounds_fwd(logits)
    prm0 = _prep_early(pieces0, small, 0)
    got = {}
    if fwd_ride is None:
        prm0.update(_prep_late(pieces0))
        h, sv0 = _layer_fwd(xp, prm0, lbs[0:1], rope, t_end)
    else:
        def on_rode(rode):
            late0, got["pieces1"] = fwd_ride[1](rode)
            prm0.update(_prep_late(late0))
        h, sv0 = _layer_fwd(xp, prm0, lbs[0:1], rope, t_end, fwd_ride[0], on_rode)
        pieces1 = got["pieces1"]
    prm1 = _prep_early(pieces1, small, 1)
    if fwd_ride is None:
        prm1.update(_prep_late(pieces1))
        h, sv1 = _layer_fwd(h, prm1, lbs[1:2], rope, t_end)
    else:
        h, sv1 = _layer_fwd(h, prm1, lbs[1:2], rope, t_end, fwd_ride[2],
                            lambda rode: prm1.update(_prep_late(fwd_ride[3](rode))))
    dh, lsum = _loss_head(h, tp, t_end)
    loss = jnp.sum(lsum)
    rides1, make_rides0 = (None, None) if bwd_rides is None else bwd_rides
    dh, g1, dlb1, rode1 = _layer_bwd(dh, sv1, prm1, lbs[1:2], rope, t_end, rides1)
    dh, g0, dlb0, rode = _layer_bwd(dh, sv0, prm0, lbs[0:1], rope, t_end,
                                    None if make_rides0 is None else make_rides0(g1, rode1))
    dlogits = _lower_bounds_bwd(logits, jnp.stack([dlb0, dlb1]))
    grads = [g0, g1]
    for l in range(DEPTH):
        grads[l]["hgrn_lb_logits"] = dlogits[l]
    return loss, dh[ROW0:t_end], grads, dh[FRONT:ROW0], rode


MESH = pl.DeviceIdType.MESH
_ANY = pl.BlockSpec(memory_space=pl.ANY)
SMALL_ROWS = 64
SMALL_LEN = SMALL_ROWS * 1024


def _mesh_pos():
    return lax.axis_index("x"), lax.axis_index("y"), lax.axis_index("c")


def _other_chips(x, y):
    return [(1 - x, y), (x, 1 - y), (1 - x, 1 - y)]


class _Plan:
    def __init__(self, name, ins, out_shapes, sems, start, finish, relay=None):
        self.name, self.ins, self.out_shapes, self.sems = name, list(ins), list(out_shapes), list(sems)
        self.start, self.finish, self.relay = start, finish, relay


def _run_plan(plan):
    ni, no = len(plan.ins), len(plan.out_shapes)

    def body(*refs):
        ins, outs, sems = refs[:ni], refs[ni:ni + no], refs[ni + no:]
        plan.start(ins, outs, sems)
        if plan.relay is not None:
            plan.relay(ins, outs, sems)
        plan.finish(ins, outs, sems)

    return pl.pallas_call(body, name=plan.name, in_specs=[_ANY] * ni, out_specs=[_ANY] * no,
                          out_shape=plan.out_shapes, scratch_shapes=plan.sems)(*plan.ins)


def _plan_specs(plan):
    if plan is None:
        return [], [], [], [], []
    return plan.ins, [_ANY] * len(plan.ins), [_ANY] * len(plan.out_shapes), plan.out_shapes, plan.sems


def _host_refs(refs, n_in, n_out, n_scratch, plan):
    ni = 0 if plan is None else len(plan.ins)
    no = 0 if plan is None else len(plan.out_shapes)
    o0 = n_in + ni
    s0 = o0 + n_out + no
    own = (refs[:n_in], refs[o0:o0 + n_out], refs[s0:s0 + n_scratch])
    rider = (refs[n_in:o0], refs[o0 + n_out:s0], refs[s0 + n_scratch:])
    return own, rider


def _ride(plan, rider, step, last):
    if plan is None:
        return lambda: None

    @pl.when(step == 0)
    def _():
        plan.start(*rider)

    def done():
        if plan.relay is not None:
            @pl.when(step == last - 1)
            def _():
                plan.relay(*rider)

        @pl.when(step == last)
        def _():
            plan.finish(*rider)
    return done


def _merge_plans(name, plans):
    def parts(ins, outs, sems):
        i = o = s = 0
        for p in plans:
            ni, no, ns = len(p.ins), len(p.out_shapes), len(p.sems)
            yield p, (ins[i:i + ni], outs[o:o + no], sems[s:s + ns])
            i, o, s = i + ni, o + no, s + ns

    def start(ins, outs, sems):
        for p, refs in parts(ins, outs, sems):
            p.start(*refs)

    def relay(ins, outs, sems):
        for p, refs in parts(ins, outs, sems):
            if p.relay is not None:
                p.relay(*refs)

    def finish(ins, outs, sems):
        for p, refs in parts(ins, outs, sems):
            p.finish(*refs)

    return _Plan(name, [a for p in plans for a in p.ins], [a for p in plans for a in p.out_shapes],
                 [a for p in plans for a in p.sems], start, finish, relay)


def _plan_gather(own, layer, name):
    nw = len(own)

    def copies(ins, outs, sems):
        send_sems, recv_sems = sems

        def over_ici(w, j, chip_of_data, to):
            return pltpu.make_async_remote_copy(
                src_ref=ins[w].at[layer], dst_ref=outs[w].at[chip_of_data], send_sem=send_sems.at[w, j],
                recv_sem=recv_sems.at[w, j], device_id=to, device_id_type=MESH)

        def over_d2d(w, j, chip_of_data, to):
            return pltpu.make_async_remote_copy(
                src_ref=outs[w].at[chip_of_data], dst_ref=outs[w].at[chip_of_data], send_sem=send_sems.at[w, 3 + j],
                recv_sem=recv_sems.at[w, 3 + j], device_id=to, device_id_type=MESH)
        return over_ici, over_d2d

    def start(ins, outs, sems):
        x, y, c = _mesh_pos()
        over_ici, _ = copies(ins, outs, sems)

        @pl.when(c == layer)
        def _():
            for j, (px, py) in enumerate(_other_chips(x, y)):
                for w in range(nw):
                    over_ici(w, j, 2 * x + y, (px, py, layer)).start()

    def relay(ins, outs, sems):
        x, y, c = _mesh_pos()
        over_ici, over_d2d = copies(ins, outs, sems)

        @pl.when(c == layer)
        def _():
            for j, (px, py) in enumerate(_other_chips(x, y)):
                for w in range(nw):
                    over_ici(w, j, 2 * px + py, (x, y, c)).wait_recv()
                    over_d2d(w, j, 2 * px + py, (x, y, 1 - layer)).start()

    def finish(ins, outs, sems):
        x, y, c = _mesh_pos()
        over_ici, over_d2d = copies(ins, outs, sems)
        chips = _other_chips(x, y)

        @pl.when(c == layer)
        def _():
            for j, (px, py) in enumerate(chips):
                for w in range(nw):
                    over_ici(w, j, 2 * x + y, (px, py, layer)).wait_send()
                    over_d2d(w, j, 2 * px + py, (x, y, 1 - layer)).wait_send()

        @pl.when(c != layer)
        def _():
            for j, (px, py) in enumerate(chips):
                for w in range(nw):
                    over_d2d(w, j, 2 * px + py, (x, y, c)).wait_recv()

    return _Plan(name, own,
                 [jax.ShapeDtypeStruct((4,) + a.shape[1:], a.dtype) for a in own],
                 [pltpu.SemaphoreType.DMA((nw, 6)), pltpu.SemaphoreType.DMA((nw, 6))], start, finish, relay)


def _plan_to_sibling(arrs, layer, name):
    nw = len(arrs)

    def copy(ins, outs, sems, w):
        x, y, _ = _mesh_pos()
        return pltpu.make_async_remote_copy(src_ref=ins[w], dst_ref=outs[w], send_sem=sems[0].at[w],
                                            recv_sem=sems[1].at[w], device_id=(x, y, layer), device_id_type=MESH)

    def start(ins, outs, sems):
        @pl.when(lax.axis_index("c") != layer)
        def _():
            for w in range(nw):
                copy(ins, outs, sems, w).start()

    def finish(ins, outs, sems):
        c = lax.axis_index("c")

        @pl.when(c != layer)
        def _():
            for w in range(nw):
                copy(ins, outs, sems, w).wait_send()

        @pl.when(c == layer)
        def _():
            for w in range(nw):
                copy(ins, outs, sems, w).wait_recv()

    return _Plan(name, arrs, [jax.ShapeDtypeStruct(a.shape, a.dtype) for a in arrs],
                 [pltpu.SemaphoreType.DMA((nw,)), pltpu.SemaphoreType.DMA((nw,))], start, finish)


def _plan_scatter(parts, layer, name):
    nw = len(parts)

    def start(ins, outs, sems):
        x, y, c = _mesh_pos()

        @pl.when(c == layer)
        def _():
            for j, (px, py) in enumerate(_other_chips(x, y)):
                for w in range(nw):
                    pltpu.make_async_remote_copy(
                        src_ref=ins[w].at[2 * px + py], dst_ref=outs[w].at[2 * x + y], send_sem=sems[0].at[w, j],
                        recv_sem=sems[1].at[w, j], device_id=(px, py, layer), device_id_type=MESH).start()

    def finish(ins, outs, sems):
        x, y, c = _mesh_pos()

        @pl.when(c == layer)
        def _():
            for j, (px, py) in enumerate(_other_chips(x, y)):
                for w in range(nw):
                    pltpu.make_async_remote_copy(
                        src_ref=ins[w].at[2 * px + py], dst_ref=outs[w].at[2 * px + py], send_sem=sems[0].at[w, j],
                        recv_sem=sems[1].at[w, j], device_id=(x, y, c), device_id_type=MESH).wait()

    return _Plan(name, parts, [jax.ShapeDtypeStruct(a.shape, a.dtype) for a in parts],
                 [pltpu.SemaphoreType.DMA((nw, 3)), pltpu.SemaphoreType.DMA((nw, 3))], start, finish)


def _sibling_exchange(reds0, reds1):
    nw = len(reds0)

    def body(*refs):
        a0, a1, outs = refs[:nw], refs[nw:2 * nw], refs[2 * nw:3 * nw]
        send_sems, recv_sems = refs[3 * nw:]
        x, y, c = _mesh_pos()

        def copy(w, src):
            return pltpu.make_async_remote_copy(src_ref=src, dst_ref=outs[w], send_sem=send_sems.at[w],
                                                recv_sem=recv_sems.at[w], device_id=(x, y, 1 - c),
                                                device_id_type=MESH)

        @pl.when(c == 0)
        def _():
            for w in range(nw):
                copy(w, a0[w]).start()

        @pl.when(c == 1)
        def _():
            for w in range(nw):
                copy(w, a1[w]).start()

        for w in range(nw):
            copy(w, a0[w]).wait()

    return pl.pallas_call(
        body, name="sibling_exchange", in_specs=[_ANY] * (2 * nw), out_specs=[_ANY] * nw,
        out_shape=[jax.ShapeDtypeStruct(a.shape, a.dtype) for a in reds0],
        scratch_shapes=[pltpu.SemaphoreType.DMA((nw,)), pltpu.SemaphoreType.DMA((nw,))],
    )(*reds0, *reds1)


def _all_reduce_small(v, name):
    rows, cols = v.shape

    def body(v_ref, o_ref, slots, send_sems, recv_sems):
        x, y, c = _mesh_pos()
        me = 4 * x + 2 * y + c
        slots[me] = v_ref[...]
        peers = []
        for rel in range(1, 8):
            fx, fy, fc = (rel >> 2) & 1, (rel >> 1) & 1, rel & 1
            px = 1 - x if fx else x
            py = 1 - y if fy else y
            pc = 1 - c if fc else c
            peers.append((px, py, pc))
        cps = [pltpu.make_async_remote_copy(src_ref=v_ref, dst_ref=slots.at[me], send_sem=send_sems.at[k],
                                            recv_sem=recv_sems.at[k], device_id=peer, device_id_type=MESH)
               for k, peer in enumerate(peers)]
        for cp in cps:
            cp.start()
        for k, (px, py, pc) in enumerate(peers):
            pltpu.make_async_remote_copy(src_ref=v_ref, dst_ref=slots.at[4 * px + 2 * py + pc],
                                         send_sem=send_sems.at[k], recv_sem=recv_sems.at[k], device_id=(x, y, c),
                                         device_id_type=MESH).wait_recv()
        for cp in cps:
            cp.wait_send()
        acc = slots[0]
        for d in range(1, 8):
            acc = acc + slots[d]
        o_ref[...] = acc

    vm = pl.BlockSpec(memory_space=pltpu.VMEM)
    return pl.pallas_call(
        body, name=name, in_specs=[vm], out_specs=vm,
        out_shape=jax.ShapeDtypeStruct((rows, cols), F32),
        scratch_shapes=[pltpu.VMEM((8, rows, cols), F32), pltpu.SemaphoreType.DMA((7,)),
                        pltpu.SemaphoreType.DMA((7,))],
    )(v)


def _add_to_wire(a, b, name):
    n4, r, c_ = a.shape
    rows = n4 * r
    tr = _ew_tile(rows, c_, 3)

    def body(a_ref, b_ref, o_ref):
        o_ref[...] = (a_ref[...] + b_ref[...]).astype(o_ref.dtype)

    spec = pl.BlockSpec((tr, c_), lambda i: (i, 0))
    out = pl.pallas_call(
        body, name="add_to_wire_" + name, grid=(rows // tr,), in_specs=[spec, spec], out_specs=spec,
        out_shape=jax.ShapeDtypeStruct((rows, c_), jnp.bfloat16), compiler_params=_cp("parallel"),
    )(a.reshape(rows, c_), b.reshape(rows, c_))
    return out.reshape(n4, r, c_)


def _sum_chips(recv, own, name):
    _, r, c_ = recv.shape
    tr = _ew_tile(r, c_, 4)

    def body(r_ref, own_ref, o_ref):
        chip = 2 * lax.axis_index("x") + lax.axis_index("y")
        own_v = own_ref[...].astype(F32)
        acc = None
        for s in range(4):
            term = jnp.where(chip == s, own_v, r_ref[s].astype(F32))
            acc = term if acc is None else acc + term
        o_ref[...] = acc

    return pl.pallas_call(
        body, name="sum_chips_" + name, grid=(r // tr,),
        in_specs=[pl.BlockSpec((4, tr, c_), lambda i: (0, i, 0)), pl.BlockSpec((tr, c_), lambda i: (i, 0))],
        out_specs=pl.BlockSpec((tr, c_), lambda i: (i, 0)),
        out_shape=jax.ShapeDtypeStruct((r, c_), F32),
        compiler_params=_cp("parallel"),
    )(recv, own)


def _pack_small(vals, meta_full, conv_w_full):
    flat = jnp.concatenate([vals[k].reshape(-1) for k in SMALL] + [meta_full.reshape(-1), conv_w_full.reshape(-1)])
    return jnp.pad(flat, (0, SMALL_LEN - flat.shape[0])).reshape(SMALL_ROWS, 1024)


def _unpack_small(buf):
    flat = buf.reshape(-1)
    out, off = {}, 0
    for k in SMALL:
        n = DEPTH * SMALL_SIZES[k]
        out[k] = flat[off:off + n].reshape(DEPTH, SMALL_SIZES[k])
        off += n
    meta = flat[off:off + N_META * D].reshape(N_META, D)
    off += N_META * D
    conv_w = flat[off:off + DEPTH * CONV_K * CONV_DIM].reshape(DEPTH, CONV_K, CONV_DIM)
    return out, meta, conv_w


def kernel(x, meta, norm1_g, w_in, conv_w, conv_b, conv_ln_g, conv_ln_b, w_conv_out, q_a_norm_g, w_uq, kv_a_norm_g, w_ukv, q_norm_g, k_norm_g, w_attn_out, hgrn_lb_logits, hgrn_norm_g, w_hgrn_out, w_out, norm2_g, w_ff1, w_ff2, loss_target, m_meta, m_norm1_g, m_w_in, m_conv_w, m_conv_b, m_conv_ln_g, m_conv_ln_b, m_w_conv_out, m_q_a_norm_g, m_w_uq, m_kv_a_norm_g, m_w_ukv, m_q_norm_g, m_k_norm_g, m_w_attn_out, m_hgrn_lb_logits, m_hgrn_norm_g, m_w_hgrn_out, m_w_out, m_norm2_g, m_w_ff1, m_w_ff2, v_meta, v_norm1_g, v_w_in, v_conv_w, v_conv_b, v_conv_ln_g, v_conv_ln_b, v_w_conv_out, v_q_a_norm_g, v_w_uq, v_kv_a_norm_g, v_w_ukv, v_q_norm_g, v_k_norm_g, v_w_attn_out, v_hgrn_lb_logits, v_hgrn_norm_g, v_w_hgrn_out, v_w_out, v_norm2_g, v_w_ff1, v_w_ff2):
    names = ("meta", "norm1_g", "w_in", "conv_w", "conv_b", "conv_ln_g", "conv_ln_b", "w_conv_out", "q_a_norm_g",
             "w_uq", "kv_a_norm_g", "w_ukv", "q_norm_g", "k_norm_g", "w_attn_out", "hgrn_lb_logits", "hgrn_norm_g",
             "w_hgrn_out", "w_out", "norm2_g", "w_ff1", "w_ff2")
    w = dict(zip(names, (meta, norm1_g, w_in, conv_w, conv_b, conv_ln_g, conv_ln_b, w_conv_out, q_a_norm_g, w_uq,
                         kv_a_norm_g, w_ukv, q_norm_g, k_norm_g, w_attn_out, hgrn_lb_logits, hgrn_norm_g, w_hgrn_out,
                         w_out, norm2_g, w_ff1, w_ff2)))
    m = dict(zip(names, (m_meta, m_norm1_g, m_w_in, m_conv_w, m_conv_b, m_conv_ln_g, m_conv_ln_b, m_w_conv_out,
                         m_q_a_norm_g, m_w_uq, m_kv_a_norm_g, m_w_ukv, m_q_norm_g, m_k_norm_g, m_w_attn_out,
                         m_hgrn_lb_logits, m_hgrn_norm_g, m_w_hgrn_out, m_w_out, m_norm2_g, m_w_ff1, m_w_ff2)))
    v = dict(zip(names, (v_meta, v_norm1_g, v_w_in, v_conv_w, v_conv_b, v_conv_ln_g, v_conv_ln_b, v_w_conv_out,
                         v_q_a_norm_g, v_w_uq, v_kv_a_norm_g, v_w_ukv, v_q_norm_g, v_k_norm_g, v_w_attn_out,
                         v_hgrn_lb_logits, v_hgrn_norm_g, v_w_hgrn_out, v_w_out, v_norm2_g, v_w_ff1, v_w_ff2)))
    cx, cy, cc = _mesh_pos()
    chip = 2 * cx + cy
    zero = jnp.zeros((), jnp.int32)

    own = {k: w[k].astype(_MM) for k in BIG}

    def as_pieces(names, gathered, layer):
        return {k: [jnp.where(chip == s, own[k][layer], g[s]) for s in range(4)] for k, g in zip(names, gathered)}

    pieces0 = as_pieces(EARLY, _run_plan(_plan_gather([own[k] for k in EARLY], 0, "gather_l0_early")), 0)
    fwd_ride = (_merge_plans("gather_mid", [_plan_gather([own[k] for k in LATE], 0, "gather_l0_late"),
                                            _plan_gather([own[k] for k in EARLY], 1, "gather_l1_early")]),
                lambda got: (as_pieces(LATE, got[:len(LATE)], 0), as_pieces(EARLY, got[len(LATE):], 1)),
                _plan_gather([own[k] for k in LATE], 1, "gather_l1_late"),
                lambda got: as_pieces(LATE, got, 1))
    meta_slab = lax.dynamic_update_slice(jnp.zeros((N_META, D), F32), meta, (zero, chip * (D // 4)))
    convw_slab = lax.dynamic_update_slice(jnp.zeros((DEPTH, CONV_K, CONV_DIM), F32), conv_w,
                                          (zero, zero, chip * (CONV_DIM // 4)))
    zsmall = {k: jnp.zeros((DEPTH, SMALL_SIZES[k]), F32) for k in SMALL}
    south = (cc == 0).astype(F32)
    _, meta_full, convw_full = _unpack_small(
        _all_reduce_small(_pack_small(zsmall, meta_slab, convw_slab) * south, "gather_small"))
    small = {k: w[k] for k in SMALL}
    small["meta"] = meta_full
    small["conv_w"] = convw_full

    FFN = ("w_ff1", "w_ff2")
    MID = ("w_out", "w_conv_out", "w_attn_out", "w_hgrn_out")
    REST = tuple(k for k in BIG if k not in FFN + MID)
    held = {}

    def to_wire(names, layer, mine, from_sibling):
        return lax.cond(
            cc == layer,
            lambda: [_add_to_wire(a, b, "%s_l%d" % (k, layer)) for k, a, b in zip(names, mine, from_sibling)],
            lambda: [jnp.zeros(a.shape, jnp.bfloat16) for a in mine])

    def chip_sum(names, layer, got, wire):
        return lax.cond(
            cc == layer,
            lambda: [_sum_chips(r, lax.dynamic_index_in_dim(s, chip, 0, keepdims=False), "%s_l%d" % (k, layer))
                     for k, r, s in zip(names, got, wire)],
            lambda: [jnp.zeros(s.shape[1:], F32) for s in wire])

    NONFFN = tuple(k for k in BIG if k not in FFN)

    def ride_attn_l1(_, g1):
        held["g1_ffn"] = [g1[k] for k in FFN]
        return _plan_to_sibling(held["g1_ffn"], 1, "swap_grads_l1_ffn")

    def rides_l0(g1, rode_l1):
        g1_rest = [g1[k] for k in NONFFN]

        def ride_mix(from_sibling1, g0_ffn):
            wire1 = dict(zip(FFN, to_wire(FFN, 1, held["g1_ffn"], rode_l1[0])))
            wire1.update(zip(NONFFN, to_wire(NONFFN, 1, g1_rest, from_sibling1)))
            held["wire1"] = [wire1[k] for k in BIG]
            held["g0_ffn"] = [g0_ffn[k] for k in FFN]
            return _plan_to_sibling(held["g0_ffn"], 0, "swap_grads_l0_ffn")

        def ride_attn(from_sibling0, g0):
            held["wire0_ffn"] = to_wire(FFN, 0, held["g0_ffn"], from_sibling0)
            held["g0_mid"] = [g0[k] for k in MID]
            return _merge_plans("exchange_grads_mid", [
                _plan_scatter(held["wire1"], 1, "scatter_grads_l1"),
                _plan_scatter(held["wire0_ffn"], 0, "scatter_grads_l0_ffn"),
                _plan_to_sibling(held["g0_mid"], 0, "swap_grads_l0_mid")])

        def ride_hgrn(rode_attn):
            held["wire0_mid"] = to_wire(MID, 0, held["g0_mid"], rode_attn[len(BIG) + len(FFN):])
            return _plan_scatter(held["wire0_mid"], 0, "scatter_grads_l0_mid")

        return {"ffn": _plan_to_sibling(g1_rest, 1, "swap_grads_l1_rest"), "mix": ride_mix, "attn": ride_attn,
                "hgrn": ride_hgrn}

    loss_share, grad_x, gl, g_meta, (got, got_mid) = _device_step(
        x[0], loss_target[0], small, pieces0, None, fwd_ride, ({"attn": ride_attn_l1}, rides_l0))

    reds1 = chip_sum(BIG, 1, got[:len(BIG)], held["wire1"])
    reds0 = dict(zip(FFN, chip_sum(FFN, 0, got[len(BIG):len(BIG) + len(FFN)], held["wire0_ffn"])))
    reds0.update(zip(MID, chip_sum(MID, 0, got_mid, held["wire0_mid"])))
    g0_rest = [gl[0][k] for k in REST]
    wire0 = to_wire(REST, 0, g0_rest, _run_plan(_plan_to_sibling(g0_rest, 0, "swap_grads_l0_rest")))
    reds0.update(zip(REST, chip_sum(REST, 0, _run_plan(_plan_scatter(wire0, 0, "scatter_grads_l0_rest")), wire0)))
    reds0 = [reds0[k] for k in BIG]
    reds_sibling = _sibling_exchange(reds0, reds1)
    grads, delta, new_m, new_v = {}, {}, {}, {}
    t_view = lambda k, a: jnp.swapaxes(a, -1, -2) if k == "w_in" else a
    for k, r0, r1, theirs in zip(BIG, reds0, reds1, reds_sibling):
        grads[k], delta[k], new_m[k], new_v[k] = [
            t_view(k, a) for a in _adamw_layers(t_view(k, w[k]), t_view(k, m[k]), t_view(k, v[k]), t_view(k, r0),
                                                t_view(k, r1), t_view(k, theirs), k)]

    g_small_local = {k: jnp.stack([gl[l][k] for l in range(DEPTH)]) for k in SMALL}
    g_convw_local = jnp.stack([gl[l]["conv_w"] for l in range(DEPTH)])
    reduced = _all_reduce_small(
        _pack_small(g_small_local, g_meta, g_convw_local).at[SMALL_ROWS - 1, 1023].set(loss_share), "reduce_small")
    loss = reduced[SMALL_ROWS - 1, 1023]
    g_small, g_meta_full, g_convw_full = _unpack_small(reduced)
    grads.update(g_small)
    grads["meta"] = lax.dynamic_slice(g_meta_full, (zero, chip * (D // 4)), (N_META, D // 4))
    grads["conv_w"] = lax.dynamic_slice(g_convw_full, (zero, zero, chip * (CONV_DIM // 4)),
                                        (DEPTH, CONV_K, CONV_DIM // 4))

    def small_pack(src):
        return _pack_small(src, jnp.pad(src["meta"], ((0, 0), (0, D - D // 4))),
                           jnp.pad(src["conv_w"], ((0, 0), (0, 0), (0, CONV_DIM - CONV_DIM // 4))))

    def small_unpack(buf):
        out, meta_p, convw_p = _unpack_small(buf)
        out["meta"] = meta_p[:, :D // 4]
        out["conv_w"] = convw_p[:, :, :CONV_DIM // 4]
        return out

    d_s, m_s, v_s = [small_unpack(a) for a in _adamw(small_pack(w), small_pack(grads), small_pack(m),
                                                     small_pack(v), "small")]
    delta.update(d_s)
    new_m.update(m_s)
    new_v.update(v_s)
    return (loss, grad_x[None], *[grads[k] for k in names], *[delta[k] for k in names],
            *[new_m[k] for k in names], *[new_v[k] for k in names])
```

```python
import functools

import jax
import jax.numpy as jnp
from jax import lax
from jax.experimental import pallas as pl
from jax.experimental.pallas import tpu as pltpu

F32 = jnp.float32
_MM = jnp.bfloat16

D = 1024
N_META = 16
FRONT = 48
ROW0 = FRONT + N_META
EPS = 1e-6
GATE_CLAMP = 1.0 - 1e-6
CONV_K = 31
CONV_DIM = 512
NH = 8
QK_DIM = 96
ATT_SCALE = QK_DIM ** -0.5
HH = 4
CHUNK = 64
SUB = 16
EXP_CLIP = 60.0
NEG = -1e30
LANE = 128

SEG_GATES = (0, 3072)
SEG_AG = (3072, 4096)
SEG_H4 = (4096, 6144)
SEG_CQ = (6144, 6400)
SEG_CKV = (6400, 6528)
SEG_KR = (6528, 6656)
N_IN_P = 6656

ADAM_LR = 0.001
ADAM_B1 = 0.9
ADAM_B2 = 0.999
ADAM_EPS = 1e-08
ADAM_WD = 0.01
ADAM_STEP = 10

VMEM_LIMIT = 56 * 1024 * 1024


def _tile(n, pref):
    best = 64
    for t in range(64, pref + 1, 64):
        if n % t == 0:
            best = t
    return best


def _cp(*sem):
    return pltpu.CompilerParams(dimension_semantics=tuple(sem), vmem_limit_bytes=VMEM_LIMIT)


def _row(tm, n, col=0):
    return pl.BlockSpec((tm, n), lambda i: (i, col))


def _full(shape):
    return pl.BlockSpec(shape, lambda i: (0,) * len(shape))


def _mm(a, b):
    return jnp.dot(a.astype(_MM), b.astype(_MM), preferred_element_type=F32)


def _mm_nt(a, b):
    return lax.dot_general(a.astype(_MM), b.astype(_MM), (((1,), (1,)), ((), ())), preferred_element_type=F32)


def _mm_tn(a, b):
    return lax.dot_general(a.astype(_MM), b.astype(_MM), (((0,), (0,)), ((), ())), preferred_element_type=F32)


def _split3(x):
    hi = x.astype(jnp.bfloat16)
    return hi, (x - hi.astype(F32)).astype(jnp.bfloat16)


def _dot3(a, b, dims):
    ah, al = _split3(a)
    bh, bl = _split3(b)
    dg = lambda u, v: lax.dot_general(u, v, (dims, ((), ())), preferred_element_type=F32)
    return dg(ah, bh) + (dg(ah, bl) + dg(al, bh))


def _hmm(a, b):
    return _dot3(a, b, ((1,), (0,)))


def _hmm_nt(a, b):
    return _dot3(a, b, ((1,), (1,)))


def _hmm_tn(a, b):
    return _dot3(a, b, ((0,), (0,)))


def _sigmoid(x):
    return 1.0 / (1.0 + jnp.exp(-x))


def _rstd(x, n=None):
    n = x.shape[-1] if n is None else n
    return lax.rsqrt(jnp.sum(x * x, axis=-1, keepdims=True) * (1.0 / n) + EPS)


def _rms_bwd(dy, x, rstd, g, n=None):
    n = x.shape[-1] if n is None else n
    xh = x * rstd
    dxh = dy * g
    dx = rstd * (dxh - xh * (jnp.sum(dxh * xh, axis=-1, keepdims=True) * (1.0 / n)))
    return dx, dy * xh


def _valid_rows(i, tm, t_valid_end):
    r = i * tm + lax.broadcasted_iota(jnp.int32, (tm, 1), 0)
    return ((r >= FRONT) & (r < t_valid_end)).astype(F32)


def _colsum8(x):
    n, c = x.shape
    return jnp.sum(x.reshape(n // 8, 8, c), axis=0)


def _in_proj_fwd(x, g1, w):
    t = x.shape[0]
    tm = _tile(t, 192)
    segs = (SEG_GATES, SEG_AG, SEG_H4, SEG_CQ, SEG_CKV, SEG_KR)

    def body(x_ref, g_ref, w_ref, gates_ref, ag_ref, h4_ref, cq_ref, ckv_ref, kr_ref, hb_ref):
        xv = x_ref[...]
        hb = (xv * _rstd(xv) * g_ref[...]).astype(_MM)
        hb_ref[...] = hb
        for ref, (a, b) in zip((gates_ref, ag_ref, h4_ref, cq_ref, ckv_ref, kr_ref), segs):
            ref[...] = jnp.dot(hb, w_ref[:, a:b], preferred_element_type=F32)

    outs = [jax.ShapeDtypeStruct((t, b - a), F32) for a, b in segs] + [jax.ShapeDtypeStruct((t, D), _MM)]
    return pl.pallas_call(
        body, name="in_proj_fwd", grid=(t // tm,),
        in_specs=[_row(tm, D), _full((1, D)), _full((D, N_IN_P))],
        out_specs=[_row(tm, b - a) for a, b in segs] + [_row(tm, D)],
        out_shape=outs, compiler_params=_cp("parallel"),
    )(x, g1, w)


def _in_proj_bwd(du, x, dx1, g1, wt, t_end):
    t = x.shape[0]
    tm = _tile(t, 192)

    def body(du_ref, x_ref, dx1_ref, g_ref, wt_ref, dx_ref, dg_ref):
        i = pl.program_id(0)
        dh = jnp.dot(du_ref[...], wt_ref[...], preferred_element_type=F32)
        xv = x_ref[...]
        dxn, dgrow = _rms_bwd(dh, xv, _rstd(xv), g_ref[...])
        dx_ref[...] = _valid_rows(i, tm, t_end) * (dx1_ref[...] + dxn)

        @pl.when(i == 0)
        def _():
            dg_ref[...] = jnp.zeros_like(dg_ref)
        dg_ref[...] += _colsum8(dgrow)

    return pl.pallas_call(
        body, name="in_proj_bwd", grid=(t // tm,),
        in_specs=[_row(tm, N_IN_P), _row(tm, D), _row(tm, D), _full((1, D)), _full((N_IN_P, D))],
        out_specs=[_row(tm, D), _full((8, D))],
        out_shape=[jax.ShapeDtypeStruct((t, D), F32), jax.ShapeDtypeStruct((8, D), F32)],
        compiler_params=_cp("arbitrary"),
    )(du, x, dx1, g1, wt)


CONV_CH = 128


def _conv_fwd(ag, cw, cb):
    t = ag.shape[0]
    n = t // CONV_CH

    def body(a_ref, g_ref, w_ref, b_ref, z_ref, hp):
        hp[0:32, :] = jnp.zeros((32, LANE), F32)

        def fill(i, c):
            r = pl.multiple_of(i * CONV_CH, CONV_CH)
            hp[pl.ds(32 + r, CONV_CH), :] = a_ref[pl.ds(r, CONV_CH), :] * _sigmoid(g_ref[pl.ds(r, CONV_CH), :])
            return c
        lax.fori_loop(0, n, fill, 0)

        def conv(i, c):
            r = pl.multiple_of(i * CONV_CH, CONV_CH)
            acc = jnp.broadcast_to(b_ref[...], (CONV_CH, LANE))
            for k in range(CONV_K):
                acc = acc + w_ref[k:k + 1, :] * hp[pl.ds(r + (k + 2), CONV_CH), :]
            z_ref[pl.ds(r, CONV_CH), :] = acc
            return c
        lax.fori_loop(0, n, conv, 0)

    nb = CONV_DIM // LANE
    return pl.pallas_call(
        body, name="conv_fwd", grid=(nb,),
        in_specs=[pl.BlockSpec((t, LANE), lambda j: (0, j)), pl.BlockSpec((t, LANE), lambda j: (0, nb + j)),
                  pl.BlockSpec((32, LANE), lambda j: (0, j)), pl.BlockSpec((1, LANE), lambda j: (0, j))],
        out_specs=pl.BlockSpec((t, LANE), lambda j: (0, j)),
        out_shape=jax.ShapeDtypeStruct((t, CONV_DIM), F32),
        scratch_shapes=[pltpu.VMEM((t + 32, LANE), F32)],
        compiler_params=_cp("parallel"),
    )(ag, ag, cw, cb)


def _conv_bwd(ag, cw, dz):
    t = ag.shape[0]
    n = t // CONV_CH

    def body(a_ref, g_ref, w_ref, dz_ref, da_ref, dg_ref, dcw_ref, hp, dzp, accw):
        hp[0:32, :] = jnp.zeros((32, LANE), F32)
        dzp[pl.ds(t, 32), :] = jnp.zeros((32, LANE), F32)
        accw[...] = jnp.zeros_like(accw)

        def fill(i, c):
            r = pl.multiple_of(i * CONV_CH, CONV_CH)
            hp[pl.ds(32 + r, CONV_CH), :] = a_ref[pl.ds(r, CONV_CH), :] * _sigmoid(g_ref[pl.ds(r, CONV_CH), :])
            dzp[pl.ds(r, CONV_CH), :] = dz_ref[pl.ds(r, CONV_CH), :]
            return c
        lax.fori_loop(0, n, fill, 0)

        def step(i, c):
            r = pl.multiple_of(i * CONV_CH, CONV_CH)
            dzc = dz_ref[pl.ds(r, CONV_CH), :]
            dh = jnp.zeros((CONV_CH, LANE), F32)
            for k in range(CONV_K):
                dh = dh + w_ref[k:k + 1, :] * dzp[pl.ds(r + (CONV_K - 1 - k), CONV_CH), :]
                accw[8 * k:8 * k + 8, :] += _colsum8(dzc * hp[pl.ds(r + (k + 2), CONV_CH), :])
            a = a_ref[pl.ds(r, CONV_CH), :]
            sg = _sigmoid(g_ref[pl.ds(r, CONV_CH), :])
            da_ref[pl.ds(r, CONV_CH), :] = dh * sg
            dg_ref[pl.ds(r, CONV_CH), :] = dh * a * sg * (1.0 - sg)
            return c
        lax.fori_loop(0, n, step, 0)

        for k in range(CONV_K):
            dcw_ref[k:k + 1, :] = jnp.sum(accw[8 * k:8 * k + 8, :], axis=0, keepdims=True)
        dcw_ref[CONV_K:32, :] = jnp.zeros((32 - CONV_K, LANE), F32)

    nb = CONV_DIM // LANE
    colspec = pl.BlockSpec((t, LANE), lambda j: (0, j))
    return pl.pallas_call(
        body, name="conv_bwd", grid=(nb,),
        in_specs=[colspec, pl.BlockSpec((t, LANE), lambda j: (0, nb + j)),
                  pl.BlockSpec((32, LANE), lambda j: (0, j)), colspec],
        out_specs=[colspec, colspec, pl.BlockSpec((32, LANE), lambda j: (0, j))],
        out_shape=[jax.ShapeDtypeStruct((t, CONV_DIM), F32), jax.ShapeDtypeStruct((t, CONV_DIM), F32),
                   jax.ShapeDtypeStruct((32, CONV_DIM), F32)],
        scratch_shapes=[pltpu.VMEM((t + 32, LANE), F32), pltpu.VMEM((t + 32, LANE), F32),
                        pltpu.VMEM((8 * 32, LANE), F32)],
        compiler_params=_cp("parallel"),
    )(ag, ag, cw, dz)


def _rope(x, c, s1, s2):
    return x * c + pltpu.roll(x, LANE - 16, 1) * s1 + pltpu.roll(x, 16, 1) * s2


def _rope_t(dy, c, s1, s2):
    return dy * c + pltpu.roll(dy * s1, 16, 1) + pltpu.roll(dy * s2, LANE - 16, 1)


def _mla_pre_fwd(cq, ckv, kr, qag, wuq, kvag, wk, wv, qng, kng, rc, rs1, rs2):
    t = cq.shape[0]
    tm = _tile(t, 384)

    def body(cq_ref, ckv_ref, kr_ref, qag_ref, wuq_ref, kvag_ref, wk_ref, wv_ref, qng_ref, kng_ref,
             c_ref, s1_ref, s2_ref, q_ref, k_ref, v_ref, cqn_ref, ckvn_ref):
        cqv = cq_ref[...]
        cqn = (cqv * _rstd(cqv) * qag_ref[...]).astype(_MM)
        cqn_ref[...] = cqn
        ckvv = ckv_ref[...]
        ckvn = (ckvv * _rstd(ckvv) * kvag_ref[...]).astype(_MM)
        ckvn_ref[...] = ckvn
        qraw = jnp.dot(cqn, wuq_ref[...], preferred_element_type=F32)
        kraw = jnp.dot(ckvn, wk_ref[...], preferred_element_type=F32)
        v_ref[...] = jnp.dot(ckvn, wv_ref[...], preferred_element_type=F32).astype(_MM)
        krv = kr_ref[...]
        c, s1, s2 = c_ref[...], s1_ref[...], s2_ref[...]
        for h in range(NH):
            sl = slice(LANE * h, LANE * (h + 1))
            qh = qraw[:, sl]
            qn = qh * _rstd(qh, QK_DIM) * qng_ref[...]
            q_ref[:, sl] = (_rope(qn, c, s1, s2) * ATT_SCALE).astype(_MM)
            kh = kraw[:, sl] + krv
            kn = kh * _rstd(kh, QK_DIM) * kng_ref[...]
            k_ref[:, sl] = _rope(kn, c, s1, s2).astype(_MM)

    hd = NH * LANE
    return pl.pallas_call(
        body, name="mla_pre_fwd", grid=(t // tm,),
        in_specs=[_row(tm, 256), _row(tm, 128), _row(tm, 128), _full((1, 256)), _full((256, hd)),
                  _full((1, 128)), _full((128, hd)), _full((128, hd)), _full((1, LANE)), _full((1, LANE)),
                  _row(tm, LANE), _row(tm, LANE), _row(tm, LANE)],
        out_specs=[_row(tm, hd), _row(tm, hd), _row(tm, hd), _row(tm, 256), _row(tm, 128)],
        out_shape=[jax.ShapeDtypeStruct((t, hd), _MM)] * 3 + [jax.ShapeDtypeStruct((t, 256), _MM),
                                                              jax.ShapeDtypeStruct((t, 128), _MM)],
        compiler_params=_cp("parallel"),
    )(cq, ckv, kr, qag, wuq, kvag, wk, wv, qng, kng, rc, rs1, rs2)


def _mla_pre_bwd(dq, dk, dv, cq, ckv, kr, qag, wuq, kvag, wk, wv, qng, kng, rc, rs1, rs2):
    t = cq.shape[0]
    tm = _tile(t, 192)
    hd = NH * LANE

    def body(dq_ref, dk_ref, dv_ref, cq_ref, ckv_ref, kr_ref, qag_ref, wuq_ref, kvag_ref, wk_ref,
             wv_ref, qng_ref, kng_ref, c_ref, s1_ref, s2_ref,
             dcq_ref, dckv_ref, dkr_ref, dqraw_ref, dkraw_ref, dqag_ref, dkvag_ref, dqng_ref, dkng_ref):
        i = pl.program_id(0)
        cqv = cq_ref[...]
        rq_in = _rstd(cqv)
        cqn = (cqv * rq_in * qag_ref[...]).astype(_MM)
        ckvv = ckv_ref[...]
        rkv_in = _rstd(ckvv)
        ckvn = (ckvv * rkv_in * kvag_ref[...]).astype(_MM)
        qraw = jnp.dot(cqn, wuq_ref[...], preferred_element_type=F32)
        kraw = jnp.dot(ckvn, wk_ref[...], preferred_element_type=F32)
        krv = kr_ref[...]
        c, s1, s2 = c_ref[...], s1_ref[...], s2_ref[...]
        dkr = jnp.zeros((tm, LANE), F32)
        dqng = jnp.zeros((8, LANE), F32)
        dkng = jnp.zeros((8, LANE), F32)
        for h in range(NH):
            sl = slice(LANE * h, LANE * (h + 1))
            qh = qraw[:, sl]
            dqn = _rope_t(dq_ref[:, sl] * ATT_SCALE, c, s1, s2)
            dqh, gq = _rms_bwd(dqn, qh, _rstd(qh, QK_DIM), qng_ref[...], QK_DIM)
            dqraw_ref[:, sl] = dqh.astype(_MM)
            dqng = dqng + _colsum8(gq)
            kh = kraw[:, sl] + krv
            dkn = _rope_t(dk_ref[:, sl], c, s1, s2)
            dkh, gk = _rms_bwd(dkn, kh, _rstd(kh, QK_DIM), kng_ref[...], QK_DIM)
            dkraw_ref[:, sl] = dkh.astype(_MM)
            dkr = dkr + dkh
            dkng = dkng + _colsum8(gk)
        dkr_ref[...] = dkr.astype(_MM)
        dcqn = _mm_nt(dqraw_ref[...], wuq_ref[...])
        dcq, gqa = _rms_bwd(dcqn, cqv, rq_in, qag_ref[...])
        dcq_ref[...] = dcq.astype(_MM)
        dckvn = _mm_nt(dkraw_ref[...], wk_ref[...]) + _mm_nt(dv_ref[...], wv_ref[...])
        dckv, gkva = _rms_bwd(dckvn, ckvv, rkv_in, kvag_ref[...])
        dckv_ref[...] = dckv.astype(_MM)

        @pl.when(i == 0)
        def _():
            dqag_ref[...] = jnp.zeros_like(dqag_ref)
            dkvag_ref[...] = jnp.zeros_like(dkvag_ref)
            dqng_ref[...] = jnp.zeros_like(dqng_ref)
            dkng_ref[...] = jnp.zeros_like(dkng_ref)
        dqag_ref[...] += _colsum8(gqa)
        dkvag_ref[...] += _colsum8(gkva)
        dqng_ref[...] += dqng
        dkng_ref[...] += dkng

    return pl.pallas_call(
        body, name="mla_pre_bwd", grid=(t // tm,),
        in_specs=[_row(tm, hd), _row(tm, hd), _row(tm, hd), _row(tm, 256), _row(tm, 128), _row(tm, 128),
                  _full((1, 256)), _full((256, hd)), _full((1, 128)), _full((128, hd)),
                  _full((128, hd)), _full((1, LANE)), _full((1, LANE)),
                  _row(tm, LANE), _row(tm, LANE), _row(tm, LANE)],
        out_specs=[_row(tm, 256), _row(tm, 128), _row(tm, 128), _row(tm, hd), _row(tm, hd),
                   _full((8, 256)), _full((8, 128)), _full((8, LANE)), _full((8, LANE))],
        out_shape=[jax.ShapeDtypeStruct((t, 256), _MM), jax.ShapeDtypeStruct((t, 128), _MM),
                   jax.ShapeDtypeStruct((t, 128), _MM), jax.ShapeDtypeStruct((t, hd), _MM),
                   jax.ShapeDtypeStruct((t, hd), _MM), jax.ShapeDtypeStruct((8, 256), F32),
                   jax.ShapeDtypeStruct((8, 128), F32), jax.ShapeDtypeStruct((8, LANE), F32),
                   jax.ShapeDtypeStruct((8, LANE), F32)],
        compiler_params=_cp("arbitrary"),
    )(dq, dk, dv, cq, ckv, kr, qag, wuq, kvag, wk, wv, qng, kng, rc, rs1, rs2)


ATT_TILE = 704


def _attn_mask(r0, c0, tq):
    rows = r0 + lax.broadcasted_iota(jnp.int32, (tq, 1), 0)
    cols = c0 + lax.broadcasted_iota(jnp.int32, (1, tq), 1)
    return (cols <= rows) & (cols >= FRONT)


def _attn_fwd(q, k, v, plan=None):
    t = q.shape[0]
    tq = _tile(t, ATT_TILE)
    nq = t // tq
    p_args, p_in, p_out, p_shape, p_sem = _plan_specs(plan)

    def body(*refs):
        ((q_ref, k_ref, v_ref), (o_ref, lse_ref), _), rider = _host_refs(refs, 3, 2, 0, plan)
        done = _ride(plan, rider, pl.program_id(0), NH - 1)

        def qloop(qi, carry):
            r0 = pl.multiple_of(qi * tq, tq)
            qb = q_ref[pl.ds(r0, tq), :]

            def kstep(kj, st, masked):
                m, l, acc = st
                c0 = pl.multiple_of(kj * tq, tq)
                s = _mm_nt(qb, k_ref[pl.ds(c0, tq), :])
                if masked:
                    s = jnp.where(_attn_mask(r0, c0, tq), s, NEG)
                m2 = jnp.maximum(m, jnp.max(s, axis=-1, keepdims=True))
                p = jnp.exp(s - m2)
                a = jnp.exp(m - m2)
                l = a * l + jnp.sum(p, axis=-1, keepdims=True)
                acc = a * acc + _mm(p, v_ref[pl.ds(c0, tq), :])
                return m2, l, acc

            st = kstep(0, (jnp.full((tq, 1), NEG, F32), jnp.zeros((tq, 1), F32), jnp.zeros((tq, LANE), F32)), True)
            st = lax.fori_loop(1, qi, lambda kj, s_: kstep(kj, s_, False), st)
            m, l, acc = lax.cond(qi > 0, lambda s_: kstep(qi, s_, True), lambda s_: s_, st)
            o_ref[pl.ds(r0, tq), :] = acc / l
            lse_ref[pl.ds(r0, tq), :] = m + jnp.log(l)
            return carry
        lax.fori_loop(0, nq, qloop, 0)
        done()

    hs = pl.BlockSpec((t, LANE), lambda h: (0, h))
    res = pl.pallas_call(
        body, name="attn_fwd", grid=(NH,),
        in_specs=[hs, hs, hs] + p_in,
        out_specs=[hs, pl.BlockSpec((None, t, 1), lambda h: (h, 0, 0))] + p_out,
        out_shape=[jax.ShapeDtypeStruct((t, NH * LANE), F32), jax.ShapeDtypeStruct((NH, t, 1), F32)] + p_shape,
        scratch_shapes=p_sem,
        compiler_params=_cp("parallel" if plan is None else "arbitrary"),
    )(q, k, v, *p_args)
    return res[:2], res[2:]


def _attn_bwd(q, k, v, o, lse, do, plan=None):
    t = q.shape[0]
    tq = _tile(t, ATT_TILE)
    nq = t // tq
    p_args, p_in, p_out, p_shape, p_sem = _plan_specs(plan)

    def body(*refs):
        (ins, (dq_ref, dk_ref, dv_ref), (delta,)), rider = _host_refs(refs, 6, 3, 1, plan)
        q_ref, k_ref, v_ref, o_ref, lse_ref, do_ref = ins
        done = _ride(plan, rider, pl.program_id(0), NH - 1)

        def prep(i, c):
            r0 = pl.multiple_of(i * tq, tq)
            delta[pl.ds(r0, tq), :] = jnp.sum(do_ref[pl.ds(r0, tq), :] * o_ref[pl.ds(r0, tq), :], axis=-1,
                                              keepdims=True)
            dq_ref[pl.ds(r0, tq), :] = jnp.zeros((tq, LANE), F32)
            return c
        lax.fori_loop(0, nq, prep, 0)

        def kloop(kj, carry):
            c0 = pl.multiple_of(kj * tq, tq)
            kb = k_ref[pl.ds(c0, tq), :]
            vb = v_ref[pl.ds(c0, tq), :]

            def qstep(qi, st, masked):
                dkb, dvb = st
                r0 = pl.multiple_of(qi * tq, tq)
                qb = q_ref[pl.ds(r0, tq), :]
                dob = do_ref[pl.ds(r0, tq), :].astype(_MM)
                s = _mm_nt(qb, kb)
                if masked:
                    s = jnp.where(_attn_mask(r0, c0, tq), s, NEG)
                p = jnp.exp(s - lse_ref[pl.ds(r0, tq), :])
                dvb = dvb + _mm_tn(p, dob)
                dp = _mm_nt(dob, vb)
                ds = (p * (dp - delta[pl.ds(r0, tq), :])).astype(_MM)
                dkb = dkb + _mm_tn(ds, qb)
                dq_ref[pl.ds(r0, tq), :] += _mm(ds, kb)
                return dkb, dvb

            st = qstep(kj, (jnp.zeros((tq, LANE), F32), jnp.zeros((tq, LANE), F32)), True)
            dkb, dvb = lax.cond(
                kj == 0,
                lambda s_: lax.fori_loop(kj + 1, nq, lambda qi, t_: qstep(qi, t_, True), s_),
                lambda s_: lax.fori_loop(kj + 1, nq, lambda qi, t_: qstep(qi, t_, False), s_), st)
            dk_ref[pl.ds(c0, tq), :] = dkb
            dv_ref[pl.ds(c0, tq), :] = dvb
            return carry
        lax.fori_loop(0, nq, kloop, 0)
        done()

    hs = pl.BlockSpec((t, LANE), lambda h: (0, h))
    res = pl.pallas_call(
        body, name="attn_bwd", grid=(NH,),
        in_specs=[hs, hs, hs, hs, pl.BlockSpec((None, t, 1), lambda h: (h, 0, 0)), hs] + p_in,
        out_specs=[hs, hs, hs] + p_out,
        out_shape=[jax.ShapeDtypeStruct((t, NH * LANE), F32)] * 3 + p_shape,
        scratch_shapes=[pltpu.VMEM((t, 1), F32)] + p_sem,
        compiler_params=_cp("parallel" if plan is None else "arbitrary"),
    )(q, k, v, o, lse, do, *p_args)
    return res[:3], res[3:]


def _cumsum_rows(x):
    n = x.shape[0]
    rows = lax.broadcasted_iota(jnp.int32, (n, 1), 0)
    d = 1
    while d < n:
        x = x + jnp.where(rows >= d, pltpu.roll(x, d, 0), 0.0)
        d *= 2
    return x


def _revcumsum_rows(x):
    n = x.shape[0]
    rows = lax.broadcasted_iota(jnp.int32, (n, 1), 0)
    d = 1
    while d < n:
        x = x + jnp.where(rows < n - d, pltpu.roll(x, n - d, 0), 0.0)
        d *= 2
    return x


def _hgrn_gates(f, lb):
    sneg = _sigmoid(-f)
    kk = (1.0 - lb) * sneg
    lf = jnp.log1p(-jnp.minimum(kk, GATE_CLAMP))
    return kk, lf, sneg


def _silu(x):
    return x * _sigmoid(x)


def _dsilu(x):
    s = _sigmoid(x)
    return s * (1.0 + x * (1.0 - s))


def _hgrn_intra(q, kk, b):
    parts = []
    for blk in range(CHUNK // SUB):
        lo = blk * SUB
        ref = jnp.zeros((1, LANE), F32) if blk == 0 else b[lo - 1:lo, :]
        eq = jnp.exp(b[lo:lo + SUB, :] - ref)
        ek = jnp.exp(jnp.minimum(ref - b, EXP_CLIP))
        parts.append((q[lo:lo + SUB, :] * eq, kk * ek, eq, ek))
    return parts


def _chunk_causal():
    return lax.broadcasted_iota(jnp.int32, (CHUNK, CHUNK), 1) <= lax.broadcasted_iota(jnp.int32, (CHUNK, CHUNK), 0)


HGRN_PAIR = 2


def _hgrn_fwd(h4, lb):
    t = h4.shape[0]
    nc = t // CHUNK

    def body(q_ref, f_ref, i_ref, lb_ref, o_ref, s_ref, st):
        st[...] = jnp.zeros_like(st)
        causal = _chunk_causal()

        def chunk(c, carry):
            r0 = pl.multiple_of(c * CHUNK, CHUNK)
            rows = pl.ds(r0, CHUNK)
            for hh in range(HGRN_PAIR):
                sl = slice(LANE * hh, LANE * (hh + 1))
                q = q_ref[rows, sl]
                kk, lf, _ = _hgrn_gates(f_ref[rows, sl], lb_ref[:, sl])
                v = _silu(i_ref[rows, sl])
                b = _cumsum_rows(lf)
                s_prev = st[hh]
                s_ref[hh, c] = s_prev
                o = _hmm_nt(q * jnp.exp(b), s_prev)
                a = jnp.concatenate([_hmm_nt(qs, ks) for qs, ks, _, _ in _hgrn_intra(q, kk, b)], axis=0)
                a = jnp.where(causal, a, 0.0)
                o_ref[rows, sl] = o + _hmm(a, v)
                bl = b[CHUNK - 1:CHUNK, :]
                st[hh] = s_prev * jnp.exp(bl) + _hmm_tn(v, kk * jnp.exp(bl - b))
            return carry
        lax.fori_loop(0, nc, chunk, 0, unroll=2)

    pw = HGRN_PAIR * LANE

    def col(j):
        return pl.BlockSpec((t, pw), lambda g: (0, (HH // HGRN_PAIR) * j + g))
    return pl.pallas_call(
        body, name="hgrn_fwd", grid=(HH // HGRN_PAIR,),
        in_specs=[col(0), col(1), col(2), pl.BlockSpec((1, pw), lambda g: (0, g))],
        out_specs=[pl.BlockSpec((t, pw), lambda g: (0, g)),
                   pl.BlockSpec((HGRN_PAIR, nc, LANE, LANE), lambda g: (g, 0, 0, 0))],
        out_shape=[jax.ShapeDtypeStruct((t, HH * LANE), F32), jax.ShapeDtypeStruct((HH, nc, LANE, LANE), F32)],
        scratch_shapes=[pltpu.VMEM((HGRN_PAIR, LANE, LANE), F32)],
        compiler_params=_cp("parallel"),
    )(h4, h4, h4, lb)


def _hgrn_bwd(h4, lb, do, states, plan=None):
    t = h4.shape[0]
    nc = t // CHUNK
    p_args, p_in, p_out, p_shape, p_sem = _plan_specs(plan)

    def body(*refs):
        (ins, outs, (dst, carry)), rider = _host_refs(refs, 6, 4, 2, plan)
        q_ref, f_ref, i_ref, lb_ref, do_ref, s_ref = ins
        dq_ref, df_ref, di_ref, dlb_ref = outs
        done = _ride(plan, rider, pl.program_id(0), HH - 1)
        dst[...] = jnp.zeros_like(dst)
        carry[...] = jnp.zeros_like(carry)
        dlb_ref[...] = jnp.zeros_like(dlb_ref)
        causal = _chunk_causal()

        def chunk(cc, cr):
            c = nc - 1 - cc
            r0 = pl.multiple_of(c * CHUNK, CHUNK)
            q = q_ref[pl.ds(r0, CHUNK), :]
            lbv = lb_ref[...]
            kk, lf, sneg = _hgrn_gates(f_ref[pl.ds(r0, CHUNK), :], lbv)
            iv = i_ref[pl.ds(r0, CHUNK), :]
            v = _silu(iv)
            b = _cumsum_rows(lf)
            s_prev = s_ref[c]
            ds_new = dst[...]
            dob = do_ref[pl.ds(r0, CHUNK), :]
            e = jnp.exp(b)
            qe = q * e
            bl = b[CHUNK - 1:CHUNK, :]
            etail = jnp.exp(bl - b)
            kd = kk * etail
            dq_inter = _hmm(dob, s_prev) * e
            dv = _hmm_nt(kd, ds_new)
            dkk = _hmm(v, ds_new) * etail
            parts = _hgrn_intra(q, kk, b)
            a = jnp.where(causal, jnp.concatenate([_hmm_nt(qs, ks) for qs, ks, _, _ in parts], axis=0), 0.0)
            da = jnp.where(causal, _hmm_nt(dob, v), 0.0)
            dv = dv + _hmm_tn(a, dob)
            dq_rows = []
            for blk, (qs, ks, eq, ek) in enumerate(parts):
                da_blk = da[blk * SUB:(blk + 1) * SUB, :]
                dq_rows.append(_hmm(da_blk, ks) * eq)
                dkk = dkk + _hmm_tn(da_blk, qs) * ek
            dq = dq_inter + jnp.concatenate(dq_rows, axis=0)
            dst[...] = ds_new * jnp.exp(bl) + _hmm_tn(dob, qe)
            g = q * dq - kk * dkk
            dlf = _revcumsum_rows(g) + carry[0:1, :]
            carry[0:1, :] += jnp.sum(g, axis=0, keepdims=True)
            dkk_tot = dkk + dlf * jnp.where(kk < GATE_CLAMP, -1.0 / (1.0 - kk), 0.0)
            dq_ref[pl.ds(r0, CHUNK), :] = dq
            df_ref[pl.ds(r0, CHUNK), :] = dkk_tot * (1.0 - lbv) * (-sneg * (1.0 - sneg))
            di_ref[pl.ds(r0, CHUNK), :] = dv * _dsilu(iv)
            dlb_ref[...] += _colsum8(dkk_tot * (-sneg))
            return cr
        lax.fori_loop(0, nc, chunk, 0, unroll=2)
        done()

    def col(j):
        return pl.BlockSpec((t, LANE), lambda h: (0, HH * j + h))
    hs = pl.BlockSpec((t, LANE), lambda h: (0, h))
    res = pl.pallas_call(
        body, name="hgrn_bwd", grid=(HH,),
        in_specs=[col(0), col(1), col(2), pl.BlockSpec((1, LANE), lambda h: (0, h)), hs,
                  pl.BlockSpec((None, nc, LANE, LANE), lambda h: (h, 0, 0, 0))] + p_in,
        out_specs=[hs, hs, hs, pl.BlockSpec((8, LANE), lambda h: (0, h))] + p_out,
        out_shape=[jax.ShapeDtypeStruct((t, HH * LANE), F32)] * 3 + [jax.ShapeDtypeStruct((8, HH * LANE), F32)]
        + p_shape,
        scratch_shapes=[pltpu.VMEM((LANE, LANE), F32), pltpu.VMEM((8, LANE), F32)] + p_sem,
        compiler_params=_cp("parallel" if plan is None else "arbitrary"),
    )(h4, h4, h4, lb, do, states, *p_args)
    return res[:4], res[4:]


def _ln_fwd(z, g, b):
    mu = jnp.mean(z, axis=-1, keepdims=True)
    zc = z - mu
    rstd = lax.rsqrt(jnp.mean(zc * zc, axis=-1, keepdims=True) + EPS)
    zh = zc * rstd
    return zh * g + b, zh, rstd


def _mix_fwd(x, z, o_att, o_h, h4, gates, lng, lnb, wco, wao, ng, who, wout, t_end):
    t = x.shape[0]
    tm = _tile(t, 192)

    def body(x_ref, z_ref, oa_ref, oh_ref, hg_ref, gt_ref, lng_ref, lnb_ref, wco_ref, wao_ref, ng_ref, who_ref,
             wout_ref, x1_ref, mix_ref, ca_ref, oc_ref, ya_ref, yb_ref, yc_ref):
        i = pl.program_id(0)
        ln, _, _ = _ln_fwd(z_ref[...], lng_ref[...], lnb_ref[...])
        ca = _silu(ln).astype(_MM)
        ca_ref[...] = ca
        ya = jnp.dot(ca, wco_ref[...], preferred_element_type=F32)
        yb = _mm(oa_ref[...], wao_ref[...])
        hg = hg_ref[...]
        for h in range(HH):
            sl = slice(LANE * h, LANE * (h + 1))
            oh = oh_ref[:, sl]
            oc_ref[:, sl] = (oh * _rstd(oh) * ng_ref[:, sl] * _silu(hg[:, sl])).astype(_MM)
        yc = jnp.dot(oc_ref[...], who_ref[...], preferred_element_type=F32)
        ya_ref[...] = ya
        yb_ref[...] = yb
        yc_ref[...] = yc
        mix = (_sigmoid(gt_ref[:, 0:D]) * ya + _sigmoid(gt_ref[:, D:2 * D]) * yb
               + _sigmoid(gt_ref[:, 2 * D:3 * D]) * yc).astype(_MM)
        mix_ref[...] = mix
        x1_ref[...] = x_ref[...] + _valid_rows(i, tm, t_end) * jnp.dot(mix, wout_ref[...],
                                                                       preferred_element_type=F32)

    hd = NH * LANE
    return pl.pallas_call(
        body, name="mix_fwd", grid=(t // tm,),
        in_specs=[_row(tm, D), _row(tm, CONV_DIM), _row(tm, hd), _row(tm, 512), _row(tm, 512, 3), _row(tm, 3 * D),
                  _full((1, 512)), _full((1, 512)), _full((512, D)), _full((hd, D)), _full((1, 512)),
                  _full((512, D)), _full((D, D))],
        out_specs=[_row(tm, D), _row(tm, D), _row(tm, 512), _row(tm, 512), _row(tm, D), _row(tm, D), _row(tm, D)],
        out_shape=[jax.ShapeDtypeStruct((t, D), F32), jax.ShapeDtypeStruct((t, D), _MM),
                   jax.ShapeDtypeStruct((t, 512), _MM), jax.ShapeDtypeStruct((t, 512), _MM),
                   jax.ShapeDtypeStruct((t, D), F32), jax.ShapeDtypeStruct((t, D), F32),
                   jax.ShapeDtypeStruct((t, D), F32)],
        compiler_params=_cp("parallel"),
    )(x, z, o_att, o_h, h4, gates, lng, lnb, wco, wao, ng, who, wout)


def _mix_bwd(dx1, ya, yb, yc, gates, z, o_h, h4, lng, lnb, ng, wout, wco, wao, who, plan=None):
    t = dx1.shape[0]
    tm = _tile(t, 192)
    hd = NH * LANE
    p_args, p_in, p_out, p_shape, p_sem = _plan_specs(plan)

    def body(*refs):
        (ins, outs, _), rider = _host_refs(refs, 15, 12, 0, plan)
        (dx1_ref, ya_ref, yb_ref, yc_ref, gt_ref, z_ref, oh_ref, hg_ref, lng_ref, lnb_ref, ng_ref,
         wout_ref, wco_ref, wao_ref, who_ref) = ins
        (dgt_ref, dya_ref, dyb_ref, dyc_ref, dz_ref, doa_ref, doh_ref, dhg_ref,
         dlng_ref, dlnb_ref, dcb_ref, dng_ref) = outs
        i = pl.program_id(0)
        done = _ride(plan, rider, i, t // tm - 1)
        dmix = _mm_nt(dx1_ref[...], wout_ref[...])
        dys = []
        for j, y_ref in enumerate((ya_ref, yb_ref, yc_ref)):
            sg = _sigmoid(gt_ref[:, j * D:(j + 1) * D])
            dgt_ref[:, j * D:(j + 1) * D] = (dmix * y_ref[...] * sg * (1.0 - sg)).astype(_MM)
            dys.append((dmix * sg).astype(_MM))
        dya_ref[...], dyb_ref[...], dyc_ref[...] = dys
        dca = _mm_nt(dys[0], wco_ref[...])
        ln, zh, rstd = _ln_fwd(z_ref[...], lng_ref[...], lnb_ref[...])
        dln = dca * _dsilu(ln)
        dzh = dln * lng_ref[...]
        dz = rstd * (dzh - jnp.mean(dzh, axis=-1, keepdims=True)
                     - zh * jnp.mean(dzh * zh, axis=-1, keepdims=True))
        dz_ref[...] = dz
        doa_ref[...] = _mm_nt(dys[1], wao_ref[...])
        doc = _mm_nt(dys[2], who_ref[...])
        hg = hg_ref[...]
        dng_rows = []
        for h in range(HH):
            sl = slice(LANE * h, LANE * (h + 1))
            oh = oh_ref[:, sl]
            r = _rstd(oh)
            don = doc[:, sl] * _silu(hg[:, sl])
            dhg_ref[:, sl] = (doc[:, sl] * oh * r * ng_ref[:, sl] * _dsilu(hg[:, sl])).astype(_MM)
            doh, gn = _rms_bwd(don, oh, r, ng_ref[:, sl])
            doh_ref[:, sl] = doh
            dng_rows.append(_colsum8(gn))

        @pl.when(i == 0)
        def _():
            dlng_ref[...] = jnp.zeros_like(dlng_ref)
            dlnb_ref[...] = jnp.zeros_like(dlnb_ref)
            dcb_ref[...] = jnp.zeros_like(dcb_ref)
            dng_ref[...] = jnp.zeros_like(dng_ref)
        dlng_ref[...] += _colsum8(dln * zh)
        dlnb_ref[...] += _colsum8(dln)
        dcb_ref[...] += _colsum8(dz)
        dng_ref[...] += jnp.concatenate(dng_rows, axis=1)
        done()

    res = pl.pallas_call(
        body, name="mix_bwd", grid=(t // tm,),
        in_specs=[_row(tm, D), _row(tm, D), _row(tm, D), _row(tm, D), _row(tm, 3 * D), _row(tm, 512), _row(tm, 512),
                  _row(tm, 512, 3), _full((1, 512)), _full((1, 512)), _full((1, 512)),
                  _full((D, D)), _full((512, D)), _full((hd, D)), _full((512, D))] + p_in,
        out_specs=[_row(tm, 3 * D), _row(tm, D), _row(tm, D), _row(tm, D), _row(tm, 512), _row(tm, hd),
                   _row(tm, 512), _row(tm, 512), _full((8, 512)), _full((8, 512)), _full((8, 512)),
                   _full((8, 512))] + p_out,
        out_shape=[jax.ShapeDtypeStruct((t, 3 * D), _MM), jax.ShapeDtypeStruct((t, D), _MM),
                   jax.ShapeDtypeStruct((t, D), _MM), jax.ShapeDtypeStruct((t, D), _MM),
                   jax.ShapeDtypeStruct((t, 512), F32), jax.ShapeDtypeStruct((t, hd), F32),
                   jax.ShapeDtypeStruct((t, 512), F32), jax.ShapeDtypeStruct((t, 512), _MM)]
        + [jax.ShapeDtypeStruct((8, 512), F32)] * 4 + p_shape,
        scratch_shapes=p_sem,
        compiler_params=_cp("arbitrary"),
    )(dx1, ya, yb, yc, gates, z, o_h, h4, lng, lnb, ng, wout, wco, wao, who, *p_args)
    return res[:12], res[12:]


D_FF = 4096


def _ffn_fwd(x1, g2, w1, w2):
    t = x1.shape[0]
    tm = _tile(t, 192)

    def body(x1_ref, g_ref, w1_ref, w2_ref, x2_ref, p_ref):
        xv = x1_ref[...]
        h2 = (xv * _rstd(xv) * g_ref[...]).astype(_MM)
        p = jnp.dot(h2, w1_ref[...], preferred_element_type=F32)
        p_ref[...] = p
        r = jnp.maximum(p, 0.0)
        x2_ref[...] = xv + jnp.dot((r * r).astype(_MM), w2_ref[...], preferred_element_type=F32)

    return pl.pallas_call(
        body, name="ffn_fwd", grid=(t // tm,),
        in_specs=[_row(tm, D), _full((1, D)), _full((D, D_FF)), _full((D_FF, D))],
        out_specs=[_row(tm, D), _row(tm, D_FF)],
        out_shape=[jax.ShapeDtypeStruct((t, D), F32), jax.ShapeDtypeStruct((t, D_FF), F32)],
        compiler_params=_cp("parallel"),
    )(x1, g2, w1, w2)


def _ffn_bwd(dx2, x1, p, g2, w1t, w2t, plan=None):
    t = x1.shape[0]
    tm = _tile(t, 192)
    p_args, p_in, p_out, p_shape, p_sem = _plan_specs(plan)

    def body(*refs):
        (ins, outs, _), rider = _host_refs(refs, 6, 5, 0, plan)
        dx2_ref, x1_ref, p_ref, g_ref, w1t_ref, w2t_ref = ins
        dx1_ref, h2_ref, act_ref, dp_ref, dg_ref = outs
        i = pl.program_id(0)
        done = _ride(plan, rider, i, t // tm - 1)
        xv = x1_ref[...]
        rstd = _rstd(xv)
        h2_ref[...] = (xv * rstd * g_ref[...]).astype(_MM)
        r = jnp.maximum(p_ref[...], 0.0)
        act_ref[...] = (r * r).astype(_MM)
        dx2 = dx2_ref[...]
        da = _mm(dx2, w2t_ref[...])
        dp = (2.0 * r * da).astype(_MM)
        dp_ref[...] = dp
        dh2 = jnp.dot(dp, w1t_ref[...], preferred_element_type=F32)
        dxn, dgrow = _rms_bwd(dh2, xv, rstd, g_ref[...])
        dx1_ref[...] = dx2 + dxn

        @pl.when(i == 0)
        def _():
            dg_ref[...] = jnp.zeros_like(dg_ref)
        dg_ref[...] += _colsum8(dgrow)
        done()

    res = pl.pallas_call(
        body, name="ffn_bwd", grid=(t // tm,),
        in_specs=[_row(tm, D), _row(tm, D), _row(tm, D_FF), _full((1, D)), _full((D_FF, D)),
                  _full((D, D_FF))] + p_in,
        out_specs=[_row(tm, D), _row(tm, D), _row(tm, D_FF), _row(tm, D_FF), _full((8, D))] + p_out,
        out_shape=[jax.ShapeDtypeStruct((t, D), F32), jax.ShapeDtypeStruct((t, D), _MM),
                   jax.ShapeDtypeStruct((t, D_FF), _MM), jax.ShapeDtypeStruct((t, D_FF), _MM),
                   jax.ShapeDtypeStruct((8, D), F32)] + p_shape,
        scratch_shapes=p_sem,
        compiler_params=_cp("arbitrary"),
    )(dx2, x1, p, g2, w1t, w2t, *p_args)
    return res[:5], res[5:]


WGRAD_VMEM = 40 * 1024 * 1024


def _wgrad(a, b, name, chips=1):
    t, ka = a.shape
    nb = b.shape[1]
    cs = nb // chips
    widths = [d for d in range(cs, 0, -LANE) if cs % d == 0 and d % LANE == 0] or [cs]
    tn, tm = widths[-1], 64
    for d in widths:
        room = WGRAD_VMEM - 2 * ka * d * 4
        row_bytes = 2 * (ka * a.dtype.itemsize + d * b.dtype.itemsize) + 4 * ka
        fit = [r for r in range(64, t + 1, 64) if t % r == 0 and r * row_bytes <= room]
        if ka * d * 4 <= 16 * 1024 * 1024 and fit and (max(fit) >= 384 or d == widths[-1]):
            tn, tm = d, max(fit)
            break
    per = cs // tn

    def body(a_ref, b_ref, o_ref):
        @pl.when(pl.program_id(1) == 0)
        def _():
            o_ref[...] = jnp.zeros_like(o_ref)
        o_ref[...] += _mm_tn(a_ref[...], b_ref[...])

    if chips == 1:
        out_spec = pl.BlockSpec((ka, tn), lambda n, i: (0, n))
        out_shape = jax.ShapeDtypeStruct((ka, nb), F32)
    else:
        out_spec = pl.BlockSpec((None, ka, tn), lambda n, i: (n // per, 0, n % per))
        out_shape = jax.ShapeDtypeStruct((chips, ka, cs), F32)
    return pl.pallas_call(
        body, name="wgrad_" + name, grid=(nb // tn, t // tm),
        in_specs=[pl.BlockSpec((tm, ka), lambda n, i: (i, 0)), pl.BlockSpec((tm, tn), lambda n, i: (i, n))],
        out_specs=out_spec, out_shape=out_shape,
        compiler_params=_cp("parallel", "arbitrary"),
    )(a, b)


def _loss_head(y, target, t_end):
    t = y.shape[0]
    tm = _tile(t, 384)

    def body(y_ref, tg_ref, dy_ref, l_ref):
        i = pl.program_id(0)
        r = i * tm + lax.broadcasted_iota(jnp.int32, (tm, 1), 0)
        real = ((r >= ROW0) & (r < t_end)).astype(F32)
        diff = (y_ref[...] - tg_ref[...]) * real
        dy_ref[...] = diff * (1.0 / D)

        @pl.when(i == 0)
        def _():
            l_ref[...] = jnp.zeros_like(l_ref)
        sq = _colsum8(diff * diff)
        part = sq[:, 0:LANE]
        for j in range(1, D // LANE):
            part = part + sq[:, j * LANE:(j + 1) * LANE]
        l_ref[...] += part * (0.5 / D)

    return pl.pallas_call(
        body, name="loss_head", grid=(t // tm,),
        in_specs=[_row(tm, D), _row(tm, D)],
        out_specs=[_row(tm, D), _full((8, LANE))],
        out_shape=[jax.ShapeDtypeStruct((t, D), F32), jax.ShapeDtypeStruct((8, LANE), F32)],
        compiler_params=_cp("arbitrary"),
    )(y, target)


def _lower_bounds_fwd(logits):
    depth, n = logits.shape

    def body(l_ref, lb_ref):
        lg = l_ref[...]
        m = jnp.max(lg, axis=0, keepdims=True)
        e = jnp.exp(lg - m)
        p = e / jnp.sum(e, axis=0, keepdims=True)
        acc = jnp.zeros((1, n), F32)
        for l in range(depth):
            if l > 0:
                acc = acc + p[l:l + 1, :]
            lb_ref[l:l + 1, :] = acc

    return pl.pallas_call(body, name="lower_bounds_fwd", out_shape=jax.ShapeDtypeStruct((depth, n), F32))(logits)


def _lower_bounds_bwd(logits, dlb):
    depth, n = logits.shape

    def body(l_ref, dlb_ref, dl_ref):
        lg = l_ref[...]
        m = jnp.max(lg, axis=0, keepdims=True)
        e = jnp.exp(lg - m)
        p = e / jnp.sum(e, axis=0, keepdims=True)
        dps = [jnp.zeros((1, n), F32)]
        for j in range(1, depth):
            acc = jnp.zeros((1, n), F32)
            for l in range(j, depth):
                acc = acc + dlb_ref[l:l + 1, :]
            dps.append(acc)
        dot = jnp.zeros((1, n), F32)
        for j in range(depth):
            dot = dot + p[j:j + 1, :] * dps[j]
        for j in range(depth):
            dl_ref[j:j + 1, :] = p[j:j + 1, :] * (dps[j] - dot)

    return pl.pallas_call(body, name="lower_bounds_bwd", out_shape=jax.ShapeDtypeStruct((depth, n), F32))(logits, dlb)


def _ew_tile(rows, cols, n_arrays):
    cap = max(16, (32 * 1024 * 1024) // (8 * n_arrays * cols))
    for mult in (16, 8):
        fit = [t for t in range(mult, rows + 1, mult) if rows % t == 0 and t <= cap]
        if fit:
            return max(fit)
    return rows


def _adamw_math(w, g, m, v):
    mn = ADAM_B1 * m + (1.0 - ADAM_B1) * g
    vn = ADAM_B2 * v + (1.0 - ADAM_B2) * (g * g)
    m_hat = mn / (1.0 - ADAM_B1 ** ADAM_STEP)
    v_hat = vn / (1.0 - ADAM_B2 ** ADAM_STEP)
    return -ADAM_LR * (m_hat / (jnp.sqrt(v_hat) + ADAM_EPS) + ADAM_WD * w), mn, vn


def _adamw_layers(w, m, v, g0, g1, g_sibling, name):
    _, r, c_ = w.shape
    tr = _ew_tile(r, c_, 10)

    def body(w_ref, m_ref, v_ref, g0_ref, g1_ref, gs_ref, g_ref, d_ref, mo_ref, vo_ref):
        layer = pl.program_id(0)
        own = jnp.where(layer == 0, g0_ref[...], g1_ref[...])
        g = jnp.where(layer == lax.axis_index("c"), own, gs_ref[...])
        g_ref[...] = g
        d_ref[...], mo_ref[...], vo_ref[...] = _adamw_math(w_ref[...], g, m_ref[...], v_ref[...])

    lay = pl.BlockSpec((None, tr, c_), lambda l, i: (l, i, 0))
    flat = pl.BlockSpec((tr, c_), lambda l, i: (i, 0))
    return pl.pallas_call(
        body, name="adamw_" + name, grid=(2, r // tr),
        in_specs=[lay, lay, lay, flat, flat, flat], out_specs=[lay] * 4,
        out_shape=[jax.ShapeDtypeStruct(w.shape, F32)] * 4,
        compiler_params=_cp("parallel", "parallel"),
    )(w, m, v, g0, g1, g_sibling)


def _adamw(w, g, m, v, name):
    rows, cols = w.shape
    tr = _ew_tile(rows, cols, 7)

    def body(w_ref, g_ref, m_ref, v_ref, d_ref, mo_ref, vo_ref):
        d_ref[...], mo_ref[...], vo_ref[...] = _adamw_math(w_ref[...], g_ref[...], m_ref[...], v_ref[...])

    spec = pl.BlockSpec((tr, cols), lambda i: (i, 0))
    return pl.pallas_call(
        body, name="adamw_" + name, grid=(rows // tr,),
        in_specs=[spec] * 4, out_specs=[spec] * 3,
        out_shape=[jax.ShapeDtypeStruct((rows, cols), F32)] * 3,
        compiler_params=_cp("parallel"),
    )(w, g, m, v)


DEPTH = 2
BIG_SHAPES = {"w_in": ((1024, 6560), 1), "w_conv_out": ((512, 1024), 1), "w_uq": ((256, 768), 1),
              "w_ukv": ((128, 1024), 1), "w_attn_out": ((512, 1024), 1), "w_hgrn_out": ((512, 1024), 1),
              "w_out": ((1024, 1024), 0), "w_ff1": ((1024, 4096), 1), "w_ff2": ((4096, 1024), 0)}
BIG = tuple(BIG_SHAPES)
SMALL_SIZES = {"norm1_g": 1024, "conv_b": 512, "conv_ln_g": 512, "conv_ln_b": 512, "q_a_norm_g": 256,
               "kv_a_norm_g": 128, "q_norm_g": 96, "k_norm_g": 96, "hgrn_lb_logits": 512, "hgrn_norm_g": 512,
               "norm2_g": 1024}
SMALL = tuple(SMALL_SIZES)
W_IN_COLS = 6560
W_IN_SHARD = W_IN_COLS // 4
W_IN_SEGS = ((0, 1024, SEG_AG[0]), (1024, 1280, SEG_CQ[0]), (1280, 1408, SEG_CKV[0]), (1408, 1440, SEG_KR[0] + 64),
             (1440, 3488, SEG_H4[0]), (3488, 6560, SEG_GATES[0]))


def _pad_heads(w, nh, used, axis):
    shp = w.shape
    w = w.reshape(shp[:axis] + (nh, used) + shp[axis + 1:])
    pad = [(0, 0)] * w.ndim
    pad[axis + 1] = (0, LANE - used)
    w = jnp.pad(w, pad)
    return w.reshape(shp[:axis] + (nh * LANE,) + shp[axis + 1:])


def _unpad_heads(w, nh, used, axis):
    shp = w.shape
    w = w.reshape(shp[:axis] + (nh, LANE) + shp[axis + 1:])
    w = lax.slice_in_dim(w, 0, used, axis=axis + 1)
    return w.reshape(shp[:axis] + (nh * used,) + shp[axis + 1:])


def _w_in_from_chips(p4):
    def orig(a, b):
        out = []
        while a < b:
            s = a // W_IN_SHARD
            e = min(b, (s + 1) * W_IN_SHARD)
            out.append(p4[s][:, a - W_IN_SHARD * s:e - W_IN_SHARD * s])
            a = e
        return out
    zc = lambda n: jnp.zeros((D, n), p4[0].dtype)
    parts = (orig(3488, 6560) + orig(0, 1024) + orig(1440, 3488) + orig(1024, 1280) + orig(1280, 1408)
             + [zc(64)] + orig(1408, 1440) + [zc(32)])
    return jnp.concatenate(parts, axis=1)


def _w_in_grad_to_chips(dw):
    chips = []
    for s in range(4):
        a, b = W_IN_SHARD * s, W_IN_SHARD * (s + 1)
        parts = []
        for o0, o1, p0 in W_IN_SEGS:
            lo, hi = max(a, o0), min(b, o1)
            if lo < hi:
                parts.append(dw[:, p0 + lo - o0:p0 + hi - o0])
        chips.append(jnp.concatenate(parts, axis=1))
    return jnp.stack(chips)


def _cat_chips(p4, axis):
    return jnp.concatenate([p4[s] for s in range(4)], axis=axis)


EARLY = ("w_in", "w_uq", "w_ukv")
LATE = tuple(k for k in BIG if k not in EARLY)


def _prep_late(pieces):
    pc = lambda k: [pieces[k][s].astype(_MM) for s in range(4)]
    return dict(wao=_pad_heads(_cat_chips(pc("w_attn_out"), 1), NH, 64, 0), wco=_cat_chips(pc("w_conv_out"), 1),
                who=_cat_chips(pc("w_hgrn_out"), 1), wout=_cat_chips(pc("w_out"), 0),
                w1=_cat_chips(pc("w_ff1"), 1), w2=_cat_chips(pc("w_ff2"), 0))


def _prep_early(pieces, small, l):
    mm = lambda a: a.astype(_MM)
    pc = lambda k: [mm(pieces[k][s]) for s in range(4)]
    w_in_p = _w_in_from_chips(pc("w_in"))
    wuq = jnp.concatenate([_pad_heads(pc("w_uq")[s], 2, QK_DIM, 1) for s in range(4)], axis=1)
    wukv = _cat_chips(pc("w_ukv"), 1).reshape(128, NH, 128)
    wk = _pad_heads(wukv[:, :, :64].reshape(128, NH * 64), NH, 64, 1)
    wv = _pad_heads(wukv[:, :, 64:].reshape(128, NH * 64), NH, 64, 1)
    row = lambda a: a.astype(F32).reshape(1, -1)
    p = dict(
        w_in=w_in_p, w_in_t=w_in_p.T, wuq=wuq, wk=wk, wv=wv,
        g1=row(small["norm1_g"][l]), g2=row(small["norm2_g"][l]),
        cw=jnp.pad(small["conv_w"][l].astype(F32), ((0, 1), (0, 0))), cb=row(small["conv_b"][l]),
        lng=row(small["conv_ln_g"][l]), lnb=row(small["conv_ln_b"][l]),
        qag=row(small["q_a_norm_g"][l]), kvag=row(small["kv_a_norm_g"][l]),
        qng=jnp.pad(row(small["q_norm_g"][l]), ((0, 0), (0, LANE - QK_DIM))),
        kng=jnp.pad(row(small["k_norm_g"][l]), ((0, 0), (0, LANE - QK_DIM))),
        ng=row(small["hgrn_norm_g"][l]),
    )
    return p


def _rope_tables(t):
    pos = (jnp.arange(t, dtype=jnp.int32) - FRONT).astype(F32)
    inv_freq = 10000.0 ** (-jnp.arange(16, dtype=F32) / 16)
    ang = pos[:, None] * inv_freq[None, :]
    cos, sin = jnp.cos(ang), jnp.sin(ang)
    one = jnp.ones((t, 64), F32)
    z16, z32, z64 = jnp.zeros((t, 16), F32), jnp.zeros((t, 32), F32), jnp.zeros((t, 64), F32)
    c = jnp.concatenate([one, cos, cos, z32], axis=1)
    s1 = jnp.concatenate([z64, -sin, z16, z32], axis=1)
    s2 = jnp.concatenate([z64, z16, sin, z32], axis=1)
    return c, s1, s2


def _layer_fwd(x, p, lb, rope, t_end, plan=None, on_rode=None):
    gates, ag, h4, cq, ckv, kr, hb = _in_proj_fwd(x, p["g1"], p["w_in"])
    z = _conv_fwd(ag, p["cw"], p["cb"])
    q, k, v, cqn, ckvn = _mla_pre_fwd(cq, ckv, kr, p["qag"], p["wuq"], p["kvag"], p["wk"], p["wv"], p["qng"],
                                      p["kng"], *rope)
    (o_att, lse), rode = _attn_fwd(q, k, v, plan)
    if on_rode is not None:
        on_rode(rode)
    o_h, states = _hgrn_fwd(h4, lb)
    x1, mix, ca, oc, ya, yb, yc = _mix_fwd(x, z, o_att, o_h, h4, gates, p["lng"], p["lnb"], p["wco"], p["wao"],
                                           p["ng"], p["who"], p["wout"], t_end)
    x2, pre = _ffn_fwd(x1, p["g2"], p["w1"], p["w2"])
    saved = dict(x=x, gates=gates, ag=ag, h4=h4, cq=cq, ckv=ckv, kr=kr, hb=hb, z=z, q=q, k=k, v=v, cqn=cqn,
                 ckvn=ckvn, o_att=o_att, lse=lse, o_h=o_h, states=states, x1=x1, mix=mix, ca=ca, oc=oc,
                 ya=ya, yb=yb, yc=yc, pre=pre)
    return x2, saved


def _layer_bwd(dx2, s, p, lb, rope, t_end, rides=None):
    rides = rides or {}
    (dx1, h2, act, dp, dg2), rode = _ffn_bwd(dx2, s["x1"], s["pre"], p["g2"], p["w1"].T, p["w2"].T,
                                             rides.get("ffn"))
    g = {"w_ff1": _wgrad(h2, dp, "ff1", 4), "w_ff2": _wgrad(act, dx2, "ff2").reshape(4, D_FF // 4, D),
         "norm2_g": dg2.sum(0)}
    plan_mix = rides["mix"](rode, g) if "mix" in rides else None
    (dgt, dya, dyb, dyc, dz, doa, doh, dhg, dlng, dlnb, dcb, dng), rode = _mix_bwd(
        dx1, s["ya"], s["yb"], s["yc"], s["gates"], s["z"], s["o_h"], s["h4"], p["lng"], p["lnb"], p["ng"],
        p["wout"], p["wco"], p["wao"], p["who"], plan_mix)
    g["w_out"] = _wgrad(s["mix"], dx1, "out").reshape(4, D // 4, D)
    g["w_conv_out"] = _wgrad(s["ca"], dya, "conv_out", 4)
    g["w_attn_out"] = _unpad_heads(_wgrad(s["o_att"], dyb, "attn_out", 4), NH, 64, 1)
    g["w_hgrn_out"] = _wgrad(s["oc"], dyc, "hgrn_out", 4)
    g["conv_ln_g"], g["conv_ln_b"], g["conv_b"], g["hgrn_norm_g"] = dlng.sum(0), dlnb.sum(0), dcb.sum(0), dng.sum(0)
    da, dg, dcw = _conv_bwd(s["ag"], p["cw"], dz)
    g["conv_w"] = dcw[:CONV_K]
    plan_attn = rides["attn"](rode, g) if "attn" in rides else None
    (dq, dk, dv), rode_attn = _attn_bwd(s["q"], s["k"], s["v"], s["o_att"], s["lse"], doa, plan_attn)
    dcq, dckv, dkr, dqraw, dkraw, dqag, dkvag, dqng, dkng = _mla_pre_bwd(
        dq, dk, dv, s["cq"], s["ckv"], s["kr"], p["qag"], p["wuq"], p["kvag"], p["wk"], p["wv"], p["qng"],
        p["kng"], *rope)
    g["w_uq"] = _unpad_heads(_wgrad(s["cqn"], dqraw, "uq", 4), 2, QK_DIM, 2)
    dwk = _unpad_heads(_wgrad(s["ckvn"], dkraw, "uk"), NH, 64, 1).reshape(128, NH, 64)
    dwv = _unpad_heads(_wgrad(s["ckvn"], dv, "uv"), NH, 64, 1).reshape(128, NH, 64)
    g["w_ukv"] = jnp.concatenate([dwk, dwv], axis=2).reshape(128, 4, 256).transpose(1, 0, 2)
    g["q_a_norm_g"], g["kv_a_norm_g"] = dqag.sum(0), dkvag.sum(0)
    g["q_norm_g"], g["k_norm_g"] = dqng.sum(0)[:QK_DIM], dkng.sum(0)[:QK_DIM]
    plan_hgrn = rides["hgrn"](rode_attn) if "hgrn" in rides else None
    (dhq, dhf, dhi, dlb), rode_hgrn = _hgrn_bwd(s["h4"], lb, doh, s["states"], plan_hgrn)
    mm = lambda a: a.astype(_MM)
    du = jnp.concatenate([dgt, mm(da), mm(dg), mm(dhq), mm(dhf), mm(dhi), dhg, dcq, dckv, dkr], axis=1)
    dx, dg1 = _in_proj_bwd(du, s["x"], dx1, p["g1"], p["w_in_t"], t_end)
    g["norm1_g"] = dg1.sum(0)
    g["w_in"] = _w_in_grad_to_chips(_wgrad(s["hb"], du, "in"))
    return dx, g, dlb.sum(0), (rode_attn, rode_hgrn)


def _device_step(x, target, small, pieces0, pieces1=None, fwd_ride=None, bwd_rides=None):
    s_real = x.shape[0]
    t_end = ROW0 + s_real
    t = -(-t_end // LANE) * LANE
    zrow = lambda n: jnp.zeros((n, D), F32)
    xp = jnp.concatenate([zrow(FRONT), small["meta"].astype(F32), x, zrow(t - t_end)], axis=0)
    tp = jnp.concatenate([zrow(ROW0), target, zrow(t - t_end)], axis=0)
    rope = _rope_tables(t)
    logits = small["hgrn_lb_logits"].astype(F32)
    lbs = _lower_bounds_fwd(logits)
    prm0 = _prep_early(pieces0, small, 0)
    got = {}
    if fwd_ride is None:
        prm0.update(_prep_late(pieces0))
        h, sv0 = _layer_fwd(xp, prm0, lbs[0:1], rope, t_end)
    else:
        def on_rode(rode):
            late0, got["pieces1"] = fwd_ride[1](rode)
            prm0.update(_prep_late(late0))
        h, sv0 = _layer_fwd(xp, prm0, lbs[0:1], rope, t_end, fwd_ride[0], on_rode)
        pieces1 = got["pieces1"]
    prm1 = _prep_early(pieces1, small, 1)
    if fwd_ride is None:
        prm1.update(_prep_late(pieces1))
        h, sv1 = _layer_fwd(h, prm1, lbs[1:2], rope, t_end)
    else:
        h, sv1 = _layer_fwd(h, prm1, lbs[1:2], rope, t_end, fwd_ride[2],
                            lambda rode: prm1.update(_prep_late(fwd_ride[3](rode))))
    dh, lsum = _loss_head(h, tp, t_end)
    loss = jnp.sum(lsum)
    rides1, make_rides0 = (None, None) if bwd_rides is None else bwd_rides
    dh, g1, dlb1, rode1 = _layer_bwd(dh, sv1, prm1, lbs[1:2], rope, t_end, rides1)
    dh, g0, dlb0, rode = _layer_bwd(dh, sv0, prm0, lbs[0:1], rope, t_end,
                                    None if make_rides0 is None else make_rides0(g1, rode1))
    dlogits = _lower_bounds_bwd(logits, jnp.stack([dlb0, dlb1]))
    grads = [g0, g1]
    for l in range(DEPTH):
        grads[l]["hgrn_lb_logits"] = dlogits[l]
    return loss, dh[ROW0:t_end], grads, dh[FRONT:ROW0], rode


MESH = pl.DeviceIdType.MESH
_ANY = pl.BlockSpec(memory_space=pl.ANY)
SMALL_ROWS = 64
SMALL_LEN = SMALL_ROWS * 1024


def _mesh_pos():
    return lax.axis_index("x"), lax.axis_index("y"), lax.axis_index("c")


def _other_chips(x, y):
    return [(1 - x, y), (x, 1 - y), (1 - x, 1 - y)]


class _Plan:
    def __init__(self, name, ins, out_shapes, sems, start, finish, relay=None):
        self.name, self.ins, self.out_shapes, self.sems = name, list(ins), list(out_shapes), list(sems)
        self.start, self.finish, self.relay = start, finish, relay


def _run_plan(plan):
    ni, no = len(plan.ins), len(plan.out_shapes)

    def body(*refs):
        ins, outs, sems = refs[:ni], refs[ni:ni + no], refs[ni + no:]
        plan.start(ins, outs, sems)
        if plan.relay is not None:
            plan.relay(ins, outs, sems)
        plan.finish(ins, outs, sems)

    return pl.pallas_call(body, name=plan.name, in_specs=[_ANY] * ni, out_specs=[_ANY] * no,
                          out_shape=plan.out_shapes, scratch_shapes=plan.sems)(*plan.ins)


def _plan_specs(plan):
    if plan is None:
        return [], [], [], [], []
    return plan.ins, [_ANY] * len(plan.ins), [_ANY] * len(plan.out_shapes), plan.out_shapes, plan.sems


def _host_refs(refs, n_in, n_out, n_scratch, plan):
    ni = 0 if plan is None else len(plan.ins)
    no = 0 if plan is None else len(plan.out_shapes)
    o0 = n_in + ni
    s0 = o0 + n_out + no
    own = (refs[:n_in], refs[o0:o0 + n_out], refs[s0:s0 + n_scratch])
    rider = (refs[n_in:o0], refs[o0 + n_out:s0], refs[s0 + n_scratch:])
    return own, rider


def _ride(plan, rider, step, last):
    if plan is None:
        return lambda: None

    @pl.when(step == 0)
    def _():
        plan.start(*rider)

    def done():
        if plan.relay is not None:
            @pl.when(step == last - 1)
            def _():
                plan.relay(*rider)

        @pl.when(step == last)
        def _():
            plan.finish(*rider)
    return done


def _merge_plans(name, plans):
    def parts(ins, outs, sems):
        i = o = s = 0
        for p in plans:
            ni, no, ns = len(p.ins), len(p.out_shapes), len(p.sems)
            yield p, (ins[i:i + ni], outs[o:o + no], sems[s:s + ns])
            i, o, s = i + ni, o + no, s + ns

    def start(ins, outs, sems):
        for p, refs in parts(ins, outs, sems):
            p.start(*refs)

    def relay(ins, outs, sems):
        for p, refs in parts(ins, outs, sems):
            if p.relay is not None:
                p.relay(*refs)

    def finish(ins, outs, sems):
        for p, refs in parts(ins, outs, sems):
            p.finish(*refs)

    return _Plan(name, [a for p in plans for a in p.ins], [a for p in plans for a in p.out_shapes],
                 [a for p in plans for a in p.sems], start, finish, relay)


def _plan_gather(own, layer, name):
    nw = len(own)

    def copies(ins, outs, sems):
        send_sems, recv_sems = sems

        def over_ici(w, j, chip_of_data, to):
            return pltpu.make_async_remote_copy(
                src_ref=ins[w].at[layer], dst_ref=outs[w].at[chip_of_data], send_sem=send_sems.at[w, j],
                recv_sem=recv_sems.at[w, j], device_id=to, device_id_type=MESH)

        def over_d2d(w, j, chip_of_data, to):
            return pltpu.make_async_remote_copy(
                src_ref=outs[w].at[chip_of_data], dst_ref=outs[w].at[chip_of_data], send_sem=send_sems.at[w, 3 + j],
                recv_sem=recv_sems.at[w, 3 + j], device_id=to, device_id_type=MESH)
        return over_ici, over_d2d

    def start(ins, outs, sems):
        x, y, c = _mesh_pos()
        over_ici, _ = copies(ins, outs, sems)

        @pl.when(c == layer)
        def _():
            for j, (px, py) in enumerate(_other_chips(x, y)):
                for w in range(nw):
                    over_ici(w, j, 2 * x + y, (px, py, layer)).start()

    def relay(ins, outs, sems):
        x, y, c = _mesh_pos()
        over_ici, over_d2d = copies(ins, outs, sems)

        @pl.when(c == layer)
        def _():
            for j, (px, py) in enumerate(_other_chips(x, y)):
                for w in range(nw):
                    over_ici(w, j, 2 * px + py, (x, y, c)).wait_recv()
                    over_d2d(w, j, 2 * px + py, (x, y, 1 - layer)).start()

    def finish(ins, outs, sems):
        x, y, c = _mesh_pos()
        over_ici, over_d2d = copies(ins, outs, sems)
        chips = _other_chips(x, y)

        @pl.when(c == layer)
        def _():
            for j, (px, py) in enumerate(chips):
                for w in range(nw):
                    over_ici(w, j, 2 * x + y, (px, py, layer)).wait_send()
                    over_d2d(w, j, 2 * px + py, (x, y, 1 - layer)).wait_send()

        @pl.when(c != layer)
        def _():
            for j, (px, py) in enumerate(chips):
                for w in range(nw):
                    over_d2d(w, j, 2 * px + py, (x, y, c)).wait_recv()

    return _Plan(name, own,
                 [jax.ShapeDtypeStruct((4,) + a.shape[1:], a.dtype) for a in own],
                 [pltpu.SemaphoreType.DMA((nw, 6)), pltpu.SemaphoreType.DMA((nw, 6))], start, finish, relay)


def _plan_to_sibling(arrs, layer, name):
    nw = len(arrs)

    def copy(ins, outs, sems, w):
        x, y, _ = _mesh_pos()
        return pltpu.make_async_remote_copy(src_ref=ins[w], dst_ref=outs[w], send_sem=sems[0].at[w],
                                            recv_sem=sems[1].at[w], device_id=(x, y, layer), device_id_type=MESH)

    def start(ins, outs, sems):
        @pl.when(lax.axis_index("c") != layer)
        def _():
            for w in range(nw):
                copy(ins, outs, sems, w).start()

    def finish(ins, outs, sems):
        c = lax.axis_index("c")

        @pl.when(c != layer)
        def _():
            for w in range(nw):
                copy(ins, outs, sems, w).wait_send()

        @pl.when(c == layer)
        def _():
            for w in range(nw):
                copy(ins, outs, sems, w).wait_recv()

    return _Plan(name, arrs, [jax.ShapeDtypeStruct(a.shape, a.dtype) for a in arrs],
                 [pltpu.SemaphoreType.DMA((nw,)), pltpu.SemaphoreType.DMA((nw,))], start, finish)


def _plan_scatter(parts, layer, name):
    nw = len(parts)

    def start(ins, outs, sems):
        x, y, c = _mesh_pos()

        @pl.when(c == layer)
        def _():
            for j, (px, py) in enumerate(_other_chips(x, y)):
                for w in range(nw):
                    pltpu.make_async_remote_copy(
                        src_ref=ins[w].at[2 * px + py], dst_ref=outs[w].at[2 * x + y], send_sem=sems[0].at[w, j],
                        recv_sem=sems[1].at[w, j], device_id=(px, py, layer), device_id_type=MESH).start()

    def finish(ins, outs, sems):
        x, y, c = _mesh_pos()

        @pl.when(c == layer)
        def _():
            for j, (px, py) in enumerate(_other_chips(x, y)):
                for w in range(nw):
                    pltpu.make_async_remote_copy(
                        src_ref=ins[w].at[2 * px + py], dst_ref=outs[w].at[2 * px + py], send_sem=sems[0].at[w, j],
                        recv_sem=sems[1].at[w, j], device_id=(x, y, c), device_id_type=MESH).wait()

    return _Plan(name, parts, [jax.ShapeDtypeStruct(a.shape, a.dtype) for a in parts],
                 [pltpu.SemaphoreType.DMA((nw, 3)), pltpu.SemaphoreType.DMA((nw, 3))], start, finish)


def _sibling_exchange(reds0, reds1):
    nw = len(reds0)

    def body(*refs):
        a0, a1, outs = refs[:nw], refs[nw:2 * nw], refs[2 * nw:3 * nw]
        send_sems, recv_sems = refs[3 * nw:]
        x, y, c = _mesh_pos()

        def copy(w, src):
            return pltpu.make_async_remote_copy(src_ref=src, dst_ref=outs[w], send_sem=send_sems.at[w],
                                                recv_sem=recv_sems.at[w], device_id=(x, y, 1 - c),
                                                device_id_type=MESH)

        @pl.when(c == 0)
        def _():
            for w in range(nw):
                copy(w, a0[w]).start()

        @pl.when(c == 1)
        def _():
            for w in range(nw):
                copy(w, a1[w]).start()

        for w in range(nw):
            copy(w, a0[w]).wait()

    return pl.pallas_call(
        body, name="sibling_exchange", in_specs=[_ANY] * (2 * nw), out_specs=[_ANY] * nw,
        out_shape=[jax.ShapeDtypeStruct(a.shape, a.dtype) for a in reds0],
        scratch_shapes=[pltpu.SemaphoreType.DMA((nw,)), pltpu.SemaphoreType.DMA((nw,))],
    )(*reds0, *reds1)


def _all_reduce_small(v, name):
    rows, cols = v.shape

    def body(v_ref, o_ref, slots, send_sems, recv_sems):
        x, y, c = _mesh_pos()
        me = 4 * x + 2 * y + c
        slots[me] = v_ref[...]
        peers = []
        for rel in range(1, 8):
            fx, fy, fc = (rel >> 2) & 1, (rel >> 1) & 1, rel & 1
            px = 1 - x if fx else x
            py = 1 - y if fy else y
            pc = 1 - c if fc else c
            peers.append((px, py, pc))
        cps = [pltpu.make_async_remote_copy(src_ref=v_ref, dst_ref=slots.at[me], send_sem=send_sems.at[k],
                                            recv_sem=recv_sems.at[k], device_id=peer, device_id_type=MESH)
               for k, peer in enumerate(peers)]
        for cp in cps:
            cp.start()
        for k, (px, py, pc) in enumerate(peers):
            pltpu.make_async_remote_copy(src_ref=v_ref, dst_ref=slots.at[4 * px + 2 * py + pc],
                                         send_sem=send_sems.at[k], recv_sem=recv_sems.at[k], device_id=(x, y, c),
                                         device_id_type=MESH).wait_recv()
        for cp in cps:
            cp.wait_send()
        acc = slots[0]
        for d in range(1, 8):
            acc = acc + slots[d]
        o_ref[...] = acc

    vm = pl.BlockSpec(memory_space=pltpu.VMEM)
    return pl.pallas_call(
        body, name=name, in_specs=[vm], out_specs=vm,
        out_shape=jax.ShapeDtypeStruct((rows, cols), F32),
        scratch_shapes=[pltpu.VMEM((8, rows, cols), F32), pltpu.SemaphoreType.DMA((7,)),
                        pltpu.SemaphoreType.DMA((7,))],
    )(v)


def _add_to_wire(a, b, name):
    n4, r, c_ = a.shape
    rows = n4 * r
    tr = _ew_tile(rows, c_, 3)

    def body(a_ref, b_ref, o_ref):
        o_ref[...] = (a_ref[...] + b_ref[...]).astype(o_ref.dtype)

    spec = pl.BlockSpec((tr, c_), lambda i: (i, 0))
    out = pl.pallas_call(
        body, name="add_to_wire_" + name, grid=(rows // tr,), in_specs=[spec, spec], out_specs=spec,
        out_shape=jax.ShapeDtypeStruct((rows, c_), jnp.bfloat16), compiler_params=_cp("parallel"),
    )(a.reshape(rows, c_), b.reshape(rows, c_))
    return out.reshape(n4, r, c_)


def _sum_chips(recv, wire, name):
    _, r, c_ = recv.shape
    tr = _ew_tile(r, c_, 6)

    def body(r_ref, w_ref, o_ref):
        chip = 2 * lax.axis_index("x") + lax.axis_index("y")
        acc = None
        for s in range(4):
            term = jnp.where(chip == s, w_ref[s], r_ref[s]).astype(F32)
            acc = term if acc is None else acc + term
        o_ref[...] = acc

    blk = pl.BlockSpec((4, tr, c_), lambda i: (0, i, 0))
    return pl.pallas_call(
        body, name="sum_chips_" + name, grid=(r // tr,),
        in_specs=[blk, blk], out_specs=pl.BlockSpec((tr, c_), lambda i: (i, 0)),
        out_shape=jax.ShapeDtypeStruct((r, c_), F32),
        compiler_params=_cp("parallel"),
    )(recv, wire)


def _pack_small(vals, meta_full, conv_w_full):
    flat = jnp.concatenate([vals[k].reshape(-1) for k in SMALL] + [meta_full.reshape(-1), conv_w_full.reshape(-1)])
    return jnp.pad(flat, (0, SMALL_LEN - flat.shape[0])).reshape(SMALL_ROWS, 1024)


def _unpack_small(buf):
    flat = buf.reshape(-1)
    out, off = {}, 0
    for k in SMALL:
        n = DEPTH * SMALL_SIZES[k]
        out[k] = flat[off:off + n].reshape(DEPTH, SMALL_SIZES[k])
        off += n
    meta = flat[off:off + N_META * D].reshape(N_META, D)
    off += N_META * D
    conv_w = flat[off:off + DEPTH * CONV_K * CONV_DIM].reshape(DEPTH, CONV_K, CONV_DIM)
    return out, meta, conv_w


def kernel(x, meta, norm1_g, w_in, conv_w, conv_b, conv_ln_g, conv_ln_b, w_conv_out, q_a_norm_g, w_uq, kv_a_norm_g, w_ukv, q_norm_g, k_norm_g, w_attn_out, hgrn_lb_logits, hgrn_norm_g, w_hgrn_out, w_out, norm2_g, w_ff1, w_ff2, loss_target, m_meta, m_norm1_g, m_w_in, m_conv_w, m_conv_b, m_conv_ln_g, m_conv_ln_b, m_w_conv_out, m_q_a_norm_g, m_w_uq, m_kv_a_norm_g, m_w_ukv, m_q_norm_g, m_k_norm_g, m_w_attn_out, m_hgrn_lb_logits, m_hgrn_norm_g, m_w_hgrn_out, m_w_out, m_norm2_g, m_w_ff1, m_w_ff2, v_meta, v_norm1_g, v_w_in, v_conv_w, v_conv_b, v_conv_ln_g, v_conv_ln_b, v_w_conv_out, v_q_a_norm_g, v_w_uq, v_kv_a_norm_g, v_w_ukv, v_q_norm_g, v_k_norm_g, v_w_attn_out, v_hgrn_lb_logits, v_hgrn_norm_g, v_w_hgrn_out, v_w_out, v_norm2_g, v_w_ff1, v_w_ff2):
    names = ("meta", "norm1_g", "w_in", "conv_w", "conv_b", "conv_ln_g", "conv_ln_b", "w_conv_out", "q_a_norm_g",
             "w_uq", "kv_a_norm_g", "w_ukv", "q_norm_g", "k_norm_g", "w_attn_out", "hgrn_lb_logits", "hgrn_norm_g",
             "w_hgrn_out", "w_out", "norm2_g", "w_ff1", "w_ff2")
    w = dict(zip(names, (meta, norm1_g, w_in, conv_w, conv_b, conv_ln_g, conv_ln_b, w_conv_out, q_a_norm_g, w_uq,
                         kv_a_norm_g, w_ukv, q_norm_g, k_norm_g, w_attn_out, hgrn_lb_logits, hgrn_norm_g, w_hgrn_out,
                         w_out, norm2_g, w_ff1, w_ff2)))
    m = dict(zip(names, (m_meta, m_norm1_g, m_w_in, m_conv_w, m_conv_b, m_conv_ln_g, m_conv_ln_b, m_w_conv_out,
                         m_q_a_norm_g, m_w_uq, m_kv_a_norm_g, m_w_ukv, m_q_norm_g, m_k_norm_g, m_w_attn_out,
                         m_hgrn_lb_logits, m_hgrn_norm_g, m_w_hgrn_out, m_w_out, m_norm2_g, m_w_ff1, m_w_ff2)))
    v = dict(zip(names, (v_meta, v_norm1_g, v_w_in, v_conv_w, v_conv_b, v_conv_ln_g, v_conv_ln_b, v_w_conv_out,
                         v_q_a_norm_g, v_w_uq, v_kv_a_norm_g, v_w_ukv, v_q_norm_g, v_k_norm_g, v_w_attn_out,
                         v_hgrn_lb_logits, v_hgrn_norm_g, v_w_hgrn_out, v_w_out, v_norm2_g, v_w_ff1, v_w_ff2)))
    cx, cy, cc = _mesh_pos()
    chip = 2 * cx + cy
    zero = jnp.zeros((), jnp.int32)

    own = {k: w[k].astype(_MM) for k in BIG}

    def as_pieces(names, gathered, layer):
        return {k: [jnp.where(chip == s, own[k][layer], g[s]) for s in range(4)] for k, g in zip(names, gathered)}

    pieces0 = as_pieces(EARLY, _run_plan(_plan_gather([own[k] for k in EARLY], 0, "gather_l0_early")), 0)
    fwd_ride = (_merge_plans("gather_mid", [_plan_gather([own[k] for k in LATE], 0, "gather_l0_late"),
                                            _plan_gather([own[k] for k in EARLY], 1, "gather_l1_early")]),
                lambda got: (as_pieces(LATE, got[:len(LATE)], 0), as_pieces(EARLY, got[len(LATE):], 1)),
                _plan_gather([own[k] for k in LATE], 1, "gather_l1_late"),
                lambda got: as_pieces(LATE, got, 1))
    meta_slab = lax.dynamic_update_slice(jnp.zeros((N_META, D), F32), meta, (zero, chip * (D // 4)))
    convw_slab = lax.dynamic_update_slice(jnp.zeros((DEPTH, CONV_K, CONV_DIM), F32), conv_w,
                                          (zero, zero, chip * (CONV_DIM // 4)))
    zsmall = {k: jnp.zeros((DEPTH, SMALL_SIZES[k]), F32) for k in SMALL}
    south = (cc == 0).astype(F32)
    _, meta_full, convw_full = _unpack_small(
        _all_reduce_small(_pack_small(zsmall, meta_slab, convw_slab) * south, "gather_small"))
    small = {k: w[k] for k in SMALL}
    small["meta"] = meta_full
    small["conv_w"] = convw_full

    FFN = ("w_ff1", "w_ff2")
    MID = ("w_out", "w_conv_out", "w_attn_out", "w_hgrn_out")
    REST = tuple(k for k in BIG if k not in FFN + MID)
    held = {}

    def to_wire(names, layer, mine, from_sibling):
        return lax.cond(
            cc == layer,
            lambda: [_add_to_wire(a, b, "%s_l%d" % (k, layer)) for k, a, b in zip(names, mine, from_sibling)],
            lambda: [jnp.zeros(a.shape, jnp.bfloat16) for a in mine])

    def chip_sum(names, layer, got, wire):
        return lax.cond(
            cc == layer,
            lambda: [_sum_chips(r, s, "%s_l%d" % (k, layer)) for k, r, s in zip(names, got, wire)],
            lambda: [jnp.zeros(s.shape[1:], F32) for s in wire])

    NONFFN = tuple(k for k in BIG if k not in FFN)

    def ride_attn_l1(_, g1):
        held["g1_ffn"] = [g1[k] for k in FFN]
        return _plan_to_sibling(held["g1_ffn"], 1, "swap_grads_l1_ffn")

    def rides_l0(g1, rode_l1):
        g1_rest = [g1[k] for k in NONFFN]

        def ride_mix(from_sibling1, g0_ffn):
            wire1 = dict(zip(FFN, to_wire(FFN, 1, held["g1_ffn"], rode_l1[0])))
            wire1.update(zip(NONFFN, to_wire(NONFFN, 1, g1_rest, from_sibling1)))
            held["wire1"] = [wire1[k] for k in BIG]
            held["g0_ffn"] = [g0_ffn[k] for k in FFN]
            return _plan_to_sibling(held["g0_ffn"], 0, "swap_grads_l0_ffn")

        def ride_attn(from_sibling0, g0):
            held["wire0_ffn"] = to_wire(FFN, 0, held["g0_ffn"], from_sibling0)
            held["g0_mid"] = [g0[k] for k in MID]
            return _merge_plans("exchange_grads_mid", [
                _plan_scatter(held["wire1"], 1, "scatter_grads_l1"),
                _plan_scatter(held["wire0_ffn"], 0, "scatter_grads_l0_ffn"),
                _plan_to_sibling(held["g0_mid"], 0, "swap_grads_l0_mid")])

        def ride_hgrn(rode_attn):
            held["wire0_mid"] = to_wire(MID, 0, held["g0_mid"], rode_attn[len(BIG) + len(FFN):])
            return _plan_scatter(held["wire0_mid"], 0, "scatter_grads_l0_mid")

        return {"ffn": _plan_to_sibling(g1_rest, 1, "swap_grads_l1_rest"), "mix": ride_mix, "attn": ride_attn,
                "hgrn": ride_hgrn}

    loss_share, grad_x, gl, g_meta, (got, got_mid) = _device_step(
        x[0], loss_target[0], small, pieces0, None, fwd_ride, ({"attn": ride_attn_l1}, rides_l0))

    reds1 = chip_sum(BIG, 1, got[:len(BIG)], held["wire1"])
    reds0 = dict(zip(FFN, chip_sum(FFN, 0, got[len(BIG):len(BIG) + len(FFN)], held["wire0_ffn"])))
    reds0.update(zip(MID, chip_sum(MID, 0, got_mid, held["wire0_mid"])))
    g0_rest = [gl[0][k] for k in REST]
    wire0 = to_wire(REST, 0, g0_rest, _run_plan(_plan_to_sibling(g0_rest, 0, "swap_grads_l0_rest")))
    reds0.update(zip(REST, chip_sum(REST, 0, _run_plan(_plan_scatter(wire0, 0, "scatter_grads_l0_rest")), wire0)))
    reds0 = [reds0[k] for k in BIG]
    reds_sibling = _sibling_exchange(reds0, reds1)
    grads, delta, new_m, new_v = {}, {}, {}, {}
    t_view = lambda k, a: jnp.swapaxes(a, -1, -2) if k == "w_in" else a
    for k, r0, r1, theirs in zip(BIG, reds0, reds1, reds_sibling):
        grads[k], delta[k], new_m[k], new_v[k] = [
            t_view(k, a) for a in _adamw_layers(t_view(k, w[k]), t_view(k, m[k]), t_view(k, v[k]), t_view(k, r0),
                                                t_view(k, r1), t_view(k, theirs), k)]

    g_small_local = {k: jnp.stack([gl[l][k] for l in range(DEPTH)]) for k in SMALL}
    g_convw_local = jnp.stack([gl[l]["conv_w"] for l in range(DEPTH)])
    reduced = _all_reduce_small(
        _pack_small(g_small_local, g_meta, g_convw_local).at[SMALL_ROWS - 1, 1023].set(loss_share), "reduce_small")
    loss = reduced[SMALL_ROWS - 1, 1023]
    g_small, g_meta_full, g_convw_full = _unpack_small(reduced)
    grads.update(g_small)
    grads["meta"] = lax.dynamic_slice(g_meta_full, (zero, chip * (D // 4)), (N_META, D // 4))
    grads["conv_w"] = lax.dynamic_slice(g_convw_full, (zero, zero, chip * (CONV_DIM // 4)),
                                        (DEPTH, CONV_K, CONV_DIM // 4))

    def small_pack(src):
        return _pack_small(src, jnp.pad(src["meta"], ((0, 0), (0, D - D // 4))),
                           jnp.pad(src["conv_w"], ((0, 0), (0, 0), (0, CONV_DIM - CONV_DIM // 4))))

    def small_unpack(buf):
        out, meta_p, convw_p = _unpack_small(buf)
        out["meta"] = meta_p[:, :D // 4]
        out["conv_w"] = convw_p[:, :, :CONV_DIM // 4]
        return out

    d_s, m_s, v_s = [small_unpack(a) for a in _adamw(small_pack(w), small_pack(grads), small_pack(m),
                                                     small_pack(v), "small")]
    delta.update(d_s)
    new_m.update(m_s)
    new_v.update(v_s)
    return (loss, grad_x[None], *[grads[k] for k in names], *[delta[k] for k in names],
            *[new_m[k] for k in names], *[new_v[k] for k in names])
```

```python
import functools

import jax
import jax.numpy as jnp
from jax import lax
from jax.experimental import pallas as pl
from jax.experimental.pallas import tpu as pltpu

F32 = jnp.float32
_MM = jnp.bfloat16

D = 1024
N_META = 16
FRONT = 48
ROW0 = FRONT + N_META
EPS = 1e-6
GATE_CLAMP = 1.0 - 1e-6
CONV_K = 31
CONV_DIM = 512
NH = 8
QK_DIM = 96
ATT_SCALE = QK_DIM ** -0.5
HH = 4
CHUNK = 64
SUB = 16
EXP_CLIP = 60.0
NEG = -1e30
LANE = 128

SEG_GATES = (0, 3072)
SEG_AG = (3072, 4096)
SEG_H4 = (4096, 6144)
SEG_CQ = (6144, 6400)
SEG_CKV = (6400, 6528)
SEG_KR = (6528, 6656)
N_IN_P = 6656

ADAM_LR = 0.001
ADAM_B1 = 0.9
ADAM_B2 = 0.999
ADAM_EPS = 1e-08
ADAM_WD = 0.01
ADAM_STEP = 10

VMEM_LIMIT = 56 * 1024 * 1024


def _tile(n, pref):
    best = 64
    for t in range(64, pref + 1, 64):
        if n % t == 0:
            best = t
    return best


def _cp(*sem):
    return pltpu.CompilerParams(dimension_semantics=tuple(sem), vmem_limit_bytes=VMEM_LIMIT)


def _row(tm, n, col=0):
    return pl.BlockSpec((tm, n), lambda i: (i, col))


def _full(shape):
    return pl.BlockSpec(shape, lambda i: (0,) * len(shape))


def _mm(a, b):
    return jnp.dot(a.astype(_MM), b.astype(_MM), preferred_element_type=F32)


def _mm_nt(a, b):
    return lax.dot_general(a.astype(_MM), b.astype(_MM), (((1,), (1,)), ((), ())), preferred_element_type=F32)


def _mm_tn(a, b):
    return lax.dot_general(a.astype(_MM), b.astype(_MM), (((0,), (0,)), ((), ())), preferred_element_type=F32)


def _split3(x):
    hi = x.astype(jnp.bfloat16)
    return hi, (x - hi.astype(F32)).astype(jnp.bfloat16)


def _dot3(a, b, dims):
    ah, al = _split3(a)
    bh, bl = _split3(b)
    dg = lambda u, v: lax.dot_general(u, v, (dims, ((), ())), preferred_element_type=F32)
    return dg(ah, bh) + (dg(ah, bl) + dg(al, bh))


def _hmm(a, b):
    return _dot3(a, b, ((1,), (0,)))


def _hmm_nt(a, b):
    return _dot3(a, b, ((1,), (1,)))


def _hmm_tn(a, b):
    return _dot3(a, b, ((0,), (0,)))


def _sigmoid(x):
    return 1.0 / (1.0 + jnp.exp(-x))


def _rstd(x, n=None):
    n = x.shape[-1] if n is None else n
    return lax.rsqrt(jnp.sum(x * x, axis=-1, keepdims=True) * (1.0 / n) + EPS)


def _rms_bwd(dy, x, rstd, g, n=None):
    n = x.shape[-1] if n is None else n
    xh = x * rstd
    dxh = dy * g
    dx = rstd * (dxh - xh * (jnp.sum(dxh * xh, axis=-1, keepdims=True) * (1.0 / n)))
    return dx, dy * xh


def _valid_rows(i, tm, t_valid_end):
    r = i * tm + lax.broadcasted_iota(jnp.int32, (tm, 1), 0)
    return ((r >= FRONT) & (r < t_valid_end)).astype(F32)


def _colsum8(x):
    n, c = x.shape
    return jnp.sum(x.reshape(n // 8, 8, c), axis=0)


def _in_proj_fwd(x, g1, w):
    t = x.shape[0]
    tm = _tile(t, 192)
    segs = (SEG_GATES, SEG_AG, SEG_H4, SEG_CQ, SEG_CKV, SEG_KR)

    def body(x_ref, g_ref, w_ref, gates_ref, ag_ref, h4_ref, cq_ref, ckv_ref, kr_ref, hb_ref):
        xv = x_ref[...]
        hb = (xv * _rstd(xv) * g_ref[...]).astype(_MM)
        hb_ref[...] = hb
        for ref, (a, b) in zip((gates_ref, ag_ref, h4_ref, cq_ref, ckv_ref, kr_ref), segs):
            ref[...] = jnp.dot(hb, w_ref[:, a:b], preferred_element_type=F32)

    outs = [jax.ShapeDtypeStruct((t, b - a), F32) for a, b in segs] + [jax.ShapeDtypeStruct((t, D), _MM)]
    return pl.pallas_call(
        body, name="in_proj_fwd", grid=(t // tm,),
        in_specs=[_row(tm, D), _full((1, D)), _full((D, N_IN_P))],
        out_specs=[_row(tm, b - a) for a, b in segs] + [_row(tm, D)],
        out_shape=outs, compiler_params=_cp("parallel"),
    )(x, g1, w)


def _in_proj_bwd(du, x, dx1, g1, wt, t_end):
    t = x.shape[0]
    tm = _tile(t, 192)

    def body(du_ref, x_ref, dx1_ref, g_ref, wt_ref, dx_ref, dg_ref):
        i = pl.program_id(0)
        dh = jnp.dot(du_ref[...], wt_ref[...], preferred_element_type=F32)
        xv = x_ref[...]
        dxn, dgrow = _rms_bwd(dh, xv, _rstd(xv), g_ref[...])
        dx_ref[...] = _valid_rows(i, tm, t_end) * (dx1_ref[...] + dxn)

        @pl.when(i == 0)
        def _():
            dg_ref[...] = jnp.zeros_like(dg_ref)
        dg_ref[...] += _colsum8(dgrow)

    return pl.pallas_call(
        body, name="in_proj_bwd", grid=(t // tm,),
        in_specs=[_row(tm, N_IN_P), _row(tm, D), _row(tm, D), _full((1, D)), _full((N_IN_P, D))],
        out_specs=[_row(tm, D), _full((8, D))],
        out_shape=[jax.ShapeDtypeStruct((t, D), F32), jax.ShapeDtypeStruct((8, D), F32)],
        compiler_params=_cp("arbitrary"),
    )(du, x, dx1, g1, wt)


CONV_CH = 128


def _conv_fwd(ag, cw, cb):
    t = ag.shape[0]
    n = t // CONV_CH

    def body(a_ref, g_ref, w_ref, b_ref, z_ref, hp):
        hp[0:32, :] = jnp.zeros((32, LANE), F32)

        def fill(i, c):
            r = pl.multiple_of(i * CONV_CH, CONV_CH)
            hp[pl.ds(32 + r, CONV_CH), :] = a_ref[pl.ds(r, CONV_CH), :] * _sigmoid(g_ref[pl.ds(r, CONV_CH), :])
            return c
        lax.fori_loop(0, n, fill, 0)

        def conv(i, c):
            r = pl.multiple_of(i * CONV_CH, CONV_CH)
            acc = jnp.broadcast_to(b_ref[...], (CONV_CH, LANE))
            for k in range(CONV_K):
                acc = acc + w_ref[k:k + 1, :] * hp[pl.ds(r + (k + 2), CONV_CH), :]
            z_ref[pl.ds(r, CONV_CH), :] = acc
            return c
        lax.fori_loop(0, n, conv, 0)

    nb = CONV_DIM // LANE
    return pl.pallas_call(
        body, name="conv_fwd", grid=(nb,),
        in_specs=[pl.BlockSpec((t, LANE), lambda j: (0, j)), pl.BlockSpec((t, LANE), lambda j: (0, nb + j)),
                  pl.BlockSpec((32, LANE), lambda j: (0, j)), pl.BlockSpec((1, LANE), lambda j: (0, j))],
        out_specs=pl.BlockSpec((t, LANE), lambda j: (0, j)),
        out_shape=jax.ShapeDtypeStruct((t, CONV_DIM), F32),
        scratch_shapes=[pltpu.VMEM((t + 32, LANE), F32)],
        compiler_params=_cp("parallel"),
    )(ag, ag, cw, cb)


def _conv_bwd(ag, cw, dz):
    t = ag.shape[0]
    n = t // CONV_CH

    def body(a_ref, g_ref, w_ref, dz_ref, da_ref, dg_ref, dcw_ref, hp, dzp, accw):
        hp[0:32, :] = jnp.zeros((32, LANE), F32)
        dzp[pl.ds(t, 32), :] = jnp.zeros((32, LANE), F32)
        accw[...] = jnp.zeros_like(accw)

        def fill(i, c):
            r = pl.multiple_of(i * CONV_CH, CONV_CH)
            hp[pl.ds(32 + r, CONV_CH), :] = a_ref[pl.ds(r, CONV_CH), :] * _sigmoid(g_ref[pl.ds(r, CONV_CH), :])
            dzp[pl.ds(r, CONV_CH), :] = dz_ref[pl.ds(r, CONV_CH), :]
            return c
        lax.fori_loop(0, n, fill, 0)

        def step(i, c):
            r = pl.multiple_of(i * CONV_CH, CONV_CH)
            dzc = dz_ref[pl.ds(r, CONV_CH), :]
            dh = jnp.zeros((CONV_CH, LANE), F32)
            for k in range(CONV_K):
                dh = dh + w_ref[k:k + 1, :] * dzp[pl.ds(r + (CONV_K - 1 - k), CONV_CH), :]
                accw[8 * k:8 * k + 8, :] += _colsum8(dzc * hp[pl.ds(r + (k + 2), CONV_CH), :])
            a = a_ref[pl.ds(r, CONV_CH), :]
            sg = _sigmoid(g_ref[pl.ds(r, CONV_CH), :])
            da_ref[pl.ds(r, CONV_CH), :] = dh * sg
            dg_ref[pl.ds(r, CONV_CH), :] = dh * a * sg * (1.0 - sg)
            return c
        lax.fori_loop(0, n, step, 0)

        for k in range(CONV_K):
            dcw_ref[k:k + 1, :] = jnp.sum(accw[8 * k:8 * k + 8, :], axis=0, keepdims=True)
        dcw_ref[CONV_K:32, :] = jnp.zeros((32 - CONV_K, LANE), F32)

    nb = CONV_DIM // LANE
    colspec = pl.BlockSpec((t, LANE), lambda j: (0, j))
    return pl.pallas_call(
        body, name="conv_bwd", grid=(nb,),
        in_specs=[colspec, pl.BlockSpec((t, LANE), lambda j: (0, nb + j)),
                  pl.BlockSpec((32, LANE), lambda j: (0, j)), colspec],
        out_specs=[colspec, colspec, pl.BlockSpec((32, LANE), lambda j: (0, j))],
        out_shape=[jax.ShapeDtypeStruct((t, CONV_DIM), F32), jax.ShapeDtypeStruct((t, CONV_DIM), F32),
                   jax.ShapeDtypeStruct((32, CONV_DIM), F32)],
        scratch_shapes=[pltpu.VMEM((t + 32, LANE), F32), pltpu.VMEM((t + 32, LANE), F32),
                        pltpu.VMEM((8 * 32, LANE), F32)],
        compiler_params=_cp("parallel"),
    )(ag, ag, cw, dz)


def _rope(x, c, s1, s2):
    return x * c + pltpu.roll(x, LANE - 16, 1) * s1 + pltpu.roll(x, 16, 1) * s2


def _rope_t(dy, c, s1, s2):
    return dy * c + pltpu.roll(dy * s1, 16, 1) + pltpu.roll(dy * s2, LANE - 16, 1)


def _mla_pre_fwd(cq, ckv, kr, qag, wuq, kvag, wk, wv, qng, kng, rc, rs1, rs2):
    t = cq.shape[0]
    tm = _tile(t, 384)

    def body(cq_ref, ckv_ref, kr_ref, qag_ref, wuq_ref, kvag_ref, wk_ref, wv_ref, qng_ref, kng_ref,
             c_ref, s1_ref, s2_ref, q_ref, k_ref, v_ref, cqn_ref, ckvn_ref):
        cqv = cq_ref[...]
        cqn = (cqv * _rstd(cqv) * qag_ref[...]).astype(_MM)
        cqn_ref[...] = cqn
        ckvv = ckv_ref[...]
        ckvn = (ckvv * _rstd(ckvv) * kvag_ref[...]).astype(_MM)
        ckvn_ref[...] = ckvn
        qraw = jnp.dot(cqn, wuq_ref[...], preferred_element_type=F32)
        kraw = jnp.dot(ckvn, wk_ref[...], preferred_element_type=F32)
        v_ref[...] = jnp.dot(ckvn, wv_ref[...], preferred_element_type=F32).astype(_MM)
        krv = kr_ref[...]
        c, s1, s2 = c_ref[...], s1_ref[...], s2_ref[...]
        for h in range(NH):
            sl = slice(LANE * h, LANE * (h + 1))
            qh = qraw[:, sl]
            qn = qh * _rstd(qh, QK_DIM) * qng_ref[...]
            q_ref[:, sl] = (_rope(qn, c, s1, s2) * ATT_SCALE).astype(_MM)
            kh = kraw[:, sl] + krv
            kn = kh * _rstd(kh, QK_DIM) * kng_ref[...]
            k_ref[:, sl] = _rope(kn, c, s1, s2).astype(_MM)

    hd = NH * LANE
    return pl.pallas_call(
        body, name="mla_pre_fwd", grid=(t // tm,),
        in_specs=[_row(tm, 256), _row(tm, 128), _row(tm, 128), _full((1, 256)), _full((256, hd)),
                  _full((1, 128)), _full((128, hd)), _full((128, hd)), _full((1, LANE)), _full((1, LANE)),
                  _row(tm, LANE), _row(tm, LANE), _row(tm, LANE)],
        out_specs=[_row(tm, hd), _row(tm, hd), _row(tm, hd), _row(tm, 256), _row(tm, 128)],
        out_shape=[jax.ShapeDtypeStruct((t, hd), _MM)] * 3 + [jax.ShapeDtypeStruct((t, 256), _MM),
                                                              jax.ShapeDtypeStruct((t, 128), _MM)],
        compiler_params=_cp("parallel"),
    )(cq, ckv, kr, qag, wuq, kvag, wk, wv, qng, kng, rc, rs1, rs2)


def _mla_pre_bwd(dq, dk, dv, cq, ckv, kr, qag, wuq, kvag, wk, wv, qng, kng, rc, rs1, rs2):
    t = cq.shape[0]
    tm = _tile(t, 192)
    hd = NH * LANE

    def body(dq_ref, dk_ref, dv_ref, cq_ref, ckv_ref, kr_ref, qag_ref, wuq_ref, kvag_ref, wk_ref,
             wv_ref, qng_ref, kng_ref, c_ref, s1_ref, s2_ref,
             dcq_ref, dckv_ref, dkr_ref, dqraw_ref, dkraw_ref, dqag_ref, dkvag_ref, dqng_ref, dkng_ref):
        i = pl.program_id(0)
        cqv = cq_ref[...]
        rq_in = _rstd(cqv)
        cqn = (cqv * rq_in * qag_ref[...]).astype(_MM)
        ckvv = ckv_ref[...]
        rkv_in = _rstd(ckvv)
        ckvn = (ckvv * rkv_in * kvag_ref[...]).astype(_MM)
        qraw = jnp.dot(cqn, wuq_ref[...], preferred_element_type=F32)
        kraw = jnp.dot(ckvn, wk_ref[...], preferred_element_type=F32)
        krv = kr_ref[...]
        c, s1, s2 = c_ref[...], s1_ref[...], s2_ref[...]
        dkr = jnp.zeros((tm, LANE), F32)
        dqng = jnp.zeros((8, LANE), F32)
        dkng = jnp.zeros((8, LANE), F32)
        for h in range(NH):
            sl = slice(LANE * h, LANE * (h + 1))
            qh = qraw[:, sl]
            dqn = _rope_t(dq_ref[:, sl] * ATT_SCALE, c, s1, s2)
            dqh, gq = _rms_bwd(dqn, qh, _rstd(qh, QK_DIM), qng_ref[...], QK_DIM)
            dqraw_ref[:, sl] = dqh.astype(_MM)
            dqng = dqng + _colsum8(gq)
            kh = kraw[:, sl] + krv
            dkn = _rope_t(dk_ref[:, sl], c, s1, s2)
            dkh, gk = _rms_bwd(dkn, kh, _rstd(kh, QK_DIM), kng_ref[...], QK_DIM)
            dkraw_ref[:, sl] = dkh.astype(_MM)
            dkr = dkr + dkh
            dkng = dkng + _colsum8(gk)
        dkr_ref[...] = dkr.astype(_MM)
        dcqn = _mm_nt(dqraw_ref[...], wuq_ref[...])
        dcq, gqa = _rms_bwd(dcqn, cqv, rq_in, qag_ref[...])
        dcq_ref[...] = dcq.astype(_MM)
        dckvn = _mm_nt(dkraw_ref[...], wk_ref[...]) + _mm_nt(dv_ref[...], wv_ref[...])
        dckv, gkva = _rms_bwd(dckvn, ckvv, rkv_in, kvag_ref[...])
        dckv_ref[...] = dckv.astype(_MM)

        @pl.when(i == 0)
        def _():
            dqag_ref[...] = jnp.zeros_like(dqag_ref)
            dkvag_ref[...] = jnp.zeros_like(dkvag_ref)
            dqng_ref[...] = jnp.zeros_like(dqng_ref)
            dkng_ref[...] = jnp.zeros_like(dkng_ref)
        dqag_ref[...] += _colsum8(gqa)
        dkvag_ref[...] += _colsum8(gkva)
        dqng_ref[...] += dqng
        dkng_ref[...] += dkng

    return pl.pallas_call(
        body, name="mla_pre_bwd", grid=(t // tm,),
        in_specs=[_row(tm, hd), _row(tm, hd), _row(tm, hd), _row(tm, 256), _row(tm, 128), _row(tm, 128),
                  _full((1, 256)), _full((256, hd)), _full((1, 128)), _full((128, hd)),
                  _full((128, hd)), _full((1, LANE)), _full((1, LANE)),
                  _row(tm, LANE), _row(tm, LANE), _row(tm, LANE)],
        out_specs=[_row(tm, 256), _row(tm, 128), _row(tm, 128), _row(tm, hd), _row(tm, hd),
                   _full((8, 256)), _full((8, 128)), _full((8, LANE)), _full((8, LANE))],
        out_shape=[jax.ShapeDtypeStruct((t, 256), _MM), jax.ShapeDtypeStruct((t, 128), _MM),
                   jax.ShapeDtypeStruct((t, 128), _MM), jax.ShapeDtypeStruct((t, hd), _MM),
                   jax.ShapeDtypeStruct((t, hd), _MM), jax.ShapeDtypeStruct((8, 256), F32),
                   jax.ShapeDtypeStruct((8, 128), F32), jax.ShapeDtypeStruct((8, LANE), F32),
                   jax.ShapeDtypeStruct((8, LANE), F32)],
        compiler_params=_cp("arbitrary"),
    )(dq, dk, dv, cq, ckv, kr, qag, wuq, kvag, wk, wv, qng, kng, rc, rs1, rs2)


ATT_TILE = 704


def _attn_mask(r0, c0, tq):
    rows = r0 + lax.broadcasted_iota(jnp.int32, (tq, 1), 0)
    cols = c0 + lax.broadcasted_iota(jnp.int32, (1, tq), 1)
    return (cols <= rows) & (cols >= FRONT)


def _attn_fwd(q, k, v, plan=None):
    t = q.shape[0]
    tq = _tile(t, ATT_TILE)
    nq = t // tq
    p_args, p_in, p_out, p_shape, p_sem = _plan_specs(plan)

    def body(*refs):
        ((q_ref, k_ref, v_ref), (o_ref, lse_ref), _), rider = _host_refs(refs, 3, 2, 0, plan)
        done = _ride(plan, rider, pl.program_id(0), NH - 1)

        def qloop(qi, carry):
            r0 = pl.multiple_of(qi * tq, tq)
            qb = q_ref[pl.ds(r0, tq), :]

            def kstep(kj, st, masked):
                m, l, acc = st
                c0 = pl.multiple_of(kj * tq, tq)
                s = _mm_nt(qb, k_ref[pl.ds(c0, tq), :])
                if masked:
                    s = jnp.where(_attn_mask(r0, c0, tq), s, NEG)
                m2 = jnp.maximum(m, jnp.max(s, axis=-1, keepdims=True))
                p = jnp.exp(s - m2)
                a = jnp.exp(m - m2)
                l = a * l + jnp.sum(p, axis=-1, keepdims=True)
                acc = a * acc + _mm(p, v_ref[pl.ds(c0, tq), :])
                return m2, l, acc

            st = kstep(0, (jnp.full((tq, 1), NEG, F32), jnp.zeros((tq, 1), F32), jnp.zeros((tq, LANE), F32)), True)
            st = lax.fori_loop(1, qi, lambda kj, s_: kstep(kj, s_, False), st)
            m, l, acc = lax.cond(qi > 0, lambda s_: kstep(qi, s_, True), lambda s_: s_, st)
            o_ref[pl.ds(r0, tq), :] = acc / l
            lse_ref[pl.ds(r0, tq), :] = m + jnp.log(l)
            return carry
        lax.fori_loop(0, nq, qloop, 0)
        done()

    hs = pl.BlockSpec((t, LANE), lambda h: (0, h))
    res = pl.pallas_call(
        body, name="attn_fwd", grid=(NH,),
        in_specs=[hs, hs, hs] + p_in,
        out_specs=[hs, pl.BlockSpec((None, t, 1), lambda h: (h, 0, 0))] + p_out,
        out_shape=[jax.ShapeDtypeStruct((t, NH * LANE), F32), jax.ShapeDtypeStruct((NH, t, 1), F32)] + p_shape,
        scratch_shapes=p_sem,
        compiler_params=_cp("parallel" if plan is None else "arbitrary"),
    )(q, k, v, *p_args)
    return res[:2], res[2:]


def _attn_bwd(q, k, v, o, lse, do, plan=None):
    t = q.shape[0]
    tq = _tile(t, ATT_TILE)
    nq = t // tq
    p_args, p_in, p_out, p_shape, p_sem = _plan_specs(plan)

    def body(*refs):
        (ins, (dq_ref, dk_ref, dv_ref), (delta,)), rider = _host_refs(refs, 6, 3, 1, plan)
        q_ref, k_ref, v_ref, o_ref, lse_ref, do_ref = ins
        done = _ride(plan, rider, pl.program_id(0), NH - 1)

        def prep(i, c):
            r0 = pl.multiple_of(i * tq, tq)
            delta[pl.ds(r0, tq), :] = jnp.sum(do_ref[pl.ds(r0, tq), :] * o_ref[pl.ds(r0, tq), :], axis=-1,
                                              keepdims=True)
            dq_ref[pl.ds(r0, tq), :] = jnp.zeros((tq, LANE), F32)
            return c
        lax.fori_loop(0, nq, prep, 0)

        def kloop(kj, carry):
            c0 = pl.multiple_of(kj * tq, tq)
            kb = k_ref[pl.ds(c0, tq), :]
            vb = v_ref[pl.ds(c0, tq), :]

            def qstep(qi, st, masked):
                dkb, dvb = st
                r0 = pl.multiple_of(qi * tq, tq)
                qb = q_ref[pl.ds(r0, tq), :]
                dob = do_ref[pl.ds(r0, tq), :].astype(_MM)
                s = _mm_nt(qb, kb)
                if masked:
                    s = jnp.where(_attn_mask(r0, c0, tq), s, NEG)
                p = jnp.exp(s - lse_ref[pl.ds(r0, tq), :])
                dvb = dvb + _mm_tn(p, dob)
                dp = _mm_nt(dob, vb)
                ds = (p * (dp - delta[pl.ds(r0, tq), :])).astype(_MM)
                dkb = dkb + _mm_tn(ds, qb)
                dq_ref[pl.ds(r0, tq), :] += _mm(ds, kb)
                return dkb, dvb

            st = qstep(kj, (jnp.zeros((tq, LANE), F32), jnp.zeros((tq, LANE), F32)), True)
            dkb, dvb = lax.cond(
                kj == 0,
                lambda s_: lax.fori_loop(kj + 1, nq, lambda qi, t_: qstep(qi, t_, True), s_),
                lambda s_: lax.fori_loop(kj + 1, nq, lambda qi, t_: qstep(qi, t_, False), s_), st)
            dk_ref[pl.ds(c0, tq), :] = dkb
            dv_ref[pl.ds(c0, tq), :] = dvb
            return carry
        lax.fori_loop(0, nq, kloop, 0)
        done()

    hs = pl.BlockSpec((t, LANE), lambda h: (0, h))
    res = pl.pallas_call(
        body, name="attn_bwd", grid=(NH,),
        in_specs=[hs, hs, hs, hs, pl.BlockSpec((None, t, 1), lambda h: (h, 0, 0)), hs] + p_in,
        out_specs=[hs, hs, hs] + p_out,
        out_shape=[jax.ShapeDtypeStruct((t, NH * LANE), F32)] * 3 + p_shape,
        scratch_shapes=[pltpu.VMEM((t, 1), F32)] + p_sem,
        compiler_params=_cp("parallel" if plan is None else "arbitrary"),
    )(q, k, v, o, lse, do, *p_args)
    return res[:3], res[3:]


def _cumsum_rows(x):
    n = x.shape[0]
    rows = lax.broadcasted_iota(jnp.int32, (n, 1), 0)
    d = 1
    while d < n:
        x = x + jnp.where(rows >= d, pltpu.roll(x, d, 0), 0.0)
        d *= 2
    return x


def _revcumsum_rows(x):
    n = x.shape[0]
    rows = lax.broadcasted_iota(jnp.int32, (n, 1), 0)
    d = 1
    while d < n:
        x = x + jnp.where(rows < n - d, pltpu.roll(x, n - d, 0), 0.0)
        d *= 2
    return x


def _hgrn_gates(f, lb):
    sneg = _sigmoid(-f)
    kk = (1.0 - lb) * sneg
    lf = jnp.log1p(-jnp.minimum(kk, GATE_CLAMP))
    return kk, lf, sneg


def _silu(x):
    return x * _sigmoid(x)


def _dsilu(x):
    s = _sigmoid(x)
    return s * (1.0 + x * (1.0 - s))


def _hgrn_intra(q, kk, b):
    parts = []
    for blk in range(CHUNK // SUB):
        lo = blk * SUB
        ref = jnp.zeros((1, LANE), F32) if blk == 0 else b[lo - 1:lo, :]
        eq = jnp.exp(b[lo:lo + SUB, :] - ref)
        ek = jnp.exp(jnp.minimum(ref - b, EXP_CLIP))
        parts.append((q[lo:lo + SUB, :] * eq, kk * ek, eq, ek))
    return parts


def _chunk_causal():
    return lax.broadcasted_iota(jnp.int32, (CHUNK, CHUNK), 1) <= lax.broadcasted_iota(jnp.int32, (CHUNK, CHUNK), 0)


def _hgrn_fwd(h4, lb):
    t = h4.shape[0]
    nc = t // CHUNK

    def body(q_ref, f_ref, i_ref, lb_ref, o_ref, s_ref, st):
        st[...] = jnp.zeros_like(st)
        causal = _chunk_causal()

        def chunk(c, carry):
            r0 = pl.multiple_of(c * CHUNK, CHUNK)
            q = q_ref[pl.ds(r0, CHUNK), :]
            kk, lf, _ = _hgrn_gates(f_ref[pl.ds(r0, CHUNK), :], lb_ref[...])
            v = _silu(i_ref[pl.ds(r0, CHUNK), :])
            b = _cumsum_rows(lf)
            s_prev = st[...]
            s_ref[c] = s_prev
            o = _hmm_nt(q * jnp.exp(b), s_prev)
            a = jnp.concatenate([_hmm_nt(qs, ks) for qs, ks, _, _ in _hgrn_intra(q, kk, b)], axis=0)
            a = jnp.where(causal, a, 0.0)
            o_ref[pl.ds(r0, CHUNK), :] = o + _hmm(a, v)
            bl = b[CHUNK - 1:CHUNK, :]
            st[...] = s_prev * jnp.exp(bl) + _hmm_tn(v, kk * jnp.exp(bl - b))
            return carry
        lax.fori_loop(0, nc, chunk, 0, unroll=2)

    def col(j):
        return pl.BlockSpec((t, LANE), lambda h: (0, HH * j + h))
    return pl.pallas_call(
        body, name="hgrn_fwd", grid=(HH,),
        in_specs=[col(0), col(1), col(2), pl.BlockSpec((1, LANE), lambda h: (0, h))],
        out_specs=[pl.BlockSpec((t, LANE), lambda h: (0, h)),
                   pl.BlockSpec((None, nc, LANE, LANE), lambda h: (h, 0, 0, 0))],
        out_shape=[jax.ShapeDtypeStruct((t, HH * LANE), F32), jax.ShapeDtypeStruct((HH, nc, LANE, LANE), F32)],
        scratch_shapes=[pltpu.VMEM((LANE, LANE), F32)],
        compiler_params=_cp("parallel"),
    )(h4, h4, h4, lb)


def _hgrn_bwd(h4, lb, do, states, plan=None):
    t = h4.shape[0]
    nc = t // CHUNK
    p_args, p_in, p_out, p_shape, p_sem = _plan_specs(plan)

    def body(*refs):
        (ins, outs, (dst, carry)), rider = _host_refs(refs, 6, 4, 2, plan)
        q_ref, f_ref, i_ref, lb_ref, do_ref, s_ref = ins
        dq_ref, df_ref, di_ref, dlb_ref = outs
        done = _ride(plan, rider, pl.program_id(0), HH - 1)
        dst[...] = jnp.zeros_like(dst)
        carry[...] = jnp.zeros_like(carry)
        dlb_ref[...] = jnp.zeros_like(dlb_ref)
        causal = _chunk_causal()

        def chunk(cc, cr):
            c = nc - 1 - cc
            r0 = pl.multiple_of(c * CHUNK, CHUNK)
            q = q_ref[pl.ds(r0, CHUNK), :]
            lbv = lb_ref[...]
            kk, lf, sneg = _hgrn_gates(f_ref[pl.ds(r0, CHUNK), :], lbv)
            iv = i_ref[pl.ds(r0, CHUNK), :]
            v = _silu(iv)
            b = _cumsum_rows(lf)
            s_prev = s_ref[c]
            ds_new = dst[...]
            dob = do_ref[pl.ds(r0, CHUNK), :]
            e = jnp.exp(b)
            qe = q * e
            bl = b[CHUNK - 1:CHUNK, :]
            etail = jnp.exp(bl - b)
            kd = kk * etail
            dq_inter = _hmm(dob, s_prev) * e
            dv = _hmm_nt(kd, ds_new)
            dkk = _hmm(v, ds_new) * etail
            parts = _hgrn_intra(q, kk, b)
            a = jnp.where(causal, jnp.concatenate([_hmm_nt(qs, ks) for qs, ks, _, _ in parts], axis=0), 0.0)
            da = jnp.where(causal, _hmm_nt(dob, v), 0.0)
            dv = dv + _hmm_tn(a, dob)
            dq_rows = []
            for blk, (qs, ks, eq, ek) in enumerate(parts):
                da_blk = da[blk * SUB:(blk + 1) * SUB, :]
                dq_rows.append(_hmm(da_blk, ks) * eq)
                dkk = dkk + _hmm_tn(da_blk, qs) * ek
            dq = dq_inter + jnp.concatenate(dq_rows, axis=0)
            dst[...] = ds_new * jnp.exp(bl) + _hmm_tn(dob, qe)
            g = q * dq - kk * dkk
            dlf = _revcumsum_rows(g) + carry[0:1, :]
            carry[0:1, :] += jnp.sum(g, axis=0, keepdims=True)
            dkk_tot = dkk + dlf * jnp.where(kk < GATE_CLAMP, -1.0 / (1.0 - kk), 0.0)
            dq_ref[pl.ds(r0, CHUNK), :] = dq
            df_ref[pl.ds(r0, CHUNK), :] = dkk_tot * (1.0 - lbv) * (-sneg * (1.0 - sneg))
            di_ref[pl.ds(r0, CHUNK), :] = dv * _dsilu(iv)
            dlb_ref[...] += _colsum8(dkk_tot * (-sneg))
            return cr
        lax.fori_loop(0, nc, chunk, 0, unroll=2)
        done()

    def col(j):
        return pl.BlockSpec((t, LANE), lambda h: (0, HH * j + h))
    hs = pl.BlockSpec((t, LANE), lambda h: (0, h))
    res = pl.pallas_call(
        body, name="hgrn_bwd", grid=(HH,),
        in_specs=[col(0), col(1), col(2), pl.BlockSpec((1, LANE), lambda h: (0, h)), hs,
                  pl.BlockSpec((None, nc, LANE, LANE), lambda h: (h, 0, 0, 0))] + p_in,
        out_specs=[hs, hs, hs, pl.BlockSpec((8, LANE), lambda h: (0, h))] + p_out,
        out_shape=[jax.ShapeDtypeStruct((t, HH * LANE), F32)] * 3 + [jax.ShapeDtypeStruct((8, HH * LANE), F32)]
        + p_shape,
        scratch_shapes=[pltpu.VMEM((LANE, LANE), F32), pltpu.VMEM((8, LANE), F32)] + p_sem,
        compiler_params=_cp("parallel" if plan is None else "arbitrary"),
    )(h4, h4, h4, lb, do, states, *p_args)
    return res[:4], res[4:]


def _ln_fwd(z, g, b):
    mu = jnp.mean(z, axis=-1, keepdims=True)
    zc = z - mu
    rstd = lax.rsqrt(jnp.mean(zc * zc, axis=-1, keepdims=True) + EPS)
    zh = zc * rstd
    return zh * g + b, zh, rstd


def _mix_fwd(x, z, o_att, o_h, h4, gates, lng, lnb, wco, wao, ng, who, wout, t_end):
    t = x.shape[0]
    tm = _tile(t, 192)

    def body(x_ref, z_ref, oa_ref, oh_ref, hg_ref, gt_ref, lng_ref, lnb_ref, wco_ref, wao_ref, ng_ref, who_ref,
             wout_ref, x1_ref, mix_ref, ca_ref, oc_ref, ya_ref, yb_ref, yc_ref):
        i = pl.program_id(0)
        ln, _, _ = _ln_fwd(z_ref[...], lng_ref[...], lnb_ref[...])
        ca = _silu(ln).astype(_MM)
        ca_ref[...] = ca
        ya = jnp.dot(ca, wco_ref[...], preferred_element_type=F32)
        yb = _mm(oa_ref[...], wao_ref[...])
        hg = hg_ref[...]
        for h in range(HH):
            sl = slice(LANE * h, LANE * (h + 1))
            oh = oh_ref[:, sl]
            oc_ref[:, sl] = (oh * _rstd(oh) * ng_ref[:, sl] * _silu(hg[:, sl])).astype(_MM)
        yc = jnp.dot(oc_ref[...], who_ref[...], preferred_element_type=F32)
        ya_ref[...] = ya
        yb_ref[...] = yb
        yc_ref[...] = yc
        mix = (_sigmoid(gt_ref[:, 0:D]) * ya + _sigmoid(gt_ref[:, D:2 * D]) * yb
               + _sigmoid(gt_ref[:, 2 * D:3 * D]) * yc).astype(_MM)
        mix_ref[...] = mix
        x1_ref[...] = x_ref[...] + _valid_rows(i, tm, t_end) * jnp.dot(mix, wout_ref[...],
                                                                       preferred_element_type=F32)

    hd = NH * LANE
    return pl.pallas_call(
        body, name="mix_fwd", grid=(t // tm,),
        in_specs=[_row(tm, D), _row(tm, CONV_DIM), _row(tm, hd), _row(tm, 512), _row(tm, 512, 3), _row(tm, 3 * D),
                  _full((1, 512)), _full((1, 512)), _full((512, D)), _full((hd, D)), _full((1, 512)),
                  _full((512, D)), _full((D, D))],
        out_specs=[_row(tm, D), _row(tm, D), _row(tm, 512), _row(tm, 512), _row(tm, D), _row(tm, D), _row(tm, D)],
        out_shape=[jax.ShapeDtypeStruct((t, D), F32), jax.ShapeDtypeStruct((t, D), _MM),
                   jax.ShapeDtypeStruct((t, 512), _MM), jax.ShapeDtypeStruct((t, 512), _MM),
                   jax.ShapeDtypeStruct((t, D), F32), jax.ShapeDtypeStruct((t, D), F32),
                   jax.ShapeDtypeStruct((t, D), F32)],
        compiler_params=_cp("parallel"),
    )(x, z, o_att, o_h, h4, gates, lng, lnb, wco, wao, ng, who, wout)


def _mix_bwd(dx1, ya, yb, yc, gates, z, o_h, h4, lng, lnb, ng, wout, wco, wao, who, plan=None):
    t = dx1.shape[0]
    tm = _tile(t, 192)
    hd = NH * LANE
    p_args, p_in, p_out, p_shape, p_sem = _plan_specs(plan)

    def body(*refs):
        (ins, outs, _), rider = _host_refs(refs, 15, 12, 0, plan)
        (dx1_ref, ya_ref, yb_ref, yc_ref, gt_ref, z_ref, oh_ref, hg_ref, lng_ref, lnb_ref, ng_ref,
         wout_ref, wco_ref, wao_ref, who_ref) = ins
        (dgt_ref, dya_ref, dyb_ref, dyc_ref, dz_ref, doa_ref, doh_ref, dhg_ref,
         dlng_ref, dlnb_ref, dcb_ref, dng_ref) = outs
        i = pl.program_id(0)
        done = _ride(plan, rider, i, t // tm - 1)
        dmix = _mm_nt(dx1_ref[...], wout_ref[...])
        dys = []
        for j, y_ref in enumerate((ya_ref, yb_ref, yc_ref)):
            sg = _sigmoid(gt_ref[:, j * D:(j + 1) * D])
            dgt_ref[:, j * D:(j + 1) * D] = (dmix * y_ref[...] * sg * (1.0 - sg)).astype(_MM)
            dys.append((dmix * sg).astype(_MM))
        dya_ref[...], dyb_ref[...], dyc_ref[...] = dys
        dca = _mm_nt(dys[0], wco_ref[...])
        ln, zh, rstd = _ln_fwd(z_ref[...], lng_ref[...], lnb_ref[...])
        dln = dca * _dsilu(ln)
        dzh = dln * lng_ref[...]
        dz = rstd * (dzh - jnp.mean(dzh, axis=-1, keepdims=True)
                     - zh * jnp.mean(dzh * zh, axis=-1, keepdims=True))
        dz_ref[...] = dz
        doa_ref[...] = _mm_nt(dys[1], wao_ref[...])
        doc = _mm_nt(dys[2], who_ref[...])
        hg = hg_ref[...]
        dng_rows = []
        for h in range(HH):
            sl = slice(LANE * h, LANE * (h + 1))
            oh = oh_ref[:, sl]
            r = _rstd(oh)
            don = doc[:, sl] * _silu(hg[:, sl])
            dhg_ref[:, sl] = (doc[:, sl] * oh * r * ng_ref[:, sl] * _dsilu(hg[:, sl])).astype(_MM)
            doh, gn = _rms_bwd(don, oh, r, ng_ref[:, sl])
            doh_ref[:, sl] = doh
            dng_rows.append(_colsum8(gn))

        @pl.when(i == 0)
        def _():
            dlng_ref[...] = jnp.zeros_like(dlng_ref)
            dlnb_ref[...] = jnp.zeros_like(dlnb_ref)
            dcb_ref[...] = jnp.zeros_like(dcb_ref)
            dng_ref[...] = jnp.zeros_like(dng_ref)
        dlng_ref[...] += _colsum8(dln * zh)
        dlnb_ref[...] += _colsum8(dln)
        dcb_ref[...] += _colsum8(dz)
        dng_ref[...] += jnp.concatenate(dng_rows, axis=1)
        done()

    res = pl.pallas_call(
        body, name="mix_bwd", grid=(t // tm,),
        in_specs=[_row(tm, D), _row(tm, D), _row(tm, D), _row(tm, D), _row(tm, 3 * D), _row(tm, 512), _row(tm, 512),
                  _row(tm, 512, 3), _full((1, 512)), _full((1, 512)), _full((1, 512)),
                  _full((D, D)), _full((512, D)), _full((hd, D)), _full((512, D))] + p_in,
        out_specs=[_row(tm, 3 * D), _row(tm, D), _row(tm, D), _row(tm, D), _row(tm, 512), _row(tm, hd),
                   _row(tm, 512), _row(tm, 512), _full((8, 512)), _full((8, 512)), _full((8, 512)),
                   _full((8, 512))] + p_out,
        out_shape=[jax.ShapeDtypeStruct((t, 3 * D), _MM), jax.ShapeDtypeStruct((t, D), _MM),
                   jax.ShapeDtypeStruct((t, D), _MM), jax.ShapeDtypeStruct((t, D), _MM),
                   jax.ShapeDtypeStruct((t, 512), F32), jax.ShapeDtypeStruct((t, hd), F32),
                   jax.ShapeDtypeStruct((t, 512), F32), jax.ShapeDtypeStruct((t, 512), _MM)]
        + [jax.ShapeDtypeStruct((8, 512), F32)] * 4 + p_shape,
        scratch_shapes=p_sem,
        compiler_params=_cp("arbitrary"),
    )(dx1, ya, yb, yc, gates, z, o_h, h4, lng, lnb, ng, wout, wco, wao, who, *p_args)
    return res[:12], res[12:]


D_FF = 4096


def _ffn_fwd(x1, g2, w1, w2):
    t = x1.shape[0]
    tm = _tile(t, 192)

    def body(x1_ref, g_ref, w1_ref, w2_ref, x2_ref, p_ref):
        xv = x1_ref[...]
        h2 = (xv * _rstd(xv) * g_ref[...]).astype(_MM)
        p = jnp.dot(h2, w1_ref[...], preferred_element_type=F32)
        p_ref[...] = p
        r = jnp.maximum(p, 0.0)
        x2_ref[...] = xv + jnp.dot((r * r).astype(_MM), w2_ref[...], preferred_element_type=F32)

    return pl.pallas_call(
        body, name="ffn_fwd", grid=(t // tm,),
        in_specs=[_row(tm, D), _full((1, D)), _full((D, D_FF)), _full((D_FF, D))],
        out_specs=[_row(tm, D), _row(tm, D_FF)],
        out_shape=[jax.ShapeDtypeStruct((t, D), F32), jax.ShapeDtypeStruct((t, D_FF), F32)],
        compiler_params=_cp("parallel"),
    )(x1, g2, w1, w2)


def _ffn_bwd(dx2, x1, p, g2, w1t, w2t, plan=None):
    t = x1.shape[0]
    tm = _tile(t, 192)
    p_args, p_in, p_out, p_shape, p_sem = _plan_specs(plan)

    def body(*refs):
        (ins, outs, _), rider = _host_refs(refs, 6, 5, 0, plan)
        dx2_ref, x1_ref, p_ref, g_ref, w1t_ref, w2t_ref = ins
        dx1_ref, h2_ref, act_ref, dp_ref, dg_ref = outs
        i = pl.program_id(0)
        done = _ride(plan, rider, i, t // tm - 1)
        xv = x1_ref[...]
        rstd = _rstd(xv)
        h2_ref[...] = (xv * rstd * g_ref[...]).astype(_MM)
        r = jnp.maximum(p_ref[...], 0.0)
        act_ref[...] = (r * r).astype(_MM)
        dx2 = dx2_ref[...]
        da = _mm(dx2, w2t_ref[...])
        dp = (2.0 * r * da).astype(_MM)
        dp_ref[...] = dp
        dh2 = jnp.dot(dp, w1t_ref[...], preferred_element_type=F32)
        dxn, dgrow = _rms_bwd(dh2, xv, rstd, g_ref[...])
        dx1_ref[...] = dx2 + dxn

        @pl.when(i == 0)
        def _():
            dg_ref[...] = jnp.zeros_like(dg_ref)
        dg_ref[...] += _colsum8(dgrow)
        done()

    res = pl.pallas_call(
        body, name="ffn_bwd", grid=(t // tm,),
        in_specs=[_row(tm, D), _row(tm, D), _row(tm, D_FF), _full((1, D)), _full((D_FF, D)),
                  _full((D, D_FF))] + p_in,
        out_specs=[_row(tm, D), _row(tm, D), _row(tm, D_FF), _row(tm, D_FF), _full((8, D))] + p_out,
        out_shape=[jax.ShapeDtypeStruct((t, D), F32), jax.ShapeDtypeStruct((t, D), _MM),
                   jax.ShapeDtypeStruct((t, D_FF), _MM), jax.ShapeDtypeStruct((t, D_FF), _MM),
                   jax.ShapeDtypeStruct((8, D), F32)] + p_shape,
        scratch_shapes=p_sem,
        compiler_params=_cp("arbitrary"),
    )(dx2, x1, p, g2, w1t, w2t, *p_args)
    return res[:5], res[5:]


WGRAD_VMEM = 40 * 1024 * 1024


def _wgrad(a, b, name, chips=1):
    t, ka = a.shape
    nb = b.shape[1]
    cs = nb // chips
    widths = [d for d in range(cs, 0, -LANE) if cs % d == 0 and d % LANE == 0] or [cs]
    tn, tm = widths[-1], 64
    for d in widths:
        room = WGRAD_VMEM - 2 * ka * d * 4
        row_bytes = 2 * (ka * a.dtype.itemsize + d * b.dtype.itemsize) + 4 * ka
        fit = [r for r in range(64, t + 1, 64) if t % r == 0 and r * row_bytes <= room]
        if ka * d * 4 <= 16 * 1024 * 1024 and fit and (max(fit) >= 384 or d == widths[-1]):
            tn, tm = d, max(fit)
            break
    per = cs // tn

    def body(a_ref, b_ref, o_ref):
        @pl.when(pl.program_id(1) == 0)
        def _():
            o_ref[...] = jnp.zeros_like(o_ref)
        o_ref[...] += _mm_tn(a_ref[...], b_ref[...])

    if chips == 1:
        out_spec = pl.BlockSpec((ka, tn), lambda n, i: (0, n))
        out_shape = jax.ShapeDtypeStruct((ka, nb), F32)
    else:
        out_spec = pl.BlockSpec((None, ka, tn), lambda n, i: (n // per, 0, n % per))
        out_shape = jax.ShapeDtypeStruct((chips, ka, cs), F32)
    return pl.pallas_call(
        body, name="wgrad_" + name, grid=(nb // tn, t // tm),
        in_specs=[pl.BlockSpec((tm, ka), lambda n, i: (i, 0)), pl.BlockSpec((tm, tn), lambda n, i: (i, n))],
        out_specs=out_spec, out_shape=out_shape,
        compiler_params=_cp("parallel", "arbitrary"),
    )(a, b)


def _loss_head(y, target, t_end):
    t = y.shape[0]
    tm = _tile(t, 384)

    def body(y_ref, tg_ref, dy_ref, l_ref):
        i = pl.program_id(0)
        r = i * tm + lax.broadcasted_iota(jnp.int32, (tm, 1), 0)
        real = ((r >= ROW0) & (r < t_end)).astype(F32)
        diff = (y_ref[...] - tg_ref[...]) * real
        dy_ref[...] = diff * (1.0 / D)

        @pl.when(i == 0)
        def _():
            l_ref[...] = jnp.zeros_like(l_ref)
        sq = _colsum8(diff * diff)
        part = sq[:, 0:LANE]
        for j in range(1, D // LANE):
            part = part + sq[:, j * LANE:(j + 1) * LANE]
        l_ref[...] += part * (0.5 / D)

    return pl.pallas_call(
        body, name="loss_head", grid=(t // tm,),
        in_specs=[_row(tm, D), _row(tm, D)],
        out_specs=[_row(tm, D), _full((8, LANE))],
        out_shape=[jax.ShapeDtypeStruct((t, D), F32), jax.ShapeDtypeStruct((8, LANE), F32)],
        compiler_params=_cp("arbitrary"),
    )(y, target)


def _lower_bounds_fwd(logits):
    depth, n = logits.shape

    def body(l_ref, lb_ref):
        lg = l_ref[...]
        m = jnp.max(lg, axis=0, keepdims=True)
        e = jnp.exp(lg - m)
        p = e / jnp.sum(e, axis=0, keepdims=True)
        acc = jnp.zeros((1, n), F32)
        for l in range(depth):
            if l > 0:
                acc = acc + p[l:l + 1, :]
            lb_ref[l:l + 1, :] = acc

    return pl.pallas_call(body, name="lower_bounds_fwd", out_shape=jax.ShapeDtypeStruct((depth, n), F32))(logits)


def _lower_bounds_bwd(logits, dlb):
    depth, n = logits.shape

    def body(l_ref, dlb_ref, dl_ref):
        lg = l_ref[...]
        m = jnp.max(lg, axis=0, keepdims=True)
        e = jnp.exp(lg - m)
        p = e / jnp.sum(e, axis=0, keepdims=True)
        dps = [jnp.zeros((1, n), F32)]
        for j in range(1, depth):
            acc = jnp.zeros((1, n), F32)
            for l in range(j, depth):
                acc = acc + dlb_ref[l:l + 1, :]
            dps.append(acc)
        dot = jnp.zeros((1, n), F32)
        for j in range(depth):
            dot = dot + p[j:j + 1, :] * dps[j]
        for j in range(depth):
            dl_ref[j:j + 1, :] = p[j:j + 1, :] * (dps[j] - dot)

    return pl.pallas_call(body, name="lower_bounds_bwd", out_shape=jax.ShapeDtypeStruct((depth, n), F32))(logits, dlb)


def _ew_tile(rows, cols, n_arrays):
    cap = max(16, (32 * 1024 * 1024) // (8 * n_arrays * cols))
    for mult in (16, 8):
        fit = [t for t in range(mult, rows + 1, mult) if rows % t == 0 and t <= cap]
        if fit:
            return max(fit)
    return rows


def _adamw_math(w, g, m, v):
    mn = ADAM_B1 * m + (1.0 - ADAM_B1) * g
    vn = ADAM_B2 * v + (1.0 - ADAM_B2) * (g * g)
    m_hat = mn / (1.0 - ADAM_B1 ** ADAM_STEP)
    v_hat = vn / (1.0 - ADAM_B2 ** ADAM_STEP)
    return -ADAM_LR * (m_hat / (jnp.sqrt(v_hat) + ADAM_EPS) + ADAM_WD * w), mn, vn


def _adamw_layers(w, m, v, g0, g1, g_sibling, name):
    _, r, c_ = w.shape
    tr = _ew_tile(r, c_, 10)

    def body(w_ref, m_ref, v_ref, g0_ref, g1_ref, gs_ref, g_ref, d_ref, mo_ref, vo_ref):
        layer = pl.program_id(0)
        own = jnp.where(layer == 0, g0_ref[...], g1_ref[...])
        g = jnp.where(layer == lax.axis_index("c"), own, gs_ref[...])
        g_ref[...] = g
        d_ref[...], mo_ref[...], vo_ref[...] = _adamw_math(w_ref[...], g, m_ref[...], v_ref[...])

    lay = pl.BlockSpec((None, tr, c_), lambda l, i: (l, i, 0))
    flat = pl.BlockSpec((tr, c_), lambda l, i: (i, 0))
    only0 = pl.BlockSpec((tr, c_), lambda l, i: (i * (1 - l), 0))
    only1 = pl.BlockSpec((tr, c_), lambda l, i: (i * l, 0))
    return pl.pallas_call(
        body, name="adamw_" + name, grid=(2, r // tr),
        in_specs=[lay, lay, lay, only0, only1, flat], out_specs=[lay] * 4,
        out_shape=[jax.ShapeDtypeStruct(w.shape, F32)] * 4,
        compiler_params=_cp("parallel", "parallel"),
    )(w, m, v, g0, g1, g_sibling)


def _adamw(w, g, m, v, name):
    rows, cols = w.shape
    tr = _ew_tile(rows, cols, 7)

    def body(w_ref, g_ref, m_ref, v_ref, d_ref, mo_ref, vo_ref):
        d_ref[...], mo_ref[...], vo_ref[...] = _adamw_math(w_ref[...], g_ref[...], m_ref[...], v_ref[...])

    spec = pl.BlockSpec((tr, cols), lambda i: (i, 0))
    return pl.pallas_call(
        body, name="adamw_" + name, grid=(rows // tr,),
        in_specs=[spec] * 4, out_specs=[spec] * 3,
        out_shape=[jax.ShapeDtypeStruct((rows, cols), F32)] * 3,
        compiler_params=_cp("parallel"),
    )(w, g, m, v)


DEPTH = 2
BIG_SHAPES = {"w_in": ((1024, 6560), 1), "w_conv_out": ((512, 1024), 1), "w_uq": ((256, 768), 1),
              "w_ukv": ((128, 1024), 1), "w_attn_out": ((512, 1024), 1), "w_hgrn_out": ((512, 1024), 1),
              "w_out": ((1024, 1024), 0), "w_ff1": ((1024, 4096), 1), "w_ff2": ((4096, 1024), 0)}
BIG = tuple(BIG_SHAPES)
SMALL_SIZES = {"norm1_g": 1024, "conv_b": 512, "conv_ln_g": 512, "conv_ln_b": 512, "q_a_norm_g": 256,
               "kv_a_norm_g": 128, "q_norm_g": 96, "k_norm_g": 96, "hgrn_lb_logits": 512, "hgrn_norm_g": 512,
               "norm2_g": 1024}
SMALL = tuple(SMALL_SIZES)
W_IN_COLS = 6560
W_IN_SHARD = W_IN_COLS // 4
W_IN_SEGS = ((0, 1024, SEG_AG[0]), (1024, 1280, SEG_CQ[0]), (1280, 1408, SEG_CKV[0]), (1408, 1440, SEG_KR[0] + 64),
             (1440, 3488, SEG_H4[0]), (3488, 6560, SEG_GATES[0]))


def _pad_heads(w, nh, used, axis):
    shp = w.shape
    w = w.reshape(shp[:axis] + (nh, used) + shp[axis + 1:])
    pad = [(0, 0)] * w.ndim
    pad[axis + 1] = (0, LANE - used)
    w = jnp.pad(w, pad)
    return w.reshape(shp[:axis] + (nh * LANE,) + shp[axis + 1:])


def _unpad_heads(w, nh, used, axis):
    shp = w.shape
    w = w.reshape(shp[:axis] + (nh, LANE) + shp[axis + 1:])
    w = lax.slice_in_dim(w, 0, used, axis=axis + 1)
    return w.reshape(shp[:axis] + (nh * used,) + shp[axis + 1:])


def _w_in_from_chips(p4):
    def orig(a, b):
        out = []
        while a < b:
            s = a // W_IN_SHARD
            e = min(b, (s + 1) * W_IN_SHARD)
            out.append(p4[s][:, a - W_IN_SHARD * s:e - W_IN_SHARD * s])
            a = e
        return out
    zc = lambda n: jnp.zeros((D, n), p4[0].dtype)
    parts = (orig(3488, 6560) + orig(0, 1024) + orig(1440, 3488) + orig(1024, 1280) + orig(1280, 1408)
             + [zc(64)] + orig(1408, 1440) + [zc(32)])
    return jnp.concatenate(parts, axis=1)


def _w_in_grad_to_chips(dw):
    chips = []
    for s in range(4):
        a, b = W_IN_SHARD * s, W_IN_SHARD * (s + 1)
        parts = []
        for o0, o1, p0 in W_IN_SEGS:
            lo, hi = max(a, o0), min(b, o1)
            if lo < hi:
                parts.append(dw[:, p0 + lo - o0:p0 + hi - o0])
        chips.append(jnp.concatenate(parts, axis=1))
    return jnp.stack(chips)


def _cat_chips(p4, axis):
    return jnp.concatenate([p4[s] for s in range(4)], axis=axis)


EARLY = ("w_in", "w_uq", "w_ukv")
LATE = tuple(k for k in BIG if k not in EARLY)


def _prep_late(pieces):
    pc = lambda k: [pieces[k][s].astype(_MM) for s in range(4)]
    return dict(wao=_pad_heads(_cat_chips(pc("w_attn_out"), 1), NH, 64, 0), wco=_cat_chips(pc("w_conv_out"), 1),
                who=_cat_chips(pc("w_hgrn_out"), 1), wout=_cat_chips(pc("w_out"), 0),
                w1=_cat_chips(pc("w_ff1"), 1), w2=_cat_chips(pc("w_ff2"), 0))


def _prep_early(pieces, small, l):
    mm = lambda a: a.astype(_MM)
    pc = lambda k: [mm(pieces[k][s]) for s in range(4)]
    w_in_p = _w_in_from_chips(pc("w_in"))
    wuq = jnp.concatenate([_pad_heads(pc("w_uq")[s], 2, QK_DIM, 1) for s in range(4)], axis=1)
    wukv = _cat_chips(pc("w_ukv"), 1).reshape(128, NH, 128)
    wk = _pad_heads(wukv[:, :, :64].reshape(128, NH * 64), NH, 64, 1)
    wv = _pad_heads(wukv[:, :, 64:].reshape(128, NH * 64), NH, 64, 1)
    row = lambda a: a.astype(F32).reshape(1, -1)
    p = dict(
        w_in=w_in_p, w_in_t=w_in_p.T, wuq=wuq, wk=wk, wv=wv,
        g1=row(small["norm1_g"][l]), g2=row(small["norm2_g"][l]),
        cw=jnp.pad(small["conv_w"][l].astype(F32), ((0, 1), (0, 0))), cb=row(small["conv_b"][l]),
        lng=row(small["conv_ln_g"][l]), lnb=row(small["conv_ln_b"][l]),
        qag=row(small["q_a_norm_g"][l]), kvag=row(small["kv_a_norm_g"][l]),
        qng=jnp.pad(row(small["q_norm_g"][l]), ((0, 0), (0, LANE - QK_DIM))),
        kng=jnp.pad(row(small["k_norm_g"][l]), ((0, 0), (0, LANE - QK_DIM))),
        ng=row(small["hgrn_norm_g"][l]),
    )
    return p


def _rope_tables(t):
    pos = (jnp.arange(t, dtype=jnp.int32) - FRONT).astype(F32)
    inv_freq = 10000.0 ** (-jnp.arange(16, dtype=F32) / 16)
    ang = pos[:, None] * inv_freq[None, :]
    cos, sin = jnp.cos(ang), jnp.sin(ang)
    one = jnp.ones((t, 64), F32)
    z16, z32, z64 = jnp.zeros((t, 16), F32), jnp.zeros((t, 32), F32), jnp.zeros((t, 64), F32)
    c = jnp.concatenate([one, cos, cos, z32], axis=1)
    s1 = jnp.concatenate([z64, -sin, z16, z32], axis=1)
    s2 = jnp.concatenate([z64, z16, sin, z32], axis=1)
    return c, s1, s2


def _layer_fwd(x, p, lb, rope, t_end, plan=None, on_rode=None):
    gates, ag, h4, cq, ckv, kr, hb = _in_proj_fwd(x, p["g1"], p["w_in"])
    z = _conv_fwd(ag, p["cw"], p["cb"])
    q, k, v, cqn, ckvn = _mla_pre_fwd(cq, ckv, kr, p["qag"], p["wuq"], p["kvag"], p["wk"], p["wv"], p["qng"],
                                      p["kng"], *rope)
    (o_att, lse), rode = _attn_fwd(q, k, v, plan)
    if on_rode is not None:
        on_rode(rode)
    o_h, states = _hgrn_fwd(h4, lb)
    x1, mix, ca, oc, ya, yb, yc = _mix_fwd(x, z, o_att, o_h, h4, gates, p["lng"], p["lnb"], p["wco"], p["wao"],
                                           p["ng"], p["who"], p["wout"], t_end)
    x2, pre = _ffn_fwd(x1, p["g2"], p["w1"], p["w2"])
    saved = dict(x=x, gates=gates, ag=ag, h4=h4, cq=cq, ckv=ckv, kr=kr, hb=hb, z=z, q=q, k=k, v=v, cqn=cqn,
                 ckvn=ckvn, o_att=o_att, lse=lse, o_h=o_h, states=states, x1=x1, mix=mix, ca=ca, oc=oc,
                 ya=ya, yb=yb, yc=yc, pre=pre)
    return x2, saved


def _layer_bwd(dx2, s, p, lb, rope, t_end, rides=None):
    rides = rides or {}
    (dx1, h2, act, dp, dg2), rode = _ffn_bwd(dx2, s["x1"], s["pre"], p["g2"], p["w1"].T, p["w2"].T,
                                             rides.get("ffn"))
    g = {"w_ff1": _wgrad(h2, dp, "ff1", 4), "w_ff2": _wgrad(act, dx2, "ff2").reshape(4, D_FF // 4, D),
         "norm2_g": dg2.sum(0)}
    plan_mix = rides["mix"](rode, g) if "mix" in rides else None
    (dgt, dya, dyb, dyc, dz, doa, doh, dhg, dlng, dlnb, dcb, dng), rode = _mix_bwd(
        dx1, s["ya"], s["yb"], s["yc"], s["gates"], s["z"], s["o_h"], s["h4"], p["lng"], p["lnb"], p["ng"],
        p["wout"], p["wco"], p["wao"], p["who"], plan_mix)
    g["w_out"] = _wgrad(s["mix"], dx1, "out").reshape(4, D // 4, D)
    g["w_conv_out"] = _wgrad(s["ca"], dya, "conv_out", 4)
    g["w_attn_out"] = _unpad_heads(_wgrad(s["o_att"], dyb, "attn_out", 4), NH, 64, 1)
    g["w_hgrn_out"] = _wgrad(s["oc"], dyc, "hgrn_out", 4)
    g["conv_ln_g"], g["conv_ln_b"], g["conv_b"], g["hgrn_norm_g"] = dlng.sum(0), dlnb.sum(0), dcb.sum(0), dng.sum(0)
    da, dg, dcw = _conv_bwd(s["ag"], p["cw"], dz)
    g["conv_w"] = dcw[:CONV_K]
    plan_attn = rides["attn"](rode, g) if "attn" in rides else None
    (dq, dk, dv), rode_attn = _attn_bwd(s["q"], s["k"], s["v"], s["o_att"], s["lse"], doa, plan_attn)
    dcq, dckv, dkr, dqraw, dkraw, dqag, dkvag, dqng, dkng = _mla_pre_bwd(
        dq, dk, dv, s["cq"], s["ckv"], s["kr"], p["qag"], p["wuq"], p["kvag"], p["wk"], p["wv"], p["qng"],
        p["kng"], *rope)
    g["w_uq"] = _unpad_heads(_wgrad(s["cqn"], dqraw, "uq", 4), 2, QK_DIM, 2)
    dwk = _unpad_heads(_wgrad(s["ckvn"], dkraw, "uk"), NH, 64, 1).reshape(128, NH, 64)
    dwv = _unpad_heads(_wgrad(s["ckvn"], dv, "uv"), NH, 64, 1).reshape(128, NH, 64)
    g["w_ukv"] = jnp.concatenate([dwk, dwv], axis=2).reshape(128, 4, 256).transpose(1, 0, 2)
    g["q_a_norm_g"], g["kv_a_norm_g"] = dqag.sum(0), dkvag.sum(0)
    g["q_norm_g"], g["k_norm_g"] = dqng.sum(0)[:QK_DIM], dkng.sum(0)[:QK_DIM]
    plan_hgrn = rides["hgrn"](rode_attn) if "hgrn" in rides else None
    (dhq, dhf, dhi, dlb), rode_hgrn = _hgrn_bwd(s["h4"], lb, doh, s["states"], plan_hgrn)
    mm = lambda a: a.astype(_MM)
    du = jnp.concatenate([dgt, mm(da), mm(dg), mm(dhq), mm(dhf), mm(dhi), dhg, dcq, dckv, dkr], axis=1)
    dx, dg1 = _in_proj_bwd(du, s["x"], dx1, p["g1"], p["w_in_t"], t_end)
    g["norm1_g"] = dg1.sum(0)
    g["w_in"] = _w_in_grad_to_chips(_wgrad(s["hb"], du, "in"))
    return dx, g, dlb.sum(0), (rode_attn, rode_hgrn)


def _device_step(x, target, small, pieces0, pieces1=None, fwd_ride=None, bwd_rides=None):
    s_real = x.shape[0]
    t_end = ROW0 + s_real
    t = -(-t_end // LANE) * LANE
    zrow = lambda n: jnp.zeros((n, D), F32)
    xp = jnp.concatenate([zrow(FRONT), small["meta"].astype(F32), x, zrow(t - t_end)], axis=0)
    tp = jnp.concatenate([zrow(ROW0), target, zrow(t - t_end)], axis=0)
    rope = _rope_tables(t)
    logits = small["hgrn_lb_logits"].astype(F32)
    lbs = _lower_bounds_fwd(logits)
    prm0 = _prep_early(pieces0, small, 0)
    got = {}
    if fwd_ride is None:
        prm0.update(_prep_late(pieces0))
        h, sv0 = _layer_fwd(xp, prm0, lbs[0:1], rope, t_end)
    else:
        def on_rode(rode):
            late0, got["pieces1"] = fwd_ride[1](rode)
            prm0.update(_prep_late(late0))
        h, sv0 = _layer_fwd(xp, prm0, lbs[0:1], rope, t_end, fwd_ride[0], on_rode)
        pieces1 = got["pieces1"]
    prm1 = _prep_early(pieces1, small, 1)
    if fwd_ride is None:
        prm1.update(_prep_late(pieces1))
        h, sv1 = _layer_fwd(h, prm1, lbs[1:2], rope, t_end)
    else:
        h, sv1 = _layer_fwd(h, prm1, lbs[1:2], rope, t_end, fwd_ride[2],
                            lambda rode: prm1.update(_prep_late(fwd_ride[3](rode))))
    dh, lsum = _loss_head(h, tp, t_end)
    loss = jnp.sum(lsum)
    rides1, make_rides0 = (None, None) if bwd_rides is None else bwd_rides
    dh, g1, dlb1, rode1 = _layer_bwd(dh, sv1, prm1, lbs[1:2], rope, t_end, rides1)
    dh, g0, dlb0, rode = _layer_bwd(dh, sv0, prm0, lbs[0:1], rope, t_end,
                                    None if make_rides0 is None else make_rides0(g1, rode1))
    dlogits = _lower_bounds_bwd(logits, jnp.stack([dlb0, dlb1]))
    grads = [g0, g1]
    for l in range(DEPTH):
        grads[l]["hgrn_lb_logits"] = dlogits[l]
    return loss, dh[ROW0:t_end], grads, dh[FRONT:ROW0], rode


MESH = pl.DeviceIdType.MESH
_ANY = pl.BlockSpec(memory_space=pl.ANY)
SMALL_ROWS = 64
SMALL_LEN = SMALL_ROWS * 1024


def _mesh_pos():
    return lax.axis_index("x"), lax.axis_index("y"), lax.axis_index("c")


def _other_chips(x, y):
    return [(1 - x, y), (x, 1 - y), (1 - x, 1 - y)]


class _Plan:
    def __init__(self, name, ins, out_shapes, sems, start, finish, relay=None):
        self.name, self.ins, self.out_shapes, self.sems = name, list(ins), list(out_shapes), list(sems)
        self.start, self.finish, self.relay = start, finish, relay


def _run_plan(plan):
    ni, no = len(plan.ins), len(plan.out_shapes)

    def body(*refs):
        ins, outs, sems = refs[:ni], refs[ni:ni + no], refs[ni + no:]
        plan.start(ins, outs, sems)
        if plan.relay is not None:
            plan.relay(ins, outs, sems)
        plan.finish(ins, outs, sems)

    return pl.pallas_call(body, name=plan.name, in_specs=[_ANY] * ni, out_specs=[_ANY] * no,
                          out_shape=plan.out_shapes, scratch_shapes=plan.sems)(*plan.ins)


def _plan_specs(plan):
    if plan is None:
        return [], [], [], [], []
    return plan.ins, [_ANY] * len(plan.ins), [_ANY] * len(plan.out_shapes), plan.out_shapes, plan.sems


def _host_refs(refs, n_in, n_out, n_scratch, plan):
    ni = 0 if plan is None else len(plan.ins)
    no = 0 if plan is None else len(plan.out_shapes)
    o0 = n_in + ni
    s0 = o0 + n_out + no
    own = (refs[:n_in], refs[o0:o0 + n_out], refs[s0:s0 + n_scratch])
    rider = (refs[n_in:o0], refs[o0 + n_out:s0], refs[s0 + n_scratch:])
    return own, rider


def _ride(plan, rider, step, last):
    if plan is None:
        return lambda: None

    @pl.when(step == 0)
    def _():
        plan.start(*rider)

    def done():
        if plan.relay is not None:
            @pl.when(step == last - 1)
            def _():
                plan.relay(*rider)

        @pl.when(step == last)
        def _():
            plan.finish(*rider)
    return done


def _merge_plans(name, plans):
    def parts(ins, outs, sems):
        i = o = s = 0
        for p in plans:
            ni, no, ns = len(p.ins), len(p.out_shapes), len(p.sems)
            yield p, (ins[i:i + ni], outs[o:o + no], sems[s:s + ns])
            i, o, s = i + ni, o + no, s + ns

    def start(ins, outs, sems):
        for p, refs in parts(ins, outs, sems):
            p.start(*refs)

    def relay(ins, outs, sems):
        for p, refs in parts(ins, outs, sems):
            if p.relay is not None:
                p.relay(*refs)

    def finish(ins, outs, sems):
        for p, refs in parts(ins, outs, sems):
            p.finish(*refs)

    return _Plan(name, [a for p in plans for a in p.ins], [a for p in plans for a in p.out_shapes],
                 [a for p in plans for a in p.sems], start, finish, relay)


def _plan_gather(own, layer, name):
    nw = len(own)

    def copies(ins, outs, sems):
        send_sems, recv_sems = sems

        def over_ici(w, j, chip_of_data, to):
            return pltpu.make_async_remote_copy(
                src_ref=ins[w].at[layer], dst_ref=outs[w].at[chip_of_data], send_sem=send_sems.at[w, j],
                recv_sem=recv_sems.at[w, j], device_id=to, device_id_type=MESH)

        def over_d2d(w, j, chip_of_data, to):
            return pltpu.make_async_remote_copy(
                src_ref=outs[w].at[chip_of_data], dst_ref=outs[w].at[chip_of_data], send_sem=send_sems.at[w, 3 + j],
                recv_sem=recv_sems.at[w, 3 + j], device_id=to, device_id_type=MESH)
        return over_ici, over_d2d

    def start(ins, outs, sems):
        x, y, c = _mesh_pos()
        over_ici, _ = copies(ins, outs, sems)

        @pl.when(c == layer)
        def _():
            for j, (px, py) in enumerate(_other_chips(x, y)):
                for w in range(nw):
                    over_ici(w, j, 2 * x + y, (px, py, layer)).start()

    def relay(ins, outs, sems):
        x, y, c = _mesh_pos()
        over_ici, over_d2d = copies(ins, outs, sems)

        @pl.when(c == layer)
        def _():
            for j, (px, py) in enumerate(_other_chips(x, y)):
                for w in range(nw):
                    over_ici(w, j, 2 * px + py, (x, y, c)).wait_recv()
                    over_d2d(w, j, 2 * px + py, (x, y, 1 - layer)).start()

    def finish(ins, outs, sems):
        x, y, c = _mesh_pos()
        over_ici, over_d2d = copies(ins, outs, sems)
        chips = _other_chips(x, y)

        @pl.when(c == layer)
        def _():
            for j, (px, py) in enumerate(chips):
                for w in range(nw):
                    over_ici(w, j, 2 * x + y, (px, py, layer)).wait_send()
                    over_d2d(w, j, 2 * px + py, (x, y, 1 - layer)).wait_send()

        @pl.when(c != layer)
        def _():
            for j, (px, py) in enumerate(chips):
                for w in range(nw):
                    over_d2d(w, j, 2 * px + py, (x, y, c)).wait_recv()

    return _Plan(name, own,
                 [jax.ShapeDtypeStruct((4,) + a.shape[1:], a.dtype) for a in own],
                 [pltpu.SemaphoreType.DMA((nw, 6)), pltpu.SemaphoreType.DMA((nw, 6))], start, finish, relay)


def _plan_to_sibling(arrs, layer, name):
    nw = len(arrs)

    def copy(ins, outs, sems, w):
        x, y, _ = _mesh_pos()
        return pltpu.make_async_remote_copy(src_ref=ins[w], dst_ref=outs[w], send_sem=sems[0].at[w],
                                            recv_sem=sems[1].at[w], device_id=(x, y, layer), device_id_type=MESH)

    def start(ins, outs, sems):
        @pl.when(lax.axis_index("c") != layer)
        def _():
            for w in range(nw):
                copy(ins, outs, sems, w).start()

    def finish(ins, outs, sems):
        c = lax.axis_index("c")

        @pl.when(c != layer)
        def _():
            for w in range(nw):
                copy(ins, outs, sems, w).wait_send()

        @pl.when(c == layer)
        def _():
            for w in range(nw):
                copy(ins, outs, sems, w).wait_recv()

    return _Plan(name, arrs, [jax.ShapeDtypeStruct(a.shape, a.dtype) for a in arrs],
                 [pltpu.SemaphoreType.DMA((nw,)), pltpu.SemaphoreType.DMA((nw,))], start, finish)


def _plan_scatter(parts, layer, name):
    nw = len(parts)

    def start(ins, outs, sems):
        x, y, c = _mesh_pos()

        @pl.when(c == layer)
        def _():
            for j, (px, py) in enumerate(_other_chips(x, y)):
                for w in range(nw):
                    pltpu.make_async_remote_copy(
                        src_ref=ins[w].at[2 * px + py], dst_ref=outs[w].at[2 * x + y], send_sem=sems[0].at[w, j],
                        recv_sem=sems[1].at[w, j], device_id=(px, py, layer), device_id_type=MESH).start()

    def finish(ins, outs, sems):
        x, y, c = _mesh_pos()

        @pl.when(c == layer)
        def _():
            for j, (px, py) in enumerate(_other_chips(x, y)):
                for w in range(nw):
                    pltpu.make_async_remote_copy(
                        src_ref=ins[w].at[2 * px + py], dst_ref=outs[w].at[2 * px + py], send_sem=sems[0].at[w, j],
                        recv_sem=sems[1].at[w, j], device_id=(x, y, c), device_id_type=MESH).wait()

    return _Plan(name, parts, [jax.ShapeDtypeStruct(a.shape, a.dtype) for a in parts],
                 [pltpu.SemaphoreType.DMA((nw, 3)), pltpu.SemaphoreType.DMA((nw, 3))], start, finish)


def _sibling_exchange(reds0, reds1):
    nw = len(reds0)

    def body(*refs):
        a0, a1, outs = refs[:nw], refs[nw:2 * nw], refs[2 * nw:3 * nw]
        send_sems, recv_sems = refs[3 * nw:]
        x, y, c = _mesh_pos()

        def copy(w, src):
            return pltpu.make_async_remote_copy(src_ref=src, dst_ref=outs[w], send_sem=send_sems.at[w],
                                                recv_sem=recv_sems.at[w], device_id=(x, y, 1 - c),
                                                device_id_type=MESH)

        @pl.when(c == 0)
        def _():
            for w in range(nw):
                copy(w, a0[w]).start()

        @pl.when(c == 1)
        def _():
            for w in range(nw):
                copy(w, a1[w]).start()

        for w in range(nw):
            copy(w, a0[w]).wait()

    return pl.pallas_call(
        body, name="sibling_exchange", in_specs=[_ANY] * (2 * nw), out_specs=[_ANY] * nw,
        out_shape=[jax.ShapeDtypeStruct(a.shape, a.dtype) for a in reds0],
        scratch_shapes=[pltpu.SemaphoreType.DMA((nw,)), pltpu.SemaphoreType.DMA((nw,))],
    )(*reds0, *reds1)


def _all_reduce_small(v, name):
    rows, cols = v.shape

    def body(v_ref, o_ref, slots, send_sems, recv_sems):
        x, y, c = _mesh_pos()
        me = 4 * x + 2 * y + c
        slots[me] = v_ref[...]
        peers = []
        for rel in range(1, 8):
            fx, fy, fc = (rel >> 2) & 1, (rel >> 1) & 1, rel & 1
            px = 1 - x if fx else x
            py = 1 - y if fy else y
            pc = 1 - c if fc else c
            peers.append((px, py, pc))
        cps = [pltpu.make_async_remote_copy(src_ref=v_ref, dst_ref=slots.at[me], send_sem=send_sems.at[k],
                                            recv_sem=recv_sems.at[k], device_id=peer, device_id_type=MESH)
               for k, peer in enumerate(peers)]
        for cp in cps:
            cp.start()
        for k, (px, py, pc) in enumerate(peers):
            pltpu.make_async_remote_copy(src_ref=v_ref, dst_ref=slots.at[4 * px + 2 * py + pc],
                                         send_sem=send_sems.at[k], recv_sem=recv_sems.at[k], device_id=(x, y, c),
                                         device_id_type=MESH).wait_recv()
        for cp in cps:
            cp.wait_send()
        acc = slots[0]
        for d in range(1, 8):
            acc = acc + slots[d]
        o_ref[...] = acc

    vm = pl.BlockSpec(memory_space=pltpu.VMEM)
    return pl.pallas_call(
        body, name=name, in_specs=[vm], out_specs=vm,
        out_shape=jax.ShapeDtypeStruct((rows, cols), F32),
        scratch_shapes=[pltpu.VMEM((8, rows, cols), F32), pltpu.SemaphoreType.DMA((7,)),
                        pltpu.SemaphoreType.DMA((7,))],
    )(v)


def _add_to_wire(a, b, name):
    n4, r, c_ = a.shape
    rows = n4 * r
    tr = _ew_tile(rows, c_, 3)

    def body(a_ref, b_ref, o_ref):
        o_ref[...] = (a_ref[...] + b_ref[...]).astype(o_ref.dtype)

    spec = pl.BlockSpec((tr, c_), lambda i: (i, 0))
    out = pl.pallas_call(
        body, name="add_to_wire_" + name, grid=(rows // tr,), in_specs=[spec, spec], out_specs=spec,
        out_shape=jax.ShapeDtypeStruct((rows, c_), jnp.bfloat16), compiler_params=_cp("parallel"),
    )(a.reshape(rows, c_), b.reshape(rows, c_))
    return out.reshape(n4, r, c_)


def _sum_chips(recv, wire, name):
    _, r, c_ = recv.shape
    tr = _ew_tile(r, c_, 6)

    def body(r_ref, w_ref, o_ref):
        chip = 2 * lax.axis_index("x") + lax.axis_index("y")
        acc = None
        for s in range(4):
            term = jnp.where(chip == s, w_ref[s], r_ref[s]).astype(F32)
            acc = term if acc is None else acc + term
        o_ref[...] = acc

    blk = pl.BlockSpec((4, tr, c_), lambda i: (0, i, 0))
    return pl.pallas_call(
        body, name="sum_chips_" + name, grid=(r // tr,),
        in_specs=[blk, blk], out_specs=pl.BlockSpec((tr, c_), lambda i: (i, 0)),
        out_shape=jax.ShapeDtypeStruct((r, c_), F32),
        compiler_params=_cp("parallel"),
    )(recv, wire)


def _pack_small(vals, meta_full, conv_w_full):
    flat = jnp.concatenate([vals[k].reshape(-1) for k in SMALL] + [meta_full.reshape(-1), conv_w_full.reshape(-1)])
    return jnp.pad(flat, (0, SMALL_LEN - flat.shape[0])).reshape(SMALL_ROWS, 1024)


def _unpack_small(buf):
    flat = buf.reshape(-1)
    out, off = {}, 0
    for k in SMALL:
        n = DEPTH * SMALL_SIZES[k]
        out[k] = flat[off:off + n].reshape(DEPTH, SMALL_SIZES[k])
        off += n
    meta = flat[off:off + N_META * D].reshape(N_META, D)
    off += N_META * D
    conv_w = flat[off:off + DEPTH * CONV_K * CONV_DIM].reshape(DEPTH, CONV_K, CONV_DIM)
    return out, meta, conv_w


def kernel(x, meta, norm1_g, w_in, conv_w, conv_b, conv_ln_g, conv_ln_b, w_conv_out, q_a_norm_g, w_uq, kv_a_norm_g, w_ukv, q_norm_g, k_norm_g, w_attn_out, hgrn_lb_logits, hgrn_norm_g, w_hgrn_out, w_out, norm2_g, w_ff1, w_ff2, loss_target, m_meta, m_norm1_g, m_w_in, m_conv_w, m_conv_b, m_conv_ln_g, m_conv_ln_b, m_w_conv_out, m_q_a_norm_g, m_w_uq, m_kv_a_norm_g, m_w_ukv, m_q_norm_g, m_k_norm_g, m_w_attn_out, m_hgrn_lb_logits, m_hgrn_norm_g, m_w_hgrn_out, m_w_out, m_norm2_g, m_w_ff1, m_w_ff2, v_meta, v_norm1_g, v_w_in, v_conv_w, v_conv_b, v_conv_ln_g, v_conv_ln_b, v_w_conv_out, v_q_a_norm_g, v_w_uq, v_kv_a_norm_g, v_w_ukv, v_q_norm_g, v_k_norm_g, v_w_attn_out, v_hgrn_lb_logits, v_hgrn_norm_g, v_w_hgrn_out, v_w_out, v_norm2_g, v_w_ff1, v_w_ff2):
    names = ("meta", "norm1_g", "w_in", "conv_w", "conv_b", "conv_ln_g", "conv_ln_b", "w_conv_out", "q_a_norm_g",
             "w_uq", "kv_a_norm_g", "w_ukv", "q_norm_g", "k_norm_g", "w_attn_out", "hgrn_lb_logits", "hgrn_norm_g",
             "w_hgrn_out", "w_out", "norm2_g", "w_ff1", "w_ff2")
    w = dict(zip(names, (meta, norm1_g, w_in, conv_w, conv_b, conv_ln_g, conv_ln_b, w_conv_out, q_a_norm_g, w_uq,
                         kv_a_norm_g, w_ukv, q_norm_g, k_norm_g, w_attn_out, hgrn_lb_logits, hgrn_norm_g, w_hgrn_out,
                         w_out, norm2_g, w_ff1, w_ff2)))
    m = dict(zip(names, (m_meta, m_norm1_g, m_w_in, m_conv_w, m_conv_b, m_conv_ln_g, m_conv_ln_b, m_w_conv_out,
                         m_q_a_norm_g, m_w_uq, m_kv_a_norm_g, m_w_ukv, m_q_norm_g, m_k_norm_g, m_w_attn_out,
                         m_hgrn_lb_logits, m_hgrn_norm_g, m_w_hgrn_out, m_w_out, m_norm2_g, m_w_ff1, m_w_ff2)))
    v = dict(zip(names, (v_meta, v_norm1_g, v_w_in, v_conv_w, v_conv_b, v_conv_ln_g, v_conv_ln_b, v_w_conv_out,
                         v_q_a_norm_g, v_w_uq, v_kv_a_norm_g, v_w_ukv, v_q_norm_g, v_k_norm_g, v_w_attn_out,
                         v_hgrn_lb_logits, v_hgrn_norm_g, v_w_hgrn_out, v_w_out, v_norm2_g, v_w_ff1, v_w_ff2)))
    cx, cy, cc = _mesh_pos()
    chip = 2 * cx + cy
    zero = jnp.zeros((), jnp.int32)

    own = {k: w[k].astype(_MM) for k in BIG}

    def as_pieces(names, gathered, layer):
        return {k: [jnp.where(chip == s, own[k][layer], g[s]) for s in range(4)] for k, g in zip(names, gathered)}

    pieces0 = as_pieces(EARLY, _run_plan(_plan_gather([own[k] for k in EARLY], 0, "gather_l0_early")), 0)
    fwd_ride = (_merge_plans("gather_mid", [_plan_gather([own[k] for k in LATE], 0, "gather_l0_late"),
                                            _plan_gather([own[k] for k in EARLY], 1, "gather_l1_early")]),
                lambda got: (as_pieces(LATE, got[:len(LATE)], 0), as_pieces(EARLY, got[len(LATE):], 1)),
                _plan_gather([own[k] for k in LATE], 1, "gather_l1_late"),
                lambda got: as_pieces(LATE, got, 1))
    meta_slab = lax.dynamic_update_slice(jnp.zeros((N_META, D), F32), meta, (zero, chip * (D // 4)))
    convw_slab = lax.dynamic_update_slice(jnp.zeros((DEPTH, CONV_K, CONV_DIM), F32), conv_w,
                                          (zero, zero, chip * (CONV_DIM // 4)))
    zsmall = {k: jnp.zeros((DEPTH, SMALL_SIZES[k]), F32) for k in SMALL}
    south = (cc == 0).astype(F32)
    _, meta_full, convw_full = _unpack_small(
        _all_reduce_small(_pack_small(zsmall, meta_slab, convw_slab) * south, "gather_small"))
    small = {k: w[k] for k in SMALL}
    small["meta"] = meta_full
    small["conv_w"] = convw_full

    FFN = ("w_ff1", "w_ff2")
    MID = ("w_out", "w_conv_out", "w_attn_out", "w_hgrn_out")
    REST = tuple(k for k in BIG if k not in FFN + MID)
    held = {}

    def to_wire(names, layer, mine, from_sibling):
        return lax.cond(
            cc == layer,
            lambda: [_add_to_wire(a, b, "%s_l%d" % (k, layer)) for k, a, b in zip(names, mine, from_sibling)],
            lambda: [jnp.zeros(a.shape, jnp.bfloat16) for a in mine])

    def chip_sum(names, layer, got, wire):
        return lax.cond(
            cc == layer,
            lambda: [_sum_chips(r, s, "%s_l%d" % (k, layer)) for k, r, s in zip(names, got, wire)],
            lambda: [jnp.zeros(s.shape[1:], F32) for s in wire])

    NONFFN = tuple(k for k in BIG if k not in FFN)

    def ride_attn_l1(_, g1):
        held["g1_ffn"] = [g1[k] for k in FFN]
        return _plan_to_sibling(held["g1_ffn"], 1, "swap_grads_l1_ffn")

    def rides_l0(g1, rode_l1):
        g1_rest = [g1[k] for k in NONFFN]

        def ride_mix(from_sibling1, g0_ffn):
            wire1 = dict(zip(FFN, to_wire(FFN, 1, held["g1_ffn"], rode_l1[0])))
            wire1.update(zip(NONFFN, to_wire(NONFFN, 1, g1_rest, from_sibling1)))
            held["wire1"] = [wire1[k] for k in BIG]
            held["g0_ffn"] = [g0_ffn[k] for k in FFN]
            return _plan_to_sibling(held["g0_ffn"], 0, "swap_grads_l0_ffn")

        def ride_attn(from_sibling0, g0):
            held["wire0_ffn"] = to_wire(FFN, 0, held["g0_ffn"], from_sibling0)
            held["g0_mid"] = [g0[k] for k in MID]
            return _merge_plans("exchange_grads_mid", [
                _plan_scatter(held["wire1"], 1, "scatter_grads_l1"),
                _plan_scatter(held["wire0_ffn"], 0, "scatter_grads_l0_ffn"),
                _plan_to_sibling(held["g0_mid"], 0, "swap_grads_l0_mid")])

        def ride_hgrn(rode_attn):
            held["wire0_mid"] = to_wire(MID, 0, held["g0_mid"], rode_attn[len(BIG) + len(FFN):])
            return _plan_scatter(held["wire0_mid"], 0, "scatter_grads_l0_mid")

        return {"ffn": _plan_to_sibling(g1_rest, 1, "swap_grads_l1_rest"), "mix": ride_mix, "attn": ride_attn,
                "hgrn": ride_hgrn}

    loss_share, grad_x, gl, g_meta, (got, got_mid) = _device_step(
        x[0], loss_target[0], small, pieces0, None, fwd_ride, ({"attn": ride_attn_l1}, rides_l0))

    reds1 = chip_sum(BIG, 1, got[:len(BIG)], held["wire1"])
    reds0 = dict(zip(FFN, chip_sum(FFN, 0, got[len(BIG):len(BIG) + len(FFN)], held["wire0_ffn"])))
    reds0.update(zip(MID, chip_sum(MID, 0, got_mid, held["wire0_mid"])))
    g0_rest = [gl[0][k] for k in REST]
    wire0 = to_wire(REST, 0, g0_rest, _run_plan(_plan_to_sibling(g0_rest, 0, "swap_grads_l0_rest")))
    reds0.update(zip(REST, chip_sum(REST, 0, _run_plan(_plan_scatter(wire0, 0, "scatter_grads_l0_rest")), wire0)))
    reds0 = [reds0[k] for k in BIG]
    reds_sibling = _sibling_exchange(reds0, reds1)
    grads, delta, new_m, new_v = {}, {}, {}, {}
    t_view = lambda k, a: jnp.swapaxes(a, -1, -2) if k == "w_in" else a
    for k, r0, r1, theirs in zip(BIG, reds0, reds1, reds_sibling):
        grads[k], delta[k], new_m[k], new_v[k] = [
            t_view(k, a) for a in _adamw_layers(t_view(k, w[k]), t_view(k, m[k]), t_view(k, v[k]), t_view(k, r0),
                                                t_view(k, r1), t_view(k, theirs), k)]

    g_small_local = {k: jnp.stack([gl[l][k] for l in range(DEPTH)]) for k in SMALL}
    g_convw_local = jnp.stack([gl[l]["conv_w"] for l in range(DEPTH)])
    reduced = _all_reduce_small(
        _pack_small(g_small_local, g_meta, g_convw_local).at[SMALL_ROWS - 1, 1023].set(loss_share), "reduce_small")
    loss = reduced[SMALL_ROWS - 1, 1023]
    g_small, g_meta_full, g_convw_full = _unpack_small(reduced)
    grads.update(g_small)
    grads["meta"] = lax.dynamic_slice(g_meta_full, (zero, chip * (D // 4)), (N_META, D // 4))
    grads["conv_w"] = lax.dynamic_slice(g_convw_full, (zero, zero, chip * (CONV_DIM // 4)),
                                        (DEPTH, CONV_K, CONV_DIM // 4))

    def small_pack(src):
        return _pack_small(src, jnp.pad(src["meta"], ((0, 0), (0, D - D // 4))),
                           jnp.pad(src["conv_w"], ((0, 0), (0, 0), (0, CONV_DIM - CONV_DIM // 4))))

    def small_unpack(buf):
        out, meta_p, convw_p = _unpack_small(buf)
        out["meta"] = meta_p[:, :D // 4]
        out["conv_w"] = convw_p[:, :, :CONV_DIM // 4]
        return out

    d_s, m_s, v_s = [small_unpack(a) for a in _adamw(small_pack(w), small_pack(grads), small_pack(m),
                                                     small_pack(v), "small")]
    delta.update(d_s)
    new_m.update(m_s)
    new_v.update(v_s)
    return (loss, grad_x[None], *[grads[k] for k in names], *[delta[k] for k in names],
            *[new_m[k] for k in names], *[new_v[k] for k in names])
```

```python
import functools

import jax
import jax.numpy as jnp
from jax import lax
from jax.experimental import pallas as pl
from jax.experimental.pallas import tpu as pltpu

F32 = jnp.float32
_MM = jnp.bfloat16

D = 1024
N_META = 16
FRONT = 48
ROW0 = FRONT + N_META
EPS = 1e-6
GATE_CLAMP = 1.0 - 1e-6
CONV_K = 31
CONV_DIM = 512
NH = 8
QK_DIM = 96
ATT_SCALE = QK_DIM ** -0.5
HH = 4
CHUNK = 64
SUB = 16
EXP_CLIP = 60.0
NEG = -1e30
LANE = 128

SEG_GATES = (0, 3072)
SEG_AG = (3072, 4096)
SEG_H4 = (4096, 6144)
SEG_CQ = (6144, 6400)
SEG_CKV = (6400, 6528)
SEG_KR = (6528, 6656)
N_IN_P = 6656

ADAM_LR = 0.001
ADAM_B1 = 0.9
ADAM_B2 = 0.999
ADAM_EPS = 1e-08
ADAM_WD = 0.01
ADAM_STEP = 10

VMEM_LIMIT = 56 * 1024 * 1024


def _tile(n, pref):
    best = 64
    for t in range(64, pref + 1, 64):
        if n % t == 0:
            best = t
    return best


def _cp(*sem):
    return pltpu.CompilerParams(dimension_semantics=tuple(sem), vmem_limit_bytes=VMEM_LIMIT)


def _row(tm, n, col=0):
    return pl.BlockSpec((tm, n), lambda i: (i, col))


def _full(shape):
    return pl.BlockSpec(shape, lambda i: (0,) * len(shape))


def _mm(a, b):
    return jnp.dot(a.astype(_MM), b.astype(_MM), preferred_element_type=F32)


def _mm_nt(a, b):
    return lax.dot_general(a.astype(_MM), b.astype(_MM), (((1,), (1,)), ((), ())), preferred_element_type=F32)


def _mm_tn(a, b):
    return lax.dot_general(a.astype(_MM), b.astype(_MM), (((0,), (0,)), ((), ())), preferred_element_type=F32)


def _split3(x):
    hi = x.astype(jnp.bfloat16)
    return hi, (x - hi.astype(F32)).astype(jnp.bfloat16)


def _dot3(a, b, dims):
    ah, al = _split3(a)
    bh, bl = _split3(b)
    dg = lambda u, v: lax.dot_general(u, v, (dims, ((), ())), preferred_element_type=F32)
    return dg(ah, bh) + (dg(ah, bl) + dg(al, bh))


def _hmm(a, b):
    return _dot3(a, b, ((1,), (0,)))


def _hmm_nt(a, b):
    return _dot3(a, b, ((1,), (1,)))


def _hmm_tn(a, b):
    return _dot3(a, b, ((0,), (0,)))


def _sigmoid(x):
    return 1.0 / (1.0 + jnp.exp(-x))


def _rstd(x, n=None):
    n = x.shape[-1] if n is None else n
    return lax.rsqrt(jnp.sum(x * x, axis=-1, keepdims=True) * (1.0 / n) + EPS)


def _rms_bwd(dy, x, rstd, g, n=None):
    n = x.shape[-1] if n is None else n
    xh = x * rstd
    dxh = dy * g
    dx = rstd * (dxh - xh * (jnp.sum(dxh * xh, axis=-1, keepdims=True) * (1.0 / n)))
    return dx, dy * xh


def _valid_rows(i, tm, t_valid_end):
    r = i * tm + lax.broadcasted_iota(jnp.int32, (tm, 1), 0)
    return ((r >= FRONT) & (r < t_valid_end)).astype(F32)


def _colsum8(x):
    n, c = x.shape
    return jnp.sum(x.reshape(n // 8, 8, c), axis=0)


def _in_proj_fwd(x, g1, w):
    t = x.shape[0]
    tm = _tile(t, 192)
    segs = (SEG_GATES, SEG_AG, SEG_H4, SEG_CQ, SEG_CKV, SEG_KR)

    def body(x_ref, g_ref, w_ref, gates_ref, ag_ref, h4_ref, cq_ref, ckv_ref, kr_ref, hb_ref):
        xv = x_ref[...]
        hb = (xv * _rstd(xv) * g_ref[...]).astype(_MM)
        hb_ref[...] = hb
        for ref, (a, b) in zip((gates_ref, ag_ref, h4_ref, cq_ref, ckv_ref, kr_ref), segs):
            ref[...] = jnp.dot(hb, w_ref[:, a:b], preferred_element_type=F32)

    outs = [jax.ShapeDtypeStruct((t, b - a), F32) for a, b in segs] + [jax.ShapeDtypeStruct((t, D), _MM)]
    return pl.pallas_call(
        body, name="in_proj_fwd", grid=(t // tm,),
        in_specs=[_row(tm, D), _full((1, D)), _full((D, N_IN_P))],
        out_specs=[_row(tm, b - a) for a, b in segs] + [_row(tm, D)],
        out_shape=outs, compiler_params=_cp("parallel"),
    )(x, g1, w)


def _in_proj_bwd(du, x, dx1, g1, wt, t_end):
    t = x.shape[0]
    tm = _tile(t, 384)

    def body(du_ref, x_ref, dx1_ref, g_ref, wt_ref, dx_ref, dg_ref):
        i = pl.program_id(0)
        dh = jnp.dot(du_ref[...], wt_ref[...], preferred_element_type=F32)
        xv = x_ref[...]
        dxn, dgrow = _rms_bwd(dh, xv, _rstd(xv), g_ref[...])
        dx_ref[...] = _valid_rows(i, tm, t_end) * (dx1_ref[...] + dxn)

        @pl.when(i == 0)
        def _():
            dg_ref[...] = jnp.zeros_like(dg_ref)
        dg_ref[...] += _colsum8(dgrow)

    return pl.pallas_call(
        body, name="in_proj_bwd", grid=(t // tm,),
        in_specs=[_row(tm, N_IN_P), _row(tm, D), _row(tm, D), _full((1, D)), _full((N_IN_P, D))],
        out_specs=[_row(tm, D), _full((8, D))],
        out_shape=[jax.ShapeDtypeStruct((t, D), F32), jax.ShapeDtypeStruct((8, D), F32)],
        compiler_params=_cp("arbitrary"),
    )(du, x, dx1, g1, wt)


CONV_CH = 128


def _conv_fwd(ag, cw, cb):
    t = ag.shape[0]
    n = t // CONV_CH

    def body(a_ref, g_ref, w_ref, b_ref, z_ref, hp):
        hp[0:32, :] = jnp.zeros((32, LANE), F32)

        def fill(i, c):
            r = pl.multiple_of(i * CONV_CH, CONV_CH)
            hp[pl.ds(32 + r, CONV_CH), :] = a_ref[pl.ds(r, CONV_CH), :] * _sigmoid(g_ref[pl.ds(r, CONV_CH), :])
            return c
        lax.fori_loop(0, n, fill, 0)

        def conv(i, c):
            r = pl.multiple_of(i * CONV_CH, CONV_CH)
            acc = jnp.broadcast_to(b_ref[...], (CONV_CH, LANE))
            for k in range(CONV_K):
                acc = acc + w_ref[k:k + 1, :] * hp[pl.ds(r + (k + 2), CONV_CH), :]
            z_ref[pl.ds(r, CONV_CH), :] = acc
            return c
        lax.fori_loop(0, n, conv, 0)

    nb = CONV_DIM // LANE
    return pl.pallas_call(
        body, name="conv_fwd", grid=(nb,),
        in_specs=[pl.BlockSpec((t, LANE), lambda j: (0, j)), pl.BlockSpec((t, LANE), lambda j: (0, nb + j)),
                  pl.BlockSpec((32, LANE), lambda j: (0, j)), pl.BlockSpec((1, LANE), lambda j: (0, j))],
        out_specs=pl.BlockSpec((t, LANE), lambda j: (0, j)),
        out_shape=jax.ShapeDtypeStruct((t, CONV_DIM), F32),
        scratch_shapes=[pltpu.VMEM((t + 32, LANE), F32)],
        compiler_params=_cp("parallel"),
    )(ag, ag, cw, cb)


def _conv_bwd(ag, cw, dz):
    t = ag.shape[0]
    n = t // CONV_CH

    def body(a_ref, g_ref, w_ref, dz_ref, da_ref, dg_ref, dcw_ref, hp, dzp, accw):
        hp[0:32, :] = jnp.zeros((32, LANE), F32)
        dzp[pl.ds(t, 32), :] = jnp.zeros((32, LANE), F32)
        accw[...] = jnp.zeros_like(accw)

        def fill(i, c):
            r = pl.multiple_of(i * CONV_CH, CONV_CH)
            hp[pl.ds(32 + r, CONV_CH), :] = a_ref[pl.ds(r, CONV_CH), :] * _sigmoid(g_ref[pl.ds(r, CONV_CH), :])
            dzp[pl.ds(r, CONV_CH), :] = dz_ref[pl.ds(r, CONV_CH), :]
            return c
        lax.fori_loop(0, n, fill, 0)

        def step(i, c):
            r = pl.multiple_of(i * CONV_CH, CONV_CH)
            dzc = dz_ref[pl.ds(r, CONV_CH), :]
            dh = jnp.zeros((CONV_CH, LANE), F32)
            for k in range(CONV_K):
                dh = dh + w_ref[k:k + 1, :] * dzp[pl.ds(r + (CONV_K - 1 - k), CONV_CH), :]
                accw[8 * k:8 * k + 8, :] += _colsum8(dzc * hp[pl.ds(r + (k + 2), CONV_CH), :])
            a = a_ref[pl.ds(r, CONV_CH), :]
            sg = _sigmoid(g_ref[pl.ds(r, CONV_CH), :])
            da_ref[pl.ds(r, CONV_CH), :] = dh * sg
            dg_ref[pl.ds(r, CONV_CH), :] = dh * a * sg * (1.0 - sg)
            return c
        lax.fori_loop(0, n, step, 0)

        for k in range(CONV_K):
            dcw_ref[k:k + 1, :] = jnp.sum(accw[8 * k:8 * k + 8, :], axis=0, keepdims=True)
        dcw_ref[CONV_K:32, :] = jnp.zeros((32 - CONV_K, LANE), F32)

    nb = CONV_DIM // LANE
    colspec = pl.BlockSpec((t, LANE), lambda j: (0, j))
    return pl.pallas_call(
        body, name="conv_bwd", grid=(nb,),
        in_specs=[colspec, pl.BlockSpec((t, LANE), lambda j: (0, nb + j)),
                  pl.BlockSpec((32, LANE), lambda j: (0, j)), colspec],
        out_specs=[colspec, colspec, pl.BlockSpec((32, LANE), lambda j: (0, j))],
        out_shape=[jax.ShapeDtypeStruct((t, CONV_DIM), F32), jax.ShapeDtypeStruct((t, CONV_DIM), F32),
                   jax.ShapeDtypeStruct((32, CONV_DIM), F32)],
        scratch_shapes=[pltpu.VMEM((t + 32, LANE), F32), pltpu.VMEM((t + 32, LANE), F32),
                        pltpu.VMEM((8 * 32, LANE), F32)],
        compiler_params=_cp("parallel"),
    )(ag, ag, cw, dz)


def _rope(x, c, s1, s2):
    return x * c + pltpu.roll(x, LANE - 16, 1) * s1 + pltpu.roll(x, 16, 1) * s2


def _rope_t(dy, c, s1, s2):
    return dy * c + pltpu.roll(dy * s1, 16, 1) + pltpu.roll(dy * s2, LANE - 16, 1)


def _mla_pre_fwd(cq, ckv, kr, qag, wuq, kvag, wk, wv, qng, kng, rc, rs1, rs2):
    t = cq.shape[0]
    tm = _tile(t, 384)

    def body(cq_ref, ckv_ref, kr_ref, qag_ref, wuq_ref, kvag_ref, wk_ref, wv_ref, qng_ref, kng_ref,
             c_ref, s1_ref, s2_ref, q_ref, k_ref, v_ref, cqn_ref, ckvn_ref):
        cqv = cq_ref[...]
        cqn = (cqv * _rstd(cqv) * qag_ref[...]).astype(_MM)
        cqn_ref[...] = cqn
        ckvv = ckv_ref[...]
        ckvn = (ckvv * _rstd(ckvv) * kvag_ref[...]).astype(_MM)
        ckvn_ref[...] = ckvn
        qraw = jnp.dot(cqn, wuq_ref[...], preferred_element_type=F32)
        kraw = jnp.dot(ckvn, wk_ref[...], preferred_element_type=F32)
        v_ref[...] = jnp.dot(ckvn, wv_ref[...], preferred_element_type=F32).astype(_MM)
        krv = kr_ref[...]
        c, s1, s2 = c_ref[...], s1_ref[...], s2_ref[...]
        for h in range(NH):
            sl = slice(LANE * h, LANE * (h + 1))
            qh = qraw[:, sl]
            qn = qh * _rstd(qh, QK_DIM) * qng_ref[...]
            q_ref[:, sl] = (_rope(qn, c, s1, s2) * ATT_SCALE).astype(_MM)
            kh = kraw[:, sl] + krv
            kn = kh * _rstd(kh, QK_DIM) * kng_ref[...]
            k_ref[:, sl] = _rope(kn, c, s1, s2).astype(_MM)

    hd = NH * LANE
    return pl.pallas_call(
        body, name="mla_pre_fwd", grid=(t // tm,),
        in_specs=[_row(tm, 256), _row(tm, 128), _row(tm, 128), _full((1, 256)), _full((256, hd)),
                  _full((1, 128)), _full((128, hd)), _full((128, hd)), _full((1, LANE)), _full((1, LANE)),
                  _row(tm, LANE), _row(tm, LANE), _row(tm, LANE)],
        out_specs=[_row(tm, hd), _row(tm, hd), _row(tm, hd), _row(tm, 256), _row(tm, 128)],
        out_shape=[jax.ShapeDtypeStruct((t, hd), _MM)] * 3 + [jax.ShapeDtypeStruct((t, 256), _MM),
                                                              jax.ShapeDtypeStruct((t, 128), _MM)],
        compiler_params=_cp("parallel"),
    )(cq, ckv, kr, qag, wuq, kvag, wk, wv, qng, kng, rc, rs1, rs2)


def _mla_pre_bwd(dq, dk, dv, cq, ckv, kr, qag, wuq, kvag, wk, wv, qng, kng, rc, rs1, rs2):
    t = cq.shape[0]
    tm = _tile(t, 384)
    hd = NH * LANE

    def body(dq_ref, dk_ref, dv_ref, cq_ref, ckv_ref, kr_ref, qag_ref, wuq_ref, kvag_ref, wk_ref,
             wv_ref, qng_ref, kng_ref, c_ref, s1_ref, s2_ref,
             dcq_ref, dckv_ref, dkr_ref, dqraw_ref, dkraw_ref, dqag_ref, dkvag_ref, dqng_ref, dkng_ref):
        i = pl.program_id(0)
        cqv = cq_ref[...]
        rq_in = _rstd(cqv)
        cqn = (cqv * rq_in * qag_ref[...]).astype(_MM)
        ckvv = ckv_ref[...]
        rkv_in = _rstd(ckvv)
        ckvn = (ckvv * rkv_in * kvag_ref[...]).astype(_MM)
        qraw = jnp.dot(cqn, wuq_ref[...], preferred_element_type=F32)
        kraw = jnp.dot(ckvn, wk_ref[...], preferred_element_type=F32)
        krv = kr_ref[...]
        c, s1, s2 = c_ref[...], s1_ref[...], s2_ref[...]
        dkr = jnp.zeros((tm, LANE), F32)
        dqng = jnp.zeros((8, LANE), F32)
        dkng = jnp.zeros((8, LANE), F32)
        for h in range(NH):
            sl = slice(LANE * h, LANE * (h + 1))
            qh = qraw[:, sl]
            dqn = _rope_t(dq_ref[:, sl] * ATT_SCALE, c, s1, s2)
            dqh, gq = _rms_bwd(dqn, qh, _rstd(qh, QK_DIM), qng_ref[...], QK_DIM)
            dqraw_ref[:, sl] = dqh.astype(_MM)
            dqng = dqng + _colsum8(gq)
            kh = kraw[:, sl] + krv
            dkn = _rope_t(dk_ref[:, sl], c, s1, s2)
            dkh, gk = _rms_bwd(dkn, kh, _rstd(kh, QK_DIM), kng_ref[...], QK_DIM)
            dkraw_ref[:, sl] = dkh.astype(_MM)
            dkr = dkr + dkh
            dkng = dkng + _colsum8(gk)
        dkr_ref[...] = dkr.astype(_MM)
        dcqn = _mm_nt(dqraw_ref[...], wuq_ref[...])
        dcq, gqa = _rms_bwd(dcqn, cqv, rq_in, qag_ref[...])
        dcq_ref[...] = dcq.astype(_MM)
        dckvn = _mm_nt(dkraw_ref[...], wk_ref[...]) + _mm_nt(dv_ref[...], wv_ref[...])
        dckv, gkva = _rms_bwd(dckvn, ckvv, rkv_in, kvag_ref[...])
        dckv_ref[...] = dckv.astype(_MM)

        @pl.when(i == 0)
        def _():
            dqag_ref[...] = jnp.zeros_like(dqag_ref)
            dkvag_ref[...] = jnp.zeros_like(dkvag_ref)
            dqng_ref[...] = jnp.zeros_like(dqng_ref)
            dkng_ref[...] = jnp.zeros_like(dkng_ref)
        dqag_ref[...] += _colsum8(gqa)
        dkvag_ref[...] += _colsum8(gkva)
        dqng_ref[...] += dqng
        dkng_ref[...] += dkng

    return pl.pallas_call(
        body, name="mla_pre_bwd", grid=(t // tm,),
        in_specs=[_row(tm, hd), _row(tm, hd), _row(tm, hd), _row(tm, 256), _row(tm, 128), _row(tm, 128),
                  _full((1, 256)), _full((256, hd)), _full((1, 128)), _full((128, hd)),
                  _full((128, hd)), _full((1, LANE)), _full((1, LANE)),
                  _row(tm, LANE), _row(tm, LANE), _row(tm, LANE)],
        out_specs=[_row(tm, 256), _row(tm, 128), _row(tm, 128), _row(tm, hd), _row(tm, hd),
                   _full((8, 256)), _full((8, 128)), _full((8, LANE)), _full((8, LANE))],
        out_shape=[jax.ShapeDtypeStruct((t, 256), _MM), jax.ShapeDtypeStruct((t, 128), _MM),
                   jax.ShapeDtypeStruct((t, 128), _MM), jax.ShapeDtypeStruct((t, hd), _MM),
                   jax.ShapeDtypeStruct((t, hd), _MM), jax.ShapeDtypeStruct((8, 256), F32),
                   jax.ShapeDtypeStruct((8, 128), F32), jax.ShapeDtypeStruct((8, LANE), F32),
                   jax.ShapeDtypeStruct((8, LANE), F32)],
        compiler_params=_cp("arbitrary"),
    )(dq, dk, dv, cq, ckv, kr, qag, wuq, kvag, wk, wv, qng, kng, rc, rs1, rs2)


ATT_TILE = 704


def _attn_mask(r0, c0, tq):
    rows = r0 + lax.broadcasted_iota(jnp.int32, (tq, 1), 0)
    cols = c0 + lax.broadcasted_iota(jnp.int32, (1, tq), 1)
    return (cols <= rows) & (cols >= FRONT)


def _attn_fwd(q, k, v, plan=None):
    t = q.shape[0]
    tq = _tile(t, ATT_TILE)
    nq = t // tq
    p_args, p_in, p_out, p_shape, p_sem = _plan_specs(plan)

    def body(*refs):
        ((q_ref, k_ref, v_ref), (o_ref, lse_ref), _), rider = _host_refs(refs, 3, 2, 0, plan)
        done = _ride(plan, rider, pl.program_id(0), NH - 1)

        def qloop(qi, carry):
            r0 = pl.multiple_of(qi * tq, tq)
            qb = q_ref[pl.ds(r0, tq), :]

            def kstep(kj, st, masked):
                m, l, acc = st
                c0 = pl.multiple_of(kj * tq, tq)
                s = _mm_nt(qb, k_ref[pl.ds(c0, tq), :])
                if masked:
                    s = jnp.where(_attn_mask(r0, c0, tq), s, NEG)
                m2 = jnp.maximum(m, jnp.max(s, axis=-1, keepdims=True))
                p = jnp.exp(s - m2)
                a = jnp.exp(m - m2)
                l = a * l + jnp.sum(p, axis=-1, keepdims=True)
                acc = a * acc + _mm(p, v_ref[pl.ds(c0, tq), :])
                return m2, l, acc

            st = kstep(0, (jnp.full((tq, 1), NEG, F32), jnp.zeros((tq, 1), F32), jnp.zeros((tq, LANE), F32)), True)
            st = lax.fori_loop(1, qi, lambda kj, s_: kstep(kj, s_, False), st)
            m, l, acc = lax.cond(qi > 0, lambda s_: kstep(qi, s_, True), lambda s_: s_, st)
            o_ref[pl.ds(r0, tq), :] = acc / l
            lse_ref[pl.ds(r0, tq), :] = m + jnp.log(l)
            return carry
        lax.fori_loop(0, nq, qloop, 0)
        done()

    hs = pl.BlockSpec((t, LANE), lambda h: (0, h))
    res = pl.pallas_call(
        body, name="attn_fwd", grid=(NH,),
        in_specs=[hs, hs, hs] + p_in,
        out_specs=[hs, pl.BlockSpec((None, t, 1), lambda h: (h, 0, 0))] + p_out,
        out_shape=[jax.ShapeDtypeStruct((t, NH * LANE), F32), jax.ShapeDtypeStruct((NH, t, 1), F32)] + p_shape,
        scratch_shapes=p_sem,
        compiler_params=_cp("parallel" if plan is None else "arbitrary"),
    )(q, k, v, *p_args)
    return res[:2], res[2:]


def _attn_bwd(q, k, v, o, lse, do, plan=None):
    t = q.shape[0]
    tq = _tile(t, ATT_TILE)
    nq = t // tq
    p_args, p_in, p_out, p_shape, p_sem = _plan_specs(plan)

    def body(*refs):
        (ins, (dq_ref, dk_ref, dv_ref), (delta,)), rider = _host_refs(refs, 6, 3, 1, plan)
        q_ref, k_ref, v_ref, o_ref, lse_ref, do_ref = ins
        done = _ride(plan, rider, pl.program_id(0), NH - 1)

        def prep(i, c):
            r0 = pl.multiple_of(i * tq, tq)
            delta[pl.ds(r0, tq), :] = jnp.sum(do_ref[pl.ds(r0, tq), :] * o_ref[pl.ds(r0, tq), :], axis=-1,
                                              keepdims=True)
            dq_ref[pl.ds(r0, tq), :] = jnp.zeros((tq, LANE), F32)
            return c
        lax.fori_loop(0, nq, prep, 0)

        def kloop(kj, carry):
            c0 = pl.multiple_of(kj * tq, tq)
            kb = k_ref[pl.ds(c0, tq), :]
            vb = v_ref[pl.ds(c0, tq), :]

            def qstep(qi, st, masked):
                dkb, dvb = st
                r0 = pl.multiple_of(qi * tq, tq)
                qb = q_ref[pl.ds(r0, tq), :]
                dob = do_ref[pl.ds(r0, tq), :].astype(_MM)
                s = _mm_nt(qb, kb)
                if masked:
                    s = jnp.where(_attn_mask(r0, c0, tq), s, NEG)
                p = jnp.exp(s - lse_ref[pl.ds(r0, tq), :])
                dvb = dvb + _mm_tn(p, dob)
                dp = _mm_nt(dob, vb)
                ds = (p * (dp - delta[pl.ds(r0, tq), :])).astype(_MM)
                dkb = dkb + _mm_tn(ds, qb)
                dq_ref[pl.ds(r0, tq), :] += _mm(ds, kb)
                return dkb, dvb

            st = qstep(kj, (jnp.zeros((tq, LANE), F32), jnp.zeros((tq, LANE), F32)), True)
            dkb, dvb = lax.cond(
                kj == 0,
                lambda s_: lax.fori_loop(kj + 1, nq, lambda qi, t_: qstep(qi, t_, True), s_),
                lambda s_: lax.fori_loop(kj + 1, nq, lambda qi, t_: qstep(qi, t_, False), s_), st)
            dk_ref[pl.ds(c0, tq), :] = dkb
            dv_ref[pl.ds(c0, tq), :] = dvb
            return carry
        lax.fori_loop(0, nq, kloop, 0)
        done()

    hs = pl.BlockSpec((t, LANE), lambda h: (0, h))
    res = pl.pallas_call(
        body, name="attn_bwd", grid=(NH,),
        in_specs=[hs, hs, hs, hs, pl.BlockSpec((None, t, 1), lambda h: (h, 0, 0)), hs] + p_in,
        out_specs=[hs, hs, hs] + p_out,
        out_shape=[jax.ShapeDtypeStruct((t, NH * LANE), F32)] * 3 + p_shape,
        scratch_shapes=[pltpu.VMEM((t, 1), F32)] + p_sem,
        compiler_params=_cp("parallel" if plan is None else "arbitrary"),
    )(q, k, v, o, lse, do, *p_args)
    return res[:3], res[3:]


def _cumsum_rows(x):
    n = x.shape[0]
    rows = lax.broadcasted_iota(jnp.int32, (n, 1), 0)
    d = 1
    while d < n:
        x = x + jnp.where(rows >= d, pltpu.roll(x, d, 0), 0.0)
        d *= 2
    return x


def _revcumsum_rows(x):
    n = x.shape[0]
    rows = lax.broadcasted_iota(jnp.int32, (n, 1), 0)
    d = 1
    while d < n:
        x = x + jnp.where(rows < n - d, pltpu.roll(x, n - d, 0), 0.0)
        d *= 2
    return x


def _hgrn_gates(f, lb):
    sneg = _sigmoid(-f)
    kk = (1.0 - lb) * sneg
    lf = jnp.log1p(-jnp.minimum(kk, GATE_CLAMP))
    return kk, lf, sneg


def _silu(x):
    return x * _sigmoid(x)


def _dsilu(x):
    s = _sigmoid(x)
    return s * (1.0 + x * (1.0 - s))


def _hgrn_intra(q, kk, b):
    parts = []
    for blk in range(CHUNK // SUB):
        lo = blk * SUB
        ref = jnp.zeros((1, LANE), F32) if blk == 0 else b[lo - 1:lo, :]
        eq = jnp.exp(b[lo:lo + SUB, :] - ref)
        ek = jnp.exp(jnp.minimum(ref - b, EXP_CLIP))
        parts.append((q[lo:lo + SUB, :] * eq, kk * ek, eq, ek))
    return parts


def _chunk_causal():
    return lax.broadcasted_iota(jnp.int32, (CHUNK, CHUNK), 1) <= lax.broadcasted_iota(jnp.int32, (CHUNK, CHUNK), 0)


def _hgrn_fwd(h4, lb):
    t = h4.shape[0]
    nc = t // CHUNK

    def body(q_ref, f_ref, i_ref, lb_ref, o_ref, s_ref, st):
        st[...] = jnp.zeros_like(st)
        causal = _chunk_causal()

        def chunk(c, carry):
            r0 = pl.multiple_of(c * CHUNK, CHUNK)
            q = q_ref[pl.ds(r0, CHUNK), :]
            kk, lf, _ = _hgrn_gates(f_ref[pl.ds(r0, CHUNK), :], lb_ref[...])
            v = _silu(i_ref[pl.ds(r0, CHUNK), :])
            b = _cumsum_rows(lf)
            s_prev = st[...]
            s_ref[c] = s_prev
            o = _hmm_nt(q * jnp.exp(b), s_prev)
            a = jnp.concatenate([_hmm_nt(qs, ks) for qs, ks, _, _ in _hgrn_intra(q, kk, b)], axis=0)
            a = jnp.where(causal, a, 0.0)
            o_ref[pl.ds(r0, CHUNK), :] = o + _hmm(a, v)
            bl = b[CHUNK - 1:CHUNK, :]
            st[...] = s_prev * jnp.exp(bl) + _hmm_tn(v, kk * jnp.exp(bl - b))
            return carry
        lax.fori_loop(0, nc, chunk, 0, unroll=2)

    def col(j):
        return pl.BlockSpec((t, LANE), lambda h: (0, HH * j + h))
    return pl.pallas_call(
        body, name="hgrn_fwd", grid=(HH,),
        in_specs=[col(0), col(1), col(2), pl.BlockSpec((1, LANE), lambda h: (0, h))],
        out_specs=[pl.BlockSpec((t, LANE), lambda h: (0, h)),
                   pl.BlockSpec((None, nc, LANE, LANE), lambda h: (h, 0, 0, 0))],
        out_shape=[jax.ShapeDtypeStruct((t, HH * LANE), F32), jax.ShapeDtypeStruct((HH, nc, LANE, LANE), F32)],
        scratch_shapes=[pltpu.VMEM((LANE, LANE), F32)],
        compiler_params=_cp("parallel"),
    )(h4, h4, h4, lb)


def _hgrn_bwd(h4, lb, do, states, plan=None):
    t = h4.shape[0]
    nc = t // CHUNK
    p_args, p_in, p_out, p_shape, p_sem = _plan_specs(plan)

    def body(*refs):
        (ins, outs, (dst, carry)), rider = _host_refs(refs, 6, 4, 2, plan)
        q_ref, f_ref, i_ref, lb_ref, do_ref, s_ref = ins
        dq_ref, df_ref, di_ref, dlb_ref = outs
        done = _ride(plan, rider, pl.program_id(0), HH - 1)
        dst[...] = jnp.zeros_like(dst)
        carry[...] = jnp.zeros_like(carry)
        dlb_ref[...] = jnp.zeros_like(dlb_ref)
        causal = _chunk_causal()

        def chunk(cc, cr):
            c = nc - 1 - cc
            r0 = pl.multiple_of(c * CHUNK, CHUNK)
            q = q_ref[pl.ds(r0, CHUNK), :]
            lbv = lb_ref[...]
            kk, lf, sneg = _hgrn_gates(f_ref[pl.ds(r0, CHUNK), :], lbv)
            iv = i_ref[pl.ds(r0, CHUNK), :]
            v = _silu(iv)
            b = _cumsum_rows(lf)
            s_prev = s_ref[c]
            ds_new = dst[...]
            dob = do_ref[pl.ds(r0, CHUNK), :]
            e = jnp.exp(b)
            qe = q * e
            bl = b[CHUNK - 1:CHUNK, :]
            etail = jnp.exp(bl - b)
            kd = kk * etail
            dq_inter = _hmm(dob, s_prev) * e
            dv = _hmm_nt(kd, ds_new)
            dkk = _hmm(v, ds_new) * etail
            parts = _hgrn_intra(q, kk, b)
            a = jnp.where(causal, jnp.concatenate([_hmm_nt(qs, ks) for qs, ks, _, _ in parts], axis=0), 0.0)
            da = jnp.where(causal, _hmm_nt(dob, v), 0.0)
            dv = dv + _hmm_tn(a, dob)
            dq_rows = []
            for blk, (qs, ks, eq, ek) in enumerate(parts):
                da_blk = da[blk * SUB:(blk + 1) * SUB, :]
                dq_rows.append(_hmm(da_blk, ks) * eq)
                dkk = dkk + _hmm_tn(da_blk, qs) * ek
            dq = dq_inter + jnp.concatenate(dq_rows, axis=0)
            dst[...] = ds_new * jnp.exp(bl) + _hmm_tn(dob, qe)
            g = q * dq - kk * dkk
            dlf = _revcumsum_rows(g) + carry[0:1, :]
            carry[0:1, :] += jnp.sum(g, axis=0, keepdims=True)
            dkk_tot = dkk + dlf * jnp.where(kk < GATE_CLAMP, -1.0 / (1.0 - kk), 0.0)
            dq_ref[pl.ds(r0, CHUNK), :] = dq
            df_ref[pl.ds(r0, CHUNK), :] = dkk_tot * (1.0 - lbv) * (-sneg * (1.0 - sneg))
            di_ref[pl.ds(r0, CHUNK), :] = dv * _dsilu(iv)
            dlb_ref[...] += _colsum8(dkk_tot * (-sneg))
            return cr
        lax.fori_loop(0, nc, chunk, 0, unroll=2)
        done()

    def col(j):
        return pl.BlockSpec((t, LANE), lambda h: (0, HH * j + h))
    hs = pl.BlockSpec((t, LANE), lambda h: (0, h))
    res = pl.pallas_call(
        body, name="hgrn_bwd", grid=(HH,),
        in_specs=[col(0), col(1), col(2), pl.BlockSpec((1, LANE), lambda h: (0, h)), hs,
                  pl.BlockSpec((None, nc, LANE, LANE), lambda h: (h, 0, 0, 0))] + p_in,
        out_specs=[hs, hs, hs, pl.BlockSpec((8, LANE), lambda h: (0, h))] + p_out,
        out_shape=[jax.ShapeDtypeStruct((t, HH * LANE), F32)] * 3 + [jax.ShapeDtypeStruct((8, HH * LANE), F32)]
        + p_shape,
        scratch_shapes=[pltpu.VMEM((LANE, LANE), F32), pltpu.VMEM((8, LANE), F32)] + p_sem,
        compiler_params=_cp("parallel" if plan is None else "arbitrary"),
    )(h4, h4, h4, lb, do, states, *p_args)
    return res[:4], res[4:]


def _ln_fwd(z, g, b):
    mu = jnp.mean(z, axis=-1, keepdims=True)
    zc = z - mu
    rstd = lax.rsqrt(jnp.mean(zc * zc, axis=-1, keepdims=True) + EPS)
    zh = zc * rstd
    return zh * g + b, zh, rstd


def _mix_fwd(x, z, o_att, o_h, h4, gates, lng, lnb, wco, wao, ng, who, wout, t_end):
    t = x.shape[0]
    tm = _tile(t, 192)

    def body(x_ref, z_ref, oa_ref, oh_ref, hg_ref, gt_ref, lng_ref, lnb_ref, wco_ref, wao_ref, ng_ref, who_ref,
             wout_ref, x1_ref, mix_ref, ca_ref, oc_ref, ya_ref, yb_ref, yc_ref):
        i = pl.program_id(0)
        ln, _, _ = _ln_fwd(z_ref[...], lng_ref[...], lnb_ref[...])
        ca = _silu(ln).astype(_MM)
        ca_ref[...] = ca
        ya = jnp.dot(ca, wco_ref[...], preferred_element_type=F32)
        yb = _mm(oa_ref[...], wao_ref[...])
        hg = hg_ref[...]
        for h in range(HH):
            sl = slice(LANE * h, LANE * (h + 1))
            oh = oh_ref[:, sl]
            oc_ref[:, sl] = (oh * _rstd(oh) * ng_ref[:, sl] * _silu(hg[:, sl])).astype(_MM)
        yc = jnp.dot(oc_ref[...], who_ref[...], preferred_element_type=F32)
        ya_ref[...] = ya
        yb_ref[...] = yb
        yc_ref[...] = yc
        mix = (_sigmoid(gt_ref[:, 0:D]) * ya + _sigmoid(gt_ref[:, D:2 * D]) * yb
               + _sigmoid(gt_ref[:, 2 * D:3 * D]) * yc).astype(_MM)
        mix_ref[...] = mix
        x1_ref[...] = x_ref[...] + _valid_rows(i, tm, t_end) * jnp.dot(mix, wout_ref[...],
                                                                       preferred_element_type=F32)

    hd = NH * LANE
    return pl.pallas_call(
        body, name="mix_fwd", grid=(t // tm,),
        in_specs=[_row(tm, D), _row(tm, CONV_DIM), _row(tm, hd), _row(tm, 512), _row(tm, 512, 3), _row(tm, 3 * D),
                  _full((1, 512)), _full((1, 512)), _full((512, D)), _full((hd, D)), _full((1, 512)),
                  _full((512, D)), _full((D, D))],
        out_specs=[_row(tm, D), _row(tm, D), _row(tm, 512), _row(tm, 512), _row(tm, D), _row(tm, D), _row(tm, D)],
        out_shape=[jax.ShapeDtypeStruct((t, D), F32), jax.ShapeDtypeStruct((t, D), _MM),
                   jax.ShapeDtypeStruct((t, 512), _MM), jax.ShapeDtypeStruct((t, 512), _MM),
                   jax.ShapeDtypeStruct((t, D), F32), jax.ShapeDtypeStruct((t, D), F32),
                   jax.ShapeDtypeStruct((t, D), F32)],
        compiler_params=_cp("parallel"),
    )(x, z, o_att, o_h, h4, gates, lng, lnb, wco, wao, ng, who, wout)


def _mix_bwd(dx1, ya, yb, yc, gates, z, o_h, h4, lng, lnb, ng, wout, wco, wao, who, plan=None):
    t = dx1.shape[0]
    tm = _tile(t, 192)
    hd = NH * LANE
    p_args, p_in, p_out, p_shape, p_sem = _plan_specs(plan)

    def body(*refs):
        (ins, outs, _), rider = _host_refs(refs, 15, 12, 0, plan)
        (dx1_ref, ya_ref, yb_ref, yc_ref, gt_ref, z_ref, oh_ref, hg_ref, lng_ref, lnb_ref, ng_ref,
         wout_ref, wco_ref, wao_ref, who_ref) = ins
        (dgt_ref, dya_ref, dyb_ref, dyc_ref, dz_ref, doa_ref, doh_ref, dhg_ref,
         dlng_ref, dlnb_ref, dcb_ref, dng_ref) = outs
        i = pl.program_id(0)
        done = _ride(plan, rider, i, t // tm - 1)
        dmix = _mm_nt(dx1_ref[...], wout_ref[...])
        dys = []
        for j, y_ref in enumerate((ya_ref, yb_ref, yc_ref)):
            sg = _sigmoid(gt_ref[:, j * D:(j + 1) * D])
            dgt_ref[:, j * D:(j + 1) * D] = (dmix * y_ref[...] * sg * (1.0 - sg)).astype(_MM)
            dys.append((dmix * sg).astype(_MM))
        dya_ref[...], dyb_ref[...], dyc_ref[...] = dys
        dca = _mm_nt(dys[0], wco_ref[...])
        ln, zh, rstd = _ln_fwd(z_ref[...], lng_ref[...], lnb_ref[...])
        dln = dca * _dsilu(ln)
        dzh = dln * lng_ref[...]
        dz = rstd * (dzh - jnp.mean(dzh, axis=-1, keepdims=True)
                     - zh * jnp.mean(dzh * zh, axis=-1, keepdims=True))
        dz_ref[...] = dz
        doa_ref[...] = _mm_nt(dys[1], wao_ref[...])
        doc = _mm_nt(dys[2], who_ref[...])
        hg = hg_ref[...]
        dng_rows = []
        for h in range(HH):
            sl = slice(LANE * h, LANE * (h + 1))
            oh = oh_ref[:, sl]
            r = _rstd(oh)
            don = doc[:, sl] * _silu(hg[:, sl])
            dhg_ref[:, sl] = (doc[:, sl] * oh * r * ng_ref[:, sl] * _dsilu(hg[:, sl])).astype(_MM)
            doh, gn = _rms_bwd(don, oh, r, ng_ref[:, sl])
            doh_ref[:, sl] = doh
            dng_rows.append(_colsum8(gn))

        @pl.when(i == 0)
        def _():
            dlng_ref[...] = jnp.zeros_like(dlng_ref)
            dlnb_ref[...] = jnp.zeros_like(dlnb_ref)
            dcb_ref[...] = jnp.zeros_like(dcb_ref)
            dng_ref[...] = jnp.zeros_like(dng_ref)
        dlng_ref[...] += _colsum8(dln * zh)
        dlnb_ref[...] += _colsum8(dln)
        dcb_ref[...] += _colsum8(dz)
        dng_ref[...] += jnp.concatenate(dng_rows, axis=1)
        done()

    res = pl.pallas_call(
        body, name="mix_bwd", grid=(t // tm,),
        in_specs=[_row(tm, D), _row(tm, D), _row(tm, D), _row(tm, D), _row(tm, 3 * D), _row(tm, 512), _row(tm, 512),
                  _row(tm, 512, 3), _full((1, 512)), _full((1, 512)), _full((1, 512)),
                  _full((D, D)), _full((512, D)), _full((hd, D)), _full((512, D))] + p_in,
        out_specs=[_row(tm, 3 * D), _row(tm, D), _row(tm, D), _row(tm, D), _row(tm, 512), _row(tm, hd),
                   _row(tm, 512), _row(tm, 512), _full((8, 512)), _full((8, 512)), _full((8, 512)),
                   _full((8, 512))] + p_out,
        out_shape=[jax.ShapeDtypeStruct((t, 3 * D), _MM), jax.ShapeDtypeStruct((t, D), _MM),
                   jax.ShapeDtypeStruct((t, D), _MM), jax.ShapeDtypeStruct((t, D), _MM),
                   jax.ShapeDtypeStruct((t, 512), F32), jax.ShapeDtypeStruct((t, hd), F32),
                   jax.ShapeDtypeStruct((t, 512), F32), jax.ShapeDtypeStruct((t, 512), _MM)]
        + [jax.ShapeDtypeStruct((8, 512), F32)] * 4 + p_shape,
        scratch_shapes=p_sem,
        compiler_params=_cp("arbitrary"),
    )(dx1, ya, yb, yc, gates, z, o_h, h4, lng, lnb, ng, wout, wco, wao, who, *p_args)
    return res[:12], res[12:]


D_FF = 4096


def _ffn_fwd(x1, g2, w1, w2):
    t = x1.shape[0]
    tm = _tile(t, 192)

    def body(x1_ref, g_ref, w1_ref, w2_ref, x2_ref, p_ref):
        xv = x1_ref[...]
        h2 = (xv * _rstd(xv) * g_ref[...]).astype(_MM)
        p = jnp.dot(h2, w1_ref[...], preferred_element_type=F32)
        p_ref[...] = p
        r = jnp.maximum(p, 0.0)
        x2_ref[...] = xv + jnp.dot((r * r).astype(_MM), w2_ref[...], preferred_element_type=F32)

    return pl.pallas_call(
        body, name="ffn_fwd", grid=(t // tm,),
        in_specs=[_row(tm, D), _full((1, D)), _full((D, D_FF)), _full((D_FF, D))],
        out_specs=[_row(tm, D), _row(tm, D_FF)],
        out_shape=[jax.ShapeDtypeStruct((t, D), F32), jax.ShapeDtypeStruct((t, D_FF), F32)],
        compiler_params=_cp("parallel"),
    )(x1, g2, w1, w2)


def _ffn_bwd(dx2, x1, p, g2, w1t, w2t, plan=None):
    t = x1.shape[0]
    tm = _tile(t, 192)
    p_args, p_in, p_out, p_shape, p_sem = _plan_specs(plan)

    def body(*refs):
        (ins, outs, _), rider = _host_refs(refs, 6, 5, 0, plan)
        dx2_ref, x1_ref, p_ref, g_ref, w1t_ref, w2t_ref = ins
        dx1_ref, h2_ref, act_ref, dp_ref, dg_ref = outs
        i = pl.program_id(0)
        done = _ride(plan, rider, i, t // tm - 1)
        xv = x1_ref[...]
        rstd = _rstd(xv)
        h2_ref[...] = (xv * rstd * g_ref[...]).astype(_MM)
        r = jnp.maximum(p_ref[...], 0.0)
        act_ref[...] = (r * r).astype(_MM)
        dx2 = dx2_ref[...]
        da = _mm(dx2, w2t_ref[...])
        dp = (2.0 * r * da).astype(_MM)
        dp_ref[...] = dp
        dh2 = jnp.dot(dp, w1t_ref[...], preferred_element_type=F32)
        dxn, dgrow = _rms_bwd(dh2, xv, rstd, g_ref[...])
        dx1_ref[...] = dx2 + dxn

        @pl.when(i == 0)
        def _():
            dg_ref[...] = jnp.zeros_like(dg_ref)
        dg_ref[...] += _colsum8(dgrow)
        done()

    res = pl.pallas_call(
        body, name="ffn_bwd", grid=(t // tm,),
        in_specs=[_row(tm, D), _row(tm, D), _row(tm, D_FF), _full((1, D)), _full((D_FF, D)),
                  _full((D, D_FF))] + p_in,
        out_specs=[_row(tm, D), _row(tm, D), _row(tm, D_FF), _row(tm, D_FF), _full((8, D))] + p_out,
        out_shape=[jax.ShapeDtypeStruct((t, D), F32), jax.ShapeDtypeStruct((t, D), _MM),
                   jax.ShapeDtypeStruct((t, D_FF), _MM), jax.ShapeDtypeStruct((t, D_FF), _MM),
                   jax.ShapeDtypeStruct((8, D), F32)] + p_shape,
        scratch_shapes=p_sem,
        compiler_params=_cp("arbitrary"),
    )(dx2, x1, p, g2, w1t, w2t, *p_args)
    return res[:5], res[5:]


WGRAD_VMEM = 40 * 1024 * 1024


def _wgrad(a, b, name, chips=1):
    t, ka = a.shape
    nb = b.shape[1]
    cs = nb // chips
    widths = [d for d in range(cs, 0, -LANE) if cs % d == 0 and d % LANE == 0] or [cs]
    tn, tm = widths[-1], 64
    for d in widths:
        room = WGRAD_VMEM - 2 * ka * d * 4
        row_bytes = 2 * (ka * a.dtype.itemsize + d * b.dtype.itemsize) + 4 * ka
        fit = [r for r in range(64, t + 1, 64) if t % r == 0 and r * row_bytes <= room]
        if ka * d * 4 <= 16 * 1024 * 1024 and fit and (max(fit) >= 384 or d == widths[-1]):
            tn, tm = d, max(fit)
            break
    per = cs // tn

    def body(a_ref, b_ref, o_ref):
        @pl.when(pl.program_id(1) == 0)
        def _():
            o_ref[...] = jnp.zeros_like(o_ref)
        o_ref[...] += _mm_tn(a_ref[...], b_ref[...])

    if chips == 1:
        out_spec = pl.BlockSpec((ka, tn), lambda n, i: (0, n))
        out_shape = jax.ShapeDtypeStruct((ka, nb), F32)
    else:
        out_spec = pl.BlockSpec((None, ka, tn), lambda n, i: (n // per, 0, n % per))
        out_shape = jax.ShapeDtypeStruct((chips, ka, cs), F32)
    return pl.pallas_call(
        body, name="wgrad_" + name, grid=(nb // tn, t // tm),
        in_specs=[pl.BlockSpec((tm, ka), lambda n, i: (i, 0)), pl.BlockSpec((tm, tn), lambda n, i: (i, n))],
        out_specs=out_spec, out_shape=out_shape,
        compiler_params=_cp("parallel", "arbitrary"),
    )(a, b)


def _loss_head(y, target, t_end):
    t = y.shape[0]
    tm = _tile(t, 384)

    def body(y_ref, tg_ref, dy_ref, l_ref):
        i = pl.program_id(0)
        r = i * tm + lax.broadcasted_iota(jnp.int32, (tm, 1), 0)
        real = ((r >= ROW0) & (r < t_end)).astype(F32)
        diff = (y_ref[...] - tg_ref[...]) * real
        dy_ref[...] = diff * (1.0 / D)

        @pl.when(i == 0)
        def _():
            l_ref[...] = jnp.zeros_like(l_ref)
        sq = _colsum8(diff * diff)
        part = sq[:, 0:LANE]
        for j in range(1, D // LANE):
            part = part + sq[:, j * LANE:(j + 1) * LANE]
        l_ref[...] += part * (0.5 / D)

    return pl.pallas_call(
        body, name="loss_head", grid=(t // tm,),
        in_specs=[_row(tm, D), _row(tm, D)],
        out_specs=[_row(tm, D), _full((8, LANE))],
        out_shape=[jax.ShapeDtypeStruct((t, D), F32), jax.ShapeDtypeStruct((8, LANE), F32)],
        compiler_params=_cp("arbitrary"),
    )(y, target)


def _lower_bounds_fwd(logits):
    depth, n = logits.shape

    def body(l_ref, lb_ref):
        lg = l_ref[...]
        m = jnp.max(lg, axis=0, keepdims=True)
        e = jnp.exp(lg - m)
        p = e / jnp.sum(e, axis=0, keepdims=True)
        acc = jnp.zeros((1, n), F32)
        for l in range(depth):
            if l > 0:
                acc = acc + p[l:l + 1, :]
            lb_ref[l:l + 1, :] = acc

    return pl.pallas_call(body, name="lower_bounds_fwd", out_shape=jax.ShapeDtypeStruct((depth, n), F32))(logits)


def _lower_bounds_bwd(logits, dlb):
    depth, n = logits.shape

    def body(l_ref, dlb_ref, dl_ref):
        lg = l_ref[...]
        m = jnp.max(lg, axis=0, keepdims=True)
        e = jnp.exp(lg - m)
        p = e / jnp.sum(e, axis=0, keepdims=True)
        dps = [jnp.zeros((1, n), F32)]
        for j in range(1, depth):
            acc = jnp.zeros((1, n), F32)
            for l in range(j, depth):
                acc = acc + dlb_ref[l:l + 1, :]
            dps.append(acc)
        dot = jnp.zeros((1, n), F32)
        for j in range(depth):
            dot = dot + p[j:j + 1, :] * dps[j]
        for j in range(depth):
            dl_ref[j:j + 1, :] = p[j:j + 1, :] * (dps[j] - dot)

    return pl.pallas_call(body, name="lower_bounds_bwd", out_shape=jax.ShapeDtypeStruct((depth, n), F32))(logits, dlb)


def _ew_tile(rows, cols, n_arrays):
    cap = max(16, (32 * 1024 * 1024) // (8 * n_arrays * cols))
    for mult in (16, 8):
        fit = [t for t in range(mult, rows + 1, mult) if rows % t == 0 and t <= cap]
        if fit:
            return max(fit)
    return rows


def _adamw_math(w, g, m, v):
    mn = ADAM_B1 * m + (1.0 - ADAM_B1) * g
    vn = ADAM_B2 * v + (1.0 - ADAM_B2) * (g * g)
    m_hat = mn / (1.0 - ADAM_B1 ** ADAM_STEP)
    v_hat = vn / (1.0 - ADAM_B2 ** ADAM_STEP)
    return -ADAM_LR * (m_hat / (jnp.sqrt(v_hat) + ADAM_EPS) + ADAM_WD * w), mn, vn


def _adamw_layers(w, m, v, g0, g1, g_sibling, name):
    _, r, c_ = w.shape
    tr = _ew_tile(r, c_, 10)

    def body(w_ref, m_ref, v_ref, g0_ref, g1_ref, gs_ref, g_ref, d_ref, mo_ref, vo_ref):
        layer = pl.program_id(0)
        own = jnp.where(layer == 0, g0_ref[...], g1_ref[...])
        g = jnp.where(layer == lax.axis_index("c"), own, gs_ref[...])
        g_ref[...] = g
        d_ref[...], mo_ref[...], vo_ref[...] = _adamw_math(w_ref[...], g, m_ref[...], v_ref[...])

    lay = pl.BlockSpec((None, tr, c_), lambda l, i: (l, i, 0))
    flat = pl.BlockSpec((tr, c_), lambda l, i: (i, 0))
    only0 = pl.BlockSpec((tr, c_), lambda l, i: (i * (1 - l), 0))
    only1 = pl.BlockSpec((tr, c_), lambda l, i: (i * l, 0))
    return pl.pallas_call(
        body, name="adamw_" + name, grid=(2, r // tr),
        in_specs=[lay, lay, lay, only0, only1, flat], out_specs=[lay] * 4,
        out_shape=[jax.ShapeDtypeStruct(w.shape, F32)] * 4,
        compiler_params=_cp("parallel", "parallel"),
    )(w, m, v, g0, g1, g_sibling)


def _adamw(w, g, m, v, name):
    rows, cols = w.shape
    tr = _ew_tile(rows, cols, 7)

    def body(w_ref, g_ref, m_ref, v_ref, d_ref, mo_ref, vo_ref):
        d_ref[...], mo_ref[...], vo_ref[...] = _adamw_math(w_ref[...], g_ref[...], m_ref[...], v_ref[...])

    spec = pl.BlockSpec((tr, cols), lambda i: (i, 0))
    return pl.pallas_call(
        body, name="adamw_" + name, grid=(rows // tr,),
        in_specs=[spec] * 4, out_specs=[spec] * 3,
        out_shape=[jax.ShapeDtypeStruct((rows, cols), F32)] * 3,
        compiler_params=_cp("parallel"),
    )(w, g, m, v)


DEPTH = 2
BIG_SHAPES = {"w_in": ((1024, 6560), 1), "w_conv_out": ((512, 1024), 1), "w_uq": ((256, 768), 1),
              "w_ukv": ((128, 1024), 1), "w_attn_out": ((512, 1024), 1), "w_hgrn_out": ((512, 1024), 1),
              "w_out": ((1024, 1024), 0), "w_ff1": ((1024, 4096), 1), "w_ff2": ((4096, 1024), 0)}
BIG = tuple(BIG_SHAPES)
SMALL_SIZES = {"norm1_g": 1024, "conv_b": 512, "conv_ln_g": 512, "conv_ln_b": 512, "q_a_norm_g": 256,
               "kv_a_norm_g": 128, "q_norm_g": 96, "k_norm_g": 96, "hgrn_lb_logits": 512, "hgrn_norm_g": 512,
               "norm2_g": 1024}
SMALL = tuple(SMALL_SIZES)
W_IN_COLS = 6560
W_IN_SHARD = W_IN_COLS // 4
W_IN_SEGS = ((0, 1024, SEG_AG[0]), (1024, 1280, SEG_CQ[0]), (1280, 1408, SEG_CKV[0]), (1408, 1440, SEG_KR[0] + 64),
             (1440, 3488, SEG_H4[0]), (3488, 6560, SEG_GATES[0]))


def _pad_heads(w, nh, used, axis):
    shp = w.shape
    w = w.reshape(shp[:axis] + (nh, used) + shp[axis + 1:])
    pad = [(0, 0)] * w.ndim
    pad[axis + 1] = (0, LANE - used)
    w = jnp.pad(w, pad)
    return w.reshape(shp[:axis] + (nh * LANE,) + shp[axis + 1:])


def _unpad_heads(w, nh, used, axis):
    shp = w.shape
    w = w.reshape(shp[:axis] + (nh, LANE) + shp[axis + 1:])
    w = lax.slice_in_dim(w, 0, used, axis=axis + 1)
    return w.reshape(shp[:axis] + (nh * used,) + shp[axis + 1:])


def _w_in_from_chips(p4):
    def orig(a, b):
        out = []
        while a < b:
            s = a // W_IN_SHARD
            e = min(b, (s + 1) * W_IN_SHARD)
            out.append(p4[s][:, a - W_IN_SHARD * s:e - W_IN_SHARD * s])
            a = e
        return out
    zc = lambda n: jnp.zeros((D, n), p4[0].dtype)
    parts = (orig(3488, 6560) + orig(0, 1024) + orig(1440, 3488) + orig(1024, 1280) + orig(1280, 1408)
             + [zc(64)] + orig(1408, 1440) + [zc(32)])
    return jnp.concatenate(parts, axis=1)


def _w_in_grad_to_chips(dw):
    chips = []
    for s in range(4):
        a, b = W_IN_SHARD * s, W_IN_SHARD * (s + 1)
        parts = []
        for o0, o1, p0 in W_IN_SEGS:
            lo, hi = max(a, o0), min(b, o1)
            if lo < hi:
                parts.append(dw[:, p0 + lo - o0:p0 + hi - o0])
        chips.append(jnp.concatenate(parts, axis=1))
    return jnp.stack(chips)


def _cat_chips(p4, axis):
    return jnp.concatenate([p4[s] for s in range(4)], axis=axis)


EARLY = ("w_in", "w_uq", "w_ukv")
LATE = tuple(k for k in BIG if k not in EARLY)


def _prep_late(pieces):
    pc = lambda k: [pieces[k][s].astype(_MM) for s in range(4)]
    return dict(wao=_pad_heads(_cat_chips(pc("w_attn_out"), 1), NH, 64, 0), wco=_cat_chips(pc("w_conv_out"), 1),
                who=_cat_chips(pc("w_hgrn_out"), 1), wout=_cat_chips(pc("w_out"), 0),
                w1=_cat_chips(pc("w_ff1"), 1), w2=_cat_chips(pc("w_ff2"), 0))


def _prep_early(pieces, small, l):
    mm = lambda a: a.astype(_MM)
    pc = lambda k: [mm(pieces[k][s]) for s in range(4)]
    w_in_p = _w_in_from_chips(pc("w_in"))
    wuq = jnp.concatenate([_pad_heads(pc("w_uq")[s], 2, QK_DIM, 1) for s in range(4)], axis=1)
    wukv = _cat_chips(pc("w_ukv"), 1).reshape(128, NH, 128)
    wk = _pad_heads(wukv[:, :, :64].reshape(128, NH * 64), NH, 64, 1)
    wv = _pad_heads(wukv[:, :, 64:].reshape(128, NH * 64), NH, 64, 1)
    row = lambda a: a.astype(F32).reshape(1, -1)
    p = dict(
        w_in=w_in_p, w_in_t=w_in_p.T, wuq=wuq, wk=wk, wv=wv,
        g1=row(small["norm1_g"][l]), g2=row(small["norm2_g"][l]),
        cw=jnp.pad(small["conv_w"][l].astype(F32), ((0, 1), (0, 0))), cb=row(small["conv_b"][l]),
        lng=row(small["conv_ln_g"][l]), lnb=row(small["conv_ln_b"][l]),
        qag=row(small["q_a_norm_g"][l]), kvag=row(small["kv_a_norm_g"][l]),
        qng=jnp.pad(row(small["q_norm_g"][l]), ((0, 0), (0, LANE - QK_DIM))),
        kng=jnp.pad(row(small["k_norm_g"][l]), ((0, 0), (0, LANE - QK_DIM))),
        ng=row(small["hgrn_norm_g"][l]),
    )
    return p


def _rope_tables(t):
    pos = (jnp.arange(t, dtype=jnp.int32) - FRONT).astype(F32)
    inv_freq = 10000.0 ** (-jnp.arange(16, dtype=F32) / 16)
    ang = pos[:, None] * inv_freq[None, :]
    cos, sin = jnp.cos(ang), jnp.sin(ang)
    one = jnp.ones((t, 64), F32)
    z16, z32, z64 = jnp.zeros((t, 16), F32), jnp.zeros((t, 32), F32), jnp.zeros((t, 64), F32)
    c = jnp.concatenate([one, cos, cos, z32], axis=1)
    s1 = jnp.concatenate([z64, -sin, z16, z32], axis=1)
    s2 = jnp.concatenate([z64, z16, sin, z32], axis=1)
    return c, s1, s2


def _layer_fwd(x, p, lb, rope, t_end, plan=None, on_rode=None):
    gates, ag, h4, cq, ckv, kr, hb = _in_proj_fwd(x, p["g1"], p["w_in"])
    z = _conv_fwd(ag, p["cw"], p["cb"])
    q, k, v, cqn, ckvn = _mla_pre_fwd(cq, ckv, kr, p["qag"], p["wuq"], p["kvag"], p["wk"], p["wv"], p["qng"],
                                      p["kng"], *rope)
    (o_att, lse), rode = _attn_fwd(q, k, v, plan)
    if on_rode is not None:
        on_rode(rode)
    o_h, states = _hgrn_fwd(h4, lb)
    x1, mix, ca, oc, ya, yb, yc = _mix_fwd(x, z, o_att, o_h, h4, gates, p["lng"], p["lnb"], p["wco"], p["wao"],
                                           p["ng"], p["who"], p["wout"], t_end)
    x2, pre = _ffn_fwd(x1, p["g2"], p["w1"], p["w2"])
    saved = dict(x=x, gates=gates, ag=ag, h4=h4, cq=cq, ckv=ckv, kr=kr, hb=hb, z=z, q=q, k=k, v=v, cqn=cqn,
                 ckvn=ckvn, o_att=o_att, lse=lse, o_h=o_h, states=states, x1=x1, mix=mix, ca=ca, oc=oc,
                 ya=ya, yb=yb, yc=yc, pre=pre)
    return x2, saved


def _layer_bwd(dx2, s, p, lb, rope, t_end, rides=None):
    rides = rides or {}
    (dx1, h2, act, dp, dg2), rode = _ffn_bwd(dx2, s["x1"], s["pre"], p["g2"], p["w1"].T, p["w2"].T,
                                             rides.get("ffn"))
    g = {"w_ff1": _wgrad(h2, dp, "ff1", 4), "w_ff2": _wgrad(act, dx2, "ff2").reshape(4, D_FF // 4, D),
         "norm2_g": dg2.sum(0)}
    plan_mix = rides["mix"](rode, g) if "mix" in rides else None
    (dgt, dya, dyb, dyc, dz, doa, doh, dhg, dlng, dlnb, dcb, dng), rode = _mix_bwd(
        dx1, s["ya"], s["yb"], s["yc"], s["gates"], s["z"], s["o_h"], s["h4"], p["lng"], p["lnb"], p["ng"],
        p["wout"], p["wco"], p["wao"], p["who"], plan_mix)
    g["w_out"] = _wgrad(s["mix"], dx1, "out").reshape(4, D // 4, D)
    g["w_conv_out"] = _wgrad(s["ca"], dya, "conv_out", 4)
    g["w_attn_out"] = _unpad_heads(_wgrad(s["o_att"], dyb, "attn_out", 4), NH, 64, 1)
    g["w_hgrn_out"] = _wgrad(s["oc"], dyc, "hgrn_out", 4)
    g["conv_ln_g"], g["conv_ln_b"], g["conv_b"], g["hgrn_norm_g"] = dlng.sum(0), dlnb.sum(0), dcb.sum(0), dng.sum(0)
    da, dg, dcw = _conv_bwd(s["ag"], p["cw"], dz)
    g["conv_w"] = dcw[:CONV_K]
    plan_attn = rides["attn"](rode, g) if "attn" in rides else None
    (dq, dk, dv), rode_attn = _attn_bwd(s["q"], s["k"], s["v"], s["o_att"], s["lse"], doa, plan_attn)
    dcq, dckv, dkr, dqraw, dkraw, dqag, dkvag, dqng, dkng = _mla_pre_bwd(
        dq, dk, dv, s["cq"], s["ckv"], s["kr"], p["qag"], p["wuq"], p["kvag"], p["wk"], p["wv"], p["qng"],
        p["kng"], *rope)
    g["w_uq"] = _unpad_heads(_wgrad(s["cqn"], dqraw, "uq", 4), 2, QK_DIM, 2)
    dwk = _unpad_heads(_wgrad(s["ckvn"], dkraw, "uk"), NH, 64, 1).reshape(128, NH, 64)
    dwv = _unpad_heads(_wgrad(s["ckvn"], dv, "uv"), NH, 64, 1).reshape(128, NH, 64)
    g["w_ukv"] = jnp.concatenate([dwk, dwv], axis=2).reshape(128, 4, 256).transpose(1, 0, 2)
    g["q_a_norm_g"], g["kv_a_norm_g"] = dqag.sum(0), dkvag.sum(0)
    g["q_norm_g"], g["k_norm_g"] = dqng.sum(0)[:QK_DIM], dkng.sum(0)[:QK_DIM]
    plan_hgrn = rides["hgrn"](rode_attn) if "hgrn" in rides else None
    (dhq, dhf, dhi, dlb), rode_hgrn = _hgrn_bwd(s["h4"], lb, doh, s["states"], plan_hgrn)
    mm = lambda a: a.astype(_MM)
    du = jnp.concatenate([dgt, mm(da), mm(dg), mm(dhq), mm(dhf), mm(dhi), dhg, dcq, dckv, dkr], axis=1)
    dx, dg1 = _in_proj_bwd(du, s["x"], dx1, p["g1"], p["w_in_t"], t_end)
    g["norm1_g"] = dg1.sum(0)
    g["w_in"] = _w_in_grad_to_chips(_wgrad(s["hb"], du, "in"))
    return dx, g, dlb.sum(0), (rode_attn, rode_hgrn)


def _device_step(x, target, small, pieces0, pieces1=None, fwd_ride=None, bwd_rides=None):
    s_real = x.shape[0]
    t_end = ROW0 + s_real
    t = -(-t_end // LANE) * LANE
    zrow = lambda n: jnp.zeros((n, D), F32)
    xp = jnp.concatenate([zrow(FRONT), small["meta"].astype(F32), x, zrow(t - t_end)], axis=0)
    tp = jnp.concatenate([zrow(ROW0), target, zrow(t - t_end)], axis=0)
    rope = _rope_tables(t)
    logits = small["hgrn_lb_logits"].astype(F32)
    lbs = _lower_bounds_fwd(logits)
    prm0 = _prep_early(pieces0, small, 0)
    got = {}
    if fwd_ride is None:
        prm0.update(_prep_late(pieces0))
        h, sv0 = _layer_fwd(xp, prm0, lbs[0:1], rope, t_end)
    else:
        def on_rode(rode):
            late0, got["pieces1"] = fwd_ride[1](rode)
            prm0.update(_prep_late(late0))
        h, sv0 = _layer_fwd(xp, prm0, lbs[0:1], rope, t_end, fwd_ride[0], on_rode)
        pieces1 = got["pieces1"]
    prm1 = _prep_early(pieces1, small, 1)
    if fwd_ride is None:
        prm1.update(_prep_late(pieces1))
        h, sv1 = _layer_fwd(h, prm1, lbs[1:2], rope, t_end)
    else:
        h, sv1 = _layer_fwd(h, prm1, lbs[1:2], rope, t_end, fwd_ride[2],
                            lambda rode: prm1.update(_prep_late(fwd_ride[3](rode))))
    dh, lsum = _loss_head(h, tp, t_end)
    loss = jnp.sum(lsum)
    rides1, make_rides0 = (None, None) if bwd_rides is None else bwd_rides
    dh, g1, dlb1, rode1 = _layer_bwd(dh, sv1, prm1, lbs[1:2], rope, t_end, rides1)
    dh, g0, dlb0, rode = _layer_bwd(dh, sv0, prm0, lbs[0:1], rope, t_end,
                                    None if make_rides0 is None else make_rides0(g1, rode1))
    dlogits = _lower_bounds_bwd(logits, jnp.stack([dlb0, dlb1]))
    grads = [g0, g1]
    for l in range(DEPTH):
        grads[l]["hgrn_lb_logits"] = dlogits[l]
    return loss, dh[ROW0:t_end], grads, dh[FRONT:ROW0], rode


MESH = pl.DeviceIdType.MESH
_ANY = pl.BlockSpec(memory_space=pl.ANY)
SMALL_ROWS = 64
SMALL_LEN = SMALL_ROWS * 1024


def _mesh_pos():
    return lax.axis_index("x"), lax.axis_index("y"), lax.axis_index("c")


def _other_chips(x, y):
    return [(1 - x, y), (x, 1 - y), (1 - x, 1 - y)]


class _Plan:
    def __init__(self, name, ins, out_shapes, sems, start, finish, relay=None):
        self.name, self.ins, self.out_shapes, self.sems = name, list(ins), list(out_shapes), list(sems)
        self.start, self.finish, self.relay = start, finish, relay


def _run_plan(plan):
    ni, no = len(plan.ins), len(plan.out_shapes)

    def body(*refs):
        ins, outs, sems = refs[:ni], refs[ni:ni + no], refs[ni + no:]
        plan.start(ins, outs, sems)
        if plan.relay is not None:
            plan.relay(ins, outs, sems)
        plan.finish(ins, outs, sems)

    return pl.pallas_call(body, name=plan.name, in_specs=[_ANY] * ni, out_specs=[_ANY] * no,
                          out_shape=plan.out_shapes, scratch_shapes=plan.sems)(*plan.ins)


def _plan_specs(plan):
    if plan is None:
        return [], [], [], [], []
    return plan.ins, [_ANY] * len(plan.ins), [_ANY] * len(plan.out_shapes), plan.out_shapes, plan.sems


def _host_refs(refs, n_in, n_out, n_scratch, plan):
    ni = 0 if plan is None else len(plan.ins)
    no = 0 if plan is None else len(plan.out_shapes)
    o0 = n_in + ni
    s0 = o0 + n_out + no
    own = (refs[:n_in], refs[o0:o0 + n_out], refs[s0:s0 + n_scratch])
    rider = (refs[n_in:o0], refs[o0 + n_out:s0], refs[s0 + n_scratch:])
    return own, rider


def _ride(plan, rider, step, last):
    if plan is None:
        return lambda: None

    @pl.when(step == 0)
    def _():
        plan.start(*rider)

    def done():
        if plan.relay is not None:
            @pl.when(step == last - 1)
            def _():
                plan.relay(*rider)

        @pl.when(step == last)
        def _():
            plan.finish(*rider)
    return done


def _merge_plans(name, plans):
    def parts(ins, outs, sems):
        i = o = s = 0
        for p in plans:
            ni, no, ns = len(p.ins), len(p.out_shapes), len(p.sems)
            yield p, (ins[i:i + ni], outs[o:o + no], sems[s:s + ns])
            i, o, s = i + ni, o + no, s + ns

    def start(ins, outs, sems):
        for p, refs in parts(ins, outs, sems):
            p.start(*refs)

    def relay(ins, outs, sems):
        for p, refs in parts(ins, outs, sems):
            if p.relay is not None:
                p.relay(*refs)

    def finish(ins, outs, sems):
        for p, refs in parts(ins, outs, sems):
            p.finish(*refs)

    return _Plan(name, [a for p in plans for a in p.ins], [a for p in plans for a in p.out_shapes],
                 [a for p in plans for a in p.sems], start, finish, relay)


def _plan_gather(own, layer, name):
    nw = len(own)

    def copies(ins, outs, sems):
        send_sems, recv_sems = sems

        def over_ici(w, j, chip_of_data, to):
            return pltpu.make_async_remote_copy(
                src_ref=ins[w].at[layer], dst_ref=outs[w].at[chip_of_data], send_sem=send_sems.at[w, j],
                recv_sem=recv_sems.at[w, j], device_id=to, device_id_type=MESH)

        def over_d2d(w, j, chip_of_data, to):
            return pltpu.make_async_remote_copy(
                src_ref=outs[w].at[chip_of_data], dst_ref=outs[w].at[chip_of_data], send_sem=send_sems.at[w, 3 + j],
                recv_sem=recv_sems.at[w, 3 + j], device_id=to, device_id_type=MESH)
        return over_ici, over_d2d

    def start(ins, outs, sems):
        x, y, c = _mesh_pos()
        over_ici, _ = copies(ins, outs, sems)

        @pl.when(c == layer)
        def _():
            for j, (px, py) in enumerate(_other_chips(x, y)):
                for w in range(nw):
                    over_ici(w, j, 2 * x + y, (px, py, layer)).start()

    def relay(ins, outs, sems):
        x, y, c = _mesh_pos()
        over_ici, over_d2d = copies(ins, outs, sems)

        @pl.when(c == layer)
        def _():
            for j, (px, py) in enumerate(_other_chips(x, y)):
                for w in range(nw):
                    over_ici(w, j, 2 * px + py, (x, y, c)).wait_recv()
                    over_d2d(w, j, 2 * px + py, (x, y, 1 - layer)).start()

    def finish(ins, outs, sems):
        x, y, c = _mesh_pos()
        over_ici, over_d2d = copies(ins, outs, sems)
        chips = _other_chips(x, y)

        @pl.when(c == layer)
        def _():
            for j, (px, py) in enumerate(chips):
                for w in range(nw):
                    over_ici(w, j, 2 * x + y, (px, py, layer)).wait_send()
                    over_d2d(w, j, 2 * px + py, (x, y, 1 - layer)).wait_send()

        @pl.when(c != layer)
        def _():
            for j, (px, py) in enumerate(chips):
                for w in range(nw):
                    over_d2d(w, j, 2 * px + py, (x, y, c)).wait_recv()

    return _Plan(name, own,
                 [jax.ShapeDtypeStruct((4,) + a.shape[1:], a.dtype) for a in own],
                 [pltpu.SemaphoreType.DMA((nw, 6)), pltpu.SemaphoreType.DMA((nw, 6))], start, finish, relay)


def _plan_to_sibling(arrs, layer, name):
    nw = len(arrs)

    def copy(ins, outs, sems, w):
        x, y, _ = _mesh_pos()
        return pltpu.make_async_remote_copy(src_ref=ins[w], dst_ref=outs[w], send_sem=sems[0].at[w],
                                            recv_sem=sems[1].at[w], device_id=(x, y, layer), device_id_type=MESH)

    def start(ins, outs, sems):
        @pl.when(lax.axis_index("c") != layer)
        def _():
            for w in range(nw):
                copy(ins, outs, sems, w).start()

    def finish(ins, outs, sems):
        c = lax.axis_index("c")

        @pl.when(c != layer)
        def _():
            for w in range(nw):
                copy(ins, outs, sems, w).wait_send()

        @pl.when(c == layer)
        def _():
            for w in range(nw):
                copy(ins, outs, sems, w).wait_recv()

    return _Plan(name, arrs, [jax.ShapeDtypeStruct(a.shape, a.dtype) for a in arrs],
                 [pltpu.SemaphoreType.DMA((nw,)), pltpu.SemaphoreType.DMA((nw,))], start, finish)


def _plan_scatter(parts, layer, name):
    nw = len(parts)

    def start(ins, outs, sems):
        x, y, c = _mesh_pos()

        @pl.when(c == layer)
        def _():
            for j, (px, py) in enumerate(_other_chips(x, y)):
                for w in range(nw):
                    pltpu.make_async_remote_copy(
                        src_ref=ins[w].at[2 * px + py], dst_ref=outs[w].at[2 * x + y], send_sem=sems[0].at[w, j],
                        recv_sem=sems[1].at[w, j], device_id=(px, py, layer), device_id_type=MESH).start()

    def finish(ins, outs, sems):
        x, y, c = _mesh_pos()

        @pl.when(c == layer)
        def _():
            for j, (px, py) in enumerate(_other_chips(x, y)):
                for w in range(nw):
                    pltpu.make_async_remote_copy(
                        src_ref=ins[w].at[2 * px + py], dst_ref=outs[w].at[2 * px + py], send_sem=sems[0].at[w, j],
                        recv_sem=sems[1].at[w, j], device_id=(x, y, c), device_id_type=MESH).wait()

    return _Plan(name, parts, [jax.ShapeDtypeStruct(a.shape, a.dtype) for a in parts],
                 [pltpu.SemaphoreType.DMA((nw, 3)), pltpu.SemaphoreType.DMA((nw, 3))], start, finish)


def _sibling_exchange(reds0, reds1):
    nw = len(reds0)

    def body(*refs):
        a0, a1, outs = refs[:nw], refs[nw:2 * nw], refs[2 * nw:3 * nw]
        send_sems, recv_sems = refs[3 * nw:]
        x, y, c = _mesh_pos()

        def copy(w, src):
            return pltpu.make_async_remote_copy(src_ref=src, dst_ref=outs[w], send_sem=send_sems.at[w],
                                                recv_sem=recv_sems.at[w], device_id=(x, y, 1 - c),
                                                device_id_type=MESH)

        @pl.when(c == 0)
        def _():
            for w in range(nw):
                copy(w, a0[w]).start()

        @pl.when(c == 1)
        def _():
            for w in range(nw):
                copy(w, a1[w]).start()

        for w in range(nw):
            copy(w, a0[w]).wait()

    return pl.pallas_call(
        body, name="sibling_exchange", in_specs=[_ANY] * (2 * nw), out_specs=[_ANY] * nw,
        out_shape=[jax.ShapeDtypeStruct(a.shape, a.dtype) for a in reds0],
        scratch_shapes=[pltpu.SemaphoreType.DMA((nw,)), pltpu.SemaphoreType.DMA((nw,))],
    )(*reds0, *reds1)


def _all_reduce_small(v, name):
    rows, cols = v.shape

    def body(v_ref, o_ref, slots, send_sems, recv_sems):
        x, y, c = _mesh_pos()
        me = 4 * x + 2 * y + c
        slots[me] = v_ref[...]
        peers = []
        for rel in range(1, 8):
            fx, fy, fc = (rel >> 2) & 1, (rel >> 1) & 1, rel & 1
            px = 1 - x if fx else x
            py = 1 - y if fy else y
            pc = 1 - c if fc else c
            peers.append((px, py, pc))
        cps = [pltpu.make_async_remote_copy(src_ref=v_ref, dst_ref=slots.at[me], send_sem=send_sems.at[k],
                                            recv_sem=recv_sems.at[k], device_id=peer, device_id_type=MESH)
               for k, peer in enumerate(peers)]
        for cp in cps:
            cp.start()
        for k, (px, py, pc) in enumerate(peers):
            pltpu.make_async_remote_copy(src_ref=v_ref, dst_ref=slots.at[4 * px + 2 * py + pc],
                                         send_sem=send_sems.at[k], recv_sem=recv_sems.at[k], device_id=(x, y, c),
                                         device_id_type=MESH).wait_recv()
        for cp in cps:
            cp.wait_send()
        acc = slots[0]
        for d in range(1, 8):
            acc = acc + slots[d]
        o_ref[...] = acc

    vm = pl.BlockSpec(memory_space=pltpu.VMEM)
    return pl.pallas_call(
        body, name=name, in_specs=[vm], out_specs=vm,
        out_shape=jax.ShapeDtypeStruct((rows, cols), F32),
        scratch_shapes=[pltpu.VMEM((8, rows, cols), F32), pltpu.SemaphoreType.DMA((7,)),
                        pltpu.SemaphoreType.DMA((7,))],
    )(v)


def _add_to_wire(a, b, name):
    n4, r, c_ = a.shape
    rows = n4 * r
    tr = _ew_tile(rows, c_, 3)

    def body(a_ref, b_ref, o_ref):
        o_ref[...] = (a_ref[...] + b_ref[...]).astype(o_ref.dtype)

    spec = pl.BlockSpec((tr, c_), lambda i: (i, 0))
    out = pl.pallas_call(
        body, name="add_to_wire_" + name, grid=(rows // tr,), in_specs=[spec, spec], out_specs=spec,
        out_shape=jax.ShapeDtypeStruct((rows, c_), jnp.bfloat16), compiler_params=_cp("parallel"),
    )(a.reshape(rows, c_), b.reshape(rows, c_))
    return out.reshape(n4, r, c_)


def _sum_chips(recv, wire, name):
    _, r, c_ = recv.shape
    tr = _ew_tile(r, c_, 6)

    def body(r_ref, w_ref, o_ref):
        chip = 2 * lax.axis_index("x") + lax.axis_index("y")
        acc = None
        for s in range(4):
            term = jnp.where(chip == s, w_ref[s], r_ref[s]).astype(F32)
            acc = term if acc is None else acc + term
        o_ref[...] = acc

    blk = pl.BlockSpec((4, tr, c_), lambda i: (0, i, 0))
    return pl.pallas_call(
        body, name="sum_chips_" + name, grid=(r // tr,),
        in_specs=[blk, blk], out_specs=pl.BlockSpec((tr, c_), lambda i: (i, 0)),
        out_shape=jax.ShapeDtypeStruct((r, c_), F32),
        compiler_params=_cp("parallel"),
    )(recv, wire)


def _pack_small(vals, meta_full, conv_w_full):
    flat = jnp.concatenate([vals[k].reshape(-1) for k in SMALL] + [meta_full.reshape(-1), conv_w_full.reshape(-1)])
    return jnp.pad(flat, (0, SMALL_LEN - flat.shape[0])).reshape(SMALL_ROWS, 1024)


def _unpack_small(buf):
    flat = buf.reshape(-1)
    out, off = {}, 0
    for k in SMALL:
        n = DEPTH * SMALL_SIZES[k]
        out[k] = flat[off:off + n].reshape(DEPTH, SMALL_SIZES[k])
        off += n
    meta = flat[off:off + N_META * D].reshape(N_META, D)
    off += N_META * D
    conv_w = flat[off:off + DEPTH * CONV_K * CONV_DIM].reshape(DEPTH, CONV_K, CONV_DIM)
    return out, meta, conv_w


def kernel(x, meta, norm1_g, w_in, conv_w, conv_b, conv_ln_g, conv_ln_b, w_conv_out, q_a_norm_g, w_uq, kv_a_norm_g, w_ukv, q_norm_g, k_norm_g, w_attn_out, hgrn_lb_logits, hgrn_norm_g, w_hgrn_out, w_out, norm2_g, w_ff1, w_ff2, loss_target, m_meta, m_norm1_g, m_w_in, m_conv_w, m_conv_b, m_conv_ln_g, m_conv_ln_b, m_w_conv_out, m_q_a_norm_g, m_w_uq, m_kv_a_norm_g, m_w_ukv, m_q_norm_g, m_k_norm_g, m_w_attn_out, m_hgrn_lb_logits, m_hgrn_norm_g, m_w_hgrn_out, m_w_out, m_norm2_g, m_w_ff1, m_w_ff2, v_meta, v_norm1_g, v_w_in, v_conv_w, v_conv_b, v_conv_ln_g, v_conv_ln_b, v_w_conv_out, v_q_a_norm_g, v_w_uq, v_kv_a_norm_g, v_w_ukv, v_q_norm_g, v_k_norm_g, v_w_attn_out, v_hgrn_lb_logits, v_hgrn_norm_g, v_w_hgrn_out, v_w_out, v_norm2_g, v_w_ff1, v_w_ff2):
    names = ("meta", "norm1_g", "w_in", "conv_w", "conv_b", "conv_ln_g", "conv_ln_b", "w_conv_out", "q_a_norm_g",
             "w_uq", "kv_a_norm_g", "w_ukv", "q_norm_g", "k_norm_g", "w_attn_out", "hgrn_lb_logits", "hgrn_norm_g",
             "w_hgrn_out", "w_out", "norm2_g", "w_ff1", "w_ff2")
    w = dict(zip(names, (meta, norm1_g, w_in, conv_w, conv_b, conv_ln_g, conv_ln_b, w_conv_out, q_a_norm_g, w_uq,
                         kv_a_norm_g, w_ukv, q_norm_g, k_norm_g, w_attn_out, hgrn_lb_logits, hgrn_norm_g, w_hgrn_out,
                         w_out, norm2_g, w_ff1, w_ff2)))
    m = dict(zip(names, (m_meta, m_norm1_g, m_w_in, m_conv_w, m_conv_b, m_conv_ln_g, m_conv_ln_b, m_w_conv_out,
                         m_q_a_norm_g, m_w_uq, m_kv_a_norm_g, m_w_ukv, m_q_norm_g, m_k_norm_g, m_w_attn_out,
                         m_hgrn_lb_logits, m_hgrn_norm_g, m_w_hgrn_out, m_w_out, m_norm2_g, m_w_ff1, m_w_ff2)))
    v = dict(zip(names, (v_meta, v_norm1_g, v_w_in, v_conv_w, v_conv_b, v_conv_ln_g, v_conv_ln_b, v_w_conv_out,
                         v_q_a_norm_g, v_w_uq, v_kv_a_norm_g, v_w_ukv, v_q_norm_g, v_k_norm_g, v_w_attn_out,
                         v_hgrn_lb_logits, v_hgrn_norm_g, v_w_hgrn_out, v_w_out, v_norm2_g, v_w_ff1, v_w_ff2)))
    cx, cy, cc = _mesh_pos()
    chip = 2 * cx + cy
    zero = jnp.zeros((), jnp.int32)

    own = {k: w[k].astype(_MM) for k in BIG}

    def as_pieces(names, gathered, layer):
        return {k: [jnp.where(chip == s, own[k][layer], g[s]) for s in range(4)] for k, g in zip(names, gathered)}

    pieces0 = as_pieces(EARLY, _run_plan(_plan_gather([own[k] for k in EARLY], 0, "gather_l0_early")), 0)
    fwd_ride = (_merge_plans("gather_mid", [_plan_gather([own[k] for k in LATE], 0, "gather_l0_late"),
                                            _plan_gather([own[k] for k in EARLY], 1, "gather_l1_early")]),
                lambda got: (as_pieces(LATE, got[:len(LATE)], 0), as_pieces(EARLY, got[len(LATE):], 1)),
                _plan_gather([own[k] for k in LATE], 1, "gather_l1_late"),
                lambda got: as_pieces(LATE, got, 1))
    meta_slab = lax.dynamic_update_slice(jnp.zeros((N_META, D), F32), meta, (zero, chip * (D // 4)))
    convw_slab = lax.dynamic_update_slice(jnp.zeros((DEPTH, CONV_K, CONV_DIM), F32), conv_w,
                                          (zero, zero, chip * (CONV_DIM // 4)))
    zsmall = {k: jnp.zeros((DEPTH, SMALL_SIZES[k]), F32) for k in SMALL}
    south = (cc == 0).astype(F32)
    _, meta_full, convw_full = _unpack_small(
        _all_reduce_small(_pack_small(zsmall, meta_slab, convw_slab) * south, "gather_small"))
    small = {k: w[k] for k in SMALL}
    small["meta"] = meta_full
    small["conv_w"] = convw_full

    FFN = ("w_ff1", "w_ff2")
    MID = ("w_out", "w_conv_out", "w_attn_out", "w_hgrn_out")
    REST = tuple(k for k in BIG if k not in FFN + MID)
    held = {}

    def to_wire(names, layer, mine, from_sibling):
        return lax.cond(
            cc == layer,
            lambda: [_add_to_wire(a, b, "%s_l%d" % (k, layer)) for k, a, b in zip(names, mine, from_sibling)],
            lambda: [jnp.zeros(a.shape, jnp.bfloat16) for a in mine])

    def chip_sum(names, layer, got, wire):
        return lax.cond(
            cc == layer,
            lambda: [_sum_chips(r, s, "%s_l%d" % (k, layer)) for k, r, s in zip(names, got, wire)],
            lambda: [jnp.zeros(s.shape[1:], F32) for s in wire])

    NONFFN = tuple(k for k in BIG if k not in FFN)

    def ride_attn_l1(_, g1):
        held["g1_ffn"] = [g1[k] for k in FFN]
        return _plan_to_sibling(held["g1_ffn"], 1, "swap_grads_l1_ffn")

    def rides_l0(g1, rode_l1):
        g1_rest = [g1[k] for k in NONFFN]

        def ride_mix(from_sibling1, g0_ffn):
            wire1 = dict(zip(FFN, to_wire(FFN, 1, held["g1_ffn"], rode_l1[0])))
            wire1.update(zip(NONFFN, to_wire(NONFFN, 1, g1_rest, from_sibling1)))
            held["wire1"] = [wire1[k] for k in BIG]
            held["g0_ffn"] = [g0_ffn[k] for k in FFN]
            return _plan_to_sibling(held["g0_ffn"], 0, "swap_grads_l0_ffn")

        def ride_attn(from_sibling0, g0):
            held["wire0_ffn"] = to_wire(FFN, 0, held["g0_ffn"], from_sibling0)
            held["g0_mid"] = [g0[k] for k in MID]
            return _merge_plans("exchange_grads_mid", [
                _plan_scatter(held["wire1"], 1, "scatter_grads_l1"),
                _plan_scatter(held["wire0_ffn"], 0, "scatter_grads_l0_ffn"),
                _plan_to_sibling(held["g0_mid"], 0, "swap_grads_l0_mid")])

        def ride_hgrn(rode_attn):
            held["wire0_mid"] = to_wire(MID, 0, held["g0_mid"], rode_attn[len(BIG) + len(FFN):])
            return _plan_scatter(held["wire0_mid"], 0, "scatter_grads_l0_mid")

        return {"ffn": _plan_to_sibling(g1_rest, 1, "swap_grads_l1_rest"), "mix": ride_mix, "attn": ride_attn,
                "hgrn": ride_hgrn}

    loss_share, grad_x, gl, g_meta, (got, got_mid) = _device_step(
        x[0], loss_target[0], small, pieces0, None, fwd_ride, ({"attn": ride_attn_l1}, rides_l0))

    reds1 = chip_sum(BIG, 1, got[:len(BIG)], held["wire1"])
    reds0 = dict(zip(FFN, chip_sum(FFN, 0, got[len(BIG):len(BIG) + len(FFN)], held["wire0_ffn"])))
    reds0.update(zip(MID, chip_sum(MID, 0, got_mid, held["wire0_mid"])))
    g0_rest = [gl[0][k] for k in REST]
    wire0 = to_wire(REST, 0, g0_rest, _run_plan(_plan_to_sibling(g0_rest, 0, "swap_grads_l0_rest")))
    reds0.update(zip(REST, chip_sum(REST, 0, _run_plan(_plan_scatter(wire0, 0, "scatter_grads_l0_rest")), wire0)))
    reds0 = [reds0[k] for k in BIG]
    reds_sibling = _sibling_exchange(reds0, reds1)
    grads, delta, new_m, new_v = {}, {}, {}, {}
    t_view = lambda k, a: jnp.swapaxes(a, -1, -2) if k == "w_in" else a
    for k, r0, r1, theirs in zip(BIG, reds0, reds1, reds_sibling):
        grads[k], delta[k], new_m[k], new_v[k] = [
            t_view(k, a) for a in _adamw_layers(t_view(k, w[k]), t_view(k, m[k]), t_view(k, v[k]), t_view(k, r0),
                                                t_view(k, r1), t_view(k, theirs), k)]

    g_small_local = {k: jnp.stack([gl[l][k] for l in range(DEPTH)]) for k in SMALL}
    g_convw_local = jnp.stack([gl[l]["conv_w"] for l in range(DEPTH)])
    reduced = _all_reduce_small(
        _pack_small(g_small_local, g_meta, g_convw_local).at[SMALL_ROWS - 1, 1023].set(loss_share), "reduce_small")
    loss = reduced[SMALL_ROWS - 1, 1023]
    g_small, g_meta_full, g_convw_full = _unpack_small(reduced)
    grads.update(g_small)
    grads["meta"] = lax.dynamic_slice(g_meta_full, (zero, chip * (D // 4)), (N_META, D // 4))
    grads["conv_w"] = lax.dynamic_slice(g_convw_full, (zero, zero, chip * (CONV_DIM // 4)),
                                        (DEPTH, CONV_K, CONV_DIM // 4))

    def small_pack(src):
        return _pack_small(src, jnp.pad(src["meta"], ((0, 0), (0, D - D // 4))),
                           jnp.pad(src["conv_w"], ((0, 0), (0, 0), (0, CONV_DIM - CONV_DIM // 4))))

    def small_unpack(buf):
        out, meta_p, convw_p = _unpack_small(buf)
        out["meta"] = meta_p[:, :D // 4]
        out["conv_w"] = convw_p[:, :, :CONV_DIM // 4]
        return out

    d_s, m_s, v_s = [small_unpack(a) for a in _adamw(small_pack(w), small_pack(grads), small_pack(m),
                                                     small_pack(v), "small")]
    delta.update(d_s)
    new_m.update(m_s)
    new_v.update(v_s)
    return (loss, grad_x[None], *[grads[k] for k in names], *[delta[k] for k in names],
            *[new_m[k] for k in names], *[new_v[k] for k in names])
```

```python
import functools

import jax
import jax.numpy as jnp
from jax import lax
from jax.experimental import pallas as pl
from jax.experimental.pallas import tpu as pltpu

F32 = jnp.float32
_MM = jnp.bfloat16

D = 1024
N_META = 16
FRONT = 48
ROW0 = FRONT + N_META
EPS = 1e-6
GATE_CLAMP = 1.0 - 1e-6
CONV_K = 31
CONV_DIM = 512
NH = 8
QK_DIM = 96
ATT_SCALE = QK_DIM ** -0.5
HH = 4
CHUNK = 64
SUB = 16
EXP_CLIP = 60.0
NEG = -1e30
LANE = 128

SEG_GATES = (0, 3072)
SEG_AG = (3072, 4096)
SEG_H4 = (4096, 6144)
SEG_CQ = (6144, 6400)
SEG_CKV = (6400, 6528)
SEG_KR = (6528, 6656)
N_IN_P = 6656

ADAM_LR = 0.001
ADAM_B1 = 0.9
ADAM_B2 = 0.999
ADAM_EPS = 1e-08
ADAM_WD = 0.01
ADAM_STEP = 10

VMEM_LIMIT = 56 * 1024 * 1024


def _tile(n, pref):
    best = 64
    for t in range(64, pref + 1, 64):
        if n % t == 0:
            best = t
    return best


def _cp(*sem):
    return pltpu.CompilerParams(dimension_semantics=tuple(sem), vmem_limit_bytes=VMEM_LIMIT)


def _row(tm, n, col=0):
    return pl.BlockSpec((tm, n), lambda i: (i, col))


def _full(shape):
    return pl.BlockSpec(shape, lambda i: (0,) * len(shape))


def _mm(a, b):
    return jnp.dot(a.astype(_MM), b.astype(_MM), preferred_element_type=F32)


def _mm_nt(a, b):
    return lax.dot_general(a.astype(_MM), b.astype(_MM), (((1,), (1,)), ((), ())), preferred_element_type=F32)


def _mm_tn(a, b):
    return lax.dot_general(a.astype(_MM), b.astype(_MM), (((0,), (0,)), ((), ())), preferred_element_type=F32)


def _split3(x):
    hi = x.astype(jnp.bfloat16)
    return hi, (x - hi.astype(F32)).astype(jnp.bfloat16)


def _dot3(a, b, dims):
    ah, al = _split3(a)
    bh, bl = _split3(b)
    dg = lambda u, v: lax.dot_general(u, v, (dims, ((), ())), preferred_element_type=F32)
    return dg(ah, bh) + (dg(ah, bl) + dg(al, bh))


def _hmm(a, b):
    return _dot3(a, b, ((1,), (0,)))


def _hmm_nt(a, b):
    return _dot3(a, b, ((1,), (1,)))


def _hmm_tn(a, b):
    return _dot3(a, b, ((0,), (0,)))


def _sigmoid(x):
    return 1.0 / (1.0 + jnp.exp(-x))


def _rstd(x, n=None):
    n = x.shape[-1] if n is None else n
    return lax.rsqrt(jnp.sum(x * x, axis=-1, keepdims=True) * (1.0 / n) + EPS)


def _rms_bwd(dy, x, rstd, g, n=None):
    n = x.shape[-1] if n is None else n
    xh = x * rstd
    dxh = dy * g
    dx = rstd * (dxh - xh * (jnp.sum(dxh * xh, axis=-1, keepdims=True) * (1.0 / n)))
    return dx, dy * xh


def _valid_rows(i, tm, t_valid_end):
    r = i * tm + lax.broadcasted_iota(jnp.int32, (tm, 1), 0)
    return ((r >= FRONT) & (r < t_valid_end)).astype(F32)


def _colsum8(x):
    n, c = x.shape
    return jnp.sum(x.reshape(n // 8, 8, c), axis=0)


def _in_proj_fwd(x, g1, w):
    t = x.shape[0]
    tm = _tile(t, 192)
    segs = (SEG_GATES, SEG_AG, SEG_H4, SEG_CQ, SEG_CKV, SEG_KR)

    def body(x_ref, g_ref, w_ref, gates_ref, ag_ref, h4_ref, cq_ref, ckv_ref, kr_ref, hb_ref):
        xv = x_ref[...]
        hb = (xv * _rstd(xv) * g_ref[...]).astype(_MM)
        hb_ref[...] = hb
        for ref, (a, b) in zip((gates_ref, ag_ref, h4_ref, cq_ref, ckv_ref, kr_ref), segs):
            ref[...] = jnp.dot(hb, w_ref[:, a:b], preferred_element_type=F32)

    outs = [jax.ShapeDtypeStruct((t, b - a), F32) for a, b in segs] + [jax.ShapeDtypeStruct((t, D), _MM)]
    return pl.pallas_call(
        body, name="in_proj_fwd", grid=(t // tm,),
        in_specs=[_row(tm, D), _full((1, D)), _full((D, N_IN_P))],
        out_specs=[_row(tm, b - a) for a, b in segs] + [_row(tm, D)],
        out_shape=outs, compiler_params=_cp("parallel"),
    )(x, g1, w)


def _in_proj_bwd(du, x, dx1, g1, wt, t_end):
    t = x.shape[0]
    tm = _tile(t, 384)

    def body(du_ref, x_ref, dx1_ref, g_ref, wt_ref, dx_ref, dg_ref):
        i = pl.program_id(0)
        dh = jnp.dot(du_ref[...], wt_ref[...], preferred_element_type=F32)
        xv = x_ref[...]
        dxn, dgrow = _rms_bwd(dh, xv, _rstd(xv), g_ref[...])
        dx_ref[...] = _valid_rows(i, tm, t_end) * (dx1_ref[...] + dxn)

        @pl.when(i == 0)
        def _():
            dg_ref[...] = jnp.zeros_like(dg_ref)
        dg_ref[...] += _colsum8(dgrow)

    return pl.pallas_call(
        body, name="in_proj_bwd", grid=(t // tm,),
        in_specs=[_row(tm, N_IN_P), _row(tm, D), _row(tm, D), _full((1, D)), _full((N_IN_P, D))],
        out_specs=[_row(tm, D), _full((8, D))],
        out_shape=[jax.ShapeDtypeStruct((t, D), F32), jax.ShapeDtypeStruct((8, D), F32)],
        compiler_params=_cp("arbitrary"),
    )(du, x, dx1, g1, wt)


CONV_CH = 128


def _conv_fwd(ag, cw, cb):
    t = ag.shape[0]
    n = t // CONV_CH

    def body(a_ref, g_ref, w_ref, b_ref, z_ref, hp):
        hp[0:32, :] = jnp.zeros((32, LANE), F32)

        def fill(i, c):
            r = pl.multiple_of(i * CONV_CH, CONV_CH)
            hp[pl.ds(32 + r, CONV_CH), :] = a_ref[pl.ds(r, CONV_CH), :] * _sigmoid(g_ref[pl.ds(r, CONV_CH), :])
            return c
        lax.fori_loop(0, n, fill, 0)

        def conv(i, c):
            r = pl.multiple_of(i * CONV_CH, CONV_CH)
            acc = jnp.broadcast_to(b_ref[...], (CONV_CH, LANE))
            for k in range(CONV_K):
                acc = acc + w_ref[k:k + 1, :] * hp[pl.ds(r + (k + 2), CONV_CH), :]
            z_ref[pl.ds(r, CONV_CH), :] = acc
            return c
        lax.fori_loop(0, n, conv, 0)

    nb = CONV_DIM // LANE
    return pl.pallas_call(
        body, name="conv_fwd", grid=(nb,),
        in_specs=[pl.BlockSpec((t, LANE), lambda j: (0, j)), pl.BlockSpec((t, LANE), lambda j: (0, nb + j)),
                  pl.BlockSpec((32, LANE), lambda j: (0, j)), pl.BlockSpec((1, LANE), lambda j: (0, j))],
        out_specs=pl.BlockSpec((t, LANE), lambda j: (0, j)),
        out_shape=jax.ShapeDtypeStruct((t, CONV_DIM), F32),
        scratch_shapes=[pltpu.VMEM((t + 32, LANE), F32)],
        compiler_params=_cp("parallel"),
    )(ag, ag, cw, cb)


def _conv_bwd(ag, cw, dz):
    t = ag.shape[0]
    n = t // CONV_CH

    def body(a_ref, g_ref, w_ref, dz_ref, da_ref, dg_ref, dcw_ref, hp, dzp, accw):
        hp[0:32, :] = jnp.zeros((32, LANE), F32)
        dzp[pl.ds(t, 32), :] = jnp.zeros((32, LANE), F32)
        accw[...] = jnp.zeros_like(accw)

        def fill(i, c):
            r = pl.multiple_of(i * CONV_CH, CONV_CH)
            hp[pl.ds(32 + r, CONV_CH), :] = a_ref[pl.ds(r, CONV_CH), :] * _sigmoid(g_ref[pl.ds(r, CONV_CH), :])
            dzp[pl.ds(r, CONV_CH), :] = dz_ref[pl.ds(r, CONV_CH), :]
            return c
        lax.fori_loop(0, n, fill, 0)

        def step(i, c):
            r = pl.multiple_of(i * CONV_CH, CONV_CH)
            dzc = dz_ref[pl.ds(r, CONV_CH), :]
            dh = jnp.zeros((CONV_CH, LANE), F32)
            for k in range(CONV_K):
                dh = dh + w_ref[k:k + 1, :] * dzp[pl.ds(r + (CONV_K - 1 - k), CONV_CH), :]
                accw[8 * k:8 * k + 8, :] += _colsum8(dzc * hp[pl.ds(r + (k + 2), CONV_CH), :])
            a = a_ref[pl.ds(r, CONV_CH), :]
            sg = _sigmoid(g_ref[pl.ds(r, CONV_CH), :])
            da_ref[pl.ds(r, CONV_CH), :] = dh * sg
            dg_ref[pl.ds(r, CONV_CH), :] = dh * a * sg * (1.0 - sg)
            return c
        lax.fori_loop(0, n, step, 0)

        for k in range(CONV_K):
            dcw_ref[k:k + 1, :] = jnp.sum(accw[8 * k:8 * k + 8, :], axis=0, keepdims=True)
        dcw_ref[CONV_K:32, :] = jnp.zeros((32 - CONV_K, LANE), F32)

    nb = CONV_DIM // LANE
    colspec = pl.BlockSpec((t, LANE), lambda j: (0, j))
    return pl.pallas_call(
        body, name="conv_bwd", grid=(nb,),
        in_specs=[colspec, pl.BlockSpec((t, LANE), lambda j: (0, nb + j)),
                  pl.BlockSpec((32, LANE), lambda j: (0, j)), colspec],
        out_specs=[colspec, colspec, pl.BlockSpec((32, LANE), lambda j: (0, j))],
        out_shape=[jax.ShapeDtypeStruct((t, CONV_DIM), F32), jax.ShapeDtypeStruct((t, CONV_DIM), F32),
                   jax.ShapeDtypeStruct((32, CONV_DIM), F32)],
        scratch_shapes=[pltpu.VMEM((t + 32, LANE), F32), pltpu.VMEM((t + 32, LANE), F32),
                        pltpu.VMEM((8 * 32, LANE), F32)],
        compiler_params=_cp("parallel"),
    )(ag, ag, cw, dz)


def _rope(x, c, s1, s2):
    return x * c + pltpu.roll(x, LANE - 16, 1) * s1 + pltpu.roll(x, 16, 1) * s2


def _rope_t(dy, c, s1, s2):
    return dy * c + pltpu.roll(dy * s1, 16, 1) + pltpu.roll(dy * s2, LANE - 16, 1)


def _mla_pre_fwd(cq, ckv, kr, qag, wuq, kvag, wk, wv, qng, kng, rc, rs1, rs2):
    t = cq.shape[0]
    tm = _tile(t, 384)

    def body(cq_ref, ckv_ref, kr_ref, qag_ref, wuq_ref, kvag_ref, wk_ref, wv_ref, qng_ref, kng_ref,
             c_ref, s1_ref, s2_ref, q_ref, k_ref, v_ref, cqn_ref, ckvn_ref):
        cqv = cq_ref[...]
        cqn = (cqv * _rstd(cqv) * qag_ref[...]).astype(_MM)
        cqn_ref[...] = cqn
        ckvv = ckv_ref[...]
        ckvn = (ckvv * _rstd(ckvv) * kvag_ref[...]).astype(_MM)
        ckvn_ref[...] = ckvn
        qraw = jnp.dot(cqn, wuq_ref[...], preferred_element_type=F32)
        kraw = jnp.dot(ckvn, wk_ref[...], preferred_element_type=F32)
        v_ref[...] = jnp.dot(ckvn, wv_ref[...], preferred_element_type=F32).astype(_MM)
        krv = kr_ref[...]
        c, s1, s2 = c_ref[...], s1_ref[...], s2_ref[...]
        for h in range(NH):
            sl = slice(LANE * h, LANE * (h + 1))
            qh = qraw[:, sl]
            qn = qh * _rstd(qh, QK_DIM) * qng_ref[...]
            q_ref[:, sl] = (_rope(qn, c, s1, s2) * ATT_SCALE).astype(_MM)
            kh = kraw[:, sl] + krv
            kn = kh * _rstd(kh, QK_DIM) * kng_ref[...]
            k_ref[:, sl] = _rope(kn, c, s1, s2).astype(_MM)

    hd = NH * LANE
    return pl.pallas_call(
        body, name="mla_pre_fwd", grid=(t // tm,),
        in_specs=[_row(tm, 256), _row(tm, 128), _row(tm, 128), _full((1, 256)), _full((256, hd)),
                  _full((1, 128)), _full((128, hd)), _full((128, hd)), _full((1, LANE)), _full((1, LANE)),
                  _row(tm, LANE), _row(tm, LANE), _row(tm, LANE)],
        out_specs=[_row(tm, hd), _row(tm, hd), _row(tm, hd), _row(tm, 256), _row(tm, 128)],
        out_shape=[jax.ShapeDtypeStruct((t, hd), _MM)] * 3 + [jax.ShapeDtypeStruct((t, 256), _MM),
                                                              jax.ShapeDtypeStruct((t, 128), _MM)],
        compiler_params=_cp("parallel"),
    )(cq, ckv, kr, qag, wuq, kvag, wk, wv, qng, kng, rc, rs1, rs2)


def _mla_pre_bwd(dq, dk, dv, cq, ckv, kr, qag, wuq, kvag, wk, wv, qng, kng, rc, rs1, rs2):
    t = cq.shape[0]
    tm = _tile(t, 384)
    hd = NH * LANE

    def body(dq_ref, dk_ref, dv_ref, cq_ref, ckv_ref, kr_ref, qag_ref, wuq_ref, kvag_ref, wk_ref,
             wv_ref, qng_ref, kng_ref, c_ref, s1_ref, s2_ref,
             dcq_ref, dckv_ref, dkr_ref, dqraw_ref, dkraw_ref, dqag_ref, dkvag_ref, dqng_ref, dkng_ref):
        i = pl.program_id(0)
        cqv = cq_ref[...]
        rq_in = _rstd(cqv)
        cqn = (cqv * rq_in * qag_ref[...]).astype(_MM)
        ckvv = ckv_ref[...]
        rkv_in = _rstd(ckvv)
        ckvn = (ckvv * rkv_in * kvag_ref[...]).astype(_MM)
        qraw = jnp.dot(cqn, wuq_ref[...], preferred_element_type=F32)
        kraw = jnp.dot(ckvn, wk_ref[...], preferred_element_type=F32)
        krv = kr_ref[...]
        c, s1, s2 = c_ref[...], s1_ref[...], s2_ref[...]
        dkr = jnp.zeros((tm, LANE), F32)
        dqng = jnp.zeros((8, LANE), F32)
        dkng = jnp.zeros((8, LANE), F32)
        for h in range(NH):
            sl = slice(LANE * h, LANE * (h + 1))
            qh = qraw[:, sl]
            dqn = _rope_t(dq_ref[:, sl] * ATT_SCALE, c, s1, s2)
            dqh, gq = _rms_bwd(dqn, qh, _rstd(qh, QK_DIM), qng_ref[...], QK_DIM)
            dqraw_ref[:, sl] = dqh.astype(_MM)
            dqng = dqng + _colsum8(gq)
            kh = kraw[:, sl] + krv
            dkn = _rope_t(dk_ref[:, sl], c, s1, s2)
            dkh, gk = _rms_bwd(dkn, kh, _rstd(kh, QK_DIM), kng_ref[...], QK_DIM)
            dkraw_ref[:, sl] = dkh.astype(_MM)
            dkr = dkr + dkh
            dkng = dkng + _colsum8(gk)
        dkr_ref[...] = dkr.astype(_MM)
        dcqn = _mm_nt(dqraw_ref[...], wuq_ref[...])
        dcq, gqa = _rms_bwd(dcqn, cqv, rq_in, qag_ref[...])
        dcq_ref[...] = dcq.astype(_MM)
        dckvn = _mm_nt(dkraw_ref[...], wk_ref[...]) + _mm_nt(dv_ref[...], wv_ref[...])
        dckv, gkva = _rms_bwd(dckvn, ckvv, rkv_in, kvag_ref[...])
        dckv_ref[...] = dckv.astype(_MM)

        @pl.when(i == 0)
        def _():
            dqag_ref[...] = jnp.zeros_like(dqag_ref)
            dkvag_ref[...] = jnp.zeros_like(dkvag_ref)
            dqng_ref[...] = jnp.zeros_like(dqng_ref)
            dkng_ref[...] = jnp.zeros_like(dkng_ref)
        dqag_ref[...] += _colsum8(gqa)
        dkvag_ref[...] += _colsum8(gkva)
        dqng_ref[...] += dqng
        dkng_ref[...] += dkng

    return pl.pallas_call(
        body, name="mla_pre_bwd", grid=(t // tm,),
        in_specs=[_row(tm, hd), _row(tm, hd), _row(tm, hd), _row(tm, 256), _row(tm, 128), _row(tm, 128),
                  _full((1, 256)), _full((256, hd)), _full((1, 128)), _full((128, hd)),
                  _full((128, hd)), _full((1, LANE)), _full((1, LANE)),
                  _row(tm, LANE), _row(tm, LANE), _row(tm, LANE)],
        out_specs=[_row(tm, 256), _row(tm, 128), _row(tm, 128), _row(tm, hd), _row(tm, hd),
                   _full((8, 256)), _full((8, 128)), _full((8, LANE)), _full((8, LANE))],
        out_shape=[jax.ShapeDtypeStruct((t, 256), _MM), jax.ShapeDtypeStruct((t, 128), _MM),
                   jax.ShapeDtypeStruct((t, 128), _MM), jax.ShapeDtypeStruct((t, hd), _MM),
                   jax.ShapeDtypeStruct((t, hd), _MM), jax.ShapeDtypeStruct((8, 256), F32),
                   jax.ShapeDtypeStruct((8, 128), F32), jax.ShapeDtypeStruct((8, LANE), F32),
                   jax.ShapeDtypeStruct((8, LANE), F32)],
        compiler_params=_cp("arbitrary"),
    )(dq, dk, dv, cq, ckv, kr, qag, wuq, kvag, wk, wv, qng, kng, rc, rs1, rs2)


ATT_TILE = 704


def _attn_mask(r0, c0, tq):
    rows = r0 + lax.broadcasted_iota(jnp.int32, (tq, 1), 0)
    cols = c0 + lax.broadcasted_iota(jnp.int32, (1, tq), 1)
    return (cols <= rows) & (cols >= FRONT)


def _attn_fwd(q, k, v, plan=None):
    t = q.shape[0]
    tq = _tile(t, ATT_TILE)
    nq = t // tq
    p_args, p_in, p_out, p_shape, p_sem = _plan_specs(plan)

    def body(*refs):
        ((q_ref, k_ref, v_ref), (o_ref, lse_ref), _), rider = _host_refs(refs, 3, 2, 0, plan)
        done = _ride(plan, rider, pl.program_id(0), NH - 1)

        def qloop(qi, carry):
            r0 = pl.multiple_of(qi * tq, tq)
            qb = q_ref[pl.ds(r0, tq), :]

            def kstep(kj, st, masked):
                m, l, acc = st
                c0 = pl.multiple_of(kj * tq, tq)
                s = _mm_nt(qb, k_ref[pl.ds(c0, tq), :])
                if masked:
                    s = jnp.where(_attn_mask(r0, c0, tq), s, NEG)
                m2 = jnp.maximum(m, jnp.max(s, axis=-1, keepdims=True))
                p = jnp.exp(s - m2)
                a = jnp.exp(m - m2)
                l = a * l + jnp.sum(p, axis=-1, keepdims=True)
                acc = a * acc + _mm(p, v_ref[pl.ds(c0, tq), :])
                return m2, l, acc

            st = kstep(0, (jnp.full((tq, 1), NEG, F32), jnp.zeros((tq, 1), F32), jnp.zeros((tq, LANE), F32)), True)
            st = lax.fori_loop(1, qi, lambda kj, s_: kstep(kj, s_, False), st)
            m, l, acc = lax.cond(qi > 0, lambda s_: kstep(qi, s_, True), lambda s_: s_, st)
            o_ref[pl.ds(r0, tq), :] = acc / l
            lse_ref[pl.ds(r0, tq), :] = m + jnp.log(l)
            return carry
        lax.fori_loop(0, nq, qloop, 0)
        done()

    hs = pl.BlockSpec((t, LANE), lambda h: (0, h))
    res = pl.pallas_call(
        body, name="attn_fwd", grid=(NH,),
        in_specs=[hs, hs, hs] + p_in,
        out_specs=[hs, pl.BlockSpec((None, t, 1), lambda h: (h, 0, 0))] + p_out,
        out_shape=[jax.ShapeDtypeStruct((t, NH * LANE), F32), jax.ShapeDtypeStruct((NH, t, 1), F32)] + p_shape,
        scratch_shapes=p_sem,
        compiler_params=_cp("parallel" if plan is None else "arbitrary"),
    )(q, k, v, *p_args)
    return res[:2], res[2:]


def _attn_bwd(q, k, v, o, lse, do, plan=None):
    t = q.shape[0]
    tq = _tile(t, ATT_TILE)
    nq = t // tq
    p_args, p_in, p_out, p_shape, p_sem = _plan_specs(plan)

    def body(*refs):
        (ins, (dq_ref, dk_ref, dv_ref), (delta,)), rider = _host_refs(refs, 6, 3, 1, plan)
        q_ref, k_ref, v_ref, o_ref, lse_ref, do_ref = ins
        done = _ride(plan, rider, pl.program_id(0), NH - 1)

        def prep(i, c):
            r0 = pl.multiple_of(i * tq, tq)
            delta[pl.ds(r0, tq), :] = jnp.sum(do_ref[pl.ds(r0, tq), :] * o_ref[pl.ds(r0, tq), :], axis=-1,
                                              keepdims=True)
            dq_ref[pl.ds(r0, tq), :] = jnp.zeros((tq, LANE), F32)
            return c
        lax.fori_loop(0, nq, prep, 0)

        def kloop(kj, carry):
            c0 = pl.multiple_of(kj * tq, tq)
            kb = k_ref[pl.ds(c0, tq), :]
            vb = v_ref[pl.ds(c0, tq), :]

            def qstep(qi, st, masked):
                dkb, dvb = st
                r0 = pl.multiple_of(qi * tq, tq)
                qb = q_ref[pl.ds(r0, tq), :]
                dob = do_ref[pl.ds(r0, tq), :].astype(_MM)
                s = _mm_nt(qb, kb)
                if masked:
                    s = jnp.where(_attn_mask(r0, c0, tq), s, NEG)
                p = jnp.exp(s - lse_ref[pl.ds(r0, tq), :])
                dvb = dvb + _mm_tn(p, dob)
                dp = _mm_nt(dob, vb)
                ds = (p * (dp - delta[pl.ds(r0, tq), :])).astype(_MM)
                dkb = dkb + _mm_tn(ds, qb)
                dq_ref[pl.ds(r0, tq), :] += _mm(ds, kb)
                return dkb, dvb

            st = qstep(kj, (jnp.zeros((tq, LANE), F32), jnp.zeros((tq, LANE), F32)), True)
            dkb, dvb = lax.cond(
                kj == 0,
                lambda s_: lax.fori_loop(kj + 1, nq, lambda qi, t_: qstep(qi, t_, True), s_),
                lambda s_: lax.fori_loop(kj + 1, nq, lambda qi, t_: qstep(qi, t_, False), s_), st)
            dk_ref[pl.ds(c0, tq), :] = dkb
            dv_ref[pl.ds(c0, tq), :] = dvb
            return carry
        lax.fori_loop(0, nq, kloop, 0)
        done()

    hs = pl.BlockSpec((t, LANE), lambda h: (0, h))
    res = pl.pallas_call(
        body, name="attn_bwd", grid=(NH,),
        in_specs=[hs, hs, hs, hs, pl.BlockSpec((None, t, 1), lambda h: (h, 0, 0)), hs] + p_in,
        out_specs=[hs, hs, hs] + p_out,
        out_shape=[jax.ShapeDtypeStruct((t, NH * LANE), F32)] * 3 + p_shape,
        scratch_shapes=[pltpu.VMEM((t, 1), F32)] + p_sem,
        compiler_params=_cp("parallel" if plan is None else "arbitrary"),
    )(q, k, v, o, lse, do, *p_args)
    return res[:3], res[3:]


def _cumsum_rows(x):
    n = x.shape[0]
    rows = lax.broadcasted_iota(jnp.int32, (n, 1), 0)
    d = 1
    while d < n:
        x = x + jnp.where(rows >= d, pltpu.roll(x, d, 0), 0.0)
        d *= 2
    return x


def _revcumsum_rows(x):
    n = x.shape[0]
    rows = lax.broadcasted_iota(jnp.int32, (n, 1), 0)
    d = 1
    while d < n:
        x = x + jnp.where(rows < n - d, pltpu.roll(x, n - d, 0), 0.0)
        d *= 2
    return x


def _hgrn_gates(f, lb):
    sneg = _sigmoid(-f)
    kk = (1.0 - lb) * sneg
    lf = jnp.log1p(-jnp.minimum(kk, GATE_CLAMP))
    return kk, lf, sneg


def _silu(x):
    return x * _sigmoid(x)


def _dsilu(x):
    s = _sigmoid(x)
    return s * (1.0 + x * (1.0 - s))


def _hgrn_intra(q, kk, b):
    parts = []
    for blk in range(CHUNK // SUB):
        lo = blk * SUB
        ref = jnp.zeros((1, LANE), F32) if blk == 0 else b[lo - 1:lo, :]
        eq = jnp.exp(b[lo:lo + SUB, :] - ref)
        ek = jnp.exp(jnp.minimum(ref - b, EXP_CLIP))
        parts.append((q[lo:lo + SUB, :] * eq, kk * ek, eq, ek))
    return parts


def _chunk_causal():
    return lax.broadcasted_iota(jnp.int32, (CHUNK, CHUNK), 1) <= lax.broadcasted_iota(jnp.int32, (CHUNK, CHUNK), 0)


def _hgrn_fwd(h4, lb):
    t = h4.shape[0]
    nc = t // CHUNK

    def body(q_ref, f_ref, i_ref, lb_ref, o_ref, s_ref, st):
        st[...] = jnp.zeros_like(st)
        causal = _chunk_causal()

        def chunk(c, carry):
            r0 = pl.multiple_of(c * CHUNK, CHUNK)
            q = q_ref[pl.ds(r0, CHUNK), :]
            kk, lf, _ = _hgrn_gates(f_ref[pl.ds(r0, CHUNK), :], lb_ref[...])
            v = _silu(i_ref[pl.ds(r0, CHUNK), :])
            b = _cumsum_rows(lf)
            s_prev = st[...]
            s_ref[c] = s_prev
            o = _hmm_nt(q * jnp.exp(b), s_prev)
            a = jnp.concatenate([_hmm_nt(qs, ks) for qs, ks, _, _ in _hgrn_intra(q, kk, b)], axis=0)
            a = jnp.where(causal, a, 0.0)
            o_ref[pl.ds(r0, CHUNK), :] = o + _hmm(a, v)
            bl = b[CHUNK - 1:CHUNK, :]
            st[...] = s_prev * jnp.exp(bl) + _hmm_tn(v, kk * jnp.exp(bl - b))
            return carry
        lax.fori_loop(0, nc, chunk, 0, unroll=2)

    def col(j):
        return pl.BlockSpec((t, LANE), lambda h: (0, HH * j + h))
    return pl.pallas_call(
        body, name="hgrn_fwd", grid=(HH,),
        in_specs=[col(0), col(1), col(2), pl.BlockSpec((1, LANE), lambda h: (0, h))],
        out_specs=[pl.BlockSpec((t, LANE), lambda h: (0, h)),
                   pl.BlockSpec((None, nc, LANE, LANE), lambda h: (h, 0, 0, 0))],
        out_shape=[jax.ShapeDtypeStruct((t, HH * LANE), F32), jax.ShapeDtypeStruct((HH, nc, LANE, LANE), F32)],
        scratch_shapes=[pltpu.VMEM((LANE, LANE), F32)],
        compiler_params=_cp("parallel"),
    )(h4, h4, h4, lb)


def _hgrn_bwd(h4, lb, do, states, plan=None):
    t = h4.shape[0]
    nc = t // CHUNK
    p_args, p_in, p_out, p_shape, p_sem = _plan_specs(plan)

    def body(*refs):
        (ins, outs, (dst, carry)), rider = _host_refs(refs, 6, 4, 2, plan)
        q_ref, f_ref, i_ref, lb_ref, do_ref, s_ref = ins
        dq_ref, df_ref, di_ref, dlb_ref = outs
        done = _ride(plan, rider, pl.program_id(0), HH - 1)
        dst[...] = jnp.zeros_like(dst)
        carry[...] = jnp.zeros_like(carry)
        dlb_ref[...] = jnp.zeros_like(dlb_ref)
        causal = _chunk_causal()

        def chunk(cc, cr):
            c = nc - 1 - cc
            r0 = pl.multiple_of(c * CHUNK, CHUNK)
            q = q_ref[pl.ds(r0, CHUNK), :]
            lbv = lb_ref[...]
            kk, lf, sneg = _hgrn_gates(f_ref[pl.ds(r0, CHUNK), :], lbv)
            iv = i_ref[pl.ds(r0, CHUNK), :]
            v = _silu(iv)
            b = _cumsum_rows(lf)
            s_prev = s_ref[c]
            ds_new = dst[...]
            dob = do_ref[pl.ds(r0, CHUNK), :]
            e = jnp.exp(b)
            qe = q * e
            bl = b[CHUNK - 1:CHUNK, :]
            etail = jnp.exp(bl - b)
            kd = kk * etail
            dq_inter = _hmm(dob, s_prev) * e
            dv = _hmm_nt(kd, ds_new)
            dkk = _hmm(v, ds_new) * etail
            parts = _hgrn_intra(q, kk, b)
            a = jnp.where(causal, jnp.concatenate([_hmm_nt(qs, ks) for qs, ks, _, _ in parts], axis=0), 0.0)
            da = jnp.where(causal, _hmm_nt(dob, v), 0.0)
            dv = dv + _hmm_tn(a, dob)
            dq_rows = []
            for blk, (qs, ks, eq, ek) in enumerate(parts):
                da_blk = da[blk * SUB:(blk + 1) * SUB, :]
                dq_rows.append(_hmm(da_blk, ks) * eq)
                dkk = dkk + _hmm_tn(da_blk, qs) * ek
            dq = dq_inter + jnp.concatenate(dq_rows, axis=0)
            dst[...] = ds_new * jnp.exp(bl) + _hmm_tn(dob, qe)
            g = q * dq - kk * dkk
            dlf = _revcumsum_rows(g) + carry[0:1, :]
            carry[0:1, :] += jnp.sum(g, axis=0, keepdims=True)
            dkk_tot = dkk + dlf * jnp.where(kk < GATE_CLAMP, -1.0 / (1.0 - kk), 0.0)
            dq_ref[pl.ds(r0, CHUNK), :] = dq
            df_ref[pl.ds(r0, CHUNK), :] = dkk_tot * (1.0 - lbv) * (-sneg * (1.0 - sneg))
            di_ref[pl.ds(r0, CHUNK), :] = dv * _dsilu(iv)
            dlb_ref[...] += _colsum8(dkk_tot * (-sneg))
            return cr
        lax.fori_loop(0, nc, chunk, 0, unroll=2)
        done()

    def col(j):
        return pl.BlockSpec((t, LANE), lambda h: (0, HH * j + h))
    hs = pl.BlockSpec((t, LANE), lambda h: (0, h))
    res = pl.pallas_call(
        body, name="hgrn_bwd", grid=(HH,),
        in_specs=[col(0), col(1), col(2), pl.BlockSpec((1, LANE), lambda h: (0, h)), hs,
                  pl.BlockSpec((None, nc, LANE, LANE), lambda h: (h, 0, 0, 0))] + p_in,
        out_specs=[hs, hs, hs, pl.BlockSpec((8, LANE), lambda h: (0, h))] + p_out,
        out_shape=[jax.ShapeDtypeStruct((t, HH * LANE), F32)] * 3 + [jax.ShapeDtypeStruct((8, HH * LANE), F32)]
        + p_shape,
        scratch_shapes=[pltpu.VMEM((LANE, LANE), F32), pltpu.VMEM((8, LANE), F32)] + p_sem,
        compiler_params=_cp("parallel" if plan is None else "arbitrary"),
    )(h4, h4, h4, lb, do, states, *p_args)
    return res[:4], res[4:]


def _ln_fwd(z, g, b):
    mu = jnp.mean(z, axis=-1, keepdims=True)
    zc = z - mu
    rstd = lax.rsqrt(jnp.mean(zc * zc, axis=-1, keepdims=True) + EPS)
    zh = zc * rstd
    return zh * g + b, zh, rstd


def _mix_fwd(x, z, o_att, o_h, h4, gates, lng, lnb, wco, wao, ng, who, wout, t_end):
    t = x.shape[0]
    tm = _tile(t, 192)

    def body(x_ref, z_ref, oa_ref, oh_ref, hg_ref, gt_ref, lng_ref, lnb_ref, wco_ref, wao_ref, ng_ref, who_ref,
             wout_ref, x1_ref, mix_ref, ca_ref, oc_ref, ya_ref, yb_ref, yc_ref):
        i = pl.program_id(0)
        ln, _, _ = _ln_fwd(z_ref[...], lng_ref[...], lnb_ref[...])
        ca = _silu(ln).astype(_MM)
        ca_ref[...] = ca
        ya = jnp.dot(ca, wco_ref[...], preferred_element_type=F32)
        yb = _mm(oa_ref[...], wao_ref[...])
        hg = hg_ref[...]
        for h in range(HH):
            sl = slice(LANE * h, LANE * (h + 1))
            oh = oh_ref[:, sl]
            oc_ref[:, sl] = (oh * _rstd(oh) * ng_ref[:, sl] * _silu(hg[:, sl])).astype(_MM)
        yc = jnp.dot(oc_ref[...], who_ref[...], preferred_element_type=F32)
        ya_ref[...] = ya
        yb_ref[...] = yb
        yc_ref[...] = yc
        mix = (_sigmoid(gt_ref[:, 0:D]) * ya + _sigmoid(gt_ref[:, D:2 * D]) * yb
               + _sigmoid(gt_ref[:, 2 * D:3 * D]) * yc).astype(_MM)
        mix_ref[...] = mix
        x1_ref[...] = x_ref[...] + _valid_rows(i, tm, t_end) * jnp.dot(mix, wout_ref[...],
                                                                       preferred_element_type=F32)

    hd = NH * LANE
    return pl.pallas_call(
        body, name="mix_fwd", grid=(t // tm,),
        in_specs=[_row(tm, D), _row(tm, CONV_DIM), _row(tm, hd), _row(tm, 512), _row(tm, 512, 3), _row(tm, 3 * D),
                  _full((1, 512)), _full((1, 512)), _full((512, D)), _full((hd, D)), _full((1, 512)),
                  _full((512, D)), _full((D, D))],
        out_specs=[_row(tm, D), _row(tm, D), _row(tm, 512), _row(tm, 512), _row(tm, D), _row(tm, D), _row(tm, D)],
        out_shape=[jax.ShapeDtypeStruct((t, D), F32), jax.ShapeDtypeStruct((t, D), _MM),
                   jax.ShapeDtypeStruct((t, 512), _MM), jax.ShapeDtypeStruct((t, 512), _MM),
                   jax.ShapeDtypeStruct((t, D), F32), jax.ShapeDtypeStruct((t, D), F32),
                   jax.ShapeDtypeStruct((t, D), F32)],
        compiler_params=_cp("parallel"),
    )(x, z, o_att, o_h, h4, gates, lng, lnb, wco, wao, ng, who, wout)


def _mix_bwd(dx1, ya, yb, yc, gates, z, o_h, h4, lng, lnb, ng, wout, wco, wao, who, plan=None):
    t = dx1.shape[0]
    tm = _tile(t, 384)
    hd = NH * LANE
    p_args, p_in, p_out, p_shape, p_sem = _plan_specs(plan)

    def body(*refs):
        (ins, outs, _), rider = _host_refs(refs, 15, 12, 0, plan)
        (dx1_ref, ya_ref, yb_ref, yc_ref, gt_ref, z_ref, oh_ref, hg_ref, lng_ref, lnb_ref, ng_ref,
         wout_ref, wco_ref, wao_ref, who_ref) = ins
        (dgt_ref, dya_ref, dyb_ref, dyc_ref, dz_ref, doa_ref, doh_ref, dhg_ref,
         dlng_ref, dlnb_ref, dcb_ref, dng_ref) = outs
        i = pl.program_id(0)
        done = _ride(plan, rider, i, t // tm - 1)
        dmix = _mm_nt(dx1_ref[...], wout_ref[...])
        dys = []
        for j, y_ref in enumerate((ya_ref, yb_ref, yc_ref)):
            sg = _sigmoid(gt_ref[:, j * D:(j + 1) * D])
            dgt_ref[:, j * D:(j + 1) * D] = (dmix * y_ref[...] * sg * (1.0 - sg)).astype(_MM)
            dys.append((dmix * sg).astype(_MM))
        dya_ref[...], dyb_ref[...], dyc_ref[...] = dys
        dca = _mm_nt(dys[0], wco_ref[...])
        ln, zh, rstd = _ln_fwd(z_ref[...], lng_ref[...], lnb_ref[...])
        dln = dca * _dsilu(ln)
        dzh = dln * lng_ref[...]
        dz = rstd * (dzh - jnp.mean(dzh, axis=-1, keepdims=True)
                     - zh * jnp.mean(dzh * zh, axis=-1, keepdims=True))
        dz_ref[...] = dz
        doa_ref[...] = _mm_nt(dys[1], wao_ref[...])
        doc = _mm_nt(dys[2], who_ref[...])
        hg = hg_ref[...]
        dng_rows = []
        for h in range(HH):
            sl = slice(LANE * h, LANE * (h + 1))
            oh = oh_ref[:, sl]
            r = _rstd(oh)
            don = doc[:, sl] * _silu(hg[:, sl])
            dhg_ref[:, sl] = (doc[:, sl] * oh * r * ng_ref[:, sl] * _dsilu(hg[:, sl])).astype(_MM)
            doh, gn = _rms_bwd(don, oh, r, ng_ref[:, sl])
            doh_ref[:, sl] = doh
            dng_rows.append(_colsum8(gn))

        @pl.when(i == 0)
        def _():
            dlng_ref[...] = jnp.zeros_like(dlng_ref)
            dlnb_ref[...] = jnp.zeros_like(dlnb_ref)
            dcb_ref[...] = jnp.zeros_like(dcb_ref)
            dng_ref[...] = jnp.zeros_like(dng_ref)
        dlng_ref[...] += _colsum8(dln * zh)
        dlnb_ref[...] += _colsum8(dln)
        dcb_ref[...] += _colsum8(dz)
        dng_ref[...] += jnp.concatenate(dng_rows, axis=1)
        done()

    res = pl.pallas_call(
        body, name="mix_bwd", grid=(t // tm,),
        in_specs=[_row(tm, D), _row(tm, D), _row(tm, D), _row(tm, D), _row(tm, 3 * D), _row(tm, 512), _row(tm, 512),
                  _row(tm, 512, 3), _full((1, 512)), _full((1, 512)), _full((1, 512)),
                  _full((D, D)), _full((512, D)), _full((hd, D)), _full((512, D))] + p_in,
        out_specs=[_row(tm, 3 * D), _row(tm, D), _row(tm, D), _row(tm, D), _row(tm, 512), _row(tm, hd),
                   _row(tm, 512), _row(tm, 512), _full((8, 512)), _full((8, 512)), _full((8, 512)),
                   _full((8, 512))] + p_out,
        out_shape=[jax.ShapeDtypeStruct((t, 3 * D), _MM), jax.ShapeDtypeStruct((t, D), _MM),
                   jax.ShapeDtypeStruct((t, D), _MM), jax.ShapeDtypeStruct((t, D), _MM),
                   jax.ShapeDtypeStruct((t, 512), F32), jax.ShapeDtypeStruct((t, hd), F32),
                   jax.ShapeDtypeStruct((t, 512), F32), jax.ShapeDtypeStruct((t, 512), _MM)]
        + [jax.ShapeDtypeStruct((8, 512), F32)] * 4 + p_shape,
        scratch_shapes=p_sem,
        compiler_params=_cp("arbitrary"),
    )(dx1, ya, yb, yc, gates, z, o_h, h4, lng, lnb, ng, wout, wco, wao, who, *p_args)
    return res[:12], res[12:]


D_FF = 4096


def _ffn_fwd(x1, g2, w1, w2):
    t = x1.shape[0]
    tm = _tile(t, 192)

    def body(x1_ref, g_ref, w1_ref, w2_ref, x2_ref, p_ref):
        xv = x1_ref[...]
        h2 = (xv * _rstd(xv) * g_ref[...]).astype(_MM)
        p = jnp.dot(h2, w1_ref[...], preferred_element_type=F32)
        p_ref[...] = p
        r = jnp.maximum(p, 0.0)
        x2_ref[...] = xv + jnp.dot((r * r).astype(_MM), w2_ref[...], preferred_element_type=F32)

    return pl.pallas_call(
        body, name="ffn_fwd", grid=(t // tm,),
        in_specs=[_row(tm, D), _full((1, D)), _full((D, D_FF)), _full((D_FF, D))],
        out_specs=[_row(tm, D), _row(tm, D_FF)],
        out_shape=[jax.ShapeDtypeStruct((t, D), F32), jax.ShapeDtypeStruct((t, D_FF), F32)],
        compiler_params=_cp("parallel"),
    )(x1, g2, w1, w2)


def _ffn_bwd(dx2, x1, p, g2, w1t, w2t, plan=None):
    t = x1.shape[0]
    tm = _tile(t, 192)
    p_args, p_in, p_out, p_shape, p_sem = _plan_specs(plan)

    def body(*refs):
        (ins, outs, _), rider = _host_refs(refs, 6, 5, 0, plan)
        dx2_ref, x1_ref, p_ref, g_ref, w1t_ref, w2t_ref = ins
        dx1_ref, h2_ref, act_ref, dp_ref, dg_ref = outs
        i = pl.program_id(0)
        done = _ride(plan, rider, i, t // tm - 1)
        xv = x1_ref[...]
        rstd = _rstd(xv)
        h2_ref[...] = (xv * rstd * g_ref[...]).astype(_MM)
        r = jnp.maximum(p_ref[...], 0.0)
        act_ref[...] = (r * r).astype(_MM)
        dx2 = dx2_ref[...]
        da = _mm(dx2, w2t_ref[...])
        dp = (2.0 * r * da).astype(_MM)
        dp_ref[...] = dp
        dh2 = jnp.dot(dp, w1t_ref[...], preferred_element_type=F32)
        dxn, dgrow = _rms_bwd(dh2, xv, rstd, g_ref[...])
        dx1_ref[...] = dx2 + dxn

        @pl.when(i == 0)
        def _():
            dg_ref[...] = jnp.zeros_like(dg_ref)
        dg_ref[...] += _colsum8(dgrow)
        done()

    res = pl.pallas_call(
        body, name="ffn_bwd", grid=(t // tm,),
        in_specs=[_row(tm, D), _row(tm, D), _row(tm, D_FF), _full((1, D)), _full((D_FF, D)),
                  _full((D, D_FF))] + p_in,
        out_specs=[_row(tm, D), _row(tm, D), _row(tm, D_FF), _row(tm, D_FF), _full((8, D))] + p_out,
        out_shape=[jax.ShapeDtypeStruct((t, D), F32), jax.ShapeDtypeStruct((t, D), _MM),
                   jax.ShapeDtypeStruct((t, D_FF), _MM), jax.ShapeDtypeStruct((t, D_FF), _MM),
                   jax.ShapeDtypeStruct((8, D), F32)] + p_shape,
        scratch_shapes=p_sem,
        compiler_params=_cp("arbitrary"),
    )(dx2, x1, p, g2, w1t, w2t, *p_args)
    return res[:5], res[5:]


WGRAD_VMEM = 40 * 1024 * 1024


def _wgrad(a, b, name, chips=1):
    t, ka = a.shape
    nb = b.shape[1]
    cs = nb // chips
    widths = [d for d in range(cs, 0, -LANE) if cs % d == 0 and d % LANE == 0] or [cs]
    tn, tm = widths[-1], 64
    for d in widths:
        room = WGRAD_VMEM - 2 * ka * d * 4
        row_bytes = 2 * (ka * a.dtype.itemsize + d * b.dtype.itemsize) + 4 * ka
        fit = [r for r in range(64, t + 1, 64) if t % r == 0 and r * row_bytes <= room]
        if ka * d * 4 <= 16 * 1024 * 1024 and fit and (max(fit) >= 384 or d == widths[-1]):
            tn, tm = d, max(fit)
            break
    per = cs // tn

    def body(a_ref, b_ref, o_ref):
        @pl.when(pl.program_id(1) == 0)
        def _():
            o_ref[...] = jnp.zeros_like(o_ref)
        o_ref[...] += _mm_tn(a_ref[...], b_ref[...])

    if chips == 1:
        out_spec = pl.BlockSpec((ka, tn), lambda n, i: (0, n))
        out_shape = jax.ShapeDtypeStruct((ka, nb), F32)
    else:
        out_spec = pl.BlockSpec((None, ka, tn), lambda n, i: (n // per, 0, n % per))
        out_shape = jax.ShapeDtypeStruct((chips, ka, cs), F32)
    return pl.pallas_call(
        body, name="wgrad_" + name, grid=(nb // tn, t // tm),
        in_specs=[pl.BlockSpec((tm, ka), lambda n, i: (i, 0)), pl.BlockSpec((tm, tn), lambda n, i: (i, n))],
        out_specs=out_spec, out_shape=out_shape,
        compiler_params=_cp("parallel", "arbitrary"),
    )(a, b)


def _loss_head(y, target, t_end):
    t = y.shape[0]
    tm = _tile(t, 384)

    def body(y_ref, tg_ref, dy_ref, l_ref):
        i = pl.program_id(0)
        r = i * tm + lax.broadcasted_iota(jnp.int32, (tm, 1), 0)
        real = ((r >= ROW0) & (r < t_end)).astype(F32)
        diff = (y_ref[...] - tg_ref[...]) * real
        dy_ref[...] = diff * (1.0 / D)

        @pl.when(i == 0)
        def _():
            l_ref[...] = jnp.zeros_like(l_ref)
        sq = _colsum8(diff * diff)
        part = sq[:, 0:LANE]
        for j in range(1, D // LANE):
            part = part + sq[:, j * LANE:(j + 1) * LANE]
        l_ref[...] += part * (0.5 / D)

    return pl.pallas_call(
        body, name="loss_head", grid=(t // tm,),
        in_specs=[_row(tm, D), _row(tm, D)],
        out_specs=[_row(tm, D), _full((8, LANE))],
        out_shape=[jax.ShapeDtypeStruct((t, D), F32), jax.ShapeDtypeStruct((8, LANE), F32)],
        compiler_params=_cp("arbitrary"),
    )(y, target)


def _lower_bounds_fwd(logits):
    depth, n = logits.shape

    def body(l_ref, lb_ref):
        lg = l_ref[...]
        m = jnp.max(lg, axis=0, keepdims=True)
        e = jnp.exp(lg - m)
        p = e / jnp.sum(e, axis=0, keepdims=True)
        acc = jnp.zeros((1, n), F32)
        for l in range(depth):
            if l > 0:
                acc = acc + p[l:l + 1, :]
            lb_ref[l:l + 1, :] = acc

    return pl.pallas_call(body, name="lower_bounds_fwd", out_shape=jax.ShapeDtypeStruct((depth, n), F32))(logits)


def _lower_bounds_bwd(logits, dlb):
    depth, n = logits.shape

    def body(l_ref, dlb_ref, dl_ref):
        lg = l_ref[...]
        m = jnp.max(lg, axis=0, keepdims=True)
        e = jnp.exp(lg - m)
        p = e / jnp.sum(e, axis=0, keepdims=True)
        dps = [jnp.zeros((1, n), F32)]
        for j in range(1, depth):
            acc = jnp.zeros((1, n), F32)
            for l in range(j, depth):
                acc = acc + dlb_ref[l:l + 1, :]
            dps.append(acc)
        dot = jnp.zeros((1, n), F32)
        for j in range(depth):
            dot = dot + p[j:j + 1, :] * dps[j]
        for j in range(depth):
            dl_ref[j:j + 1, :] = p[j:j + 1, :] * (dps[j] - dot)

    return pl.pallas_call(body, name="lower_bounds_bwd", out_shape=jax.ShapeDtypeStruct((depth, n), F32))(logits, dlb)


def _ew_tile(rows, cols, n_arrays):
    cap = max(16, (32 * 1024 * 1024) // (8 * n_arrays * cols))
    for mult in (16, 8):
        fit = [t for t in range(mult, rows + 1, mult) if rows % t == 0 and t <= cap]
        if fit:
            return max(fit)
    return rows


def _adamw_math(w, g, m, v):
    mn = ADAM_B1 * m + (1.0 - ADAM_B1) * g
    vn = ADAM_B2 * v + (1.0 - ADAM_B2) * (g * g)
    m_hat = mn / (1.0 - ADAM_B1 ** ADAM_STEP)
    v_hat = vn / (1.0 - ADAM_B2 ** ADAM_STEP)
    return -ADAM_LR * (m_hat / (jnp.sqrt(v_hat) + ADAM_EPS) + ADAM_WD * w), mn, vn


def _adamw_layers(w, m, v, g0, g1, g_sibling, name):
    _, r, c_ = w.shape
    tr = _ew_tile(r, c_, 10)

    def body(w_ref, m_ref, v_ref, g0_ref, g1_ref, gs_ref, g_ref, d_ref, mo_ref, vo_ref):
        layer = pl.program_id(0)
        own = jnp.where(layer == 0, g0_ref[...], g1_ref[...])
        g = jnp.where(layer == lax.axis_index("c"), own, gs_ref[...])
        g_ref[...] = g
        d_ref[...], mo_ref[...], vo_ref[...] = _adamw_math(w_ref[...], g, m_ref[...], v_ref[...])

    lay = pl.BlockSpec((None, tr, c_), lambda l, i: (l, i, 0))
    flat = pl.BlockSpec((tr, c_), lambda l, i: (i, 0))
    only0 = pl.BlockSpec((tr, c_), lambda l, i: (i * (1 - l), 0))
    only1 = pl.BlockSpec((tr, c_), lambda l, i: (i * l, 0))
    return pl.pallas_call(
        body, name="adamw_" + name, grid=(2, r // tr),
        in_specs=[lay, lay, lay, only0, only1, flat], out_specs=[lay] * 4,
        out_shape=[jax.ShapeDtypeStruct(w.shape, F32)] * 4,
        compiler_params=_cp("parallel", "parallel"),
    )(w, m, v, g0, g1, g_sibling)


def _adamw(w, g, m, v, name):
    rows, cols = w.shape
    tr = _ew_tile(rows, cols, 7)

    def body(w_ref, g_ref, m_ref, v_ref, d_ref, mo_ref, vo_ref):
        d_ref[...], mo_ref[...], vo_ref[...] = _adamw_math(w_ref[...], g_ref[...], m_ref[...], v_ref[...])

    spec = pl.BlockSpec((tr, cols), lambda i: (i, 0))
    return pl.pallas_call(
        body, name="adamw_" + name, grid=(rows // tr,),
        in_specs=[spec] * 4, out_specs=[spec] * 3,
        out_shape=[jax.ShapeDtypeStruct((rows, cols), F32)] * 3,
        compiler_params=_cp("parallel"),
    )(w, g, m, v)


DEPTH = 2
BIG_SHAPES = {"w_in": ((1024, 6560), 1), "w_conv_out": ((512, 1024), 1), "w_uq": ((256, 768), 1),
              "w_ukv": ((128, 1024), 1), "w_attn_out": ((512, 1024), 1), "w_hgrn_out": ((512, 1024), 1),
              "w_out": ((1024, 1024), 0), "w_ff1": ((1024, 4096), 1), "w_ff2": ((4096, 1024), 0)}
BIG = tuple(BIG_SHAPES)
SMALL_SIZES = {"norm1_g": 1024, "conv_b": 512, "conv_ln_g": 512, "conv_ln_b": 512, "q_a_norm_g": 256,
               "kv_a_norm_g": 128, "q_norm_g": 96, "k_norm_g": 96, "hgrn_lb_logits": 512, "hgrn_norm_g": 512,
               "norm2_g": 1024}
SMALL = tuple(SMALL_SIZES)
W_IN_COLS = 6560
W_IN_SHARD = W_IN_COLS // 4
W_IN_SEGS = ((0, 1024, SEG_AG[0]), (1024, 1280, SEG_CQ[0]), (1280, 1408, SEG_CKV[0]), (1408, 1440, SEG_KR[0] + 64),
             (1440, 3488, SEG_H4[0]), (3488, 6560, SEG_GATES[0]))


def _pad_heads(w, nh, used, axis):
    shp = w.shape
    w = w.reshape(shp[:axis] + (nh, used) + shp[axis + 1:])
    pad = [(0, 0)] * w.ndim
    pad[axis + 1] = (0, LANE - used)
    w = jnp.pad(w, pad)
    return w.reshape(shp[:axis] + (nh * LANE,) + shp[axis + 1:])


def _unpad_heads(w, nh, used, axis):
    shp = w.shape
    w = w.reshape(shp[:axis] + (nh, LANE) + shp[axis + 1:])
    w = lax.slice_in_dim(w, 0, used, axis=axis + 1)
    return w.reshape(shp[:axis] + (nh * used,) + shp[axis + 1:])


def _w_in_from_chips(p4):
    def orig(a, b):
        out = []
        while a < b:
            s = a // W_IN_SHARD
            e = min(b, (s + 1) * W_IN_SHARD)
            out.append(p4[s][:, a - W_IN_SHARD * s:e - W_IN_SHARD * s])
            a = e
        return out
    zc = lambda n: jnp.zeros((D, n), p4[0].dtype)
    parts = (orig(3488, 6560) + orig(0, 1024) + orig(1440, 3488) + orig(1024, 1280) + orig(1280, 1408)
             + [zc(64)] + orig(1408, 1440) + [zc(32)])
    return jnp.concatenate(parts, axis=1)


def _w_in_grad_to_chips(dw):
    chips = []
    for s in range(4):
        a, b = W_IN_SHARD * s, W_IN_SHARD * (s + 1)
        parts = []
        for o0, o1, p0 in W_IN_SEGS:
            lo, hi = max(a, o0), min(b, o1)
            if lo < hi:
                parts.append(dw[:, p0 + lo - o0:p0 + hi - o0])
        chips.append(jnp.concatenate(parts, axis=1))
    return jnp.stack(chips)


def _cat_chips(p4, axis):
    return jnp.concatenate([p4[s] for s in range(4)], axis=axis)


EARLY = ("w_in", "w_uq", "w_ukv")
LATE = tuple(k for k in BIG if k not in EARLY)


def _prep_late(pieces):
    pc = lambda k: [pieces[k][s].astype(_MM) for s in range(4)]
    return dict(wao=_pad_heads(_cat_chips(pc("w_attn_out"), 1), NH, 64, 0), wco=_cat_chips(pc("w_conv_out"), 1),
                who=_cat_chips(pc("w_hgrn_out"), 1), wout=_cat_chips(pc("w_out"), 0),
                w1=_cat_chips(pc("w_ff1"), 1), w2=_cat_chips(pc("w_ff2"), 0))


def _prep_early(pieces, small, l):
    mm = lambda a: a.astype(_MM)
    pc = lambda k: [mm(pieces[k][s]) for s in range(4)]
    w_in_p = _w_in_from_chips(pc("w_in"))
    wuq = jnp.concatenate([_pad_heads(pc("w_uq")[s], 2, QK_DIM, 1) for s in range(4)], axis=1)
    wukv = _cat_chips(pc("w_ukv"), 1).reshape(128, NH, 128)
    wk = _pad_heads(wukv[:, :, :64].reshape(128, NH * 64), NH, 64, 1)
    wv = _pad_heads(wukv[:, :, 64:].reshape(128, NH * 64), NH, 64, 1)
    row = lambda a: a.astype(F32).reshape(1, -1)
    p = dict(
        w_in=w_in_p, w_in_t=w_in_p.T, wuq=wuq, wk=wk, wv=wv,
        g1=row(small["norm1_g"][l]), g2=row(small["norm2_g"][l]),
        cw=jnp.pad(small["conv_w"][l].astype(F32), ((0, 1), (0, 0))), cb=row(small["conv_b"][l]),
        lng=row(small["conv_ln_g"][l]), lnb=row(small["conv_ln_b"][l]),
        qag=row(small["q_a_norm_g"][l]), kvag=row(small["kv_a_norm_g"][l]),
        qng=jnp.pad(row(small["q_norm_g"][l]), ((0, 0), (0, LANE - QK_DIM))),
        kng=jnp.pad(row(small["k_norm_g"][l]), ((0, 0), (0, LANE - QK_DIM))),
        ng=row(small["hgrn_norm_g"][l]),
    )
    return p


def _rope_tables(t):
    pos = (jnp.arange(t, dtype=jnp.int32) - FRONT).astype(F32)
    inv_freq = 10000.0 ** (-jnp.arange(16, dtype=F32) / 16)
    ang = pos[:, None] * inv_freq[None, :]
    cos, sin = jnp.cos(ang), jnp.sin(ang)
    one = jnp.ones((t, 64), F32)
    z16, z32, z64 = jnp.zeros((t, 16), F32), jnp.zeros((t, 32), F32), jnp.zeros((t, 64), F32)
    c = jnp.concatenate([one, cos, cos, z32], axis=1)
    s1 = jnp.concatenate([z64, -sin, z16, z32], axis=1)
    s2 = jnp.concatenate([z64, z16, sin, z32], axis=1)
    return c, s1, s2


def _layer_fwd(x, p, lb, rope, t_end, plan=None, on_rode=None):
    gates, ag, h4, cq, ckv, kr, hb = _in_proj_fwd(x, p["g1"], p["w_in"])
    z = _conv_fwd(ag, p["cw"], p["cb"])
    q, k, v, cqn, ckvn = _mla_pre_fwd(cq, ckv, kr, p["qag"], p["wuq"], p["kvag"], p["wk"], p["wv"], p["qng"],
                                      p["kng"], *rope)
    (o_att, lse), rode = _attn_fwd(q, k, v, plan)
    if on_rode is not None:
        on_rode(rode)
    o_h, states = _hgrn_fwd(h4, lb)
    x1, mix, ca, oc, ya, yb, yc = _mix_fwd(x, z, o_att, o_h, h4, gates, p["lng"], p["lnb"], p["wco"], p["wao"],
                                           p["ng"], p["who"], p["wout"], t_end)
    x2, pre = _ffn_fwd(x1, p["g2"], p["w1"], p["w2"])
    saved = dict(x=x, gates=gates, ag=ag, h4=h4, cq=cq, ckv=ckv, kr=kr, hb=hb, z=z, q=q, k=k, v=v, cqn=cqn,
                 ckvn=ckvn, o_att=o_att, lse=lse, o_h=o_h, states=states, x1=x1, mix=mix, ca=ca, oc=oc,
                 ya=ya, yb=yb, yc=yc, pre=pre)
    return x2, saved


def _layer_bwd(dx2, s, p, lb, rope, t_end, rides=None):
    rides = rides or {}
    (dx1, h2, act, dp, dg2), rode = _ffn_bwd(dx2, s["x1"], s["pre"], p["g2"], p["w1"].T, p["w2"].T,
                                             rides.get("ffn"))
    g = {"w_ff1": _wgrad(h2, dp, "ff1", 4), "w_ff2": _wgrad(act, dx2, "ff2").reshape(4, D_FF // 4, D),
         "norm2_g": dg2.sum(0)}
    plan_mix = rides["mix"](rode, g) if "mix" in rides else None
    (dgt, dya, dyb, dyc, dz, doa, doh, dhg, dlng, dlnb, dcb, dng), rode = _mix_bwd(
        dx1, s["ya"], s["yb"], s["yc"], s["gates"], s["z"], s["o_h"], s["h4"], p["lng"], p["lnb"], p["ng"],
        p["wout"], p["wco"], p["wao"], p["who"], plan_mix)
    g["w_out"] = _wgrad(s["mix"], dx1, "out").reshape(4, D // 4, D)
    g["w_conv_out"] = _wgrad(s["ca"], dya, "conv_out", 4)
    g["w_attn_out"] = _unpad_heads(_wgrad(s["o_att"], dyb, "attn_out", 4), NH, 64, 1)
    g["w_hgrn_out"] = _wgrad(s["oc"], dyc, "hgrn_out", 4)
    g["conv_ln_g"], g["conv_ln_b"], g["conv_b"], g["hgrn_norm_g"] = dlng.sum(0), dlnb.sum(0), dcb.sum(0), dng.sum(0)
    da, dg, dcw = _conv_bwd(s["ag"], p["cw"], dz)
    g["conv_w"] = dcw[:CONV_K]
    plan_attn = rides["attn"](rode, g) if "attn" in rides else None
    (dq, dk, dv), rode_attn = _attn_bwd(s["q"], s["k"], s["v"], s["o_att"], s["lse"], doa, plan_attn)
    dcq, dckv, dkr, dqraw, dkraw, dqag, dkvag, dqng, dkng = _mla_pre_bwd(
        dq, dk, dv, s["cq"], s["ckv"], s["kr"], p["qag"], p["wuq"], p["kvag"], p["wk"], p["wv"], p["qng"],
        p["kng"], *rope)
    g["w_uq"] = _unpad_heads(_wgrad(s["cqn"], dqraw, "uq", 4), 2, QK_DIM, 2)
    dwk = _unpad_heads(_wgrad(s["ckvn"], dkraw, "uk"), NH, 64, 1).reshape(128, NH, 64)
    dwv = _unpad_heads(_wgrad(s["ckvn"], dv, "uv"), NH, 64, 1).reshape(128, NH, 64)
    g["w_ukv"] = jnp.concatenate([dwk, dwv], axis=2).reshape(128, 4, 256).transpose(1, 0, 2)
    g["q_a_norm_g"], g["kv_a_norm_g"] = dqag.sum(0), dkvag.sum(0)
    g["q_norm_g"], g["k_norm_g"] = dqng.sum(0)[:QK_DIM], dkng.sum(0)[:QK_DIM]
    plan_hgrn = rides["hgrn"](rode_attn) if "hgrn" in rides else None
    (dhq, dhf, dhi, dlb), rode_hgrn = _hgrn_bwd(s["h4"], lb, doh, s["states"], plan_hgrn)
    mm = lambda a: a.astype(_MM)
    du = jnp.concatenate([dgt, mm(da), mm(dg), mm(dhq), mm(dhf), mm(dhi), dhg, dcq, dckv, dkr], axis=1)
    dx, dg1 = _in_proj_bwd(du, s["x"], dx1, p["g1"], p["w_in_t"], t_end)
    g["norm1_g"] = dg1.sum(0)
    g["w_in"] = _w_in_grad_to_chips(_wgrad(s["hb"], du, "in"))
    return dx, g, dlb.sum(0), (rode_attn, rode_hgrn)


def _device_step(x, target, small, pieces0, pieces1=None, fwd_ride=None, bwd_rides=None):
    s_real = x.shape[0]
    t_end = ROW0 + s_real
    t = -(-t_end // LANE) * LANE
    zrow = lambda n: jnp.zeros((n, D), F32)
    xp = jnp.concatenate([zrow(FRONT), small["meta"].astype(F32), x, zrow(t - t_end)], axis=0)
    tp = jnp.concatenate([zrow(ROW0), target, zrow(t - t_end)], axis=0)
    rope = _rope_tables(t)
    logits = small["hgrn_lb_logits"].astype(F32)
    lbs = _lower_bounds_fwd(logits)
    prm0 = _prep_early(pieces0, small, 0)
    got = {}
    if fwd_ride is None:
        prm0.update(_prep_late(pieces0))
        h, sv0 = _layer_fwd(xp, prm0, lbs[0:1], rope, t_end)
    else:
        def on_rode(rode):
            late0, got["pieces1"] = fwd_ride[1](rode)
            prm0.update(_prep_late(late0))
        h, sv0 = _layer_fwd(xp, prm0, lbs[0:1], rope, t_end, fwd_ride[0], on_rode)
        pieces1 = got["pieces1"]
    prm1 = _prep_early(pieces1, small, 1)
    if fwd_ride is None:
        prm1.update(_prep_late(pieces1))
        h, sv1 = _layer_fwd(h, prm1, lbs[1:2], rope, t_end)
    else:
        h, sv1 = _layer_fwd(h, prm1, lbs[1:2], rope, t_end, fwd_ride[2],
                            lambda rode: prm1.update(_prep_late(fwd_ride[3](rode))))
    dh, lsum = _loss_head(h, tp, t_end)
    loss = jnp.sum(lsum)
    rides1, make_rides0 = (None, None) if bwd_rides is None else bwd_rides
    dh, g1, dlb1, rode1 = _layer_bwd(dh, sv1, prm1, lbs[1:2], rope, t_end, rides1)
    dh, g0, dlb0, rode = _layer_bwd(dh, sv0, prm0, lbs[0:1], rope, t_end,
                                    None if make_rides0 is None else make_rides0(g1, rode1))
    dlogits = _lower_bounds_bwd(logits, jnp.stack([dlb0, dlb1]))
    grads = [g0, g1]
    for l in range(DEPTH):
        grads[l]["hgrn_lb_logits"] = dlogits[l]
    return loss, dh[ROW0:t_end], grads, dh[FRONT:ROW0], rode


MESH = pl.DeviceIdType.MESH
_ANY = pl.BlockSpec(memory_space=pl.ANY)
SMALL_ROWS = 64
SMALL_LEN = SMALL_ROWS * 1024


def _mesh_pos():
    return lax.axis_index("x"), lax.axis_index("y"), lax.axis_index("c")


def _other_chips(x, y):
    return [(1 - x, y), (x, 1 - y), (1 - x, 1 - y)]


class _Plan:
    def __init__(self, name, ins, out_shapes, sems, start, finish, relay=None):
        self.name, self.ins, self.out_shapes, self.sems = name, list(ins), list(out_shapes), list(sems)
        self.start, self.finish, self.relay = start, finish, relay


def _run_plan(plan):
    ni, no = len(plan.ins), len(plan.out_shapes)

    def body(*refs):
        ins, outs, sems = refs[:ni], refs[ni:ni + no], refs[ni + no:]
        plan.start(ins, outs, sems)
        if plan.relay is not None:
            plan.relay(ins, outs, sems)
        plan.finish(ins, outs, sems)

    return pl.pallas_call(body, name=plan.name, in_specs=[_ANY] * ni, out_specs=[_ANY] * no,
                          out_shape=plan.out_shapes, scratch_shapes=plan.sems)(*plan.ins)


def _plan_specs(plan):
    if plan is None:
        return [], [], [], [], []
    return plan.ins, [_ANY] * len(plan.ins), [_ANY] * len(plan.out_shapes), plan.out_shapes, plan.sems


def _host_refs(refs, n_in, n_out, n_scratch, plan):
    ni = 0 if plan is None else len(plan.ins)
    no = 0 if plan is None else len(plan.out_shapes)
    o0 = n_in + ni
    s0 = o0 + n_out + no
    own = (refs[:n_in], refs[o0:o0 + n_out], refs[s0:s0 + n_scratch])
    rider = (refs[n_in:o0], refs[o0 + n_out:s0], refs[s0 + n_scratch:])
    return own, rider


def _ride(plan, rider, step, last):
    if plan is None:
        return lambda: None

    @pl.when(step == 0)
    def _():
        plan.start(*rider)

    def done():
        if plan.relay is not None:
            @pl.when(step == last - 1)
            def _():
                plan.relay(*rider)

        @pl.when(step == last)
        def _():
            plan.finish(*rider)
    return done


def _merge_plans(name, plans):
    def parts(ins, outs, sems):
        i = o = s = 0
        for p in plans:
            ni, no, ns = len(p.ins), len(p.out_shapes), len(p.sems)
            yield p, (ins[i:i + ni], outs[o:o + no], sems[s:s + ns])
            i, o, s = i + ni, o + no, s + ns

    def start(ins, outs, sems):
        for p, refs in parts(ins, outs, sems):
            p.start(*refs)

    def relay(ins, outs, sems):
        for p, refs in parts(ins, outs, sems):
            if p.relay is not None:
                p.relay(*refs)

    def finish(ins, outs, sems):
        for p, refs in parts(ins, outs, sems):
            p.finish(*refs)

    return _Plan(name, [a for p in plans for a in p.ins], [a for p in plans for a in p.out_shapes],
                 [a for p in plans for a in p.sems], start, finish, relay)


def _plan_gather(own, layer, name):
    nw = len(own)

    def copies(ins, outs, sems):
        send_sems, recv_sems = sems

        def over_ici(w, j, chip_of_data, to):
            return pltpu.make_async_remote_copy(
                src_ref=ins[w].at[layer], dst_ref=outs[w].at[chip_of_data], send_sem=send_sems.at[w, j],
                recv_sem=recv_sems.at[w, j], device_id=to, device_id_type=MESH)

        def over_d2d(w, j, chip_of_data, to):
            return pltpu.make_async_remote_copy(
                src_ref=outs[w].at[chip_of_data], dst_ref=outs[w].at[chip_of_data], send_sem=send_sems.at[w, 3 + j],
                recv_sem=recv_sems.at[w, 3 + j], device_id=to, device_id_type=MESH)
        return over_ici, over_d2d

    def start(ins, outs, sems):
        x, y, c = _mesh_pos()
        over_ici, _ = copies(ins, outs, sems)

        @pl.when(c == layer)
        def _():
            for j, (px, py) in enumerate(_other_chips(x, y)):
                for w in range(nw):
                    over_ici(w, j, 2 * x + y, (px, py, layer)).start()

    def relay(ins, outs, sems):
        x, y, c = _mesh_pos()
        over_ici, over_d2d = copies(ins, outs, sems)

        @pl.when(c == layer)
        def _():
            for j, (px, py) in enumerate(_other_chips(x, y)):
                for w in range(nw):
                    over_ici(w, j, 2 * px + py, (x, y, c)).wait_recv()
                    over_d2d(w, j, 2 * px + py, (x, y, 1 - layer)).start()

    def finish(ins, outs, sems):
        x, y, c = _mesh_pos()
        over_ici, over_d2d = copies(ins, outs, sems)
        chips = _other_chips(x, y)

        @pl.when(c == layer)
        def _():
            for j, (px, py) in enumerate(chips):
                for w in range(nw):
                    over_ici(w, j, 2 * x + y, (px, py, layer)).wait_send()
                    over_d2d(w, j, 2 * px + py, (x, y, 1 - layer)).wait_send()

        @pl.when(c != layer)
        def _():
            for j, (px, py) in enumerate(chips):
                for w in range(nw):
                    over_d2d(w, j, 2 * px + py, (x, y, c)).wait_recv()

    return _Plan(name, own,
                 [jax.ShapeDtypeStruct((4,) + a.shape[1:], a.dtype) for a in own],
                 [pltpu.SemaphoreType.DMA((nw, 6)), pltpu.SemaphoreType.DMA((nw, 6))], start, finish, relay)


def _plan_to_sibling(arrs, layer, name):
    nw = len(arrs)

    def copy(ins, outs, sems, w):
        x, y, _ = _mesh_pos()
        return pltpu.make_async_remote_copy(src_ref=ins[w], dst_ref=outs[w], send_sem=sems[0].at[w],
                                            recv_sem=sems[1].at[w], device_id=(x, y, layer), device_id_type=MESH)

    def start(ins, outs, sems):
        @pl.when(lax.axis_index("c") != layer)
        def _():
            for w in range(nw):
                copy(ins, outs, sems, w).start()

    def finish(ins, outs, sems):
        c = lax.axis_index("c")

        @pl.when(c != layer)
        def _():
            for w in range(nw):
                copy(ins, outs, sems, w).wait_send()

        @pl.when(c == layer)
        def _():
            for w in range(nw):
                copy(ins, outs, sems, w).wait_recv()

    return _Plan(name, arrs, [jax.ShapeDtypeStruct(a.shape, a.dtype) for a in arrs],
                 [pltpu.SemaphoreType.DMA((nw,)), pltpu.SemaphoreType.DMA((nw,))], start, finish)


def _plan_scatter(parts, layer, name):
    nw = len(parts)

    def start(ins, outs, sems):
        x, y, c = _mesh_pos()

        @pl.when(c == layer)
        def _():
            for j, (px, py) in enumerate(_other_chips(x, y)):
                for w in range(nw):
                    pltpu.make_async_remote_copy(
                        src_ref=ins[w].at[2 * px + py], dst_ref=outs[w].at[2 * x + y], send_sem=sems[0].at[w, j],
                        recv_sem=sems[1].at[w, j], device_id=(px, py, layer), device_id_type=MESH).start()

    def finish(ins, outs, sems):
        x, y, c = _mesh_pos()

        @pl.when(c == layer)
        def _():
            for j, (px, py) in enumerate(_other_chips(x, y)):
                for w in range(nw):
                    pltpu.make_async_remote_copy(
                        src_ref=ins[w].at[2 * px + py], dst_ref=outs[w].at[2 * px + py], send_sem=sems[0].at[w, j],
                        recv_sem=sems[1].at[w, j], device_id=(x, y, c), device_id_type=MESH).wait()

    return _Plan(name, parts, [jax.ShapeDtypeStruct(a.shape, a.dtype) for a in parts],
                 [pltpu.SemaphoreType.DMA((nw, 3)), pltpu.SemaphoreType.DMA((nw, 3))], start, finish)


def _sibling_exchange(reds0, reds1):
    nw = len(reds0)

    def body(*refs):
        a0, a1, outs = refs[:nw], refs[nw:2 * nw], refs[2 * nw:3 * nw]
        send_sems, recv_sems = refs[3 * nw:]
        x, y, c = _mesh_pos()

        def copy(w, src):
            return pltpu.make_async_remote_copy(src_ref=src, dst_ref=outs[w], send_sem=send_sems.at[w],
                                                recv_sem=recv_sems.at[w], device_id=(x, y, 1 - c),
                                                device_id_type=MESH)

        @pl.when(c == 0)
        def _():
            for w in range(nw):
                copy(w, a0[w]).start()

        @pl.when(c == 1)
        def _():
            for w in range(nw):
                copy(w, a1[w]).start()

        for w in range(nw):
            copy(w, a0[w]).wait()

    return pl.pallas_call(
        body, name="sibling_exchange", in_specs=[_ANY] * (2 * nw), out_specs=[_ANY] * nw,
        out_shape=[jax.ShapeDtypeStruct(a.shape, a.dtype) for a in reds0],
        scratch_shapes=[pltpu.SemaphoreType.DMA((nw,)), pltpu.SemaphoreType.DMA((nw,))],
    )(*reds0, *reds1)


def _all_reduce_small(v, name):
    rows, cols = v.shape

    def body(v_ref, o_ref, slots, send_sems, recv_sems):
        x, y, c = _mesh_pos()
        me = 4 * x + 2 * y + c
        slots[me] = v_ref[...]
        peers = []
        for rel in range(1, 8):
            fx, fy, fc = (rel >> 2) & 1, (rel >> 1) & 1, rel & 1
            px = 1 - x if fx else x
            py = 1 - y if fy else y
            pc = 1 - c if fc else c
            peers.append((px, py, pc))
        cps = [pltpu.make_async_remote_copy(src_ref=v_ref, dst_ref=slots.at[me], send_sem=send_sems.at[k],
                                            recv_sem=recv_sems.at[k], device_id=peer, device_id_type=MESH)
               for k, peer in enumerate(peers)]
        for cp in cps:
            cp.start()
        for k, (px, py, pc) in enumerate(peers):
            pltpu.make_async_remote_copy(src_ref=v_ref, dst_ref=slots.at[4 * px + 2 * py + pc],
                                         send_sem=send_sems.at[k], recv_sem=recv_sems.at[k], device_id=(x, y, c),
                                         device_id_type=MESH).wait_recv()
        for cp in cps:
            cp.wait_send()
        acc = slots[0]
        for d in range(1, 8):
            acc = acc + slots[d]
        o_ref[...] = acc

    vm = pl.BlockSpec(memory_space=pltpu.VMEM)
    return pl.pallas_call(
        body, name=name, in_specs=[vm], out_specs=vm,
        out_shape=jax.ShapeDtypeStruct((rows, cols), F32),
        scratch_shapes=[pltpu.VMEM((8, rows, cols), F32), pltpu.SemaphoreType.DMA((7,)),
                        pltpu.SemaphoreType.DMA((7,))],
    )(v)


def _add_to_wire(a, b, name):
    n4, r, c_ = a.shape
    rows = n4 * r
    tr = _ew_tile(rows, c_, 3)

    def body(a_ref, b_ref, o_ref):
        o_ref[...] = (a_ref[...] + b_ref[...]).astype(o_ref.dtype)

    spec = pl.BlockSpec((tr, c_), lambda i: (i, 0))
    out = pl.pallas_call(
        body, name="add_to_wire_" + name, grid=(rows // tr,), in_specs=[spec, spec], out_specs=spec,
        out_shape=jax.ShapeDtypeStruct((rows, c_), jnp.bfloat16), compiler_params=_cp("parallel"),
    )(a.reshape(rows, c_), b.reshape(rows, c_))
    return out.reshape(n4, r, c_)


def _sum_chips(recv, wire, name):
    _, r, c_ = recv.shape
    tr = _ew_tile(r, c_, 6)

    def body(r_ref, w_ref, o_ref):
        chip = 2 * lax.axis_index("x") + lax.axis_index("y")
        acc = None
        for s in range(4):
            term = jnp.where(chip == s, w_ref[s], r_ref[s]).astype(F32)
            acc = term if acc is None else acc + term
        o_ref[...] = acc

    blk = pl.BlockSpec((4, tr, c_), lambda i: (0, i, 0))
    return pl.pallas_call(
        body, name="sum_chips_" + name, grid=(r // tr,),
        in_specs=[blk, blk], out_specs=pl.BlockSpec((tr, c_), lambda i: (i, 0)),
        out_shape=jax.ShapeDtypeStruct((r, c_), F32),
        compiler_params=_cp("parallel"),
    )(recv, wire)


def _pack_small(vals, meta_full, conv_w_full):
    flat = jnp.concatenate([vals[k].reshape(-1) for k in SMALL] + [meta_full.reshape(-1), conv_w_full.reshape(-1)])
    return jnp.pad(flat, (0, SMALL_LEN - flat.shape[0])).reshape(SMALL_ROWS, 1024)


def _unpack_small(buf):
    flat = buf.reshape(-1)
    out, off = {}, 0
    for k in SMALL:
        n = DEPTH * SMALL_SIZES[k]
        out[k] = flat[off:off + n].reshape(DEPTH, SMALL_SIZES[k])
        off += n
    meta = flat[off:off + N_META * D].reshape(N_META, D)
    off += N_META * D
    conv_w = flat[off:off + DEPTH * CONV_K * CONV_DIM].reshape(DEPTH, CONV_K, CONV_DIM)
    return out, meta, conv_w


def kernel(x, meta, norm1_g, w_in, conv_w, conv_b, conv_ln_g, conv_ln_b, w_conv_out, q_a_norm_g, w_uq, kv_a_norm_g, w_ukv, q_norm_g, k_norm_g, w_attn_out, hgrn_lb_logits, hgrn_norm_g, w_hgrn_out, w_out, norm2_g, w_ff1, w_ff2, loss_target, m_meta, m_norm1_g, m_w_in, m_conv_w, m_conv_b, m_conv_ln_g, m_conv_ln_b, m_w_conv_out, m_q_a_norm_g, m_w_uq, m_kv_a_norm_g, m_w_ukv, m_q_norm_g, m_k_norm_g, m_w_attn_out, m_hgrn_lb_logits, m_hgrn_norm_g, m_w_hgrn_out, m_w_out, m_norm2_g, m_w_ff1, m_w_ff2, v_meta, v_norm1_g, v_w_in, v_conv_w, v_conv_b, v_conv_ln_g, v_conv_ln_b, v_w_conv_out, v_q_a_norm_g, v_w_uq, v_kv_a_norm_g, v_w_ukv, v_q_norm_g, v_k_norm_g, v_w_attn_out, v_hgrn_lb_logits, v_hgrn_norm_g, v_w_hgrn_out, v_w_out, v_norm2_g, v_w_ff1, v_w_ff2):
    names = ("meta", "norm1_g", "w_in", "conv_w", "conv_b", "conv_ln_g", "conv_ln_b", "w_conv_out", "q_a_norm_g",
             "w_uq", "kv_a_norm_g", "w_ukv", "q_norm_g", "k_norm_g", "w_attn_out", "hgrn_lb_logits", "hgrn_norm_g",
             "w_hgrn_out", "w_out", "norm2_g", "w_ff1", "w_ff2")
    w = dict(zip(names, (meta, norm1_g, w_in, conv_w, conv_b, conv_ln_g, conv_ln_b, w_conv_out, q_a_norm_g, w_uq,
                         kv_a_norm_g, w_ukv, q_norm_g, k_norm_g, w_attn_out, hgrn_lb_logits, hgrn_norm_g, w_hgrn_out,
                         w_out, norm2_g, w_ff1, w_ff2)))
    m = dict(zip(names, (m_meta, m_norm1_g, m_w_in, m_conv_w, m_conv_b, m_conv_ln_g, m_conv_ln_b, m_w_conv_out,
                         m_q_a_norm_g, m_w_uq, m_kv_a_norm_g, m_w_ukv, m_q_norm_g, m_k_norm_g, m_w_attn_out,
                         m_hgrn_lb_logits, m_hgrn_norm_g, m_w_hgrn_out, m_w_out, m_norm2_g, m_w_ff1, m_w_ff2)))
    v = dict(zip(names, (v_meta, v_norm1_g, v_w_in, v_conv_w, v_conv_b, v_conv_ln_g, v_conv_ln_b, v_w_conv_out,
                         v_q_a_norm_g, v_w_uq, v_kv_a_norm_g, v_w_ukv, v_q_norm_g, v_k_norm_g, v_w_attn_out,
                         v_hgrn_lb_logits, v_hgrn_norm_g, v_w_hgrn_out, v_w_out, v_norm2_g, v_w_ff1, v_w_ff2)))
    cx, cy, cc = _mesh_pos()
    chip = 2 * cx + cy
    zero = jnp.zeros((), jnp.int32)

    own = {k: w[k].astype(_MM) for k in BIG}

    def as_pieces(names, gathered, layer):
        return {k: [jnp.where(chip == s, own[k][layer], g[s]) for s in range(4)] for k, g in zip(names, gathered)}

    pieces0 = as_pieces(EARLY, _run_plan(_plan_gather([own[k] for k in EARLY], 0, "gather_l0_early")), 0)
    fwd_ride = (_merge_plans("gather_mid", [_plan_gather([own[k] for k in LATE], 0, "gather_l0_late"),
                                            _plan_gather([own[k] for k in EARLY], 1, "gather_l1_early")]),
                lambda got: (as_pieces(LATE, got[:len(LATE)], 0), as_pieces(EARLY, got[len(LATE):], 1)),
                _plan_gather([own[k] for k in LATE], 1, "gather_l1_late"),
                lambda got: as_pieces(LATE, got, 1))
    meta_slab = lax.dynamic_update_slice(jnp.zeros((N_META, D), F32), meta, (zero, chip * (D // 4)))
    convw_slab = lax.dynamic_update_slice(jnp.zeros((DEPTH, CONV_K, CONV_DIM), F32), conv_w,
                                          (zero, zero, chip * (CONV_DIM // 4)))
    zsmall = {k: jnp.zeros((DEPTH, SMALL_SIZES[k]), F32) for k in SMALL}
    south = (cc == 0).astype(F32)
    _, meta_full, convw_full = _unpack_small(
        _all_reduce_small(_pack_small(zsmall, meta_slab, convw_slab) * south, "gather_small"))
    small = {k: w[k] for k in SMALL}
    small["meta"] = meta_full
    small["conv_w"] = convw_full

    FFN = ("w_ff1", "w_ff2")
    MID = ("w_out", "w_conv_out", "w_attn_out", "w_hgrn_out")
    REST = tuple(k for k in BIG if k not in FFN + MID)
    held = {}

    def to_wire(names, layer, mine, from_sibling):
        return lax.cond(
            cc == layer,
            lambda: [_add_to_wire(a, b, "%s_l%d" % (k, layer)) for k, a, b in zip(names, mine, from_sibling)],
            lambda: [jnp.zeros(a.shape, jnp.bfloat16) for a in mine])

    def chip_sum(names, layer, got, wire):
        return lax.cond(
            cc == layer,
            lambda: [_sum_chips(r, s, "%s_l%d" % (k, layer)) for k, r, s in zip(names, got, wire)],
            lambda: [jnp.zeros(s.shape[1:], F32) for s in wire])

    NONFFN = tuple(k for k in BIG if k not in FFN)

    def ride_attn_l1(_, g1):
        held["g1_ffn"] = [g1[k] for k in FFN]
        return _plan_to_sibling(held["g1_ffn"], 1, "swap_grads_l1_ffn")

    def rides_l0(g1, rode_l1):
        g1_rest = [g1[k] for k in NONFFN]

        def ride_mix(from_sibling1, g0_ffn):
            wire1 = dict(zip(FFN, to_wire(FFN, 1, held["g1_ffn"], rode_l1[0])))
            wire1.update(zip(NONFFN, to_wire(NONFFN, 1, g1_rest, from_sibling1)))
            held["wire1"] = [wire1[k] for k in BIG]
            held["g0_ffn"] = [g0_ffn[k] for k in FFN]
            return _plan_to_sibling(held["g0_ffn"], 0, "swap_grads_l0_ffn")

        def ride_attn(from_sibling0, g0):
            held["wire0_ffn"] = to_wire(FFN, 0, held["g0_ffn"], from_sibling0)
            held["g0_mid"] = [g0[k] for k in MID]
            return _merge_plans("exchange_grads_mid", [
                _plan_scatter(held["wire1"], 1, "scatter_grads_l1"),
                _plan_scatter(held["wire0_ffn"], 0, "scatter_grads_l0_ffn"),
                _plan_to_sibling(held["g0_mid"], 0, "swap_grads_l0_mid")])

        def ride_hgrn(rode_attn):
            held["wire0_mid"] = to_wire(MID, 0, held["g0_mid"], rode_attn[len(BIG) + len(FFN):])
            return _plan_scatter(held["wire0_mid"], 0, "scatter_grads_l0_mid")

        return {"ffn": _plan_to_sibling(g1_rest, 1, "swap_grads_l1_rest"), "mix": ride_mix, "attn": ride_attn,
                "hgrn": ride_hgrn}

    loss_share, grad_x, gl, g_meta, (got, got_mid) = _device_step(
        x[0], loss_target[0], small, pieces0, None, fwd_ride, ({"attn": ride_attn_l1}, rides_l0))

    reds1 = chip_sum(BIG, 1, got[:len(BIG)], held["wire1"])
    reds0 = dict(zip(FFN, chip_sum(FFN, 0, got[len(BIG):len(BIG) + len(FFN)], held["wire0_ffn"])))
    reds0.update(zip(MID, chip_sum(MID, 0, got_mid, held["wire0_mid"])))
    g0_rest = [gl[0][k] for k in REST]
    wire0 = to_wire(REST, 0, g0_rest, _run_plan(_plan_to_sibling(g0_rest, 0, "swap_grads_l0_rest")))
    reds0.update(zip(REST, chip_sum(REST, 0, _run_plan(_plan_scatter(wire0, 0, "scatter_grads_l0_rest")), wire0)))
    reds0 = [reds0[k] for k in BIG]
    reds_sibling = _sibling_exchange(reds0, reds1)
    grads, delta, new_m, new_v = {}, {}, {}, {}
    t_view = lambda k, a: jnp.swapaxes(a, -1, -2) if k == "w_in" else a
    for k, r0, r1, theirs in zip(BIG, reds0, reds1, reds_sibling):
        grads[k], delta[k], new_m[k], new_v[k] = [
            t_view(k, a) for a in _adamw_layers(t_view(k, w[k]), t_view(k, m[k]), t_view(k, v[k]), t_view(k, r0),
                                                t_view(k, r1), t_view(k, theirs), k)]

    g_small_local = {k: jnp.stack([gl[l][k] for l in range(DEPTH)]) for k in SMALL}
    g_convw_local = jnp.stack([gl[l]["conv_w"] for l in range(DEPTH)])
    reduced = _all_reduce_small(
        _pack_small(g_small_local, g_meta, g_convw_local).at[SMALL_ROWS - 1, 1023].set(loss_share), "reduce_small")
    loss = reduced[SMALL_ROWS - 1, 1023]
    g_small, g_meta_full, g_convw_full = _unpack_small(reduced)
    grads.update(g_small)
    grads["meta"] = lax.dynamic_slice(g_meta_full, (zero, chip * (D // 4)), (N_META, D // 4))
    grads["conv_w"] = lax.dynamic_slice(g_convw_full, (zero, zero, chip * (CONV_DIM // 4)),
                                        (DEPTH, CONV_K, CONV_DIM // 4))

    def small_pack(src):
        return _pack_small(src, jnp.pad(src["meta"], ((0, 0), (0, D - D // 4))),
                           jnp.pad(src["conv_w"], ((0, 0), (0, 0), (0, CONV_DIM - CONV_DIM // 4))))

    def small_unpack(buf):
        out, meta_p, convw_p = _unpack_small(buf)
        out["meta"] = meta_p[:, :D // 4]
        out["conv_w"] = convw_p[:, :, :CONV_DIM // 4]
        return out

    d_s, m_s, v_s = [small_unpack(a) for a in _adamw(small_pack(w), small_pack(grads), small_pack(m),
                                                     small_pack(v), "small")]
    delta.update(d_s)
    new_m.update(m_s)
    new_v.update(v_s)
    return (loss, grad_x[None], *[grads[k] for k in names], *[delta[k] for k in names],
            *[new_m[k] for k in names], *[new_v[k] for k in names])
```

```python
import functools

import jax
import jax.numpy as jnp
from jax import lax
from jax.experimental import pallas as pl
from jax.experimental.pallas import tpu as pltpu

F32 = jnp.float32
_MM = jnp.bfloat16

D = 1024
N_META = 16
FRONT = 48
ROW0 = FRONT + N_META
EPS = 1e-6
GATE_CLAMP = 1.0 - 1e-6
CONV_K = 31
CONV_DIM = 512
NH = 8
QK_DIM = 96
ATT_SCALE = QK_DIM ** -0.5
HH = 4
CHUNK = 64
SUB = 16
EXP_CLIP = 60.0
NEG = -1e30
LANE = 128

SEG_GATES = (0, 3072)
SEG_AG = (3072, 4096)
SEG_H4 = (4096, 6144)
SEG_CQ = (6144, 6400)
SEG_CKV = (6400, 6528)
SEG_KR = (6528, 6656)
N_IN_P = 6656

ADAM_LR = 0.001
ADAM_B1 = 0.9
ADAM_B2 = 0.999
ADAM_EPS = 1e-08
ADAM_WD = 0.01
ADAM_STEP = 10

VMEM_LIMIT = 56 * 1024 * 1024


def _tile(n, pref):
    best = 64
    for t in range(64, pref + 1, 64):
        if n % t == 0:
            best = t
    return best


def _cp(*sem):
    return pltpu.CompilerParams(dimension_semantics=tuple(sem), vmem_limit_bytes=VMEM_LIMIT)


def _row(tm, n, col=0):
    return pl.BlockSpec((tm, n), lambda i: (i, col))


def _full(shape):
    return pl.BlockSpec(shape, lambda i: (0,) * len(shape))


def _mm(a, b):
    return jnp.dot(a.astype(_MM), b.astype(_MM), preferred_element_type=F32)


def _mm_nt(a, b):
    return lax.dot_general(a.astype(_MM), b.astype(_MM), (((1,), (1,)), ((), ())), preferred_element_type=F32)


def _mm_tn(a, b):
    return lax.dot_general(a.astype(_MM), b.astype(_MM), (((0,), (0,)), ((), ())), preferred_element_type=F32)


def _split3(x):
    hi = x.astype(jnp.bfloat16)
    return hi, (x - hi.astype(F32)).astype(jnp.bfloat16)


def _dot3(a, b, dims):
    ah, al = _split3(a)
    bh, bl = _split3(b)
    dg = lambda u, v: lax.dot_general(u, v, (dims, ((), ())), preferred_element_type=F32)
    return dg(ah, bh) + (dg(ah, bl) + dg(al, bh))


def _hmm(a, b):
    return _dot3(a, b, ((1,), (0,)))


def _hmm_nt(a, b):
    return _dot3(a, b, ((1,), (1,)))


def _hmm_tn(a, b):
    return _dot3(a, b, ((0,), (0,)))


def _sigmoid(x):
    return 1.0 / (1.0 + jnp.exp(-x))


def _rstd(x, n=None):
    n = x.shape[-1] if n is None else n
    return lax.rsqrt(jnp.sum(x * x, axis=-1, keepdims=True) * (1.0 / n) + EPS)


def _rms_bwd(dy, x, rstd, g, n=None):
    n = x.shape[-1] if n is None else n
    xh = x * rstd
    dxh = dy * g
    dx = rstd * (dxh - xh * (jnp.sum(dxh * xh, axis=-1, keepdims=True) * (1.0 / n)))
    return dx, dy * xh


def _valid_rows(i, tm, t_valid_end):
    r = i * tm + lax.broadcasted_iota(jnp.int32, (tm, 1), 0)
    return ((r >= FRONT) & (r < t_valid_end)).astype(F32)


def _colsum8(x):
    n, c = x.shape
    return jnp.sum(x.reshape(n // 8, 8, c), axis=0)


def _in_proj_fwd(x, g1, w):
    t = x.shape[0]
    tm = _tile(t, 192)
    segs = (SEG_GATES, SEG_AG, SEG_H4, SEG_CQ, SEG_CKV, SEG_KR)

    def body(x_ref, g_ref, w_ref, gates_ref, ag_ref, h4_ref, cq_ref, ckv_ref, kr_ref, hb_ref):
        xv = x_ref[...]
        hb = (xv * _rstd(xv) * g_ref[...]).astype(_MM)
        hb_ref[...] = hb
        for ref, (a, b) in zip((gates_ref, ag_ref, h4_ref, cq_ref, ckv_ref, kr_ref), segs):
            ref[...] = jnp.dot(hb, w_ref[:, a:b], preferred_element_type=F32)

    outs = [jax.ShapeDtypeStruct((t, b - a), F32) for a, b in segs] + [jax.ShapeDtypeStruct((t, D), _MM)]
    return pl.pallas_call(
        body, name="in_proj_fwd", grid=(t // tm,),
        in_specs=[_row(tm, D), _full((1, D)), _full((D, N_IN_P))],
        out_specs=[_row(tm, b - a) for a, b in segs] + [_row(tm, D)],
        out_shape=outs, compiler_params=_cp("parallel"),
    )(x, g1, w)


def _in_proj_bwd(du, x, dx1, g1, wt, t_end):
    t = x.shape[0]
    tm = _tile(t, 384)

    def body(du_ref, x_ref, dx1_ref, g_ref, wt_ref, dx_ref, dg_ref):
        i = pl.program_id(0)
        dh = jnp.dot(du_ref[...], wt_ref[...], preferred_element_type=F32)
        xv = x_ref[...]
        dxn, dgrow = _rms_bwd(dh, xv, _rstd(xv), g_ref[...])
        dx_ref[...] = _valid_rows(i, tm, t_end) * (dx1_ref[...] + dxn)

        @pl.when(i == 0)
        def _():
            dg_ref[...] = jnp.zeros_like(dg_ref)
        dg_ref[...] += _colsum8(dgrow)

    return pl.pallas_call(
        body, name="in_proj_bwd", grid=(t // tm,),
        in_specs=[_row(tm, N_IN_P), _row(tm, D), _row(tm, D), _full((1, D)), _full((N_IN_P, D))],
        out_specs=[_row(tm, D), _full((8, D))],
        out_shape=[jax.ShapeDtypeStruct((t, D), F32), jax.ShapeDtypeStruct((8, D), F32)],
        compiler_params=_cp("arbitrary"),
    )(du, x, dx1, g1, wt)


CONV_CH = 128


def _conv_fwd(ag, cw, cb):
    t = ag.shape[0]
    n = t // CONV_CH

    def body(a_ref, g_ref, w_ref, b_ref, z_ref, hp):
        hp[0:32, :] = jnp.zeros((32, LANE), F32)

        def fill(i, c):
            r = pl.multiple_of(i * CONV_CH, CONV_CH)
            hp[pl.ds(32 + r, CONV_CH), :] = a_ref[pl.ds(r, CONV_CH), :] * _sigmoid(g_ref[pl.ds(r, CONV_CH), :])
            return c
        lax.fori_loop(0, n, fill, 0)

        def conv(i, c):
            r = pl.multiple_of(i * CONV_CH, CONV_CH)
            acc = jnp.broadcast_to(b_ref[...], (CONV_CH, LANE))
            for k in range(CONV_K):
                acc = acc + w_ref[k:k + 1, :] * hp[pl.ds(r + (k + 2), CONV_CH), :]
            z_ref[pl.ds(r, CONV_CH), :] = acc
            return c
        lax.fori_loop(0, n, conv, 0)

    nb = CONV_DIM // LANE
    return pl.pallas_call(
        body, name="conv_fwd", grid=(nb,),
        in_specs=[pl.BlockSpec((t, LANE), lambda j: (0, j)), pl.BlockSpec((t, LANE), lambda j: (0, nb + j)),
                  pl.BlockSpec((32, LANE), lambda j: (0, j)), pl.BlockSpec((1, LANE), lambda j: (0, j))],
        out_specs=pl.BlockSpec((t, LANE), lambda j: (0, j)),
        out_shape=jax.ShapeDtypeStruct((t, CONV_DIM), F32),
        scratch_shapes=[pltpu.VMEM((t + 32, LANE), F32)],
        compiler_params=_cp("parallel"),
    )(ag, ag, cw, cb)


def _conv_bwd(ag, cw, dz):
    t = ag.shape[0]
    n = t // CONV_CH

    def body(a_ref, g_ref, w_ref, dz_ref, da_ref, dg_ref, dcw_ref, hp, dzp, accw):
        hp[0:32, :] = jnp.zeros((32, LANE), F32)
        dzp[pl.ds(t, 32), :] = jnp.zeros((32, LANE), F32)
        accw[...] = jnp.zeros_like(accw)

        def fill(i, c):
            r = pl.multiple_of(i * CONV_CH, CONV_CH)
            hp[pl.ds(32 + r, CONV_CH), :] = a_ref[pl.ds(r, CONV_CH), :] * _sigmoid(g_ref[pl.ds(r, CONV_CH), :])
            dzp[pl.ds(r, CONV_CH), :] = dz_ref[pl.ds(r, CONV_CH), :]
            return c
        lax.fori_loop(0, n, fill, 0)

        def step(i, c):
            r = pl.multiple_of(i * CONV_CH, CONV_CH)
            dzc = dz_ref[pl.ds(r, CONV_CH), :]
            dh = jnp.zeros((CONV_CH, LANE), F32)
            for k in range(CONV_K):
                dh = dh + w_ref[k:k + 1, :] * dzp[pl.ds(r + (CONV_K - 1 - k), CONV_CH), :]
                accw[8 * k:8 * k + 8, :] += _colsum8(dzc * hp[pl.ds(r + (k + 2), CONV_CH), :])
            a = a_ref[pl.ds(r, CONV_CH), :]
            sg = _sigmoid(g_ref[pl.ds(r, CONV_CH), :])
            da_ref[pl.ds(r, CONV_CH), :] = dh * sg
            dg_ref[pl.ds(r, CONV_CH), :] = dh * a * sg * (1.0 - sg)
            return c
        lax.fori_loop(0, n, step, 0)

        for k in range(CONV_K):
            dcw_ref[k:k + 1, :] = jnp.sum(accw[8 * k:8 * k + 8, :], axis=0, keepdims=True)
        dcw_ref[CONV_K:32, :] = jnp.zeros((32 - CONV_K, LANE), F32)

    nb = CONV_DIM // LANE
    colspec = pl.BlockSpec((t, LANE), lambda j: (0, j))
    return pl.pallas_call(
        body, name="conv_bwd", grid=(nb,),
        in_specs=[colspec, pl.BlockSpec((t, LANE), lambda j: (0, nb + j)),
                  pl.BlockSpec((32, LANE), lambda j: (0, j)), colspec],
        out_specs=[colspec, colspec, pl.BlockSpec((32, LANE), lambda j: (0, j))],
        out_shape=[jax.ShapeDtypeStruct((t, CONV_DIM), F32), jax.ShapeDtypeStruct((t, CONV_DIM), F32),
                   jax.ShapeDtypeStruct((32, CONV_DIM), F32)],
        scratch_shapes=[pltpu.VMEM((t + 32, LANE), F32), pltpu.VMEM((t + 32, LANE), F32),
                        pltpu.VMEM((8 * 32, LANE), F32)],
        compiler_params=_cp("parallel"),
    )(ag, ag, cw, dz)


def _rope(x, c, s1, s2):
    return x * c + pltpu.roll(x, LANE - 16, 1) * s1 + pltpu.roll(x, 16, 1) * s2


def _rope_t(dy, c, s1, s2):
    return dy * c + pltpu.roll(dy * s1, 16, 1) + pltpu.roll(dy * s2, LANE - 16, 1)


def _mla_pre_fwd(cq, ckv, kr, qag, wuq, kvag, wk, wv, qng, kng, rc, rs1, rs2):
    t = cq.shape[0]
    tm = _tile(t, 384)

    def body(cq_ref, ckv_ref, kr_ref, qag_ref, wuq_ref, kvag_ref, wk_ref, wv_ref, qng_ref, kng_ref,
             c_ref, s1_ref, s2_ref, q_ref, k_ref, v_ref, cqn_ref, ckvn_ref):
        cqv = cq_ref[...]
        cqn = (cqv * _rstd(cqv) * qag_ref[...]).astype(_MM)
        cqn_ref[...] = cqn
        ckvv = ckv_ref[...]
        ckvn = (ckvv * _rstd(ckvv) * kvag_ref[...]).astype(_MM)
        ckvn_ref[...] = ckvn
        qraw = jnp.dot(cqn, wuq_ref[...], preferred_element_type=F32)
        kraw = jnp.dot(ckvn, wk_ref[...], preferred_element_type=F32)
        v_ref[...] = jnp.dot(ckvn, wv_ref[...], preferred_element_type=F32).astype(_MM)
        krv = kr_ref[...]
        c, s1, s2 = c_ref[...], s1_ref[...], s2_ref[...]
        for h in range(NH):
            sl = slice(LANE * h, LANE * (h + 1))
            qh = qraw[:, sl]
            qn = qh * _rstd(qh, QK_DIM) * qng_ref[...]
            q_ref[:, sl] = (_rope(qn, c, s1, s2) * ATT_SCALE).astype(_MM)
            kh = kraw[:, sl] + krv
            kn = kh * _rstd(kh, QK_DIM) * kng_ref[...]
            k_ref[:, sl] = _rope(kn, c, s1, s2).astype(_MM)

    hd = NH * LANE
    return pl.pallas_call(
        body, name="mla_pre_fwd", grid=(t // tm,),
        in_specs=[_row(tm, 256), _row(tm, 128), _row(tm, 128), _full((1, 256)), _full((256, hd)),
                  _full((1, 128)), _full((128, hd)), _full((128, hd)), _full((1, LANE)), _full((1, LANE)),
                  _row(tm, LANE), _row(tm, LANE), _row(tm, LANE)],
        out_specs=[_row(tm, hd), _row(tm, hd), _row(tm, hd), _row(tm, 256), _row(tm, 128)],
        out_shape=[jax.ShapeDtypeStruct((t, hd), _MM)] * 3 + [jax.ShapeDtypeStruct((t, 256), _MM),
                                                              jax.ShapeDtypeStruct((t, 128), _MM)],
        compiler_params=_cp("parallel"),
    )(cq, ckv, kr, qag, wuq, kvag, wk, wv, qng, kng, rc, rs1, rs2)


def _mla_pre_bwd(dq, dk, dv, cq, ckv, kr, qag, wuq, kvag, wk, wv, qng, kng, rc, rs1, rs2):
    t = cq.shape[0]
    tm = _tile(t, 384)
    hd = NH * LANE

    def body(dq_ref, dk_ref, dv_ref, cq_ref, ckv_ref, kr_ref, qag_ref, wuq_ref, kvag_ref, wk_ref,
             wv_ref, qng_ref, kng_ref, c_ref, s1_ref, s2_ref,
             dcq_ref, dckv_ref, dkr_ref, dqraw_ref, dkraw_ref, dqag_ref, dkvag_ref, dqng_ref, dkng_ref):
        i = pl.program_id(0)
        cqv = cq_ref[...]
        rq_in = _rstd(cqv)
        cqn = (cqv * rq_in * qag_ref[...]).astype(_MM)
        ckvv = ckv_ref[...]
        rkv_in = _rstd(ckvv)
        ckvn = (ckvv * rkv_in * kvag_ref[...]).astype(_MM)
        qraw = jnp.dot(cqn, wuq_ref[...], preferred_element_type=F32)
        kraw = jnp.dot(ckvn, wk_ref[...], preferred_element_type=F32)
        krv = kr_ref[...]
        c, s1, s2 = c_ref[...], s1_ref[...], s2_ref[...]
        dkr = jnp.zeros((tm, LANE), F32)
        dqng = jnp.zeros((8, LANE), F32)
        dkng = jnp.zeros((8, LANE), F32)
        for h in range(NH):
            sl = slice(LANE * h, LANE * (h + 1))
            qh = qraw[:, sl]
            dqn = _rope_t(dq_ref[:, sl] * ATT_SCALE, c, s1, s2)
            dqh, gq = _rms_bwd(dqn, qh, _rstd(qh, QK_DIM), qng_ref[...], QK_DIM)
            dqraw_ref[:, sl] = dqh.astype(_MM)
            dqng = dqng + _colsum8(gq)
            kh = kraw[:, sl] + krv
            dkn = _rope_t(dk_ref[:, sl], c, s1, s2)
            dkh, gk = _rms_bwd(dkn, kh, _rstd(kh, QK_DIM), kng_ref[...], QK_DIM)
            dkraw_ref[:, sl] = dkh.astype(_MM)
            dkr = dkr + dkh
            dkng = dkng + _colsum8(gk)
        dkr_ref[...] = dkr.astype(_MM)
        dcqn = _mm_nt(dqraw_ref[...], wuq_ref[...])
        dcq, gqa = _rms_bwd(dcqn, cqv, rq_in, qag_ref[...])
        dcq_ref[...] = dcq.astype(_MM)
        dckvn = _mm_nt(dkraw_ref[...], wk_ref[...]) + _mm_nt(dv_ref[...], wv_ref[...])
        dckv, gkva = _rms_bwd(dckvn, ckvv, rkv_in, kvag_ref[...])
        dckv_ref[...] = dckv.astype(_MM)

        @pl.when(i == 0)
        def _():
            dqag_ref[...] = jnp.zeros_like(dqag_ref)
            dkvag_ref[...] = jnp.zeros_like(dkvag_ref)
            dqng_ref[...] = jnp.zeros_like(dqng_ref)
            dkng_ref[...] = jnp.zeros_like(dkng_ref)
        dqag_ref[...] += _colsum8(gqa)
        dkvag_ref[...] += _colsum8(gkva)
        dqng_ref[...] += dqng
        dkng_ref[...] += dkng

    return pl.pallas_call(
        body, name="mla_pre_bwd", grid=(t // tm,),
        in_specs=[_row(tm, hd), _row(tm, hd), _row(tm, hd), _row(tm, 256), _row(tm, 128), _row(tm, 128),
                  _full((1, 256)), _full((256, hd)), _full((1, 128)), _full((128, hd)),
                  _full((128, hd)), _full((1, LANE)), _full((1, LANE)),
                  _row(tm, LANE), _row(tm, LANE), _row(tm, LANE)],
        out_specs=[_row(tm, 256), _row(tm, 128), _row(tm, 128), _row(tm, hd), _row(tm, hd),
                   _full((8, 256)), _full((8, 128)), _full((8, LANE)), _full((8, LANE))],
        out_shape=[jax.ShapeDtypeStruct((t, 256), _MM), jax.ShapeDtypeStruct((t, 128), _MM),
                   jax.ShapeDtypeStruct((t, 128), _MM), jax.ShapeDtypeStruct((t, hd), _MM),
                   jax.ShapeDtypeStruct((t, hd), _MM), jax.ShapeDtypeStruct((8, 256), F32),
                   jax.ShapeDtypeStruct((8, 128), F32), jax.ShapeDtypeStruct((8, LANE), F32),
                   jax.ShapeDtypeStruct((8, LANE), F32)],
        compiler_params=_cp("arbitrary"),
    )(dq, dk, dv, cq, ckv, kr, qag, wuq, kvag, wk, wv, qng, kng, rc, rs1, rs2)


ATT_TILE = 704


def _attn_mask(r0, c0, tq):
    rows = r0 + lax.broadcasted_iota(jnp.int32, (tq, 1), 0)
    cols = c0 + lax.broadcasted_iota(jnp.int32, (1, tq), 1)
    return (cols <= rows) & (cols >= FRONT)


def _attn_fwd(q, k, v, plan=None):
    t = q.shape[0]
    tq = _tile(t, ATT_TILE)
    nq = t // tq
    p_args, p_in, p_out, p_shape, p_sem = _plan_specs(plan)

    def body(*refs):
        ((q_ref, k_ref, v_ref), (o_ref, lse_ref), _), rider = _host_refs(refs, 3, 2, 0, plan)
        done = _ride(plan, rider, pl.program_id(0), NH - 1)

        def qloop(qi, carry):
            r0 = pl.multiple_of(qi * tq, tq)
            qb = q_ref[pl.ds(r0, tq), :]

            def kstep(kj, st, masked):
                m, l, acc = st
                c0 = pl.multiple_of(kj * tq, tq)
                s = _mm_nt(qb, k_ref[pl.ds(c0, tq), :])
                if masked:
                    s = jnp.where(_attn_mask(r0, c0, tq), s, NEG)
                m2 = jnp.maximum(m, jnp.max(s, axis=-1, keepdims=True))
                p = jnp.exp(s - m2)
                a = jnp.exp(m - m2)
                l = a * l + jnp.sum(p, axis=-1, keepdims=True)
                acc = a * acc + _mm(p, v_ref[pl.ds(c0, tq), :])
                return m2, l, acc

            st = kstep(0, (jnp.full((tq, 1), NEG, F32), jnp.zeros((tq, 1), F32), jnp.zeros((tq, LANE), F32)), True)
            st = lax.fori_loop(1, qi, lambda kj, s_: kstep(kj, s_, False), st)
            m, l, acc = lax.cond(qi > 0, lambda s_: kstep(qi, s_, True), lambda s_: s_, st)
            o_ref[pl.ds(r0, tq), :] = acc / l
            lse_ref[pl.ds(r0, tq), :] = m + jnp.log(l)
            return carry
        lax.fori_loop(0, nq, qloop, 0)
        done()

    hs = pl.BlockSpec((t, LANE), lambda h: (0, h))
    res = pl.pallas_call(
        body, name="attn_fwd", grid=(NH,),
        in_specs=[hs, hs, hs] + p_in,
        out_specs=[hs, pl.BlockSpec((None, t, 1), lambda h: (h, 0, 0))] + p_out,
        out_shape=[jax.ShapeDtypeStruct((t, NH * LANE), F32), jax.ShapeDtypeStruct((NH, t, 1), F32)] + p_shape,
        scratch_shapes=p_sem,
        compiler_params=_cp("parallel" if plan is None else "arbitrary"),
    )(q, k, v, *p_args)
    return res[:2], res[2:]


def _attn_bwd(q, k, v, o, lse, do, plan=None):
    t = q.shape[0]
    tq = _tile(t, ATT_TILE)
    nq = t // tq
    p_args, p_in, p_out, p_shape, p_sem = _plan_specs(plan)

    def body(*refs):
        (ins, (dq_ref, dk_ref, dv_ref), (delta,)), rider = _host_refs(refs, 6, 3, 1, plan)
        q_ref, k_ref, v_ref, o_ref, lse_ref, do_ref = ins
        done = _ride(plan, rider, pl.program_id(0), NH - 1)

        def prep(i, c):
            r0 = pl.multiple_of(i * tq, tq)
            delta[pl.ds(r0, tq), :] = jnp.sum(do_ref[pl.ds(r0, tq), :] * o_ref[pl.ds(r0, tq), :], axis=-1,
                                              keepdims=True)
            dq_ref[pl.ds(r0, tq), :] = jnp.zeros((tq, LANE), F32)
            return c
        lax.fori_loop(0, nq, prep, 0)

        def kloop(kj, carry):
            c0 = pl.multiple_of(kj * tq, tq)
            kb = k_ref[pl.ds(c0, tq), :]
            vb = v_ref[pl.ds(c0, tq), :]

            def qstep(qi, st, masked):
                dkb, dvb = st
                r0 = pl.multiple_of(qi * tq, tq)
                qb = q_ref[pl.ds(r0, tq), :]
                dob = do_ref[pl.ds(r0, tq), :].astype(_MM)
                s = _mm_nt(qb, kb)
                if masked:
                    s = jnp.where(_attn_mask(r0, c0, tq), s, NEG)
                p = jnp.exp(s - lse_ref[pl.ds(r0, tq), :])
                dvb = dvb + _mm_tn(p, dob)
                dp = _mm_nt(dob, vb)
                ds = (p * (dp - delta[pl.ds(r0, tq), :])).astype(_MM)
                dkb = dkb + _mm_tn(ds, qb)
                dq_ref[pl.ds(r0, tq), :] += _mm(ds, kb)
                return dkb, dvb

            st = qstep(kj, (jnp.zeros((tq, LANE), F32), jnp.zeros((tq, LANE), F32)), True)
            dkb, dvb = lax.cond(
                kj == 0,
                lambda s_: lax.fori_loop(kj + 1, nq, lambda qi, t_: qstep(qi, t_, True), s_),
                lambda s_: lax.fori_loop(kj + 1, nq, lambda qi, t_: qstep(qi, t_, False), s_), st)
            dk_ref[pl.ds(c0, tq), :] = dkb
            dv_ref[pl.ds(c0, tq), :] = dvb
            return carry
        lax.fori_loop(0, nq, kloop, 0)
        done()

    hs = pl.BlockSpec((t, LANE), lambda h: (0, h))
    res = pl.pallas_call(
        body, name="attn_bwd", grid=(NH,),
        in_specs=[hs, hs, hs, hs, pl.BlockSpec((None, t, 1), lambda h: (h, 0, 0)), hs] + p_in,
        out_specs=[hs, hs, hs] + p_out,
        out_shape=[jax.ShapeDtypeStruct((t, NH * LANE), F32)] * 3 + p_shape,
        scratch_shapes=[pltpu.VMEM((t, 1), F32)] + p_sem,
        compiler_params=_cp("parallel" if plan is None else "arbitrary"),
    )(q, k, v, o, lse, do, *p_args)
    return res[:3], res[3:]


def _cumsum_rows(x):
    n = x.shape[0]
    rows = lax.broadcasted_iota(jnp.int32, (n, 1), 0)
    d = 1
    while d < n:
        x = x + jnp.where(rows >= d, pltpu.roll(x, d, 0), 0.0)
        d *= 2
    return x


def _revcumsum_rows(x):
    n = x.shape[0]
    rows = lax.broadcasted_iota(jnp.int32, (n, 1), 0)
    d = 1
    while d < n:
        x = x + jnp.where(rows < n - d, pltpu.roll(x, n - d, 0), 0.0)
        d *= 2
    return x


def _hgrn_gates(f, lb):
    sneg = _sigmoid(-f)
    kk = (1.0 - lb) * sneg
    lf = jnp.log1p(-jnp.minimum(kk, GATE_CLAMP))
    return kk, lf, sneg


def _silu(x):
    return x * _sigmoid(x)


def _dsilu(x):
    s = _sigmoid(x)
    return s * (1.0 + x * (1.0 - s))


def _hgrn_intra(q, kk, b):
    parts = []
    for blk in range(CHUNK // SUB):
        lo = blk * SUB
        ref = jnp.zeros((1, LANE), F32) if blk == 0 else b[lo - 1:lo, :]
        eq = jnp.exp(b[lo:lo + SUB, :] - ref)
        ek = jnp.exp(jnp.minimum(ref - b, EXP_CLIP))
        parts.append((q[lo:lo + SUB, :] * eq, kk * ek, eq, ek))
    return parts


def _chunk_causal():
    return lax.broadcasted_iota(jnp.int32, (CHUNK, CHUNK), 1) <= lax.broadcasted_iota(jnp.int32, (CHUNK, CHUNK), 0)


def _hgrn_fwd(h4, lb):
    t = h4.shape[0]
    nc = t // CHUNK

    def body(q_ref, f_ref, i_ref, lb_ref, o_ref, s_ref, st):
        st[...] = jnp.zeros_like(st)
        causal = _chunk_causal()

        def chunk(c, carry):
            r0 = pl.multiple_of(c * CHUNK, CHUNK)
            q = q_ref[pl.ds(r0, CHUNK), :]
            kk, lf, _ = _hgrn_gates(f_ref[pl.ds(r0, CHUNK), :], lb_ref[...])
            v = _silu(i_ref[pl.ds(r0, CHUNK), :])
            b = _cumsum_rows(lf)
            s_prev = st[...]
            s_ref[c] = s_prev
            o = _hmm_nt(q * jnp.exp(b), s_prev)
            a = jnp.concatenate([_hmm_nt(qs, ks) for qs, ks, _, _ in _hgrn_intra(q, kk, b)], axis=0)
            a = jnp.where(causal, a, 0.0)
            o_ref[pl.ds(r0, CHUNK), :] = o + _hmm(a, v)
            bl = b[CHUNK - 1:CHUNK, :]
            st[...] = s_prev * jnp.exp(bl) + _hmm_tn(v, kk * jnp.exp(bl - b))
            return carry
        lax.fori_loop(0, nc, chunk, 0, unroll=2)

    def col(j):
        return pl.BlockSpec((t, LANE), lambda h: (0, HH * j + h))
    return pl.pallas_call(
        body, name="hgrn_fwd", grid=(HH,),
        in_specs=[col(0), col(1), col(2), pl.BlockSpec((1, LANE), lambda h: (0, h))],
        out_specs=[pl.BlockSpec((t, LANE), lambda h: (0, h)),
                   pl.BlockSpec((None, nc, LANE, LANE), lambda h: (h, 0, 0, 0))],
        out_shape=[jax.ShapeDtypeStruct((t, HH * LANE), F32), jax.ShapeDtypeStruct((HH, nc, LANE, LANE), F32)],
        scratch_shapes=[pltpu.VMEM((LANE, LANE), F32)],
        compiler_params=_cp("parallel"),
    )(h4, h4, h4, lb)


def _hgrn_bwd(h4, lb, do, states, plan=None):
    t = h4.shape[0]
    nc = t // CHUNK
    p_args, p_in, p_out, p_shape, p_sem = _plan_specs(plan)

    def body(*refs):
        (ins, outs, (dst, carry)), rider = _host_refs(refs, 6, 4, 2, plan)
        q_ref, f_ref, i_ref, lb_ref, do_ref, s_ref = ins
        dq_ref, df_ref, di_ref, dlb_ref = outs
        done = _ride(plan, rider, pl.program_id(0), HH - 1)
        dst[...] = jnp.zeros_like(dst)
        carry[...] = jnp.zeros_like(carry)
        dlb_ref[...] = jnp.zeros_like(dlb_ref)
        causal = _chunk_causal()

        def chunk(cc, cr):
            c = nc - 1 - cc
            r0 = pl.multiple_of(c * CHUNK, CHUNK)
            q = q_ref[pl.ds(r0, CHUNK), :]
            lbv = lb_ref[...]
            kk, lf, sneg = _hgrn_gates(f_ref[pl.ds(r0, CHUNK), :], lbv)
            iv = i_ref[pl.ds(r0, CHUNK), :]
            v = _silu(iv)
            b = _cumsum_rows(lf)
            s_prev = s_ref[c]
            ds_new = dst[...]
            dob = do_ref[pl.ds(r0, CHUNK), :]
            e = jnp.exp(b)
            qe = q * e
            bl = b[CHUNK - 1:CHUNK, :]
            etail = jnp.exp(bl - b)
            kd = kk * etail
            dq_inter = _hmm(dob, s_prev) * e
            dv = _hmm_nt(kd, ds_new)
            dkk = _hmm(v, ds_new) * etail
            parts = _hgrn_intra(q, kk, b)
            a = jnp.where(causal, jnp.concatenate([_hmm_nt(qs, ks) for qs, ks, _, _ in parts], axis=0), 0.0)
            da = jnp.where(causal, _hmm_nt(dob, v), 0.0)
            dv = dv + _hmm_tn(a, dob)
            dq_rows = []
            for blk, (qs, ks, eq, ek) in enumerate(parts):
                da_blk = da[blk * SUB:(blk + 1) * SUB, :]
                dq_rows.append(_hmm(da_blk, ks) * eq)
                dkk = dkk + _hmm_tn(da_blk, qs) * ek
            dq = dq_inter + jnp.concatenate(dq_rows, axis=0)
            dst[...] = ds_new * jnp.exp(bl) + _hmm_tn(dob, qe)
            g = q * dq - kk * dkk
            dlf = _revcumsum_rows(g) + carry[0:1, :]
            carry[0:1, :] += jnp.sum(g, axis=0, keepdims=True)
            dkk_tot = dkk + dlf * jnp.where(kk < GATE_CLAMP, -1.0 / (1.0 - kk), 0.0)
            dq_ref[pl.ds(r0, CHUNK), :] = dq
            df_ref[pl.ds(r0, CHUNK), :] = dkk_tot * (1.0 - lbv) * (-sneg * (1.0 - sneg))
            di_ref[pl.ds(r0, CHUNK), :] = dv * _dsilu(iv)
            dlb_ref[...] += _colsum8(dkk_tot * (-sneg))
            return cr
        lax.fori_loop(0, nc, chunk, 0, unroll=2)
        done()

    def col(j):
        return pl.BlockSpec((t, LANE), lambda h: (0, HH * j + h))
    hs = pl.BlockSpec((t, LANE), lambda h: (0, h))
    res = pl.pallas_call(
        body, name="hgrn_bwd", grid=(HH,),
        in_specs=[col(0), col(1), col(2), pl.BlockSpec((1, LANE), lambda h: (0, h)), hs,
                  pl.BlockSpec((None, nc, LANE, LANE), lambda h: (h, 0, 0, 0))] + p_in,
        out_specs=[hs, hs, hs, pl.BlockSpec((8, LANE), lambda h: (0, h))] + p_out,
        out_shape=[jax.ShapeDtypeStruct((t, HH * LANE), F32)] * 3 + [jax.ShapeDtypeStruct((8, HH * LANE), F32)]
        + p_shape,
        scratch_shapes=[pltpu.VMEM((LANE, LANE), F32), pltpu.VMEM((8, LANE), F32)] + p_sem,
        compiler_params=_cp("parallel" if plan is None else "arbitrary"),
    )(h4, h4, h4, lb, do, states, *p_args)
    return res[:4], res[4:]


def _ln_fwd(z, g, b):
    mu = jnp.mean(z, axis=-1, keepdims=True)
    zc = z - mu
    rstd = lax.rsqrt(jnp.mean(zc * zc, axis=-1, keepdims=True) + EPS)
    zh = zc * rstd
    return zh * g + b, zh, rstd


def _mix_fwd(x, z, o_att, o_h, h4, gates, lng, lnb, wco, wao, ng, who, wout, t_end):
    t = x.shape[0]
    tm = _tile(t, 384)

    def body(x_ref, z_ref, oa_ref, oh_ref, hg_ref, gt_ref, lng_ref, lnb_ref, wco_ref, wao_ref, ng_ref, who_ref,
             wout_ref, x1_ref, mix_ref, ca_ref, oc_ref, ya_ref, yb_ref, yc_ref):
        i = pl.program_id(0)
        ln, _, _ = _ln_fwd(z_ref[...], lng_ref[...], lnb_ref[...])
        ca = _silu(ln).astype(_MM)
        ca_ref[...] = ca
        ya = jnp.dot(ca, wco_ref[...], preferred_element_type=F32)
        yb = _mm(oa_ref[...], wao_ref[...])
        hg = hg_ref[...]
        for h in range(HH):
            sl = slice(LANE * h, LANE * (h + 1))
            oh = oh_ref[:, sl]
            oc_ref[:, sl] = (oh * _rstd(oh) * ng_ref[:, sl] * _silu(hg[:, sl])).astype(_MM)
        yc = jnp.dot(oc_ref[...], who_ref[...], preferred_element_type=F32)
        ya_ref[...] = ya
        yb_ref[...] = yb
        yc_ref[...] = yc
        mix = (_sigmoid(gt_ref[:, 0:D]) * ya + _sigmoid(gt_ref[:, D:2 * D]) * yb
               + _sigmoid(gt_ref[:, 2 * D:3 * D]) * yc).astype(_MM)
        mix_ref[...] = mix
        x1_ref[...] = x_ref[...] + _valid_rows(i, tm, t_end) * jnp.dot(mix, wout_ref[...],
                                                                       preferred_element_type=F32)

    hd = NH * LANE
    return pl.pallas_call(
        body, name="mix_fwd", grid=(t // tm,),
        in_specs=[_row(tm, D), _row(tm, CONV_DIM), _row(tm, hd), _row(tm, 512), _row(tm, 512, 3), _row(tm, 3 * D),
                  _full((1, 512)), _full((1, 512)), _full((512, D)), _full((hd, D)), _full((1, 512)),
                  _full((512, D)), _full((D, D))],
        out_specs=[_row(tm, D), _row(tm, D), _row(tm, 512), _row(tm, 512), _row(tm, D), _row(tm, D), _row(tm, D)],
        out_shape=[jax.ShapeDtypeStruct((t, D), F32), jax.ShapeDtypeStruct((t, D), _MM),
                   jax.ShapeDtypeStruct((t, 512), _MM), jax.ShapeDtypeStruct((t, 512), _MM),
                   jax.ShapeDtypeStruct((t, D), F32), jax.ShapeDtypeStruct((t, D), F32),
                   jax.ShapeDtypeStruct((t, D), F32)],
        compiler_params=_cp("parallel"),
    )(x, z, o_att, o_h, h4, gates, lng, lnb, wco, wao, ng, who, wout)


def _mix_bwd(dx1, ya, yb, yc, gates, z, o_h, h4, lng, lnb, ng, wout, wco, wao, who, plan=None):
    t = dx1.shape[0]
    tm = _tile(t, 384)
    hd = NH * LANE
    p_args, p_in, p_out, p_shape, p_sem = _plan_specs(plan)

    def body(*refs):
        (ins, outs, _), rider = _host_refs(refs, 15, 12, 0, plan)
        (dx1_ref, ya_ref, yb_ref, yc_ref, gt_ref, z_ref, oh_ref, hg_ref, lng_ref, lnb_ref, ng_ref,
         wout_ref, wco_ref, wao_ref, who_ref) = ins
        (dgt_ref, dya_ref, dyb_ref, dyc_ref, dz_ref, doa_ref, doh_ref, dhg_ref,
         dlng_ref, dlnb_ref, dcb_ref, dng_ref) = outs
        i = pl.program_id(0)
        done = _ride(plan, rider, i, t // tm - 1)
        dmix = _mm_nt(dx1_ref[...], wout_ref[...])
        dys = []
        for j, y_ref in enumerate((ya_ref, yb_ref, yc_ref)):
            sg = _sigmoid(gt_ref[:, j * D:(j + 1) * D])
            dgt_ref[:, j * D:(j + 1) * D] = (dmix * y_ref[...] * sg * (1.0 - sg)).astype(_MM)
            dys.append((dmix * sg).astype(_MM))
        dya_ref[...], dyb_ref[...], dyc_ref[...] = dys
        dca = _mm_nt(dys[0], wco_ref[...])
        ln, zh, rstd = _ln_fwd(z_ref[...], lng_ref[...], lnb_ref[...])
        dln = dca * _dsilu(ln)
        dzh = dln * lng_ref[...]
        dz = rstd * (dzh - jnp.mean(dzh, axis=-1, keepdims=True)
                     - zh * jnp.mean(dzh * zh, axis=-1, keepdims=True))
        dz_ref[...] = dz
        doa_ref[...] = _mm_nt(dys[1], wao_ref[...])
        doc = _mm_nt(dys[2], who_ref[...])
        hg = hg_ref[...]
        dng_rows = []
        for h in range(HH):
            sl = slice(LANE * h, LANE * (h + 1))
            oh = oh_ref[:, sl]
            r = _rstd(oh)
            don = doc[:, sl] * _silu(hg[:, sl])
            dhg_ref[:, sl] = (doc[:, sl] * oh * r * ng_ref[:, sl] * _dsilu(hg[:, sl])).astype(_MM)
            doh, gn = _rms_bwd(don, oh, r, ng_ref[:, sl])
            doh_ref[:, sl] = doh
            dng_rows.append(_colsum8(gn))

        @pl.when(i == 0)
        def _():
            dlng_ref[...] = jnp.zeros_like(dlng_ref)
            dlnb_ref[...] = jnp.zeros_like(dlnb_ref)
            dcb_ref[...] = jnp.zeros_like(dcb_ref)
            dng_ref[...] = jnp.zeros_like(dng_ref)
        dlng_ref[...] += _colsum8(dln * zh)
        dlnb_ref[...] += _colsum8(dln)
        dcb_ref[...] += _colsum8(dz)
        dng_ref[...] += jnp.concatenate(dng_rows, axis=1)
        done()

    res = pl.pallas_call(
        body, name="mix_bwd", grid=(t // tm,),
        in_specs=[_row(tm, D), _row(tm, D), _row(tm, D), _row(tm, D), _row(tm, 3 * D), _row(tm, 512), _row(tm, 512),
                  _row(tm, 512, 3), _full((1, 512)), _full((1, 512)), _full((1, 512)),
                  _full((D, D)), _full((512, D)), _full((hd, D)), _full((512, D))] + p_in,
        out_specs=[_row(tm, 3 * D), _row(tm, D), _row(tm, D), _row(tm, D), _row(tm, 512), _row(tm, hd),
                   _row(tm, 512), _row(tm, 512), _full((8, 512)), _full((8, 512)), _full((8, 512)),
                   _full((8, 512))] + p_out,
        out_shape=[jax.ShapeDtypeStruct((t, 3 * D), _MM), jax.ShapeDtypeStruct((t, D), _MM),
                   jax.ShapeDtypeStruct((t, D), _MM), jax.ShapeDtypeStruct((t, D), _MM),
                   jax.ShapeDtypeStruct((t, 512), F32), jax.ShapeDtypeStruct((t, hd), F32),
                   jax.ShapeDtypeStruct((t, 512), F32), jax.ShapeDtypeStruct((t, 512), _MM)]
        + [jax.ShapeDtypeStruct((8, 512), F32)] * 4 + p_shape,
        scratch_shapes=p_sem,
        compiler_params=_cp("arbitrary"),
    )(dx1, ya, yb, yc, gates, z, o_h, h4, lng, lnb, ng, wout, wco, wao, who, *p_args)
    return res[:12], res[12:]


D_FF = 4096


def _ffn_fwd(x1, g2, w1, w2):
    t = x1.shape[0]
    tm = _tile(t, 192)

    def body(x1_ref, g_ref, w1_ref, w2_ref, x2_ref, p_ref):
        xv = x1_ref[...]
        h2 = (xv * _rstd(xv) * g_ref[...]).astype(_MM)
        p = jnp.dot(h2, w1_ref[...], preferred_element_type=F32)
        p_ref[...] = p
        r = jnp.maximum(p, 0.0)
        x2_ref[...] = xv + jnp.dot((r * r).astype(_MM), w2_ref[...], preferred_element_type=F32)

    return pl.pallas_call(
        body, name="ffn_fwd", grid=(t // tm,),
        in_specs=[_row(tm, D), _full((1, D)), _full((D, D_FF)), _full((D_FF, D))],
        out_specs=[_row(tm, D), _row(tm, D_FF)],
        out_shape=[jax.ShapeDtypeStruct((t, D), F32), jax.ShapeDtypeStruct((t, D_FF), F32)],
        compiler_params=_cp("parallel"),
    )(x1, g2, w1, w2)


def _ffn_bwd(dx2, x1, p, g2, w1t, w2t, plan=None):
    t = x1.shape[0]
    tm = _tile(t, 192)
    p_args, p_in, p_out, p_shape, p_sem = _plan_specs(plan)

    def body(*refs):
        (ins, outs, _), rider = _host_refs(refs, 6, 5, 0, plan)
        dx2_ref, x1_ref, p_ref, g_ref, w1t_ref, w2t_ref = ins
        dx1_ref, h2_ref, act_ref, dp_ref, dg_ref = outs
        i = pl.program_id(0)
        done = _ride(plan, rider, i, t // tm - 1)
        xv = x1_ref[...]
        rstd = _rstd(xv)
        h2_ref[...] = (xv * rstd * g_ref[...]).astype(_MM)
        r = jnp.maximum(p_ref[...], 0.0)
        act_ref[...] = (r * r).astype(_MM)
        dx2 = dx2_ref[...]
        da = _mm(dx2, w2t_ref[...])
        dp = (2.0 * r * da).astype(_MM)
        dp_ref[...] = dp
        dh2 = jnp.dot(dp, w1t_ref[...], preferred_element_type=F32)
        dxn, dgrow = _rms_bwd(dh2, xv, rstd, g_ref[...])
        dx1_ref[...] = dx2 + dxn

        @pl.when(i == 0)
        def _():
            dg_ref[...] = jnp.zeros_like(dg_ref)
        dg_ref[...] += _colsum8(dgrow)
        done()

    res = pl.pallas_call(
        body, name="ffn_bwd", grid=(t // tm,),
        in_specs=[_row(tm, D), _row(tm, D), _row(tm, D_FF), _full((1, D)), _full((D_FF, D)),
                  _full((D, D_FF))] + p_in,
        out_specs=[_row(tm, D), _row(tm, D), _row(tm, D_FF), _row(tm, D_FF), _full((8, D))] + p_out,
        out_shape=[jax.ShapeDtypeStruct((t, D), F32), jax.ShapeDtypeStruct((t, D), _MM),
                   jax.ShapeDtypeStruct((t, D_FF), _MM), jax.ShapeDtypeStruct((t, D_FF), _MM),
                   jax.ShapeDtypeStruct((8, D), F32)] + p_shape,
        scratch_shapes=p_sem,
        compiler_params=_cp("arbitrary"),
    )(dx2, x1, p, g2, w1t, w2t, *p_args)
    return res[:5], res[5:]


WGRAD_VMEM = 40 * 1024 * 1024


def _wgrad(a, b, name, chips=1):
    t, ka = a.shape
    nb = b.shape[1]
    cs = nb // chips
    widths = [d for d in range(cs, 0, -LANE) if cs % d == 0 and d % LANE == 0] or [cs]
    tn, tm = widths[-1], 64
    for d in widths:
        room = WGRAD_VMEM - 2 * ka * d * 4
        row_bytes = 2 * (ka * a.dtype.itemsize + d * b.dtype.itemsize) + 4 * ka
        fit = [r for r in range(64, t + 1, 64) if t % r == 0 and r * row_bytes <= room]
        if ka * d * 4 <= 16 * 1024 * 1024 and fit and (max(fit) >= 384 or d == widths[-1]):
            tn, tm = d, max(fit)
            break
    per = cs // tn

    def body(a_ref, b_ref, o_ref):
        @pl.when(pl.program_id(1) == 0)
        def _():
            o_ref[...] = jnp.zeros_like(o_ref)
        o_ref[...] += _mm_tn(a_ref[...], b_ref[...])

    if chips == 1:
        out_spec = pl.BlockSpec((ka, tn), lambda n, i: (0, n))
        out_shape = jax.ShapeDtypeStruct((ka, nb), F32)
    else:
        out_spec = pl.BlockSpec((None, ka, tn), lambda n, i: (n // per, 0, n % per))
        out_shape = jax.ShapeDtypeStruct((chips, ka, cs), F32)
    return pl.pallas_call(
        body, name="wgrad_" + name, grid=(nb // tn, t // tm),
        in_specs=[pl.BlockSpec((tm, ka), lambda n, i: (i, 0)), pl.BlockSpec((tm, tn), lambda n, i: (i, n))],
        out_specs=out_spec, out_shape=out_shape,
        compiler_params=_cp("parallel", "arbitrary"),
    )(a, b)


def _loss_head(y, target, t_end):
    t = y.shape[0]
    tm = _tile(t, 384)

    def body(y_ref, tg_ref, dy_ref, l_ref):
        i = pl.program_id(0)
        r = i * tm + lax.broadcasted_iota(jnp.int32, (tm, 1), 0)
        real = ((r >= ROW0) & (r < t_end)).astype(F32)
        diff = (y_ref[...] - tg_ref[...]) * real
        dy_ref[...] = diff * (1.0 / D)

        @pl.when(i == 0)
        def _():
            l_ref[...] = jnp.zeros_like(l_ref)
        sq = _colsum8(diff * diff)
        part = sq[:, 0:LANE]
        for j in range(1, D // LANE):
            part = part + sq[:, j * LANE:(j + 1) * LANE]
        l_ref[...] += part * (0.5 / D)

    return pl.pallas_call(
        body, name="loss_head", grid=(t // tm,),
        in_specs=[_row(tm, D), _row(tm, D)],
        out_specs=[_row(tm, D), _full((8, LANE))],
        out_shape=[jax.ShapeDtypeStruct((t, D), F32), jax.ShapeDtypeStruct((8, LANE), F32)],
        compiler_params=_cp("arbitrary"),
    )(y, target)


def _lower_bounds_fwd(logits):
    depth, n = logits.shape

    def body(l_ref, lb_ref):
        lg = l_ref[...]
        m = jnp.max(lg, axis=0, keepdims=True)
        e = jnp.exp(lg - m)
        p = e / jnp.sum(e, axis=0, keepdims=True)
        acc = jnp.zeros((1, n), F32)
        for l in range(depth):
            if l > 0:
                acc = acc + p[l:l + 1, :]
            lb_ref[l:l + 1, :] = acc

    return pl.pallas_call(body, name="lower_bounds_fwd", out_shape=jax.ShapeDtypeStruct((depth, n), F32))(logits)


def _lower_bounds_bwd(logits, dlb):
    depth, n = logits.shape

    def body(l_ref, dlb_ref, dl_ref):
        lg = l_ref[...]
        m = jnp.max(lg, axis=0, keepdims=True)
        e = jnp.exp(lg - m)
        p = e / jnp.sum(e, axis=0, keepdims=True)
        dps = [jnp.zeros((1, n), F32)]
        for j in range(1, depth):
            acc = jnp.zeros((1, n), F32)
            for l in range(j, depth):
                acc = acc + dlb_ref[l:l + 1, :]
            dps.append(acc)
        dot = jnp.zeros((1, n), F32)
        for j in range(depth):
            dot = dot + p[j:j + 1, :] * dps[j]
        for j in range(depth):
            dl_ref[j:j + 1, :] = p[j:j + 1, :] * (dps[j] - dot)

    return pl.pallas_call(body, name="lower_bounds_bwd", out_shape=jax.ShapeDtypeStruct((depth, n), F32))(logits, dlb)


def _ew_tile(rows, cols, n_arrays):
    cap = max(16, (32 * 1024 * 1024) // (8 * n_arrays * cols))
    for mult in (16, 8):
        fit = [t for t in range(mult, rows + 1, mult) if rows % t == 0 and t <= cap]
        if fit:
            return max(fit)
    return rows


def _adamw_math(w, g, m, v):
    mn = ADAM_B1 * m + (1.0 - ADAM_B1) * g
    vn = ADAM_B2 * v + (1.0 - ADAM_B2) * (g * g)
    m_hat = mn / (1.0 - ADAM_B1 ** ADAM_STEP)
    v_hat = vn / (1.0 - ADAM_B2 ** ADAM_STEP)
    return -ADAM_LR * (m_hat / (jnp.sqrt(v_hat) + ADAM_EPS) + ADAM_WD * w), mn, vn


def _adamw_layers(w, m, v, g0, g1, g_sibling, name):
    _, r, c_ = w.shape
    tr = _ew_tile(r, c_, 10)

    def body(w_ref, m_ref, v_ref, g0_ref, g1_ref, gs_ref, g_ref, d_ref, mo_ref, vo_ref):
        layer = pl.program_id(0)
        own = jnp.where(layer == 0, g0_ref[...], g1_ref[...])
        g = jnp.where(layer == lax.axis_index("c"), own, gs_ref[...])
        g_ref[...] = g
        d_ref[...], mo_ref[...], vo_ref[...] = _adamw_math(w_ref[...], g, m_ref[...], v_ref[...])

    lay = pl.BlockSpec((None, tr, c_), lambda l, i: (l, i, 0))
    flat = pl.BlockSpec((tr, c_), lambda l, i: (i, 0))
    only0 = pl.BlockSpec((tr, c_), lambda l, i: (i * (1 - l), 0))
    only1 = pl.BlockSpec((tr, c_), lambda l, i: (i * l, 0))
    return pl.pallas_call(
        body, name="adamw_" + name, grid=(2, r // tr),
        in_specs=[lay, lay, lay, only0, only1, flat], out_specs=[lay] * 4,
        out_shape=[jax.ShapeDtypeStruct(w.shape, F32)] * 4,
        compiler_params=_cp("parallel", "parallel"),
    )(w, m, v, g0, g1, g_sibling)


def _adamw(w, g, m, v, name):
    rows, cols = w.shape
    tr = _ew_tile(rows, cols, 7)

    def body(w_ref, g_ref, m_ref, v_ref, d_ref, mo_ref, vo_ref):
        d_ref[...], mo_ref[...], vo_ref[...] = _adamw_math(w_ref[...], g_ref[...], m_ref[...], v_ref[...])

    spec = pl.BlockSpec((tr, cols), lambda i: (i, 0))
    return pl.pallas_call(
        body, name="adamw_" + name, grid=(rows // tr,),
        in_specs=[spec] * 4, out_specs=[spec] * 3,
        out_shape=[jax.ShapeDtypeStruct((rows, cols), F32)] * 3,
        compiler_params=_cp("parallel"),
    )(w, g, m, v)


DEPTH = 2
BIG_SHAPES = {"w_in": ((1024, 6560), 1), "w_conv_out": ((512, 1024), 1), "w_uq": ((256, 768), 1),
              "w_ukv": ((128, 1024), 1), "w_attn_out": ((512, 1024), 1), "w_hgrn_out": ((512, 1024), 1),
              "w_out": ((1024, 1024), 0), "w_ff1": ((1024, 4096), 1), "w_ff2": ((4096, 1024), 0)}
BIG = tuple(BIG_SHAPES)
SMALL_SIZES = {"norm1_g": 1024, "conv_b": 512, "conv_ln_g": 512, "conv_ln_b": 512, "q_a_norm_g": 256,
               "kv_a_norm_g": 128, "q_norm_g": 96, "k_norm_g": 96, "hgrn_lb_logits": 512, "hgrn_norm_g": 512,
               "norm2_g": 1024}
SMALL = tuple(SMALL_SIZES)
W_IN_COLS = 6560
W_IN_SHARD = W_IN_COLS // 4
W_IN_SEGS = ((0, 1024, SEG_AG[0]), (1024, 1280, SEG_CQ[0]), (1280, 1408, SEG_CKV[0]), (1408, 1440, SEG_KR[0] + 64),
             (1440, 3488, SEG_H4[0]), (3488, 6560, SEG_GATES[0]))


def _pad_heads(w, nh, used, axis):
    shp = w.shape
    w = w.reshape(shp[:axis] + (nh, used) + shp[axis + 1:])
    pad = [(0, 0)] * w.ndim
    pad[axis + 1] = (0, LANE - used)
    w = jnp.pad(w, pad)
    return w.reshape(shp[:axis] + (nh * LANE,) + shp[axis + 1:])


def _unpad_heads(w, nh, used, axis):
    shp = w.shape
    w = w.reshape(shp[:axis] + (nh, LANE) + shp[axis + 1:])
    w = lax.slice_in_dim(w, 0, used, axis=axis + 1)
    return w.reshape(shp[:axis] + (nh * used,) + shp[axis + 1:])


def _w_in_from_chips(p4):
    def orig(a, b):
        out = []
        while a < b:
            s = a // W_IN_SHARD
            e = min(b, (s + 1) * W_IN_SHARD)
            out.append(p4[s][:, a - W_IN_SHARD * s:e - W_IN_SHARD * s])
            a = e
        return out
    zc = lambda n: jnp.zeros((D, n), p4[0].dtype)
    parts = (orig(3488, 6560) + orig(0, 1024) + orig(1440, 3488) + orig(1024, 1280) + orig(1280, 1408)
             + [zc(64)] + orig(1408, 1440) + [zc(32)])
    return jnp.concatenate(parts, axis=1)


def _w_in_grad_to_chips(dw):
    chips = []
    for s in range(4):
        a, b = W_IN_SHARD * s, W_IN_SHARD * (s + 1)
        parts = []
        for o0, o1, p0 in W_IN_SEGS:
            lo, hi = max(a, o0), min(b, o1)
            if lo < hi:
                parts.append(dw[:, p0 + lo - o0:p0 + hi - o0])
        chips.append(jnp.concatenate(parts, axis=1))
    return jnp.stack(chips)


def _cat_chips(p4, axis):
    return jnp.concatenate([p4[s] for s in range(4)], axis=axis)


EARLY = ("w_in", "w_uq", "w_ukv")
LATE = tuple(k for k in BIG if k not in EARLY)


def _prep_late(pieces):
    pc = lambda k: [pieces[k][s].astype(_MM) for s in range(4)]
    return dict(wao=_pad_heads(_cat_chips(pc("w_attn_out"), 1), NH, 64, 0), wco=_cat_chips(pc("w_conv_out"), 1),
                who=_cat_chips(pc("w_hgrn_out"), 1), wout=_cat_chips(pc("w_out"), 0),
                w1=_cat_chips(pc("w_ff1"), 1), w2=_cat_chips(pc("w_ff2"), 0))


def _prep_early(pieces, small, l):
    mm = lambda a: a.astype(_MM)
    pc = lambda k: [mm(pieces[k][s]) for s in range(4)]
    w_in_p = _w_in_from_chips(pc("w_in"))
    wuq = jnp.concatenate([_pad_heads(pc("w_uq")[s], 2, QK_DIM, 1) for s in range(4)], axis=1)
    wukv = _cat_chips(pc("w_ukv"), 1).reshape(128, NH, 128)
    wk = _pad_heads(wukv[:, :, :64].reshape(128, NH * 64), NH, 64, 1)
    wv = _pad_heads(wukv[:, :, 64:].reshape(128, NH * 64), NH, 64, 1)
    row = lambda a: a.astype(F32).reshape(1, -1)
    p = dict(
        w_in=w_in_p, w_in_t=w_in_p.T, wuq=wuq, wk=wk, wv=wv,
        g1=row(small["norm1_g"][l]), g2=row(small["norm2_g"][l]),
        cw=jnp.pad(small["conv_w"][l].astype(F32), ((0, 1), (0, 0))), cb=row(small["conv_b"][l]),
        lng=row(small["conv_ln_g"][l]), lnb=row(small["conv_ln_b"][l]),
        qag=row(small["q_a_norm_g"][l]), kvag=row(small["kv_a_norm_g"][l]),
        qng=jnp.pad(row(small["q_norm_g"][l]), ((0, 0), (0, LANE - QK_DIM))),
        kng=jnp.pad(row(small["k_norm_g"][l]), ((0, 0), (0, LANE - QK_DIM))),
        ng=row(small["hgrn_norm_g"][l]),
    )
    return p


def _rope_tables(t):
    pos = (jnp.arange(t, dtype=jnp.int32) - FRONT).astype(F32)
    inv_freq = 10000.0 ** (-jnp.arange(16, dtype=F32) / 16)
    ang = pos[:, None] * inv_freq[None, :]
    cos, sin = jnp.cos(ang), jnp.sin(ang)
    one = jnp.ones((t, 64), F32)
    z16, z32, z64 = jnp.zeros((t, 16), F32), jnp.zeros((t, 32), F32), jnp.zeros((t, 64), F32)
    c = jnp.concatenate([one, cos, cos, z32], axis=1)
    s1 = jnp.concatenate([z64, -sin, z16, z32], axis=1)
    s2 = jnp.concatenate([z64, z16, sin, z32], axis=1)
    return c, s1, s2


def _layer_fwd(x, p, lb, rope, t_end, plan=None, on_rode=None):
    gates, ag, h4, cq, ckv, kr, hb = _in_proj_fwd(x, p["g1"], p["w_in"])
    z = _conv_fwd(ag, p["cw"], p["cb"])
    q, k, v, cqn, ckvn = _mla_pre_fwd(cq, ckv, kr, p["qag"], p["wuq"], p["kvag"], p["wk"], p["wv"], p["qng"],
                                      p["kng"], *rope)
    (o_att, lse), rode = _attn_fwd(q, k, v, plan)
    if on_rode is not None:
        on_rode(rode)
    o_h, states = _hgrn_fwd(h4, lb)
    x1, mix, ca, oc, ya, yb, yc = _mix_fwd(x, z, o_att, o_h, h4, gates, p["lng"], p["lnb"], p["wco"], p["wao"],
                                           p["ng"], p["who"], p["wout"], t_end)
    x2, pre = _ffn_fwd(x1, p["g2"], p["w1"], p["w2"])
    saved = dict(x=x, gates=gates, ag=ag, h4=h4, cq=cq, ckv=ckv, kr=kr, hb=hb, z=z, q=q, k=k, v=v, cqn=cqn,
                 ckvn=ckvn, o_att=o_att, lse=lse, o_h=o_h, states=states, x1=x1, mix=mix, ca=ca, oc=oc,
                 ya=ya, yb=yb, yc=yc, pre=pre)
    return x2, saved


def _layer_bwd(dx2, s, p, lb, rope, t_end, rides=None):
    rides = rides or {}
    (dx1, h2, act, dp, dg2), rode = _ffn_bwd(dx2, s["x1"], s["pre"], p["g2"], p["w1"].T, p["w2"].T,
                                             rides.get("ffn"))
    g = {"w_ff1": _wgrad(h2, dp, "ff1", 4), "w_ff2": _wgrad(act, dx2, "ff2").reshape(4, D_FF // 4, D),
         "norm2_g": dg2.sum(0)}
    plan_mix = rides["mix"](rode, g) if "mix" in rides else None
    (dgt, dya, dyb, dyc, dz, doa, doh, dhg, dlng, dlnb, dcb, dng), rode = _mix_bwd(
        dx1, s["ya"], s["yb"], s["yc"], s["gates"], s["z"], s["o_h"], s["h4"], p["lng"], p["lnb"], p["ng"],
        p["wout"], p["wco"], p["wao"], p["who"], plan_mix)
    g["w_out"] = _wgrad(s["mix"], dx1, "out").reshape(4, D // 4, D)
    g["w_conv_out"] = _wgrad(s["ca"], dya, "conv_out", 4)
    g["w_attn_out"] = _unpad_heads(_wgrad(s["o_att"], dyb, "attn_out", 4), NH, 64, 1)
    g["w_hgrn_out"] = _wgrad(s["oc"], dyc, "hgrn_out", 4)
    g["conv_ln_g"], g["conv_ln_b"], g["conv_b"], g["hgrn_norm_g"] = dlng.sum(0), dlnb.sum(0), dcb.sum(0), dng.sum(0)
    da, dg, dcw = _conv_bwd(s["ag"], p["cw"], dz)
    g["conv_w"] = dcw[:CONV_K]
    plan_attn = rides["attn"](rode, g) if "attn" in rides else None
    (dq, dk, dv), rode_attn = _attn_bwd(s["q"], s["k"], s["v"], s["o_att"], s["lse"], doa, plan_attn)
    dcq, dckv, dkr, dqraw, dkraw, dqag, dkvag, dqng, dkng = _mla_pre_bwd(
        dq, dk, dv, s["cq"], s["ckv"], s["kr"], p["qag"], p["wuq"], p["kvag"], p["wk"], p["wv"], p["qng"],
        p["kng"], *rope)
    g["w_uq"] = _unpad_heads(_wgrad(s["cqn"], dqraw, "uq", 4), 2, QK_DIM, 2)
    dwk = _unpad_heads(_wgrad(s["ckvn"], dkraw, "uk"), NH, 64, 1).reshape(128, NH, 64)
    dwv = _unpad_heads(_wgrad(s["ckvn"], dv, "uv"), NH, 64, 1).reshape(128, NH, 64)
    g["w_ukv"] = jnp.concatenate([dwk, dwv], axis=2).reshape(128, 4, 256).transpose(1, 0, 2)
    g["q_a_norm_g"], g["kv_a_norm_g"] = dqag.sum(0), dkvag.sum(0)
    g["q_norm_g"], g["k_norm_g"] = dqng.sum(0)[:QK_DIM], dkng.sum(0)[:QK_DIM]
    plan_hgrn = rides["hgrn"](rode_attn) if "hgrn" in rides else None
    (dhq, dhf, dhi, dlb), rode_hgrn = _hgrn_bwd(s["h4"], lb, doh, s["states"], plan_hgrn)
    mm = lambda a: a.astype(_MM)
    du = jnp.concatenate([dgt, mm(da), mm(dg), mm(dhq), mm(dhf), mm(dhi), dhg, dcq, dckv, dkr], axis=1)
    dx, dg1 = _in_proj_bwd(du, s["x"], dx1, p["g1"], p["w_in_t"], t_end)
    g["norm1_g"] = dg1.sum(0)
    g["w_in"] = _w_in_grad_to_chips(_wgrad(s["hb"], du, "in"))
    return dx, g, dlb.sum(0), (rode_attn, rode_hgrn)


def _device_step(x, target, small, pieces0, pieces1=None, fwd_ride=None, bwd_rides=None):
    s_real = x.shape[0]
    t_end = ROW0 + s_real
    t = -(-t_end // LANE) * LANE
    zrow = lambda n: jnp.zeros((n, D), F32)
    xp = jnp.concatenate([zrow(FRONT), small["meta"].astype(F32), x, zrow(t - t_end)], axis=0)
    tp = jnp.concatenate([zrow(ROW0), target, zrow(t - t_end)], axis=0)
    rope = _rope_tables(t)
    logits = small["hgrn_lb_logits"].astype(F32)
    lbs = _lower_bounds_fwd(logits)
    prm0 = _prep_early(pieces0, small, 0)
    got = {}
    if fwd_ride is None:
        prm0.update(_prep_late(pieces0))
        h, sv0 = _layer_fwd(xp, prm0, lbs[0:1], rope, t_end)
    else:
        def on_rode(rode):
            late0, got["pieces1"] = fwd_ride[1](rode)
            prm0.update(_prep_late(late0))
        h, sv0 = _layer_fwd(xp, prm0, lbs[0:1], rope, t_end, fwd_ride[0], on_rode)
        pieces1 = got["pieces1"]
    prm1 = _prep_early(pieces1, small, 1)
    if fwd_ride is None:
        prm1.update(_prep_late(pieces1))
        h, sv1 = _layer_fwd(h, prm1, lbs[1:2], rope, t_end)
    else:
        h, sv1 = _layer_fwd(h, prm1, lbs[1:2], rope, t_end, fwd_ride[2],
                            lambda rode: prm1.update(_prep_late(fwd_ride[3](rode))))
    dh, lsum = _loss_head(h, tp, t_end)
    loss = jnp.sum(lsum)
    rides1, make_rides0 = (None, None) if bwd_rides is None else bwd_rides
    dh, g1, dlb1, rode1 = _layer_bwd(dh, sv1, prm1, lbs[1:2], rope, t_end, rides1)
    dh, g0, dlb0, rode = _layer_bwd(dh, sv0, prm0, lbs[0:1], rope, t_end,
                                    None if make_rides0 is None else make_rides0(g1, rode1))
    dlogits = _lower_bounds_bwd(logits, jnp.stack([dlb0, dlb1]))
    grads = [g0, g1]
    for l in range(DEPTH):
        grads[l]["hgrn_lb_logits"] = dlogits[l]
    return loss, dh[ROW0:t_end], grads, dh[FRONT:ROW0], rode


MESH = pl.DeviceIdType.MESH
_ANY = pl.BlockSpec(memory_space=pl.ANY)
SMALL_ROWS = 64
SMALL_LEN = SMALL_ROWS * 1024


def _mesh_pos():
    return lax.axis_index("x"), lax.axis_index("y"), lax.axis_index("c")


def _other_chips(x, y):
    return [(1 - x, y), (x, 1 - y), (1 - x, 1 - y)]


class _Plan:
    def __init__(self, name, ins, out_shapes, sems, start, finish, relay=None):
        self.name, self.ins, self.out_shapes, self.sems = name, list(ins), list(out_shapes), list(sems)
        self.start, self.finish, self.relay = start, finish, relay


def _run_plan(plan):
    ni, no = len(plan.ins), len(plan.out_shapes)

    def body(*refs):
        ins, outs, sems = refs[:ni], refs[ni:ni + no], refs[ni + no:]
        plan.start(ins, outs, sems)
        if plan.relay is not None:
            plan.relay(ins, outs, sems)
        plan.finish(ins, outs, sems)

    return pl.pallas_call(body, name=plan.name, in_specs=[_ANY] * ni, out_specs=[_ANY] * no,
                          out_shape=plan.out_shapes, scratch_shapes=plan.sems)(*plan.ins)


def _plan_specs(plan):
    if plan is None:
        return [], [], [], [], []
    return plan.ins, [_ANY] * len(plan.ins), [_ANY] * len(plan.out_shapes), plan.out_shapes, plan.sems


def _host_refs(refs, n_in, n_out, n_scratch, plan):
    ni = 0 if plan is None else len(plan.ins)
    no = 0 if plan is None else len(plan.out_shapes)
    o0 = n_in + ni
    s0 = o0 + n_out + no
    own = (refs[:n_in], refs[o0:o0 + n_out], refs[s0:s0 + n_scratch])
    rider = (refs[n_in:o0], refs[o0 + n_out:s0], refs[s0 + n_scratch:])
    return own, rider


def _ride(plan, rider, step, last):
    if plan is None:
        return lambda: None

    @pl.when(step == 0)
    def _():
        plan.start(*rider)

    def done():
        if plan.relay is not None:
            @pl.when(step == last - 1)
            def _():
                plan.relay(*rider)

        @pl.when(step == last)
        def _():
            plan.finish(*rider)
    return done


def _merge_plans(name, plans):
    def parts(ins, outs, sems):
        i = o = s = 0
        for p in plans:
            ni, no, ns = len(p.ins), len(p.out_shapes), len(p.sems)
            yield p, (ins[i:i + ni], outs[o:o + no], sems[s:s + ns])
            i, o, s = i + ni, o + no, s + ns

    def start(ins, outs, sems):
        for p, refs in parts(ins, outs, sems):
            p.start(*refs)

    def relay(ins, outs, sems):
        for p, refs in parts(ins, outs, sems):
            if p.relay is not None:
                p.relay(*refs)

    def finish(ins, outs, sems):
        for p, refs in parts(ins, outs, sems):
            p.finish(*refs)

    return _Plan(name, [a for p in plans for a in p.ins], [a for p in plans for a in p.out_shapes],
                 [a for p in plans for a in p.sems], start, finish, relay)


def _plan_gather(own, layer, name):
    nw = len(own)

    def copies(ins, outs, sems):
        send_sems, recv_sems = sems

        def over_ici(w, j, chip_of_data, to):
            return pltpu.make_async_remote_copy(
                src_ref=ins[w].at[layer], dst_ref=outs[w].at[chip_of_data], send_sem=send_sems.at[w, j],
                recv_sem=recv_sems.at[w, j], device_id=to, device_id_type=MESH)

        def over_d2d(w, j, chip_of_data, to):
            return pltpu.make_async_remote_copy(
                src_ref=outs[w].at[chip_of_data], dst_ref=outs[w].at[chip_of_data], send_sem=send_sems.at[w, 3 + j],
                recv_sem=recv_sems.at[w, 3 + j], device_id=to, device_id_type=MESH)
        return over_ici, over_d2d

    def start(ins, outs, sems):
        x, y, c = _mesh_pos()
        over_ici, _ = copies(ins, outs, sems)

        @pl.when(c == layer)
        def _():
            for j, (px, py) in enumerate(_other_chips(x, y)):
                for w in range(nw):
                    over_ici(w, j, 2 * x + y, (px, py, layer)).start()

    def relay(ins, outs, sems):
        x, y, c = _mesh_pos()
        over_ici, over_d2d = copies(ins, outs, sems)

        @pl.when(c == layer)
        def _():
            for j, (px, py) in enumerate(_other_chips(x, y)):
                for w in range(nw):
                    over_ici(w, j, 2 * px + py, (x, y, c)).wait_recv()
                    over_d2d(w, j, 2 * px + py, (x, y, 1 - layer)).start()

    def finish(ins, outs, sems):
        x, y, c = _mesh_pos()
        over_ici, over_d2d = copies(ins, outs, sems)
        chips = _other_chips(x, y)

        @pl.when(c == layer)
        def _():
            for j, (px, py) in enumerate(chips):
                for w in range(nw):
                    over_ici(w, j, 2 * x + y, (px, py, layer)).wait_send()
                    over_d2d(w, j, 2 * px + py, (x, y, 1 - layer)).wait_send()

        @pl.when(c != layer)
        def _():
            for j, (px, py) in enumerate(chips):
                for w in range(nw):
                    over_d2d(w, j, 2 * px + py, (x, y, c)).wait_recv()

    return _Plan(name, own,
                 [jax.ShapeDtypeStruct((4,) + a.shape[1:], a.dtype) for a in own],
                 [pltpu.SemaphoreType.DMA((nw, 6)), pltpu.SemaphoreType.DMA((nw, 6))], start, finish, relay)


def _plan_to_sibling(arrs, layer, name):
    nw = len(arrs)

    def copy(ins, outs, sems, w):
        x, y, _ = _mesh_pos()
        return pltpu.make_async_remote_copy(src_ref=ins[w], dst_ref=outs[w], send_sem=sems[0].at[w],
                                            recv_sem=sems[1].at[w], device_id=(x, y, layer), device_id_type=MESH)

    def start(ins, outs, sems):
        @pl.when(lax.axis_index("c") != layer)
        def _():
            for w in range(nw):
                copy(ins, outs, sems, w).start()

    def finish(ins, outs, sems):
        c = lax.axis_index("c")

        @pl.when(c != layer)
        def _():
            for w in range(nw):
                copy(ins, outs, sems, w).wait_send()

        @pl.when(c == layer)
        def _():
            for w in range(nw):
                copy(ins, outs, sems, w).wait_recv()

    return _Plan(name, arrs, [jax.ShapeDtypeStruct(a.shape, a.dtype) for a in arrs],
                 [pltpu.SemaphoreType.DMA((nw,)), pltpu.SemaphoreType.DMA((nw,))], start, finish)


def _plan_scatter(parts, layer, name):
    nw = len(parts)

    def start(ins, outs, sems):
        x, y, c = _mesh_pos()

        @pl.when(c == layer)
        def _():
            for j, (px, py) in enumerate(_other_chips(x, y)):
                for w in range(nw):
                    pltpu.make_async_remote_copy(
                        src_ref=ins[w].at[2 * px + py], dst_ref=outs[w].at[2 * x + y], send_sem=sems[0].at[w, j],
                        recv_sem=sems[1].at[w, j], device_id=(px, py, layer), device_id_type=MESH).start()

    def finish(ins, outs, sems):
        x, y, c = _mesh_pos()

        @pl.when(c == layer)
        def _():
            for j, (px, py) in enumerate(_other_chips(x, y)):
                for w in range(nw):
                    pltpu.make_async_remote_copy(
                        src_ref=ins[w].at[2 * px + py], dst_ref=outs[w].at[2 * px + py], send_sem=sems[0].at[w, j],
                        recv_sem=sems[1].at[w, j], device_id=(x, y, c), device_id_type=MESH).wait()

    return _Plan(name, parts, [jax.ShapeDtypeStruct(a.shape, a.dtype) for a in parts],
                 [pltpu.SemaphoreType.DMA((nw, 3)), pltpu.SemaphoreType.DMA((nw, 3))], start, finish)


def _sibling_exchange(reds0, reds1):
    nw = len(reds0)

    def body(*refs):
        a0, a1, outs = refs[:nw], refs[nw:2 * nw], refs[2 * nw:3 * nw]
        send_sems, recv_sems = refs[3 * nw:]
        x, y, c = _mesh_pos()

        def copy(w, src):
            return pltpu.make_async_remote_copy(src_ref=src, dst_ref=outs[w], send_sem=send_sems.at[w],
                                                recv_sem=recv_sems.at[w], device_id=(x, y, 1 - c),
                                                device_id_type=MESH)

        @pl.when(c == 0)
        def _():
            for w in range(nw):
                copy(w, a0[w]).start()

        @pl.when(c == 1)
        def _():
            for w in range(nw):
                copy(w, a1[w]).start()

        for w in range(nw):
            copy(w, a0[w]).wait()

    return pl.pallas_call(
        body, name="sibling_exchange", in_specs=[_ANY] * (2 * nw), out_specs=[_ANY] * nw,
        out_shape=[jax.ShapeDtypeStruct(a.shape, a.dtype) for a in reds0],
        scratch_shapes=[pltpu.SemaphoreType.DMA((nw,)), pltpu.SemaphoreType.DMA((nw,))],
    )(*reds0, *reds1)


def _all_reduce_small(v, name):
    rows, cols = v.shape

    def body(v_ref, o_ref, slots, send_sems, recv_sems):
        x, y, c = _mesh_pos()
        me = 4 * x + 2 * y + c
        slots[me] = v_ref[...]
        peers = []
        for rel in range(1, 8):
            fx, fy, fc = (rel >> 2) & 1, (rel >> 1) & 1, rel & 1
            px = 1 - x if fx else x
            py = 1 - y if fy else y
            pc = 1 - c if fc else c
            peers.append((px, py, pc))
        cps = [pltpu.make_async_remote_copy(src_ref=v_ref, dst_ref=slots.at[me], send_sem=send_sems.at[k],
                                            recv_sem=recv_sems.at[k], device_id=peer, device_id_type=MESH)
               for k, peer in enumerate(peers)]
        for cp in cps:
            cp.start()
        for k, (px, py, pc) in enumerate(peers):
            pltpu.make_async_remote_copy(src_ref=v_ref, dst_ref=slots.at[4 * px + 2 * py + pc],
                                         send_sem=send_sems.at[k], recv_sem=recv_sems.at[k], device_id=(x, y, c),
                                         device_id_type=MESH).wait_recv()
        for cp in cps:
            cp.wait_send()
        acc = slots[0]
        for d in range(1, 8):
            acc = acc + slots[d]
        o_ref[...] = acc

    vm = pl.BlockSpec(memory_space=pltpu.VMEM)
    return pl.pallas_call(
        body, name=name, in_specs=[vm], out_specs=vm,
        out_shape=jax.ShapeDtypeStruct((rows, cols), F32),
        scratch_shapes=[pltpu.VMEM((8, rows, cols), F32), pltpu.SemaphoreType.DMA((7,)),
                        pltpu.SemaphoreType.DMA((7,))],
    )(v)


def _add_to_wire(a, b, name):
    n4, r, c_ = a.shape
    rows = n4 * r
    tr = _ew_tile(rows, c_, 3)

    def body(a_ref, b_ref, o_ref):
        o_ref[...] = (a_ref[...] + b_ref[...]).astype(o_ref.dtype)

    spec = pl.BlockSpec((tr, c_), lambda i: (i, 0))
    out = pl.pallas_call(
        body, name="add_to_wire_" + name, grid=(rows // tr,), in_specs=[spec, spec], out_specs=spec,
        out_shape=jax.ShapeDtypeStruct((rows, c_), jnp.bfloat16), compiler_params=_cp("parallel"),
    )(a.reshape(rows, c_), b.reshape(rows, c_))
    return out.reshape(n4, r, c_)


def _sum_chips(recv, wire, name):
    _, r, c_ = recv.shape
    tr = _ew_tile(r, c_, 6)

    def body(r_ref, w_ref, o_ref):
        chip = 2 * lax.axis_index("x") + lax.axis_index("y")
        acc = None
        for s in range(4):
            term = jnp.where(chip == s, w_ref[s], r_ref[s]).astype(F32)
            acc = term if acc is None else acc + term
        o_ref[...] = acc

    blk = pl.BlockSpec((4, tr, c_), lambda i: (0, i, 0))
    return pl.pallas_call(
        body, name="sum_chips_" + name, grid=(r // tr,),
        in_specs=[blk, blk], out_specs=pl.BlockSpec((tr, c_), lambda i: (i, 0)),
        out_shape=jax.ShapeDtypeStruct((r, c_), F32),
        compiler_params=_cp("parallel"),
    )(recv, wire)


def _pack_small(vals, meta_full, conv_w_full):
    flat = jnp.concatenate([vals[k].reshape(-1) for k in SMALL] + [meta_full.reshape(-1), conv_w_full.reshape(-1)])
    return jnp.pad(flat, (0, SMALL_LEN - flat.shape[0])).reshape(SMALL_ROWS, 1024)


def _unpack_small(buf):
    flat = buf.reshape(-1)
    out, off = {}, 0
    for k in SMALL:
        n = DEPTH * SMALL_SIZES[k]
        out[k] = flat[off:off + n].reshape(DEPTH, SMALL_SIZES[k])
        off += n
    meta = flat[off:off + N_META * D].reshape(N_META, D)
    off += N_META * D
    conv_w = flat[off:off + DEPTH * CONV_K * CONV_DIM].reshape(DEPTH, CONV_K, CONV_DIM)
    return out, meta, conv_w


def kernel(x, meta, norm1_g, w_in, conv_w, conv_b, conv_ln_g, conv_ln_b, w_conv_out, q_a_norm_g, w_uq, kv_a_norm_g, w_ukv, q_norm_g, k_norm_g, w_attn_out, hgrn_lb_logits, hgrn_norm_g, w_hgrn_out, w_out, norm2_g, w_ff1, w_ff2, loss_target, m_meta, m_norm1_g, m_w_in, m_conv_w, m_conv_b, m_conv_ln_g, m_conv_ln_b, m_w_conv_out, m_q_a_norm_g, m_w_uq, m_kv_a_norm_g, m_w_ukv, m_q_norm_g, m_k_norm_g, m_w_attn_out, m_hgrn_lb_logits, m_hgrn_norm_g, m_w_hgrn_out, m_w_out, m_norm2_g, m_w_ff1, m_w_ff2, v_meta, v_norm1_g, v_w_in, v_conv_w, v_conv_b, v_conv_ln_g, v_conv_ln_b, v_w_conv_out, v_q_a_norm_g, v_w_uq, v_kv_a_norm_g, v_w_ukv, v_q_norm_g, v_k_norm_g, v_w_attn_out, v_hgrn_lb_logits, v_hgrn_norm_g, v_w_hgrn_out, v_w_out, v_norm2_g, v_w_ff1, v_w_ff2):
    names = ("meta", "norm1_g", "w_in", "conv_w", "conv_b", "conv_ln_g", "conv_ln_b", "w_conv_out", "q_a_norm_g",
             "w_uq", "kv_a_norm_g", "w_ukv", "q_norm_g", "k_norm_g", "w_attn_out", "hgrn_lb_logits", "hgrn_norm_g",
             "w_hgrn_out", "w_out", "norm2_g", "w_ff1", "w_ff2")
    w = dict(zip(names, (meta, norm1_g, w_in, conv_w, conv_b, conv_ln_g, conv_ln_b, w_conv_out, q_a_norm_g, w_uq,
                         kv_a_norm_g, w_ukv, q_norm_g, k_norm_g, w_attn_out, hgrn_lb_logits, hgrn_norm_g, w_hgrn_out,
                         w_out, norm2_g, w_ff1, w_ff2)))
    m = dict(zip(names, (m_meta, m_norm1_g, m_w_in, m_conv_w, m_conv_b, m_conv_ln_g, m_conv_ln_b, m_w_conv_out,
                         m_q_a_norm_g, m_w_uq, m_kv_a_norm_g, m_w_ukv, m_q_norm_g, m_k_norm_g, m_w_attn_out,
                         m_hgrn_lb_logits, m_hgrn_norm_g, m_w_hgrn_out, m_w_out, m_norm2_g, m_w_ff1, m_w_ff2)))
    v = dict(zip(names, (v_meta, v_norm1_g, v_w_in, v_conv_w, v_conv_b, v_conv_ln_g, v_conv_ln_b, v_w_conv_out,
                         v_q_a_norm_g, v_w_uq, v_kv_a_norm_g, v_w_ukv, v_q_norm_g, v_k_norm_g, v_w_attn_out,
                         v_hgrn_lb_logits, v_hgrn_norm_g, v_w_hgrn_out, v_w_out, v_norm2_g, v_w_ff1, v_w_ff2)))
    cx, cy, cc = _mesh_pos()
    chip = 2 * cx + cy
    zero = jnp.zeros((), jnp.int32)

    own = {k: w[k].astype(_MM) for k in BIG}

    def as_pieces(names, gathered, layer):
        return {k: [jnp.where(chip == s, own[k][layer], g[s]) for s in range(4)] for k, g in zip(names, gathered)}

    pieces0 = as_pieces(EARLY, _run_plan(_plan_gather([own[k] for k in EARLY], 0, "gather_l0_early")), 0)
    fwd_ride = (_merge_plans("gather_mid", [_plan_gather([own[k] for k in LATE], 0, "gather_l0_late"),
                                            _plan_gather([own[k] for k in EARLY], 1, "gather_l1_early")]),
                lambda got: (as_pieces(LATE, got[:len(LATE)], 0), as_pieces(EARLY, got[len(LATE):], 1)),
                _plan_gather([own[k] for k in LATE], 1, "gather_l1_late"),
                lambda got: as_pieces(LATE, got, 1))
    meta_slab = lax.dynamic_update_slice(jnp.zeros((N_META, D), F32), meta, (zero, chip * (D // 4)))
    convw_slab = lax.dynamic_update_slice(jnp.zeros((DEPTH, CONV_K, CONV_DIM), F32), conv_w,
                                          (zero, zero, chip * (CONV_DIM // 4)))
    zsmall = {k: jnp.zeros((DEPTH, SMALL_SIZES[k]), F32) for k in SMALL}
    south = (cc == 0).astype(F32)
    _, meta_full, convw_full = _unpack_small(
        _all_reduce_small(_pack_small(zsmall, meta_slab, convw_slab) * south, "gather_small"))
    small = {k: w[k] for k in SMALL}
    small["meta"] = meta_full
    small["conv_w"] = convw_full

    FFN = ("w_ff1", "w_ff2")
    MID = ("w_out", "w_conv_out", "w_attn_out", "w_hgrn_out")
    REST = tuple(k for k in BIG if k not in FFN + MID)
    held = {}

    def to_wire(names, layer, mine, from_sibling):
        return lax.cond(
            cc == layer,
            lambda: [_add_to_wire(a, b, "%s_l%d" % (k, layer)) for k, a, b in zip(names, mine, from_sibling)],
            lambda: [jnp.zeros(a.shape, jnp.bfloat16) for a in mine])

    def chip_sum(names, layer, got, wire):
        return lax.cond(
            cc == layer,
            lambda: [_sum_chips(r, s, "%s_l%d" % (k, layer)) for k, r, s in zip(names, got, wire)],
            lambda: [jnp.zeros(s.shape[1:], F32) for s in wire])

    NONFFN = tuple(k for k in BIG if k not in FFN)

    def ride_attn_l1(_, g1):
        held["g1_ffn"] = [g1[k] for k in FFN]
        return _plan_to_sibling(held["g1_ffn"], 1, "swap_grads_l1_ffn")

    def rides_l0(g1, rode_l1):
        g1_rest = [g1[k] for k in NONFFN]

        def ride_mix(from_sibling1, g0_ffn):
            wire1 = dict(zip(FFN, to_wire(FFN, 1, held["g1_ffn"], rode_l1[0])))
            wire1.update(zip(NONFFN, to_wire(NONFFN, 1, g1_rest, from_sibling1)))
            held["wire1"] = [wire1[k] for k in BIG]
            held["g0_ffn"] = [g0_ffn[k] for k in FFN]
            return _plan_to_sibling(held["g0_ffn"], 0, "swap_grads_l0_ffn")

        def ride_attn(from_sibling0, g0):
            held["wire0_ffn"] = to_wire(FFN, 0, held["g0_ffn"], from_sibling0)
            held["g0_mid"] = [g0[k] for k in MID]
            return _merge_plans("exchange_grads_mid", [
                _plan_scatter(held["wire1"], 1, "scatter_grads_l1"),
                _plan_scatter(held["wire0_ffn"], 0, "scatter_grads_l0_ffn"),
                _plan_to_sibling(held["g0_mid"], 0, "swap_grads_l0_mid")])

        def ride_hgrn(rode_attn):
            held["wire0_mid"] = to_wire(MID, 0, held["g0_mid"], rode_attn[len(BIG) + len(FFN):])
            return _plan_scatter(held["wire0_mid"], 0, "scatter_grads_l0_mid")

        return {"ffn": _plan_to_sibling(g1_rest, 1, "swap_grads_l1_rest"), "mix": ride_mix, "attn": ride_attn,
                "hgrn": ride_hgrn}

    loss_share, grad_x, gl, g_meta, (got, got_mid) = _device_step(
        x[0], loss_target[0], small, pieces0, None, fwd_ride, ({"attn": ride_attn_l1}, rides_l0))

    reds1 = chip_sum(BIG, 1, got[:len(BIG)], held["wire1"])
    reds0 = dict(zip(FFN, chip_sum(FFN, 0, got[len(BIG):len(BIG) + len(FFN)], held["wire0_ffn"])))
    reds0.update(zip(MID, chip_sum(MID, 0, got_mid, held["wire0_mid"])))
    g0_rest = [gl[0][k] for k in REST]
    wire0 = to_wire(REST, 0, g0_rest, _run_plan(_plan_to_sibling(g0_rest, 0, "swap_grads_l0_rest")))
    reds0.update(zip(REST, chip_sum(REST, 0, _run_plan(_plan_scatter(wire0, 0, "scatter_grads_l0_rest")), wire0)))
    reds0 = [reds0[k] for k in BIG]
    reds_sibling = _sibling_exchange(reds0, reds1)
    grads, delta, new_m, new_v = {}, {}, {}, {}
    t_view = lambda k, a: jnp.swapaxes(a, -1, -2) if k == "w_in" else a
    for k, r0, r1, theirs in zip(BIG, reds0, reds1, reds_sibling):
        grads[k], delta[k], new_m[k], new_v[k] = [
            t_view(k, a) for a in _adamw_layers(t_view(k, w[k]), t_view(k, m[k]), t_view(k, v[k]), t_view(k, r0),
                                                t_view(k, r1), t_view(k, theirs), k)]

    g_small_local = {k: jnp.stack([gl[l][k] for l in range(DEPTH)]) for k in SMALL}
    g_convw_local = jnp.stack([gl[l]["conv_w"] for l in range(DEPTH)])
    reduced = _all_reduce_small(
        _pack_small(g_small_local, g_meta, g_convw_local).at[SMALL_ROWS - 1, 1023].set(loss_share), "reduce_small")
    loss = reduced[SMALL_ROWS - 1, 1023]
    g_small, g_meta_full, g_convw_full = _unpack_small(reduced)
    grads.update(g_small)
    grads["meta"] = lax.dynamic_slice(g_meta_full, (zero, chip * (D // 4)), (N_META, D // 4))
    grads["conv_w"] = lax.dynamic_slice(g_convw_full, (zero, zero, chip * (CONV_DIM // 4)),
                                        (DEPTH, CONV_K, CONV_DIM // 4))

    def small_pack(src):
        return _pack_small(src, jnp.pad(src["meta"], ((0, 0), (0, D - D // 4))),
                           jnp.pad(src["conv_w"], ((0, 0), (0, 0), (0, CONV_DIM - CONV_DIM // 4))))

    def small_unpack(buf):
        out, meta_p, convw_p = _unpack_small(buf)
        out["meta"] = meta_p[:, :D // 4]
        out["conv_w"] = convw_p[:, :, :CONV_DIM // 4]
        return out

    d_s, m_s, v_s = [small_unpack(a) for a in _adamw(small_pack(w), small_pack(grads), small_pack(m),
                                                     small_pack(v), "small")]
    delta.update(d_s)
    new_m.update(m_s)
    new_v.update(v_s)
    return (loss, grad_x[None], *[grads[k] for k in names], *[delta[k] for k in names],
            *[new_m[k] for k in names], *[new_v[k] for k in names])
```
